```python
import math
import jax, jax.numpy as jnp
from jax import lax
import numpy as np

D_MODEL = 1024
BATCH = 8
SEQ = 4096
DEPTH = 2

MIX_WIDTH = D_MODEL
A_WIDTH = MIX_WIDTH // 2
A_GROUPS = 4
A_GROUP_DIM = A_WIDTH // A_GROUPS
CHUNK = 128
B_WIDTH = MIX_WIDTH - A_WIDTH
HEAD_DIM = 64
B_HEADS = B_WIDTH // HEAD_DIM
ROT_DIM = HEAD_DIM // 4
ROPE_THETA = 500000.0
DILATED_PATTERNS = ((128, 1), (512, 4), (2048, 16))
IN_COLS = 2 * A_WIDTH + 3 * B_WIDTH
D_FF = 4 * D_MODEL
CONV_WIDTH = 3
EPS = 1e-6
NEG_INF = -1e30

kernel_name = 'hybrid_gmlp_dilated_attn_convffn'


def rmsnorm(x, g):
    xf = x.astype(jnp.float32)
    y = xf * lax.rsqrt(jnp.mean(xf * xf, axis=-1, keepdims=True) + EPS)
    return (y * g.astype(jnp.float32)).astype(x.dtype)


def layernorm(x, g, b):
    xf = x.astype(jnp.float32)
    mu = jnp.mean(xf, axis=-1, keepdims=True)
    xc = xf - mu
    y = xc * lax.rsqrt(jnp.mean(xc * xc, axis=-1, keepdims=True) + EPS)
    return (y * g.astype(jnp.float32) + b.astype(jnp.float32)).astype(x.dtype)


def partial_rope(x):
    s = x.shape[1]
    half = ROT_DIM // 2
    inv = ROPE_THETA ** (-jnp.arange(0, ROT_DIM, 2, dtype=jnp.float32) / ROT_DIM)
    ang = jnp.arange(s, dtype=jnp.float32)[:, None] * inv[None, :]
    cos = jnp.cos(ang)[None, :, None, :]
    sin = jnp.sin(ang)[None, :, None, :]
    xf = x.astype(jnp.float32)
    x1, x2 = xf[..., :half], xf[..., half:ROT_DIM]
    out = jnp.concatenate([x1 * cos - x2 * sin, x2 * cos + x1 * sin, xf[..., ROT_DIM:]], axis=-1)
    return out.astype(x.dtype)


def dilated_branch(q, k, v, window, dilation):
    bsz, s, h, dh = q.shape
    band = window // dilation
    n = s // dilation
    nb = -(-n // band)
    pad = nb * band - n

    def to_blocks(t):
        t = t.reshape(bsz, n, dilation, h, dh).transpose(0, 2, 1, 3, 4)
        t = jnp.pad(t, ((0, 0), (0, 0), (0, pad), (0, 0), (0, 0)))
        return t.reshape(bsz, dilation, nb, band, h, dh)

    def with_prev(t):
        prev = jnp.pad(t[:, :, :-1], ((0, 0), (0, 0), (1, 0), (0, 0), (0, 0), (0, 0)))
        return jnp.concatenate([prev, t], axis=3)

    qb = to_blocks(q)
    kc = with_prev(to_blocks(k))
    vc = with_prev(to_blocks(v))
    scores = jnp.einsum('brnqhd,brnkhd->brnhqk', qb, kc).astype(jnp.float32) * (dh ** -0.5)
    qi = jnp.arange(band)[:, None]
    kj = jnp.arange(2 * band)[None, :]
    dist = qi + band - kj
    blk = jnp.arange(nb)[:, None, None]
    valid = (dist >= 0) & (dist <= band) & (blk * band + kj - band >= 0)
    scores = jnp.where(valid[None, None, :, None], scores, NEG_INF)
    m = jnp.max(scores, axis=-1, keepdims=True)
    p = jnp.exp(scores - m)
    l = jnp.sum(p, axis=-1, keepdims=True)
    o = jnp.einsum('brnhqk,brnkhd->brnqhd', p.astype(v.dtype), vc).astype(jnp.float32)
    l_t = l[..., 0].transpose(0, 1, 2, 4, 3)
    lse = (m[..., 0] + jnp.log(l[..., 0])).transpose(0, 1, 2, 4, 3)
    o = o / l_t[..., None]
    o = o.reshape(bsz, dilation, nb * band, h, dh)[:, :, :n]
    o = o.transpose(0, 2, 1, 3, 4).reshape(bsz, s, h, dh)
    lse = lse.reshape(bsz, dilation, nb * band, h)[:, :, :n]
    lse = lse.transpose(0, 2, 1, 3).reshape(bsz, s, h)
    return o, lse


def mixer_spatial_gating(za, v_norm_g, v_norm_b, w_spatial, b_spatial):
    bsz, s, _ = za.shape
    za = jax.nn.gelu(za, approximate=False)
    u, va = za[..., :A_WIDTH], za[..., A_WIDTH:]
    va = layernorm(va, v_norm_g, v_norm_b)
    vch = va.reshape(bsz, s // CHUNK, CHUNK, A_GROUPS, A_GROUP_DIM)
    ws = w_spatial * jnp.tril(jnp.ones((CHUNK, CHUNK), dtype=w_spatial.dtype))
    sg = jnp.einsum('gpq,bnqgc->bnpgc', ws, vch) + b_spatial.T[None, None, :, :, None]
    return u * sg.reshape(bsz, s, A_WIDTH)


def mixer_dilated_attention(zb):
    bsz, s, _ = zb.shape
    qkv = zb.reshape(bsz, s, 3, B_HEADS, HEAD_DIM)
    q = partial_rope(qkv[:, :, 0])
    k = partial_rope(qkv[:, :, 1])
    v = qkv[:, :, 2]
    outs, lses = zip(*[dilated_branch(q, k, v, w, d) for (w, d) in DILATED_PATTERNS])
    alpha = jax.nn.softmax(jnp.stack(lses, axis=0), axis=0)
    o = jnp.sum(alpha[..., None] * jnp.stack(outs, axis=0), axis=0)
    return o.reshape(bsz, s, B_WIDTH).astype(zb.dtype)


def conv_ffn(h, w_up, conv_w, conv_b, w_down):
    s = h.shape[1]
    up = jnp.einsum('bsd,df->bsf', h, w_up)
    up_pad = jnp.pad(up, ((0, 0), (CONV_WIDTH - 1, 0), (0, 0)))
    conv = conv_b + sum(conv_w[i] * up_pad[:, i:i + s] for i in range(CONV_WIDTH))
    gate, val = conv[..., :D_FF], conv[..., D_FF:]
    y = jax.nn.gelu(gate, approximate=True) * val
    return jnp.einsum('bsf,fd->bsd', y, w_down)


def _fwd_setup_inputs(seed: int = 0) -> dict:
    key = jax.random.key(seed)
    ks = jax.random.split(key, 18)
    f32 = jnp.float32

    def nrm(k, shape, scale):
        return jax.random.normal(k, shape, f32) * scale

    def gain(k, shape):
        return 1.0 + 0.05 * jax.random.normal(k, shape, f32)

    L = DEPTH
    return {
        'x': jax.random.normal(ks[0], (BATCH, SEQ, D_MODEL), f32),
        'pre_mix_norm': gain(ks[1], (L, D_MODEL)),
        'w_in': nrm(ks[2], (L, D_MODEL, IN_COLS), D_MODEL ** -0.5),
        'v_norm_g': gain(ks[3], (L, A_WIDTH)),
        'v_norm_b': nrm(ks[4], (L, A_WIDTH), 0.02),
        'w_spatial': nrm(ks[5], (L, A_GROUPS, CHUNK, CHUNK), CHUNK ** -0.5),
        'b_spatial': gain(ks[6], (L, A_GROUPS, CHUNK)),
        'out_norm_a': gain(ks[7], (L, A_WIDTH)),
        'out_norm_b': gain(ks[8], (L, B_WIDTH)),
        'w_out': nrm(ks[9], (L, MIX_WIDTH, D_MODEL), MIX_WIDTH ** -0.5),
        'post_mix_norm': gain(ks[10], (L, D_MODEL)),
        'pre_ffn_norm': gain(ks[11], (L, D_MODEL)),
        'w_up': nrm(ks[12], (L, D_MODEL, 2 * D_FF), D_MODEL ** -0.5),
        'conv_w': nrm(ks[13], (L, CONV_WIDTH, 2 * D_FF), CONV_WIDTH ** -0.5),
        'conv_b': nrm(ks[14], (L, 2 * D_FF), 0.02),
        'w_down': nrm(ks[15], (L, D_FF, D_MODEL), D_FF ** -0.5),
        'post_ffn_norm': gain(ks[16], (L, D_MODEL)),
    }


def _fwd_reference(x, pre_mix_norm, w_in, v_norm_g, v_norm_b, w_spatial, b_spatial,
              out_norm_a, out_norm_b, w_out, post_mix_norm, pre_ffn_norm,
              w_up, conv_w, conv_b, w_down, post_ffn_norm):
    for l in range(DEPTH):
        h = rmsnorm(x, pre_mix_norm[l])
        proj = jnp.einsum('bsd,de->bse', h, w_in[l])
        o_a = mixer_spatial_gating(proj[..., :2 * A_WIDTH], v_norm_g[l], v_norm_b[l],
                                   w_spatial[l], b_spatial[l])
        o_b = mixer_dilated_attention(proj[..., 2 * A_WIDTH:])
        mixed = jnp.concatenate([rmsnorm(o_a, out_norm_a[l]), rmsnorm(o_b, out_norm_b[l])], axis=-1)
        y = jnp.einsum('bse,ed->bsd', mixed, w_out[l])
        x = x + rmsnorm(y, post_mix_norm[l])
        h = rmsnorm(x, pre_ffn_norm[l])
        f = conv_ffn(h, w_up[l], conv_w[l], conv_b[l], w_down[l])
        x = x + rmsnorm(f, post_ffn_norm[l])
    return x


import jax as _jax
import jax.numpy as _jnp

TWIN_FORMAT = 'train_step'
FWD_PARAMS = ['x', 'pre_mix_norm', 'w_in', 'v_norm_g', 'v_norm_b', 'w_spatial', 'b_spatial', 'out_norm_a', 'out_norm_b', 'w_out', 'post_mix_norm', 'pre_ffn_norm', 'w_up', 'conv_w', 'conv_b', 'w_down', 'post_ffn_norm']
TWIN_WEIGHTS = ['pre_mix_norm', 'w_in', 'v_norm_g', 'v_norm_b', 'w_spatial', 'b_spatial', 'out_norm_a', 'out_norm_b', 'w_out', 'post_mix_norm', 'pre_ffn_norm', 'w_up', 'conv_w', 'conv_b', 'w_down', 'post_ffn_norm']
TWIN_DIFF_INPUT = 'x'
TWIN_INPUTS = ['x', 'pre_mix_norm', 'w_in', 'v_norm_g', 'v_norm_b', 'w_spatial', 'b_spatial', 'out_norm_a', 'out_norm_b', 'w_out', 'post_mix_norm', 'pre_ffn_norm', 'w_up', 'conv_w', 'conv_b', 'w_down', 'post_ffn_norm', 'loss_target', 'm_pre_mix_norm', 'm_w_in', 'm_v_norm_g', 'm_v_norm_b', 'm_w_spatial', 'm_b_spatial', 'm_out_norm_a', 'm_out_norm_b', 'm_w_out', 'm_post_mix_norm', 'm_pre_ffn_norm', 'm_w_up', 'm_conv_w', 'm_conv_b', 'm_w_down', 'm_post_ffn_norm', 'v_pre_mix_norm', 'v_w_in', 'v_v_norm_g', 'v_v_norm_b', 'v_w_spatial', 'v_b_spatial', 'v_out_norm_a', 'v_out_norm_b', 'v_w_out', 'v_post_mix_norm', 'v_pre_ffn_norm', 'v_w_up', 'v_conv_w', 'v_conv_b', 'v_w_down', 'v_post_ffn_norm']
TWIN_OUTPUTS = ['loss', 'grad_x', 'grad_pre_mix_norm', 'grad_w_in', 'grad_v_norm_g', 'grad_v_norm_b', 'grad_w_spatial', 'grad_b_spatial', 'grad_out_norm_a', 'grad_out_norm_b', 'grad_w_out', 'grad_post_mix_norm', 'grad_pre_ffn_norm', 'grad_w_up', 'grad_conv_w', 'grad_conv_b', 'grad_w_down', 'grad_post_ffn_norm', 'delta_pre_mix_norm', 'delta_w_in', 'delta_v_norm_g', 'delta_v_norm_b', 'delta_w_spatial', 'delta_b_spatial', 'delta_out_norm_a', 'delta_out_norm_b', 'delta_w_out', 'delta_post_mix_norm', 'delta_pre_ffn_norm', 'delta_w_up', 'delta_conv_w', 'delta_conv_b', 'delta_w_down', 'delta_post_ffn_norm', 'new_m_pre_mix_norm', 'new_m_w_in', 'new_m_v_norm_g', 'new_m_v_norm_b', 'new_m_w_spatial', 'new_m_b_spatial', 'new_m_out_norm_a', 'new_m_out_norm_b', 'new_m_w_out', 'new_m_post_mix_norm', 'new_m_pre_ffn_norm', 'new_m_w_up', 'new_m_conv_w', 'new_m_conv_b', 'new_m_w_down', 'new_m_post_ffn_norm', 'new_v_pre_mix_norm', 'new_v_w_in', 'new_v_v_norm_g', 'new_v_v_norm_b', 'new_v_w_spatial', 'new_v_b_spatial', 'new_v_out_norm_a', 'new_v_out_norm_b', 'new_v_w_out', 'new_v_post_mix_norm', 'new_v_pre_ffn_norm', 'new_v_w_up', 'new_v_conv_w', 'new_v_conv_b', 'new_v_w_down', 'new_v_post_ffn_norm']
TWIN_LEAF_KINDS = {'loss': 'loss', 'grad_x': 'grad_x', 'grad_pre_mix_norm': 'grad_w', 'grad_w_in': 'grad_w', 'grad_v_norm_g': 'grad_w', 'grad_v_norm_b': 'grad_w', 'grad_w_spatial': 'grad_w', 'grad_b_spatial': 'grad_w', 'grad_out_norm_a': 'grad_w', 'grad_out_norm_b': 'grad_w', 'grad_w_out': 'grad_w', 'grad_post_mix_norm': 'grad_w', 'grad_pre_ffn_norm': 'grad_w', 'grad_w_up': 'grad_w', 'grad_conv_w': 'grad_w', 'grad_conv_b': 'grad_w', 'grad_w_down': 'grad_w', 'grad_post_ffn_norm': 'grad_w', 'delta_pre_mix_norm': 'delta_w', 'delta_w_in': 'delta_w', 'delta_v_norm_g': 'delta_w', 'delta_v_norm_b': 'delta_w', 'delta_w_spatial': 'delta_w', 'delta_b_spatial': 'delta_w', 'delta_out_norm_a': 'delta_w', 'delta_out_norm_b': 'delta_w', 'delta_w_out': 'delta_w', 'delta_post_mix_norm': 'delta_w', 'delta_pre_ffn_norm': 'delta_w', 'delta_w_up': 'delta_w', 'delta_conv_w': 'delta_w', 'delta_conv_b': 'delta_w', 'delta_w_down': 'delta_w', 'delta_post_ffn_norm': 'delta_w', 'new_m_pre_mix_norm': 'new_m', 'new_m_w_in': 'new_m', 'new_m_v_norm_g': 'new_m', 'new_m_v_norm_b': 'new_m', 'new_m_w_spatial': 'new_m', 'new_m_b_spatial': 'new_m', 'new_m_out_norm_a': 'new_m', 'new_m_out_norm_b': 'new_m', 'new_m_w_out': 'new_m', 'new_m_post_mix_norm': 'new_m', 'new_m_pre_ffn_norm': 'new_m', 'new_m_w_up': 'new_m', 'new_m_conv_w': 'new_m', 'new_m_conv_b': 'new_m', 'new_m_w_down': 'new_m', 'new_m_post_ffn_norm': 'new_m', 'new_v_pre_mix_norm': 'new_v', 'new_v_w_in': 'new_v', 'new_v_v_norm_g': 'new_v', 'new_v_v_norm_b': 'new_v', 'new_v_w_spatial': 'new_v', 'new_v_b_spatial': 'new_v', 'new_v_out_norm_a': 'new_v', 'new_v_out_norm_b': 'new_v', 'new_v_w_out': 'new_v', 'new_v_post_mix_norm': 'new_v', 'new_v_pre_ffn_norm': 'new_v', 'new_v_w_up': 'new_v', 'new_v_conv_w': 'new_v', 'new_v_conv_b': 'new_v', 'new_v_w_down': 'new_v', 'new_v_post_ffn_norm': 'new_v'}


def _forward(args):
    return _fwd_reference(*[args[k] for k in FWD_PARAMS])


def _output_shape():
    out = _jax.eval_shape(lambda: _forward(_fwd_setup_inputs(0)))
    return out.shape, out.dtype

N_MICROBATCH = 1
ADAM_LR = 0.001
ADAM_B1 = 0.9
ADAM_B2 = 0.999
ADAM_EPS = 1e-08
ADAM_WD = 0.01
ADAM_STEP = 10
PER_EXAMPLE_BATCH_AXIS = {'x': 0, 'loss_target': 0}
SHARED_INPUTS = []
_WEIGHT_DTYPES = {'pre_mix_norm': _jnp.float32, 'w_in': _jnp.float32, 'v_norm_g': _jnp.float32, 'v_norm_b': _jnp.float32, 'w_spatial': _jnp.float32, 'b_spatial': _jnp.float32, 'out_norm_a': _jnp.float32, 'out_norm_b': _jnp.float32, 'w_out': _jnp.float32, 'post_mix_norm': _jnp.float32, 'pre_ffn_norm': _jnp.float32, 'w_up': _jnp.float32, 'conv_w': _jnp.float32, 'conv_b': _jnp.float32, 'w_down': _jnp.float32, 'post_ffn_norm': _jnp.float32}
MOMENT_SCALE = {'pre_mix_norm': 2.954746e+00, 'w_in': 1.924652e+00, 'v_norm_g': 3.686430e-01, 'v_norm_b': 5.782955e-01, 'w_spatial': 3.819400e-01, 'b_spatial': 5.393302e-01, 'out_norm_a': 7.170602e+00, 'out_norm_b': 3.601295e+00, 'w_out': 5.689194e+00, 'post_mix_norm': 3.409313e+01, 'pre_ffn_norm': 2.144910e+00, 'w_up': 8.029241e-01, 'conv_w': 9.124442e-01, 'conv_b': 4.104104e+00, 'w_down': 1.982027e+00, 'post_ffn_norm': 3.215235e+01}


def _to_microbatches(a, axis):
    t = _jnp.moveaxis(a, axis, 0)
    t = t.reshape((N_MICROBATCH, t.shape[0] // N_MICROBATCH) + t.shape[1:])
    return _jnp.moveaxis(t, 1, axis + 1)


def setup_inputs(seed: int = 0) -> dict:
    inp = _fwd_setup_inputs(seed)
    key = _jax.random.fold_in(_jax.random.key(seed), 7919)
    shape, _ = _output_shape()
    out = dict(inp)
    out["loss_target"] = _jax.random.normal(_jax.random.fold_in(key, 0), shape, _jnp.float32)
    for i, name in enumerate(TWIN_WEIGHTS):
        w = inp[name].astype(_jnp.float32)
        if MOMENT_SCALE is None:
            s = _jnp.sqrt(_jnp.mean(_jnp.square(w)) + 1e-30)
        else:
            s = MOMENT_SCALE[name]
        km, kv = _jax.random.split(_jax.random.fold_in(key, i + 1))
        out[name] = w
        out["m_" + name] = s * _jax.random.normal(km, w.shape, _jnp.float32)
        out["v_" + name] = (s * s) * _jax.random.uniform(kv, w.shape, _jnp.float32, 0.5, 1.5)
    if N_MICROBATCH > 1:
        for name, axis in PER_EXAMPLE_BATCH_AXIS.items():
            out[name] = _to_microbatches(out[name], axis)
    return {'x': out['x'], 'pre_mix_norm': out['pre_mix_norm'], 'w_in': out['w_in'], 'v_norm_g': out['v_norm_g'], 'v_norm_b': out['v_norm_b'], 'w_spatial': out['w_spatial'], 'b_spatial': out['b_spatial'], 'out_norm_a': out['out_norm_a'], 'out_norm_b': out['out_norm_b'], 'w_out': out['w_out'], 'post_mix_norm': out['post_mix_norm'], 'pre_ffn_norm': out['pre_ffn_norm'], 'w_up': out['w_up'], 'conv_w': out['conv_w'], 'conv_b': out['conv_b'], 'w_down': out['w_down'], 'post_ffn_norm': out['post_ffn_norm'], 'loss_target': out['loss_target'], 'm_pre_mix_norm': out['m_pre_mix_norm'], 'm_w_in': out['m_w_in'], 'm_v_norm_g': out['m_v_norm_g'], 'm_v_norm_b': out['m_v_norm_b'], 'm_w_spatial': out['m_w_spatial'], 'm_b_spatial': out['m_b_spatial'], 'm_out_norm_a': out['m_out_norm_a'], 'm_out_norm_b': out['m_out_norm_b'], 'm_w_out': out['m_w_out'], 'm_post_mix_norm': out['m_post_mix_norm'], 'm_pre_ffn_norm': out['m_pre_ffn_norm'], 'm_w_up': out['m_w_up'], 'm_conv_w': out['m_conv_w'], 'm_conv_b': out['m_conv_b'], 'm_w_down': out['m_w_down'], 'm_post_ffn_norm': out['m_post_ffn_norm'], 'v_pre_mix_norm': out['v_pre_mix_norm'], 'v_w_in': out['v_w_in'], 'v_v_norm_g': out['v_v_norm_g'], 'v_v_norm_b': out['v_v_norm_b'], 'v_w_spatial': out['v_w_spatial'], 'v_b_spatial': out['v_b_spatial'], 'v_out_norm_a': out['v_out_norm_a'], 'v_out_norm_b': out['v_out_norm_b'], 'v_w_out': out['v_w_out'], 'v_post_mix_norm': out['v_post_mix_norm'], 'v_pre_ffn_norm': out['v_pre_ffn_norm'], 'v_w_up': out['v_w_up'], 'v_conv_w': out['v_conv_w'], 'v_conv_b': out['v_conv_b'], 'v_w_down': out['v_w_down'], 'v_post_ffn_norm': out['v_post_ffn_norm']}


def _loss(weights, diff, rest, loss_target):
    with _jax.named_scope("forward"):
        args = {**rest, TWIN_DIFF_INPUT: diff, **{k: w.astype(_WEIGHT_DTYPES[k]) for k, w in weights.items()}}
        y = _forward(args)
    with _jax.named_scope("loss_head"):
        err = _jnp.square(y.astype(_jnp.float32) - loss_target)
        return 0.5 * _jnp.sum(_jnp.mean(err, axis=-1)) if err.ndim else 0.5 * err


def _adamw(w, g, m, v):
    m = ADAM_B1 * m + (1.0 - ADAM_B1) * g
    v = ADAM_B2 * v + (1.0 - ADAM_B2) * _jnp.square(g)
    m_hat = m / (1.0 - ADAM_B1 ** ADAM_STEP)
    v_hat = v / (1.0 - ADAM_B2 ** ADAM_STEP)
    delta = -ADAM_LR * (m_hat / (_jnp.sqrt(v_hat) + ADAM_EPS) + ADAM_WD * w)
    return delta, m, v


def reference(x, pre_mix_norm, w_in, v_norm_g, v_norm_b, w_spatial, b_spatial, out_norm_a, out_norm_b, w_out, post_mix_norm, pre_ffn_norm, w_up, conv_w, conv_b, w_down, post_ffn_norm, loss_target, m_pre_mix_norm, m_w_in, m_v_norm_g, m_v_norm_b, m_w_spatial, m_b_spatial, m_out_norm_a, m_out_norm_b, m_w_out, m_post_mix_norm, m_pre_ffn_norm, m_w_up, m_conv_w, m_conv_b, m_w_down, m_post_ffn_norm, v_pre_mix_norm, v_w_in, v_v_norm_g, v_v_norm_b, v_w_spatial, v_b_spatial, v_out_norm_a, v_out_norm_b, v_w_out, v_post_mix_norm, v_pre_ffn_norm, v_w_up, v_conv_w, v_conv_b, v_w_down, v_post_ffn_norm):
    given = dict(x=x, pre_mix_norm=pre_mix_norm, w_in=w_in, v_norm_g=v_norm_g, v_norm_b=v_norm_b, w_spatial=w_spatial, b_spatial=b_spatial, out_norm_a=out_norm_a, out_norm_b=out_norm_b, w_out=w_out, post_mix_norm=post_mix_norm, pre_ffn_norm=pre_ffn_norm, w_up=w_up, conv_w=conv_w, conv_b=conv_b, w_down=w_down, post_ffn_norm=post_ffn_norm, loss_target=loss_target, m_pre_mix_norm=m_pre_mix_norm, m_w_in=m_w_in, m_v_norm_g=m_v_norm_g, m_v_norm_b=m_v_norm_b, m_w_spatial=m_w_spatial, m_b_spatial=m_b_spatial, m_out_norm_a=m_out_norm_a, m_out_norm_b=m_out_norm_b, m_w_out=m_w_out, m_post_mix_norm=m_post_mix_norm, m_pre_ffn_norm=m_pre_ffn_norm, m_w_up=m_w_up, m_conv_w=m_conv_w, m_conv_b=m_conv_b, m_w_down=m_w_down, m_post_ffn_norm=m_post_ffn_norm, v_pre_mix_norm=v_pre_mix_norm, v_w_in=v_w_in, v_v_norm_g=v_v_norm_g, v_v_norm_b=v_v_norm_b, v_w_spatial=v_w_spatial, v_b_spatial=v_b_spatial, v_out_norm_a=v_out_norm_a, v_out_norm_b=v_out_norm_b, v_w_out=v_w_out, v_post_mix_norm=v_post_mix_norm, v_pre_ffn_norm=v_pre_ffn_norm, v_w_up=v_w_up, v_conv_w=v_conv_w, v_conv_b=v_conv_b, v_w_down=v_w_down, v_post_ffn_norm=v_post_ffn_norm)
    weights = {n: given[n] for n in TWIN_WEIGHTS}
    shared = {n: given[n] for n in SHARED_INPUTS}
    per_example = {n: given[n] for n in ['x']}
    grad_fn = _jax.value_and_grad(_loss, argnums=(0, 1))

    def one_microbatch(ex, loss_target):
        ex = dict(ex)
        diff = ex.pop(TWIN_DIFF_INPUT)
        return grad_fn(weights, diff, {**shared, **ex}, loss_target)

    if N_MICROBATCH == 1:
        loss, (grad_w, grad_x) = one_microbatch(per_example, given["loss_target"])
    else:
        def body(carry, xs):
            loss_sum, grad_sum = carry
            l_k, (gw_k, gx_k) = one_microbatch(xs[0], xs[1])
            with _jax.named_scope("update"):
                return (loss_sum + l_k, _jax.tree.map(_jnp.add, grad_sum, gw_k)), gx_k

        init = (_jnp.zeros((), _jnp.float32), _jax.tree.map(_jnp.zeros_like, weights))
        (loss, grad_w), grad_x = _jax.lax.scan(body, init, (per_example, given["loss_target"]))
    with _jax.named_scope("update"):
        delta_w, new_m, new_v = {}, {}, {}
        for n in TWIN_WEIGHTS:
            delta_w[n], new_m[n], new_v[n] = _adamw(weights[n], grad_w[n], given["m_" + n], given["v_" + n])
    return (loss, grad_x, *[grad_w[n] for n in TWIN_WEIGHTS], *[delta_w[n] for n in TWIN_WEIGHTS],
            *[new_m[n] for n in TWIN_WEIGHTS], *[new_v[n] for n in TWIN_WEIGHTS])
```

```python
import functools
import math
import types

import jax
import jax.numpy as jnp
from jax import lax
from jax.experimental import pallas as pl
from jax.experimental.pallas import tpu as pltpu

F32 = jnp.float32
BF16 = jnp.bfloat16

D_MODEL = 1024
A_WIDTH = 512
A_GROUPS = 4
CHUNK = 128
B_WIDTH = 512
HEAD_DIM = 64
N_HEADS = B_WIDTH // HEAD_DIM
ROT_DIM = 16
ROPE_THETA = 500000.0
BAND = 128
DILATIONS = (1, 4, 16)
IN_COLS = 2560
D_FF = 4096
EPS = 1e-6
NEG_INF = -1e30
N_DEV = 8
N_LAYERS = 2

ADAM_LR = 0.001
ADAM_B1 = 0.9
ADAM_B2 = 0.999
ADAM_EPS = 1e-08
ADAM_WD = 0.01
ADAM_STEP = 10

VMEM_LIMIT_BYTES = 56 * 1024 * 1024
MESH_ID = pl.DeviceIdType.MESH
ANY = pl.BlockSpec(memory_space=pl.ANY)
AXES = ("x", "y", "c")

WEIGHT_NAMES = ("pre_mix_norm", "w_in", "v_norm_g", "v_norm_b", "w_spatial", "b_spatial", "out_norm_a", "out_norm_b",
                "w_out", "post_mix_norm", "pre_ffn_norm", "w_up", "conv_w", "conv_b", "w_down", "post_ffn_norm")
BIG_NAMES = ("w_in", "w_out", "w_up", "w_down", "conv_w")
SMALL_NAMES = tuple(n for n in WEIGHT_NAMES if n not in BIG_NAMES)

PACK_ROWS = {"w_in": 640, "w_out": 256, "w_up": 2048, "w_down": 1024, "conv_w": 6}
CONV_W_PAD = 2
PACKED_F32_ROWS = 4096
PACKED_BF16_ROWS = 3984
SMALL_ROWS = 160


def _params(*sem):
    return pltpu.CompilerParams(dimension_semantics=sem, vmem_limit_bytes=VMEM_LIMIT_BYTES)


def _dotg(a, b, ca, cb):
    return lax.dot_general(a.astype(BF16), b.astype(BF16), (((ca,), (cb,)), ((), ())), preferred_element_type=F32)


@jax.custom_vjp
def _bdot(a, b):
    return _dotg(a, b, 1, 0)


def _bdot_fwd(a, b):
    return _dotg(a, b, 1, 0), (a, b)


def _bdot_bwd(res, g):
    a, b = res
    return _dotg(g, b, 1, 1), _dotg(a, g, 0, 0)


_bdot.defvjp(_bdot_fwd, _bdot_bwd)


def _rms(x, g):
    return x * lax.rsqrt(jnp.mean(x * x, axis=-1, keepdims=True) + EPS) * g


def _layernorm(x, g, b):
    mu = jnp.mean(x, axis=-1, keepdims=True)
    xc = x - mu
    return xc * lax.rsqrt(jnp.mean(xc * xc, axis=-1, keepdims=True) + EPS) * g + b


def _gelu_erf(x):
    return x * (lax.erf(x * (1.0 / math.sqrt(2.0))) + 1.0) * 0.5


def _gelu_tanh(x):
    c = math.sqrt(2.0 / math.pi)
    return 0.5 * x * (1.0 + jnp.tanh(c * (x + 0.044715 * (x * x * x))))


def _rot_half(x):
    width = x.shape[1]
    lane = lax.broadcasted_iota(jnp.int32, x.shape, 1) % HEAD_DIM
    back = pltpu.roll(x, ROT_DIM // 2, 1)
    fwd = pltpu.roll(x, width - ROT_DIM // 2, 1)
    return jnp.where(lane < ROT_DIM // 2, -fwd, jnp.where(lane < ROT_DIM, back, 0.0))


def _split3(z):
    h0 = z.astype(BF16)
    r1 = z - h0.astype(F32)
    h1 = r1.astype(BF16)
    h2 = (r1 - h1.astype(F32)).astype(BF16)
    return h0, h1, h2


def _head_sum(z):
    width = z.shape[1]
    a = lax.broadcasted_iota(jnp.int32, (width, width), 0) // HEAD_DIM
    b = lax.broadcasted_iota(jnp.int32, (width, width), 1) // HEAD_DIM
    ones = jnp.where(a == b, 1.0, 0.0).astype(BF16)
    out = None
    for part in _split3(z):
        t = lax.dot_general(part, ones, (((1,), (0,)), ((), ())), preferred_element_type=F32)
        out = t if out is None else out + t
    return out


def _matmul(a, b, *, mode, out_dtype, name, tm=512, tn=1024, tk=1024):
    if mode == "nn":
        (m, k), (_, n) = a.shape, b.shape
    elif mode == "nt":
        (m, k), (n, _) = a.shape, b.shape
    else:
        (k, m), (_, n) = a.shape, b.shape
    tm, tn, tk = min(tm, m), min(tn, n), min(tk, k)
    assert m % tm == 0 and n % tn == 0 and k % tk == 0, (name, m, n, k)
    nk = k // tk
    if mode == "nn":
        a_spec = pl.BlockSpec((tm, tk), lambda i, j, kk: (i, kk))
        b_spec = pl.BlockSpec((tk, tn), lambda i, j, kk: (kk, j))
        ca, cb = 1, 0
    elif mode == "nt":
        a_spec = pl.BlockSpec((tm, tk), lambda i, j, kk: (i, kk))
        b_spec = pl.BlockSpec((tn, tk), lambda i, j, kk: (j, kk))
        ca, cb = 1, 1
    else:
        a_spec = pl.BlockSpec((tk, tm), lambda i, j, kk: (kk, i))
        b_spec = pl.BlockSpec((tk, tn), lambda i, j, kk: (kk, j))
        ca, cb = 0, 0

    def body(a_ref, b_ref, o_ref, acc_ref):
        kk = pl.program_id(2)
        part = lax.dot_general(a_ref[...], b_ref[...], (((ca,), (cb,)), ((), ())), preferred_element_type=F32)

        @pl.when(kk == 0)
        def _():
            acc_ref[...] = part

        @pl.when(kk > 0)
        def _():
            acc_ref[...] += part

        @pl.when(kk == nk - 1)
        def _():
            o_ref[...] = acc_ref[...].astype(o_ref.dtype)

    return pl.pallas_call(
        body, name=name, grid=(m // tm, n // tn, nk),
        in_specs=[a_spec, b_spec], out_specs=pl.BlockSpec((tm, tn), lambda i, j, kk: (i, j)),
        out_shape=jax.ShapeDtypeStruct((m, n), out_dtype),
        scratch_shapes=[pltpu.VMEM((tm, tn), F32)],
        compiler_params=_params("parallel", "parallel", "arbitrary"),
    )(a, b)


def _rowwise(fn, rows, consts, out_rows, out_acc, *, ts, name):
    s = rows[0][0].shape[0]
    assert s % ts == 0, (name, s, ts)
    n_in = len(rows) + len(consts)
    n_row = len(out_rows)

    def body(*refs):
        vals = [r[...] for r in refs[:n_in]]
        row_vals, acc_vals = fn(*vals)
        for r, v in zip(refs[n_in:n_in + n_row], row_vals):
            r[...] = v.astype(r.dtype)
        first = pl.program_id(0) == 0
        for r, v in zip(refs[n_in + n_row:], acc_vals):
            @pl.when(first)
            def _(r=r, v=v):
                r[...] = v

            @pl.when(jnp.logical_not(first))
            def _(r=r, v=v):
                r[...] += v

    in_specs = [pl.BlockSpec((ts, w), functools.partial(lambda i, cb: (i, cb), cb=cb)) for _, w, cb in rows]
    in_specs += [pl.BlockSpec(c.shape, lambda i: (0, 0)) for c in consts]
    out_specs = [pl.BlockSpec((ts, w), lambda i: (i, 0)) for w, _ in out_rows]
    out_specs += [pl.BlockSpec(sh, lambda i: (0, 0)) for sh in out_acc]
    out_shape = [jax.ShapeDtypeStruct((s, w), dt) for w, dt in out_rows]
    out_shape += [jax.ShapeDtypeStruct(sh, F32) for sh in out_acc]
    outs = pl.pallas_call(
        body, name=name, grid=(s // ts,), in_specs=in_specs, out_specs=out_specs, out_shape=out_shape,
        compiler_params=_params("arbitrary" if out_acc else "parallel"),
    )(*[a for a, _, _ in rows], *consts)
    return outs[:n_row], outs[n_row:]


def _full(a):
    return (a, a.shape[1], 0)


def _gate_fn(zu, zv, vg, vb, ws0, ws1, ws2, ws3, bfull, ga):
    u = _gelu_erf(zu)
    vn = _layernorm(_gelu_erf(zv), vg, vb)
    p = lax.broadcasted_iota(jnp.int32, (CHUNK, CHUNK), 0)
    q = lax.broadcasted_iota(jnp.int32, (CHUNK, CHUNK), 1)
    tril = jnp.where(q <= p, 1.0, 0.0)
    group = lax.broadcasted_iota(jnp.int32, (1, A_WIDTH), 1) // CHUNK
    sg = bfull
    for g, w in enumerate((ws0, ws1, ws2, ws3)):
        sg = sg + _bdot(w * tril, jnp.where(group == g, vn, 0.0))
    return _rms(u * sg, ga)


def _bias_reduce(dbf, name):
    def body(x_ref, o_ref):
        lane = lax.broadcasted_iota(jnp.int32, (CHUNK, CHUNK), 1)
        out = jnp.zeros((CHUNK, CHUNK), F32)
        for g in range(A_GROUPS):
            out = jnp.where(lane == g, jnp.sum(x_ref[:, g * CHUNK:(g + 1) * CHUNK], axis=1, keepdims=True), out)
        o_ref[...] = out

    return pl.pallas_call(body, name=name, out_shape=jax.ShapeDtypeStruct((CHUNK, CHUNK), F32))(dbf)


def _pair_mask(hh):
    lane = lax.broadcasted_iota(jnp.int32, (1, 2 * HEAD_DIM), 1)
    return (lane >= HEAD_DIM * hh) & (lane < HEAD_DIM * (hh + 1))


def _lane_pick(x2, lm):
    return jnp.max(jnp.where(lm, x2, -jnp.inf), axis=1, keepdims=True)


def _attn_specs(nb):
    cur = pl.BlockSpec((BAND, B_WIDTH), lambda r, j: (j, r))
    prev = pl.BlockSpec((BAND, B_WIDTH), lambda r, j: (jnp.maximum(j - 1, 0), r))
    nxt = pl.BlockSpec((BAND, B_WIDTH), lambda r, j: (jnp.minimum(j + 1, nb - 1), r))
    return cur, prev, nxt


def _band_valid_q(j):
    row = lax.broadcasted_iota(jnp.int32, (BAND, 2 * BAND), 0)
    col = lax.broadcasted_iota(jnp.int32, (BAND, 2 * BAND), 1)
    return (col >= row) & (col <= row + BAND) & ((col >= BAND) | (j > 0))


def _attn_fwd(q, k, v, d, name):
    nb = q.shape[0] // BAND
    cur, prev, _ = _attn_specs(nb)

    def body(q_ref, kc_ref, kp_ref, vc_ref, vp_ref, o_ref, l_ref):
        valid = _band_valid_q(pl.program_id(1))
        for hp in range(N_HEADS // 2):
            sl = slice(2 * HEAD_DIM * hp, 2 * HEAD_DIM * (hp + 1))
            q2 = q_ref[:, sl]
            k2 = jnp.concatenate([kp_ref[:, sl], kc_ref[:, sl]], axis=0)
            v2 = jnp.concatenate([vp_ref[:, sl], vc_ref[:, sl]], axis=0)
            o2 = jnp.zeros((BAND, 2 * HEAD_DIM), F32)
            l2 = jnp.zeros((BAND, 2 * HEAD_DIM), F32)
            for hh in range(2):
                lm = _pair_mask(hh)
                s = _dotg(jnp.where(lm, q2, jnp.zeros_like(q2)), k2, 1, 1) * (HEAD_DIM ** -0.5)
                s = jnp.where(valid, s, NEG_INF)
                m = jnp.max(s, axis=1, keepdims=True)
                p = jnp.exp(s - m)
                den = jnp.sum(p, axis=1, keepdims=True)
                o = _dotg(p, v2, 1, 0) / den
                o2 = jnp.where(lm, o, o2)
                l2 = jnp.where(lm, m + jnp.log(den), l2)
            o_ref[:, sl] = o2
            l_ref[:, sl] = l2

    return pl.pallas_call(
        body, name=name, grid=(d, nb), in_specs=[cur, cur, prev, cur, prev], out_specs=[cur, cur],
        out_shape=[jax.ShapeDtypeStruct(q.shape, F32), jax.ShapeDtypeStruct(q.shape, F32)],
        compiler_params=_params("parallel", "parallel"),
    )(q, k, k, v, v)


def _attn_bwd_q(q, k, v, do, lse, delta, d, name):
    nb = q.shape[0] // BAND
    cur, prev, _ = _attn_specs(nb)

    def body(q_ref, kc_ref, kp_ref, vc_ref, vp_ref, do_ref, l_ref, dl_ref, dq_ref):
        valid = _band_valid_q(pl.program_id(1))
        for hp in range(N_HEADS // 2):
            sl = slice(2 * HEAD_DIM * hp, 2 * HEAD_DIM * (hp + 1))
            q2, do2, l2, dl2 = q_ref[:, sl], do_ref[:, sl], l_ref[:, sl], dl_ref[:, sl]
            k2 = jnp.concatenate([kp_ref[:, sl], kc_ref[:, sl]], axis=0)
            v2 = jnp.concatenate([vp_ref[:, sl], vc_ref[:, sl]], axis=0)
            dq2 = jnp.zeros((BAND, 2 * HEAD_DIM), F32)
            for hh in range(2):
                lm = _pair_mask(hh)
                s = _dotg(jnp.where(lm, q2, jnp.zeros_like(q2)), k2, 1, 1) * (HEAD_DIM ** -0.5)
                s = jnp.where(valid, s, NEG_INF)
                p = jnp.exp(s - _lane_pick(l2, lm))
                dp = _dotg(jnp.where(lm, do2, jnp.zeros_like(do2)), v2, 1, 1)
                ds = p * (dp - _lane_pick(dl2, lm))
                dq2 = jnp.where(lm, _dotg(ds, k2, 1, 0) * (HEAD_DIM ** -0.5), dq2)
            dq_ref[:, sl] = dq2

    return pl.pallas_call(
        body, name=name, grid=(d, nb), in_specs=[cur, cur, prev, cur, prev, cur, cur, cur], out_specs=cur,
        out_shape=jax.ShapeDtypeStruct(q.shape, F32),
        compiler_params=_params("parallel", "parallel"),
    )(q, k, k, v, v, do, lse, delta)


def _attn_bwd_kv(q, k, v, do, lse_t, delta_t, d, name):
    nb = q.shape[0] // BAND
    cur, _, nxt = _attn_specs(nb)
    t_cur = pl.BlockSpec((1, N_HEADS, BAND), lambda r, j: (r, 0, j))
    t_nxt = pl.BlockSpec((1, N_HEADS, BAND), lambda r, j: (r, 0, jnp.minimum(j + 1, nb - 1)))

    def body(k_ref, v_ref, qc_ref, qn_ref, doc_ref, don_ref, lc_ref, ln_ref, dlc_ref, dln_ref, dk_ref, dv_ref):
        j = pl.program_id(1)
        row = lax.broadcasted_iota(jnp.int32, (BAND, 2 * BAND), 0)
        col = lax.broadcasted_iota(jnp.int32, (BAND, 2 * BAND), 1)
        valid = (col >= row) & (col <= row + BAND) & ((col < BAND) | (j < nb - 1))
        for hp in range(N_HEADS // 2):
            sl = slice(2 * HEAD_DIM * hp, 2 * HEAD_DIM * (hp + 1))
            k2, v2 = k_ref[:, sl], v_ref[:, sl]
            q2 = jnp.concatenate([qc_ref[:, sl], qn_ref[:, sl]], axis=0)
            do2 = jnp.concatenate([doc_ref[:, sl], don_ref[:, sl]], axis=0)
            dk2 = jnp.zeros((BAND, 2 * HEAD_DIM), F32)
            dv2 = jnp.zeros((BAND, 2 * HEAD_DIM), F32)
            for hh in range(2):
                h = 2 * hp + hh
                lm = _pair_mask(hh)
                lse_row = jnp.concatenate([lc_ref[0, h:h + 1, :], ln_ref[0, h:h + 1, :]], axis=1)
                dl_row = jnp.concatenate([dlc_ref[0, h:h + 1, :], dln_ref[0, h:h + 1, :]], axis=1)
                st = _dotg(jnp.where(lm, k2, jnp.zeros_like(k2)), q2, 1, 1) * (HEAD_DIM ** -0.5)
                st = jnp.where(valid, st, NEG_INF)
                pt = jnp.exp(st - lse_row)
                dv2 = jnp.where(lm, _dotg(pt, do2, 1, 0), dv2)
                dpt = _dotg(jnp.where(lm, v2, jnp.zeros_like(v2)), do2, 1, 1)
                dst = pt * (dpt - dl_row)
                dk2 = jnp.where(lm, _dotg(dst, q2, 1, 0) * (HEAD_DIM ** -0.5), dk2)
            dk_ref[:, sl] = dk2
            dv_ref[:, sl] = dv2

    return pl.pallas_call(
        body, name=name, grid=(d, nb),
        in_specs=[cur, cur, cur, nxt, cur, nxt, t_cur, t_nxt, t_cur, t_nxt], out_specs=[cur, cur],
        out_shape=[jax.ShapeDtypeStruct(q.shape, F32), jax.ShapeDtypeStruct(q.shape, F32)],
        compiler_params=_params("parallel", "parallel"),
    )(k, v, q, q, do, do, lse_t, lse_t, delta_t, delta_t)


def _to_residues(a, d):
    return a if d == 1 else a.reshape(a.shape[0] // d, d * a.shape[1])


def _from_residues(a, d):
    return a if d == 1 else a.reshape(a.shape[0] * d, a.shape[1] // d)


def _head_rows(a, d):
    s = a.shape[0]
    return a[:, ::HEAD_DIM].reshape(s // d, d, N_HEADS).transpose(1, 2, 0)


FF_TS = 256
FF_TC = 512
HALO = 8


def _conv3(ext, w, b):
    return b + w[0:1, :] * pltpu.roll(ext, 2, 0) + w[1:2, :] * pltpu.roll(ext, 1, 0) + w[2:3, :] * ext


def _ffn_specs(s, cols_first):
    nrb = s // FF_TS
    per = FF_TS // HALO

    def mk(block, fn):
        if cols_first:
            return pl.BlockSpec(block, lambda j, i: fn(i, j))
        return pl.BlockSpec(block, lambda i, j: fn(i, j))

    specs = types.SimpleNamespace(
        nrb=nrb, ncb=D_FF // FF_TC,
        row=mk((FF_TS, FF_TC), lambda i, j: (i, j)),
        before=mk((HALO, FF_TC), lambda i, j: (jnp.maximum(i * per - 1, 0), j)),
        after=mk((HALO, FF_TC), lambda i, j: (jnp.minimum((i + 1) * per, nrb * per - 1), j)),
        w=mk((3, FF_TC), lambda i, j: (0, j)),
        b=mk((1, FF_TC), lambda i, j: (0, j)),
        acc=mk((HALO, FF_TC), lambda i, j: (0, j)),
    )
    return specs


def _geglu_fwd(ug, uv, wg, wv, bg, bv, name):
    s = ug.shape[0]
    sp = _ffn_specs(s, False)

    def body(ug_ref, uv_ref, hg_ref, hv_ref, wg_ref, wv_ref, bg_ref, bv_ref, y_ref):
        keep = jnp.where(pl.program_id(0) > 0, 1.0, 0.0)
        eg = jnp.concatenate([hg_ref[...] * keep, ug_ref[...]], axis=0)
        ev = jnp.concatenate([hv_ref[...] * keep, uv_ref[...]], axis=0)
        gate = _conv3(eg, wg_ref[...], bg_ref[...])[HALO:, :]
        val = _conv3(ev, wv_ref[...], bv_ref[...])[HALO:, :]
        y_ref[...] = (_gelu_tanh(gate) * val).astype(y_ref.dtype)

    return pl.pallas_call(
        body, name=name, grid=(sp.nrb, sp.ncb),
        in_specs=[sp.row, sp.row, sp.before, sp.before, sp.w, sp.w, sp.b, sp.b],
        out_specs=sp.row, out_shape=jax.ShapeDtypeStruct((s, D_FF), BF16),
        compiler_params=_params("parallel", "parallel"),
    )(ug, uv, ug, uv, wg, wv, bg, bv)


def _geglu_bwd(ug, uv, dy, wg, wv, bg, bv, name):
    s = ug.shape[0]
    sp = _ffn_specs(s, True)
    nrb = sp.nrb
    rows = FF_TS + 2 * HALO

    def body(ug_ref, uv_ref, hg_ref, hv_ref, ng_ref, nv_ref, dy_ref, dyn_ref, wg_ref, wv_ref, bg_ref, bv_ref,
             dug_ref, duv_ref, dwg_ref, dwv_ref):
        i = pl.program_id(1)
        keep_top = jnp.where(i > 0, 1.0, 0.0)
        keep_bot = jnp.where(i < nrb - 1, 1.0, 0.0)
        eg = jnp.concatenate([hg_ref[...] * keep_top, ug_ref[...], ng_ref[...]], axis=0)
        ev = jnp.concatenate([hv_ref[...] * keep_top, uv_ref[...], nv_ref[...]], axis=0)
        dye = jnp.concatenate([jnp.zeros((HALO, FF_TC), F32), dy_ref[...], dyn_ref[...] * keep_bot], axis=0)
        wg_, wv_ = wg_ref[...], wv_ref[...]
        gate = _conv3(eg, wg_, bg_ref[...])
        val = _conv3(ev, wv_, bv_ref[...])
        act, act_vjp = jax.vjp(_gelu_tanh, gate)
        dgate = act_vjp(dye * val)[0]
        dval = dye * act

        def back(dc, w):
            return w[2:3, :] * dc + w[1:2, :] * pltpu.roll(dc, rows - 1, 0) + w[0:1, :] * pltpu.roll(dc, rows - 2, 0)

        dug_ref[...] = back(dgate, wg_)[HALO:HALO + FF_TS, :].astype(dug_ref.dtype)
        duv_ref[...] = back(dval, wv_)[HALO:HALO + FF_TS, :].astype(duv_ref.dtype)

        def wsum(dc, e):
            dcc = dc[HALO:HALO + FF_TS, :]
            parts = [jnp.sum(dcc * pltpu.roll(e, 2, 0)[HALO:HALO + FF_TS, :], axis=0, keepdims=True),
                     jnp.sum(dcc * pltpu.roll(e, 1, 0)[HALO:HALO + FF_TS, :], axis=0, keepdims=True),
                     jnp.sum(dcc * e[HALO:HALO + FF_TS, :], axis=0, keepdims=True),
                     jnp.sum(dcc, axis=0, keepdims=True),
                     jnp.zeros((HALO - 4, FF_TC), F32)]
            return jnp.concatenate(parts, axis=0)

        sg, sv = wsum(dgate, eg), wsum(dval, ev)

        @pl.when(i == 0)
        def _():
            dwg_ref[...] = sg
            dwv_ref[...] = sv

        @pl.when(i > 0)
        def _():
            dwg_ref[...] += sg
            dwv_ref[...] += sv

    return pl.pallas_call(
        body, name=name, grid=(sp.ncb, nrb),
        in_specs=[sp.row, sp.row, sp.before, sp.before, sp.after, sp.after, sp.row, sp.after,
                  sp.w, sp.w, sp.b, sp.b],
        out_specs=[sp.row, sp.row, sp.acc, sp.acc],
        out_shape=[jax.ShapeDtypeStruct((s, D_FF), BF16), jax.ShapeDtypeStruct((s, D_FF), BF16),
                   jax.ShapeDtypeStruct((HALO, D_FF), F32), jax.ShapeDtypeStruct((HALO, D_FF), F32)],
        compiler_params=_params("parallel", "arbitrary"),
    )(ug, uv, ug, uv, ug, uv, dy, dy, wg, wv, bg, bv)


def _rope_tables(s):
    inv = ROPE_THETA ** (-jnp.arange(0, ROT_DIM, 2, dtype=F32) / ROT_DIM)
    ang = jnp.arange(s, dtype=F32)[:, None] * inv[None, :]
    cos8, sin8 = jnp.cos(ang), jnp.sin(ang)
    rest = HEAD_DIM - ROT_DIM
    cos_h = jnp.concatenate([cos8, cos8, jnp.ones((s, rest), F32)], axis=1)
    sin_h = jnp.concatenate([sin8, sin8, jnp.zeros((s, rest), F32)], axis=1)
    return jnp.tile(cos_h, (1, N_HEADS)), jnp.tile(sin_h, (1, N_HEADS))


def _layer_fwd(x, w, cos, sin):
    sv = types.SimpleNamespace(x=x)
    (sv.h1,), _ = _rowwise(lambda xb, g: ((_rms(xb, g),), ()), [_full(x)], [w.g_pre], [(D_MODEL, BF16)], [],
                           ts=512, name="pre_mix_norm")
    sv.proj = _matmul(sv.h1, w.w_in, mode="nn", out_dtype=F32, name="proj", tn=1280)

    gate_consts = [w.vg, w.vb, *w.ws, w.bfull, w.ga]
    (na,), _ = _rowwise(lambda *a: ((_gate_fn(*a),), ()), [(sv.proj, A_WIDTH, 0), (sv.proj, A_WIDTH, 1)], gate_consts,
                        [(A_WIDTH, BF16)], [], ts=CHUNK, name="gate_fwd")

    def rope_fn(qr, kr, vr, cs, sn):
        return (qr * cs + _rot_half(qr) * sn, kr * cs + _rot_half(kr) * sn, vr), ()

    (sv.q, sv.k, sv.v), _ = _rowwise(
        rope_fn, [(sv.proj, B_WIDTH, 2), (sv.proj, B_WIDTH, 3), (sv.proj, B_WIDTH, 4), _full(cos), _full(sin)], [],
        [(B_WIDTH, BF16)] * 3, [], ts=512, name="rope_fwd")

    branch = []
    for d in DILATIONS:
        o, l = _attn_fwd(_to_residues(sv.q, d), _to_residues(sv.k, d), _to_residues(sv.v, d), d, name=f"attn_fwd_d{d}")
        branch += [_full(_from_residues(o, d)), _full(_from_residues(l, d))]

    def combine_fn(o1, l1, o2, l2, o3, l3, nab, gb):
        m = jnp.maximum(jnp.maximum(l1, l2), l3)
        e1, e2, e3 = jnp.exp(l1 - m), jnp.exp(l2 - m), jnp.exp(l3 - m)
        den = e1 + e2 + e3
        ob = (e1 / den) * o1 + (e2 / den) * o2 + (e3 / den) * o3
        mixed = jnp.concatenate([nab, _rms(ob, gb).astype(BF16)], axis=1)
        return (mixed, ob, m + jnp.log(den)), ()

    (sv.mixed, sv.ob, sv.lse), _ = _rowwise(combine_fn, branch + [_full(na)], [w.gb],
                                            [(D_MODEL, BF16), (B_WIDTH, F32), (B_WIDTH, F32)], [], ts=512, name="combine")
    sv.y = _matmul(sv.mixed, w.w_out, mode="nn", out_dtype=F32, name="mix_out")

    def mid_fn(xb, yb, g1, g2):
        x1 = xb + _rms(yb, g1)
        return (x1, _rms(x1, g2)), ()

    (sv.x1, sv.h2), _ = _rowwise(mid_fn, [_full(x), _full(sv.y)], [w.g_pm, w.g_pf], [(D_MODEL, F32), (D_MODEL, BF16)], [],
                                 ts=512, name="post_mix_norm")
    sv.ug = _matmul(sv.h2, w.w_up_g, mode="nn", out_dtype=F32, name="ffn_up_gate")
    sv.uv = _matmul(sv.h2, w.w_up_v, mode="nn", out_dtype=F32, name="ffn_up_val")
    sv.yff = _geglu_fwd(sv.ug, sv.uv, w.cw_g, w.cw_v, w.cb_g, w.cb_v, name="geglu_fwd")
    sv.f = _matmul(sv.yff, w.w_down, mode="nn", out_dtype=F32, name="ffn_down")
    (x2,), _ = _rowwise(lambda xb, fb, g: ((xb + _rms(fb, g),), ()), [_full(sv.x1), _full(sv.f)], [w.g_post],
                        [(D_MODEL, F32)], [], ts=512, name="post_ffn_norm")
    return x2, sv


def _layer_bwd(dx2, sv, w, cos, sin):
    g = {}

    def post_fn(fb, dxb, gp):
        _, vjp = jax.vjp(_rms, fb, gp)
        df, dg = vjp(dxb)
        return (df,), (dg,)

    (df,), (g["post_ffn_norm"],) = _rowwise(post_fn, [_full(sv.f), _full(dx2)], [w.g_post], [(D_MODEL, BF16)],
                                            [(1, D_MODEL)], ts=512, name="post_ffn_norm_bwd")
    dyff = _matmul(df, w.w_down, mode="nt", out_dtype=F32, name="ffn_down_dx")
    g["w_down"] = _matmul(sv.yff, df, mode="tn", out_dtype=F32, name="ffn_down_dw")
    dug, duv, dwg, dwv = _geglu_bwd(sv.ug, sv.uv, dyff, w.cw_g, w.cw_v, w.cb_g, w.cb_v, name="geglu_bwd")
    g["conv_w"] = jnp.concatenate([dwg[0:3], dwv[0:3]], axis=1)
    g["conv_b"] = jnp.concatenate([dwg[3], dwv[3]], axis=0)
    dh2a = _matmul(dug, w.w_up_g, mode="nt", out_dtype=F32, name="ffn_up_dx_gate")
    dh2b = _matmul(duv, w.w_up_v, mode="nt", out_dtype=F32, name="ffn_up_dx_val")
    g["w_up"] = jnp.concatenate([_matmul(sv.h2, dug, mode="tn", out_dtype=F32, name="ffn_up_dw_gate"),
                                 _matmul(sv.h2, duv, mode="tn", out_dtype=F32, name="ffn_up_dw_val")], axis=1)

    def mid_fn(x1b, yb, da, db, dxb, g1, g2):
        _, vjp2 = jax.vjp(_rms, x1b, g2)
        dx1h, dg2 = vjp2(da + db)
        dx1 = dxb + dx1h
        _, vjp1 = jax.vjp(_rms, yb, g1)
        dy, dg1 = vjp1(dx1)
        return (dx1, dy), (dg1, dg2)

    (dx1, dy), (g["post_mix_norm"], g["pre_ffn_norm"]) = _rowwise(
        mid_fn, [_full(sv.x1), _full(sv.y), _full(dh2a), _full(dh2b), _full(dx2)], [w.g_pm, w.g_pf],
        [(D_MODEL, F32), (D_MODEL, BF16)], [(1, D_MODEL), (1, D_MODEL)], ts=256, name="post_mix_norm_bwd")
    dmixed = _matmul(dy, w.w_out, mode="nt", out_dtype=F32, name="mix_out_dx")
    g["w_out"] = _matmul(sv.mixed, dy, mode="tn", out_dtype=F32, name="mix_out_dw")

    def attn_out_fn(obb, dmb, gb):
        _, vjp = jax.vjp(_rms, obb, gb)
        do, dgb = vjp(dmb)
        return (do, _head_sum(do * obb)), (dgb,)

    (do, delta), (g["out_norm_b"],) = _rowwise(attn_out_fn, [_full(sv.ob), (dmixed, B_WIDTH, 1)], [w.gb],
                                               [(B_WIDTH, BF16), (B_WIDTH, F32)], [(1, B_WIDTH)], ts=512,
                                               name="attn_out_bwd")
    parts = {"q": [], "k": [], "v": []}
    for d in DILATIONS:
        qv, kv, vv, dov = (_to_residues(a, d) for a in (sv.q, sv.k, sv.v, do))
        dq = _attn_bwd_q(qv, kv, vv, dov, _to_residues(sv.lse, d), _to_residues(delta, d), d, name=f"attn_bwd_q_d{d}")
        dk, dv = _attn_bwd_kv(qv, kv, vv, dov, _head_rows(sv.lse, d), _head_rows(delta, d), d, name=f"attn_bwd_kv_d{d}")
        parts["q"].append(_full(_from_residues(dq, d)))
        parts["k"].append(_full(_from_residues(dk, d)))
        parts["v"].append(_full(_from_residues(dv, d)))

    def rope_bwd_fn(q1, q2, q3, k1, k2, k3, v1, v2, v3, cs, sn):
        def back(t):
            return t * cs - _rot_half(t * sn)
        return (jnp.concatenate([back(q1 + q2 + q3), back(k1 + k2 + k3), v1 + v2 + v3], axis=1),), ()

    (dzb,), _ = _rowwise(rope_bwd_fn, parts["q"] + parts["k"] + parts["v"] + [_full(cos), _full(sin)], [],
                         [(3 * B_WIDTH, BF16)], [], ts=256, name="rope_bwd")

    gate_consts = [w.vg, w.vb, *w.ws, w.bfull, w.ga]

    def gate_bwd_fn(zu, zv, dna, *consts):
        _, vjp = jax.vjp(_gate_fn, zu, zv, *consts)
        grads = vjp(dna)
        return (jnp.concatenate([grads[0], grads[1]], axis=1),), tuple(grads[2:])

    (dza,), gsmall = _rowwise(
        gate_bwd_fn, [(sv.proj, A_WIDTH, 0), (sv.proj, A_WIDTH, 1), (dmixed, A_WIDTH, 0)], gate_consts,
        [(2 * A_WIDTH, BF16)], [c.shape for c in gate_consts], ts=CHUNK, name="gate_bwd")
    g["v_norm_g"], g["v_norm_b"] = gsmall[0], gsmall[1]
    g["w_spatial"] = jnp.stack(gsmall[2:6])
    g["b_spatial"] = _bias_reduce(gsmall[6], name="bias_reduce")[:, :A_GROUPS].T
    g["out_norm_a"] = gsmall[7]

    dproj = jnp.concatenate([dza, dzb], axis=1)
    dh1 = _matmul(dproj, w.w_in, mode="nt", out_dtype=F32, name="proj_dx", tk=1280)
    g["w_in"] = _matmul(sv.h1, dproj, mode="tn", out_dtype=F32, name="proj_dw", tn=1280)

    def pre_fn(xb, dhb, dxb, gp):
        _, vjp = jax.vjp(_rms, xb, gp)
        dxh, dg = vjp(dhb)
        return (dxb + dxh,), (dg,)

    (dx,), (g["pre_mix_norm"],) = _rowwise(pre_fn, [_full(sv.x), _full(dh1), _full(dx1)], [w.g_pre], [(D_MODEL, F32)],
                                           [(1, D_MODEL)], ts=512, name="pre_mix_norm_bwd")
    return dx, g


def _layer_weights(l, full, small):
    row = lambda a: a[l].reshape(1, -1)
    return types.SimpleNamespace(
        g_pre=row(small["pre_mix_norm"]), w_in=full["w_in"][l], vg=row(small["v_norm_g"]), vb=row(small["v_norm_b"]),
        ws=[small["w_spatial"][l, gi] for gi in range(A_GROUPS)],
        bfull=jnp.repeat(small["b_spatial"][l].T, CHUNK, axis=1),
        ga=row(small["out_norm_a"]), gb=row(small["out_norm_b"]), w_out=full["w_out"][l],
        g_pm=row(small["post_mix_norm"]), g_pf=row(small["pre_ffn_norm"]),
        w_up_g=full["w_up"][l][:, :D_FF], w_up_v=full["w_up"][l][:, D_FF:],
        cw_g=full["conv_w"][l][:, :D_FF], cw_v=full["conv_w"][l][:, D_FF:],
        cb_g=small["conv_b"][l][:D_FF].reshape(1, -1), cb_v=small["conv_b"][l][D_FF:].reshape(1, -1),
        w_down=full["w_down"][l], g_post=row(small["post_ffn_norm"]))


def _local_step(x, target, full, small):
    s = x.shape[0]
    cos, sin = _rope_tables(s)
    ws = [_layer_weights(l, full, small) for l in range(N_LAYERS)]
    saved = []
    h = x
    for l in range(N_LAYERS):
        h, sv = _layer_fwd(h, ws[l], cos, sin)
        saved.append(sv)

    def loss_fn(yb, tb):
        diff = yb - tb
        return (diff * (1.0 / D_MODEL),), (jnp.sum(diff * diff, axis=0, keepdims=True),)

    (dh,), (sq,) = _rowwise(loss_fn, [_full(h), _full(target)], [], [(D_MODEL, F32)], [(1, D_MODEL)], ts=512, name="loss")
    loss = 0.5 * jnp.sum(sq) * (1.0 / D_MODEL)
    grads = [None] * N_LAYERS
    for l in reversed(range(N_LAYERS)):
        dh, grads[l] = _layer_bwd(dh, saved[l], ws[l], cos, sin)
    return loss, dh, grads


def _place():
    return lax.axis_index("x"), lax.axis_index("y"), lax.axis_index("c")


def _all_gather(x, name):
    def body(x_ref, out_ref, send_sems, recv_sems, local_sem):
        mx, my, mc = _place()
        me, sibling = (mx, my, mc), (mx, my, 1 - mc)
        chips = [(1 - mx, my), (mx, 1 - my), (1 - mx, 1 - my)]

        def slot(px, py, pc):
            return out_ref.at[4 * px + 2 * py + pc]

        def copy(k, block, to, src=None):
            return pltpu.make_async_remote_copy(
                src_ref=slot(*block) if src is None else src, dst_ref=slot(*block),
                send_sem=send_sems.at[k], recv_sem=recv_sems.at[k], device_id=to, device_id_type=MESH_ID)

        mine = pltpu.make_async_copy(x_ref, slot(*me), local_sem)
        mine.start()
        first = [copy(0, me, sibling, src=x_ref)]
        first += [copy(1 + j, me, (*chip, mc), src=x_ref) for j, chip in enumerate(chips)]
        for cp in first:
            cp.start()
        passed = [copy(4 + j, (*chip, mc), sibling) for j, chip in enumerate(chips)]
        for j, chip in enumerate(chips):
            copy(1 + j, (*chip, mc), me).wait_recv()
            passed[j].start()
        copy(0, sibling, me).wait_recv()
        for j, chip in enumerate(chips):
            copy(4 + j, (*chip, 1 - mc), me).wait_recv()
        for cp in first + passed:
            cp.wait_send()
        mine.wait()

    return pl.pallas_call(
        body, name=name, out_shape=jax.ShapeDtypeStruct((N_DEV,) + x.shape, x.dtype), in_specs=[ANY], out_specs=ANY,
        scratch_shapes=[pltpu.SemaphoreType.DMA((7,)), pltpu.SemaphoreType.DMA((7,)), pltpu.SemaphoreType.DMA],
    )(x)


def _pair_swap(p, name):
    def body(p_ref, got_ref, send_sems, recv_sems):
        mx, my, mc = _place()
        copies = []
        for kchip in range(4):
            cp = pltpu.make_async_remote_copy(
                src_ref=p_ref.at[2 * kchip + (1 - mc)], dst_ref=got_ref.at[kchip],
                send_sem=send_sems.at[kchip], recv_sem=recv_sems.at[kchip],
                device_id=(mx, my, 1 - mc), device_id_type=MESH_ID)
            cp.start()
            copies.append(cp)
        for cp in copies:
            cp.wait()

    return pl.pallas_call(
        body, name=name, out_shape=jax.ShapeDtypeStruct((4,) + p.shape[1:], p.dtype), in_specs=[ANY], out_specs=ANY,
        scratch_shapes=[pltpu.SemaphoreType.DMA((4,)), pltpu.SemaphoreType.DMA((4,))],
    )(p)


def _pair_add(p, got, core, name):
    _, r, c = p.shape
    tr = 512

    def body(core_ref, p_ref, got_ref, q_ref):
        q_ref[...] = p_ref[...] + got_ref[...]

    return pl.pallas_call(
        body, name=name, out_shape=jax.ShapeDtypeStruct((4, r, c), p.dtype),
        grid_spec=pltpu.PrefetchScalarGridSpec(
            num_scalar_prefetch=1, grid=(4, r // tr),
            in_specs=[pl.BlockSpec((1, tr, c), lambda kchip, i, core_ref: (2 * kchip + core_ref[0], i, 0)),
                      pl.BlockSpec((1, tr, c), lambda kchip, i, core_ref: (kchip, i, 0))],
            out_specs=pl.BlockSpec((1, tr, c), lambda kchip, i, core_ref: (kchip, i, 0))),
        compiler_params=_params("parallel", "parallel"),
    )(core, p, got)


def _chip_exchange(q, name):
    def body(q_ref, got_ref, send_sems, recv_sems):
        mx, my, mc = _place()
        chips = [(1 - mx, my), (mx, 1 - my), (1 - mx, 1 - my)]
        copies = []
        for j, (px, py) in enumerate(chips):
            cp = pltpu.make_async_remote_copy(
                src_ref=q_ref.at[2 * px + py], dst_ref=got_ref.at[j],
                send_sem=send_sems.at[j], recv_sem=recv_sems.at[j],
                device_id=(px, py, mc), device_id_type=MESH_ID)
            cp.start()
            copies.append(cp)
        for cp in copies:
            cp.wait()

    return pl.pallas_call(
        body, name=name, out_shape=jax.ShapeDtypeStruct((3,) + q.shape[1:], q.dtype), in_specs=[ANY], out_specs=ANY,
        scratch_shapes=[pltpu.SemaphoreType.DMA((3,)), pltpu.SemaphoreType.DMA((3,))],
    )(q)


def _adamw(w, g, m, v):
    m2 = ADAM_B1 * m + (1.0 - ADAM_B1) * g
    v2 = ADAM_B2 * v + (1.0 - ADAM_B2) * (g * g)
    m_hat = m2 / (1.0 - ADAM_B1 ** ADAM_STEP)
    v_hat = v2 / (1.0 - ADAM_B2 ** ADAM_STEP)
    return -ADAM_LR * (m_hat / (jnp.sqrt(v_hat) + ADAM_EPS) + ADAM_WD * w), m2, v2


def _adamw_sharded(q, got, chip, w, m, v, name):
    r, c = w.shape
    tr = 512

    def body(chip_ref, q_ref, g0_ref, g1_ref, g2_ref, w_ref, m_ref, v_ref, g_ref, d_ref, m2_ref, v2_ref):
        g = ((q_ref[0] + g0_ref[0]) + g1_ref[0]) + g2_ref[0]
        d, m2, v2 = _adamw(w_ref[...], g, m_ref[...], v_ref[...])
        g_ref[...], d_ref[...], m2_ref[...], v2_ref[...] = g, d, m2, v2

    flat = pl.BlockSpec((tr, c), lambda i, chip_ref: (i, 0))
    peer = lambda j: pl.BlockSpec((1, tr, c), functools.partial(lambda i, chip_ref, j: (j, i, 0), j=j))
    return pl.pallas_call(
        body, name=name, out_shape=[jax.ShapeDtypeStruct((r, c), F32)] * 4,
        grid_spec=pltpu.PrefetchScalarGridSpec(
            num_scalar_prefetch=1, grid=(r // tr,),
            in_specs=[pl.BlockSpec((1, tr, c), lambda i, chip_ref: (chip_ref[0], i, 0)), peer(0), peer(1), peer(2),
                      flat, flat, flat],
            out_specs=[flat] * 4),
        compiler_params=_params("parallel"),
    )(chip, q, got, got, got, w, m, v)


def _adamw_replicated(parts, w, m, v, name):
    def body(p_ref, w_ref, m_ref, v_ref, g_ref, d_ref, m2_ref, v2_ref):
        g = p_ref[0]
        for j in range(1, N_DEV):
            g = g + p_ref[j]
        d, m2, v2 = _adamw(w_ref[...], g, m_ref[...], v_ref[...])
        g_ref[...], d_ref[...], m2_ref[...], v2_ref[...] = g, d, m2, v2

    return pl.pallas_call(body, name=name, out_shape=[jax.ShapeDtypeStruct(w.shape, F32)] * 4,
                          compiler_params=pltpu.CompilerParams(vmem_limit_bytes=VMEM_LIMIT_BYTES))(parts, w, m, v)


def _pack_shard(blocks, rows):
    parts = [blocks[n].reshape(PACK_ROWS[n], D_MODEL) for n in BIG_NAMES]
    parts[-1] = jnp.pad(parts[-1], ((0, CONV_W_PAD), (0, 0)))
    used = sum(PACK_ROWS.values()) + CONV_W_PAD
    return jnp.concatenate(parts + [jnp.zeros((rows - used, D_MODEL), parts[0].dtype)], axis=0)


def _unpack_shard(packed, shapes):
    out, at = {}, 0
    for n in BIG_NAMES:
        out[n] = packed[at:at + PACK_ROWS[n]].reshape(shapes[n])
        at += PACK_ROWS[n]
    return out


def _by_destination(grads):
    st = {n: jnp.stack([g[n] for g in grads]) for n in BIG_NAMES}
    per = [
        st["w_in"].reshape(N_LAYERS, D_MODEL, N_DEV, IN_COLS // N_DEV).transpose(2, 0, 1, 3),
        st["w_out"].reshape(N_LAYERS, N_DEV, D_MODEL // N_DEV, D_MODEL).transpose(1, 0, 2, 3),
        st["w_up"].reshape(N_LAYERS, D_MODEL, N_DEV, 2 * D_FF // N_DEV).transpose(2, 0, 1, 3),
        st["w_down"].reshape(N_LAYERS, N_DEV, D_FF // N_DEV, D_MODEL).transpose(1, 0, 2, 3),
        st["conv_w"].reshape(N_LAYERS, 3, N_DEV, 2 * D_FF // N_DEV).transpose(2, 0, 1, 3),
    ]
    parts = [a.reshape(N_DEV, PACK_ROWS[n], D_MODEL) for a, n in zip(per, BIG_NAMES)]
    parts[-1] = jnp.pad(parts[-1], ((0, 0), (0, CONV_W_PAD), (0, 0)))
    used = sum(PACK_ROWS.values()) + CONV_W_PAD
    return jnp.concatenate(parts + [jnp.zeros((N_DEV, PACKED_F32_ROWS - used, D_MODEL), F32)], axis=1)


def _pack_small(vals):
    flat = jnp.concatenate([vals[n].reshape(-1) for n in SMALL_NAMES])
    return jnp.concatenate([flat, jnp.zeros((SMALL_ROWS * D_MODEL - flat.shape[0],), F32)]).reshape(SMALL_ROWS, D_MODEL)


def _unpack_small(packed, shapes):
    flat, out, at = packed.reshape(-1), {}, 0
    for n in SMALL_NAMES:
        size = math.prod(shapes[n])
        out[n] = flat[at:at + size].reshape(shapes[n])
        at += size
    return out


def _gather_weights(wts):
    cw_bits = lax.bitcast_convert_type(wts["conv_w"], BF16).reshape(12, D_MODEL)
    parts = [wts[n].astype(BF16).reshape(PACK_ROWS[n], D_MODEL) for n in ("w_in", "w_out", "w_up", "w_down")]
    pack = jnp.concatenate(parts + [jnp.pad(cw_bits, ((0, 4), (0, 0)))], axis=0)
    got = _all_gather(pack, name="gather_weights")
    at, full = 0, {}
    for n in ("w_in", "w_out", "w_up", "w_down"):
        full[n] = got[:, at:at + PACK_ROWS[n]]
        at += PACK_ROWS[n]
    cols = IN_COLS // N_DEV
    full["w_in"] = full["w_in"].reshape(N_DEV, N_LAYERS, D_MODEL, cols).transpose(1, 2, 0, 3).reshape(N_LAYERS, D_MODEL, IN_COLS)
    full["w_out"] = full["w_out"].reshape(N_DEV, N_LAYERS, D_MODEL // N_DEV, D_MODEL).transpose(1, 0, 2, 3).reshape(
        N_LAYERS, D_MODEL, D_MODEL)
    full["w_up"] = full["w_up"].reshape(N_DEV, N_LAYERS, D_MODEL, 2 * D_FF // N_DEV).transpose(1, 2, 0, 3).reshape(
        N_LAYERS, D_MODEL, 2 * D_FF)
    full["w_down"] = full["w_down"].reshape(N_DEV, N_LAYERS, D_FF // N_DEV, D_MODEL).transpose(1, 0, 2, 3).reshape(
        N_LAYERS, D_FF, D_MODEL)
    cw = lax.bitcast_convert_type(got[:, at:at + 12].reshape(N_DEV, N_LAYERS, 3, 2 * D_FF // N_DEV, 2), F32)
    full["conv_w"] = cw.transpose(1, 2, 0, 3).reshape(N_LAYERS, 3, 2 * D_FF)
    return full


def kernel(x, pre_mix_norm, w_in, v_norm_g, v_norm_b, w_spatial, b_spatial, out_norm_a, out_norm_b, w_out, post_mix_norm, pre_ffn_norm, w_up, conv_w, conv_b, w_down, post_ffn_norm, loss_target, m_pre_mix_norm, m_w_in, m_v_norm_g, m_v_norm_b, m_w_spatial, m_b_spatial, m_out_norm_a, m_out_norm_b, m_w_out, m_post_mix_norm, m_pre_ffn_norm, m_w_up, m_conv_w, m_conv_b, m_w_down, m_post_ffn_norm, v_pre_mix_norm, v_w_in, v_v_norm_g, v_v_norm_b, v_w_spatial, v_b_spatial, v_out_norm_a, v_out_norm_b, v_w_out, v_post_mix_norm, v_pre_ffn_norm, v_w_up, v_conv_w, v_conv_b, v_w_down, v_post_ffn_norm):
    wts = dict(zip(WEIGHT_NAMES, (pre_mix_norm, w_in, v_norm_g, v_norm_b, w_spatial, b_spatial, out_norm_a, out_norm_b,
                                  w_out, post_mix_norm, pre_ffn_norm, w_up, conv_w, conv_b, w_down, post_ffn_norm)))
    mom1 = dict(zip(WEIGHT_NAMES, (m_pre_mix_norm, m_w_in, m_v_norm_g, m_v_norm_b, m_w_spatial, m_b_spatial, m_out_norm_a,
                                   m_out_norm_b, m_w_out, m_post_mix_norm, m_pre_ffn_norm, m_w_up, m_conv_w, m_conv_b,
                                   m_w_down, m_post_ffn_norm)))
    mom2 = dict(zip(WEIGHT_NAMES, (v_pre_mix_norm, v_w_in, v_v_norm_g, v_v_norm_b, v_w_spatial, v_b_spatial, v_out_norm_a,
                                   v_out_norm_b, v_w_out, v_post_mix_norm, v_pre_ffn_norm, v_w_up, v_conv_w, v_conv_b,
                                   v_w_down, v_post_ffn_norm)))
    mx, my, mc = _place()

    full = _gather_weights(wts)
    loss_local, dx, grads = _local_step(x[0], loss_target[0], full, wts)
    loss = lax.psum(loss_local, AXES)

    by_dest = _by_destination(grads)
    from_sibling = _pair_swap(by_dest, name="grad_pair_swap")
    core = jnp.reshape(mc, (1,)).astype(jnp.int32)
    chip_sum = _pair_add(by_dest, from_sibling, core, name="grad_pair_add")
    from_chips = _chip_exchange(chip_sum, name="grad_chip_exchange")
    chip = jnp.reshape(2 * mx + my, (1,)).astype(jnp.int32)
    big = [_pack_shard({n: t[n] for n in BIG_NAMES}, PACKED_F32_ROWS) for t in (wts, mom1, mom2)]
    big_out = _adamw_sharded(chip_sum, from_chips, chip, *big, name="adamw_sharded")
    big_shapes = {n: wts[n].shape for n in BIG_NAMES}
    big_out = [_unpack_shard(o, big_shapes) for o in big_out]

    small_grads = {n: jnp.stack([g[n].reshape(wts[n].shape[1:]) for g in grads]) for n in SMALL_NAMES}
    everyone = _all_gather(_pack_small(small_grads), name="gather_small_grads")
    small = [_pack_small({n: t[n] for n in SMALL_NAMES}) for t in (wts, mom1, mom2)]
    small_out = _adamw_replicated(everyone, *small, name="adamw_replicated")
    small_shapes = {n: wts[n].shape for n in SMALL_NAMES}
    small_out = [_unpack_small(o, small_shapes) for o in small_out]

    outs = [loss, dx[None]]
    for kind in range(4):
        outs += [big_out[kind][n] if n in BIG_NAMES else small_out[kind][n] for n in WEIGHT_NAMES]
    return tuple(outs)
```

```python
import functools
import math
import types

import jax
import jax.numpy as jnp
from jax import lax
from jax.experimental import pallas as pl
from jax.experimental.pallas import tpu as pltpu

F32 = jnp.float32
BF16 = jnp.bfloat16

D_MODEL = 1024
A_WIDTH = 512
A_GROUPS = 4
CHUNK = 128
B_WIDTH = 512
HEAD_DIM = 64
N_HEADS = B_WIDTH // HEAD_DIM
ROT_DIM = 16
ROPE_THETA = 500000.0
BAND = 128
DILATIONS = (1, 4, 16)
IN_COLS = 2560
D_FF = 4096
EPS = 1e-6
NEG_INF = -1e30
N_DEV = 8
N_LAYERS = 2

ADAM_LR = 0.001
ADAM_B1 = 0.9
ADAM_B2 = 0.999
ADAM_EPS = 1e-08
ADAM_WD = 0.01
ADAM_STEP = 10

VMEM_LIMIT_BYTES = 56 * 1024 * 1024
MESH_ID = pl.DeviceIdType.MESH
ANY = pl.BlockSpec(memory_space=pl.ANY)
AXES = ("x", "y", "c")

WEIGHT_NAMES = ("pre_mix_norm", "w_in", "v_norm_g", "v_norm_b", "w_spatial", "b_spatial", "out_norm_a", "out_norm_b",
                "w_out", "post_mix_norm", "pre_ffn_norm", "w_up", "conv_w", "conv_b", "w_down", "post_ffn_norm")
BIG_NAMES = ("w_in", "w_out", "w_up", "w_down", "conv_w")
SMALL_NAMES = tuple(n for n in WEIGHT_NAMES if n not in BIG_NAMES)

PACK_ROWS = {"w_in": 640, "w_out": 256, "w_up": 2048, "w_down": 1024, "conv_w": 6}
CONV_W_PAD = 2
PACKED_F32_ROWS = 4096
PACKED_BF16_ROWS = 3984
SMALL_ROWS = 160


def _params(*sem):
    return pltpu.CompilerParams(dimension_semantics=sem, vmem_limit_bytes=VMEM_LIMIT_BYTES)


def _dotg(a, b, ca, cb):
    return lax.dot_general(a.astype(BF16), b.astype(BF16), (((ca,), (cb,)), ((), ())), preferred_element_type=F32)


@jax.custom_vjp
def _bdot(a, b):
    return _dotg(a, b, 1, 0)


def _bdot_fwd(a, b):
    return _dotg(a, b, 1, 0), (a, b)


def _bdot_bwd(res, g):
    a, b = res
    return _dotg(g, b, 1, 1), _dotg(a, g, 0, 0)


_bdot.defvjp(_bdot_fwd, _bdot_bwd)


def _rms(x, g):
    return x * lax.rsqrt(jnp.mean(x * x, axis=-1, keepdims=True) + EPS) * g


def _layernorm(x, g, b):
    mu = jnp.mean(x, axis=-1, keepdims=True)
    xc = x - mu
    return xc * lax.rsqrt(jnp.mean(xc * xc, axis=-1, keepdims=True) + EPS) * g + b


def _gelu_erf(x):
    return x * (lax.erf(x * (1.0 / math.sqrt(2.0))) + 1.0) * 0.5


def _gelu_tanh(x):
    c = math.sqrt(2.0 / math.pi)
    return 0.5 * x * (1.0 + jnp.tanh(c * (x + 0.044715 * (x * x * x))))


def _rot_half(x):
    width = x.shape[1]
    lane = lax.broadcasted_iota(jnp.int32, x.shape, 1) % HEAD_DIM
    back = pltpu.roll(x, ROT_DIM // 2, 1)
    fwd = pltpu.roll(x, width - ROT_DIM // 2, 1)
    return jnp.where(lane < ROT_DIM // 2, -fwd, jnp.where(lane < ROT_DIM, back, 0.0))


def _split3(z):
    h0 = z.astype(BF16)
    r1 = z - h0.astype(F32)
    h1 = r1.astype(BF16)
    h2 = (r1 - h1.astype(F32)).astype(BF16)
    return h0, h1, h2


def _head_sum(z):
    width = z.shape[1]
    a = lax.broadcasted_iota(jnp.int32, (width, width), 0) // HEAD_DIM
    b = lax.broadcasted_iota(jnp.int32, (width, width), 1) // HEAD_DIM
    ones = jnp.where(a == b, 1.0, 0.0).astype(BF16)
    out = None
    for part in _split3(z):
        t = lax.dot_general(part, ones, (((1,), (0,)), ((), ())), preferred_element_type=F32)
        out = t if out is None else out + t
    return out


def _matmul(a, b, *, mode, out_dtype, name, tm=512, tn=1024, tk=1024, cols=None):
    wide = D_MODEL if cols is not None else None
    if mode == "nn":
        (m, k), (_, n) = a.shape, (b.shape if cols is None else (b.shape[1], cols[1] * wide))
    elif mode == "nt":
        (m, k), (n, _) = a.shape, (b.shape if cols is None else (b.shape[1], cols[1] * wide))
    else:
        (k, m), (_, n) = a.shape, b.shape
    tm, tn, tk = min(tm, m), min(tn, n), min(tk, k)
    assert m % tm == 0 and n % tn == 0 and k % tk == 0, (name, m, n, k)
    nk = k // tk
    if mode == "nn":
        a_spec = pl.BlockSpec((tm, tk), lambda i, j, kk: (i, kk))
        b_spec = pl.BlockSpec((tk, tn), lambda i, j, kk: (kk, j))
        if cols is not None:
            assert tn == wide
            b_spec = pl.BlockSpec((None, tk, tn), lambda i, j, kk: (cols[0] + j, kk, 0))
        ca, cb = 1, 0
    elif mode == "nt":
        a_spec = pl.BlockSpec((tm, tk), lambda i, j, kk: (i, kk))
        b_spec = pl.BlockSpec((tn, tk), lambda i, j, kk: (j, kk))
        if cols is not None:
            assert tk == wide
            b_spec = pl.BlockSpec((None, tn, tk), lambda i, j, kk: (cols[0] + kk, j, 0))
        ca, cb = 1, 1
    else:
        a_spec = pl.BlockSpec((tk, tm), lambda i, j, kk: (kk, i))
        b_spec = pl.BlockSpec((tk, tn), lambda i, j, kk: (kk, j))
        ca, cb = 0, 0

    def body(a_ref, b_ref, o_ref, acc_ref):
        kk = pl.program_id(2)
        part = lax.dot_general(a_ref[...], b_ref[...], (((ca,), (cb,)), ((), ())), preferred_element_type=F32)

        @pl.when(kk == 0)
        def _():
            acc_ref[...] = part

        @pl.when(kk > 0)
        def _():
            acc_ref[...] += part

        @pl.when(kk == nk - 1)
        def _():
            o_ref[...] = acc_ref[...].astype(o_ref.dtype)

    return pl.pallas_call(
        body, name=name, grid=(m // tm, n // tn, nk),
        in_specs=[a_spec, b_spec], out_specs=pl.BlockSpec((tm, tn), lambda i, j, kk: (i, j)),
        out_shape=jax.ShapeDtypeStruct((m, n), out_dtype),
        scratch_shapes=[pltpu.VMEM((tm, tn), F32)],
        compiler_params=_params("parallel", "parallel", "arbitrary"),
    )(a, b)


def _matmul_by_destination(a, b_lo, b_hi, *, name, tm=512, tk=1024):
    (k, m), half = a.shape, N_DEV // 2
    assert b_lo.shape == b_hi.shape == (k, half * D_MODEL) and m % tm == 0 and k % tk == 0, name
    nk = k // tk

    def body(a_ref, lo_ref, hi_ref, o_ref, acc_ref):
        j, kk = pl.program_id(1), pl.program_id(2)

        def step(b_ref):
            part = lax.dot_general(a_ref[...], b_ref[...], (((0,), (0,)), ((), ())), preferred_element_type=F32)

            @pl.when(kk == 0)
            def _():
                acc_ref[...] = part

            @pl.when(kk > 0)
            def _():
                acc_ref[...] += part

        pl.when(j < half)(lambda: step(lo_ref))
        pl.when(j >= half)(lambda: step(hi_ref))

        @pl.when(kk == nk - 1)
        def _():
            o_ref[...] = acc_ref[...].astype(o_ref.dtype)

    lo_spec = pl.BlockSpec((tk, D_MODEL), lambda i, j, kk: (jnp.where(j < half, kk, nk - 1), jnp.minimum(j, half - 1)))
    hi_spec = pl.BlockSpec((tk, D_MODEL), lambda i, j, kk: (jnp.where(j >= half, kk, 0), jnp.maximum(j - half, 0)))
    return pl.pallas_call(
        body, name=name, grid=(m // tm, N_DEV, nk),
        in_specs=[pl.BlockSpec((tk, tm), lambda i, j, kk: (kk, i)), lo_spec, hi_spec],
        out_specs=pl.BlockSpec((None, tm, D_MODEL), lambda i, j, kk: (j, i, 0)),
        out_shape=jax.ShapeDtypeStruct((N_DEV, m, D_MODEL), BF16),
        scratch_shapes=[pltpu.VMEM((tm, D_MODEL), F32)],
        compiler_params=_params("parallel", "parallel", "arbitrary"),
    )(a, b_lo, b_hi)


def _rowwise(fn, rows, consts, out_rows, out_acc, *, ts, name):
    s = rows[0][0].shape[0]
    assert s % ts == 0, (name, s, ts)
    n_in = len(rows) + len(consts)
    n_row = len(out_rows)

    def body(*refs):
        vals = [r[...] for r in refs[:n_in]]
        row_vals, acc_vals = fn(*vals)
        for r, v in zip(refs[n_in:n_in + n_row], row_vals):
            r[...] = v.astype(r.dtype)
        first = pl.program_id(0) == 0
        for r, v in zip(refs[n_in + n_row:], acc_vals):
            @pl.when(first)
            def _(r=r, v=v):
                r[...] = v

            @pl.when(jnp.logical_not(first))
            def _(r=r, v=v):
                r[...] += v

    in_specs = [pl.BlockSpec((ts, w), functools.partial(lambda i, cb: (i, cb), cb=cb)) for _, w, cb in rows]
    in_specs += [pl.BlockSpec(c.shape, lambda i: (0, 0)) for c in consts]
    out_specs = [pl.BlockSpec((ts, w), lambda i: (i, 0)) for w, _ in out_rows]
    out_specs += [pl.BlockSpec(sh, lambda i: (0, 0)) for sh in out_acc]
    out_shape = [jax.ShapeDtypeStruct((s, w), dt) for w, dt in out_rows]
    out_shape += [jax.ShapeDtypeStruct(sh, F32) for sh in out_acc]
    outs = pl.pallas_call(
        body, name=name, grid=(s // ts,), in_specs=in_specs, out_specs=out_specs, out_shape=out_shape,
        compiler_params=_params("arbitrary" if out_acc else "parallel"),
    )(*[a for a, _, _ in rows], *consts)
    return outs[:n_row], outs[n_row:]


def _full(a):
    return (a, a.shape[1], 0)


def _gate_fn(zu, zv, vg, vb, ws0, ws1, ws2, ws3, bfull, ga):
    u = _gelu_erf(zu)
    vn = _layernorm(_gelu_erf(zv), vg, vb)
    p = lax.broadcasted_iota(jnp.int32, (CHUNK, CHUNK), 0)
    q = lax.broadcasted_iota(jnp.int32, (CHUNK, CHUNK), 1)
    tril = jnp.where(q <= p, 1.0, 0.0)
    group = lax.broadcasted_iota(jnp.int32, (1, A_WIDTH), 1) // CHUNK
    sg = bfull
    for g, w in enumerate((ws0, ws1, ws2, ws3)):
        sg = sg + _bdot(w * tril, jnp.where(group == g, vn, 0.0))
    return _rms(u * sg, ga)


def _bias_reduce(dbf, name):
    def body(x_ref, o_ref):
        lane = lax.broadcasted_iota(jnp.int32, (CHUNK, CHUNK), 1)
        out = jnp.zeros((CHUNK, CHUNK), F32)
        for g in range(A_GROUPS):
            out = jnp.where(lane == g, jnp.sum(x_ref[:, g * CHUNK:(g + 1) * CHUNK], axis=1, keepdims=True), out)
        o_ref[...] = out

    return pl.pallas_call(body, name=name, out_shape=jax.ShapeDtypeStruct((CHUNK, CHUNK), F32))(dbf)


def _pair_mask(hh):
    lane = lax.broadcasted_iota(jnp.int32, (1, 2 * HEAD_DIM), 1)
    return (lane >= HEAD_DIM * hh) & (lane < HEAD_DIM * (hh + 1))


def _lane_pick(x2, lm):
    return jnp.max(jnp.where(lm, x2, -jnp.inf), axis=1, keepdims=True)


def _attn_specs(nb):
    cur = pl.BlockSpec((BAND, B_WIDTH), lambda r, j: (j, r))
    prev = pl.BlockSpec((BAND, B_WIDTH), lambda r, j: (jnp.maximum(j - 1, 0), r))
    nxt = pl.BlockSpec((BAND, B_WIDTH), lambda r, j: (jnp.minimum(j + 1, nb - 1), r))
    return cur, prev, nxt


def _band_valid_q(j):
    row = lax.broadcasted_iota(jnp.int32, (BAND, 2 * BAND), 0)
    col = lax.broadcasted_iota(jnp.int32, (BAND, 2 * BAND), 1)
    return (col >= row) & (col <= row + BAND) & ((col >= BAND) | (j > 0))


def _attn_fwd(q, k, v, d, name):
    nb = q.shape[0] // BAND
    cur, prev, _ = _attn_specs(nb)

    def body(q_ref, kc_ref, kp_ref, vc_ref, vp_ref, o_ref, l_ref):
        valid = _band_valid_q(pl.program_id(1))
        for hp in range(N_HEADS // 2):
            sl = slice(2 * HEAD_DIM * hp, 2 * HEAD_DIM * (hp + 1))
            q2 = q_ref[:, sl]
            k2 = jnp.concatenate([kp_ref[:, sl], kc_ref[:, sl]], axis=0)
            v2 = jnp.concatenate([vp_ref[:, sl], vc_ref[:, sl]], axis=0)
            o2 = jnp.zeros((BAND, 2 * HEAD_DIM), F32)
            l2 = jnp.zeros((BAND, 2 * HEAD_DIM), F32)
            for hh in range(2):
                lm = _pair_mask(hh)
                s = _dotg(jnp.where(lm, q2, jnp.zeros_like(q2)), k2, 1, 1) * (HEAD_DIM ** -0.5)
                s = jnp.where(valid, s, NEG_INF)
                m = jnp.max(s, axis=1, keepdims=True)
                p = jnp.exp(s - m)
                den = jnp.sum(p, axis=1, keepdims=True)
                o = _dotg(p, v2, 1, 0) / den
                o2 = jnp.where(lm, o, o2)
                l2 = jnp.where(lm, m + jnp.log(den), l2)
            o_ref[:, sl] = o2
            l_ref[:, sl] = l2

    return pl.pallas_call(
        body, name=name, grid=(d, nb), in_specs=[cur, cur, prev, cur, prev], out_specs=[cur, cur],
        out_shape=[jax.ShapeDtypeStruct(q.shape, F32), jax.ShapeDtypeStruct(q.shape, F32)],
        compiler_params=_params("parallel", "parallel"),
    )(q, k, k, v, v)


def _attn_bwd_q(q, k, v, do, lse, delta, d, name):
    nb = q.shape[0] // BAND
    cur, prev, _ = _attn_specs(nb)

    def body(q_ref, kc_ref, kp_ref, vc_ref, vp_ref, do_ref, l_ref, dl_ref, dq_ref):
        valid = _band_valid_q(pl.program_id(1))
        for hp in range(N_HEADS // 2):
            sl = slice(2 * HEAD_DIM * hp, 2 * HEAD_DIM * (hp + 1))
            q2, do2, l2, dl2 = q_ref[:, sl], do_ref[:, sl], l_ref[:, sl], dl_ref[:, sl]
            k2 = jnp.concatenate([kp_ref[:, sl], kc_ref[:, sl]], axis=0)
            v2 = jnp.concatenate([vp_ref[:, sl], vc_ref[:, sl]], axis=0)
            dq2 = jnp.zeros((BAND, 2 * HEAD_DIM), F32)
            for hh in range(2):
                lm = _pair_mask(hh)
                s = _dotg(jnp.where(lm, q2, jnp.zeros_like(q2)), k2, 1, 1) * (HEAD_DIM ** -0.5)
                s = jnp.where(valid, s, NEG_INF)
                p = jnp.exp(s - _lane_pick(l2, lm))
                dp = _dotg(jnp.where(lm, do2, jnp.zeros_like(do2)), v2, 1, 1)
                ds = p * (dp - _lane_pick(dl2, lm))
                dq2 = jnp.where(lm, _dotg(ds, k2, 1, 0) * (HEAD_DIM ** -0.5), dq2)
            dq_ref[:, sl] = dq2

    return pl.pallas_call(
        body, name=name, grid=(d, nb), in_specs=[cur, cur, prev, cur, prev, cur, cur, cur], out_specs=cur,
        out_shape=jax.ShapeDtypeStruct(q.shape, F32),
        compiler_params=_params("parallel", "parallel"),
    )(q, k, k, v, v, do, lse, delta)


def _attn_bwd_kv(q, k, v, do, lse_t, delta_t, d, name):
    nb = q.shape[0] // BAND
    cur, _, nxt = _attn_specs(nb)
    t_cur = pl.BlockSpec((1, N_HEADS, BAND), lambda r, j: (r, 0, j))
    t_nxt = pl.BlockSpec((1, N_HEADS, BAND), lambda r, j: (r, 0, jnp.minimum(j + 1, nb - 1)))

    def body(k_ref, v_ref, qc_ref, qn_ref, doc_ref, don_ref, lc_ref, ln_ref, dlc_ref, dln_ref, dk_ref, dv_ref):
        j = pl.program_id(1)
        row = lax.broadcasted_iota(jnp.int32, (BAND, 2 * BAND), 0)
        col = lax.broadcasted_iota(jnp.int32, (BAND, 2 * BAND), 1)
        valid = (col >= row) & (col <= row + BAND) & ((col < BAND) | (j < nb - 1))
        for hp in range(N_HEADS // 2):
            sl = slice(2 * HEAD_DIM * hp, 2 * HEAD_DIM * (hp + 1))
            k2, v2 = k_ref[:, sl], v_ref[:, sl]
            q2 = jnp.concatenate([qc_ref[:, sl], qn_ref[:, sl]], axis=0)
            do2 = jnp.concatenate([doc_ref[:, sl], don_ref[:, sl]], axis=0)
            dk2 = jnp.zeros((BAND, 2 * HEAD_DIM), F32)
            dv2 = jnp.zeros((BAND, 2 * HEAD_DIM), F32)
            for hh in range(2):
                h = 2 * hp + hh
                lm = _pair_mask(hh)
                lse_row = jnp.concatenate([lc_ref[0, h:h + 1, :], ln_ref[0, h:h + 1, :]], axis=1)
                dl_row = jnp.concatenate([dlc_ref[0, h:h + 1, :], dln_ref[0, h:h + 1, :]], axis=1)
                st = _dotg(jnp.where(lm, k2, jnp.zeros_like(k2)), q2, 1, 1) * (HEAD_DIM ** -0.5)
                st = jnp.where(valid, st, NEG_INF)
                pt = jnp.exp(st - lse_row)
                dv2 = jnp.where(lm, _dotg(pt, do2, 1, 0), dv2)
                dpt = _dotg(jnp.where(lm, v2, jnp.zeros_like(v2)), do2, 1, 1)
                dst = pt * (dpt - dl_row)
                dk2 = jnp.where(lm, _dotg(dst, q2, 1, 0) * (HEAD_DIM ** -0.5), dk2)
            dk_ref[:, sl] = dk2
            dv_ref[:, sl] = dv2

    return pl.pallas_call(
        body, name=name, grid=(d, nb),
        in_specs=[cur, cur, cur, nxt, cur, nxt, t_cur, t_nxt, t_cur, t_nxt], out_specs=[cur, cur],
        out_shape=[jax.ShapeDtypeStruct(q.shape, F32), jax.ShapeDtypeStruct(q.shape, F32)],
        compiler_params=_params("parallel", "parallel"),
    )(k, v, q, q, do, do, lse_t, lse_t, delta_t, delta_t)


def _to_residues(a, d):
    return a if d == 1 else a.reshape(a.shape[0] // d, d * a.shape[1])


def _from_residues(a, d):
    return a if d == 1 else a.reshape(a.shape[0] * d, a.shape[1] // d)


def _head_rows(a, d):
    s = a.shape[0]
    return a[:, ::HEAD_DIM].reshape(s // d, d, N_HEADS).transpose(1, 2, 0)


FF_TS = 256
FF_TC = 512
HALO = 8


def _conv3(ext, w, b):
    return b + w[0:1, :] * pltpu.roll(ext, 2, 0) + w[1:2, :] * pltpu.roll(ext, 1, 0) + w[2:3, :] * ext


def _ffn_specs(s, cols_first):
    nrb = s // FF_TS
    per = FF_TS // HALO

    def mk(block, fn):
        if cols_first:
            return pl.BlockSpec(block, lambda j, i: fn(i, j))
        return pl.BlockSpec(block, lambda i, j: fn(i, j))

    specs = types.SimpleNamespace(
        nrb=nrb, ncb=D_FF // FF_TC,
        row=mk((FF_TS, FF_TC), lambda i, j: (i, j)),
        before=mk((HALO, FF_TC), lambda i, j: (jnp.maximum(i * per - 1, 0), j)),
        after=mk((HALO, FF_TC), lambda i, j: (jnp.minimum((i + 1) * per, nrb * per - 1), j)),
        w=mk((3, FF_TC), lambda i, j: (0, j)),
        b=mk((1, FF_TC), lambda i, j: (0, j)),
        acc=mk((HALO, FF_TC), lambda i, j: (0, j)),
    )
    return specs


def _geglu_fwd(ug, uv, wg, wv, bg, bv, name):
    s = ug.shape[0]
    sp = _ffn_specs(s, False)

    def body(ug_ref, uv_ref, hg_ref, hv_ref, wg_ref, wv_ref, bg_ref, bv_ref, y_ref):
        keep = jnp.where(pl.program_id(0) > 0, 1.0, 0.0)
        eg = jnp.concatenate([hg_ref[...] * keep, ug_ref[...]], axis=0)
        ev = jnp.concatenate([hv_ref[...] * keep, uv_ref[...]], axis=0)
        gate = _conv3(eg, wg_ref[...], bg_ref[...])[HALO:, :]
        val = _conv3(ev, wv_ref[...], bv_ref[...])[HALO:, :]
        y_ref[...] = (_gelu_tanh(gate) * val).astype(y_ref.dtype)

    return pl.pallas_call(
        body, name=name, grid=(sp.nrb, sp.ncb),
        in_specs=[sp.row, sp.row, sp.before, sp.before, sp.w, sp.w, sp.b, sp.b],
        out_specs=sp.row, out_shape=jax.ShapeDtypeStruct((s, D_FF), BF16),
        compiler_params=_params("parallel", "parallel"),
    )(ug, uv, ug, uv, wg, wv, bg, bv)


def _geglu_bwd(ug, uv, dy, wg, wv, bg, bv, name):
    s = ug.shape[0]
    sp = _ffn_specs(s, True)
    nrb = sp.nrb
    rows = FF_TS + 2 * HALO

    def body(ug_ref, uv_ref, hg_ref, hv_ref, ng_ref, nv_ref, dy_ref, dyn_ref, wg_ref, wv_ref, bg_ref, bv_ref,
             dug_ref, duv_ref, dwg_ref, dwv_ref):
        i = pl.program_id(1)
        keep_top = jnp.where(i > 0, 1.0, 0.0)
        keep_bot = jnp.where(i < nrb - 1, 1.0, 0.0)
        eg = jnp.concatenate([hg_ref[...] * keep_top, ug_ref[...], ng_ref[...]], axis=0)
        ev = jnp.concatenate([hv_ref[...] * keep_top, uv_ref[...], nv_ref[...]], axis=0)
        dye = jnp.concatenate([jnp.zeros((HALO, FF_TC), F32), dy_ref[...], dyn_ref[...] * keep_bot], axis=0)
        wg_, wv_ = wg_ref[...], wv_ref[...]
        gate = _conv3(eg, wg_, bg_ref[...])
        val = _conv3(ev, wv_, bv_ref[...])
        act, act_vjp = jax.vjp(_gelu_tanh, gate)
        dgate = act_vjp(dye * val)[0]
        dval = dye * act

        def back(dc, w):
            return w[2:3, :] * dc + w[1:2, :] * pltpu.roll(dc, rows - 1, 0) + w[0:1, :] * pltpu.roll(dc, rows - 2, 0)

        dug_ref[...] = back(dgate, wg_)[HALO:HALO + FF_TS, :].astype(dug_ref.dtype)
        duv_ref[...] = back(dval, wv_)[HALO:HALO + FF_TS, :].astype(duv_ref.dtype)

        def wsum(dc, e):
            dcc = dc[HALO:HALO + FF_TS, :]
            parts = [jnp.sum(dcc * pltpu.roll(e, 2, 0)[HALO:HALO + FF_TS, :], axis=0, keepdims=True),
                     jnp.sum(dcc * pltpu.roll(e, 1, 0)[HALO:HALO + FF_TS, :], axis=0, keepdims=True),
                     jnp.sum(dcc * e[HALO:HALO + FF_TS, :], axis=0, keepdims=True),
                     jnp.sum(dcc, axis=0, keepdims=True),
                     jnp.zeros((HALO - 4, FF_TC), F32)]
            return jnp.concatenate(parts, axis=0)

        sg, sv = wsum(dgate, eg), wsum(dval, ev)

        @pl.when(i == 0)
        def _():
            dwg_ref[...] = sg
            dwv_ref[...] = sv

        @pl.when(i > 0)
        def _():
            dwg_ref[...] += sg
            dwv_ref[...] += sv

    return pl.pallas_call(
        body, name=name, grid=(sp.ncb, nrb),
        in_specs=[sp.row, sp.row, sp.before, sp.before, sp.after, sp.after, sp.row, sp.after,
                  sp.w, sp.w, sp.b, sp.b],
        out_specs=[sp.row, sp.row, sp.acc, sp.acc],
        out_shape=[jax.ShapeDtypeStruct((s, D_FF), BF16), jax.ShapeDtypeStruct((s, D_FF), BF16),
                   jax.ShapeDtypeStruct((HALO, D_FF), F32), jax.ShapeDtypeStruct((HALO, D_FF), F32)],
        compiler_params=_params("parallel", "arbitrary"),
    )(ug, uv, ug, uv, ug, uv, dy, dy, wg, wv, bg, bv)


def _rope_tables(s):
    inv = ROPE_THETA ** (-jnp.arange(0, ROT_DIM, 2, dtype=F32) / ROT_DIM)
    ang = jnp.arange(s, dtype=F32)[:, None] * inv[None, :]
    cos8, sin8 = jnp.cos(ang), jnp.sin(ang)
    rest = HEAD_DIM - ROT_DIM
    cos_h = jnp.concatenate([cos8, cos8, jnp.ones((s, rest), F32)], axis=1)
    sin_h = jnp.concatenate([sin8, sin8, jnp.zeros((s, rest), F32)], axis=1)
    return jnp.tile(cos_h, (1, N_HEADS)), jnp.tile(sin_h, (1, N_HEADS))


def _layer_fwd(x, w, cos, sin):
    sv = types.SimpleNamespace(x=x)
    (sv.h1,), _ = _rowwise(lambda xb, g: ((_rms(xb, g),), ()), [_full(x)], [w.g_pre], [(D_MODEL, BF16)], [],
                           ts=512, name="pre_mix_norm")
    sv.proj = _matmul(sv.h1, w.big("w_in", sv.h1), mode="nn", out_dtype=F32, name="proj", tn=1280)

    gate_consts = [w.vg, w.vb, *w.ws, w.bfull, w.ga]
    (na,), _ = _rowwise(lambda *a: ((_gate_fn(*a),), ()), [(sv.proj, A_WIDTH, 0), (sv.proj, A_WIDTH, 1)], gate_consts,
                        [(A_WIDTH, BF16)], [], ts=CHUNK, name="gate_fwd")

    def rope_fn(qr, kr, vr, cs, sn):
        return (qr * cs + _rot_half(qr) * sn, kr * cs + _rot_half(kr) * sn, vr), ()

    (sv.q, sv.k, sv.v), _ = _rowwise(
        rope_fn, [(sv.proj, B_WIDTH, 2), (sv.proj, B_WIDTH, 3), (sv.proj, B_WIDTH, 4), _full(cos), _full(sin)], [],
        [(B_WIDTH, BF16)] * 3, [], ts=512, name="rope_fwd")

    branch = []
    for d in DILATIONS:
        o, l = _attn_fwd(_to_residues(sv.q, d), _to_residues(sv.k, d), _to_residues(sv.v, d), d, name=f"attn_fwd_d{d}")
        branch += [_full(_from_residues(o, d)), _full(_from_residues(l, d))]

    def combine_fn(o1, l1, o2, l2, o3, l3, nab, gb):
        m = jnp.maximum(jnp.maximum(l1, l2), l3)
        e1, e2, e3 = jnp.exp(l1 - m), jnp.exp(l2 - m), jnp.exp(l3 - m)
        den = e1 + e2 + e3
        ob = (e1 / den) * o1 + (e2 / den) * o2 + (e3 / den) * o3
        mixed = jnp.concatenate([nab, _rms(ob, gb).astype(BF16)], axis=1)
        return (mixed, ob, m + jnp.log(den)), ()

    (sv.mixed, sv.ob, sv.lse), _ = _rowwise(combine_fn, branch + [_full(na)], [w.gb],
                                            [(D_MODEL, BF16), (B_WIDTH, F32), (B_WIDTH, F32)], [], ts=512, name="combine")
    sv.y = _matmul(sv.mixed, w.big("w_out", sv.mixed), mode="nn", out_dtype=F32, name="mix_out")

    def mid_fn(xb, yb, g1, g2):
        x1 = xb + _rms(yb, g1)
        return (x1, _rms(x1, g2)), ()

    (sv.x1, sv.h2), _ = _rowwise(mid_fn, [_full(x), _full(sv.y)], [w.g_pm, w.g_pf], [(D_MODEL, F32), (D_MODEL, BF16)], [],
                                 ts=512, name="post_mix_norm")
    w_up = w.big("w_up", sv.h2)
    sv.ug = _matmul(sv.h2, w_up, mode="nn", out_dtype=F32, name="ffn_up_gate", cols=(0, N_DEV // 2))
    sv.uv = _matmul(sv.h2, w_up, mode="nn", out_dtype=F32, name="ffn_up_val", cols=(N_DEV // 2, N_DEV // 2))
    conv_w = w.big("conv_w", sv.h2)
    sv.yff = _geglu_fwd(sv.ug, sv.uv, conv_w[:, :D_FF], conv_w[:, D_FF:], w.cb_g, w.cb_v, name="geglu_fwd")
    sv.f = _matmul(sv.yff, w.big("w_down", sv.yff), mode="nn", out_dtype=F32, name="ffn_down")
    (x2,), _ = _rowwise(lambda xb, fb, g: ((xb + _rms(fb, g),), ()), [_full(sv.x1), _full(sv.f)], [w.g_post],
                        [(D_MODEL, F32)], [], ts=512, name="post_ffn_norm")
    return x2, sv


def _layer_bwd(dx2, sv, w, cos, sin, emit):
    g = {}

    def post_fn(fb, dxb, gp):
        _, vjp = jax.vjp(_rms, fb, gp)
        df, dg = vjp(dxb)
        return (df,), (dg,)

    (df,), (g["post_ffn_norm"],) = _rowwise(post_fn, [_full(sv.f), _full(dx2)], [w.g_post], [(D_MODEL, BF16)],
                                            [(1, D_MODEL)], ts=512, name="post_ffn_norm_bwd")
    dyff = _matmul(df, w.big("w_down", df), mode="nt", out_dtype=F32, name="ffn_down_dx")
    big = {"w_down": _matmul(sv.yff, df, mode="tn", out_dtype=BF16, name="ffn_down_dw").reshape(N_DEV, -1, D_MODEL)}
    conv_w, w_up = w.big("conv_w", df), w.big("w_up", df)
    dug, duv, dwg, dwv = _geglu_bwd(sv.ug, sv.uv, dyff, conv_w[:, :D_FF], conv_w[:, D_FF:], w.cb_g, w.cb_v,
                                    name="geglu_bwd")
    big["conv_w"] = jnp.concatenate([dwg[0:3], dwv[0:3]], axis=1).reshape(3, N_DEV, D_MODEL).transpose(1, 0, 2)
    g["conv_b"] = jnp.concatenate([dwg[3], dwv[3]], axis=0)
    dh2a = _matmul(dug, w_up, mode="nt", out_dtype=F32, name="ffn_up_dx_gate", cols=(0, N_DEV // 2))
    dh2b = _matmul(duv, w_up, mode="nt", out_dtype=F32, name="ffn_up_dx_val", cols=(N_DEV // 2, N_DEV // 2))
    big["w_up"] = _matmul_by_destination(sv.h2, dug, duv, name="ffn_up_dw")
    g_pm = w.g_pm + emit(big)

    def mid_fn(x1b, yb, da, db, dxb, g1, g2):
        _, vjp2 = jax.vjp(_rms, x1b, g2)
        dx1h, dg2 = vjp2(da + db)
        dx1 = dxb + dx1h
        _, vjp1 = jax.vjp(_rms, yb, g1)
        dy, dg1 = vjp1(dx1)
        return (dx1, dy), (dg1, dg2)

    (dx1, dy), (g["post_mix_norm"], g["pre_ffn_norm"]) = _rowwise(
        mid_fn, [_full(sv.x1), _full(sv.y), _full(dh2a), _full(dh2b), _full(dx2)], [g_pm, w.g_pf],
        [(D_MODEL, F32), (D_MODEL, BF16)], [(1, D_MODEL), (1, D_MODEL)], ts=256, name="post_mix_norm_bwd")
    dmixed = _matmul(dy, w.big("w_out", dy), mode="nt", out_dtype=F32, name="mix_out_dx")
    big = {"w_out": _matmul(sv.mixed, dy, mode="tn", out_dtype=BF16, name="mix_out_dw").reshape(N_DEV, -1, D_MODEL)}

    def attn_out_fn(obb, dmb, gb):
        _, vjp = jax.vjp(_rms, obb, gb)
        do, dgb = vjp(dmb)
        return (do, _head_sum(do * obb)), (dgb,)

    (do, delta), (g["out_norm_b"],) = _rowwise(attn_out_fn, [_full(sv.ob), (dmixed, B_WIDTH, 1)], [w.gb],
                                               [(B_WIDTH, BF16), (B_WIDTH, F32)], [(1, B_WIDTH)], ts=512,
                                               name="attn_out_bwd")
    parts = {"q": [], "k": [], "v": []}
    for d in DILATIONS:
        qv, kv, vv, dov = (_to_residues(a, d) for a in (sv.q, sv.k, sv.v, do))
        dq = _attn_bwd_q(qv, kv, vv, dov, _to_residues(sv.lse, d), _to_residues(delta, d), d, name=f"attn_bwd_q_d{d}")
        dk, dv = _attn_bwd_kv(qv, kv, vv, dov, _head_rows(sv.lse, d), _head_rows(delta, d), d, name=f"attn_bwd_kv_d{d}")
        parts["q"].append(_full(_from_residues(dq, d)))
        parts["k"].append(_full(_from_residues(dk, d)))
        parts["v"].append(_full(_from_residues(dv, d)))

    def rope_bwd_fn(q1, q2, q3, k1, k2, k3, v1, v2, v3, cs, sn):
        def back(t):
            return t * cs - _rot_half(t * sn)
        return (jnp.concatenate([back(q1 + q2 + q3), back(k1 + k2 + k3), v1 + v2 + v3], axis=1),), ()

    (dzb,), _ = _rowwise(rope_bwd_fn, parts["q"] + parts["k"] + parts["v"] + [_full(cos), _full(sin)], [],
                         [(3 * B_WIDTH, BF16)], [], ts=256, name="rope_bwd")

    gate_consts = [w.vg, w.vb, *w.ws, w.bfull, w.ga]

    def gate_bwd_fn(zu, zv, dna, *consts):
        _, vjp = jax.vjp(_gate_fn, zu, zv, *consts)
        grads = vjp(dna)
        return (jnp.concatenate([grads[0], grads[1]], axis=1),), tuple(grads[2:])

    (dza,), gsmall = _rowwise(
        gate_bwd_fn, [(sv.proj, A_WIDTH, 0), (sv.proj, A_WIDTH, 1), (dmixed, A_WIDTH, 0)], gate_consts,
        [(2 * A_WIDTH, BF16)], [c.shape for c in gate_consts], ts=CHUNK, name="gate_bwd")
    g["v_norm_g"], g["v_norm_b"] = gsmall[0], gsmall[1]
    g["w_spatial"] = jnp.stack(gsmall[2:6])
    g["b_spatial"] = _bias_reduce(gsmall[6], name="bias_reduce")[:, :A_GROUPS].T
    g["out_norm_a"] = gsmall[7]

    dproj = jnp.concatenate([dza, dzb], axis=1)
    dh1 = _matmul(dproj, w.big("w_in", dproj), mode="nt", out_dtype=F32, name="proj_dx", tk=1280)
    dw_in = _matmul(sv.h1, dproj, mode="tn", out_dtype=BF16, name="proj_dw", tn=1280)
    big["w_in"] = dw_in.reshape(D_MODEL, N_DEV, IN_COLS // N_DEV).transpose(1, 0, 2)
    g_pre = w.g_pre + emit(big)

    def pre_fn(xb, dhb, dxb, gp):
        _, vjp = jax.vjp(_rms, xb, gp)
        dxh, dg = vjp(dhb)
        return (dxb + dxh,), (dg,)

    (dx,), (g["pre_mix_norm"],) = _rowwise(pre_fn, [_full(sv.x), _full(dh1), _full(dx1)], [g_pre], [(D_MODEL, F32)],
                                           [(1, D_MODEL)], ts=512, name="pre_mix_norm_bwd")
    return dx, g


def _layer_weights(l, full, small):
    row = lambda a: a[l].reshape(1, -1)
    return types.SimpleNamespace(
        big=functools.partial(full, l),
        g_pre=row(small["pre_mix_norm"]), vg=row(small["v_norm_g"]), vb=row(small["v_norm_b"]),
        ws=[small["w_spatial"][l, gi] for gi in range(A_GROUPS)],
        bfull=jnp.repeat(small["b_spatial"][l].T, CHUNK, axis=1),
        ga=row(small["out_norm_a"]), gb=row(small["out_norm_b"]),
        g_pm=row(small["post_mix_norm"]), g_pf=row(small["pre_ffn_norm"]),
        cb_g=small["conv_b"][l][:D_FF].reshape(1, -1), cb_v=small["conv_b"][l][D_FF:].reshape(1, -1),
        g_post=row(small["post_ffn_norm"]))


def _local_step(x, target, full, small, emit, started):
    s = x.shape[0]
    cos, sin = _rope_tables(s)
    ws = [_layer_weights(l, full, small) for l in range(N_LAYERS)]
    ws[0].g_pre = ws[0].g_pre + started
    saved = []
    h = x
    for l in range(N_LAYERS):
        h, sv = _layer_fwd(h, ws[l], cos, sin)
        saved.append(sv)

    def loss_fn(yb, tb):
        diff = yb - tb
        return (diff * (1.0 / D_MODEL),), (jnp.sum(diff * diff, axis=0, keepdims=True),)

    (dh,), (sq,) = _rowwise(loss_fn, [_full(h), _full(target)], [], [(D_MODEL, F32)], [(1, D_MODEL)], ts=512, name="loss")
    loss = 0.5 * jnp.sum(sq) * (1.0 / D_MODEL)
    grads = [None] * N_LAYERS
    for l in reversed(range(N_LAYERS)):
        dh, grads[l] = _layer_bwd(dh, saved[l], ws[l], cos, sin, functools.partial(emit, l))
    return loss, dh, grads


def _place():
    return lax.axis_index("x"), lax.axis_index("y"), lax.axis_index("c")


def _all_gather(x, name):
    def body(x_ref, out_ref, send_sems, recv_sems, local_sem):
        mx, my, mc = _place()
        me, sibling = (mx, my, mc), (mx, my, 1 - mc)
        chips = [(1 - mx, my), (mx, 1 - my), (1 - mx, 1 - my)]

        def slot(px, py, pc):
            return out_ref.at[4 * px + 2 * py + pc]

        def copy(k, block, to, src=None):
            return pltpu.make_async_remote_copy(
                src_ref=slot(*block) if src is None else src, dst_ref=slot(*block),
                send_sem=send_sems.at[k], recv_sem=recv_sems.at[k], device_id=to, device_id_type=MESH_ID)

        mine = pltpu.make_async_copy(x_ref, slot(*me), local_sem)
        mine.start()
        first = [copy(0, me, sibling, src=x_ref)]
        first += [copy(1 + j, me, (*chip, mc), src=x_ref) for j, chip in enumerate(chips)]
        for cp in first:
            cp.start()
        passed = [copy(4 + j, (*chip, mc), sibling) for j, chip in enumerate(chips)]
        for j, chip in enumerate(chips):
            copy(1 + j, (*chip, mc), me).wait_recv()
            passed[j].start()
        copy(0, sibling, me).wait_recv()
        for j, chip in enumerate(chips):
            copy(4 + j, (*chip, 1 - mc), me).wait_recv()
        for cp in first + passed:
            cp.wait_send()
        mine.wait()

    return pl.pallas_call(
        body, name=name, out_shape=jax.ShapeDtypeStruct((N_DEV,) + x.shape, x.dtype), in_specs=[ANY], out_specs=ANY,
        scratch_shapes=[pltpu.SemaphoreType.DMA((7,)), pltpu.SemaphoreType.DMA((7,)), pltpu.SemaphoreType.DMA],
    )(x)


FLIPS = ((1, 0, 0), (0, 1, 0), (1, 1, 0), (0, 0, 1), (1, 0, 1), (0, 1, 1), (1, 1, 1))
HBM_SPEC = pl.BlockSpec(memory_space=pltpu.HBM)
SEM_SPEC = pl.BlockSpec(memory_space=pltpu.SEMAPHORE)
SPLIT_COPY = pltpu.CompilerParams(has_side_effects=pltpu.SideEffectType.DATAFLOW_SIDE_EFFECTING)


def _peers():
    mx, my, mc = _place()
    out = []
    for fx, fy, fc in FLIPS:
        px, py, pc = (1 - mx if fx else mx), (1 - my if fy else my), (1 - mc if fc else mc)
        out.append(((px, py, pc), 4 * px + 2 * py + pc))
    return out


def _flat_copies(scatter, src_refs, land_refs, send_sems, recv_sems):
    mx, my, mc = _place()
    me = 4 * mx + 2 * my + mc
    n = len(src_refs)
    copies = []
    for t in range(n):
        for i, (peer, number) in enumerate(_peers()):
            copies.append(pltpu.make_async_remote_copy(
                src_ref=src_refs[t].at[number] if scatter else src_refs[t],
                dst_ref=land_refs[t].at[i] if scatter else land_refs[t].at[me],
                send_sem=send_sems.at[t * len(FLIPS) + i], recv_sem=recv_sems.at[t * len(FLIPS) + i],
                device_id=peer, device_id_type=MESH_ID))
    return copies


def _flat_start(arrays, scatter, name):
    n = len(arrays)
    slots = len(FLIPS) if scatter else N_DEV
    lands = [lax.empty((slots,) + (a.shape[1:] if scatter else a.shape), a.dtype) for a in arrays]

    def body(*refs):
        src, land, (send_sems, recv_sems), token = refs[:n], refs[n:2 * n], refs[2 * n:2 * n + 2], refs[-1]
        for cp in _flat_copies(scatter, src, land, send_sems, recv_sems):
            cp.start()
        token[...] = jnp.zeros_like(token)

    hbm = [pltpu.HBM(a.shape, a.dtype) for a in arrays] + [pltpu.HBM(a.shape, a.dtype) for a in lands]
    sems = pltpu.SemaphoreType.DMA((n * len(FLIPS),))
    outs = pl.pallas_call(
        body, name=name, out_shape=(sems, sems, *hbm, jax.ShapeDtypeStruct((8, 128), F32)),
        in_specs=[HBM_SPEC] * (2 * n),
        out_specs=(SEM_SPEC, SEM_SPEC, *([HBM_SPEC] * (2 * n)), pl.BlockSpec(memory_space=pltpu.VMEM)),
        input_output_aliases={t: 2 + t for t in range(2 * n)}, compiler_params=SPLIT_COPY,
    )(*[pltpu.with_memory_space_constraint(a, pltpu.HBM) for a in (*arrays, *lands)])
    return types.SimpleNamespace(sems=outs[:2], thru=outs[2:2 + 2 * n], scatter=scatter, n=n), outs[-1][0:1, 0:1]


def _flat_wait(handle, after, name):
    n = handle.n

    def body(*refs):
        src, land, (send_sems, recv_sems) = refs[:n], refs[n:2 * n], refs[2 * n:2 * n + 2]
        for cp in _flat_copies(handle.scatter, src, land, send_sems, recv_sems):
            cp.wait_send()
            cp.wait_recv()

    outs = pl.pallas_call(
        body, name=name, out_shape=tuple(pltpu.HBM(a.shape, a.dtype) for a in handle.thru),
        in_specs=[HBM_SPEC] * (2 * n) + [SEM_SPEC, SEM_SPEC, ANY], out_specs=tuple([HBM_SPEC] * (2 * n)),
        input_output_aliases={t: t for t in range(2 * n)}, compiler_params=SPLIT_COPY,
    )(*handle.thru, *handle.sems, after)
    return outs[:n], outs[n:]


def _adamw(w, g, m, v):
    m2 = ADAM_B1 * m + (1.0 - ADAM_B1) * g
    v2 = ADAM_B2 * v + (1.0 - ADAM_B2) * (g * g)
    m_hat = m2 / (1.0 - ADAM_B1 ** ADAM_STEP)
    v_hat = v2 / (1.0 - ADAM_B2 ** ADAM_STEP)
    return -ADAM_LR * (m_hat / (jnp.sqrt(v_hat) + ADAM_EPS) + ADAM_WD * w), m2, v2


def _adamw_sharded(me, mine, landed, w, m, v, tr, name):
    _, r, c = w.shape
    nt = r // tr
    assert r % tr == 0 and len(mine) == len(landed) == N_LAYERS == 2, name
    per_layer = 1 + len(FLIPS)

    def body(me_ref, *refs):
        terms, (w_ref, m_ref, v_ref), outs = refs[:2 * per_layer], refs[2 * per_layer:2 * per_layer + 3], refs[-4:]
        layer = pl.program_id(0)

        def total(group):
            g = group[0][0].astype(F32)
            for t in group[1:]:
                g = g + t[0].astype(F32)
            return g

        g = jnp.where(layer == 0, total(terms[:per_layer]), total(terms[per_layer:]))
        d, m2, v2 = _adamw(w_ref[0], g, m_ref[0], v_ref[0])
        for o, val in zip(outs, (g, d, m2, v2)):
            o[0] = val

    def held(l):
        return lambda layer, i: jnp.where(layer == l, i, nt - 1 if l == 0 else 0)

    in_specs = []
    for l in range(N_LAYERS):
        rows = held(l)
        in_specs.append(pl.BlockSpec((1, tr, c), functools.partial(lambda layer, i, me_ref, rows: (me_ref[0], rows(layer, i), 0), rows=rows)))
        for k in range(len(FLIPS)):
            in_specs.append(pl.BlockSpec(
                (1, tr, c), functools.partial(lambda layer, i, me_ref, rows, k: (k, rows(layer, i), 0), rows=rows, k=k)))
    tile = pl.BlockSpec((1, tr, c), lambda layer, i, me_ref: (layer, i, 0))
    operands = []
    for l in range(N_LAYERS):
        operands += [mine[l]] + [landed[l]] * len(FLIPS)
    return pl.pallas_call(
        body, name=name, out_shape=[jax.ShapeDtypeStruct(w.shape, F32)] * 4,
        grid_spec=pltpu.PrefetchScalarGridSpec(
            num_scalar_prefetch=1, grid=(N_LAYERS, nt), in_specs=in_specs + [tile] * 3, out_specs=[tile] * 4),
        compiler_params=_params("arbitrary", "arbitrary"),
    )(me, *operands, w, m, v)


def _adamw_replicated(parts, w, m, v, name):
    def body(p_ref, w_ref, m_ref, v_ref, g_ref, d_ref, m2_ref, v2_ref):
        g = p_ref[0]
        for j in range(1, N_DEV):
            g = g + p_ref[j]
        d, m2, v2 = _adamw(w_ref[...], g, m_ref[...], v_ref[...])
        g_ref[...], d_ref[...], m2_ref[...], v2_ref[...] = g, d, m2, v2

    return pl.pallas_call(body, name=name, out_shape=[jax.ShapeDtypeStruct(w.shape, F32)] * 4,
                          compiler_params=pltpu.CompilerParams(vmem_limit_bytes=VMEM_LIMIT_BYTES))(parts, w, m, v)


def _pack_small(vals):
    flat = jnp.concatenate([vals[n].reshape(-1) for n in SMALL_NAMES])
    return jnp.concatenate([flat, jnp.zeros((SMALL_ROWS * D_MODEL - flat.shape[0],), F32)]).reshape(SMALL_ROWS, D_MODEL)


def _unpack_small(packed, shapes):
    flat, out, at = packed.reshape(-1), {}, 0
    for n in SMALL_NAMES:
        size = math.prod(shapes[n])
        out[n] = flat[at:at + size].reshape(shapes[n])
        at += size
    return out


GATHER_GROUPS = ((0, ("w_in",)), (0, ("w_out", "w_up", "conv_w", "w_down")),
                 (1, ("w_in", "w_out", "w_up", "conv_w", "w_down")))
ADAMW_TILE_ROWS = {"w_in": 512, "w_out": 128, "w_up": 256, "w_down": 256, "conv_w": 3}


def _assemble(name, land):
    if name == "w_in":
        return land.transpose(1, 0, 2).reshape(D_MODEL, IN_COLS)
    if name == "conv_w":
        return land.transpose(1, 0, 2).reshape(3, 2 * D_FF)
    if name == "w_up":
        return land
    return land.reshape(-1, D_MODEL)


def _start_gathers(wts, me):
    started, groups = jnp.zeros((1, 1), F32), []
    for gi, (l, names) in enumerate(GATHER_GROUPS):
        blocks = [wts[n][l] if n == "conv_w" else wts[n][l].astype(BF16) for n in names]
        handle, token = _flat_start(blocks, False, name=f"gather_start_{gi}")
        groups.append(types.SimpleNamespace(layer=l, names=names, blocks=blocks, handle=handle, got=None, index=gi))
        started = started + token

    def fetch(l, name, after):
        grp = next(gr for gr in groups if gr.layer == l and name in gr.names)
        if grp.got is None:
            lands = _flat_wait(grp.handle, after, name=f"gather_wait_{grp.index}")[1]
            grp.got = {}
            for n, blk, land in zip(grp.names, grp.blocks, lands):
                own = (me,) + (0,) * blk.ndim
                grp.got[n] = _assemble(n, lax.dynamic_update_slice(land, blk[None], own))
        return grp.got[name]

    return fetch, started


def kernel(x, pre_mix_norm, w_in, v_norm_g, v_norm_b, w_spatial, b_spatial, out_norm_a, out_norm_b, w_out, post_mix_norm, pre_ffn_norm, w_up, conv_w, conv_b, w_down, post_ffn_norm, loss_target, m_pre_mix_norm, m_w_in, m_v_norm_g, m_v_norm_b, m_w_spatial, m_b_spatial, m_out_norm_a, m_out_norm_b, m_w_out, m_post_mix_norm, m_pre_ffn_norm, m_w_up, m_conv_w, m_conv_b, m_w_down, m_post_ffn_norm, v_pre_mix_norm, v_w_in, v_v_norm_g, v_v_norm_b, v_w_spatial, v_b_spatial, v_out_norm_a, v_out_norm_b, v_w_out, v_post_mix_norm, v_pre_ffn_norm, v_w_up, v_conv_w, v_conv_b, v_w_down, v_post_ffn_norm):
    wts = dict(zip(WEIGHT_NAMES, (pre_mix_norm, w_in, v_norm_g, v_norm_b, w_spatial, b_spatial, out_norm_a, out_norm_b,
                                  w_out, post_mix_norm, pre_ffn_norm, w_up, conv_w, conv_b, w_down, post_ffn_norm)))
    mom1 = dict(zip(WEIGHT_NAMES, (m_pre_mix_norm, m_w_in, m_v_norm_g, m_v_norm_b, m_w_spatial, m_b_spatial, m_out_norm_a,
                                   m_out_norm_b, m_w_out, m_post_mix_norm, m_pre_ffn_norm, m_w_up, m_conv_w, m_conv_b,
                                   m_w_down, m_post_ffn_norm)))
    mom2 = dict(zip(WEIGHT_NAMES, (v_pre_mix_norm, v_w_in, v_v_norm_g, v_v_norm_b, v_w_spatial, v_b_spatial, v_out_norm_a,
                                   v_out_norm_b, v_w_out, v_post_mix_norm, v_pre_ffn_norm, v_w_up, v_conv_w, v_conv_b,
                                   v_w_down, v_post_ffn_norm)))
    mx, my, mc = _place()
    me = 4 * mx + 2 * my + mc

    fetch, started = _start_gathers(wts, me)
    scatters = []

    def emit(l, blocks):
        names = tuple(blocks)
        handle, token = _flat_start([blocks[n] for n in names], True, name=f"scatter_start_{l}_{len(scatters) % 2}")
        scatters.append((l, names, handle))
        return token

    loss_local, dx, grads = _local_step(x[0], loss_target[0], fetch, wts, emit, started)
    loss = lax.psum(loss_local, AXES)

    mine, landed = {}, {}
    for l, names, handle in scatters:
        sent, lands = _flat_wait(handle, dx, name=f"scatter_wait_{l}_{'_'.join(names)}")
        for n, a, b in zip(names, sent, lands):
            mine[l, n], landed[l, n] = a, b
    me_arr = jnp.reshape(me, (1,)).astype(jnp.int32)
    big_out = [{}, {}, {}, {}]
    for n in BIG_NAMES:
        res = _adamw_sharded(me_arr, [mine[l, n] for l in range(N_LAYERS)], [landed[l, n] for l in range(N_LAYERS)],
                             wts[n], mom1[n], mom2[n], ADAMW_TILE_ROWS[n], name=f"adamw_{n}")
        for kind in range(4):
            big_out[kind][n] = res[kind]

    small_grads = {n: jnp.stack([g[n].reshape(wts[n].shape[1:]) for g in grads]) for n in SMALL_NAMES}
    everyone = _all_gather(_pack_small(small_grads), name="gather_small_grads")
    small = [_pack_small({n: t[n] for n in SMALL_NAMES}) for t in (wts, mom1, mom2)]
    small_out = _adamw_replicated(everyone, *small, name="adamw_replicated")
    small_shapes = {n: wts[n].shape for n in SMALL_NAMES}
    small_out = [_unpack_small(o, small_shapes) for o in small_out]

    outs = [loss, dx[None]]
    for kind in range(4):
        outs += [big_out[kind][n] if n in BIG_NAMES else small_out[kind][n] for n in WEIGHT_NAMES]
    return tuple(outs)
```

```python
import functools
import math
import types

import jax
import jax.numpy as jnp
from jax import lax
from jax.experimental import pallas as pl
from jax.experimental.pallas import tpu as pltpu

F32 = jnp.float32
BF16 = jnp.bfloat16

D_MODEL = 1024
A_WIDTH = 512
A_GROUPS = 4
CHUNK = 128
B_WIDTH = 512
HEAD_DIM = 64
N_HEADS = B_WIDTH // HEAD_DIM
ROT_DIM = 16
ROPE_THETA = 500000.0
BAND = 128
DILATIONS = (1, 4, 16)
IN_COLS = 2560
D_FF = 4096
EPS = 1e-6
NEG_INF = -1e30
N_DEV = 8
N_LAYERS = 2

ADAM_LR = 0.001
ADAM_B1 = 0.9
ADAM_B2 = 0.999
ADAM_EPS = 1e-08
ADAM_WD = 0.01
ADAM_STEP = 10

VMEM_LIMIT_BYTES = 56 * 1024 * 1024
MESH_ID = pl.DeviceIdType.MESH
ANY = pl.BlockSpec(memory_space=pl.ANY)
AXES = ("x", "y", "c")

WEIGHT_NAMES = ("pre_mix_norm", "w_in", "v_norm_g", "v_norm_b", "w_spatial", "b_spatial", "out_norm_a", "out_norm_b",
                "w_out", "post_mix_norm", "pre_ffn_norm", "w_up", "conv_w", "conv_b", "w_down", "post_ffn_norm")
BIG_NAMES = ("w_in", "w_out", "w_up", "w_down", "conv_w")
SMALL_NAMES = tuple(n for n in WEIGHT_NAMES if n not in BIG_NAMES)

PACK_ROWS = {"w_in": 640, "w_out": 256, "w_up": 2048, "w_down": 1024, "conv_w": 6}
CONV_W_PAD = 2
PACKED_F32_ROWS = 4096
PACKED_BF16_ROWS = 3984
SMALL_ROWS = 160


def _params(*sem):
    return pltpu.CompilerParams(dimension_semantics=sem, vmem_limit_bytes=VMEM_LIMIT_BYTES)


def _dotg(a, b, ca, cb):
    return lax.dot_general(a.astype(BF16), b.astype(BF16), (((ca,), (cb,)), ((), ())), preferred_element_type=F32)


@jax.custom_vjp
def _bdot(a, b):
    return _dotg(a, b, 1, 0)


def _bdot_fwd(a, b):
    return _dotg(a, b, 1, 0), (a, b)


def _bdot_bwd(res, g):
    a, b = res
    return _dotg(g, b, 1, 1), _dotg(a, g, 0, 0)


_bdot.defvjp(_bdot_fwd, _bdot_bwd)


def _rms(x, g):
    return x * lax.rsqrt(jnp.mean(x * x, axis=-1, keepdims=True) + EPS) * g


def _layernorm(x, g, b):
    mu = jnp.mean(x, axis=-1, keepdims=True)
    xc = x - mu
    return xc * lax.rsqrt(jnp.mean(xc * xc, axis=-1, keepdims=True) + EPS) * g + b


def _gelu_erf(x):
    return x * (lax.erf(x * (1.0 / math.sqrt(2.0))) + 1.0) * 0.5


def _gelu_tanh(x):
    c = math.sqrt(2.0 / math.pi)
    return 0.5 * x * (1.0 + jnp.tanh(c * (x + 0.044715 * (x * x * x))))


def _rot_half(x):
    width = x.shape[1]
    lane = lax.broadcasted_iota(jnp.int32, x.shape, 1) % HEAD_DIM
    back = pltpu.roll(x, ROT_DIM // 2, 1)
    fwd = pltpu.roll(x, width - ROT_DIM // 2, 1)
    return jnp.where(lane < ROT_DIM // 2, -fwd, jnp.where(lane < ROT_DIM, back, 0.0))


def _split3(z):
    h0 = z.astype(BF16)
    r1 = z - h0.astype(F32)
    h1 = r1.astype(BF16)
    h2 = (r1 - h1.astype(F32)).astype(BF16)
    return h0, h1, h2


def _head_sum(z):
    width = z.shape[1]
    a = lax.broadcasted_iota(jnp.int32, (width, width), 0) // HEAD_DIM
    b = lax.broadcasted_iota(jnp.int32, (width, width), 1) // HEAD_DIM
    ones = jnp.where(a == b, 1.0, 0.0).astype(BF16)
    out = None
    for part in _split3(z):
        t = lax.dot_general(part, ones, (((1,), (0,)), ((), ())), preferred_element_type=F32)
        out = t if out is None else out + t
    return out


MATMUL_VMEM_BUDGET = 40 * 1024 * 1024


def _matmul_tiles(m, n, k, out_bytes):
    tn = n if n <= 1024 else (1280 if n % 1280 == 0 and n % 1024 else 1024)
    tk = k if k <= 1024 else (1280 if k % 1280 == 0 and k % 1024 else 1024)
    tm = m
    while tm > 256:
        blocks = 2 * 2 * (tm * tk + tk * tn) + 2 * out_bytes * tm * tn + (4 * tm * tn if k > tk else 0)
        if blocks <= MATMUL_VMEM_BUDGET and m % tm == 0:
            break
        tm //= 2
    return tm, tn, tk


def _matmul(a, b, *, mode, out_dtype, name, cols=None):
    wide = D_MODEL if cols is not None else None
    if mode == "nn":
        (m, k), (_, n) = a.shape, (b.shape if cols is None else (b.shape[1], cols[1] * wide))
    elif mode == "nt":
        (m, k), (n, _) = a.shape, (b.shape if cols is None else (b.shape[1], cols[1] * wide))
    else:
        (k, m), (_, n) = a.shape, b.shape
    tm, tn, tk = _matmul_tiles(m, n, k, jnp.dtype(out_dtype).itemsize)
    assert m % tm == 0 and n % tn == 0 and k % tk == 0, (name, m, n, k)
    nk = k // tk
    if mode == "nn":
        a_spec = pl.BlockSpec((tm, tk), lambda i, j, kk: (i, kk))
        b_spec = pl.BlockSpec((tk, tn), lambda i, j, kk: (kk, j))
        if cols is not None:
            assert tn == wide
            b_spec = pl.BlockSpec((None, tk, tn), lambda i, j, kk: (cols[0] + j, kk, 0))
        ca, cb = 1, 0
    elif mode == "nt":
        a_spec = pl.BlockSpec((tm, tk), lambda i, j, kk: (i, kk))
        b_spec = pl.BlockSpec((tn, tk), lambda i, j, kk: (j, kk))
        if cols is not None:
            assert tk == wide
            b_spec = pl.BlockSpec((None, tn, tk), lambda i, j, kk: (cols[0] + kk, j, 0))
        ca, cb = 1, 1
    else:
        a_spec = pl.BlockSpec((tk, tm), lambda i, j, kk: (kk, i))
        b_spec = pl.BlockSpec((tk, tn), lambda i, j, kk: (kk, j))
        ca, cb = 0, 0

    def body(a_ref, b_ref, o_ref, *acc):
        kk = pl.program_id(2)
        part = lax.dot_general(a_ref[...], b_ref[...], (((ca,), (cb,)), ((), ())), preferred_element_type=F32)
        if nk == 1:
            o_ref[...] = part.astype(o_ref.dtype)
            return
        acc_ref, = acc

        @pl.when(kk == 0)
        def _():
            acc_ref[...] = part

        @pl.when(kk > 0)
        def _():
            acc_ref[...] += part

        @pl.when(kk == nk - 1)
        def _():
            o_ref[...] = acc_ref[...].astype(o_ref.dtype)

    return pl.pallas_call(
        body, name=name, grid=(m // tm, n // tn, nk),
        in_specs=[a_spec, b_spec], out_specs=pl.BlockSpec((tm, tn), lambda i, j, kk: (i, j)),
        out_shape=jax.ShapeDtypeStruct((m, n), out_dtype),
        scratch_shapes=[pltpu.VMEM((tm, tn), F32)] if nk > 1 else [],
        compiler_params=_params("parallel", "parallel", "arbitrary"),
    )(a, b)


def _matmul_by_destination(a, b_lo, b_hi, *, name, tm=1024, tk=2048):
    (k, m), half = a.shape, N_DEV // 2
    assert b_lo.shape == b_hi.shape == (k, half * D_MODEL) and m % tm == 0 and k % tk == 0, name
    nk = k // tk

    def body(a_ref, lo_ref, hi_ref, o_ref, acc_ref):
        j, kk = pl.program_id(1), pl.program_id(2)

        def step(b_ref):
            part = lax.dot_general(a_ref[...], b_ref[...], (((0,), (0,)), ((), ())), preferred_element_type=F32)

            @pl.when(kk == 0)
            def _():
                acc_ref[...] = part

            @pl.when(kk > 0)
            def _():
                acc_ref[...] += part

        pl.when(j < half)(lambda: step(lo_ref))
        pl.when(j >= half)(lambda: step(hi_ref))

        @pl.when(kk == nk - 1)
        def _():
            o_ref[...] = acc_ref[...].astype(o_ref.dtype)

    lo_spec = pl.BlockSpec((tk, D_MODEL), lambda i, j, kk: (jnp.where(j < half, kk, nk - 1), jnp.minimum(j, half - 1)))
    hi_spec = pl.BlockSpec((tk, D_MODEL), lambda i, j, kk: (jnp.where(j >= half, kk, 0), jnp.maximum(j - half, 0)))
    return pl.pallas_call(
        body, name=name, grid=(m // tm, N_DEV, nk),
        in_specs=[pl.BlockSpec((tk, tm), lambda i, j, kk: (kk, i)), lo_spec, hi_spec],
        out_specs=pl.BlockSpec((None, tm, D_MODEL), lambda i, j, kk: (j, i, 0)),
        out_shape=jax.ShapeDtypeStruct((N_DEV, m, D_MODEL), BF16),
        scratch_shapes=[pltpu.VMEM((tm, D_MODEL), F32)],
        compiler_params=_params("parallel", "parallel", "arbitrary"),
    )(a, b_lo, b_hi)


LANES = 128


def _residues_to_rows(ref, scr, d):
    w = ref.shape[1] // d
    n = ref.shape[0]
    for r in range(d):
        for c in range(w // LANES):
            scr[c, pl.ds(r, n, stride=d), :] = ref[:, r * w + c * LANES:r * w + (c + 1) * LANES].astype(F32)
    return jnp.concatenate([scr[c] for c in range(w // LANES)], axis=1)


def _rows_to_residues(val, ref, scr, d):
    w = val.shape[1]
    n = ref.shape[0]
    for c in range(w // LANES):
        scr[c] = val[:, c * LANES:(c + 1) * LANES].astype(F32)
    for r in range(d):
        for c in range(w // LANES):
            ref[:, r * w + c * LANES:r * w + (c + 1) * LANES] = scr[c, pl.ds(r, n, stride=d), :].astype(ref.dtype)


def _rowwise(fn, rows, consts, out_rows, out_acc, *, ts, name):
    rows = [tuple(r) + (1,) * (4 - len(r)) for r in rows]
    out_rows = [tuple(o) + (1,) * (3 - len(o)) for o in out_rows]
    s = rows[0][0].shape[0] * rows[0][3]
    assert s % ts == 0, (name, s, ts)
    n_rows, n_in = len(rows), len(rows) + len(consts)
    n_row = len(out_rows)
    n_out = n_row + len(out_acc)
    moved = [(idx, w) for idx, (_, w, _, d) in enumerate(rows) if d > 1]
    moved += [(n_rows + idx, w) for idx, (w, _, d) in enumerate(out_rows) if d > 1]

    def body(*refs):
        scratch = dict(zip([key for key, _ in moved], refs[n_in + n_out:]))
        vals = []
        for idx, r in enumerate(refs[:n_in]):
            d = rows[idx][3] if idx < n_rows else 1
            vals.append(r[...] if d == 1 else _residues_to_rows(r, scratch[idx], d))
        row_vals, acc_vals = fn(*vals)
        for idx, (r, v) in enumerate(zip(refs[n_in:n_in + n_row], row_vals)):
            d = out_rows[idx][2]
            if d == 1:
                r[...] = v.astype(r.dtype)
            else:
                _rows_to_residues(v, r, scratch[n_rows + idx], d)
        first = pl.program_id(0) == 0
        for r, v in zip(refs[n_in + n_row:n_in + n_out], acc_vals):
            @pl.when(first)
            def _(r=r, v=v):
                r[...] = v

            @pl.when(jnp.logical_not(first))
            def _(r=r, v=v):
                r[...] += v

    in_specs = [pl.BlockSpec((ts // d, d * w), functools.partial(lambda i, cb: (i, cb), cb=cb)) for _, w, cb, d in rows]
    in_specs += [pl.BlockSpec(c.shape, lambda i: (0, 0)) for c in consts]
    out_specs = [pl.BlockSpec((ts // d, d * w), lambda i: (i, 0)) for w, _, d in out_rows]
    out_specs += [pl.BlockSpec(sh, lambda i: (0, 0)) for sh in out_acc]
    out_shape = [jax.ShapeDtypeStruct((s // d, d * w), dt) for w, dt, d in out_rows]
    out_shape += [jax.ShapeDtypeStruct(sh, F32) for sh in out_acc]
    outs = pl.pallas_call(
        body, name=name, grid=(s // ts,), in_specs=in_specs, out_specs=out_specs, out_shape=out_shape,
        scratch_shapes=[pltpu.VMEM((w // LANES, ts, LANES), F32) for _, w in moved],
        compiler_params=_params("arbitrary" if out_acc else "parallel"),
    )(*[a for a, _, _, _ in rows], *consts)
    return outs[:n_row], outs[n_row:]


def _full(a, d=1):
    return (a, a.shape[1] // d, 0, d)


def _gate_fn(zu, zv, vg, vb, ws0, ws1, ws2, ws3, bfull, ga):
    u = _gelu_erf(zu)
    vn = _layernorm(_gelu_erf(zv), vg, vb)
    p = lax.broadcasted_iota(jnp.int32, (CHUNK, CHUNK), 0)
    q = lax.broadcasted_iota(jnp.int32, (CHUNK, CHUNK), 1)
    tril = jnp.where(q <= p, 1.0, 0.0)
    group = lax.broadcasted_iota(jnp.int32, (1, A_WIDTH), 1) // CHUNK
    sg = bfull
    for g, w in enumerate((ws0, ws1, ws2, ws3)):
        sg = sg + _bdot(w * tril, jnp.where(group == g, vn, 0.0))
    return _rms(u * sg, ga)


def _bias_reduce(dbf, name):
    def body(x_ref, o_ref):
        lane = lax.broadcasted_iota(jnp.int32, (CHUNK, CHUNK), 1)
        out = jnp.zeros((CHUNK, CHUNK), F32)
        for g in range(A_GROUPS):
            out = jnp.where(lane == g, jnp.sum(x_ref[:, g * CHUNK:(g + 1) * CHUNK], axis=1, keepdims=True), out)
        o_ref[...] = out

    return pl.pallas_call(body, name=name, out_shape=jax.ShapeDtypeStruct((CHUNK, CHUNK), F32))(dbf)


def _pair_mask(hh):
    lane = lax.broadcasted_iota(jnp.int32, (1, 2 * HEAD_DIM), 1)
    return (lane >= HEAD_DIM * hh) & (lane < HEAD_DIM * (hh + 1))


def _lane_pick(x2, lm):
    return jnp.max(jnp.where(lm, x2, -jnp.inf), axis=1, keepdims=True)


def _attn_specs(nb):
    cur = pl.BlockSpec((BAND, B_WIDTH), lambda r, j: (j, r))
    prev = pl.BlockSpec((BAND, B_WIDTH), lambda r, j: (jnp.maximum(j - 1, 0), r))
    nxt = pl.BlockSpec((BAND, B_WIDTH), lambda r, j: (jnp.minimum(j + 1, nb - 1), r))
    return cur, prev, nxt


def _band_valid_q(j):
    row = lax.broadcasted_iota(jnp.int32, (BAND, 2 * BAND), 0)
    col = lax.broadcasted_iota(jnp.int32, (BAND, 2 * BAND), 1)
    return (col >= row) & (col <= row + BAND) & ((col >= BAND) | (j > 0))


def _attn_fwd(q, k, v, d, name):
    nb = q.shape[0] // BAND
    cur, prev, _ = _attn_specs(nb)

    def body(q_ref, kc_ref, kp_ref, vc_ref, vp_ref, o_ref, l_ref):
        valid = _band_valid_q(pl.program_id(1))
        for hp in range(N_HEADS // 2):
            sl = slice(2 * HEAD_DIM * hp, 2 * HEAD_DIM * (hp + 1))
            q2 = q_ref[:, sl]
            k2 = jnp.concatenate([kp_ref[:, sl], kc_ref[:, sl]], axis=0)
            v2 = jnp.concatenate([vp_ref[:, sl], vc_ref[:, sl]], axis=0)
            o2 = jnp.zeros((BAND, 2 * HEAD_DIM), F32)
            l2 = jnp.zeros((BAND, 2 * HEAD_DIM), F32)
            for hh in range(2):
                lm = _pair_mask(hh)
                s = _dotg(jnp.where(lm, q2, jnp.zeros_like(q2)), k2, 1, 1) * (HEAD_DIM ** -0.5)
                s = jnp.where(valid, s, NEG_INF)
                m = jnp.max(s, axis=1, keepdims=True)
                p = jnp.exp(s - m)
                den = jnp.sum(p, axis=1, keepdims=True)
                o = _dotg(p, v2, 1, 0) / den
                o2 = jnp.where(lm, o, o2)
                l2 = jnp.where(lm, m + jnp.log(den), l2)
            o_ref[:, sl] = o2
            l_ref[:, sl] = l2

    return pl.pallas_call(
        body, name=name, grid=(d, nb), in_specs=[cur, cur, prev, cur, prev], out_specs=[cur, cur],
        out_shape=[jax.ShapeDtypeStruct(q.shape, F32), jax.ShapeDtypeStruct(q.shape, F32)],
        compiler_params=_params("parallel", "parallel"),
    )(q, k, k, v, v)


def _attn_bwd_q(q, k, v, do, lse, delta, d, name):
    nb = q.shape[0] // BAND
    cur, prev, _ = _attn_specs(nb)

    def body(q_ref, kc_ref, kp_ref, vc_ref, vp_ref, do_ref, l_ref, dl_ref, dq_ref):
        valid = _band_valid_q(pl.program_id(1))
        for hp in range(N_HEADS // 2):
            sl = slice(2 * HEAD_DIM * hp, 2 * HEAD_DIM * (hp + 1))
            q2, do2, l2, dl2 = q_ref[:, sl], do_ref[:, sl], l_ref[:, sl], dl_ref[:, sl]
            k2 = jnp.concatenate([kp_ref[:, sl], kc_ref[:, sl]], axis=0)
            v2 = jnp.concatenate([vp_ref[:, sl], vc_ref[:, sl]], axis=0)
            dq2 = jnp.zeros((BAND, 2 * HEAD_DIM), F32)
            for hh in range(2):
                lm = _pair_mask(hh)
                s = _dotg(jnp.where(lm, q2, jnp.zeros_like(q2)), k2, 1, 1) * (HEAD_DIM ** -0.5)
                s = jnp.where(valid, s, NEG_INF)
                p = jnp.exp(s - _lane_pick(l2, lm))
                dp = _dotg(jnp.where(lm, do2, jnp.zeros_like(do2)), v2, 1, 1)
                ds = p * (dp - _lane_pick(dl2, lm))
                dq2 = jnp.where(lm, _dotg(ds, k2, 1, 0) * (HEAD_DIM ** -0.5), dq2)
            dq_ref[:, sl] = dq2

    return pl.pallas_call(
        body, name=name, grid=(d, nb), in_specs=[cur, cur, prev, cur, prev, cur, cur, cur], out_specs=cur,
        out_shape=jax.ShapeDtypeStruct(q.shape, F32),
        compiler_params=_params("parallel", "parallel"),
    )(q, k, k, v, v, do, lse, delta)


def _attn_bwd_kv(q, k, v, do, lse_t, delta_t, d, name):
    nb = q.shape[0] // BAND
    cur, _, nxt = _attn_specs(nb)
    t_cur = pl.BlockSpec((1, N_HEADS, BAND), lambda r, j: (r, 0, j))
    t_nxt = pl.BlockSpec((1, N_HEADS, BAND), lambda r, j: (r, 0, jnp.minimum(j + 1, nb - 1)))

    def body(k_ref, v_ref, qc_ref, qn_ref, doc_ref, don_ref, lc_ref, ln_ref, dlc_ref, dln_ref, dk_ref, dv_ref):
        j = pl.program_id(1)
        row = lax.broadcasted_iota(jnp.int32, (BAND, 2 * BAND), 0)
        col = lax.broadcasted_iota(jnp.int32, (BAND, 2 * BAND), 1)
        valid = (col >= row) & (col <= row + BAND) & ((col < BAND) | (j < nb - 1))
        for hp in range(N_HEADS // 2):
            sl = slice(2 * HEAD_DIM * hp, 2 * HEAD_DIM * (hp + 1))
            k2, v2 = k_ref[:, sl], v_ref[:, sl]
            q2 = jnp.concatenate([qc_ref[:, sl], qn_ref[:, sl]], axis=0)
            do2 = jnp.concatenate([doc_ref[:, sl], don_ref[:, sl]], axis=0)
            dk2 = jnp.zeros((BAND, 2 * HEAD_DIM), F32)
            dv2 = jnp.zeros((BAND, 2 * HEAD_DIM), F32)
            for hh in range(2):
                h = 2 * hp + hh
                lm = _pair_mask(hh)
                lse_row = jnp.concatenate([lc_ref[0, h:h + 1, :], ln_ref[0, h:h + 1, :]], axis=1)
                dl_row = jnp.concatenate([dlc_ref[0, h:h + 1, :], dln_ref[0, h:h + 1, :]], axis=1)
                st = _dotg(jnp.where(lm, k2, jnp.zeros_like(k2)), q2, 1, 1) * (HEAD_DIM ** -0.5)
                st = jnp.where(valid, st, NEG_INF)
                pt = jnp.exp(st - lse_row)
                dv2 = jnp.where(lm, _dotg(pt, do2, 1, 0), dv2)
                dpt = _dotg(jnp.where(lm, v2, jnp.zeros_like(v2)), do2, 1, 1)
                dst = pt * (dpt - dl_row)
                dk2 = jnp.where(lm, _dotg(dst, q2, 1, 0) * (HEAD_DIM ** -0.5), dk2)
            dk_ref[:, sl] = dk2
            dv_ref[:, sl] = dv2

    return pl.pallas_call(
        body, name=name, grid=(d, nb),
        in_specs=[cur, cur, cur, nxt, cur, nxt, t_cur, t_nxt, t_cur, t_nxt], out_specs=[cur, cur],
        out_shape=[jax.ShapeDtypeStruct(q.shape, F32), jax.ShapeDtypeStruct(q.shape, F32)],
        compiler_params=_params("parallel", "parallel"),
    )(k, v, q, q, do, do, lse_t, lse_t, delta_t, delta_t)


def _head_rows(a, d):
    s = a.shape[0]
    return a[:, ::HEAD_DIM].reshape(s // d, d, N_HEADS).transpose(1, 2, 0)


FF_TS = 256
FF_TC = 512
HALO = 8


def _conv3(ext, w, b):
    return b + w[0:1, :] * pltpu.roll(ext, 2, 0) + w[1:2, :] * pltpu.roll(ext, 1, 0) + w[2:3, :] * ext


def _ffn_specs(s, cols_first):
    nrb = s // FF_TS
    per = FF_TS // HALO

    def mk(block, fn):
        if cols_first:
            return pl.BlockSpec(block, lambda j, i: fn(i, j))
        return pl.BlockSpec(block, lambda i, j: fn(i, j))

    specs = types.SimpleNamespace(
        nrb=nrb, ncb=D_FF // FF_TC,
        row=mk((FF_TS, FF_TC), lambda i, j: (i, j)),
        before=mk((HALO, FF_TC), lambda i, j: (jnp.maximum(i * per - 1, 0), j)),
        after=mk((HALO, FF_TC), lambda i, j: (jnp.minimum((i + 1) * per, nrb * per - 1), j)),
        w=mk((3, FF_TC), lambda i, j: (0, j)),
        b=mk((1, FF_TC), lambda i, j: (0, j)),
        acc=mk((HALO, FF_TC), lambda i, j: (0, j)),
    )
    return specs


def _geglu_fwd(ug, uv, wg, wv, bg, bv, name):
    s = ug.shape[0]
    sp = _ffn_specs(s, False)

    def body(ug_ref, uv_ref, hg_ref, hv_ref, wg_ref, wv_ref, bg_ref, bv_ref, y_ref):
        keep = jnp.where(pl.program_id(0) > 0, 1.0, 0.0)
        eg = jnp.concatenate([hg_ref[...] * keep, ug_ref[...]], axis=0)
        ev = jnp.concatenate([hv_ref[...] * keep, uv_ref[...]], axis=0)
        gate = _conv3(eg, wg_ref[...], bg_ref[...])[HALO:, :]
        val = _conv3(ev, wv_ref[...], bv_ref[...])[HALO:, :]
        y_ref[...] = (_gelu_tanh(gate) * val).astype(y_ref.dtype)

    return pl.pallas_call(
        body, name=name, grid=(sp.nrb, sp.ncb),
        in_specs=[sp.row, sp.row, sp.before, sp.before, sp.w, sp.w, sp.b, sp.b],
        out_specs=sp.row, out_shape=jax.ShapeDtypeStruct((s, D_FF), BF16),
        compiler_params=_params("parallel", "parallel"),
    )(ug, uv, ug, uv, wg, wv, bg, bv)


def _geglu_bwd(ug, uv, dy, wg, wv, bg, bv, name):
    s = ug.shape[0]
    sp = _ffn_specs(s, True)
    nrb = sp.nrb
    rows = FF_TS + 2 * HALO

    def body(ug_ref, uv_ref, hg_ref, hv_ref, ng_ref, nv_ref, dy_ref, dyn_ref, wg_ref, wv_ref, bg_ref, bv_ref,
             dug_ref, duv_ref, dwg_ref, dwv_ref):
        i = pl.program_id(1)
        keep_top = jnp.where(i > 0, 1.0, 0.0)
        keep_bot = jnp.where(i < nrb - 1, 1.0, 0.0)
        eg = jnp.concatenate([hg_ref[...] * keep_top, ug_ref[...], ng_ref[...]], axis=0)
        ev = jnp.concatenate([hv_ref[...] * keep_top, uv_ref[...], nv_ref[...]], axis=0)
        dye = jnp.concatenate([jnp.zeros((HALO, FF_TC), F32), dy_ref[...], dyn_ref[...] * keep_bot], axis=0)
        wg_, wv_ = wg_ref[...], wv_ref[...]
        gate = _conv3(eg, wg_, bg_ref[...])
        val = _conv3(ev, wv_, bv_ref[...])
        act, act_vjp = jax.vjp(_gelu_tanh, gate)
        dgate = act_vjp(dye * val)[0]
        dval = dye * act

        lo, hi = HALO, HALO + FF_TS

        def back(dc, e, w, du_ref):
            up1 = pltpu.roll(dc, rows - 1, 0)
            up2 = pltpu.roll(dc, rows - 2, 0)
            du_ref[...] = (w[2:3, :] * dc + w[1:2, :] * up1 + w[0:1, :] * up2)[lo:hi, :].astype(du_ref.dtype)
            p1, p2 = up1 * e, up2 * e
            colsum = lambda p: jnp.sum(p[lo:hi, :], axis=0, keepdims=True)
            row = lambda p, t: p[t:t + 1, :]
            d_w1 = colsum(p1) + row(p1, lo - 1) - row(p1, hi - 1)
            d_w0 = colsum(p2) + row(p2, lo - 2) + row(p2, lo - 1) - row(p2, hi - 2) - row(p2, hi - 1)
            return jnp.concatenate([d_w0, d_w1, colsum(dc * e), colsum(dc), jnp.zeros((HALO - 4, FF_TC), F32)], axis=0)

        sg = back(dgate, eg, wg_, dug_ref)
        sv = back(dval, ev, wv_, duv_ref)

        @pl.when(i == 0)
        def _():
            dwg_ref[...] = sg
            dwv_ref[...] = sv

        @pl.when(i > 0)
        def _():
            dwg_ref[...] += sg
            dwv_ref[...] += sv

    return pl.pallas_call(
        body, name=name, grid=(sp.ncb, nrb),
        in_specs=[sp.row, sp.row, sp.before, sp.before, sp.after, sp.after, sp.row, sp.after,
                  sp.w, sp.w, sp.b, sp.b],
        out_specs=[sp.row, sp.row, sp.acc, sp.acc],
        out_shape=[jax.ShapeDtypeStruct((s, D_FF), BF16), jax.ShapeDtypeStruct((s, D_FF), BF16),
                   jax.ShapeDtypeStruct((HALO, D_FF), F32), jax.ShapeDtypeStruct((HALO, D_FF), F32)],
        compiler_params=_params("parallel", "arbitrary"),
    )(ug, uv, ug, uv, ug, uv, dy, dy, wg, wv, bg, bv)


def _rope_tables(s):
    inv = ROPE_THETA ** (-jnp.arange(0, ROT_DIM, 2, dtype=F32) / ROT_DIM)
    ang = jnp.arange(s, dtype=F32)[:, None] * inv[None, :]
    cos8, sin8 = jnp.cos(ang), jnp.sin(ang)
    rest = HEAD_DIM - ROT_DIM
    cos_h = jnp.concatenate([cos8, cos8, jnp.ones((s, rest), F32)], axis=1)
    sin_h = jnp.concatenate([sin8, sin8, jnp.zeros((s, rest), F32)], axis=1)
    return jnp.tile(cos_h, (1, N_HEADS)), jnp.tile(sin_h, (1, N_HEADS))


def _layer_fwd(x, w, cos, sin):
    sv = types.SimpleNamespace(x=x)
    (sv.h1,), _ = _rowwise(lambda xb, g: ((_rms(xb, g),), ()), [_full(x)], [w.g_pre], [(D_MODEL, BF16)], [],
                           ts=512, name="pre_mix_norm")
    sv.proj = _matmul(sv.h1, w.big("w_in", sv.h1), mode="nn", out_dtype=F32, name="proj")

    gate_consts = [w.vg, w.vb, *w.ws, w.bfull, w.ga]
    (na,), _ = _rowwise(lambda *a: ((_gate_fn(*a),), ()), [(sv.proj, A_WIDTH, 0), (sv.proj, A_WIDTH, 1)], gate_consts,
                        [(A_WIDTH, BF16)], [], ts=CHUNK, name="gate_fwd")

    def rope_fn(qr, kr, vr, cs, sn):
        return (qr * cs + _rot_half(qr) * sn, kr * cs + _rot_half(kr) * sn, vr), ()

    def rope_all(qr, kr, vr, cs, sn):
        return rope_fn(qr, kr, vr, cs, sn)[0] * len(DILATIONS), ()

    qkv, _ = _rowwise(
        rope_all, [(sv.proj, B_WIDTH, 2), (sv.proj, B_WIDTH, 3), (sv.proj, B_WIDTH, 4), _full(cos), _full(sin)], [],
        [(B_WIDTH, BF16, d) for d in DILATIONS for _ in range(3)], [], ts=512, name="rope_fwd")
    sv.qkv = {d: qkv[3 * i:3 * i + 3] for i, d in enumerate(DILATIONS)}

    branch = []
    for d in DILATIONS:
        o, l = _attn_fwd(*sv.qkv[d], d, name=f"attn_fwd_d{d}")
        branch += [_full(o, d), _full(l, d)]

    def combine_fn(o1, l1, o2, l2, o3, l3, nab, gb):
        m = jnp.maximum(jnp.maximum(l1, l2), l3)
        e1, e2, e3 = jnp.exp(l1 - m), jnp.exp(l2 - m), jnp.exp(l3 - m)
        den = e1 + e2 + e3
        ob = (e1 / den) * o1 + (e2 / den) * o2 + (e3 / den) * o3
        mixed = jnp.concatenate([nab, _rms(ob, gb).astype(BF16)], axis=1)
        lse = m + jnp.log(den)
        return (mixed, ob) + (lse,) * len(DILATIONS), ()

    (sv.mixed, sv.ob, *lses), _ = _rowwise(
        combine_fn, branch + [_full(na)], [w.gb],
        [(D_MODEL, BF16), (B_WIDTH, F32)] + [(B_WIDTH, F32, d) for d in DILATIONS], [], ts=512, name="combine")
    sv.lse = dict(zip(DILATIONS, lses))
    sv.y = _matmul(sv.mixed, w.big("w_out", sv.mixed), mode="nn", out_dtype=F32, name="mix_out")

    def mid_fn(xb, yb, g1, g2):
        x1 = xb + _rms(yb, g1)
        return (x1, _rms(x1, g2)), ()

    (sv.x1, sv.h2), _ = _rowwise(mid_fn, [_full(x), _full(sv.y)], [w.g_pm, w.g_pf], [(D_MODEL, F32), (D_MODEL, BF16)], [],
                                 ts=512, name="post_mix_norm")
    w_up = w.big("w_up", sv.h2)
    sv.ug = _matmul(sv.h2, w_up, mode="nn", out_dtype=F32, name="ffn_up_gate", cols=(0, N_DEV // 2))
    sv.uv = _matmul(sv.h2, w_up, mode="nn", out_dtype=F32, name="ffn_up_val", cols=(N_DEV // 2, N_DEV // 2))
    conv_w = w.big("conv_w", sv.h2)
    sv.yff = _geglu_fwd(sv.ug, sv.uv, conv_w[:, :D_FF], conv_w[:, D_FF:], w.cb_g, w.cb_v, name="geglu_fwd")
    sv.f = _matmul(sv.yff, w.big("w_down", sv.yff), mode="nn", out_dtype=F32, name="ffn_down")
    (x2,), _ = _rowwise(lambda xb, fb, g: ((xb + _rms(fb, g),), ()), [_full(sv.x1), _full(sv.f)], [w.g_post],
                        [(D_MODEL, F32)], [], ts=512, name="post_ffn_norm")
    return x2, sv


def _layer_bwd(dx2, sv, w, cos, sin, emit):
    g = {}

    def post_fn(fb, dxb, gp):
        _, vjp = jax.vjp(_rms, fb, gp)
        df, dg = vjp(dxb)
        return (df,), (dg,)

    (df,), (g["post_ffn_norm"],) = _rowwise(post_fn, [_full(sv.f), _full(dx2)], [w.g_post], [(D_MODEL, BF16)],
                                            [(1, D_MODEL)], ts=512, name="post_ffn_norm_bwd")
    dyff = _matmul(df, w.big("w_down", df), mode="nt", out_dtype=F32, name="ffn_down_dx")
    big = {"w_down": _matmul(sv.yff, df, mode="tn", out_dtype=BF16, name="ffn_down_dw").reshape(N_DEV, -1, D_MODEL)}
    conv_w, w_up = w.big("conv_w", df), w.big("w_up", df)
    dug, duv, dwg, dwv = _geglu_bwd(sv.ug, sv.uv, dyff, conv_w[:, :D_FF], conv_w[:, D_FF:], w.cb_g, w.cb_v,
                                    name="geglu_bwd")
    big["conv_w"] = jnp.concatenate([dwg[0:3], dwv[0:3]], axis=1).reshape(3, N_DEV, D_MODEL).transpose(1, 0, 2)
    g["conv_b"] = jnp.concatenate([dwg[3], dwv[3]], axis=0)
    dh2a = _matmul(dug, w_up, mode="nt", out_dtype=F32, name="ffn_up_dx_gate", cols=(0, N_DEV // 2))
    dh2b = _matmul(duv, w_up, mode="nt", out_dtype=F32, name="ffn_up_dx_val", cols=(N_DEV // 2, N_DEV // 2))
    big["w_up"] = _matmul_by_destination(sv.h2, dug, duv, name="ffn_up_dw")
    g_pm = w.g_pm + emit(big)

    def mid_fn(x1b, yb, da, db, dxb, g1, g2):
        _, vjp2 = jax.vjp(_rms, x1b, g2)
        dx1h, dg2 = vjp2(da + db)
        dx1 = dxb + dx1h
        _, vjp1 = jax.vjp(_rms, yb, g1)
        dy, dg1 = vjp1(dx1)
        return (dx1, dy), (dg1, dg2)

    (dx1, dy), (g["post_mix_norm"], g["pre_ffn_norm"]) = _rowwise(
        mid_fn, [_full(sv.x1), _full(sv.y), _full(dh2a), _full(dh2b), _full(dx2)], [g_pm, w.g_pf],
        [(D_MODEL, F32), (D_MODEL, BF16)], [(1, D_MODEL), (1, D_MODEL)], ts=256, name="post_mix_norm_bwd")
    dmixed = _matmul(dy, w.big("w_out", dy), mode="nt", out_dtype=F32, name="mix_out_dx")
    big = {"w_out": _matmul(sv.mixed, dy, mode="tn", out_dtype=BF16, name="mix_out_dw").reshape(N_DEV, -1, D_MODEL)}

    def attn_out_fn(obb, dmb, gb):
        _, vjp = jax.vjp(_rms, obb, gb)
        do, dgb = vjp(dmb)
        delta = _head_sum(do * obb)
        return (do,) * len(DILATIONS) + (delta,) * len(DILATIONS), (dgb,)

    outs, (g["out_norm_b"],) = _rowwise(
        attn_out_fn, [_full(sv.ob), (dmixed, B_WIDTH, 1)], [w.gb],
        [(B_WIDTH, BF16, d) for d in DILATIONS] + [(B_WIDTH, F32, d) for d in DILATIONS], [(1, B_WIDTH)], ts=512,
        name="attn_out_bwd")
    do = dict(zip(DILATIONS, outs[:len(DILATIONS)]))
    delta = dict(zip(DILATIONS, outs[len(DILATIONS):]))
    parts = {"q": [], "k": [], "v": []}
    for d in DILATIONS:
        qv, kv, vv = sv.qkv[d]
        dq = _attn_bwd_q(qv, kv, vv, do[d], sv.lse[d], delta[d], d, name=f"attn_bwd_q_d{d}")
        dk, dv = _attn_bwd_kv(qv, kv, vv, do[d], _head_rows(sv.lse[1], d), _head_rows(delta[1], d), d,
                              name=f"attn_bwd_kv_d{d}")
        parts["q"].append(_full(dq, d))
        parts["k"].append(_full(dk, d))
        parts["v"].append(_full(dv, d))

    def rope_bwd_fn(q1, q2, q3, k1, k2, k3, v1, v2, v3, cs, sn):
        def back(t):
            return t * cs - _rot_half(t * sn)
        return (jnp.concatenate([back(q1 + q2 + q3), back(k1 + k2 + k3), v1 + v2 + v3], axis=1),), ()

    (dzb,), _ = _rowwise(rope_bwd_fn, parts["q"] + parts["k"] + parts["v"] + [_full(cos), _full(sin)], [],
                         [(3 * B_WIDTH, BF16)], [], ts=256, name="rope_bwd")

    gate_consts = [w.vg, w.vb, *w.ws, w.bfull, w.ga]

    def gate_bwd_fn(zu, zv, dna, *consts):
        _, vjp = jax.vjp(_gate_fn, zu, zv, *consts)
        grads = vjp(dna)
        return (jnp.concatenate([grads[0], grads[1]], axis=1),), tuple(grads[2:])

    (dza,), gsmall = _rowwise(
        gate_bwd_fn, [(sv.proj, A_WIDTH, 0), (sv.proj, A_WIDTH, 1), (dmixed, A_WIDTH, 0)], gate_consts,
        [(2 * A_WIDTH, BF16)], [c.shape for c in gate_consts], ts=CHUNK, name="gate_bwd")
    g["v_norm_g"], g["v_norm_b"] = gsmall[0], gsmall[1]
    g["w_spatial"] = jnp.stack(gsmall[2:6])
    g["b_spatial"] = _bias_reduce(gsmall[6], name="bias_reduce")[:, :A_GROUPS].T
    g["out_norm_a"] = gsmall[7]

    dproj = jnp.concatenate([dza, dzb], axis=1)
    dh1 = _matmul(dproj, w.big("w_in", dproj), mode="nt", out_dtype=F32, name="proj_dx")
    dw_in = _matmul(sv.h1, dproj, mode="tn", out_dtype=BF16, name="proj_dw")
    big["w_in"] = dw_in.reshape(D_MODEL, N_DEV, IN_COLS // N_DEV).transpose(1, 0, 2)
    g_pre = w.g_pre + emit(big)

    def pre_fn(xb, dhb, dxb, gp):
        _, vjp = jax.vjp(_rms, xb, gp)
        dxh, dg = vjp(dhb)
        return (dxb + dxh,), (dg,)

    (dx,), (g["pre_mix_norm"],) = _rowwise(pre_fn, [_full(sv.x), _full(dh1), _full(dx1)], [g_pre], [(D_MODEL, F32)],
                                           [(1, D_MODEL)], ts=512, name="pre_mix_norm_bwd")
    return dx, g


def _layer_weights(l, full, small):
    row = lambda a: a[l].reshape(1, -1)
    return types.SimpleNamespace(
        big=functools.partial(full, l),
        g_pre=row(small["pre_mix_norm"]), vg=row(small["v_norm_g"]), vb=row(small["v_norm_b"]),
        ws=[small["w_spatial"][l, gi] for gi in range(A_GROUPS)],
        bfull=jnp.repeat(small["b_spatial"][l].T, CHUNK, axis=1),
        ga=row(small["out_norm_a"]), gb=row(small["out_norm_b"]),
        g_pm=row(small["post_mix_norm"]), g_pf=row(small["pre_ffn_norm"]),
        cb_g=small["conv_b"][l][:D_FF].reshape(1, -1), cb_v=small["conv_b"][l][D_FF:].reshape(1, -1),
        g_post=row(small["post_ffn_norm"]))


def _local_step(x, target, full, small, emit, started):
    s = x.shape[0]
    cos, sin = _rope_tables(s)
    ws = [_layer_weights(l, full, small) for l in range(N_LAYERS)]
    ws[0].g_pre = ws[0].g_pre + started
    saved = []
    h = x
    for l in range(N_LAYERS):
        h, sv = _layer_fwd(h, ws[l], cos, sin)
        saved.append(sv)

    def loss_fn(yb, tb):
        diff = yb - tb
        return (diff * (1.0 / D_MODEL),), (jnp.sum(diff * diff, axis=0, keepdims=True),)

    (dh,), (sq,) = _rowwise(loss_fn, [_full(h), _full(target)], [], [(D_MODEL, F32)], [(1, D_MODEL)], ts=512, name="loss")
    loss = 0.5 * jnp.sum(sq) * (1.0 / D_MODEL)
    grads = [None] * N_LAYERS
    for l in reversed(range(N_LAYERS)):
        dh, grads[l] = _layer_bwd(dh, saved[l], ws[l], cos, sin, functools.partial(emit, l))
    return loss, dh, grads


def _place():
    return lax.axis_index("x"), lax.axis_index("y"), lax.axis_index("c")


def _all_gather(x, name):
    def body(x_ref, out_ref, send_sems, recv_sems, local_sem):
        mx, my, mc = _place()
        me, sibling = (mx, my, mc), (mx, my, 1 - mc)
        chips = [(1 - mx, my), (mx, 1 - my), (1 - mx, 1 - my)]

        def slot(px, py, pc):
            return out_ref.at[4 * px + 2 * py + pc]

        def copy(k, block, to, src=None):
            return pltpu.make_async_remote_copy(
                src_ref=slot(*block) if src is None else src, dst_ref=slot(*block),
                send_sem=send_sems.at[k], recv_sem=recv_sems.at[k], device_id=to, device_id_type=MESH_ID)

        mine = pltpu.make_async_copy(x_ref, slot(*me), local_sem)
        mine.start()
        first = [copy(0, me, sibling, src=x_ref)]
        first += [copy(1 + j, me, (*chip, mc), src=x_ref) for j, chip in enumerate(chips)]
        for cp in first:
            cp.start()
        passed = [copy(4 + j, (*chip, mc), sibling) for j, chip in enumerate(chips)]
        for j, chip in enumerate(chips):
            copy(1 + j, (*chip, mc), me).wait_recv()
            passed[j].start()
        copy(0, sibling, me).wait_recv()
        for j, chip in enumerate(chips):
            copy(4 + j, (*chip, 1 - mc), me).wait_recv()
        for cp in first + passed:
            cp.wait_send()
        mine.wait()

    return pl.pallas_call(
        body, name=name, out_shape=jax.ShapeDtypeStruct((N_DEV,) + x.shape, x.dtype), in_specs=[ANY], out_specs=ANY,
        scratch_shapes=[pltpu.SemaphoreType.DMA((7,)), pltpu.SemaphoreType.DMA((7,)), pltpu.SemaphoreType.DMA],
    )(x)


FLIPS = ((1, 0, 0), (0, 1, 0), (1, 1, 0), (0, 0, 1), (1, 0, 1), (0, 1, 1), (1, 1, 1))
HBM_SPEC = pl.BlockSpec(memory_space=pltpu.HBM)
SEM_SPEC = pl.BlockSpec(memory_space=pltpu.SEMAPHORE)
SPLIT_COPY = pltpu.CompilerParams(has_side_effects=pltpu.SideEffectType.DATAFLOW_SIDE_EFFECTING)


def _peers():
    mx, my, mc = _place()
    out = []
    for fx, fy, fc in FLIPS:
        px, py, pc = (1 - mx if fx else mx), (1 - my if fy else my), (1 - mc if fc else mc)
        out.append(((px, py, pc), 4 * px + 2 * py + pc))
    return out


def _flat_copies(scatter, src_refs, land_refs, send_sems, recv_sems):
    mx, my, mc = _place()
    me = 4 * mx + 2 * my + mc
    n = len(src_refs)
    copies = []
    for t in range(n):
        for i, (peer, number) in enumerate(_peers()):
            copies.append(pltpu.make_async_remote_copy(
                src_ref=src_refs[t].at[number] if scatter else src_refs[t],
                dst_ref=land_refs[t].at[i] if scatter else land_refs[t].at[me],
                send_sem=send_sems.at[t * len(FLIPS) + i], recv_sem=recv_sems.at[t * len(FLIPS) + i],
                device_id=peer, device_id_type=MESH_ID))
    return copies


def _flat_start(arrays, scatter, name):
    n = len(arrays)
    slots = len(FLIPS) if scatter else N_DEV
    lands = [lax.empty((slots,) + (a.shape[1:] if scatter else a.shape), a.dtype) for a in arrays]

    def body(*refs):
        src, land, (send_sems, recv_sems), token = refs[:n], refs[n:2 * n], refs[2 * n:2 * n + 2], refs[-1]
        for cp in _flat_copies(scatter, src, land, send_sems, recv_sems):
            cp.start()
        token[...] = jnp.zeros_like(token)

    hbm = [pltpu.HBM(a.shape, a.dtype) for a in arrays] + [pltpu.HBM(a.shape, a.dtype) for a in lands]
    sems = pltpu.SemaphoreType.DMA((n * len(FLIPS),))
    outs = pl.pallas_call(
        body, name=name, out_shape=(sems, sems, *hbm, jax.ShapeDtypeStruct((8, 128), F32)),
        in_specs=[HBM_SPEC] * (2 * n),
        out_specs=(SEM_SPEC, SEM_SPEC, *([HBM_SPEC] * (2 * n)), pl.BlockSpec(memory_space=pltpu.VMEM)),
        input_output_aliases={t: 2 + t for t in range(2 * n)}, compiler_params=SPLIT_COPY,
    )(*[pltpu.with_memory_space_constraint(a, pltpu.HBM) for a in (*arrays, *lands)])
    return types.SimpleNamespace(sems=outs[:2], thru=outs[2:2 + 2 * n], scatter=scatter, n=n), outs[-1][0:1, 0:1]


def _flat_wait(handle, after, name):
    n = handle.n

    def body(*refs):
        src, land, (send_sems, recv_sems) = refs[:n], refs[n:2 * n], refs[2 * n:2 * n + 2]
        for cp in _flat_copies(handle.scatter, src, land, send_sems, recv_sems):
            cp.wait_send()
            cp.wait_recv()

    outs = pl.pallas_call(
        body, name=name, out_shape=tuple(pltpu.HBM(a.shape, a.dtype) for a in handle.thru),
        in_specs=[HBM_SPEC] * (2 * n) + [SEM_SPEC, SEM_SPEC, ANY], out_specs=tuple([HBM_SPEC] * (2 * n)),
        input_output_aliases={t: t for t in range(2 * n)}, compiler_params=SPLIT_COPY,
    )(*handle.thru, *handle.sems, after)
    return outs[:n], outs[n:]


def _adamw(w, g, m, v):
    m2 = ADAM_B1 * m + (1.0 - ADAM_B1) * g
    v2 = ADAM_B2 * v + (1.0 - ADAM_B2) * (g * g)
    m_hat = m2 / (1.0 - ADAM_B1 ** ADAM_STEP)
    v_hat = v2 / (1.0 - ADAM_B2 ** ADAM_STEP)
    return -ADAM_LR * (m_hat / (jnp.sqrt(v_hat) + ADAM_EPS) + ADAM_WD * w), m2, v2


def _adamw_sharded(me, mine, landed, w, m, v, tr, name):
    _, r, c = w.shape
    nt = r // tr
    assert r % tr == 0 and len(mine) == len(landed) == N_LAYERS == 2, name
    per_layer = 1 + len(FLIPS)

    def body(me_ref, *refs):
        terms, (w_ref, m_ref, v_ref), outs = refs[:2 * per_layer], refs[2 * per_layer:2 * per_layer + 3], refs[-4:]
        layer = pl.program_id(0)

        def total(group):
            g = group[0][0].astype(F32)
            for t in group[1:]:
                g = g + t[0].astype(F32)
            return g

        g = jnp.where(layer == 0, total(terms[:per_layer]), total(terms[per_layer:]))
        d, m2, v2 = _adamw(w_ref[0], g, m_ref[0], v_ref[0])
        for o, val in zip(outs, (g, d, m2, v2)):
            o[0] = val

    def held(l):
        return lambda layer, i: jnp.where(layer == l, i, nt - 1 if l == 0 else 0)

    in_specs = []
    for l in range(N_LAYERS):
        rows = held(l)
        in_specs.append(pl.BlockSpec((1, tr, c), functools.partial(lambda layer, i, me_ref, rows: (me_ref[0], rows(layer, i), 0), rows=rows)))
        for k in range(len(FLIPS)):
            in_specs.append(pl.BlockSpec(
                (1, tr, c), functools.partial(lambda layer, i, me_ref, rows, k: (k, rows(layer, i), 0), rows=rows, k=k)))
    tile = pl.BlockSpec((1, tr, c), lambda layer, i, me_ref: (layer, i, 0))
    operands = []
    for l in range(N_LAYERS):
        operands += [mine[l]] + [landed[l]] * len(FLIPS)
    return pl.pallas_call(
        body, name=name, out_shape=[jax.ShapeDtypeStruct(w.shape, F32)] * 4,
        grid_spec=pltpu.PrefetchScalarGridSpec(
            num_scalar_prefetch=1, grid=(N_LAYERS, nt), in_specs=in_specs + [tile] * 3, out_specs=[tile] * 4),
        compiler_params=_params("arbitrary", "arbitrary"),
    )(me, *operands, w, m, v)


def _adamw_replicated(parts, w, m, v, name):
    def body(p_ref, w_ref, m_ref, v_ref, g_ref, d_ref, m2_ref, v2_ref):
        g = p_ref[0]
        for j in range(1, N_DEV):
            g = g + p_ref[j]
        d, m2, v2 = _adamw(w_ref[...], g, m_ref[...], v_ref[...])
        g_ref[...], d_ref[...], m2_ref[...], v2_ref[...] = g, d, m2, v2

    return pl.pallas_call(body, name=name, out_shape=[jax.ShapeDtypeStruct(w.shape, F32)] * 4,
                          compiler_params=pltpu.CompilerParams(vmem_limit_bytes=VMEM_LIMIT_BYTES))(parts, w, m, v)


def _pack_small(vals):
    flat = jnp.concatenate([vals[n].reshape(-1) for n in SMALL_NAMES])
    return jnp.concatenate([flat, jnp.zeros((SMALL_ROWS * D_MODEL - flat.shape[0],), F32)]).reshape(SMALL_ROWS, D_MODEL)


def _unpack_small(packed, shapes):
    flat, out, at = packed.reshape(-1), {}, 0
    for n in SMALL_NAMES:
        size = math.prod(shapes[n])
        out[n] = flat[at:at + size].reshape(shapes[n])
        at += size
    return out


GATHER_GROUPS = ((0, ("w_in",)), (0, ("w_out", "w_up", "conv_w", "w_down")),
                 (1, ("w_in", "w_out", "w_up", "conv_w", "w_down")))
ADAMW_TILE_ROWS = {"w_in": 512, "w_out": 128, "w_up": 256, "w_down": 256, "conv_w": 3}


def _assemble(name, land):
    if name == "w_in":
        return land.transpose(1, 0, 2).reshape(D_MODEL, IN_COLS)
    if name == "conv_w":
        return land.transpose(1, 0, 2).reshape(3, 2 * D_FF)
    if name == "w_up":
        return land
    return land.reshape(-1, D_MODEL)


def _start_gathers(wts, me):
    started, groups = jnp.zeros((1, 1), F32), []
    for gi, (l, names) in enumerate(GATHER_GROUPS):
        blocks = [wts[n][l] if n == "conv_w" else wts[n][l].astype(BF16) for n in names]
        handle, token = _flat_start(blocks, False, name=f"gather_start_{gi}")
        groups.append(types.SimpleNamespace(layer=l, names=names, blocks=blocks, handle=handle, got=None, index=gi))
        started = started + token

    def fetch(l, name, after):
        grp = next(gr for gr in groups if gr.layer == l and name in gr.names)
        if grp.got is None:
            lands = _flat_wait(grp.handle, after, name=f"gather_wait_{grp.index}")[1]
            grp.got = {}
            for n, blk, land in zip(grp.names, grp.blocks, lands):
                own = (me,) + (0,) * blk.ndim
                grp.got[n] = _assemble(n, lax.dynamic_update_slice(land, blk[None], own))
        return grp.got[name]

    return fetch, started


def kernel(x, pre_mix_norm, w_in, v_norm_g, v_norm_b, w_spatial, b_spatial, out_norm_a, out_norm_b, w_out, post_mix_norm, pre_ffn_norm, w_up, conv_w, conv_b, w_down, post_ffn_norm, loss_target, m_pre_mix_norm, m_w_in, m_v_norm_g, m_v_norm_b, m_w_spatial, m_b_spatial, m_out_norm_a, m_out_norm_b, m_w_out, m_post_mix_norm, m_pre_ffn_norm, m_w_up, m_conv_w, m_conv_b, m_w_down, m_post_ffn_norm, v_pre_mix_norm, v_w_in, v_v_norm_g, v_v_norm_b, v_w_spatial, v_b_spatial, v_out_norm_a, v_out_norm_b, v_w_out, v_post_mix_norm, v_pre_ffn_norm, v_w_up, v_conv_w, v_conv_b, v_w_down, v_post_ffn_norm):
    wts = dict(zip(WEIGHT_NAMES, (pre_mix_norm, w_in, v_norm_g, v_norm_b, w_spatial, b_spatial, out_norm_a, out_norm_b,
                                  w_out, post_mix_norm, pre_ffn_norm, w_up, conv_w, conv_b, w_down, post_ffn_norm)))
    mom1 = dict(zip(WEIGHT_NAMES, (m_pre_mix_norm, m_w_in, m_v_norm_g, m_v_norm_b, m_w_spatial, m_b_spatial, m_out_norm_a,
                                   m_out_norm_b, m_w_out, m_post_mix_norm, m_pre_ffn_norm, m_w_up, m_conv_w, m_conv_b,
                                   m_w_down, m_post_ffn_norm)))
    mom2 = dict(zip(WEIGHT_NAMES, (v_pre_mix_norm, v_w_in, v_v_norm_g, v_v_norm_b, v_w_spatial, v_b_spatial, v_out_norm_a,
                                   v_out_norm_b, v_w_out, v_post_mix_norm, v_pre_ffn_norm, v_w_up, v_conv_w, v_conv_b,
                                   v_w_down, v_post_ffn_norm)))
    mx, my, mc = _place()
    me = 4 * mx + 2 * my + mc

    fetch, started = _start_gathers(wts, me)
    scatters = []

    def emit(l, blocks):
        names = tuple(blocks)
        handle, token = _flat_start([blocks[n] for n in names], True, name=f"scatter_start_{l}_{len(scatters) % 2}")
        scatters.append((l, names, handle))
        return token

    loss_local, dx, grads = _local_step(x[0], loss_target[0], fetch, wts, emit, started)
    loss = lax.psum(loss_local, AXES)

    mine, landed = {}, {}
    for l, names, handle in scatters:
        sent, lands = _flat_wait(handle, dx, name=f"scatter_wait_{l}_{'_'.join(names)}")
        for n, a, b in zip(names, sent, lands):
            mine[l, n], landed[l, n] = a, b
    me_arr = jnp.reshape(me, (1,)).astype(jnp.int32)
    big_out = [{}, {}, {}, {}]
    for n in BIG_NAMES:
        res = _adamw_sharded(me_arr, [mine[l, n] for l in range(N_LAYERS)], [landed[l, n] for l in range(N_LAYERS)],
                             wts[n], mom1[n], mom2[n], ADAMW_TILE_ROWS[n], name=f"adamw_{n}")
        for kind in range(4):
            big_out[kind][n] = res[kind]

    small_grads = {n: jnp.stack([g[n].reshape(wts[n].shape[1:]) for g in grads]) for n in SMALL_NAMES}
    everyone = _all_gather(_pack_small(small_grads), name="gather_small_grads")
    small = [_pack_small({n: t[n] for n in SMALL_NAMES}) for t in (wts, mom1, mom2)]
    small_out = _adamw_replicated(everyone, *small, name="adamw_replicated")
    small_shapes = {n: wts[n].shape for n in SMALL_NAMES}
    small_out = [_unpack_small(o, small_shapes) for o in small_out]

    outs = [loss, dx[None]]
    for kind in range(4):
        outs += [big_out[kind][n] if n in BIG_NAMES else small_out[kind][n] for n in WEIGHT_NAMES]
    return tuple(outs)
```

```python
import functools
import math
import types

import jax
import jax.numpy as jnp
from jax import lax
from jax.experimental import pallas as pl
from jax.experimental.pallas import tpu as pltpu

F32 = jnp.float32
BF16 = jnp.bfloat16

D_MODEL = 1024
A_WIDTH = 512
A_GROUPS = 4
CHUNK = 128
B_WIDTH = 512
HEAD_DIM = 64
N_HEADS = B_WIDTH // HEAD_DIM
ROT_DIM = 16
ROPE_THETA = 500000.0
BAND = 128
DILATIONS = (1, 4, 16)
IN_COLS = 2560
D_FF = 4096
EPS = 1e-6
NEG_INF = -1e30
N_DEV = 8
N_LAYERS = 2

ADAM_LR = 0.001
ADAM_B1 = 0.9
ADAM_B2 = 0.999
ADAM_EPS = 1e-08
ADAM_WD = 0.01
ADAM_STEP = 10

VMEM_LIMIT_BYTES = 56 * 1024 * 1024
MESH_ID = pl.DeviceIdType.MESH
ANY = pl.BlockSpec(memory_space=pl.ANY)
AXES = ("x", "y", "c")

WEIGHT_NAMES = ("pre_mix_norm", "w_in", "v_norm_g", "v_norm_b", "w_spatial", "b_spatial", "out_norm_a", "out_norm_b",
                "w_out", "post_mix_norm", "pre_ffn_norm", "w_up", "conv_w", "conv_b", "w_down", "post_ffn_norm")
BIG_NAMES = ("w_in", "w_out", "w_up", "w_down", "conv_w")
SMALL_NAMES = tuple(n for n in WEIGHT_NAMES if n not in BIG_NAMES)

PACK_ROWS = {"w_in": 640, "w_out": 256, "w_up": 2048, "w_down": 1024, "conv_w": 6}
CONV_W_PAD = 2
PACKED_F32_ROWS = 4096
PACKED_BF16_ROWS = 3984
SMALL_ROWS = 160


def _params(*sem):
    return pltpu.CompilerParams(dimension_semantics=sem, vmem_limit_bytes=VMEM_LIMIT_BYTES)


def _dotg(a, b, ca, cb):
    return lax.dot_general(a.astype(BF16), b.astype(BF16), (((ca,), (cb,)), ((), ())), preferred_element_type=F32)


@jax.custom_vjp
def _bdot(a, b):
    return _dotg(a, b, 1, 0)


def _bdot_fwd(a, b):
    return _dotg(a, b, 1, 0), (a, b)


def _bdot_bwd(res, g):
    a, b = res
    return _dotg(g, b, 1, 1), _dotg(a, g, 0, 0)


_bdot.defvjp(_bdot_fwd, _bdot_bwd)


def _rms(x, g):
    return x * lax.rsqrt(jnp.mean(x * x, axis=-1, keepdims=True) + EPS) * g


def _layernorm(x, g, b):
    mu = jnp.mean(x, axis=-1, keepdims=True)
    xc = x - mu
    return xc * lax.rsqrt(jnp.mean(xc * xc, axis=-1, keepdims=True) + EPS) * g + b


def _gelu_erf(x):
    return x * (lax.erf(x * (1.0 / math.sqrt(2.0))) + 1.0) * 0.5


def _gelu_tanh(x):
    c = math.sqrt(2.0 / math.pi)
    return 0.5 * x * (1.0 + jnp.tanh(c * (x + 0.044715 * (x * x * x))))


def _gelu_tanh_and_slope(x):
    c, k = math.sqrt(2.0 / math.pi), 0.044715
    x2 = x * x
    t = jnp.tanh(c * (x + k * (x2 * x)))
    half_x, one_t = 0.5 * x, 1.0 + t
    return half_x * one_t, 0.5 * one_t + (half_x * (1.0 - t * t)) * (c + (3.0 * k * c) * x2)


def _rot_half(x):
    width = x.shape[1]
    lane = lax.broadcasted_iota(jnp.int32, x.shape, 1) % HEAD_DIM
    back = pltpu.roll(x, ROT_DIM // 2, 1)
    fwd = pltpu.roll(x, width - ROT_DIM // 2, 1)
    return jnp.where(lane < ROT_DIM // 2, -fwd, jnp.where(lane < ROT_DIM, back, 0.0))


def _split3(z):
    h0 = z.astype(BF16)
    r1 = z - h0.astype(F32)
    h1 = r1.astype(BF16)
    h2 = (r1 - h1.astype(F32)).astype(BF16)
    return h0, h1, h2


def _head_sum(z):
    width = z.shape[1]
    a = lax.broadcasted_iota(jnp.int32, (width, width), 0) // HEAD_DIM
    b = lax.broadcasted_iota(jnp.int32, (width, width), 1) // HEAD_DIM
    ones = jnp.where(a == b, 1.0, 0.0).astype(BF16)
    out = None
    for part in _split3(z):
        t = lax.dot_general(part, ones, (((1,), (0,)), ((), ())), preferred_element_type=F32)
        out = t if out is None else out + t
    return out


MATMUL_VMEM_BUDGET = 40 * 1024 * 1024


def _matmul_tiles(m, n, k, out_bytes):
    tn = n if n <= 1024 else (1280 if n % 1280 == 0 and n % 1024 else 1024)
    tk = k if k <= 1024 else (1280 if k % 1280 == 0 and k % 1024 else 1024)
    tm = m
    while tm > 256:
        blocks = 2 * 2 * (tm * tk + tk * tn) + 2 * out_bytes * tm * tn + (4 * tm * tn if k > tk else 0)
        if blocks <= MATMUL_VMEM_BUDGET and m % tm == 0:
            break
        tm //= 2
    return tm, tn, tk


def _matmul(a, b, *, mode, out_dtype, name, cols=None):
    wide = D_MODEL if cols is not None else None
    if mode == "nn":
        (m, k), (_, n) = a.shape, (b.shape if cols is None else (b.shape[1], cols[1] * wide))
    elif mode == "nt":
        (m, k), (n, _) = a.shape, (b.shape if cols is None else (b.shape[1], cols[1] * wide))
    else:
        (k, m), (_, n) = a.shape, b.shape
    tm, tn, tk = _matmul_tiles(m, n, k, jnp.dtype(out_dtype).itemsize)
    assert m % tm == 0 and n % tn == 0 and k % tk == 0, (name, m, n, k)
    nk = k // tk
    if mode == "nn":
        a_spec = pl.BlockSpec((tm, tk), lambda i, j, kk: (i, kk))
        b_spec = pl.BlockSpec((tk, tn), lambda i, j, kk: (kk, j))
        if cols is not None:
            assert tn == wide
            b_spec = pl.BlockSpec((None, tk, tn), lambda i, j, kk: (cols[0] + j, kk, 0))
        ca, cb = 1, 0
    elif mode == "nt":
        a_spec = pl.BlockSpec((tm, tk), lambda i, j, kk: (i, kk))
        b_spec = pl.BlockSpec((tn, tk), lambda i, j, kk: (j, kk))
        if cols is not None:
            assert tk == wide
            b_spec = pl.BlockSpec((None, tn, tk), lambda i, j, kk: (cols[0] + kk, j, 0))
        ca, cb = 1, 1
    else:
        a_spec = pl.BlockSpec((tk, tm), lambda i, j, kk: (kk, i))
        b_spec = pl.BlockSpec((tk, tn), lambda i, j, kk: (kk, j))
        ca, cb = 0, 0

    def body(a_ref, b_ref, o_ref, *acc):
        kk = pl.program_id(2)
        part = lax.dot_general(a_ref[...], b_ref[...], (((ca,), (cb,)), ((), ())), preferred_element_type=F32)
        if nk == 1:
            o_ref[...] = part.astype(o_ref.dtype)
            return
        acc_ref, = acc

        @pl.when(kk == 0)
        def _():
            acc_ref[...] = part

        @pl.when(kk > 0)
        def _():
            acc_ref[...] += part

        @pl.when(kk == nk - 1)
        def _():
            o_ref[...] = acc_ref[...].astype(o_ref.dtype)

    return pl.pallas_call(
        body, name=name, grid=(m // tm, n // tn, nk),
        in_specs=[a_spec, b_spec], out_specs=pl.BlockSpec((tm, tn), lambda i, j, kk: (i, j)),
        out_shape=jax.ShapeDtypeStruct((m, n), out_dtype),
        scratch_shapes=[pltpu.VMEM((tm, tn), F32)] if nk > 1 else [],
        compiler_params=_params("parallel", "parallel", "arbitrary"),
    )(a, b)


def _matmul_by_destination(a, b_lo, b_hi, *, name, tm=1024, tk=2048):
    (k, m), half = a.shape, N_DEV // 2
    assert b_lo.shape == b_hi.shape == (k, half * D_MODEL) and m % tm == 0 and k % tk == 0, name
    nk = k // tk

    def body(a_ref, lo_ref, hi_ref, o_ref, acc_ref):
        j, kk = pl.program_id(1), pl.program_id(2)

        def step(b_ref):
            part = lax.dot_general(a_ref[...], b_ref[...], (((0,), (0,)), ((), ())), preferred_element_type=F32)

            @pl.when(kk == 0)
            def _():
                acc_ref[...] = part

            @pl.when(kk > 0)
            def _():
                acc_ref[...] += part

        pl.when(j < half)(lambda: step(lo_ref))
        pl.when(j >= half)(lambda: step(hi_ref))

        @pl.when(kk == nk - 1)
        def _():
            o_ref[...] = acc_ref[...].astype(o_ref.dtype)

    lo_spec = pl.BlockSpec((tk, D_MODEL), lambda i, j, kk: (jnp.where(j < half, kk, nk - 1), jnp.minimum(j, half - 1)))
    hi_spec = pl.BlockSpec((tk, D_MODEL), lambda i, j, kk: (jnp.where(j >= half, kk, 0), jnp.maximum(j - half, 0)))
    return pl.pallas_call(
        body, name=name, grid=(m // tm, N_DEV, nk),
        in_specs=[pl.BlockSpec((tk, tm), lambda i, j, kk: (kk, i)), lo_spec, hi_spec],
        out_specs=pl.BlockSpec((None, tm, D_MODEL), lambda i, j, kk: (j, i, 0)),
        out_shape=jax.ShapeDtypeStruct((N_DEV, m, D_MODEL), BF16),
        scratch_shapes=[pltpu.VMEM((tm, D_MODEL), F32)],
        compiler_params=_params("parallel", "parallel", "arbitrary"),
    )(a, b_lo, b_hi)


LANES = 128


def _residues_to_rows(ref, scr, d):
    w = ref.shape[1] // d
    n = ref.shape[0]
    for r in range(d):
        for c in range(w // LANES):
            scr[c, pl.ds(r, n, stride=d), :] = ref[:, r * w + c * LANES:r * w + (c + 1) * LANES].astype(F32)
    return jnp.concatenate([scr[c] for c in range(w // LANES)], axis=1)


def _rows_to_residues(val, ref, scr, d):
    w = val.shape[1]
    n = ref.shape[0]
    for c in range(w // LANES):
        scr[c] = val[:, c * LANES:(c + 1) * LANES].astype(F32)
    for r in range(d):
        for c in range(w // LANES):
            ref[:, r * w + c * LANES:r * w + (c + 1) * LANES] = scr[c, pl.ds(r, n, stride=d), :].astype(ref.dtype)


HEAD_ROWS = 0


def _head_rows_block(z):
    width = z.shape[1]
    a = lax.broadcasted_iota(jnp.int32, (width, LANES), 0)
    b = lax.broadcasted_iota(jnp.int32, (width, LANES), 1)
    pick = jnp.where(a == b * HEAD_DIM, 1.0, 0.0).astype(BF16)
    out = None
    for part in _split3(z):
        t = lax.dot_general(part, pick, (((1,), (0,)), ((), ())), preferred_element_type=F32)
        out = t if out is None else out + t
    return out.T[:N_HEADS, :]


def _rowwise(fn, rows, consts, out_rows, out_acc, *, ts, name):
    rows = [tuple(r) + (1,) * (4 - len(r)) for r in rows]
    out_rows = [tuple(o) + (1,) * (3 - len(o)) for o in out_rows]
    s = rows[0][0].shape[0] * rows[0][3]
    assert s % ts == 0, (name, s, ts)
    n_rows, n_in = len(rows), len(rows) + len(consts)
    n_row = len(out_rows)
    n_out = n_row + len(out_acc)
    moved = [(idx, w) for idx, (_, w, _, d) in enumerate(rows) if d > 1]
    moved += [(n_rows + idx, w) for idx, (w, _, d) in enumerate(out_rows) if d > 1]

    def body(*refs):
        scratch = dict(zip([key for key, _ in moved], refs[n_in + n_out:]))
        vals = []
        for idx, r in enumerate(refs[:n_in]):
            d = rows[idx][3] if idx < n_rows else 1
            vals.append(r[...] if d == 1 else _residues_to_rows(r, scratch[idx], d))
        row_vals, acc_vals = fn(*vals)
        for idx, (r, v) in enumerate(zip(refs[n_in:n_in + n_row], row_vals)):
            d = out_rows[idx][2]
            if d == 1:
                r[...] = v.astype(r.dtype)
            elif d == HEAD_ROWS:
                r[...] = _head_rows_block(v)
            else:
                _rows_to_residues(v, r, scratch[n_rows + idx], d)
        first = pl.program_id(0) == 0
        for r, v in zip(refs[n_in + n_row:n_in + n_out], acc_vals):
            @pl.when(first)
            def _(r=r, v=v):
                r[...] = v

            @pl.when(jnp.logical_not(first))
            def _(r=r, v=v):
                r[...] += v

    in_specs = [pl.BlockSpec((ts // d, d * w), functools.partial(lambda i, cb: (i, cb), cb=cb)) for _, w, cb, d in rows]
    in_specs += [pl.BlockSpec(c.shape, lambda i: (0, 0)) for c in consts]
    out_specs = [pl.BlockSpec((N_HEADS, ts), lambda i: (0, i)) if d == HEAD_ROWS else
                 pl.BlockSpec((ts // d, d * w), lambda i: (i, 0)) for w, _, d in out_rows]
    out_specs += [pl.BlockSpec(sh, lambda i: (0, 0)) for sh in out_acc]
    out_shape = [jax.ShapeDtypeStruct((N_HEADS, s) if d == HEAD_ROWS else (s // d, d * w), dt) for w, dt, d in out_rows]
    out_shape += [jax.ShapeDtypeStruct(sh, F32) for sh in out_acc]
    outs = pl.pallas_call(
        body, name=name, grid=(s // ts,), in_specs=in_specs, out_specs=out_specs, out_shape=out_shape,
        scratch_shapes=[pltpu.VMEM((w // LANES, ts, LANES), F32) for _, w in moved],
        compiler_params=_params("arbitrary" if out_acc else "parallel"),
    )(*[a for a, _, _, _ in rows], *consts)
    return outs[:n_row], outs[n_row:]


def _full(a, d=1):
    return (a, a.shape[1] // d, 0, d)


def _gate_fn(zu, zv, vg, vb, ws0, ws1, ws2, ws3, bfull, ga):
    u = _gelu_erf(zu)
    vn = _layernorm(_gelu_erf(zv), vg, vb)
    p = lax.broadcasted_iota(jnp.int32, (CHUNK, CHUNK), 0)
    q = lax.broadcasted_iota(jnp.int32, (CHUNK, CHUNK), 1)
    tril = jnp.where(q <= p, 1.0, 0.0)
    group = lax.broadcasted_iota(jnp.int32, (1, A_WIDTH), 1) // CHUNK
    sg = bfull
    for g, w in enumerate((ws0, ws1, ws2, ws3)):
        sg = sg + _bdot(w * tril, jnp.where(group == g, vn, 0.0))
    return _rms(u * sg, ga)


def _bias_reduce(dbf, name):
    def body(x_ref, o_ref):
        lane = lax.broadcasted_iota(jnp.int32, (CHUNK, CHUNK), 1)
        out = jnp.zeros((CHUNK, CHUNK), F32)
        for g in range(A_GROUPS):
            out = jnp.where(lane == g, jnp.sum(x_ref[:, g * CHUNK:(g + 1) * CHUNK], axis=1, keepdims=True), out)
        o_ref[...] = out

    return pl.pallas_call(body, name=name, out_shape=jax.ShapeDtypeStruct((CHUNK, CHUNK), F32))(dbf)


def _pair_mask(hh):
    lane = lax.broadcasted_iota(jnp.int32, (1, 2 * HEAD_DIM), 1)
    return (lane >= HEAD_DIM * hh) & (lane < HEAD_DIM * (hh + 1))


def _lane_pick(x2, lm):
    return jnp.max(jnp.where(lm, x2, -jnp.inf), axis=1, keepdims=True)


def _attn_specs(nb):
    cur = pl.BlockSpec((BAND, B_WIDTH), lambda r, j: (j, r))
    prev = pl.BlockSpec((BAND, B_WIDTH), lambda r, j: (jnp.maximum(j - 1, 0), r))
    nxt = pl.BlockSpec((BAND, B_WIDTH), lambda r, j: (jnp.minimum(j + 1, nb - 1), r))
    return cur, prev, nxt


def _band_valid_q(j):
    row = lax.broadcasted_iota(jnp.int32, (BAND, 2 * BAND), 0)
    col = lax.broadcasted_iota(jnp.int32, (BAND, 2 * BAND), 1)
    return (col >= row) & (col <= row + BAND) & ((col >= BAND) | (j > 0))


def _attn_fwd(q, k, v, d, name):
    nb = q.shape[0] // BAND
    cur, prev, _ = _attn_specs(nb)

    def body(q_ref, kc_ref, kp_ref, vc_ref, vp_ref, o_ref, l_ref):
        valid = _band_valid_q(pl.program_id(1))
        for hp in range(N_HEADS // 2):
            sl = slice(2 * HEAD_DIM * hp, 2 * HEAD_DIM * (hp + 1))
            q2 = q_ref[:, sl]
            k2 = jnp.concatenate([kp_ref[:, sl], kc_ref[:, sl]], axis=0)
            v2 = jnp.concatenate([vp_ref[:, sl], vc_ref[:, sl]], axis=0)
            o2 = jnp.zeros((BAND, 2 * HEAD_DIM), F32)
            l2 = jnp.zeros((BAND, 2 * HEAD_DIM), F32)
            for hh in range(2):
                lm = _pair_mask(hh)
                s = _dotg(jnp.where(lm, q2, jnp.zeros_like(q2)), k2, 1, 1) * (HEAD_DIM ** -0.5)
                s = jnp.where(valid, s, NEG_INF)
                m = jnp.max(s, axis=1, keepdims=True)
                p = jnp.exp(s - m)
                den = jnp.sum(p, axis=1, keepdims=True)
                o = _dotg(p, v2, 1, 0) / den
                o2 = jnp.where(lm, o, o2)
                l2 = jnp.where(lm, m + jnp.log(den), l2)
            o_ref[:, sl] = o2
            l_ref[:, sl] = l2

    return pl.pallas_call(
        body, name=name, grid=(d, nb), in_specs=[cur, cur, prev, cur, prev], out_specs=[cur, cur],
        out_shape=[jax.ShapeDtypeStruct(q.shape, F32), jax.ShapeDtypeStruct(q.shape, F32)],
        compiler_params=_params("parallel", "parallel"),
    )(q, k, k, v, v)


def _attn_bwd_q(q, k, v, do, lse, delta, d, name):
    nb = q.shape[0] // BAND
    cur, prev, _ = _attn_specs(nb)

    def body(q_ref, kc_ref, kp_ref, vc_ref, vp_ref, do_ref, l_ref, dl_ref, dq_ref):
        valid = _band_valid_q(pl.program_id(1))
        for hp in range(N_HEADS // 2):
            sl = slice(2 * HEAD_DIM * hp, 2 * HEAD_DIM * (hp + 1))
            q2, do2, l2, dl2 = q_ref[:, sl], do_ref[:, sl], l_ref[:, sl], dl_ref[:, sl]
            k2 = jnp.concatenate([kp_ref[:, sl], kc_ref[:, sl]], axis=0)
            v2 = jnp.concatenate([vp_ref[:, sl], vc_ref[:, sl]], axis=0)
            dq2 = jnp.zeros((BAND, 2 * HEAD_DIM), F32)
            for hh in range(2):
                lm = _pair_mask(hh)
                s = _dotg(jnp.where(lm, q2, jnp.zeros_like(q2)), k2, 1, 1) * (HEAD_DIM ** -0.5)
                s = jnp.where(valid, s, NEG_INF)
                p = jnp.exp(s - _lane_pick(l2, lm))
                dp = _dotg(jnp.where(lm, do2, jnp.zeros_like(do2)), v2, 1, 1)
                ds = p * (dp - _lane_pick(dl2, lm))
                dq2 = jnp.where(lm, _dotg(ds, k2, 1, 0) * (HEAD_DIM ** -0.5), dq2)
            dq_ref[:, sl] = dq2

    return pl.pallas_call(
        body, name=name, grid=(d, nb), in_specs=[cur, cur, prev, cur, prev, cur, cur, cur], out_specs=cur,
        out_shape=jax.ShapeDtypeStruct(q.shape, F32),
        compiler_params=_params("parallel", "parallel"),
    )(q, k, k, v, v, do, lse, delta)


def _attn_bwd_kv(q, k, v, do, lse_t, delta_t, d, name):
    nb = q.shape[0] // BAND
    cur, _, nxt = _attn_specs(nb)
    t_cur = pl.BlockSpec((1, N_HEADS, BAND), lambda r, j: (r, 0, j))
    t_nxt = pl.BlockSpec((1, N_HEADS, BAND), lambda r, j: (r, 0, jnp.minimum(j + 1, nb - 1)))

    def body(k_ref, v_ref, qc_ref, qn_ref, doc_ref, don_ref, lc_ref, ln_ref, dlc_ref, dln_ref, dk_ref, dv_ref):
        j = pl.program_id(1)
        row = lax.broadcasted_iota(jnp.int32, (BAND, 2 * BAND), 0)
        col = lax.broadcasted_iota(jnp.int32, (BAND, 2 * BAND), 1)
        valid = (col >= row) & (col <= row + BAND) & ((col < BAND) | (j < nb - 1))
        for hp in range(N_HEADS // 2):
            sl = slice(2 * HEAD_DIM * hp, 2 * HEAD_DIM * (hp + 1))
            k2, v2 = k_ref[:, sl], v_ref[:, sl]
            q2 = jnp.concatenate([qc_ref[:, sl], qn_ref[:, sl]], axis=0)
            do2 = jnp.concatenate([doc_ref[:, sl], don_ref[:, sl]], axis=0)
            dk2 = jnp.zeros((BAND, 2 * HEAD_DIM), F32)
            dv2 = jnp.zeros((BAND, 2 * HEAD_DIM), F32)
            for hh in range(2):
                h = 2 * hp + hh
                lm = _pair_mask(hh)
                lse_row = jnp.concatenate([lc_ref[0, h:h + 1, :], ln_ref[0, h:h + 1, :]], axis=1)
                dl_row = jnp.concatenate([dlc_ref[0, h:h + 1, :], dln_ref[0, h:h + 1, :]], axis=1)
                st = _dotg(jnp.where(lm, k2, jnp.zeros_like(k2)), q2, 1, 1) * (HEAD_DIM ** -0.5)
                st = jnp.where(valid, st, NEG_INF)
                pt = jnp.exp(st - lse_row)
                dv2 = jnp.where(lm, _dotg(pt, do2, 1, 0), dv2)
                dpt = _dotg(jnp.where(lm, v2, jnp.zeros_like(v2)), do2, 1, 1)
                dst = pt * (dpt - dl_row)
                dk2 = jnp.where(lm, _dotg(dst, q2, 1, 0) * (HEAD_DIM ** -0.5), dk2)
            dk_ref[:, sl] = dk2
            dv_ref[:, sl] = dv2

    return pl.pallas_call(
        body, name=name, grid=(d, nb),
        in_specs=[cur, cur, cur, nxt, cur, nxt, t_cur, t_nxt, t_cur, t_nxt], out_specs=[cur, cur],
        out_shape=[jax.ShapeDtypeStruct(q.shape, F32), jax.ShapeDtypeStruct(q.shape, F32)],
        compiler_params=_params("parallel", "parallel"),
    )(k, v, q, q, do, do, lse_t, lse_t, delta_t, delta_t)


def _spread_rows(a, d):
    return a.reshape(N_HEADS, a.shape[1] // d, d).transpose(2, 0, 1)


FF_TS = 256
FF_TC = 512
HALO = 8


def _conv3(ext, w, b):
    return b + w[0:1, :] * pltpu.roll(ext, 2, 0) + w[1:2, :] * pltpu.roll(ext, 1, 0) + w[2:3, :] * ext


def _ffn_specs(s, cols_first):
    nrb = s // FF_TS
    per = FF_TS // HALO

    def mk(block, fn):
        if cols_first:
            return pl.BlockSpec(block, lambda j, i: fn(i, j))
        return pl.BlockSpec(block, lambda i, j: fn(i, j))

    specs = types.SimpleNamespace(
        nrb=nrb, ncb=D_FF // FF_TC,
        row=mk((FF_TS, FF_TC), lambda i, j: (i, j)),
        before=mk((HALO, FF_TC), lambda i, j: (jnp.maximum(i * per - 1, 0), j)),
        after=mk((HALO, FF_TC), lambda i, j: (jnp.minimum((i + 1) * per, nrb * per - 1), j)),
        w=mk((3, FF_TC), lambda i, j: (0, j)),
        b=mk((1, FF_TC), lambda i, j: (0, j)),
        acc=mk((HALO, FF_TC), lambda i, j: (0, j)),
    )
    return specs


def _geglu_fwd(ug, uv, wg, wv, bg, bv, name):
    s = ug.shape[0]
    sp = _ffn_specs(s, False)

    def body(ug_ref, uv_ref, hg_ref, hv_ref, wg_ref, wv_ref, bg_ref, bv_ref, y_ref):
        keep = jnp.where(pl.program_id(0) > 0, 1.0, 0.0)
        eg = jnp.concatenate([hg_ref[...] * keep, ug_ref[...]], axis=0)
        ev = jnp.concatenate([hv_ref[...] * keep, uv_ref[...]], axis=0)
        gate = _conv3(eg, wg_ref[...], bg_ref[...])[HALO:, :]
        val = _conv3(ev, wv_ref[...], bv_ref[...])[HALO:, :]
        y_ref[...] = (_gelu_tanh(gate) * val).astype(y_ref.dtype)

    return pl.pallas_call(
        body, name=name, grid=(sp.nrb, sp.ncb),
        in_specs=[sp.row, sp.row, sp.before, sp.before, sp.w, sp.w, sp.b, sp.b],
        out_specs=sp.row, out_shape=jax.ShapeDtypeStruct((s, D_FF), BF16),
        compiler_params=_params("parallel", "parallel"),
    )(ug, uv, ug, uv, wg, wv, bg, bv)


def _geglu_bwd(ug, uv, dy, wg, wv, bg, bv, name):
    s = ug.shape[0]
    sp = _ffn_specs(s, True)
    nrb = sp.nrb
    rows = FF_TS + 2 * HALO

    def body(ug_ref, uv_ref, hg_ref, hv_ref, ng_ref, nv_ref, dy_ref, dyn_ref, wg_ref, wv_ref, bg_ref, bv_ref,
             dug_ref, duv_ref, dwg_ref, dwv_ref):
        i = pl.program_id(1)
        keep_top = jnp.where(i > 0, 1.0, 0.0)
        keep_bot = jnp.where(i < nrb - 1, 1.0, 0.0)
        eg = jnp.concatenate([hg_ref[...] * keep_top, ug_ref[...], ng_ref[...]], axis=0)
        ev = jnp.concatenate([hv_ref[...] * keep_top, uv_ref[...], nv_ref[...]], axis=0)
        dye = jnp.concatenate([jnp.zeros((HALO, FF_TC), F32), dy_ref[...], dyn_ref[...] * keep_bot], axis=0)
        wg_, wv_ = wg_ref[...], wv_ref[...]
        gate = _conv3(eg, wg_, bg_ref[...])
        val = _conv3(ev, wv_, bv_ref[...])
        act, slope = _gelu_tanh_and_slope(gate)
        dgate = (dye * val) * slope
        dval = dye * act

        lo, hi = HALO, HALO + FF_TS

        def back(dc, e, w, du_ref):
            up1 = pltpu.roll(dc, rows - 1, 0)
            up2 = pltpu.roll(dc, rows - 2, 0)
            du_ref[...] = (w[2:3, :] * dc + w[1:2, :] * up1 + w[0:1, :] * up2)[lo:hi, :].astype(du_ref.dtype)
            p1, p2 = up1 * e, up2 * e
            colsum = lambda p: jnp.sum(p[lo:hi, :], axis=0, keepdims=True)
            row = lambda p, t: p[t:t + 1, :]
            d_w1 = colsum(p1) + row(p1, lo - 1) - row(p1, hi - 1)
            d_w0 = colsum(p2) + row(p2, lo - 2) + row(p2, lo - 1) - row(p2, hi - 2) - row(p2, hi - 1)
            return jnp.concatenate([d_w0, d_w1, colsum(dc * e), colsum(dc), jnp.zeros((HALO - 4, FF_TC), F32)], axis=0)

        sg = back(dgate, eg, wg_, dug_ref)
        sv = back(dval, ev, wv_, duv_ref)

        @pl.when(i == 0)
        def _():
            dwg_ref[...] = sg
            dwv_ref[...] = sv

        @pl.when(i > 0)
        def _():
            dwg_ref[...] += sg
            dwv_ref[...] += sv

    return pl.pallas_call(
        body, name=name, grid=(sp.ncb, nrb),
        in_specs=[sp.row, sp.row, sp.before, sp.before, sp.after, sp.after, sp.row, sp.after,
                  sp.w, sp.w, sp.b, sp.b],
        out_specs=[sp.row, sp.row, sp.acc, sp.acc],
        out_shape=[jax.ShapeDtypeStruct((s, D_FF), BF16), jax.ShapeDtypeStruct((s, D_FF), BF16),
                   jax.ShapeDtypeStruct((HALO, D_FF), F32), jax.ShapeDtypeStruct((HALO, D_FF), F32)],
        compiler_params=_params("parallel", "arbitrary"),
    )(ug, uv, ug, uv, ug, uv, dy, dy, wg, wv, bg, bv)


def _rope_tables(s):
    inv = ROPE_THETA ** (-jnp.arange(0, ROT_DIM, 2, dtype=F32) / ROT_DIM)
    ang = jnp.arange(s, dtype=F32)[:, None] * inv[None, :]
    cos8, sin8 = jnp.cos(ang), jnp.sin(ang)
    rest = HEAD_DIM - ROT_DIM
    cos_h = jnp.concatenate([cos8, cos8, jnp.ones((s, rest), F32)], axis=1)
    sin_h = jnp.concatenate([sin8, sin8, jnp.zeros((s, rest), F32)], axis=1)
    return jnp.tile(cos_h, (1, N_HEADS)), jnp.tile(sin_h, (1, N_HEADS))


def _layer_fwd(x, w, cos, sin):
    sv = types.SimpleNamespace(x=x)
    (sv.h1,), _ = _rowwise(lambda xb, g: ((_rms(xb, g),), ()), [_full(x)], [w.g_pre], [(D_MODEL, BF16)], [],
                           ts=512, name="pre_mix_norm")
    sv.proj = _matmul(sv.h1, w.big("w_in", sv.h1), mode="nn", out_dtype=F32, name="proj")

    gate_consts = [w.vg, w.vb, *w.ws, w.bfull, w.ga]
    (na,), _ = _rowwise(lambda *a: ((_gate_fn(*a),), ()), [(sv.proj, A_WIDTH, 0), (sv.proj, A_WIDTH, 1)], gate_consts,
                        [(A_WIDTH, BF16)], [], ts=CHUNK, name="gate_fwd")

    def rope_fn(qr, kr, vr, cs, sn):
        return (qr * cs + _rot_half(qr) * sn, kr * cs + _rot_half(kr) * sn, vr), ()

    def rope_all(qr, kr, vr, cs, sn):
        return rope_fn(qr, kr, vr, cs, sn)[0] * len(DILATIONS), ()

    qkv, _ = _rowwise(
        rope_all, [(sv.proj, B_WIDTH, 2), (sv.proj, B_WIDTH, 3), (sv.proj, B_WIDTH, 4), _full(cos), _full(sin)], [],
        [(B_WIDTH, BF16, d) for d in DILATIONS for _ in range(3)], [], ts=512, name="rope_fwd")
    sv.qkv = {d: qkv[3 * i:3 * i + 3] for i, d in enumerate(DILATIONS)}

    branch = []
    for d in DILATIONS:
        o, l = _attn_fwd(*sv.qkv[d], d, name=f"attn_fwd_d{d}")
        branch += [_full(o, d), _full(l, d)]

    def combine_fn(o1, l1, o2, l2, o3, l3, nab, gb):
        m = jnp.maximum(jnp.maximum(l1, l2), l3)
        e1, e2, e3 = jnp.exp(l1 - m), jnp.exp(l2 - m), jnp.exp(l3 - m)
        den = e1 + e2 + e3
        ob = (e1 / den) * o1 + (e2 / den) * o2 + (e3 / den) * o3
        mixed = jnp.concatenate([nab, _rms(ob, gb).astype(BF16)], axis=1)
        lse = m + jnp.log(den)
        return (mixed, ob, lse) + (lse,) * len(DILATIONS), ()

    (sv.mixed, sv.ob, sv.lse_rows, *lses), _ = _rowwise(
        combine_fn, branch + [_full(na)], [w.gb],
        [(D_MODEL, BF16), (B_WIDTH, F32), (B_WIDTH, F32, HEAD_ROWS)] + [(B_WIDTH, F32, d) for d in DILATIONS], [],
        ts=512, name="combine")
    sv.lse = dict(zip(DILATIONS, lses))
    sv.y = _matmul(sv.mixed, w.big("w_out", sv.mixed), mode="nn", out_dtype=F32, name="mix_out")

    def mid_fn(xb, yb, g1, g2):
        x1 = xb + _rms(yb, g1)
        return (x1, _rms(x1, g2)), ()

    (sv.x1, sv.h2), _ = _rowwise(mid_fn, [_full(x), _full(sv.y)], [w.g_pm, w.g_pf], [(D_MODEL, F32), (D_MODEL, BF16)], [],
                                 ts=512, name="post_mix_norm")
    w_up = w.big("w_up", sv.h2)
    sv.ug = _matmul(sv.h2, w_up, mode="nn", out_dtype=F32, name="ffn_up_gate", cols=(0, N_DEV // 2))
    sv.uv = _matmul(sv.h2, w_up, mode="nn", out_dtype=F32, name="ffn_up_val", cols=(N_DEV // 2, N_DEV // 2))
    conv_w = w.big("conv_w", sv.h2)
    sv.yff = _geglu_fwd(sv.ug, sv.uv, conv_w[:, :D_FF], conv_w[:, D_FF:], w.cb_g, w.cb_v, name="geglu_fwd")
    sv.f = _matmul(sv.yff, w.big("w_down", sv.yff), mode="nn", out_dtype=F32, name="ffn_down")
    (x2,), _ = _rowwise(lambda xb, fb, g: ((xb + _rms(fb, g),), ()), [_full(sv.x1), _full(sv.f)], [w.g_post],
                        [(D_MODEL, F32)], [], ts=512, name="post_ffn_norm")
    return x2, sv


def _layer_bwd(dx2, sv, w, cos, sin, emit):
    g = {}

    def post_fn(fb, dxb, gp):
        _, vjp = jax.vjp(_rms, fb, gp)
        df, dg = vjp(dxb)
        return (df,), (dg,)

    (df,), (g["post_ffn_norm"],) = _rowwise(post_fn, [_full(sv.f), _full(dx2)], [w.g_post], [(D_MODEL, BF16)],
                                            [(1, D_MODEL)], ts=512, name="post_ffn_norm_bwd")
    dyff = _matmul(df, w.big("w_down", df), mode="nt", out_dtype=F32, name="ffn_down_dx")
    big = {"w_down": _matmul(sv.yff, df, mode="tn", out_dtype=BF16, name="ffn_down_dw").reshape(N_DEV, -1, D_MODEL)}
    conv_w, w_up = w.big("conv_w", df), w.big("w_up", df)
    dug, duv, dwg, dwv = _geglu_bwd(sv.ug, sv.uv, dyff, conv_w[:, :D_FF], conv_w[:, D_FF:], w.cb_g, w.cb_v,
                                    name="geglu_bwd")
    big["conv_w"] = jnp.concatenate([dwg[0:3], dwv[0:3]], axis=1).reshape(3, N_DEV, D_MODEL).transpose(1, 0, 2)
    g["conv_b"] = jnp.concatenate([dwg[3], dwv[3]], axis=0)
    dh2a = _matmul(dug, w_up, mode="nt", out_dtype=F32, name="ffn_up_dx_gate", cols=(0, N_DEV // 2))
    dh2b = _matmul(duv, w_up, mode="nt", out_dtype=F32, name="ffn_up_dx_val", cols=(N_DEV // 2, N_DEV // 2))
    big["w_up"] = _matmul_by_destination(sv.h2, dug, duv, name="ffn_up_dw")
    g_pm = w.g_pm + emit(big)

    def mid_fn(x1b, yb, da, db, dxb, g1, g2):
        _, vjp2 = jax.vjp(_rms, x1b, g2)
        dx1h, dg2 = vjp2(da + db)
        dx1 = dxb + dx1h
        _, vjp1 = jax.vjp(_rms, yb, g1)
        dy, dg1 = vjp1(dx1)
        return (dx1, dy), (dg1, dg2)

    (dx1, dy), (g["post_mix_norm"], g["pre_ffn_norm"]) = _rowwise(
        mid_fn, [_full(sv.x1), _full(sv.y), _full(dh2a), _full(dh2b), _full(dx2)], [g_pm, w.g_pf],
        [(D_MODEL, F32), (D_MODEL, BF16)], [(1, D_MODEL), (1, D_MODEL)], ts=256, name="post_mix_norm_bwd")
    dmixed = _matmul(dy, w.big("w_out", dy), mode="nt", out_dtype=F32, name="mix_out_dx")
    big = {"w_out": _matmul(sv.mixed, dy, mode="tn", out_dtype=BF16, name="mix_out_dw").reshape(N_DEV, -1, D_MODEL)}

    def attn_out_fn(obb, dmb, gb):
        _, vjp = jax.vjp(_rms, obb, gb)
        do, dgb = vjp(dmb)
        delta = _head_sum(do * obb)
        return (delta,) + (do,) * len(DILATIONS) + (delta,) * len(DILATIONS), (dgb,)

    (delta_rows, *outs), (g["out_norm_b"],) = _rowwise(
        attn_out_fn, [_full(sv.ob), (dmixed, B_WIDTH, 1)], [w.gb],
        [(B_WIDTH, F32, HEAD_ROWS)] + [(B_WIDTH, BF16, d) for d in DILATIONS] + [(B_WIDTH, F32, d) for d in DILATIONS],
        [(1, B_WIDTH)], ts=512, name="attn_out_bwd")
    do = dict(zip(DILATIONS, outs[:len(DILATIONS)]))
    delta = dict(zip(DILATIONS, outs[len(DILATIONS):]))
    parts = {"q": [], "k": [], "v": []}
    for d in DILATIONS:
        qv, kv, vv = sv.qkv[d]
        dq = _attn_bwd_q(qv, kv, vv, do[d], sv.lse[d], delta[d], d, name=f"attn_bwd_q_d{d}")
        dk, dv = _attn_bwd_kv(qv, kv, vv, do[d], _spread_rows(sv.lse_rows, d), _spread_rows(delta_rows, d), d,
                              name=f"attn_bwd_kv_d{d}")
        parts["q"].append(_full(dq, d))
        parts["k"].append(_full(dk, d))
        parts["v"].append(_full(dv, d))

    def rope_bwd_fn(q1, q2, q3, k1, k2, k3, v1, v2, v3, cs, sn):
        def back(t):
            return t * cs - _rot_half(t * sn)
        return (jnp.concatenate([back(q1 + q2 + q3), back(k1 + k2 + k3), v1 + v2 + v3], axis=1),), ()

    (dzb,), _ = _rowwise(rope_bwd_fn, parts["q"] + parts["k"] + parts["v"] + [_full(cos), _full(sin)], [],
                         [(3 * B_WIDTH, BF16)], [], ts=256, name="rope_bwd")

    gate_consts = [w.vg, w.vb, *w.ws, w.bfull, w.ga]

    def gate_bwd_fn(zu, zv, dna, *consts):
        _, vjp = jax.vjp(_gate_fn, zu, zv, *consts)
        grads = vjp(dna)
        return (jnp.concatenate([grads[0], grads[1]], axis=1),), tuple(grads[2:])

    (dza,), gsmall = _rowwise(
        gate_bwd_fn, [(sv.proj, A_WIDTH, 0), (sv.proj, A_WIDTH, 1), (dmixed, A_WIDTH, 0)], gate_consts,
        [(2 * A_WIDTH, BF16)], [c.shape for c in gate_consts], ts=CHUNK, name="gate_bwd")
    g["v_norm_g"], g["v_norm_b"] = gsmall[0], gsmall[1]
    g["w_spatial"] = jnp.stack(gsmall[2:6])
    g["b_spatial"] = _bias_reduce(gsmall[6], name="bias_reduce")[:, :A_GROUPS].T
    g["out_norm_a"] = gsmall[7]

    dproj = jnp.concatenate([dza, dzb], axis=1)
    dh1 = _matmul(dproj, w.big("w_in", dproj), mode="nt", out_dtype=F32, name="proj_dx")
    dw_in = _matmul(sv.h1, dproj, mode="tn", out_dtype=BF16, name="proj_dw")
    big["w_in"] = dw_in.reshape(D_MODEL, N_DEV, IN_COLS // N_DEV).transpose(1, 0, 2)
    g_pre = w.g_pre + emit(big)

    def pre_fn(xb, dhb, dxb, gp):
        _, vjp = jax.vjp(_rms, xb, gp)
        dxh, dg = vjp(dhb)
        return (dxb + dxh,), (dg,)

    (dx,), (g["pre_mix_norm"],) = _rowwise(pre_fn, [_full(sv.x), _full(dh1), _full(dx1)], [g_pre], [(D_MODEL, F32)],
                                           [(1, D_MODEL)], ts=512, name="pre_mix_norm_bwd")
    return dx, g


def _layer_weights(l, full, small):
    row = lambda a: a[l].reshape(1, -1)
    return types.SimpleNamespace(
        big=functools.partial(full, l),
        g_pre=row(small["pre_mix_norm"]), vg=row(small["v_norm_g"]), vb=row(small["v_norm_b"]),
        ws=[small["w_spatial"][l, gi] for gi in range(A_GROUPS)],
        bfull=jnp.repeat(small["b_spatial"][l].T, CHUNK, axis=1),
        ga=row(small["out_norm_a"]), gb=row(small["out_norm_b"]),
        g_pm=row(small["post_mix_norm"]), g_pf=row(small["pre_ffn_norm"]),
        cb_g=small["conv_b"][l][:D_FF].reshape(1, -1), cb_v=small["conv_b"][l][D_FF:].reshape(1, -1),
        g_post=row(small["post_ffn_norm"]))


def _local_step(x, target, full, small, emit, started):
    s = x.shape[0]
    cos, sin = _rope_tables(s)
    ws = [_layer_weights(l, full, small) for l in range(N_LAYERS)]
    ws[0].g_pre = ws[0].g_pre + started
    saved = []
    h = x
    for l in range(N_LAYERS):
        h, sv = _layer_fwd(h, ws[l], cos, sin)
        saved.append(sv)

    def loss_fn(yb, tb):
        diff = yb - tb
        return (diff * (1.0 / D_MODEL),), (jnp.sum(diff * diff, axis=0, keepdims=True),)

    (dh,), (sq,) = _rowwise(loss_fn, [_full(h), _full(target)], [], [(D_MODEL, F32)], [(1, D_MODEL)], ts=512, name="loss")
    loss = 0.5 * jnp.sum(sq) * (1.0 / D_MODEL)
    grads = [None] * N_LAYERS
    for l in reversed(range(N_LAYERS)):
        dh, grads[l] = _layer_bwd(dh, saved[l], ws[l], cos, sin, functools.partial(emit, l))
    return loss, dh, grads


def _place():
    return lax.axis_index("x"), lax.axis_index("y"), lax.axis_index("c")


def _all_gather(x, after, name):
    def body(x_ref, after_ref, out_ref, send_sems, recv_sems, local_sem):
        mx, my, mc = _place()
        me, sibling = (mx, my, mc), (mx, my, 1 - mc)
        chips = [(1 - mx, my), (mx, 1 - my), (1 - mx, 1 - my)]

        def slot(px, py, pc):
            return out_ref.at[4 * px + 2 * py + pc]

        def copy(k, block, to, src=None):
            return pltpu.make_async_remote_copy(
                src_ref=slot(*block) if src is None else src, dst_ref=slot(*block),
                send_sem=send_sems.at[k], recv_sem=recv_sems.at[k], device_id=to, device_id_type=MESH_ID)

        mine = pltpu.make_async_copy(x_ref, slot(*me), local_sem)
        mine.start()
        first = [copy(0, me, sibling, src=x_ref)]
        first += [copy(1 + j, me, (*chip, mc), src=x_ref) for j, chip in enumerate(chips)]
        for cp in first:
            cp.start()
        passed = [copy(4 + j, (*chip, mc), sibling) for j, chip in enumerate(chips)]
        for j, chip in enumerate(chips):
            copy(1 + j, (*chip, mc), me).wait_recv()
            passed[j].start()
        copy(0, sibling, me).wait_recv()
        for j, chip in enumerate(chips):
            copy(4 + j, (*chip, 1 - mc), me).wait_recv()
        for cp in first + passed:
            cp.wait_send()
        mine.wait()

    return pl.pallas_call(
        body, name=name, out_shape=jax.ShapeDtypeStruct((N_DEV,) + x.shape, x.dtype), in_specs=[ANY, ANY],
        out_specs=ANY,
        scratch_shapes=[pltpu.SemaphoreType.DMA((7,)), pltpu.SemaphoreType.DMA((7,)), pltpu.SemaphoreType.DMA],
    )(x, after)


FLIPS = ((1, 0, 0), (0, 1, 0), (1, 1, 0), (0, 0, 1), (1, 0, 1), (0, 1, 1), (1, 1, 1))
HBM_SPEC = pl.BlockSpec(memory_space=pltpu.HBM)
SEM_SPEC = pl.BlockSpec(memory_space=pltpu.SEMAPHORE)
SPLIT_COPY = pltpu.CompilerParams(has_side_effects=pltpu.SideEffectType.DATAFLOW_SIDE_EFFECTING)


def _peers():
    mx, my, mc = _place()
    out = []
    for fx, fy, fc in FLIPS:
        px, py, pc = (1 - mx if fx else mx), (1 - my if fy else my), (1 - mc if fc else mc)
        out.append(((px, py, pc), 4 * px + 2 * py + pc))
    return out


def _flat_copies(scatter, src_refs, land_refs, send_sems, recv_sems):
    mx, my, mc = _place()
    me = 4 * mx + 2 * my + mc
    n = len(src_refs)
    copies = []
    for t in range(n):
        for i, (peer, number) in enumerate(_peers()):
            copies.append(pltpu.make_async_remote_copy(
                src_ref=src_refs[t].at[number] if scatter else src_refs[t],
                dst_ref=land_refs[t].at[i] if scatter else land_refs[t].at[me],
                send_sem=send_sems.at[t * len(FLIPS) + i], recv_sem=recv_sems.at[t * len(FLIPS) + i],
                device_id=peer, device_id_type=MESH_ID))
    return copies


def _flat_start(arrays, scatter, name):
    n = len(arrays)
    slots = len(FLIPS) if scatter else N_DEV
    lands = [lax.empty((slots,) + (a.shape[1:] if scatter else a.shape), a.dtype) for a in arrays]

    def body(*refs):
        src, land, (send_sems, recv_sems), token = refs[:n], refs[n:2 * n], refs[2 * n:2 * n + 2], refs[-1]
        for cp in _flat_copies(scatter, src, land, send_sems, recv_sems):
            cp.start()
        token[...] = jnp.zeros_like(token)

    hbm = [pltpu.HBM(a.shape, a.dtype) for a in arrays] + [pltpu.HBM(a.shape, a.dtype) for a in lands]
    sems = pltpu.SemaphoreType.DMA((n * len(FLIPS),))
    outs = pl.pallas_call(
        body, name=name, out_shape=(sems, sems, *hbm, jax.ShapeDtypeStruct((8, 128), F32)),
        in_specs=[HBM_SPEC] * (2 * n),
        out_specs=(SEM_SPEC, SEM_SPEC, *([HBM_SPEC] * (2 * n)), pl.BlockSpec(memory_space=pltpu.VMEM)),
        input_output_aliases={t: 2 + t for t in range(2 * n)}, compiler_params=SPLIT_COPY,
    )(*[pltpu.with_memory_space_constraint(a, pltpu.HBM) for a in (*arrays, *lands)])
    return types.SimpleNamespace(sems=outs[:2], thru=outs[2:2 + 2 * n], scatter=scatter, n=n), outs[-1][0:1, 0:1]


def _flat_wait(handle, after, name):
    n = handle.n

    def body(*refs):
        src, land, (send_sems, recv_sems) = refs[:n], refs[n:2 * n], refs[2 * n:2 * n + 2]
        for cp in _flat_copies(handle.scatter, src, land, send_sems, recv_sems):
            cp.wait_send()
            cp.wait_recv()

    outs = pl.pallas_call(
        body, name=name, out_shape=tuple(pltpu.HBM(a.shape, a.dtype) for a in handle.thru),
        in_specs=[HBM_SPEC] * (2 * n) + [SEM_SPEC, SEM_SPEC, ANY], out_specs=tuple([HBM_SPEC] * (2 * n)),
        input_output_aliases={t: t for t in range(2 * n)}, compiler_params=SPLIT_COPY,
    )(*handle.thru, *handle.sems, after)
    return outs[:n], outs[n:]


def _adamw(w, g, m, v):
    m2 = ADAM_B1 * m + (1.0 - ADAM_B1) * g
    v2 = ADAM_B2 * v + (1.0 - ADAM_B2) * (g * g)
    m_hat = m2 / (1.0 - ADAM_B1 ** ADAM_STEP)
    v_hat = v2 / (1.0 - ADAM_B2 ** ADAM_STEP)
    return -ADAM_LR * (m_hat / (jnp.sqrt(v_hat) + ADAM_EPS) + ADAM_WD * w), m2, v2


def _adamw_sharded(me, mine, landed, w, m, v, tr, name):
    _, r, c = w.shape
    nt = r // tr
    assert r % tr == 0 and len(mine) == len(landed) == N_LAYERS == 2, name
    per_layer = 1 + len(FLIPS)

    def body(me_ref, *refs):
        terms, (w_ref, m_ref, v_ref), outs = refs[:2 * per_layer], refs[2 * per_layer:2 * per_layer + 3], refs[-4:]
        layer = pl.program_id(0)

        def total(group):
            g = group[0][0].astype(F32)
            for t in group[1:]:
                g = g + t[0].astype(F32)
            return g

        g = jnp.where(layer == 0, total(terms[:per_layer]), total(terms[per_layer:]))
        d, m2, v2 = _adamw(w_ref[0], g, m_ref[0], v_ref[0])
        for o, val in zip(outs, (g, d, m2, v2)):
            o[0] = val

    def held(l):
        return lambda layer, i: jnp.where(layer == l, i, nt - 1 if l == 0 else 0)

    in_specs = []
    for l in range(N_LAYERS):
        rows = held(l)
        in_specs.append(pl.BlockSpec((1, tr, c), functools.partial(lambda layer, i, me_ref, rows: (me_ref[0], rows(layer, i), 0), rows=rows)))
        for k in range(len(FLIPS)):
            in_specs.append(pl.BlockSpec(
                (1, tr, c), functools.partial(lambda layer, i, me_ref, rows, k: (k, rows(layer, i), 0), rows=rows, k=k)))
    tile = pl.BlockSpec((1, tr, c), lambda layer, i, me_ref: (layer, i, 0))
    operands = []
    for l in range(N_LAYERS):
        operands += [mine[l]] + [landed[l]] * len(FLIPS)
    return pl.pallas_call(
        body, name=name, out_shape=[jax.ShapeDtypeStruct(w.shape, F32)] * 4,
        grid_spec=pltpu.PrefetchScalarGridSpec(
            num_scalar_prefetch=1, grid=(N_LAYERS, nt), in_specs=in_specs + [tile] * 3, out_specs=[tile] * 4),
        compiler_params=_params("arbitrary", "arbitrary"),
    )(me, *operands, w, m, v)


def _adamw_replicated(parts, w, m, v, name):
    def body(p_ref, w_ref, m_ref, v_ref, g_ref, d_ref, m2_ref, v2_ref):
        g = p_ref[0]
        for j in range(1, N_DEV):
            g = g + p_ref[j]
        d, m2, v2 = _adamw(w_ref[...], g, m_ref[...], v_ref[...])
        g_ref[...], d_ref[...], m2_ref[...], v2_ref[...] = g, d, m2, v2

    return pl.pallas_call(body, name=name, out_shape=[jax.ShapeDtypeStruct(w.shape, F32)] * 4,
                          compiler_params=pltpu.CompilerParams(vmem_limit_bytes=VMEM_LIMIT_BYTES))(parts, w, m, v)


def _pack_small(vals):
    flat = jnp.concatenate([vals[n].reshape(-1) for n in SMALL_NAMES])
    return jnp.concatenate([flat, jnp.zeros((SMALL_ROWS * D_MODEL - flat.shape[0],), F32)]).reshape(SMALL_ROWS, D_MODEL)


def _unpack_small(packed, shapes):
    flat, out, at = packed.reshape(-1), {}, 0
    for n in SMALL_NAMES:
        size = math.prod(shapes[n])
        out[n] = flat[at:at + size].reshape(shapes[n])
        at += size
    return out


GATHER_GROUPS = ((0, ("w_in",)), (0, ("w_out", "w_up", "conv_w", "w_down")),
                 (1, ("w_in", "w_out", "w_up", "conv_w", "w_down")))
ADAMW_TILE_ROWS = {"w_in": 512, "w_out": 128, "w_up": 256, "w_down": 256, "conv_w": 3}


def _assemble(name, land):
    if name == "w_in":
        return land.transpose(1, 0, 2).reshape(D_MODEL, IN_COLS)
    if name == "conv_w":
        return land.transpose(1, 0, 2).reshape(3, 2 * D_FF)
    if name == "w_up":
        return land
    return land.reshape(-1, D_MODEL)


def _start_gathers(wts, me):
    started, groups = jnp.zeros((1, 1), F32), []
    for gi, (l, names) in enumerate(GATHER_GROUPS):
        blocks = [wts[n][l] if n == "conv_w" else wts[n][l].astype(BF16) for n in names]
        handle, token = _flat_start(blocks, False, name=f"gather_start_{gi}")
        groups.append(types.SimpleNamespace(layer=l, names=names, blocks=blocks, handle=handle, got=None, index=gi))
        started = started + token

    def fetch(l, name, after):
        grp = next(gr for gr in groups if gr.layer == l and name in gr.names)
        if grp.got is None:
            lands = _flat_wait(grp.handle, after, name=f"gather_wait_{grp.index}")[1]
            grp.got = {}
            for n, blk, land in zip(grp.names, grp.blocks, lands):
                own = (me,) + (0,) * blk.ndim
                grp.got[n] = _assemble(n, lax.dynamic_update_slice(land, blk[None], own))
        return grp.got[name]

    return fetch, started


def kernel(x, pre_mix_norm, w_in, v_norm_g, v_norm_b, w_spatial, b_spatial, out_norm_a, out_norm_b, w_out, post_mix_norm, pre_ffn_norm, w_up, conv_w, conv_b, w_down, post_ffn_norm, loss_target, m_pre_mix_norm, m_w_in, m_v_norm_g, m_v_norm_b, m_w_spatial, m_b_spatial, m_out_norm_a, m_out_norm_b, m_w_out, m_post_mix_norm, m_pre_ffn_norm, m_w_up, m_conv_w, m_conv_b, m_w_down, m_post_ffn_norm, v_pre_mix_norm, v_w_in, v_v_norm_g, v_v_norm_b, v_w_spatial, v_b_spatial, v_out_norm_a, v_out_norm_b, v_w_out, v_post_mix_norm, v_pre_ffn_norm, v_w_up, v_conv_w, v_conv_b, v_w_down, v_post_ffn_norm):
    wts = dict(zip(WEIGHT_NAMES, (pre_mix_norm, w_in, v_norm_g, v_norm_b, w_spatial, b_spatial, out_norm_a, out_norm_b,
                                  w_out, post_mix_norm, pre_ffn_norm, w_up, conv_w, conv_b, w_down, post_ffn_norm)))
    mom1 = dict(zip(WEIGHT_NAMES, (m_pre_mix_norm, m_w_in, m_v_norm_g, m_v_norm_b, m_w_spatial, m_b_spatial, m_out_norm_a,
                                   m_out_norm_b, m_w_out, m_post_mix_norm, m_pre_ffn_norm, m_w_up, m_conv_w, m_conv_b,
                                   m_w_down, m_post_ffn_norm)))
    mom2 = dict(zip(WEIGHT_NAMES, (v_pre_mix_norm, v_w_in, v_v_norm_g, v_v_norm_b, v_w_spatial, v_b_spatial, v_out_norm_a,
                                   v_out_norm_b, v_w_out, v_post_mix_norm, v_pre_ffn_norm, v_w_up, v_conv_w, v_conv_b,
                                   v_w_down, v_post_ffn_norm)))
    mx, my, mc = _place()
    me = 4 * mx + 2 * my + mc

    fetch, started = _start_gathers(wts, me)
    scatters = []

    def emit(l, blocks):
        names = tuple(blocks)
        handle, token = _flat_start([blocks[n] for n in names], True, name=f"scatter_start_{l}_{len(scatters) % 2}")
        scatters.append((l, names, handle))
        return token

    loss_local, dx, grads = _local_step(x[0], loss_target[0], fetch, wts, emit, started)
    loss = lax.psum(loss_local, AXES)

    me_arr = jnp.reshape(me, (1,)).astype(jnp.int32)
    big_out = [{}, {}, {}, {}]

    def finish(group, after):
        mine, landed = {}, {}
        for l, names, handle in scatters:
            if names == group:
                sent, lands = _flat_wait(handle, after, name=f"scatter_wait_{l}_{'_'.join(names)}")
                for n, a, b in zip(names, sent, lands):
                    mine[l, n], landed[l, n] = a, b
        for n in group:
            res = _adamw_sharded(me_arr, [mine[l, n] for l in range(N_LAYERS)], [landed[l, n] for l in range(N_LAYERS)],
                                 wts[n], mom1[n], mom2[n], ADAMW_TILE_ROWS[n], name=f"adamw_{n}")
            for kind in range(4):
                big_out[kind][n] = res[kind]
        return res[0]

    early, late = scatters[0][1], scatters[1][1]
    done_early = finish(early, dx)
    small_grads = {n: jnp.stack([g[n].reshape(wts[n].shape[1:]) for g in grads]) for n in SMALL_NAMES}
    everyone = _all_gather(_pack_small(small_grads), done_early, name="gather_small_grads")
    small = [_pack_small({n: t[n] for n in SMALL_NAMES}) for t in (wts, mom1, mom2)]
    small_out = _adamw_replicated(everyone, *small, name="adamw_replicated")
    finish(late, small_out[0])
    small_shapes = {n: wts[n].shape for n in SMALL_NAMES}
    small_out = [_unpack_small(o, small_shapes) for o in small_out]

    outs = [loss, dx[None]]
    for kind in range(4):
        outs += [big_out[kind][n] if n in BIG_NAMES else small_out[kind][n] for n in WEIGHT_NAMES]
    return tuple(outs)
```

```python
import functools
import math
import types

import jax
import jax.numpy as jnp
from jax import lax
from jax.experimental import pallas as pl
from jax.experimental.pallas import tpu as pltpu

F32 = jnp.float32
BF16 = jnp.bfloat16

D_MODEL = 1024
A_WIDTH = 512
A_GROUPS = 4
CHUNK = 128
B_WIDTH = 512
HEAD_DIM = 64
N_HEADS = B_WIDTH // HEAD_DIM
ROT_DIM = 16
ROPE_THETA = 500000.0
BAND = 128
DILATIONS = (1, 4, 16)
IN_COLS = 2560
D_FF = 4096
EPS = 1e-6
NEG_INF = -1e30
N_DEV = 8
N_LAYERS = 2

ADAM_LR = 0.001
ADAM_B1 = 0.9
ADAM_B2 = 0.999
ADAM_EPS = 1e-08
ADAM_WD = 0.01
ADAM_STEP = 10

VMEM_LIMIT_BYTES = 56 * 1024 * 1024
MESH_ID = pl.DeviceIdType.MESH
ANY = pl.BlockSpec(memory_space=pl.ANY)
AXES = ("x", "y", "c")

WEIGHT_NAMES = ("pre_mix_norm", "w_in", "v_norm_g", "v_norm_b", "w_spatial", "b_spatial", "out_norm_a", "out_norm_b",
                "w_out", "post_mix_norm", "pre_ffn_norm", "w_up", "conv_w", "conv_b", "w_down", "post_ffn_norm")
BIG_NAMES = ("w_in", "w_out", "w_up", "w_down", "conv_w")
SMALL_NAMES = tuple(n for n in WEIGHT_NAMES if n not in BIG_NAMES)

PACK_ROWS = {"w_in": 640, "w_out": 256, "w_up": 2048, "w_down": 1024, "conv_w": 6}
CONV_W_PAD = 2
PACKED_F32_ROWS = 4096
PACKED_BF16_ROWS = 3984
SMALL_ROWS = 160


def _params(*sem):
    return pltpu.CompilerParams(dimension_semantics=sem, vmem_limit_bytes=VMEM_LIMIT_BYTES)


def _dotg(a, b, ca, cb):
    return lax.dot_general(a.astype(BF16), b.astype(BF16), (((ca,), (cb,)), ((), ())), preferred_element_type=F32)


@jax.custom_vjp
def _bdot(a, b):
    return _dotg(a, b, 1, 0)


def _bdot_fwd(a, b):
    return _dotg(a, b, 1, 0), (a, b)


def _bdot_bwd(res, g):
    a, b = res
    return _dotg(g, b, 1, 1), _dotg(a, g, 0, 0)


_bdot.defvjp(_bdot_fwd, _bdot_bwd)


def _rms(x, g):
    return x * lax.rsqrt(jnp.mean(x * x, axis=-1, keepdims=True) + EPS) * g


def _layernorm(x, g, b):
    mu = jnp.mean(x, axis=-1, keepdims=True)
    xc = x - mu
    return xc * lax.rsqrt(jnp.mean(xc * xc, axis=-1, keepdims=True) + EPS) * g + b


def _gelu_erf(x):
    return x * (lax.erf(x * (1.0 / math.sqrt(2.0))) + 1.0) * 0.5


def _gelu_tanh(x):
    c = math.sqrt(2.0 / math.pi)
    return 0.5 * x * (1.0 + jnp.tanh(c * (x + 0.044715 * (x * x * x))))


def _gelu_tanh_and_slope(x):
    c, k = math.sqrt(2.0 / math.pi), 0.044715
    x2 = x * x
    t = jnp.tanh(c * (x + k * (x2 * x)))
    half_x, one_t = 0.5 * x, 1.0 + t
    return half_x * one_t, 0.5 * one_t + (half_x * (1.0 - t * t)) * (c + (3.0 * k * c) * x2)


def _rot_half(x):
    width = x.shape[1]
    lane = lax.broadcasted_iota(jnp.int32, x.shape, 1) % HEAD_DIM
    back = pltpu.roll(x, ROT_DIM // 2, 1)
    fwd = pltpu.roll(x, width - ROT_DIM // 2, 1)
    return jnp.where(lane < ROT_DIM // 2, -fwd, jnp.where(lane < ROT_DIM, back, 0.0))


def _split3(z):
    h0 = z.astype(BF16)
    r1 = z - h0.astype(F32)
    h1 = r1.astype(BF16)
    h2 = (r1 - h1.astype(F32)).astype(BF16)
    return h0, h1, h2


def _head_sum(z):
    width = z.shape[1]
    a = lax.broadcasted_iota(jnp.int32, (width, width), 0) // HEAD_DIM
    b = lax.broadcasted_iota(jnp.int32, (width, width), 1) // HEAD_DIM
    ones = jnp.where(a == b, 1.0, 0.0).astype(BF16)
    out = None
    for part in _split3(z):
        t = lax.dot_general(part, ones, (((1,), (0,)), ((), ())), preferred_element_type=F32)
        out = t if out is None else out + t
    return out


MATMUL_VMEM_BUDGET = 40 * 1024 * 1024


def _matmul_tiles(m, n, k, out_bytes):
    tn = n if n <= 1024 else (1280 if n % 1280 == 0 and n % 1024 else 1024)
    tk = k if k <= 1024 else (1280 if k % 1280 == 0 and k % 1024 else 1024)
    tm = m
    while tm > 256:
        blocks = 2 * 2 * (tm * tk + tk * tn) + 2 * out_bytes * tm * tn + (4 * tm * tn if k > tk else 0)
        if blocks <= MATMUL_VMEM_BUDGET and m % tm == 0:
            break
        tm //= 2
    return tm, tn, tk


def _matmul(a, b, *, mode, out_dtype, name, cols=None):
    wide = D_MODEL if cols is not None else None
    if mode == "nn":
        (m, k), (_, n) = a.shape, (b.shape if cols is None else (b.shape[1], cols[1] * wide))
    elif mode == "nt":
        (m, k), (n, _) = a.shape, (b.shape if cols is None else (b.shape[1], cols[1] * wide))
    else:
        (k, m), (_, n) = a.shape, b.shape
    tm, tn, tk = _matmul_tiles(m, n, k, jnp.dtype(out_dtype).itemsize)
    assert m % tm == 0 and n % tn == 0 and k % tk == 0, (name, m, n, k)
    nk = k // tk
    if mode == "nn":
        a_spec = pl.BlockSpec((tm, tk), lambda i, j, kk: (i, kk))
        b_spec = pl.BlockSpec((tk, tn), lambda i, j, kk: (kk, j))
        if cols is not None:
            assert tn == wide
            b_spec = pl.BlockSpec((None, tk, tn), lambda i, j, kk: (cols[0] + j, kk, 0))
        ca, cb = 1, 0
    elif mode == "nt":
        a_spec = pl.BlockSpec((tm, tk), lambda i, j, kk: (i, kk))
        b_spec = pl.BlockSpec((tn, tk), lambda i, j, kk: (j, kk))
        if cols is not None:
            assert tk == wide
            b_spec = pl.BlockSpec((None, tn, tk), lambda i, j, kk: (cols[0] + kk, j, 0))
        ca, cb = 1, 1
    else:
        a_spec = pl.BlockSpec((tk, tm), lambda i, j, kk: (kk, i))
        b_spec = pl.BlockSpec((tk, tn), lambda i, j, kk: (kk, j))
        ca, cb = 0, 0

    def body(a_ref, b_ref, o_ref, *acc):
        kk = pl.program_id(2)
        part = lax.dot_general(a_ref[...], b_ref[...], (((ca,), (cb,)), ((), ())), preferred_element_type=F32)
        if nk == 1:
            o_ref[...] = part.astype(o_ref.dtype)
            return
        acc_ref, = acc

        @pl.when(kk == 0)
        def _():
            acc_ref[...] = part

        @pl.when(kk > 0)
        def _():
            acc_ref[...] += part

        @pl.when(kk == nk - 1)
        def _():
            o_ref[...] = acc_ref[...].astype(o_ref.dtype)

    return pl.pallas_call(
        body, name=name, grid=(m // tm, n // tn, nk),
        in_specs=[a_spec, b_spec], out_specs=pl.BlockSpec((tm, tn), lambda i, j, kk: (i, j)),
        out_shape=jax.ShapeDtypeStruct((m, n), out_dtype),
        scratch_shapes=[pltpu.VMEM((tm, tn), F32)] if nk > 1 else [],
        compiler_params=_params("parallel", "parallel", "arbitrary"),
    )(a, b)


def _matmul_by_destination(a, b_lo, b_hi, *, name, tm=1024, tk=2048):
    (k, m), half = a.shape, N_DEV // 2
    assert b_lo.shape == b_hi.shape == (k, half * D_MODEL) and m % tm == 0 and k % tk == 0, name
    nk = k // tk

    def body(a_ref, lo_ref, hi_ref, o_ref, acc_ref):
        j, kk = pl.program_id(1), pl.program_id(2)

        def step(b_ref):
            part = lax.dot_general(a_ref[...], b_ref[...], (((0,), (0,)), ((), ())), preferred_element_type=F32)

            @pl.when(kk == 0)
            def _():
                acc_ref[...] = part

            @pl.when(kk > 0)
            def _():
                acc_ref[...] += part

        pl.when(j < half)(lambda: step(lo_ref))
        pl.when(j >= half)(lambda: step(hi_ref))

        @pl.when(kk == nk - 1)
        def _():
            o_ref[...] = acc_ref[...].astype(o_ref.dtype)

    lo_spec = pl.BlockSpec((tk, D_MODEL), lambda i, j, kk: (jnp.where(j < half, kk, nk - 1), jnp.minimum(j, half - 1)))
    hi_spec = pl.BlockSpec((tk, D_MODEL), lambda i, j, kk: (jnp.where(j >= half, kk, 0), jnp.maximum(j - half, 0)))
    return pl.pallas_call(
        body, name=name, grid=(m // tm, N_DEV, nk),
        in_specs=[pl.BlockSpec((tk, tm), lambda i, j, kk: (kk, i)), lo_spec, hi_spec],
        out_specs=pl.BlockSpec((None, tm, D_MODEL), lambda i, j, kk: (j, i, 0)),
        out_shape=jax.ShapeDtypeStruct((N_DEV, m, D_MODEL), BF16),
        scratch_shapes=[pltpu.VMEM((tm, D_MODEL), F32)],
        compiler_params=_params("parallel", "parallel", "arbitrary"),
    )(a, b_lo, b_hi)


LANES = 128


def _residues_to_rows(ref, scr, d):
    w = ref.shape[1] // d
    n = ref.shape[0]
    for r in range(d):
        for c in range(w // LANES):
            scr[c, pl.ds(r, n, stride=d), :] = ref[:, r * w + c * LANES:r * w + (c + 1) * LANES].astype(F32)
    return jnp.concatenate([scr[c] for c in range(w // LANES)], axis=1)


def _rows_to_residues(val, ref, scr, d):
    w = val.shape[1]
    n = ref.shape[0]
    for c in range(w // LANES):
        scr[c] = val[:, c * LANES:(c + 1) * LANES].astype(F32)
    for r in range(d):
        for c in range(w // LANES):
            ref[:, r * w + c * LANES:r * w + (c + 1) * LANES] = scr[c, pl.ds(r, n, stride=d), :].astype(ref.dtype)


HEAD_ROWS = 0


def _head_rows_block(z):
    width = z.shape[1]
    a = lax.broadcasted_iota(jnp.int32, (width, LANES), 0)
    b = lax.broadcasted_iota(jnp.int32, (width, LANES), 1)
    pick = jnp.where(a == b * HEAD_DIM, 1.0, 0.0).astype(BF16)
    out = None
    for part in _split3(z):
        t = lax.dot_general(part, pick, (((1,), (0,)), ((), ())), preferred_element_type=F32)
        out = t if out is None else out + t
    return out.T[:N_HEADS, :]


def _rowwise(fn, rows, consts, out_rows, out_acc, *, ts, name):
    rows = [tuple(r) + (1,) * (4 - len(r)) for r in rows]
    out_rows = [tuple(o) + (1,) * (3 - len(o)) for o in out_rows]
    s = rows[0][0].shape[0] * rows[0][3]
    assert s % ts == 0, (name, s, ts)
    n_rows, n_in = len(rows), len(rows) + len(consts)
    n_row = len(out_rows)
    n_out = n_row + len(out_acc)
    moved = [(idx, w) for idx, (_, w, _, d) in enumerate(rows) if d > 1]
    moved += [(n_rows + idx, w) for idx, (w, _, d) in enumerate(out_rows) if d > 1]

    def body(*refs):
        scratch = dict(zip([key for key, _ in moved], refs[n_in + n_out:]))
        vals = []
        for idx, r in enumerate(refs[:n_in]):
            d = rows[idx][3] if idx < n_rows else 1
            vals.append(r[...] if d == 1 else _residues_to_rows(r, scratch[idx], d))
        row_vals, acc_vals = fn(*vals)
        for idx, (r, v) in enumerate(zip(refs[n_in:n_in + n_row], row_vals)):
            d = out_rows[idx][2]
            if d == 1:
                r[...] = v.astype(r.dtype)
            elif d == HEAD_ROWS:
                r[...] = _head_rows_block(v)
            else:
                _rows_to_residues(v, r, scratch[n_rows + idx], d)
        first = pl.program_id(0) == 0
        for r, v in zip(refs[n_in + n_row:n_in + n_out], acc_vals):
            @pl.when(first)
            def _(r=r, v=v):
                r[...] = v

            @pl.when(jnp.logical_not(first))
            def _(r=r, v=v):
                r[...] += v

    in_specs = [pl.BlockSpec((ts // d, d * w), functools.partial(lambda i, cb: (i, cb), cb=cb)) for _, w, cb, d in rows]
    in_specs += [pl.BlockSpec(c.shape, lambda i: (0, 0)) for c in consts]
    out_specs = [pl.BlockSpec((N_HEADS, ts), lambda i: (0, i)) if d == HEAD_ROWS else
                 pl.BlockSpec((ts // d, d * w), lambda i: (i, 0)) for w, _, d in out_rows]
    out_specs += [pl.BlockSpec(sh, lambda i: (0, 0)) for sh in out_acc]
    out_shape = [jax.ShapeDtypeStruct((N_HEADS, s) if d == HEAD_ROWS else (s // d, d * w), dt) for w, dt, d in out_rows]
    out_shape += [jax.ShapeDtypeStruct(sh, F32) for sh in out_acc]
    outs = pl.pallas_call(
        body, name=name, grid=(s // ts,), in_specs=in_specs, out_specs=out_specs, out_shape=out_shape,
        scratch_shapes=[pltpu.VMEM((w // LANES, ts, LANES), F32) for _, w in moved],
        compiler_params=_params("arbitrary" if out_acc else "parallel"),
    )(*[a for a, _, _, _ in rows], *consts)
    return outs[:n_row], outs[n_row:]


def _full(a, d=1):
    return (a, a.shape[1] // d, 0, d)


def _gate_fn(zu, zv, vg, vb, ws0, ws1, ws2, ws3, bfull, ga):
    u = _gelu_erf(zu)
    vn = _layernorm(_gelu_erf(zv), vg, vb)
    p = lax.broadcasted_iota(jnp.int32, (CHUNK, CHUNK), 0)
    q = lax.broadcasted_iota(jnp.int32, (CHUNK, CHUNK), 1)
    tril = jnp.where(q <= p, 1.0, 0.0)
    group = lax.broadcasted_iota(jnp.int32, (1, A_WIDTH), 1) // CHUNK
    sg = bfull
    for g, w in enumerate((ws0, ws1, ws2, ws3)):
        sg = sg + _bdot(w * tril, jnp.where(group == g, vn, 0.0))
    return _rms(u * sg, ga)


def _bias_reduce(dbf, name):
    def body(x_ref, o_ref):
        lane = lax.broadcasted_iota(jnp.int32, (CHUNK, CHUNK), 1)
        out = jnp.zeros((CHUNK, CHUNK), F32)
        for g in range(A_GROUPS):
            out = jnp.where(lane == g, jnp.sum(x_ref[:, g * CHUNK:(g + 1) * CHUNK], axis=1, keepdims=True), out)
        o_ref[...] = out

    return pl.pallas_call(body, name=name, out_shape=jax.ShapeDtypeStruct((CHUNK, CHUNK), F32))(dbf)


def _pair_mask(hh):
    lane = lax.broadcasted_iota(jnp.int32, (1, 2 * HEAD_DIM), 1)
    return (lane >= HEAD_DIM * hh) & (lane < HEAD_DIM * (hh + 1))


def _lane_pick(x2, lm):
    return jnp.max(jnp.where(lm, x2, -jnp.inf), axis=1, keepdims=True)


def _attn_specs(nb):
    cur = pl.BlockSpec((BAND, B_WIDTH), lambda r, j: (j, r))
    prev = pl.BlockSpec((BAND, B_WIDTH), lambda r, j: (jnp.maximum(j - 1, 0), r))
    nxt = pl.BlockSpec((BAND, B_WIDTH), lambda r, j: (jnp.minimum(j + 1, nb - 1), r))
    return cur, prev, nxt


def _band_valid_q(j):
    row = lax.broadcasted_iota(jnp.int32, (BAND, 2 * BAND), 0)
    col = lax.broadcasted_iota(jnp.int32, (BAND, 2 * BAND), 1)
    return (col >= row) & (col <= row + BAND) & ((col >= BAND) | (j > 0))


def _attn_fwd(q, k, v, d, name):
    nb = q.shape[0] // BAND
    cur, prev, _ = _attn_specs(nb)

    def body(q_ref, kc_ref, kp_ref, vc_ref, vp_ref, o_ref, l_ref):
        valid = _band_valid_q(pl.program_id(1))
        for hp in range(N_HEADS // 2):
            sl = slice(2 * HEAD_DIM * hp, 2 * HEAD_DIM * (hp + 1))
            q2 = q_ref[:, sl]
            k2 = jnp.concatenate([kp_ref[:, sl], kc_ref[:, sl]], axis=0)
            v2 = jnp.concatenate([vp_ref[:, sl], vc_ref[:, sl]], axis=0)
            o2 = jnp.zeros((BAND, 2 * HEAD_DIM), F32)
            l2 = jnp.zeros((BAND, 2 * HEAD_DIM), F32)
            for hh in range(2):
                lm = _pair_mask(hh)
                s = _dotg(jnp.where(lm, q2, jnp.zeros_like(q2)), k2, 1, 1) * (HEAD_DIM ** -0.5)
                s = jnp.where(valid, s, NEG_INF)
                m = jnp.max(s, axis=1, keepdims=True)
                p = jnp.exp(s - m)
                den = jnp.sum(p, axis=1, keepdims=True)
                o = _dotg(p, v2, 1, 0) / den
                o2 = jnp.where(lm, o, o2)
                l2 = jnp.where(lm, m + jnp.log(den), l2)
            o_ref[:, sl] = o2
            l_ref[:, sl] = l2

    return pl.pallas_call(
        body, name=name, grid=(d, nb), in_specs=[cur, cur, prev, cur, prev], out_specs=[cur, cur],
        out_shape=[jax.ShapeDtypeStruct(q.shape, F32), jax.ShapeDtypeStruct(q.shape, F32)],
        compiler_params=_params("parallel", "parallel"),
    )(q, k, k, v, v)


def _attn_bwd_q(q, k, v, do, lse, delta, d, name):
    nb = q.shape[0] // BAND
    cur, prev, _ = _attn_specs(nb)

    def body(q_ref, kc_ref, kp_ref, vc_ref, vp_ref, do_ref, l_ref, dl_ref, dq_ref):
        valid = _band_valid_q(pl.program_id(1))
        for hp in range(N_HEADS // 2):
            sl = slice(2 * HEAD_DIM * hp, 2 * HEAD_DIM * (hp + 1))
            q2, do2, l2, dl2 = q_ref[:, sl], do_ref[:, sl], l_ref[:, sl], dl_ref[:, sl]
            k2 = jnp.concatenate([kp_ref[:, sl], kc_ref[:, sl]], axis=0)
            v2 = jnp.concatenate([vp_ref[:, sl], vc_ref[:, sl]], axis=0)
            dq2 = jnp.zeros((BAND, 2 * HEAD_DIM), F32)
            for hh in range(2):
                lm = _pair_mask(hh)
                s = _dotg(jnp.where(lm, q2, jnp.zeros_like(q2)), k2, 1, 1) * (HEAD_DIM ** -0.5)
                s = jnp.where(valid, s, NEG_INF)
                p = jnp.exp(s - _lane_pick(l2, lm))
                dp = _dotg(jnp.where(lm, do2, jnp.zeros_like(do2)), v2, 1, 1)
                ds = p * (dp - _lane_pick(dl2, lm))
                dq2 = jnp.where(lm, _dotg(ds, k2, 1, 0) * (HEAD_DIM ** -0.5), dq2)
            dq_ref[:, sl] = dq2

    return pl.pallas_call(
        body, name=name, grid=(d, nb), in_specs=[cur, cur, prev, cur, prev, cur, cur, cur], out_specs=cur,
        out_shape=jax.ShapeDtypeStruct(q.shape, F32),
        compiler_params=_params("parallel", "parallel"),
    )(q, k, k, v, v, do, lse, delta)


def _attn_bwd_kv(q, k, v, do, lse_t, delta_t, d, name):
    nb = q.shape[0] // BAND
    cur, _, nxt = _attn_specs(nb)
    t_cur = pl.BlockSpec((1, N_HEADS, BAND), lambda r, j: (r, 0, j))
    t_nxt = pl.BlockSpec((1, N_HEADS, BAND), lambda r, j: (r, 0, jnp.minimum(j + 1, nb - 1)))

    def body(k_ref, v_ref, qc_ref, qn_ref, doc_ref, don_ref, lc_ref, ln_ref, dlc_ref, dln_ref, dk_ref, dv_ref):
        j = pl.program_id(1)
        row = lax.broadcasted_iota(jnp.int32, (BAND, 2 * BAND), 0)
        col = lax.broadcasted_iota(jnp.int32, (BAND, 2 * BAND), 1)
        valid = (col >= row) & (col <= row + BAND) & ((col < BAND) | (j < nb - 1))
        for hp in range(N_HEADS // 2):
            sl = slice(2 * HEAD_DIM * hp, 2 * HEAD_DIM * (hp + 1))
            k2, v2 = k_ref[:, sl], v_ref[:, sl]
            q2 = jnp.concatenate([qc_ref[:, sl], qn_ref[:, sl]], axis=0)
            do2 = jnp.concatenate([doc_ref[:, sl], don_ref[:, sl]], axis=0)
            dk2 = jnp.zeros((BAND, 2 * HEAD_DIM), F32)
            dv2 = jnp.zeros((BAND, 2 * HEAD_DIM), F32)
            for hh in range(2):
                h = 2 * hp + hh
                lm = _pair_mask(hh)
                lse_row = jnp.concatenate([lc_ref[0, h:h + 1, :], ln_ref[0, h:h + 1, :]], axis=1)
                dl_row = jnp.concatenate([dlc_ref[0, h:h + 1, :], dln_ref[0, h:h + 1, :]], axis=1)
                st = _dotg(jnp.where(lm, k2, jnp.zeros_like(k2)), q2, 1, 1) * (HEAD_DIM ** -0.5)
                st = jnp.where(valid, st, NEG_INF)
                pt = jnp.exp(st - lse_row)
                dv2 = jnp.where(lm, _dotg(pt, do2, 1, 0), dv2)
                dpt = _dotg(jnp.where(lm, v2, jnp.zeros_like(v2)), do2, 1, 1)
                dst = pt * (dpt - dl_row)
                dk2 = jnp.where(lm, _dotg(dst, q2, 1, 0) * (HEAD_DIM ** -0.5), dk2)
            dk_ref[:, sl] = dk2
            dv_ref[:, sl] = dv2

    return pl.pallas_call(
        body, name=name, grid=(d, nb),
        in_specs=[cur, cur, cur, nxt, cur, nxt, t_cur, t_nxt, t_cur, t_nxt], out_specs=[cur, cur],
        out_shape=[jax.ShapeDtypeStruct(q.shape, F32), jax.ShapeDtypeStruct(q.shape, F32)],
        compiler_params=_params("parallel", "parallel"),
    )(k, v, q, q, do, do, lse_t, lse_t, delta_t, delta_t)


def _spread_rows(a, d):
    return a.reshape(N_HEADS, a.shape[1] // d, d).transpose(2, 0, 1)


FF_TS = 512
FF_TC = 1024
FF_SUB = 256
HALO = 8
HALO_BF16 = 16
UP_BLOCKS = D_MODEL // FF_TC


def _conv3(ext, w, b):
    return b + w[0:1, :] * pltpu.roll(ext, 2, 0) + w[1:2, :] * pltpu.roll(ext, 1, 0) + w[2:3, :] * ext


def _ffn_specs(s, cols_first):
    nrb = s // FF_TS
    per, per16 = FF_TS // HALO, FF_TS // HALO_BF16

    def mk(block, fn):
        if cols_first:
            return pl.BlockSpec(block, lambda j, i: fn(i, j))
        return pl.BlockSpec(block, lambda i, j: fn(i, j))

    specs = types.SimpleNamespace(
        nrb=nrb, ncb=D_FF // FF_TC,
        row=mk((FF_TS, FF_TC), lambda i, j: (i, j)),
        before=mk((HALO, FF_TC), lambda i, j: (jnp.maximum(i * per - 1, 0), j)),
        after=mk((HALO, FF_TC), lambda i, j: (jnp.minimum((i + 1) * per, nrb * per - 1), j)),
        w=mk((3, FF_TC), lambda i, j: (0, j)),
        b=mk((1, FF_TC), lambda i, j: (0, j)),
        part=mk((HALO, FF_TC), lambda i, j: (i, j)),
        act=mk((FF_TS, D_MODEL), lambda i, j: (i, 0)),
        act_before=mk((HALO_BF16, D_MODEL), lambda i, j: (jnp.maximum(i * per16 - 1, 0), 0)),
        act_after=mk((HALO_BF16, D_MODEL), lambda i, j: (jnp.minimum((i + 1) * per16, nrb * per16 - 1), 0)),
        up_gate=mk((None, D_MODEL, FF_TC), lambda i, j: (j // UP_BLOCKS, 0, j % UP_BLOCKS)),
        up_val=mk((None, D_MODEL, FF_TC), lambda i, j: (N_DEV // 2 + j // UP_BLOCKS, 0, j % UP_BLOCKS)),
        down=mk((FF_TC, D_MODEL), lambda i, j: (j, 0)),
    )
    return specs


def _ffn_up_geglu(h, w_up, wg, wv, bg, bv, name):
    s = h.shape[0]
    sp = _ffn_specs(s, True)

    def body(h_ref, hb_ref, ugw_ref, uvw_ref, wg_ref, wv_ref, bg_ref, bv_ref, ug_ref, uv_ref, y_ref):
        keep = jnp.where(pl.program_id(1) > 0, 1.0, 0.0).astype(BF16)
        hext = jnp.concatenate([hb_ref[...] * keep, h_ref[...]], axis=0)
        n_sub = FF_TC // FF_SUB
        cols = [slice(c * FF_SUB, (c + 1) * FF_SUB) for c in range(n_sub)]
        up = lambda c: (_dotg(hext, ugw_ref[:, cols[c]], 1, 0), _dotg(hext, uvw_ref[:, cols[c]], 1, 0))
        ahead = up(0)
        for c in range(n_sub):
            sl, (eg, ev) = cols[c], ahead
            if c + 1 < n_sub:
                ahead = up(c + 1)
            gate = _conv3(eg, wg_ref[:, sl], bg_ref[:, sl])[HALO_BF16:, :]
            val = _conv3(ev, wv_ref[:, sl], bv_ref[:, sl])[HALO_BF16:, :]
            ug_ref[:, sl] = eg[HALO_BF16:, :]
            uv_ref[:, sl] = ev[HALO_BF16:, :]
            y_ref[:, sl] = (_gelu_tanh(gate) * val).astype(y_ref.dtype)

    return pl.pallas_call(
        body, name=name, grid=(sp.ncb, sp.nrb),
        in_specs=[sp.act, sp.act_before, sp.up_gate, sp.up_val, sp.w, sp.w, sp.b, sp.b],
        out_specs=[sp.row, sp.row, sp.row],
        out_shape=[jax.ShapeDtypeStruct((s, D_FF), F32), jax.ShapeDtypeStruct((s, D_FF), F32),
                   jax.ShapeDtypeStruct((s, D_FF), BF16)],
        compiler_params=_params("parallel", "parallel"),
    )(h, h, w_up, w_up, wg, wv, bg, bv)


def _sum_parts(parts, name):
    n = parts.shape[0] // HALO

    def body(p_ref, o_ref):
        acc = p_ref[0:HALO, :]
        for t in range(1, n):
            acc = acc + p_ref[t * HALO:(t + 1) * HALO, :]
        o_ref[...] = acc

    return pl.pallas_call(body, name=name, out_shape=jax.ShapeDtypeStruct((HALO, parts.shape[1]), F32))(parts)


def _ffn_geglu_bwd(ug, uv, df, w_down, w_up, wg, wv, bg, bv, name):
    s = ug.shape[0]
    sp = _ffn_specs(s, False)
    nrb = sp.nrb
    rows = FF_TS + 2 * HALO
    lo, hi = HALO, HALO + FF_TS

    def body(ug_ref, uv_ref, hg_ref, hv_ref, ng_ref, nv_ref, df_ref, dfn_ref, dw_ref, ugw_ref, uvw_ref,
             wg_ref, wv_ref, bg_ref, bv_ref, dug_ref, duv_ref, dh_ref, dwg_ref, dwv_ref):
        i, j = pl.program_id(0), pl.program_id(1)
        keep_top = jnp.where(i > 0, 1.0, 0.0)
        keep_bot = jnp.where(i < nrb - 1, 1.0, 0.0).astype(BF16)
        dfe = jnp.concatenate([df_ref[...], dfn_ref[...] * keep_bot], axis=0)

        def back(dc, e, w, du_ref, sl):
            up1 = pltpu.roll(dc, rows - 1, 0)
            up2 = pltpu.roll(dc, rows - 2, 0)
            du = (w[2:3, :] * dc + w[1:2, :] * up1 + w[0:1, :] * up2)[lo:hi, :].astype(BF16)
            du_ref[:, sl] = du
            p1, p2 = up1 * e, up2 * e
            colsum = lambda p: jnp.sum(p[lo:hi, :], axis=0, keepdims=True)
            row = lambda p, t: p[t:t + 1, :]
            d_w1 = colsum(p1) + row(p1, lo - 1) - row(p1, hi - 1)
            d_w0 = colsum(p2) + row(p2, lo - 2) + row(p2, lo - 1) - row(p2, hi - 2) - row(p2, hi - 1)
            sums = [d_w0, d_w1, colsum(dc * e), colsum(dc), jnp.zeros((HALO - 4, FF_SUB), F32)]
            return du, jnp.concatenate(sums, axis=0)

        dh = None
        n_sub = FF_TC // FF_SUB
        cols = [slice(c * FF_SUB, (c + 1) * FF_SUB) for c in range(n_sub)]
        d_act = lambda c: _dotg(dfe, dw_ref[cols[c], :], 1, 1)[:FF_TS + HALO, :]
        ahead = d_act(0)
        for c in range(n_sub):
            sl, dy = cols[c], ahead
            if c + 1 < n_sub:
                ahead = d_act(c + 1)
            dye = jnp.concatenate([jnp.zeros((HALO, FF_SUB), F32), dy], axis=0)
            eg = jnp.concatenate([hg_ref[:, sl] * keep_top, ug_ref[:, sl], ng_ref[:, sl]], axis=0)
            ev = jnp.concatenate([hv_ref[:, sl] * keep_top, uv_ref[:, sl], nv_ref[:, sl]], axis=0)
            wg_, wv_ = wg_ref[:, sl], wv_ref[:, sl]
            gate = _conv3(eg, wg_, bg_ref[:, sl])
            val = _conv3(ev, wv_, bv_ref[:, sl])
            act, slope = _gelu_tanh_and_slope(gate)
            dug, dwg_ref[:, sl] = back((dye * val) * slope, eg, wg_, dug_ref, sl)
            duv, dwv_ref[:, sl] = back(dye * act, ev, wv_, duv_ref, sl)
            term = _dotg(dug, ugw_ref[:, sl], 1, 1) + _dotg(duv, uvw_ref[:, sl], 1, 1)
            dh = term if dh is None else dh + term

        @pl.when(j == 0)
        def _():
            dh_ref[...] = dh

        @pl.when(j > 0)
        def _():
            dh_ref[...] += dh

    parts = jax.ShapeDtypeStruct((nrb * HALO, D_FF), F32)
    dug, duv, dh, pg, pv = pl.pallas_call(
        body, name=name, grid=(nrb, sp.ncb),
        in_specs=[sp.row, sp.row, sp.before, sp.before, sp.after, sp.after, sp.act, sp.act_after, sp.down,
                  sp.up_gate, sp.up_val, sp.w, sp.w, sp.b, sp.b],
        out_specs=[sp.row, sp.row, sp.act, sp.part, sp.part],
        out_shape=[jax.ShapeDtypeStruct((s, D_FF), BF16), jax.ShapeDtypeStruct((s, D_FF), BF16),
                   jax.ShapeDtypeStruct((s, D_MODEL), F32), parts, parts],
        compiler_params=_params("parallel", "arbitrary"),
    )(ug, uv, ug, uv, ug, uv, df, df, w_down, w_up, w_up, wg, wv, bg, bv)
    return dug, duv, dh, _sum_parts(pg, name=name + "_sum_gate"), _sum_parts(pv, name=name + "_sum_val")


def _rope_tables(s):
    inv = ROPE_THETA ** (-jnp.arange(0, ROT_DIM, 2, dtype=F32) / ROT_DIM)
    ang = jnp.arange(s, dtype=F32)[:, None] * inv[None, :]
    cos8, sin8 = jnp.cos(ang), jnp.sin(ang)
    rest = HEAD_DIM - ROT_DIM
    cos_h = jnp.concatenate([cos8, cos8, jnp.ones((s, rest), F32)], axis=1)
    sin_h = jnp.concatenate([sin8, sin8, jnp.zeros((s, rest), F32)], axis=1)
    return jnp.tile(cos_h, (1, N_HEADS)), jnp.tile(sin_h, (1, N_HEADS))


def _layer_fwd(x, w, cos, sin):
    sv = types.SimpleNamespace(x=x)
    (sv.h1,), _ = _rowwise(lambda xb, g: ((_rms(xb, g),), ()), [_full(x)], [w.g_pre], [(D_MODEL, BF16)], [],
                           ts=512, name="pre_mix_norm")
    sv.proj = _matmul(sv.h1, w.big("w_in", sv.h1), mode="nn", out_dtype=F32, name="proj")

    gate_consts = [w.vg, w.vb, *w.ws, w.bfull, w.ga]
    (na,), _ = _rowwise(lambda *a: ((_gate_fn(*a),), ()), [(sv.proj, A_WIDTH, 0), (sv.proj, A_WIDTH, 1)], gate_consts,
                        [(A_WIDTH, BF16)], [], ts=CHUNK, name="gate_fwd")

    def rope_fn(qr, kr, vr, cs, sn):
        return (qr * cs + _rot_half(qr) * sn, kr * cs + _rot_half(kr) * sn, vr), ()

    def rope_all(qr, kr, vr, cs, sn):
        return rope_fn(qr, kr, vr, cs, sn)[0] * len(DILATIONS), ()

    qkv, _ = _rowwise(
        rope_all, [(sv.proj, B_WIDTH, 2), (sv.proj, B_WIDTH, 3), (sv.proj, B_WIDTH, 4), _full(cos), _full(sin)], [],
        [(B_WIDTH, BF16, d) for d in DILATIONS for _ in range(3)], [], ts=512, name="rope_fwd")
    sv.qkv = {d: qkv[3 * i:3 * i + 3] for i, d in enumerate(DILATIONS)}

    branch = []
    for d in DILATIONS:
        o, l = _attn_fwd(*sv.qkv[d], d, name=f"attn_fwd_d{d}")
        branch += [_full(o, d), _full(l, d)]

    def combine_fn(o1, l1, o2, l2, o3, l3, nab, gb):
        m = jnp.maximum(jnp.maximum(l1, l2), l3)
        e1, e2, e3 = jnp.exp(l1 - m), jnp.exp(l2 - m), jnp.exp(l3 - m)
        den = e1 + e2 + e3
        ob = (e1 / den) * o1 + (e2 / den) * o2 + (e3 / den) * o3
        mixed = jnp.concatenate([nab, _rms(ob, gb).astype(BF16)], axis=1)
        lse = m + jnp.log(den)
        return (mixed, ob, lse) + (lse,) * len(DILATIONS), ()

    (sv.mixed, sv.ob, sv.lse_rows, *lses), _ = _rowwise(
        combine_fn, branch + [_full(na)], [w.gb],
        [(D_MODEL, BF16), (B_WIDTH, F32), (B_WIDTH, F32, HEAD_ROWS)] + [(B_WIDTH, F32, d) for d in DILATIONS], [],
        ts=512, name="combine")
    sv.lse = dict(zip(DILATIONS, lses))
    sv.y = _matmul(sv.mixed, w.big("w_out", sv.mixed), mode="nn", out_dtype=F32, name="mix_out")

    def mid_fn(xb, yb, g1, g2):
        x1 = xb + _rms(yb, g1)
        return (x1, _rms(x1, g2)), ()

    (sv.x1, sv.h2), _ = _rowwise(mid_fn, [_full(x), _full(sv.y)], [w.g_pm, w.g_pf], [(D_MODEL, F32), (D_MODEL, BF16)], [],
                                 ts=512, name="post_mix_norm")
    conv_w = w.big("conv_w", sv.h2)
    sv.ug, sv.uv, sv.yff = _ffn_up_geglu(sv.h2, w.big("w_up", sv.h2), conv_w[:, :D_FF], conv_w[:, D_FF:],
                                         w.cb_g, w.cb_v, name="ffn_up_geglu")
    sv.f = _matmul(sv.yff, w.big("w_down", sv.yff), mode="nn", out_dtype=F32, name="ffn_down")
    (x2,), _ = _rowwise(lambda xb, fb, g: ((xb + _rms(fb, g),), ()), [_full(sv.x1), _full(sv.f)], [w.g_post],
                        [(D_MODEL, F32)], [], ts=512, name="post_ffn_norm")
    return x2, sv


def _layer_bwd(dx2, sv, w, cos, sin, emit):
    g = {}

    def post_fn(fb, dxb, gp):
        _, vjp = jax.vjp(_rms, fb, gp)
        df, dg = vjp(dxb)
        return (df,), (dg,)

    (df,), (g["post_ffn_norm"],) = _rowwise(post_fn, [_full(sv.f), _full(dx2)], [w.g_post], [(D_MODEL, BF16)],
                                            [(1, D_MODEL)], ts=512, name="post_ffn_norm_bwd")
    big = {"w_down": _matmul(sv.yff, df, mode="tn", out_dtype=BF16, name="ffn_down_dw").reshape(N_DEV, -1, D_MODEL)}
    conv_w = w.big("conv_w", df)
    dug, duv, dh2, dwg, dwv = _ffn_geglu_bwd(sv.ug, sv.uv, df, w.big("w_down", df), w.big("w_up", df),
                                             conv_w[:, :D_FF], conv_w[:, D_FF:], w.cb_g, w.cb_v, name="ffn_geglu_bwd")
    big["conv_w"] = jnp.concatenate([dwg[0:3], dwv[0:3]], axis=1).reshape(3, N_DEV, D_MODEL).transpose(1, 0, 2)
    g["conv_b"] = jnp.concatenate([dwg[3], dwv[3]], axis=0)
    big["w_up"] = _matmul_by_destination(sv.h2, dug, duv, name="ffn_up_dw")
    g_pm = w.g_pm + emit(big)

    def mid_fn(x1b, yb, dhb, dxb, g1, g2):
        _, vjp2 = jax.vjp(_rms, x1b, g2)
        dx1h, dg2 = vjp2(dhb)
        dx1 = dxb + dx1h
        _, vjp1 = jax.vjp(_rms, yb, g1)
        dy, dg1 = vjp1(dx1)
        return (dx1, dy), (dg1, dg2)

    (dx1, dy), (g["post_mix_norm"], g["pre_ffn_norm"]) = _rowwise(
        mid_fn, [_full(sv.x1), _full(sv.y), _full(dh2), _full(dx2)], [g_pm, w.g_pf],
        [(D_MODEL, F32), (D_MODEL, BF16)], [(1, D_MODEL), (1, D_MODEL)], ts=256, name="post_mix_norm_bwd")
    dmixed = _matmul(dy, w.big("w_out", dy), mode="nt", out_dtype=F32, name="mix_out_dx")
    big = {"w_out": _matmul(sv.mixed, dy, mode="tn", out_dtype=BF16, name="mix_out_dw").reshape(N_DEV, -1, D_MODEL)}

    def attn_out_fn(obb, dmb, gb):
        _, vjp = jax.vjp(_rms, obb, gb)
        do, dgb = vjp(dmb)
        delta = _head_sum(do * obb)
        return (delta,) + (do,) * len(DILATIONS) + (delta,) * len(DILATIONS), (dgb,)

    (delta_rows, *outs), (g["out_norm_b"],) = _rowwise(
        attn_out_fn, [_full(sv.ob), (dmixed, B_WIDTH, 1)], [w.gb],
        [(B_WIDTH, F32, HEAD_ROWS)] + [(B_WIDTH, BF16, d) for d in DILATIONS] + [(B_WIDTH, F32, d) for d in DILATIONS],
        [(1, B_WIDTH)], ts=512, name="attn_out_bwd")
    do = dict(zip(DILATIONS, outs[:len(DILATIONS)]))
    delta = dict(zip(DILATIONS, outs[len(DILATIONS):]))
    parts = {"q": [], "k": [], "v": []}
    for d in DILATIONS:
        qv, kv, vv = sv.qkv[d]
        dq = _attn_bwd_q(qv, kv, vv, do[d], sv.lse[d], delta[d], d, name=f"attn_bwd_q_d{d}")
        dk, dv = _attn_bwd_kv(qv, kv, vv, do[d], _spread_rows(sv.lse_rows, d), _spread_rows(delta_rows, d), d,
                              name=f"attn_bwd_kv_d{d}")
        parts["q"].append(_full(dq, d))
        parts["k"].append(_full(dk, d))
        parts["v"].append(_full(dv, d))

    def rope_bwd_fn(q1, q2, q3, k1, k2, k3, v1, v2, v3, cs, sn):
        def back(t):
            return t * cs - _rot_half(t * sn)
        return (jnp.concatenate([back(q1 + q2 + q3), back(k1 + k2 + k3), v1 + v2 + v3], axis=1),), ()

    (dzb,), _ = _rowwise(rope_bwd_fn, parts["q"] + parts["k"] + parts["v"] + [_full(cos), _full(sin)], [],
                         [(3 * B_WIDTH, BF16)], [], ts=256, name="rope_bwd")

    gate_consts = [w.vg, w.vb, *w.ws, w.bfull, w.ga]

    def gate_bwd_fn(zu, zv, dna, *consts):
        _, vjp = jax.vjp(_gate_fn, zu, zv, *consts)
        grads = vjp(dna)
        return (jnp.concatenate([grads[0], grads[1]], axis=1),), tuple(grads[2:])

    (dza,), gsmall = _rowwise(
        gate_bwd_fn, [(sv.proj, A_WIDTH, 0), (sv.proj, A_WIDTH, 1), (dmixed, A_WIDTH, 0)], gate_consts,
        [(2 * A_WIDTH, BF16)], [c.shape for c in gate_consts], ts=CHUNK, name="gate_bwd")
    g["v_norm_g"], g["v_norm_b"] = gsmall[0], gsmall[1]
    g["w_spatial"] = jnp.stack(gsmall[2:6])
    g["b_spatial"] = _bias_reduce(gsmall[6], name="bias_reduce")[:, :A_GROUPS].T
    g["out_norm_a"] = gsmall[7]

    dproj = jnp.concatenate([dza, dzb], axis=1)
    dh1 = _matmul(dproj, w.big("w_in", dproj), mode="nt", out_dtype=F32, name="proj_dx")
    dw_in = _matmul(sv.h1, dproj, mode="tn", out_dtype=BF16, name="proj_dw")
    big["w_in"] = dw_in.reshape(D_MODEL, N_DEV, IN_COLS // N_DEV).transpose(1, 0, 2)
    g_pre = w.g_pre + emit(big)

    def pre_fn(xb, dhb, dxb, gp):
        _, vjp = jax.vjp(_rms, xb, gp)
        dxh, dg = vjp(dhb)
        return (dxb + dxh,), (dg,)

    (dx,), (g["pre_mix_norm"],) = _rowwise(pre_fn, [_full(sv.x), _full(dh1), _full(dx1)], [g_pre], [(D_MODEL, F32)],
                                           [(1, D_MODEL)], ts=512, name="pre_mix_norm_bwd")
    return dx, g


def _layer_weights(l, full, small):
    row = lambda a: a[l].reshape(1, -1)
    return types.SimpleNamespace(
        big=functools.partial(full, l),
        g_pre=row(small["pre_mix_norm"]), vg=row(small["v_norm_g"]), vb=row(small["v_norm_b"]),
        ws=[small["w_spatial"][l, gi] for gi in range(A_GROUPS)],
        bfull=jnp.repeat(small["b_spatial"][l].T, CHUNK, axis=1),
        ga=row(small["out_norm_a"]), gb=row(small["out_norm_b"]),
        g_pm=row(small["post_mix_norm"]), g_pf=row(small["pre_ffn_norm"]),
        cb_g=small["conv_b"][l][:D_FF].reshape(1, -1), cb_v=small["conv_b"][l][D_FF:].reshape(1, -1),
        g_post=row(small["post_ffn_norm"]))


def _local_step(x, target, full, small, emit, started):
    s = x.shape[0]
    cos, sin = _rope_tables(s)
    ws = [_layer_weights(l, full, small) for l in range(N_LAYERS)]
    ws[0].g_pre = ws[0].g_pre + started
    saved = []
    h = x
    for l in range(N_LAYERS):
        h, sv = _layer_fwd(h, ws[l], cos, sin)
        saved.append(sv)

    def loss_fn(yb, tb):
        diff = yb - tb
        return (diff * (1.0 / D_MODEL),), (jnp.sum(diff * diff, axis=0, keepdims=True),)

    (dh,), (sq,) = _rowwise(loss_fn, [_full(h), _full(target)], [], [(D_MODEL, F32)], [(1, D_MODEL)], ts=512, name="loss")
    loss = 0.5 * jnp.sum(sq) * (1.0 / D_MODEL)
    grads = [None] * N_LAYERS
    for l in reversed(range(N_LAYERS)):
        dh, grads[l] = _layer_bwd(dh, saved[l], ws[l], cos, sin, functools.partial(emit, l))
    return loss, dh, grads


def _place():
    return lax.axis_index("x"), lax.axis_index("y"), lax.axis_index("c")


def _all_gather(x, after, name):
    def body(x_ref, after_ref, out_ref, send_sems, recv_sems, local_sem):
        mx, my, mc = _place()
        me, sibling = (mx, my, mc), (mx, my, 1 - mc)
        chips = [(1 - mx, my), (mx, 1 - my), (1 - mx, 1 - my)]

        def slot(px, py, pc):
            return out_ref.at[4 * px + 2 * py + pc]

        def copy(k, block, to, src=None):
            return pltpu.make_async_remote_copy(
                src_ref=slot(*block) if src is None else src, dst_ref=slot(*block),
                send_sem=send_sems.at[k], recv_sem=recv_sems.at[k], device_id=to, device_id_type=MESH_ID)

        mine = pltpu.make_async_copy(x_ref, slot(*me), local_sem)
        mine.start()
        first = [copy(0, me, sibling, src=x_ref)]
        first += [copy(1 + j, me, (*chip, mc), src=x_ref) for j, chip in enumerate(chips)]
        for cp in first:
            cp.start()
        passed = [copy(4 + j, (*chip, mc), sibling) for j, chip in enumerate(chips)]
        for j, chip in enumerate(chips):
            copy(1 + j, (*chip, mc), me).wait_recv()
            passed[j].start()
        copy(0, sibling, me).wait_recv()
        for j, chip in enumerate(chips):
            copy(4 + j, (*chip, 1 - mc), me).wait_recv()
        for cp in first + passed:
            cp.wait_send()
        mine.wait()

    return pl.pallas_call(
        body, name=name, out_shape=jax.ShapeDtypeStruct((N_DEV,) + x.shape, x.dtype), in_specs=[ANY, ANY],
        out_specs=ANY,
        scratch_shapes=[pltpu.SemaphoreType.DMA((7,)), pltpu.SemaphoreType.DMA((7,)), pltpu.SemaphoreType.DMA],
    )(x, after)


FLIPS = ((1, 0, 0), (0, 1, 0), (1, 1, 0), (0, 0, 1), (1, 0, 1), (0, 1, 1), (1, 1, 1))
HBM_SPEC = pl.BlockSpec(memory_space=pltpu.HBM)
SEM_SPEC = pl.BlockSpec(memory_space=pltpu.SEMAPHORE)
SPLIT_COPY = pltpu.CompilerParams(has_side_effects=pltpu.SideEffectType.DATAFLOW_SIDE_EFFECTING)


def _peers():
    mx, my, mc = _place()
    out = []
    for fx, fy, fc in FLIPS:
        px, py, pc = (1 - mx if fx else mx), (1 - my if fy else my), (1 - mc if fc else mc)
        out.append(((px, py, pc), 4 * px + 2 * py + pc))
    return out


def _flat_copies(scatter, src_refs, land_refs, send_sems, recv_sems):
    mx, my, mc = _place()
    me = 4 * mx + 2 * my + mc
    n = len(src_refs)
    copies = []
    for t in range(n):
        for i, (peer, number) in enumerate(_peers()):
            copies.append(pltpu.make_async_remote_copy(
                src_ref=src_refs[t].at[number] if scatter else src_refs[t],
                dst_ref=land_refs[t].at[i] if scatter else land_refs[t].at[me],
                send_sem=send_sems.at[t * len(FLIPS) + i], recv_sem=recv_sems.at[t * len(FLIPS) + i],
                device_id=peer, device_id_type=MESH_ID))
    return copies


def _flat_start(arrays, scatter, name):
    n = len(arrays)
    slots = len(FLIPS) if scatter else N_DEV
    lands = [lax.empty((slots,) + (a.shape[1:] if scatter else a.shape), a.dtype) for a in arrays]

    def body(*refs):
        src, land, (send_sems, recv_sems), token = refs[:n], refs[n:2 * n], refs[2 * n:2 * n + 2], refs[-1]
        for cp in _flat_copies(scatter, src, land, send_sems, recv_sems):
            cp.start()
        token[...] = jnp.zeros_like(token)

    hbm = [pltpu.HBM(a.shape, a.dtype) for a in arrays] + [pltpu.HBM(a.shape, a.dtype) for a in lands]
    sems = pltpu.SemaphoreType.DMA((n * len(FLIPS),))
    outs = pl.pallas_call(
        body, name=name, out_shape=(sems, sems, *hbm, jax.ShapeDtypeStruct((8, 128), F32)),
        in_specs=[HBM_SPEC] * (2 * n),
        out_specs=(SEM_SPEC, SEM_SPEC, *([HBM_SPEC] * (2 * n)), pl.BlockSpec(memory_space=pltpu.VMEM)),
        input_output_aliases={t: 2 + t for t in range(2 * n)}, compiler_params=SPLIT_COPY,
    )(*[pltpu.with_memory_space_constraint(a, pltpu.HBM) for a in (*arrays, *lands)])
    return types.SimpleNamespace(sems=outs[:2], thru=outs[2:2 + 2 * n], scatter=scatter, n=n), outs[-1][0:1, 0:1]


def _flat_wait(handle, after, name):
    n = handle.n

    def body(*refs):
        src, land, (send_sems, recv_sems) = refs[:n], refs[n:2 * n], refs[2 * n:2 * n + 2]
        for cp in _flat_copies(handle.scatter, src, land, send_sems, recv_sems):
            cp.wait_send()
            cp.wait_recv()

    outs = pl.pallas_call(
        body, name=name, out_shape=tuple(pltpu.HBM(a.shape, a.dtype) for a in handle.thru),
        in_specs=[HBM_SPEC] * (2 * n) + [SEM_SPEC, SEM_SPEC, ANY], out_specs=tuple([HBM_SPEC] * (2 * n)),
        input_output_aliases={t: t for t in range(2 * n)}, compiler_params=SPLIT_COPY,
    )(*handle.thru, *handle.sems, after)
    return outs[:n], outs[n:]


def _adamw(w, g, m, v):
    m2 = ADAM_B1 * m + (1.0 - ADAM_B1) * g
    v2 = ADAM_B2 * v + (1.0 - ADAM_B2) * (g * g)
    m_hat = m2 / (1.0 - ADAM_B1 ** ADAM_STEP)
    v_hat = v2 / (1.0 - ADAM_B2 ** ADAM_STEP)
    return -ADAM_LR * (m_hat / (jnp.sqrt(v_hat) + ADAM_EPS) + ADAM_WD * w), m2, v2


def _adamw_sharded(me, mine, landed, w, m, v, tr, name):
    _, r, c = w.shape
    nt = r // tr
    assert r % tr == 0 and len(mine) == len(landed) == N_LAYERS == 2, name
    per_layer = 1 + len(FLIPS)

    def body(me_ref, *refs):
        terms, (w_ref, m_ref, v_ref), outs = refs[:2 * per_layer], refs[2 * per_layer:2 * per_layer + 3], refs[-4:]
        layer = pl.program_id(0)

        def total(group):
            g = group[0][0].astype(F32)
            for t in group[1:]:
                g = g + t[0].astype(F32)
            return g

        g = jnp.where(layer == 0, total(terms[:per_layer]), total(terms[per_layer:]))
        d, m2, v2 = _adamw(w_ref[0], g, m_ref[0], v_ref[0])
        for o, val in zip(outs, (g, d, m2, v2)):
            o[0] = val

    def held(l):
        return lambda layer, i: jnp.where(layer == l, i, nt - 1 if l == 0 else 0)

    in_specs = []
    for l in range(N_LAYERS):
        rows = held(l)
        in_specs.append(pl.BlockSpec((1, tr, c), functools.partial(lambda layer, i, me_ref, rows: (me_ref[0], rows(layer, i), 0), rows=rows)))
        for k in range(len(FLIPS)):
            in_specs.append(pl.BlockSpec(
                (1, tr, c), functools.partial(lambda layer, i, me_ref, rows, k: (k, rows(layer, i), 0), rows=rows, k=k)))
    tile = pl.BlockSpec((1, tr, c), lambda layer, i, me_ref: (layer, i, 0))
    operands = []
    for l in range(N_LAYERS):
        operands += [mine[l]] + [landed[l]] * len(FLIPS)
    return pl.pallas_call(
        body, name=name, out_shape=[jax.ShapeDtypeStruct(w.shape, F32)] * 4,
        grid_spec=pltpu.PrefetchScalarGridSpec(
            num_scalar_prefetch=1, grid=(N_LAYERS, nt), in_specs=in_specs + [tile] * 3, out_specs=[tile] * 4),
        compiler_params=_params("arbitrary", "arbitrary"),
    )(me, *operands, w, m, v)


def _adamw_replicated(parts, w, m, v, name):
    def body(p_ref, w_ref, m_ref, v_ref, g_ref, d_ref, m2_ref, v2_ref):
        g = p_ref[0]
        for j in range(1, N_DEV):
            g = g + p_ref[j]
        d, m2, v2 = _adamw(w_ref[...], g, m_ref[...], v_ref[...])
        g_ref[...], d_ref[...], m2_ref[...], v2_ref[...] = g, d, m2, v2

    return pl.pallas_call(body, name=name, out_shape=[jax.ShapeDtypeStruct(w.shape, F32)] * 4,
                          compiler_params=pltpu.CompilerParams(vmem_limit_bytes=VMEM_LIMIT_BYTES))(parts, w, m, v)


def _pack_small(vals):
    flat = jnp.concatenate([vals[n].reshape(-1) for n in SMALL_NAMES])
    return jnp.concatenate([flat, jnp.zeros((SMALL_ROWS * D_MODEL - flat.shape[0],), F32)]).reshape(SMALL_ROWS, D_MODEL)


def _unpack_small(packed, shapes):
    flat, out, at = packed.reshape(-1), {}, 0
    for n in SMALL_NAMES:
        size = math.prod(shapes[n])
        out[n] = flat[at:at + size].reshape(shapes[n])
        at += size
    return out


GATHER_GROUPS = ((0, ("w_in",)), (0, ("w_out", "w_up", "conv_w", "w_down")),
                 (1, ("w_in", "w_out", "w_up", "conv_w", "w_down")))
ADAMW_TILE_ROWS = {"w_in": 512, "w_out": 128, "w_up": 256, "w_down": 256, "conv_w": 3}


def _assemble(name, land):
    if name == "w_in":
        return land.transpose(1, 0, 2).reshape(D_MODEL, IN_COLS)
    if name == "conv_w":
        return land.transpose(1, 0, 2).reshape(3, 2 * D_FF)
    if name == "w_up":
        return land
    return land.reshape(-1, D_MODEL)


def _start_gathers(wts, me):
    started, groups = jnp.zeros((1, 1), F32), []
    for gi, (l, names) in enumerate(GATHER_GROUPS):
        blocks = [wts[n][l] if n == "conv_w" else wts[n][l].astype(BF16) for n in names]
        handle, token = _flat_start(blocks, False, name=f"gather_start_{gi}")
        groups.append(types.SimpleNamespace(layer=l, names=names, blocks=blocks, handle=handle, got=None, index=gi))
        started = started + token

    def fetch(l, name, after):
        grp = next(gr for gr in groups if gr.layer == l and name in gr.names)
        if grp.got is None:
            lands = _flat_wait(grp.handle, after, name=f"gather_wait_{grp.index}")[1]
            grp.got = {}
            for n, blk, land in zip(grp.names, grp.blocks, lands):
                own = (me,) + (0,) * blk.ndim
                grp.got[n] = _assemble(n, lax.dynamic_update_slice(land, blk[None], own))
        return grp.got[name]

    return fetch, started


def kernel(x, pre_mix_norm, w_in, v_norm_g, v_norm_b, w_spatial, b_spatial, out_norm_a, out_norm_b, w_out, post_mix_norm, pre_ffn_norm, w_up, conv_w, conv_b, w_down, post_ffn_norm, loss_target, m_pre_mix_norm, m_w_in, m_v_norm_g, m_v_norm_b, m_w_spatial, m_b_spatial, m_out_norm_a, m_out_norm_b, m_w_out, m_post_mix_norm, m_pre_ffn_norm, m_w_up, m_conv_w, m_conv_b, m_w_down, m_post_ffn_norm, v_pre_mix_norm, v_w_in, v_v_norm_g, v_v_norm_b, v_w_spatial, v_b_spatial, v_out_norm_a, v_out_norm_b, v_w_out, v_post_mix_norm, v_pre_ffn_norm, v_w_up, v_conv_w, v_conv_b, v_w_down, v_post_ffn_norm):
    wts = dict(zip(WEIGHT_NAMES, (pre_mix_norm, w_in, v_norm_g, v_norm_b, w_spatial, b_spatial, out_norm_a, out_norm_b,
                                  w_out, post_mix_norm, pre_ffn_norm, w_up, conv_w, conv_b, w_down, post_ffn_norm)))
    mom1 = dict(zip(WEIGHT_NAMES, (m_pre_mix_norm, m_w_in, m_v_norm_g, m_v_norm_b, m_w_spatial, m_b_spatial, m_out_norm_a,
                                   m_out_norm_b, m_w_out, m_post_mix_norm, m_pre_ffn_norm, m_w_up, m_conv_w, m_conv_b,
                                   m_w_down, m_post_ffn_norm)))
    mom2 = dict(zip(WEIGHT_NAMES, (v_pre_mix_norm, v_w_in, v_v_norm_g, v_v_norm_b, v_w_spatial, v_b_spatial, v_out_norm_a,
                                   v_out_norm_b, v_w_out, v_post_mix_norm, v_pre_ffn_norm, v_w_up, v_conv_w, v_conv_b,
                                   v_w_down, v_post_ffn_norm)))
    mx, my, mc = _place()
    me = 4 * mx + 2 * my + mc

    fetch, started = _start_gathers(wts, me)
    scatters = []

    def emit(l, blocks):
        names = tuple(blocks)
        handle, token = _flat_start([blocks[n] for n in names], True, name=f"scatter_start_{l}_{len(scatters) % 2}")
        scatters.append((l, names, handle))
        return token

    loss_local, dx, grads = _local_step(x[0], loss_target[0], fetch, wts, emit, started)
    loss = lax.psum(loss_local, AXES)

    me_arr = jnp.reshape(me, (1,)).astype(jnp.int32)
    big_out = [{}, {}, {}, {}]

    def finish(group, after):
        mine, landed = {}, {}
        for l, names, handle in scatters:
            if names == group:
                sent, lands = _flat_wait(handle, after, name=f"scatter_wait_{l}_{'_'.join(names)}")
                for n, a, b in zip(names, sent, lands):
                    mine[l, n], landed[l, n] = a, b
        for n in group:
            res = _adamw_sharded(me_arr, [mine[l, n] for l in range(N_LAYERS)], [landed[l, n] for l in range(N_LAYERS)],
                                 wts[n], mom1[n], mom2[n], ADAMW_TILE_ROWS[n], name=f"adamw_{n}")
            for kind in range(4):
                big_out[kind][n] = res[kind]
        return res[0]

    early, late = scatters[0][1], scatters[1][1]
    done_early = finish(early, dx)
    small_grads = {n: jnp.stack([g[n].reshape(wts[n].shape[1:]) for g in grads]) for n in SMALL_NAMES}
    everyone = _all_gather(_pack_small(small_grads), done_early, name="gather_small_grads")
    small = [_pack_small({n: t[n] for n in SMALL_NAMES}) for t in (wts, mom1, mom2)]
    small_out = _adamw_replicated(everyone, *small, name="adamw_replicated")
    finish(late, small_out[0])
    small_shapes = {n: wts[n].shape for n in SMALL_NAMES}
    small_out = [_unpack_small(o, small_shapes) for o in small_out]

    outs = [loss, dx[None]]
    for kind in range(4):
        outs += [big_out[kind][n] if n in BIG_NAMES else small_out[kind][n] for n in WEIGHT_NAMES]
    return tuple(outs)
```

```python
import functools
import math
import types

import jax
import jax.numpy as jnp
from jax import lax
from jax.experimental import pallas as pl
from jax.experimental.pallas import tpu as pltpu

F32 = jnp.float32
BF16 = jnp.bfloat16

D_MODEL = 1024
A_WIDTH = 512
A_GROUPS = 4
CHUNK = 128
B_WIDTH = 512
HEAD_DIM = 64
N_HEADS = B_WIDTH // HEAD_DIM
ROT_DIM = 16
ROPE_THETA = 500000.0
BAND = 128
DILATIONS = (1, 4, 16)
IN_COLS = 2560
D_FF = 4096
EPS = 1e-6
NEG_INF = -1e30
N_DEV = 8
N_LAYERS = 2

ADAM_LR = 0.001
ADAM_B1 = 0.9
ADAM_B2 = 0.999
ADAM_EPS = 1e-08
ADAM_WD = 0.01
ADAM_STEP = 10

VMEM_LIMIT_BYTES = 56 * 1024 * 1024
MESH_ID = pl.DeviceIdType.MESH
ANY = pl.BlockSpec(memory_space=pl.ANY)
AXES = ("x", "y", "c")

WEIGHT_NAMES = ("pre_mix_norm", "w_in", "v_norm_g", "v_norm_b", "w_spatial", "b_spatial", "out_norm_a", "out_norm_b",
                "w_out", "post_mix_norm", "pre_ffn_norm", "w_up", "conv_w", "conv_b", "w_down", "post_ffn_norm")
BIG_NAMES = ("w_in", "w_out", "w_up", "w_down", "conv_w")
SMALL_NAMES = tuple(n for n in WEIGHT_NAMES if n not in BIG_NAMES)

PACK_ROWS = {"w_in": 640, "w_out": 256, "w_up": 2048, "w_down": 1024, "conv_w": 6}
CONV_W_PAD = 2
PACKED_F32_ROWS = 4096
PACKED_BF16_ROWS = 3984
SMALL_ROWS = 160


def _params(*sem):
    return pltpu.CompilerParams(dimension_semantics=sem, vmem_limit_bytes=VMEM_LIMIT_BYTES)


def _dotg(a, b, ca, cb):
    return lax.dot_general(a.astype(BF16), b.astype(BF16), (((ca,), (cb,)), ((), ())), preferred_element_type=F32)


@jax.custom_vjp
def _bdot(a, b):
    return _dotg(a, b, 1, 0)


def _bdot_fwd(a, b):
    return _dotg(a, b, 1, 0), (a, b)


def _bdot_bwd(res, g):
    a, b = res
    return _dotg(g, b, 1, 1), _dotg(a, g, 0, 0)


_bdot.defvjp(_bdot_fwd, _bdot_bwd)


def _rms(x, g):
    return x * lax.rsqrt(jnp.mean(x * x, axis=-1, keepdims=True) + EPS) * g


def _layernorm(x, g, b):
    mu = jnp.mean(x, axis=-1, keepdims=True)
    xc = x - mu
    return xc * lax.rsqrt(jnp.mean(xc * xc, axis=-1, keepdims=True) + EPS) * g + b


def _gelu_erf(x):
    return x * (lax.erf(x * (1.0 / math.sqrt(2.0))) + 1.0) * 0.5


def _gelu_tanh(x):
    c = math.sqrt(2.0 / math.pi)
    return 0.5 * x * (1.0 + jnp.tanh(c * (x + 0.044715 * (x * x * x))))


def _gelu_tanh_and_slope(x):
    c, k = math.sqrt(2.0 / math.pi), 0.044715
    x2 = x * x
    t = jnp.tanh(c * (x + k * (x2 * x)))
    half_x, one_t = 0.5 * x, 1.0 + t
    return half_x * one_t, 0.5 * one_t + (half_x * (1.0 - t * t)) * (c + (3.0 * k * c) * x2)


def _rot_half(x):
    width = x.shape[1]
    lane = lax.broadcasted_iota(jnp.int32, x.shape, 1) % HEAD_DIM
    back = pltpu.roll(x, ROT_DIM // 2, 1)
    fwd = pltpu.roll(x, width - ROT_DIM // 2, 1)
    return jnp.where(lane < ROT_DIM // 2, -fwd, jnp.where(lane < ROT_DIM, back, 0.0))


def _split3(z):
    h0 = z.astype(BF16)
    r1 = z - h0.astype(F32)
    h1 = r1.astype(BF16)
    h2 = (r1 - h1.astype(F32)).astype(BF16)
    return h0, h1, h2


def _head_sum(z):
    width = z.shape[1]
    a = lax.broadcasted_iota(jnp.int32, (width, width), 0) // HEAD_DIM
    b = lax.broadcasted_iota(jnp.int32, (width, width), 1) // HEAD_DIM
    ones = jnp.where(a == b, 1.0, 0.0).astype(BF16)
    out = None
    for part in _split3(z):
        t = lax.dot_general(part, ones, (((1,), (0,)), ((), ())), preferred_element_type=F32)
        out = t if out is None else out + t
    return out


MATMUL_VMEM_BUDGET = 40 * 1024 * 1024


def _matmul_tiles(m, n, k, out_bytes):
    tn = n if n <= 1024 else (1280 if n % 1280 == 0 and n % 1024 else 1024)
    tk = k if k <= 1024 else (1280 if k % 1280 == 0 and k % 1024 else 1024)
    tm = m
    while tm > 256:
        blocks = 2 * 2 * (tm * tk + tk * tn) + 2 * out_bytes * tm * tn + (4 * tm * tn if k > tk else 0)
        if blocks <= MATMUL_VMEM_BUDGET and m % tm == 0:
            break
        tm //= 2
    return tm, tn, tk


def _matmul(a, b, *, mode, out_dtype, name, cols=None):
    wide = D_MODEL if cols is not None else None
    if mode == "nn":
        (m, k), (_, n) = a.shape, (b.shape if cols is None else (b.shape[1], cols[1] * wide))
    elif mode == "nt":
        (m, k), (n, _) = a.shape, (b.shape if cols is None else (b.shape[1], cols[1] * wide))
    else:
        (k, m), (_, n) = a.shape, b.shape
    tm, tn, tk = _matmul_tiles(m, n, k, jnp.dtype(out_dtype).itemsize)
    assert m % tm == 0 and n % tn == 0 and k % tk == 0, (name, m, n, k)
    nk = k // tk
    if mode == "nn":
        a_spec = pl.BlockSpec((tm, tk), lambda i, j, kk: (i, kk))
        b_spec = pl.BlockSpec((tk, tn), lambda i, j, kk: (kk, j))
        if cols is not None:
            assert tn == wide
            b_spec = pl.BlockSpec((None, tk, tn), lambda i, j, kk: (cols[0] + j, kk, 0))
        ca, cb = 1, 0
    elif mode == "nt":
        a_spec = pl.BlockSpec((tm, tk), lambda i, j, kk: (i, kk))
        b_spec = pl.BlockSpec((tn, tk), lambda i, j, kk: (j, kk))
        if cols is not None:
            assert tk == wide
            b_spec = pl.BlockSpec((None, tn, tk), lambda i, j, kk: (cols[0] + kk, j, 0))
        ca, cb = 1, 1
    else:
        a_spec = pl.BlockSpec((tk, tm), lambda i, j, kk: (kk, i))
        b_spec = pl.BlockSpec((tk, tn), lambda i, j, kk: (kk, j))
        ca, cb = 0, 0

    def body(a_ref, b_ref, o_ref, *acc):
        kk = pl.program_id(2)
        part = lax.dot_general(a_ref[...], b_ref[...], (((ca,), (cb,)), ((), ())), preferred_element_type=F32)
        if nk == 1:
            o_ref[...] = part.astype(o_ref.dtype)
            return
        acc_ref, = acc

        @pl.when(kk == 0)
        def _():
            acc_ref[...] = part

        @pl.when(kk > 0)
        def _():
            acc_ref[...] += part

        @pl.when(kk == nk - 1)
        def _():
            o_ref[...] = acc_ref[...].astype(o_ref.dtype)

    return pl.pallas_call(
        body, name=name, grid=(m // tm, n // tn, nk),
        in_specs=[a_spec, b_spec], out_specs=pl.BlockSpec((tm, tn), lambda i, j, kk: (i, j)),
        out_shape=jax.ShapeDtypeStruct((m, n), out_dtype),
        scratch_shapes=[pltpu.VMEM((tm, tn), F32)] if nk > 1 else [],
        compiler_params=_params("parallel", "parallel", "arbitrary"),
    )(a, b)


def _matmul_by_destination(a, b_lo, b_hi, *, name, tm=1024, tk=2048):
    (k, m), half = a.shape, N_DEV // 2
    assert b_lo.shape == b_hi.shape == (k, half * D_MODEL) and m % tm == 0 and k % tk == 0, name
    nk = k // tk

    def body(a_ref, lo_ref, hi_ref, o_ref, acc_ref):
        j, kk = pl.program_id(1), pl.program_id(2)

        def step(b_ref):
            part = lax.dot_general(a_ref[...], b_ref[...], (((0,), (0,)), ((), ())), preferred_element_type=F32)

            @pl.when(kk == 0)
            def _():
                acc_ref[...] = part

            @pl.when(kk > 0)
            def _():
                acc_ref[...] += part

        pl.when(j < half)(lambda: step(lo_ref))
        pl.when(j >= half)(lambda: step(hi_ref))

        @pl.when(kk == nk - 1)
        def _():
            o_ref[...] = acc_ref[...].astype(o_ref.dtype)

    lo_spec = pl.BlockSpec((tk, D_MODEL), lambda i, j, kk: (jnp.where(j < half, kk, nk - 1), jnp.minimum(j, half - 1)))
    hi_spec = pl.BlockSpec((tk, D_MODEL), lambda i, j, kk: (jnp.where(j >= half, kk, 0), jnp.maximum(j - half, 0)))
    return pl.pallas_call(
        body, name=name, grid=(m // tm, N_DEV, nk),
        in_specs=[pl.BlockSpec((tk, tm), lambda i, j, kk: (kk, i)), lo_spec, hi_spec],
        out_specs=pl.BlockSpec((None, tm, D_MODEL), lambda i, j, kk: (j, i, 0)),
        out_shape=jax.ShapeDtypeStruct((N_DEV, m, D_MODEL), BF16),
        scratch_shapes=[pltpu.VMEM((tm, D_MODEL), F32)],
        compiler_params=_params("parallel", "parallel", "arbitrary"),
    )(a, b_lo, b_hi)


LANES = 128


def _residues_to_rows(ref, scr, d):
    w = ref.shape[1] // d
    n = ref.shape[0]
    for r in range(d):
        for c in range(w // LANES):
            scr[c, pl.ds(r, n, stride=d), :] = ref[:, r * w + c * LANES:r * w + (c + 1) * LANES].astype(F32)
    return jnp.concatenate([scr[c] for c in range(w // LANES)], axis=1)


def _rows_to_residues(val, ref, scr, d):
    w = val.shape[1]
    n = ref.shape[0]
    for c in range(w // LANES):
        scr[c] = val[:, c * LANES:(c + 1) * LANES].astype(F32)
    for r in range(d):
        for c in range(w // LANES):
            ref[:, r * w + c * LANES:r * w + (c + 1) * LANES] = scr[c, pl.ds(r, n, stride=d), :].astype(ref.dtype)


HEAD_ROWS = 0


def _head_rows_block(z):
    width = z.shape[1]
    a = lax.broadcasted_iota(jnp.int32, (width, LANES), 0)
    b = lax.broadcasted_iota(jnp.int32, (width, LANES), 1)
    pick = jnp.where(a == b * HEAD_DIM, 1.0, 0.0).astype(BF16)
    out = None
    for part in _split3(z):
        t = lax.dot_general(part, pick, (((1,), (0,)), ((), ())), preferred_element_type=F32)
        out = t if out is None else out + t
    return out.T[:N_HEADS, :]


def _rowwise(fn, rows, consts, out_rows, out_acc, *, ts, name):
    rows = [tuple(r) + (1,) * (4 - len(r)) for r in rows]
    out_rows = [tuple(o) + (1,) * (3 - len(o)) for o in out_rows]
    s = rows[0][0].shape[0] * rows[0][3]
    assert s % ts == 0, (name, s, ts)
    n_rows, n_in = len(rows), len(rows) + len(consts)
    n_row = len(out_rows)
    n_out = n_row + len(out_acc)
    moved = [(idx, w) for idx, (_, w, _, d) in enumerate(rows) if d > 1]
    moved += [(n_rows + idx, w) for idx, (w, _, d) in enumerate(out_rows) if d > 1]

    def body(*refs):
        scratch = dict(zip([key for key, _ in moved], refs[n_in + n_out:]))
        vals = []
        for idx, r in enumerate(refs[:n_in]):
            d = rows[idx][3] if idx < n_rows else 1
            vals.append(r[...] if d == 1 else _residues_to_rows(r, scratch[idx], d))
        row_vals, acc_vals = fn(*vals)
        for idx, (r, v) in enumerate(zip(refs[n_in:n_in + n_row], row_vals)):
            d = out_rows[idx][2]
            if d == 1:
                r[...] = v.astype(r.dtype)
            elif d == HEAD_ROWS:
                r[...] = _head_rows_block(v)
            else:
                _rows_to_residues(v, r, scratch[n_rows + idx], d)
        first = pl.program_id(0) == 0
        for r, v in zip(refs[n_in + n_row:n_in + n_out], acc_vals):
            @pl.when(first)
            def _(r=r, v=v):
                r[...] = v

            @pl.when(jnp.logical_not(first))
            def _(r=r, v=v):
                r[...] += v

    in_specs = [pl.BlockSpec((ts // d, d * w), functools.partial(lambda i, cb: (i, cb), cb=cb)) for _, w, cb, d in rows]
    in_specs += [pl.BlockSpec(c.shape, lambda i: (0, 0)) for c in consts]
    out_specs = [pl.BlockSpec((N_HEADS, ts), lambda i: (0, i)) if d == HEAD_ROWS else
                 pl.BlockSpec((ts // d, d * w), lambda i: (i, 0)) for w, _, d in out_rows]
    out_specs += [pl.BlockSpec(sh, lambda i: (0, 0)) for sh in out_acc]
    out_shape = [jax.ShapeDtypeStruct((N_HEADS, s) if d == HEAD_ROWS else (s // d, d * w), dt) for w, dt, d in out_rows]
    out_shape += [jax.ShapeDtypeStruct(sh, F32) for sh in out_acc]
    outs = pl.pallas_call(
        body, name=name, grid=(s // ts,), in_specs=in_specs, out_specs=out_specs, out_shape=out_shape,
        scratch_shapes=[pltpu.VMEM((w // LANES, ts, LANES), F32) for _, w in moved],
        compiler_params=_params("arbitrary" if out_acc else "parallel"),
    )(*[a for a, _, _, _ in rows], *consts)
    return outs[:n_row], outs[n_row:]


def _full(a, d=1):
    return (a, a.shape[1] // d, 0, d)


def _gate_fn(zu, zv, vg, vb, ws0, ws1, ws2, ws3, bfull, ga):
    u = _gelu_erf(zu)
    vn = _layernorm(_gelu_erf(zv), vg, vb)
    p = lax.broadcasted_iota(jnp.int32, (CHUNK, CHUNK), 0)
    q = lax.broadcasted_iota(jnp.int32, (CHUNK, CHUNK), 1)
    tril = jnp.where(q <= p, 1.0, 0.0)
    group = lax.broadcasted_iota(jnp.int32, (1, A_WIDTH), 1) // CHUNK
    sg = bfull
    for g, w in enumerate((ws0, ws1, ws2, ws3)):
        sg = sg + _bdot(w * tril, jnp.where(group == g, vn, 0.0))
    return _rms(u * sg, ga)


def _bias_reduce(dbf, name):
    def body(x_ref, o_ref):
        lane = lax.broadcasted_iota(jnp.int32, (CHUNK, CHUNK), 1)
        out = jnp.zeros((CHUNK, CHUNK), F32)
        for g in range(A_GROUPS):
            out = jnp.where(lane == g, jnp.sum(x_ref[:, g * CHUNK:(g + 1) * CHUNK], axis=1, keepdims=True), out)
        o_ref[...] = out

    return pl.pallas_call(body, name=name, out_shape=jax.ShapeDtypeStruct((CHUNK, CHUNK), F32))(dbf)


def _pair_mask(hh):
    lane = lax.broadcasted_iota(jnp.int32, (1, 2 * HEAD_DIM), 1)
    return (lane >= HEAD_DIM * hh) & (lane < HEAD_DIM * (hh + 1))


def _lane_pick(x2, lm):
    return jnp.max(jnp.where(lm, x2, -jnp.inf), axis=1, keepdims=True)


ATTN_BLOCKS = 2
ATTN_UNITS = [(b, hp, hh) for b in range(ATTN_BLOCKS) for hp in range(N_HEADS // 2) for hh in range(2)]
SCALE = HEAD_DIM ** -0.5


def _attn_specs(nb):
    cur = pl.BlockSpec((ATTN_BLOCKS * BAND, B_WIDTH), lambda r, j: (j, r))
    prev = pl.BlockSpec((BAND, B_WIDTH), lambda r, j: (jnp.maximum(ATTN_BLOCKS * j - 1, 0), r))
    nxt = pl.BlockSpec((BAND, B_WIDTH), lambda r, j: (jnp.minimum(ATTN_BLOCKS * (j + 1), nb - 1), r))
    return cur, prev, nxt


def _pair_cols(hp):
    return slice(2 * HEAD_DIM * hp, 2 * HEAD_DIM * (hp + 1))


def _rows(b):
    return slice(b * BAND, (b + 1) * BAND)


def _with_prev(cur_ref, prev_ref, b, sl):
    if b == 0:
        return jnp.concatenate([prev_ref[:, sl], cur_ref[_rows(0), sl]], axis=0)
    return cur_ref[(b - 1) * BAND:(b + 1) * BAND, sl]


def _with_next(cur_ref, next_ref, b, sl):
    if b == ATTN_BLOCKS - 1:
        return jnp.concatenate([cur_ref[_rows(b), sl], next_ref[:, sl]], axis=0)
    return cur_ref[b * BAND:(b + 2) * BAND, sl]


def _band_valid(other_exists):
    row = lax.broadcasted_iota(jnp.int32, (BAND, 2 * BAND), 0)
    col = lax.broadcasted_iota(jnp.int32, (BAND, 2 * BAND), 1)
    return (col >= row) & (col <= row + BAND), other_exists


def _masked(lm, x):
    return jnp.where(lm, x, jnp.zeros_like(x))


def _attn_fwd(q, k, v, d, name):
    nb = q.shape[0] // BAND
    cur, prev, _ = _attn_specs(nb)

    def body(q_ref, kc_ref, kp_ref, vc_ref, vp_ref, o_ref, l_ref):
        band, has_prev = _band_valid(pl.program_id(1) > 0)
        col = lax.broadcasted_iota(jnp.int32, (BAND, 2 * BAND), 1)
        valid = [band & ((col >= BAND) | has_prev)] + [band] * (ATTN_BLOCKS - 1)

        def scores(unit):
            b, hp, hh = unit
            sl = _pair_cols(hp)
            return _dotg(_masked(_pair_mask(hh), q_ref[_rows(b), sl]), _with_prev(kc_ref, kp_ref, b, sl), 1, 1)

        ahead, half = scores(ATTN_UNITS[0]), None
        for idx, (b, hp, hh) in enumerate(ATTN_UNITS):
            raw = ahead
            if idx + 1 < len(ATTN_UNITS):
                ahead = scores(ATTN_UNITS[idx + 1])
            sl, lm = _pair_cols(hp), _pair_mask(hh)
            s = jnp.where(valid[b], raw * SCALE, NEG_INF)
            m = jnp.max(s, axis=1, keepdims=True)
            p = jnp.exp(s - m)
            den = jnp.sum(p, axis=1, keepdims=True)
            o = _dotg(p, _with_prev(vc_ref, vp_ref, b, sl), 1, 0) / den
            lse = m + jnp.log(den)
            if hh == 0:
                half = (o, lse)
            else:
                o_ref[_rows(b), sl] = jnp.where(lm, o, half[0])
                l_ref[_rows(b), sl] = jnp.where(lm, lse, half[1])

    return pl.pallas_call(
        body, name=name, grid=(d, nb // ATTN_BLOCKS), in_specs=[cur, cur, prev, cur, prev], out_specs=[cur, cur],
        out_shape=[jax.ShapeDtypeStruct(q.shape, F32), jax.ShapeDtypeStruct(q.shape, F32)],
        compiler_params=_params("parallel", "parallel"),
    )(q, k, k, v, v)


def _attn_bwd_q(q, k, v, do, lse, delta, d, name):
    nb = q.shape[0] // BAND
    cur, prev, _ = _attn_specs(nb)

    def body(q_ref, kc_ref, kp_ref, vc_ref, vp_ref, do_ref, l_ref, dl_ref, dq_ref):
        band, has_prev = _band_valid(pl.program_id(1) > 0)
        col = lax.broadcasted_iota(jnp.int32, (BAND, 2 * BAND), 1)
        valid = [band & ((col >= BAND) | has_prev)] + [band] * (ATTN_BLOCKS - 1)

        def products(unit):
            b, hp, hh = unit
            sl, lm = _pair_cols(hp), _pair_mask(hh)
            return (_dotg(_masked(lm, q_ref[_rows(b), sl]), _with_prev(kc_ref, kp_ref, b, sl), 1, 1),
                    _dotg(_masked(lm, do_ref[_rows(b), sl]), _with_prev(vc_ref, vp_ref, b, sl), 1, 1))

        ahead, half = products(ATTN_UNITS[0]), None
        for idx, (b, hp, hh) in enumerate(ATTN_UNITS):
            raw, dp = ahead
            if idx + 1 < len(ATTN_UNITS):
                ahead = products(ATTN_UNITS[idx + 1])
            sl, lm = _pair_cols(hp), _pair_mask(hh)
            s = jnp.where(valid[b], raw * SCALE, NEG_INF)
            p = jnp.exp(s - _lane_pick(l_ref[_rows(b), sl], lm))
            ds = p * (dp - _lane_pick(dl_ref[_rows(b), sl], lm))
            dq = _dotg(ds, _with_prev(kc_ref, kp_ref, b, sl), 1, 0) * SCALE
            if hh == 0:
                half = dq
            else:
                dq_ref[_rows(b), sl] = jnp.where(lm, dq, half)

    return pl.pallas_call(
        body, name=name, grid=(d, nb // ATTN_BLOCKS), in_specs=[cur, cur, prev, cur, prev, cur, cur, cur],
        out_specs=cur,
        out_shape=jax.ShapeDtypeStruct(q.shape, F32),
        compiler_params=_params("parallel", "parallel"),
    )(q, k, k, v, v, do, lse, delta)


def _attn_bwd_kv(q, k, v, do, lse_t, delta_t, d, name):
    nb = q.shape[0] // BAND
    cur, _, nxt = _attn_specs(nb)
    t_cur = pl.BlockSpec((1, N_HEADS, ATTN_BLOCKS * BAND), lambda r, j: (r, 0, j))
    t_nxt = pl.BlockSpec((1, N_HEADS, BAND), lambda r, j: (r, 0, jnp.minimum(ATTN_BLOCKS * (j + 1), nb - 1)))

    def body(k_ref, v_ref, qc_ref, qn_ref, doc_ref, don_ref, lc_ref, ln_ref, dlc_ref, dln_ref, dk_ref, dv_ref):
        band, has_next = _band_valid(pl.program_id(1) < nb // ATTN_BLOCKS - 1)
        col = lax.broadcasted_iota(jnp.int32, (BAND, 2 * BAND), 1)
        valid = [band] * (ATTN_BLOCKS - 1) + [band & ((col < BAND) | has_next)]

        def head_row(c_ref, n_ref, b, h):
            if b == ATTN_BLOCKS - 1:
                return jnp.concatenate([c_ref[0, h:h + 1, b * BAND:(b + 1) * BAND], n_ref[0, h:h + 1, :]], axis=1)
            return c_ref[0, h:h + 1, b * BAND:(b + 2) * BAND]

        def products(unit):
            b, hp, hh = unit
            sl, lm = _pair_cols(hp), _pair_mask(hh)
            return (_dotg(_masked(lm, k_ref[_rows(b), sl]), _with_next(qc_ref, qn_ref, b, sl), 1, 1),
                    _dotg(_masked(lm, v_ref[_rows(b), sl]), _with_next(doc_ref, don_ref, b, sl), 1, 1))

        ahead, half = products(ATTN_UNITS[0]), None
        for idx, (b, hp, hh) in enumerate(ATTN_UNITS):
            raw, dpt = ahead
            if idx + 1 < len(ATTN_UNITS):
                ahead = products(ATTN_UNITS[idx + 1])
            sl, lm, h = _pair_cols(hp), _pair_mask(hh), 2 * hp + hh
            st = jnp.where(valid[b], raw * SCALE, NEG_INF)
            pt = jnp.exp(st - head_row(lc_ref, ln_ref, b, h))
            dv = _dotg(pt, _with_next(doc_ref, don_ref, b, sl), 1, 0)
            dst = pt * (dpt - head_row(dlc_ref, dln_ref, b, h))
            dk = _dotg(dst, _with_next(qc_ref, qn_ref, b, sl), 1, 0) * SCALE
            if hh == 0:
                half = (dk, dv)
            else:
                dk_ref[_rows(b), sl] = jnp.where(lm, dk, half[0])
                dv_ref[_rows(b), sl] = jnp.where(lm, dv, half[1])

    return pl.pallas_call(
        body, name=name, grid=(d, nb // ATTN_BLOCKS),
        in_specs=[cur, cur, cur, nxt, cur, nxt, t_cur, t_nxt, t_cur, t_nxt], out_specs=[cur, cur],
        out_shape=[jax.ShapeDtypeStruct(q.shape, F32), jax.ShapeDtypeStruct(q.shape, F32)],
        compiler_params=_params("parallel", "parallel"),
    )(k, v, q, q, do, do, lse_t, lse_t, delta_t, delta_t)


def _spread_rows(a, d):
    return a.reshape(N_HEADS, a.shape[1] // d, d).transpose(2, 0, 1)


FF_TS = 512
FF_TC = 1024
FF_SUB = 256
HALO = 8
HALO_BF16 = 16
UP_BLOCKS = D_MODEL // FF_TC


def _conv3(ext, w, b):
    return b + w[0:1, :] * pltpu.roll(ext, 2, 0) + w[1:2, :] * pltpu.roll(ext, 1, 0) + w[2:3, :] * ext


def _ffn_specs(s, cols_first):
    nrb = s // FF_TS
    per, per16 = FF_TS // HALO, FF_TS // HALO_BF16

    def mk(block, fn):
        if cols_first:
            return pl.BlockSpec(block, lambda j, i: fn(i, j))
        return pl.BlockSpec(block, lambda i, j: fn(i, j))

    specs = types.SimpleNamespace(
        nrb=nrb, ncb=D_FF // FF_TC,
        row=mk((FF_TS, FF_TC), lambda i, j: (i, j)),
        before=mk((HALO, FF_TC), lambda i, j: (jnp.maximum(i * per - 1, 0), j)),
        after=mk((HALO, FF_TC), lambda i, j: (jnp.minimum((i + 1) * per, nrb * per - 1), j)),
        w=mk((3, FF_TC), lambda i, j: (0, j)),
        b=mk((1, FF_TC), lambda i, j: (0, j)),
        part=mk((HALO, FF_TC), lambda i, j: (i, j)),
        act=mk((FF_TS, D_MODEL), lambda i, j: (i, 0)),
        act_before=mk((HALO_BF16, D_MODEL), lambda i, j: (jnp.maximum(i * per16 - 1, 0), 0)),
        act_after=mk((HALO_BF16, D_MODEL), lambda i, j: (jnp.minimum((i + 1) * per16, nrb * per16 - 1), 0)),
        up_gate=mk((None, D_MODEL, FF_TC), lambda i, j: (j // UP_BLOCKS, 0, j % UP_BLOCKS)),
        up_val=mk((None, D_MODEL, FF_TC), lambda i, j: (N_DEV // 2 + j // UP_BLOCKS, 0, j % UP_BLOCKS)),
        down=mk((FF_TC, D_MODEL), lambda i, j: (j, 0)),
    )
    return specs


def _ffn_up_geglu(h, w_up, wg, wv, bg, bv, name):
    s = h.shape[0]
    sp = _ffn_specs(s, True)

    def body(h_ref, hb_ref, ugw_ref, uvw_ref, wg_ref, wv_ref, bg_ref, bv_ref, ug_ref, uv_ref, y_ref):
        keep = jnp.where(pl.program_id(1) > 0, 1.0, 0.0).astype(BF16)
        hext = jnp.concatenate([hb_ref[...] * keep, h_ref[...]], axis=0)
        n_sub = FF_TC // FF_SUB
        cols = [slice(c * FF_SUB, (c + 1) * FF_SUB) for c in range(n_sub)]
        up = lambda c: (_dotg(hext, ugw_ref[:, cols[c]], 1, 0), _dotg(hext, uvw_ref[:, cols[c]], 1, 0))
        ahead = up(0)
        for c in range(n_sub):
            sl, (eg, ev) = cols[c], ahead
            if c + 1 < n_sub:
                ahead = up(c + 1)
            gate = _conv3(eg, wg_ref[:, sl], bg_ref[:, sl])[HALO_BF16:, :]
            val = _conv3(ev, wv_ref[:, sl], bv_ref[:, sl])[HALO_BF16:, :]
            ug_ref[:, sl] = eg[HALO_BF16:, :]
            uv_ref[:, sl] = ev[HALO_BF16:, :]
            y_ref[:, sl] = (_gelu_tanh(gate) * val).astype(y_ref.dtype)

    return pl.pallas_call(
        body, name=name, grid=(sp.ncb, sp.nrb),
        in_specs=[sp.act, sp.act_before, sp.up_gate, sp.up_val, sp.w, sp.w, sp.b, sp.b],
        out_specs=[sp.row, sp.row, sp.row],
        out_shape=[jax.ShapeDtypeStruct((s, D_FF), F32), jax.ShapeDtypeStruct((s, D_FF), F32),
                   jax.ShapeDtypeStruct((s, D_FF), BF16)],
        compiler_params=_params("parallel", "parallel"),
    )(h, h, w_up, w_up, wg, wv, bg, bv)


def _sum_parts(parts, name):
    n = parts.shape[0] // HALO

    def body(p_ref, o_ref):
        acc = p_ref[0:HALO, :]
        for t in range(1, n):
            acc = acc + p_ref[t * HALO:(t + 1) * HALO, :]
        o_ref[...] = acc

    return pl.pallas_call(body, name=name, out_shape=jax.ShapeDtypeStruct((HALO, parts.shape[1]), F32))(parts)


def _ffn_geglu_bwd(ug, uv, df, w_down, w_up, wg, wv, bg, bv, name):
    s = ug.shape[0]
    sp = _ffn_specs(s, False)
    nrb = sp.nrb
    rows = FF_TS + 2 * HALO
    lo, hi = HALO, HALO + FF_TS

    def body(ug_ref, uv_ref, hg_ref, hv_ref, ng_ref, nv_ref, df_ref, dfn_ref, dw_ref, ugw_ref, uvw_ref,
             wg_ref, wv_ref, bg_ref, bv_ref, dug_ref, duv_ref, dh_ref, dwg_ref, dwv_ref):
        i, j = pl.program_id(0), pl.program_id(1)
        keep_top = jnp.where(i > 0, 1.0, 0.0)
        keep_bot = jnp.where(i < nrb - 1, 1.0, 0.0).astype(BF16)
        dfe = jnp.concatenate([df_ref[...], dfn_ref[...] * keep_bot], axis=0)

        def back(dc, e, w, du_ref, sl):
            up1 = pltpu.roll(dc, rows - 1, 0)
            up2 = pltpu.roll(dc, rows - 2, 0)
            du = (w[2:3, :] * dc + w[1:2, :] * up1 + w[0:1, :] * up2)[lo:hi, :].astype(BF16)
            du_ref[:, sl] = du
            p1, p2 = up1 * e, up2 * e
            colsum = lambda p: jnp.sum(p[lo:hi, :], axis=0, keepdims=True)
            row = lambda p, t: p[t:t + 1, :]
            d_w1 = colsum(p1) + row(p1, lo - 1) - row(p1, hi - 1)
            d_w0 = colsum(p2) + row(p2, lo - 2) + row(p2, lo - 1) - row(p2, hi - 2) - row(p2, hi - 1)
            sums = [d_w0, d_w1, colsum(dc * e), colsum(dc), jnp.zeros((HALO - 4, FF_SUB), F32)]
            return du, jnp.concatenate(sums, axis=0)

        dh = None
        n_sub = FF_TC // FF_SUB
        cols = [slice(c * FF_SUB, (c + 1) * FF_SUB) for c in range(n_sub)]
        d_act = lambda c: _dotg(dfe, dw_ref[cols[c], :], 1, 1)[:FF_TS + HALO, :]
        ahead = d_act(0)
        for c in range(n_sub):
            sl, dy = cols[c], ahead
            if c + 1 < n_sub:
                ahead = d_act(c + 1)
            dye = jnp.concatenate([jnp.zeros((HALO, FF_SUB), F32), dy], axis=0)
            eg = jnp.concatenate([hg_ref[:, sl] * keep_top, ug_ref[:, sl], ng_ref[:, sl]], axis=0)
            ev = jnp.concatenate([hv_ref[:, sl] * keep_top, uv_ref[:, sl], nv_ref[:, sl]], axis=0)
            wg_, wv_ = wg_ref[:, sl], wv_ref[:, sl]
            gate = _conv3(eg, wg_, bg_ref[:, sl])
            val = _conv3(ev, wv_, bv_ref[:, sl])
            act, slope = _gelu_tanh_and_slope(gate)
            dug, dwg_ref[:, sl] = back((dye * val) * slope, eg, wg_, dug_ref, sl)
            duv, dwv_ref[:, sl] = back(dye * act, ev, wv_, duv_ref, sl)
            term = _dotg(dug, ugw_ref[:, sl], 1, 1) + _dotg(duv, uvw_ref[:, sl], 1, 1)
            dh = term if dh is None else dh + term

        @pl.when(j == 0)
        def _():
            dh_ref[...] = dh

        @pl.when(j > 0)
        def _():
            dh_ref[...] += dh

    parts = jax.ShapeDtypeStruct((nrb * HALO, D_FF), F32)
    dug, duv, dh, pg, pv = pl.pallas_call(
        body, name=name, grid=(nrb, sp.ncb),
        in_specs=[sp.row, sp.row, sp.before, sp.before, sp.after, sp.after, sp.act, sp.act_after, sp.down,
                  sp.up_gate, sp.up_val, sp.w, sp.w, sp.b, sp.b],
        out_specs=[sp.row, sp.row, sp.act, sp.part, sp.part],
        out_shape=[jax.ShapeDtypeStruct((s, D_FF), BF16), jax.ShapeDtypeStruct((s, D_FF), BF16),
                   jax.ShapeDtypeStruct((s, D_MODEL), F32), parts, parts],
        compiler_params=_params("parallel", "arbitrary"),
    )(ug, uv, ug, uv, ug, uv, df, df, w_down, w_up, w_up, wg, wv, bg, bv)
    return dug, duv, dh, _sum_parts(pg, name=name + "_sum_gate"), _sum_parts(pv, name=name + "_sum_val")


def _rope_tables(s):
    inv = ROPE_THETA ** (-jnp.arange(0, ROT_DIM, 2, dtype=F32) / ROT_DIM)
    ang = jnp.arange(s, dtype=F32)[:, None] * inv[None, :]
    cos8, sin8 = jnp.cos(ang), jnp.sin(ang)
    rest = HEAD_DIM - ROT_DIM
    cos_h = jnp.concatenate([cos8, cos8, jnp.ones((s, rest), F32)], axis=1)
    sin_h = jnp.concatenate([sin8, sin8, jnp.zeros((s, rest), F32)], axis=1)
    return jnp.tile(cos_h, (1, N_HEADS)), jnp.tile(sin_h, (1, N_HEADS))


def _layer_fwd(x, w, cos, sin):
    sv = types.SimpleNamespace(x=x)
    (sv.h1,), _ = _rowwise(lambda xb, g: ((_rms(xb, g),), ()), [_full(x)], [w.g_pre], [(D_MODEL, BF16)], [],
                           ts=512, name="pre_mix_norm")
    sv.proj = _matmul(sv.h1, w.big("w_in", sv.h1), mode="nn", out_dtype=F32, name="proj")

    gate_consts = [w.vg, w.vb, *w.ws, w.bfull, w.ga]
    (na,), _ = _rowwise(lambda *a: ((_gate_fn(*a),), ()), [(sv.proj, A_WIDTH, 0), (sv.proj, A_WIDTH, 1)], gate_consts,
                        [(A_WIDTH, BF16)], [], ts=CHUNK, name="gate_fwd")

    def rope_fn(qr, kr, vr, cs, sn):
        return (qr * cs + _rot_half(qr) * sn, kr * cs + _rot_half(kr) * sn, vr), ()

    def rope_all(qr, kr, vr, cs, sn):
        return rope_fn(qr, kr, vr, cs, sn)[0] * len(DILATIONS), ()

    qkv, _ = _rowwise(
        rope_all, [(sv.proj, B_WIDTH, 2), (sv.proj, B_WIDTH, 3), (sv.proj, B_WIDTH, 4), _full(cos), _full(sin)], [],
        [(B_WIDTH, BF16, d) for d in DILATIONS for _ in range(3)], [], ts=512, name="rope_fwd")
    sv.qkv = {d: qkv[3 * i:3 * i + 3] for i, d in enumerate(DILATIONS)}

    branch = []
    for d in DILATIONS:
        o, l = _attn_fwd(*sv.qkv[d], d, name=f"attn_fwd_d{d}")
        branch += [_full(o, d), _full(l, d)]

    def combine_fn(o1, l1, o2, l2, o3, l3, nab, gb):
        m = jnp.maximum(jnp.maximum(l1, l2), l3)
        e1, e2, e3 = jnp.exp(l1 - m), jnp.exp(l2 - m), jnp.exp(l3 - m)
        den = e1 + e2 + e3
        ob = (e1 / den) * o1 + (e2 / den) * o2 + (e3 / den) * o3
        mixed = jnp.concatenate([nab, _rms(ob, gb).astype(BF16)], axis=1)
        lse = m + jnp.log(den)
        return (mixed, ob, lse) + (lse,) * len(DILATIONS), ()

    (sv.mixed, sv.ob, sv.lse_rows, *lses), _ = _rowwise(
        combine_fn, branch + [_full(na)], [w.gb],
        [(D_MODEL, BF16), (B_WIDTH, F32), (B_WIDTH, F32, HEAD_ROWS)] + [(B_WIDTH, F32, d) for d in DILATIONS], [],
        ts=512, name="combine")
    sv.lse = dict(zip(DILATIONS, lses))
    sv.y = _matmul(sv.mixed, w.big("w_out", sv.mixed), mode="nn", out_dtype=F32, name="mix_out")

    def mid_fn(xb, yb, g1, g2):
        x1 = xb + _rms(yb, g1)
        return (x1, _rms(x1, g2)), ()

    (sv.x1, sv.h2), _ = _rowwise(mid_fn, [_full(x), _full(sv.y)], [w.g_pm, w.g_pf], [(D_MODEL, F32), (D_MODEL, BF16)], [],
                                 ts=512, name="post_mix_norm")
    conv_w = w.big("conv_w", sv.h2)
    sv.ug, sv.uv, sv.yff = _ffn_up_geglu(sv.h2, w.big("w_up", sv.h2), conv_w[:, :D_FF], conv_w[:, D_FF:],
                                         w.cb_g, w.cb_v, name="ffn_up_geglu")
    sv.f = _matmul(sv.yff, w.big("w_down", sv.yff), mode="nn", out_dtype=F32, name="ffn_down")
    (x2,), _ = _rowwise(lambda xb, fb, g: ((xb + _rms(fb, g),), ()), [_full(sv.x1), _full(sv.f)], [w.g_post],
                        [(D_MODEL, F32)], [], ts=512, name="post_ffn_norm")
    return x2, sv


def _layer_bwd(dx2, sv, w, cos, sin, emit):
    g = {}

    def post_fn(fb, dxb, gp):
        _, vjp = jax.vjp(_rms, fb, gp)
        df, dg = vjp(dxb)
        return (df,), (dg,)

    (df,), (g["post_ffn_norm"],) = _rowwise(post_fn, [_full(sv.f), _full(dx2)], [w.g_post], [(D_MODEL, BF16)],
                                            [(1, D_MODEL)], ts=512, name="post_ffn_norm_bwd")
    big = {"w_down": _matmul(sv.yff, df, mode="tn", out_dtype=BF16, name="ffn_down_dw").reshape(N_DEV, -1, D_MODEL)}
    conv_w = w.big("conv_w", df)
    dug, duv, dh2, dwg, dwv = _ffn_geglu_bwd(sv.ug, sv.uv, df, w.big("w_down", df), w.big("w_up", df),
                                             conv_w[:, :D_FF], conv_w[:, D_FF:], w.cb_g, w.cb_v, name="ffn_geglu_bwd")
    big["conv_w"] = jnp.concatenate([dwg[0:3], dwv[0:3]], axis=1).reshape(3, N_DEV, D_MODEL).transpose(1, 0, 2)
    g["conv_b"] = jnp.concatenate([dwg[3], dwv[3]], axis=0)
    big["w_up"] = _matmul_by_destination(sv.h2, dug, duv, name="ffn_up_dw")
    g_pm = w.g_pm + emit(big)

    def mid_fn(x1b, yb, dhb, dxb, g1, g2):
        _, vjp2 = jax.vjp(_rms, x1b, g2)
        dx1h, dg2 = vjp2(dhb)
        dx1 = dxb + dx1h
        _, vjp1 = jax.vjp(_rms, yb, g1)
        dy, dg1 = vjp1(dx1)
        return (dx1, dy), (dg1, dg2)

    (dx1, dy), (g["post_mix_norm"], g["pre_ffn_norm"]) = _rowwise(
        mid_fn, [_full(sv.x1), _full(sv.y), _full(dh2), _full(dx2)], [g_pm, w.g_pf],
        [(D_MODEL, F32), (D_MODEL, BF16)], [(1, D_MODEL), (1, D_MODEL)], ts=256, name="post_mix_norm_bwd")
    dmixed = _matmul(dy, w.big("w_out", dy), mode="nt", out_dtype=F32, name="mix_out_dx")
    big = {"w_out": _matmul(sv.mixed, dy, mode="tn", out_dtype=BF16, name="mix_out_dw").reshape(N_DEV, -1, D_MODEL)}

    def attn_out_fn(obb, dmb, gb):
        _, vjp = jax.vjp(_rms, obb, gb)
        do, dgb = vjp(dmb)
        delta = _head_sum(do * obb)
        return (delta,) + (do,) * len(DILATIONS) + (delta,) * len(DILATIONS), (dgb,)

    (delta_rows, *outs), (g["out_norm_b"],) = _rowwise(
        attn_out_fn, [_full(sv.ob), (dmixed, B_WIDTH, 1)], [w.gb],
        [(B_WIDTH, F32, HEAD_ROWS)] + [(B_WIDTH, BF16, d) for d in DILATIONS] + [(B_WIDTH, F32, d) for d in DILATIONS],
        [(1, B_WIDTH)], ts=512, name="attn_out_bwd")
    do = dict(zip(DILATIONS, outs[:len(DILATIONS)]))
    delta = dict(zip(DILATIONS, outs[len(DILATIONS):]))
    parts = {"q": [], "k": [], "v": []}
    for d in DILATIONS:
        qv, kv, vv = sv.qkv[d]
        dq = _attn_bwd_q(qv, kv, vv, do[d], sv.lse[d], delta[d], d, name=f"attn_bwd_q_d{d}")
        dk, dv = _attn_bwd_kv(qv, kv, vv, do[d], _spread_rows(sv.lse_rows, d), _spread_rows(delta_rows, d), d,
                              name=f"attn_bwd_kv_d{d}")
        parts["q"].append(_full(dq, d))
        parts["k"].append(_full(dk, d))
        parts["v"].append(_full(dv, d))

    def rope_bwd_fn(q1, q2, q3, k1, k2, k3, v1, v2, v3, cs, sn):
        def back(t):
            return t * cs - _rot_half(t * sn)
        return (jnp.concatenate([back(q1 + q2 + q3), back(k1 + k2 + k3), v1 + v2 + v3], axis=1),), ()

    (dzb,), _ = _rowwise(rope_bwd_fn, parts["q"] + parts["k"] + parts["v"] + [_full(cos), _full(sin)], [],
                         [(3 * B_WIDTH, BF16)], [], ts=256, name="rope_bwd")

    gate_consts = [w.vg, w.vb, *w.ws, w.bfull, w.ga]

    def gate_bwd_fn(zu, zv, dna, *consts):
        _, vjp = jax.vjp(_gate_fn, zu, zv, *consts)
        grads = vjp(dna)
        return (jnp.concatenate([grads[0], grads[1]], axis=1),), tuple(grads[2:])

    (dza,), gsmall = _rowwise(
        gate_bwd_fn, [(sv.proj, A_WIDTH, 0), (sv.proj, A_WIDTH, 1), (dmixed, A_WIDTH, 0)], gate_consts,
        [(2 * A_WIDTH, BF16)], [c.shape for c in gate_consts], ts=CHUNK, name="gate_bwd")
    g["v_norm_g"], g["v_norm_b"] = gsmall[0], gsmall[1]
    g["w_spatial"] = jnp.stack(gsmall[2:6])
    g["b_spatial"] = _bias_reduce(gsmall[6], name="bias_reduce")[:, :A_GROUPS].T
    g["out_norm_a"] = gsmall[7]

    dproj = jnp.concatenate([dza, dzb], axis=1)
    dh1 = _matmul(dproj, w.big("w_in", dproj), mode="nt", out_dtype=F32, name="proj_dx")
    dw_in = _matmul(sv.h1, dproj, mode="tn", out_dtype=BF16, name="proj_dw")
    big["w_in"] = dw_in.reshape(D_MODEL, N_DEV, IN_COLS // N_DEV).transpose(1, 0, 2)
    g_pre = w.g_pre + emit(big)

    def pre_fn(xb, dhb, dxb, gp):
        _, vjp = jax.vjp(_rms, xb, gp)
        dxh, dg = vjp(dhb)
        return (dxb + dxh,), (dg,)

    (dx,), (g["pre_mix_norm"],) = _rowwise(pre_fn, [_full(sv.x), _full(dh1), _full(dx1)], [g_pre], [(D_MODEL, F32)],
                                           [(1, D_MODEL)], ts=512, name="pre_mix_norm_bwd")
    return dx, g


def _layer_weights(l, full, small):
    row = lambda a: a[l].reshape(1, -1)
    return types.SimpleNamespace(
        big=functools.partial(full, l),
        g_pre=row(small["pre_mix_norm"]), vg=row(small["v_norm_g"]), vb=row(small["v_norm_b"]),
        ws=[small["w_spatial"][l, gi] for gi in range(A_GROUPS)],
        bfull=jnp.repeat(small["b_spatial"][l].T, CHUNK, axis=1),
        ga=row(small["out_norm_a"]), gb=row(small["out_norm_b"]),
        g_pm=row(small["post_mix_norm"]), g_pf=row(small["pre_ffn_norm"]),
        cb_g=small["conv_b"][l][:D_FF].reshape(1, -1), cb_v=small["conv_b"][l][D_FF:].reshape(1, -1),
        g_post=row(small["post_ffn_norm"]))


def _local_step(x, target, full, small, emit, started):
    s = x.shape[0]
    cos, sin = _rope_tables(s)
    ws = [_layer_weights(l, full, small) for l in range(N_LAYERS)]
    ws[0].g_pre = ws[0].g_pre + started
    saved = []
    h = x
    for l in range(N_LAYERS):
        h, sv = _layer_fwd(h, ws[l], cos, sin)
        saved.append(sv)

    def loss_fn(yb, tb):
        diff = yb - tb
        return (diff * (1.0 / D_MODEL),), (jnp.sum(diff * diff, axis=0, keepdims=True),)

    (dh,), (sq,) = _rowwise(loss_fn, [_full(h), _full(target)], [], [(D_MODEL, F32)], [(1, D_MODEL)], ts=512, name="loss")
    loss = 0.5 * jnp.sum(sq) * (1.0 / D_MODEL)
    grads = [None] * N_LAYERS
    for l in reversed(range(N_LAYERS)):
        dh, grads[l] = _layer_bwd(dh, saved[l], ws[l], cos, sin, functools.partial(emit, l))
    return loss, dh, grads


def _place():
    return lax.axis_index("x"), lax.axis_index("y"), lax.axis_index("c")


def _all_gather(x, after, name):
    def body(x_ref, after_ref, out_ref, send_sems, recv_sems, local_sem):
        mx, my, mc = _place()
        me, sibling = (mx, my, mc), (mx, my, 1 - mc)
        chips = [(1 - mx, my), (mx, 1 - my), (1 - mx, 1 - my)]

        def slot(px, py, pc):
            return out_ref.at[4 * px + 2 * py + pc]

        def copy(k, block, to, src=None):
            return pltpu.make_async_remote_copy(
                src_ref=slot(*block) if src is None else src, dst_ref=slot(*block),
                send_sem=send_sems.at[k], recv_sem=recv_sems.at[k], device_id=to, device_id_type=MESH_ID)

        mine = pltpu.make_async_copy(x_ref, slot(*me), local_sem)
        mine.start()
        first = [copy(0, me, sibling, src=x_ref)]
        first += [copy(1 + j, me, (*chip, mc), src=x_ref) for j, chip in enumerate(chips)]
        for cp in first:
            cp.start()
        passed = [copy(4 + j, (*chip, mc), sibling) for j, chip in enumerate(chips)]
        for j, chip in enumerate(chips):
            copy(1 + j, (*chip, mc), me).wait_recv()
            passed[j].start()
        copy(0, sibling, me).wait_recv()
        for j, chip in enumerate(chips):
            copy(4 + j, (*chip, 1 - mc), me).wait_recv()
        for cp in first + passed:
            cp.wait_send()
        mine.wait()

    return pl.pallas_call(
        body, name=name, out_shape=jax.ShapeDtypeStruct((N_DEV,) + x.shape, x.dtype), in_specs=[ANY, ANY],
        out_specs=ANY,
        scratch_shapes=[pltpu.SemaphoreType.DMA((7,)), pltpu.SemaphoreType.DMA((7,)), pltpu.SemaphoreType.DMA],
    )(x, after)


FLIPS = ((1, 0, 0), (0, 1, 0), (1, 1, 0), (0, 0, 1), (1, 0, 1), (0, 1, 1), (1, 1, 1))
HBM_SPEC = pl.BlockSpec(memory_space=pltpu.HBM)
SEM_SPEC = pl.BlockSpec(memory_space=pltpu.SEMAPHORE)
SPLIT_COPY = pltpu.CompilerParams(has_side_effects=pltpu.SideEffectType.DATAFLOW_SIDE_EFFECTING)


def _peers():
    mx, my, mc = _place()
    out = []
    for fx, fy, fc in FLIPS:
        px, py, pc = (1 - mx if fx else mx), (1 - my if fy else my), (1 - mc if fc else mc)
        out.append(((px, py, pc), 4 * px + 2 * py + pc))
    return out


def _flat_copies(scatter, src_refs, land_refs, send_sems, recv_sems):
    mx, my, mc = _place()
    me = 4 * mx + 2 * my + mc
    n = len(src_refs)
    copies = []
    for t in range(n):
        for i, (peer, number) in enumerate(_peers()):
            copies.append(pltpu.make_async_remote_copy(
                src_ref=src_refs[t].at[number] if scatter else src_refs[t],
                dst_ref=land_refs[t].at[i] if scatter else land_refs[t].at[me],
                send_sem=send_sems.at[t * len(FLIPS) + i], recv_sem=recv_sems.at[t * len(FLIPS) + i],
                device_id=peer, device_id_type=MESH_ID))
    return copies


def _flat_start(arrays, scatter, name):
    n = len(arrays)
    slots = len(FLIPS) if scatter else N_DEV
    lands = [lax.empty((slots,) + (a.shape[1:] if scatter else a.shape), a.dtype) for a in arrays]

    def body(*refs):
        src, land, (send_sems, recv_sems), token = refs[:n], refs[n:2 * n], refs[2 * n:2 * n + 2], refs[-1]
        for cp in _flat_copies(scatter, src, land, send_sems, recv_sems):
            cp.start()
        token[...] = jnp.zeros_like(token)

    hbm = [pltpu.HBM(a.shape, a.dtype) for a in arrays] + [pltpu.HBM(a.shape, a.dtype) for a in lands]
    sems = pltpu.SemaphoreType.DMA((n * len(FLIPS),))
    outs = pl.pallas_call(
        body, name=name, out_shape=(sems, sems, *hbm, jax.ShapeDtypeStruct((8, 128), F32)),
        in_specs=[HBM_SPEC] * (2 * n),
        out_specs=(SEM_SPEC, SEM_SPEC, *([HBM_SPEC] * (2 * n)), pl.BlockSpec(memory_space=pltpu.VMEM)),
        input_output_aliases={t: 2 + t for t in range(2 * n)}, compiler_params=SPLIT_COPY,
    )(*[pltpu.with_memory_space_constraint(a, pltpu.HBM) for a in (*arrays, *lands)])
    return types.SimpleNamespace(sems=outs[:2], thru=outs[2:2 + 2 * n], scatter=scatter, n=n), outs[-1][0:1, 0:1]


def _flat_wait(handle, after, name):
    n = handle.n

    def body(*refs):
        src, land, (send_sems, recv_sems) = refs[:n], refs[n:2 * n], refs[2 * n:2 * n + 2]
        for cp in _flat_copies(handle.scatter, src, land, send_sems, recv_sems):
            cp.wait_send()
            cp.wait_recv()

    outs = pl.pallas_call(
        body, name=name, out_shape=tuple(pltpu.HBM(a.shape, a.dtype) for a in handle.thru),
        in_specs=[HBM_SPEC] * (2 * n) + [SEM_SPEC, SEM_SPEC, ANY], out_specs=tuple([HBM_SPEC] * (2 * n)),
        input_output_aliases={t: t for t in range(2 * n)}, compiler_params=SPLIT_COPY,
    )(*handle.thru, *handle.sems, after)
    return outs[:n], outs[n:]


def _adamw(w, g, m, v):
    m2 = ADAM_B1 * m + (1.0 - ADAM_B1) * g
    v2 = ADAM_B2 * v + (1.0 - ADAM_B2) * (g * g)
    m_hat = m2 / (1.0 - ADAM_B1 ** ADAM_STEP)
    v_hat = v2 / (1.0 - ADAM_B2 ** ADAM_STEP)
    return -ADAM_LR * (m_hat / (jnp.sqrt(v_hat) + ADAM_EPS) + ADAM_WD * w), m2, v2


def _adamw_sharded(me, mine, landed, w, m, v, tr, name):
    _, r, c = w.shape
    nt = r // tr
    assert r % tr == 0 and len(mine) == len(landed) == N_LAYERS == 2, name
    per_layer = 1 + len(FLIPS)

    def body(me_ref, *refs):
        terms, (w_ref, m_ref, v_ref), outs = refs[:2 * per_layer], refs[2 * per_layer:2 * per_layer + 3], refs[-4:]
        layer = pl.program_id(0)

        def total(group):
            g = group[0][0].astype(F32)
            for t in group[1:]:
                g = g + t[0].astype(F32)
            return g

        g = jnp.where(layer == 0, total(terms[:per_layer]), total(terms[per_layer:]))
        d, m2, v2 = _adamw(w_ref[0], g, m_ref[0], v_ref[0])
        for o, val in zip(outs, (g, d, m2, v2)):
            o[0] = val

    def held(l):
        return lambda layer, i: jnp.where(layer == l, i, nt - 1 if l == 0 else 0)

    in_specs = []
    for l in range(N_LAYERS):
        rows = held(l)
        in_specs.append(pl.BlockSpec((1, tr, c), functools.partial(lambda layer, i, me_ref, rows: (me_ref[0], rows(layer, i), 0), rows=rows)))
        for k in range(len(FLIPS)):
            in_specs.append(pl.BlockSpec(
                (1, tr, c), functools.partial(lambda layer, i, me_ref, rows, k: (k, rows(layer, i), 0), rows=rows, k=k)))
    tile = pl.BlockSpec((1, tr, c), lambda layer, i, me_ref: (layer, i, 0))
    operands = []
    for l in range(N_LAYERS):
        operands += [mine[l]] + [landed[l]] * len(FLIPS)
    return pl.pallas_call(
        body, name=name, out_shape=[jax.ShapeDtypeStruct(w.shape, F32)] * 4,
        grid_spec=pltpu.PrefetchScalarGridSpec(
            num_scalar_prefetch=1, grid=(N_LAYERS, nt), in_specs=in_specs + [tile] * 3, out_specs=[tile] * 4),
        compiler_params=_params("arbitrary", "arbitrary"),
    )(me, *operands, w, m, v)


def _adamw_replicated(parts, w, m, v, name):
    def body(p_ref, w_ref, m_ref, v_ref, g_ref, d_ref, m2_ref, v2_ref):
        g = p_ref[0]
        for j in range(1, N_DEV):
            g = g + p_ref[j]
        d, m2, v2 = _adamw(w_ref[...], g, m_ref[...], v_ref[...])
        g_ref[...], d_ref[...], m2_ref[...], v2_ref[...] = g, d, m2, v2

    return pl.pallas_call(body, name=name, out_shape=[jax.ShapeDtypeStruct(w.shape, F32)] * 4,
                          compiler_params=pltpu.CompilerParams(vmem_limit_bytes=VMEM_LIMIT_BYTES))(parts, w, m, v)


def _pack_small(vals):
    flat = jnp.concatenate([vals[n].reshape(-1) for n in SMALL_NAMES])
    return jnp.concatenate([flat, jnp.zeros((SMALL_ROWS * D_MODEL - flat.shape[0],), F32)]).reshape(SMALL_ROWS, D_MODEL)


def _unpack_small(packed, shapes):
    flat, out, at = packed.reshape(-1), {}, 0
    for n in SMALL_NAMES:
        size = math.prod(shapes[n])
        out[n] = flat[at:at + size].reshape(shapes[n])
        at += size
    return out


GATHER_GROUPS = ((0, ("w_in",)), (0, ("w_out", "w_up", "conv_w")), (0, ("w_down",)),
                 (1, ("w_in", "w_out", "w_up", "conv_w")), (1, ("w_down",)))
ADAMW_TILE_ROWS = {"w_in": 512, "w_out": 128, "w_up": 256, "w_down": 256, "conv_w": 3}


def _assemble(name, land):
    if name == "w_in":
        return land.transpose(1, 0, 2).reshape(D_MODEL, IN_COLS)
    if name == "conv_w":
        return land.transpose(1, 0, 2).reshape(3, 2 * D_FF)
    if name == "w_up":
        return land
    return land.reshape(-1, D_MODEL)


def _start_gathers(wts, me):
    started, groups = jnp.zeros((1, 1), F32), []
    for gi, (l, names) in enumerate(GATHER_GROUPS):
        blocks = [wts[n][l] if n == "conv_w" else wts[n][l].astype(BF16) for n in names]
        handle, token = _flat_start(blocks, False, name=f"gather_start_{gi}")
        groups.append(types.SimpleNamespace(layer=l, names=names, blocks=blocks, handle=handle, got=None, index=gi))
        started = started + token

    def fetch(l, name, after):
        grp = next(gr for gr in groups if gr.layer == l and name in gr.names)
        if grp.got is None:
            lands = _flat_wait(grp.handle, after, name=f"gather_wait_{grp.index}")[1]
            grp.got = {}
            for n, blk, land in zip(grp.names, grp.blocks, lands):
                own = (me,) + (0,) * blk.ndim
                grp.got[n] = _assemble(n, lax.dynamic_update_slice(land, blk[None], own))
        return grp.got[name]

    return fetch, started


def kernel(x, pre_mix_norm, w_in, v_norm_g, v_norm_b, w_spatial, b_spatial, out_norm_a, out_norm_b, w_out, post_mix_norm, pre_ffn_norm, w_up, conv_w, conv_b, w_down, post_ffn_norm, loss_target, m_pre_mix_norm, m_w_in, m_v_norm_g, m_v_norm_b, m_w_spatial, m_b_spatial, m_out_norm_a, m_out_norm_b, m_w_out, m_post_mix_norm, m_pre_ffn_norm, m_w_up, m_conv_w, m_conv_b, m_w_down, m_post_ffn_norm, v_pre_mix_norm, v_w_in, v_v_norm_g, v_v_norm_b, v_w_spatial, v_b_spatial, v_out_norm_a, v_out_norm_b, v_w_out, v_post_mix_norm, v_pre_ffn_norm, v_w_up, v_conv_w, v_conv_b, v_w_down, v_post_ffn_norm):
    wts = dict(zip(WEIGHT_NAMES, (pre_mix_norm, w_in, v_norm_g, v_norm_b, w_spatial, b_spatial, out_norm_a, out_norm_b,
                                  w_out, post_mix_norm, pre_ffn_norm, w_up, conv_w, conv_b, w_down, post_ffn_norm)))
    mom1 = dict(zip(WEIGHT_NAMES, (m_pre_mix_norm, m_w_in, m_v_norm_g, m_v_norm_b, m_w_spatial, m_b_spatial, m_out_norm_a,
                                   m_out_norm_b, m_w_out, m_post_mix_norm, m_pre_ffn_norm, m_w_up, m_conv_w, m_conv_b,
                                   m_w_down, m_post_ffn_norm)))
    mom2 = dict(zip(WEIGHT_NAMES, (v_pre_mix_norm, v_w_in, v_v_norm_g, v_v_norm_b, v_w_spatial, v_b_spatial, v_out_norm_a,
                                   v_out_norm_b, v_w_out, v_post_mix_norm, v_pre_ffn_norm, v_w_up, v_conv_w, v_conv_b,
                                   v_w_down, v_post_ffn_norm)))
    mx, my, mc = _place()
    me = 4 * mx + 2 * my + mc

    fetch, started = _start_gathers(wts, me)
    scatters = []

    def emit(l, blocks):
        names = tuple(blocks)
        handle, token = _flat_start([blocks[n] for n in names], True, name=f"scatter_start_{l}_{len(scatters) % 2}")
        scatters.append((l, names, handle))
        return token

    loss_local, dx, grads = _local_step(x[0], loss_target[0], fetch, wts, emit, started)
    loss = lax.psum(loss_local, AXES)

    me_arr = jnp.reshape(me, (1,)).astype(jnp.int32)
    big_out = [{}, {}, {}, {}]

    def finish(group, after):
        mine, landed = {}, {}
        for l, names, handle in scatters:
            if names == group:
                sent, lands = _flat_wait(handle, after, name=f"scatter_wait_{l}_{'_'.join(names)}")
                for n, a, b in zip(names, sent, lands):
                    mine[l, n], landed[l, n] = a, b
        for n in group:
            res = _adamw_sharded(me_arr, [mine[l, n] for l in range(N_LAYERS)], [landed[l, n] for l in range(N_LAYERS)],
                                 wts[n], mom1[n], mom2[n], ADAMW_TILE_ROWS[n], name=f"adamw_{n}")
            for kind in range(4):
                big_out[kind][n] = res[kind]
        return res[0]

    early, late = scatters[0][1], scatters[1][1]
    done_early = finish(early, dx)
    small_grads = {n: jnp.stack([g[n].reshape(wts[n].shape[1:]) for g in grads]) for n in SMALL_NAMES}
    everyone = _all_gather(_pack_small(small_grads), done_early, name="gather_small_grads")
    small = [_pack_small({n: t[n] for n in SMALL_NAMES}) for t in (wts, mom1, mom2)]
    small_out = _adamw_replicated(everyone, *small, name="adamw_replicated")
    finish(late, small_out[0])
    small_shapes = {n: wts[n].shape for n in SMALL_NAMES}
    small_out = [_unpack_small(o, small_shapes) for o in small_out]

    outs = [loss, dx[None]]
    for kind in range(4):
        outs += [big_out[kind][n] if n in BIG_NAMES else small_out[kind][n] for n in WEIGHT_NAMES]
    return tuple(outs)
```

```python
import functools
import math
import types

import jax
import jax.numpy as jnp
from jax import lax
from jax.experimental import pallas as pl
from jax.experimental.pallas import tpu as pltpu

F32 = jnp.float32
BF16 = jnp.bfloat16

D_MODEL = 1024
A_WIDTH = 512
A_GROUPS = 4
CHUNK = 128
B_WIDTH = 512
HEAD_DIM = 64
N_HEADS = B_WIDTH // HEAD_DIM
ROT_DIM = 16
ROPE_THETA = 500000.0
BAND = 128
DILATIONS = (1, 4, 16)
IN_COLS = 2560
D_FF = 4096
EPS = 1e-6
NEG_INF = -1e30
N_DEV = 8
N_LAYERS = 2

ADAM_LR = 0.001
ADAM_B1 = 0.9
ADAM_B2 = 0.999
ADAM_EPS = 1e-08
ADAM_WD = 0.01
ADAM_STEP = 10

VMEM_LIMIT_BYTES = 56 * 1024 * 1024
MESH_ID = pl.DeviceIdType.MESH
ANY = pl.BlockSpec(memory_space=pl.ANY)
AXES = ("x", "y", "c")

WEIGHT_NAMES = ("pre_mix_norm", "w_in", "v_norm_g", "v_norm_b", "w_spatial", "b_spatial", "out_norm_a", "out_norm_b",
                "w_out", "post_mix_norm", "pre_ffn_norm", "w_up", "conv_w", "conv_b", "w_down", "post_ffn_norm")
BIG_NAMES = ("w_in", "w_out", "w_up", "w_down", "conv_w")
SMALL_NAMES = tuple(n for n in WEIGHT_NAMES if n not in BIG_NAMES)

PACK_ROWS = {"w_in": 640, "w_out": 256, "w_up": 2048, "w_down": 1024, "conv_w": 6}
CONV_W_PAD = 2
PACKED_F32_ROWS = 4096
PACKED_BF16_ROWS = 3984
SMALL_ROWS = 160


def _params(*sem):
    return pltpu.CompilerParams(dimension_semantics=sem, vmem_limit_bytes=VMEM_LIMIT_BYTES)


def _dotg(a, b, ca, cb):
    return lax.dot_general(a.astype(BF16), b.astype(BF16), (((ca,), (cb,)), ((), ())), preferred_element_type=F32)


@jax.custom_vjp
def _bdot(a, b):
    return _dotg(a, b, 1, 0)


def _bdot_fwd(a, b):
    return _dotg(a, b, 1, 0), (a, b)


def _bdot_bwd(res, g):
    a, b = res
    return _dotg(g, b, 1, 1), _dotg(a, g, 0, 0)


_bdot.defvjp(_bdot_fwd, _bdot_bwd)


def _rms(x, g):
    return x * lax.rsqrt(jnp.mean(x * x, axis=-1, keepdims=True) + EPS) * g


def _layernorm(x, g, b):
    mu = jnp.mean(x, axis=-1, keepdims=True)
    xc = x - mu
    return xc * lax.rsqrt(jnp.mean(xc * xc, axis=-1, keepdims=True) + EPS) * g + b


def _gelu_erf(x):
    return x * (lax.erf(x * (1.0 / math.sqrt(2.0))) + 1.0) * 0.5


def _gelu_tanh(x):
    c = math.sqrt(2.0 / math.pi)
    return 0.5 * x * (1.0 + jnp.tanh(c * (x + 0.044715 * (x * x * x))))


def _gelu_tanh_and_slope(x):
    c, k = math.sqrt(2.0 / math.pi), 0.044715
    x2 = x * x
    t = jnp.tanh(c * (x + k * (x2 * x)))
    half_x, one_t = 0.5 * x, 1.0 + t
    return half_x * one_t, 0.5 * one_t + (half_x * (1.0 - t * t)) * (c + (3.0 * k * c) * x2)


def _rot_half(x):
    width = x.shape[1]
    lane = lax.broadcasted_iota(jnp.int32, x.shape, 1) % HEAD_DIM
    back = pltpu.roll(x, ROT_DIM // 2, 1)
    fwd = pltpu.roll(x, width - ROT_DIM // 2, 1)
    return jnp.where(lane < ROT_DIM // 2, -fwd, jnp.where(lane < ROT_DIM, back, 0.0))


def _split3(z):
    h0 = z.astype(BF16)
    r1 = z - h0.astype(F32)
    h1 = r1.astype(BF16)
    h2 = (r1 - h1.astype(F32)).astype(BF16)
    return h0, h1, h2


def _head_sum(z):
    width = z.shape[1]
    a = lax.broadcasted_iota(jnp.int32, (width, width), 0) // HEAD_DIM
    b = lax.broadcasted_iota(jnp.int32, (width, width), 1) // HEAD_DIM
    ones = jnp.where(a == b, 1.0, 0.0).astype(BF16)
    out = None
    for part in _split3(z):
        t = lax.dot_general(part, ones, (((1,), (0,)), ((), ())), preferred_element_type=F32)
        out = t if out is None else out + t
    return out


MATMUL_VMEM_BUDGET = 40 * 1024 * 1024


def _matmul_tiles(m, n, k, out_bytes):
    tn = n if n <= 1024 else (1280 if n % 1280 == 0 and n % 1024 else 1024)
    tk = k if k <= 1024 else (1280 if k % 1280 == 0 and k % 1024 else 1024)
    tm = m
    while tm > 256:
        blocks = 2 * 2 * (tm * tk + tk * tn) + 2 * out_bytes * tm * tn + (4 * tm * tn if k > tk else 0)
        if blocks <= MATMUL_VMEM_BUDGET and m % tm == 0:
            break
        tm //= 2
    return tm, tn, tk


def _matmul(a, b, *, mode, out_dtype, name, cols=None):
    wide = D_MODEL if cols is not None else None
    if mode == "nn":
        (m, k), (_, n) = a.shape, (b.shape if cols is None else (b.shape[1], cols[1] * wide))
    elif mode == "nt":
        (m, k), (n, _) = a.shape, (b.shape if cols is None else (b.shape[1], cols[1] * wide))
    else:
        (k, m), (_, n) = a.shape, b.shape
    tm, tn, tk = _matmul_tiles(m, n, k, jnp.dtype(out_dtype).itemsize)
    assert m % tm == 0 and n % tn == 0 and k % tk == 0, (name, m, n, k)
    nk = k // tk
    if mode == "nn":
        a_spec = pl.BlockSpec((tm, tk), lambda i, j, kk: (i, kk))
        b_spec = pl.BlockSpec((tk, tn), lambda i, j, kk: (kk, j))
        if cols is not None:
            assert tn == wide
            b_spec = pl.BlockSpec((None, tk, tn), lambda i, j, kk: (cols[0] + j, kk, 0))
        ca, cb = 1, 0
    elif mode == "nt":
        a_spec = pl.BlockSpec((tm, tk), lambda i, j, kk: (i, kk))
        b_spec = pl.BlockSpec((tn, tk), lambda i, j, kk: (j, kk))
        if cols is not None:
            assert tk == wide
            b_spec = pl.BlockSpec((None, tn, tk), lambda i, j, kk: (cols[0] + kk, j, 0))
        ca, cb = 1, 1
    else:
        a_spec = pl.BlockSpec((tk, tm), lambda i, j, kk: (kk, i))
        b_spec = pl.BlockSpec((tk, tn), lambda i, j, kk: (kk, j))
        ca, cb = 0, 0

    def body(a_ref, b_ref, o_ref, *acc):
        kk = pl.program_id(2)
        part = lax.dot_general(a_ref[...], b_ref[...], (((ca,), (cb,)), ((), ())), preferred_element_type=F32)
        if nk == 1:
            o_ref[...] = part.astype(o_ref.dtype)
            return
        acc_ref, = acc

        @pl.when(kk == 0)
        def _():
            acc_ref[...] = part

        @pl.when(kk > 0)
        def _():
            acc_ref[...] += part

        @pl.when(kk == nk - 1)
        def _():
            o_ref[...] = acc_ref[...].astype(o_ref.dtype)

    return pl.pallas_call(
        body, name=name, grid=(m // tm, n // tn, nk),
        in_specs=[a_spec, b_spec], out_specs=pl.BlockSpec((tm, tn), lambda i, j, kk: (i, j)),
        out_shape=jax.ShapeDtypeStruct((m, n), out_dtype),
        scratch_shapes=[pltpu.VMEM((tm, tn), F32)] if nk > 1 else [],
        compiler_params=_params("parallel", "parallel", "arbitrary"),
    )(a, b)


def _matmul_by_destination(a, b_lo, b_hi, *, name, tm=1024, tk=2048):
    (k, m), half = a.shape, N_DEV // 2
    assert b_lo.shape == b_hi.shape == (k, half * D_MODEL) and m % tm == 0 and k % tk == 0, name
    nk = k // tk

    def body(a_ref, lo_ref, hi_ref, o_ref, acc_ref):
        j, kk = pl.program_id(1), pl.program_id(2)

        def step(b_ref):
            part = lax.dot_general(a_ref[...], b_ref[...], (((0,), (0,)), ((), ())), preferred_element_type=F32)

            @pl.when(kk == 0)
            def _():
                acc_ref[...] = part

            @pl.when(kk > 0)
            def _():
                acc_ref[...] += part

        pl.when(j < half)(lambda: step(lo_ref))
        pl.when(j >= half)(lambda: step(hi_ref))

        @pl.when(kk == nk - 1)
        def _():
            o_ref[...] = acc_ref[...].astype(o_ref.dtype)

    lo_spec = pl.BlockSpec((tk, D_MODEL), lambda i, j, kk: (jnp.where(j < half, kk, nk - 1), jnp.minimum(j, half - 1)))
    hi_spec = pl.BlockSpec((tk, D_MODEL), lambda i, j, kk: (jnp.where(j >= half, kk, 0), jnp.maximum(j - half, 0)))
    return pl.pallas_call(
        body, name=name, grid=(m // tm, N_DEV, nk),
        in_specs=[pl.BlockSpec((tk, tm), lambda i, j, kk: (kk, i)), lo_spec, hi_spec],
        out_specs=pl.BlockSpec((None, tm, D_MODEL), lambda i, j, kk: (j, i, 0)),
        out_shape=jax.ShapeDtypeStruct((N_DEV, m, D_MODEL), BF16),
        scratch_shapes=[pltpu.VMEM((tm, D_MODEL), F32)],
        compiler_params=_params("parallel", "parallel", "arbitrary"),
    )(a, b_lo, b_hi)


LANES = 128


def _residues_to_rows(ref, scr, d):
    w = ref.shape[1] // d
    n = ref.shape[0]
    for r in range(d):
        for c in range(w // LANES):
            scr[c, pl.ds(r, n, stride=d), :] = ref[:, r * w + c * LANES:r * w + (c + 1) * LANES].astype(F32)
    return jnp.concatenate([scr[c] for c in range(w // LANES)], axis=1)


def _rows_to_residues(val, ref, scr, d):
    w = val.shape[1]
    n = ref.shape[0]
    for c in range(w // LANES):
        scr[c] = val[:, c * LANES:(c + 1) * LANES].astype(F32)
    for r in range(d):
        for c in range(w // LANES):
            ref[:, r * w + c * LANES:r * w + (c + 1) * LANES] = scr[c, pl.ds(r, n, stride=d), :].astype(ref.dtype)


HEAD_ROWS = 0


def _head_rows_block(z):
    width = z.shape[1]
    a = lax.broadcasted_iota(jnp.int32, (width, LANES), 0)
    b = lax.broadcasted_iota(jnp.int32, (width, LANES), 1)
    pick = jnp.where(a == b * HEAD_DIM, 1.0, 0.0).astype(BF16)
    out = None
    for part in _split3(z):
        t = lax.dot_general(part, pick, (((1,), (0,)), ((), ())), preferred_element_type=F32)
        out = t if out is None else out + t
    return out.T[:N_HEADS, :]


def _rowwise(fn, rows, consts, out_rows, out_acc, *, ts, name):
    rows = [tuple(r) + (1,) * (4 - len(r)) for r in rows]
    out_rows = [tuple(o) + (1,) * (3 - len(o)) for o in out_rows]
    s = rows[0][0].shape[0] * rows[0][3]
    assert s % ts == 0, (name, s, ts)
    n_rows, n_in = len(rows), len(rows) + len(consts)
    n_row = len(out_rows)
    n_out = n_row + len(out_acc)
    moved = [(idx, w) for idx, (_, w, _, d) in enumerate(rows) if d > 1]
    moved += [(n_rows + idx, w) for idx, (w, _, d) in enumerate(out_rows) if d > 1]

    def body(*refs):
        scratch = dict(zip([key for key, _ in moved], refs[n_in + n_out:]))
        vals = []
        for idx, r in enumerate(refs[:n_in]):
            d = rows[idx][3] if idx < n_rows else 1
            vals.append(r[...] if d == 1 else _residues_to_rows(r, scratch[idx], d))
        row_vals, acc_vals = fn(*vals)
        for idx, (r, v) in enumerate(zip(refs[n_in:n_in + n_row], row_vals)):
            d = out_rows[idx][2]
            if d == 1:
                r[...] = v.astype(r.dtype)
            elif d == HEAD_ROWS:
                r[...] = _head_rows_block(v)
            else:
                _rows_to_residues(v, r, scratch[n_rows + idx], d)
        first = pl.program_id(0) == 0
        for r, v in zip(refs[n_in + n_row:n_in + n_out], acc_vals):
            @pl.when(first)
            def _(r=r, v=v):
                r[...] = v

            @pl.when(jnp.logical_not(first))
            def _(r=r, v=v):
                r[...] += v

    in_specs = [pl.BlockSpec((ts // d, d * w), functools.partial(lambda i, cb: (i, cb), cb=cb)) for _, w, cb, d in rows]
    in_specs += [pl.BlockSpec(c.shape, lambda i: (0, 0)) for c in consts]
    out_specs = [pl.BlockSpec((N_HEADS, ts), lambda i: (0, i)) if d == HEAD_ROWS else
                 pl.BlockSpec((ts // d, d * w), lambda i: (i, 0)) for w, _, d in out_rows]
    out_specs += [pl.BlockSpec(sh, lambda i: (0, 0)) for sh in out_acc]
    out_shape = [jax.ShapeDtypeStruct((N_HEADS, s) if d == HEAD_ROWS else (s // d, d * w), dt) for w, dt, d in out_rows]
    out_shape += [jax.ShapeDtypeStruct(sh, F32) for sh in out_acc]
    outs = pl.pallas_call(
        body, name=name, grid=(s // ts,), in_specs=in_specs, out_specs=out_specs, out_shape=out_shape,
        scratch_shapes=[pltpu.VMEM((w // LANES, ts, LANES), F32) for _, w in moved],
        compiler_params=_params("arbitrary" if out_acc else "parallel"),
    )(*[a for a, _, _, _ in rows], *consts)
    return outs[:n_row], outs[n_row:]


def _full(a, d=1):
    return (a, a.shape[1] // d, 0, d)


def _gate_fn(zu, zv, vg, vb, ws0, ws1, ws2, ws3, bfull, ga):
    u = _gelu_erf(zu)
    vn = _layernorm(_gelu_erf(zv), vg, vb)
    p = lax.broadcasted_iota(jnp.int32, (CHUNK, CHUNK), 0)
    q = lax.broadcasted_iota(jnp.int32, (CHUNK, CHUNK), 1)
    tril = jnp.where(q <= p, 1.0, 0.0)
    group = lax.broadcasted_iota(jnp.int32, (1, A_WIDTH), 1) // CHUNK
    sg = bfull
    for g, w in enumerate((ws0, ws1, ws2, ws3)):
        sg = sg + _bdot(w * tril, jnp.where(group == g, vn, 0.0))
    return _rms(u * sg, ga)


def _bias_reduce(dbf, name):
    def body(x_ref, o_ref):
        lane = lax.broadcasted_iota(jnp.int32, (CHUNK, CHUNK), 1)
        out = jnp.zeros((CHUNK, CHUNK), F32)
        for g in range(A_GROUPS):
            out = jnp.where(lane == g, jnp.sum(x_ref[:, g * CHUNK:(g + 1) * CHUNK], axis=1, keepdims=True), out)
        o_ref[...] = out

    return pl.pallas_call(body, name=name, out_shape=jax.ShapeDtypeStruct((CHUNK, CHUNK), F32))(dbf)


def _pair_mask(hh):
    lane = lax.broadcasted_iota(jnp.int32, (1, 2 * HEAD_DIM), 1)
    return (lane >= HEAD_DIM * hh) & (lane < HEAD_DIM * (hh + 1))


def _lane_pick(x2, lm):
    return jnp.max(jnp.where(lm, x2, -jnp.inf), axis=1, keepdims=True)


ATTN_BLOCKS = 2
ATTN_UNITS = [(b, hp, hh) for b in range(ATTN_BLOCKS) for hp in range(N_HEADS // 2) for hh in range(2)]
SCALE = HEAD_DIM ** -0.5


def _attn_specs(nb):
    cur = pl.BlockSpec((ATTN_BLOCKS * BAND, B_WIDTH), lambda r, j: (j, r))
    prev = pl.BlockSpec((BAND, B_WIDTH), lambda r, j: (jnp.maximum(ATTN_BLOCKS * j - 1, 0), r))
    nxt = pl.BlockSpec((BAND, B_WIDTH), lambda r, j: (jnp.minimum(ATTN_BLOCKS * (j + 1), nb - 1), r))
    return cur, prev, nxt


def _pair_cols(hp):
    return slice(2 * HEAD_DIM * hp, 2 * HEAD_DIM * (hp + 1))


def _rows(b):
    return slice(b * BAND, (b + 1) * BAND)


def _with_prev(cur_ref, prev_ref, b, sl):
    if b == 0:
        return jnp.concatenate([prev_ref[:, sl], cur_ref[_rows(0), sl]], axis=0)
    return cur_ref[(b - 1) * BAND:(b + 1) * BAND, sl]


def _with_next(cur_ref, next_ref, b, sl):
    if b == ATTN_BLOCKS - 1:
        return jnp.concatenate([cur_ref[_rows(b), sl], next_ref[:, sl]], axis=0)
    return cur_ref[b * BAND:(b + 2) * BAND, sl]


def _band_valid(other_exists):
    row = lax.broadcasted_iota(jnp.int32, (BAND, 2 * BAND), 0)
    col = lax.broadcasted_iota(jnp.int32, (BAND, 2 * BAND), 1)
    return (col >= row) & (col <= row + BAND), other_exists


def _masked(lm, x):
    return jnp.where(lm, x, jnp.zeros_like(x))


def _attn_fwd(q, k, v, d, name):
    nb = q.shape[0] // BAND
    cur, prev, _ = _attn_specs(nb)

    def body(q_ref, kc_ref, kp_ref, vc_ref, vp_ref, o_ref, l_ref):
        band, has_prev = _band_valid(pl.program_id(1) > 0)
        col = lax.broadcasted_iota(jnp.int32, (BAND, 2 * BAND), 1)
        valid = [band & ((col >= BAND) | has_prev)] + [band] * (ATTN_BLOCKS - 1)

        def scores(unit):
            b, hp, hh = unit
            sl = _pair_cols(hp)
            return _dotg(_masked(_pair_mask(hh), q_ref[_rows(b), sl]), _with_prev(kc_ref, kp_ref, b, sl), 1, 1)

        ahead, half = scores(ATTN_UNITS[0]), None
        for idx, (b, hp, hh) in enumerate(ATTN_UNITS):
            raw = ahead
            if idx + 1 < len(ATTN_UNITS):
                ahead = scores(ATTN_UNITS[idx + 1])
            sl, lm = _pair_cols(hp), _pair_mask(hh)
            s = jnp.where(valid[b], raw * SCALE, NEG_INF)
            m = jnp.max(s, axis=1, keepdims=True)
            p = jnp.exp(s - m)
            den = jnp.sum(p, axis=1, keepdims=True)
            o = _dotg(p, _with_prev(vc_ref, vp_ref, b, sl), 1, 0) / den
            lse = m + jnp.log(den)
            if hh == 0:
                half = (o, lse)
            else:
                o_ref[_rows(b), sl] = jnp.where(lm, o, half[0])
                l_ref[_rows(b), sl] = jnp.where(lm, lse, half[1])

    return pl.pallas_call(
        body, name=name, grid=(d, nb // ATTN_BLOCKS), in_specs=[cur, cur, prev, cur, prev], out_specs=[cur, cur],
        out_shape=[jax.ShapeDtypeStruct(q.shape, F32), jax.ShapeDtypeStruct(q.shape, F32)],
        compiler_params=_params("parallel", "parallel"),
    )(q, k, k, v, v)


def _attn_bwd_q(q, k, v, do, lse, delta, d, name):
    nb = q.shape[0] // BAND
    cur, prev, _ = _attn_specs(nb)

    def body(q_ref, kc_ref, kp_ref, vc_ref, vp_ref, do_ref, l_ref, dl_ref, dq_ref):
        band, has_prev = _band_valid(pl.program_id(1) > 0)
        col = lax.broadcasted_iota(jnp.int32, (BAND, 2 * BAND), 1)
        valid = [band & ((col >= BAND) | has_prev)] + [band] * (ATTN_BLOCKS - 1)

        def products(unit):
            b, hp, hh = unit
            sl, lm = _pair_cols(hp), _pair_mask(hh)
            return (_dotg(_masked(lm, q_ref[_rows(b), sl]), _with_prev(kc_ref, kp_ref, b, sl), 1, 1),
                    _dotg(_masked(lm, do_ref[_rows(b), sl]), _with_prev(vc_ref, vp_ref, b, sl), 1, 1))

        ahead, half = products(ATTN_UNITS[0]), None
        for idx, (b, hp, hh) in enumerate(ATTN_UNITS):
            raw, dp = ahead
            if idx + 1 < len(ATTN_UNITS):
                ahead = products(ATTN_UNITS[idx + 1])
            sl, lm = _pair_cols(hp), _pair_mask(hh)
            s = jnp.where(valid[b], raw * SCALE, NEG_INF)
            p = jnp.exp(s - _lane_pick(l_ref[_rows(b), sl], lm))
            ds = p * (dp - _lane_pick(dl_ref[_rows(b), sl], lm))
            dq = _dotg(ds, _with_prev(kc_ref, kp_ref, b, sl), 1, 0) * SCALE
            if hh == 0:
                half = dq
            else:
                dq_ref[_rows(b), sl] = jnp.where(lm, dq, half)

    return pl.pallas_call(
        body, name=name, grid=(d, nb // ATTN_BLOCKS), in_specs=[cur, cur, prev, cur, prev, cur, cur, cur],
        out_specs=cur,
        out_shape=jax.ShapeDtypeStruct(q.shape, F32),
        compiler_params=_params("parallel", "parallel"),
    )(q, k, k, v, v, do, lse, delta)


def _attn_bwd_kv(q, k, v, do, lse_t, delta_t, d, name):
    nb = q.shape[0] // BAND
    cur, _, nxt = _attn_specs(nb)
    t_cur = pl.BlockSpec((1, N_HEADS, ATTN_BLOCKS * BAND), lambda r, j: (r, 0, j))
    t_nxt = pl.BlockSpec((1, N_HEADS, BAND), lambda r, j: (r, 0, jnp.minimum(ATTN_BLOCKS * (j + 1), nb - 1)))

    def body(k_ref, v_ref, qc_ref, qn_ref, doc_ref, don_ref, lc_ref, ln_ref, dlc_ref, dln_ref, dk_ref, dv_ref):
        band, has_next = _band_valid(pl.program_id(1) < nb // ATTN_BLOCKS - 1)
        col = lax.broadcasted_iota(jnp.int32, (BAND, 2 * BAND), 1)
        valid = [band] * (ATTN_BLOCKS - 1) + [band & ((col < BAND) | has_next)]

        def head_row(c_ref, n_ref, b, h):
            if b == ATTN_BLOCKS - 1:
                return jnp.concatenate([c_ref[0, h:h + 1, b * BAND:(b + 1) * BAND], n_ref[0, h:h + 1, :]], axis=1)
            return c_ref[0, h:h + 1, b * BAND:(b + 2) * BAND]

        def products(unit):
            b, hp, hh = unit
            sl, lm = _pair_cols(hp), _pair_mask(hh)
            return (_dotg(_masked(lm, k_ref[_rows(b), sl]), _with_next(qc_ref, qn_ref, b, sl), 1, 1),
                    _dotg(_masked(lm, v_ref[_rows(b), sl]), _with_next(doc_ref, don_ref, b, sl), 1, 1))

        ahead, half = products(ATTN_UNITS[0]), None
        for idx, (b, hp, hh) in enumerate(ATTN_UNITS):
            raw, dpt = ahead
            if idx + 1 < len(ATTN_UNITS):
                ahead = products(ATTN_UNITS[idx + 1])
            sl, lm, h = _pair_cols(hp), _pair_mask(hh), 2 * hp + hh
            st = jnp.where(valid[b], raw * SCALE, NEG_INF)
            pt = jnp.exp(st - head_row(lc_ref, ln_ref, b, h))
            dv = _dotg(pt, _with_next(doc_ref, don_ref, b, sl), 1, 0)
            dst = pt * (dpt - head_row(dlc_ref, dln_ref, b, h))
            dk = _dotg(dst, _with_next(qc_ref, qn_ref, b, sl), 1, 0) * SCALE
            if hh == 0:
                half = (dk, dv)
            else:
                dk_ref[_rows(b), sl] = jnp.where(lm, dk, half[0])
                dv_ref[_rows(b), sl] = jnp.where(lm, dv, half[1])

    return pl.pallas_call(
        body, name=name, grid=(d, nb // ATTN_BLOCKS),
        in_specs=[cur, cur, cur, nxt, cur, nxt, t_cur, t_nxt, t_cur, t_nxt], out_specs=[cur, cur],
        out_shape=[jax.ShapeDtypeStruct(q.shape, F32), jax.ShapeDtypeStruct(q.shape, F32)],
        compiler_params=_params("parallel", "parallel"),
    )(k, v, q, q, do, do, lse_t, lse_t, delta_t, delta_t)


def _spread_rows(a, d):
    return a.reshape(N_HEADS, a.shape[1] // d, d).transpose(2, 0, 1)


FF_TS = 512
FF_TC = 1024
FF_SUB = 256
HALO = 8
HALO_BF16 = 16
UP_BLOCKS = D_MODEL // FF_TC


def _conv3(ext, w, b):
    return b + w[0:1, :] * pltpu.roll(ext, 2, 0) + w[1:2, :] * pltpu.roll(ext, 1, 0) + w[2:3, :] * ext


def _ffn_specs(s, cols_first):
    nrb = s // FF_TS
    per, per16 = FF_TS // HALO, FF_TS // HALO_BF16

    def mk(block, fn):
        if cols_first:
            return pl.BlockSpec(block, lambda j, i: fn(i, j))
        return pl.BlockSpec(block, lambda i, j: fn(i, j))

    specs = types.SimpleNamespace(
        nrb=nrb, ncb=D_FF // FF_TC,
        row=mk((FF_TS, FF_TC), lambda i, j: (i, j)),
        before=mk((HALO, FF_TC), lambda i, j: (jnp.maximum(i * per - 1, 0), j)),
        after=mk((HALO, FF_TC), lambda i, j: (jnp.minimum((i + 1) * per, nrb * per - 1), j)),
        w=mk((3, FF_TC), lambda i, j: (0, j)),
        b=mk((1, FF_TC), lambda i, j: (0, j)),
        part=mk((HALO, FF_TC), lambda i, j: (i, j)),
        act=mk((FF_TS, D_MODEL), lambda i, j: (i, 0)),
        act_before=mk((HALO_BF16, D_MODEL), lambda i, j: (jnp.maximum(i * per16 - 1, 0), 0)),
        act_after=mk((HALO_BF16, D_MODEL), lambda i, j: (jnp.minimum((i + 1) * per16, nrb * per16 - 1), 0)),
        up_gate=mk((None, D_MODEL, FF_TC), lambda i, j: (j // UP_BLOCKS, 0, j % UP_BLOCKS)),
        up_val=mk((None, D_MODEL, FF_TC), lambda i, j: (N_DEV // 2 + j // UP_BLOCKS, 0, j % UP_BLOCKS)),
        down=mk((FF_TC, D_MODEL), lambda i, j: (j, 0)),
    )
    return specs


def _ffn_up_geglu(h, w_up, wg, wv, bg, bv, name):
    s = h.shape[0]
    sp = _ffn_specs(s, True)

    def body(h_ref, hb_ref, ugw_ref, uvw_ref, wg_ref, wv_ref, bg_ref, bv_ref, ug_ref, uv_ref, y_ref):
        keep = jnp.where(pl.program_id(1) > 0, 1.0, 0.0).astype(BF16)
        hext = jnp.concatenate([hb_ref[...] * keep, h_ref[...]], axis=0)
        n_sub = FF_TC // FF_SUB
        cols = [slice(c * FF_SUB, (c + 1) * FF_SUB) for c in range(n_sub)]
        up = lambda c: (_dotg(hext, ugw_ref[:, cols[c]], 1, 0), _dotg(hext, uvw_ref[:, cols[c]], 1, 0))
        ahead = up(0)
        for c in range(n_sub):
            sl, (eg, ev) = cols[c], ahead
            if c + 1 < n_sub:
                ahead = up(c + 1)
            gate = _conv3(eg, wg_ref[:, sl], bg_ref[:, sl])[HALO_BF16:, :]
            val = _conv3(ev, wv_ref[:, sl], bv_ref[:, sl])[HALO_BF16:, :]
            ug_ref[:, sl] = eg[HALO_BF16:, :]
            uv_ref[:, sl] = ev[HALO_BF16:, :]
            y_ref[:, sl] = (_gelu_tanh(gate) * val).astype(y_ref.dtype)

    return pl.pallas_call(
        body, name=name, grid=(sp.ncb, sp.nrb),
        in_specs=[sp.act, sp.act_before, sp.up_gate, sp.up_val, sp.w, sp.w, sp.b, sp.b],
        out_specs=[sp.row, sp.row, sp.row],
        out_shape=[jax.ShapeDtypeStruct((s, D_FF), F32), jax.ShapeDtypeStruct((s, D_FF), F32),
                   jax.ShapeDtypeStruct((s, D_FF), BF16)],
        compiler_params=_params("parallel", "parallel"),
    )(h, h, w_up, w_up, wg, wv, bg, bv)


def _sum_parts(parts, name):
    n = parts.shape[0] // HALO

    def body(p_ref, o_ref):
        acc = p_ref[0:HALO, :]
        for t in range(1, n):
            acc = acc + p_ref[t * HALO:(t + 1) * HALO, :]
        o_ref[...] = acc

    return pl.pallas_call(body, name=name, out_shape=jax.ShapeDtypeStruct((HALO, parts.shape[1]), F32))(parts)


def _ffn_geglu_bwd(ug, uv, df, w_down, w_up, wg, wv, bg, bv, name):
    s = ug.shape[0]
    sp = _ffn_specs(s, False)
    nrb = sp.nrb
    rows = FF_TS + 2 * HALO
    lo, hi = HALO, HALO + FF_TS

    def body(ug_ref, uv_ref, hg_ref, hv_ref, ng_ref, nv_ref, df_ref, dfn_ref, dw_ref, ugw_ref, uvw_ref,
             wg_ref, wv_ref, bg_ref, bv_ref, dug_ref, duv_ref, dh_ref, dwg_ref, dwv_ref):
        i, j = pl.program_id(0), pl.program_id(1)
        keep_top = jnp.where(i > 0, 1.0, 0.0)
        keep_bot = jnp.where(i < nrb - 1, 1.0, 0.0).astype(BF16)
        dfe = jnp.concatenate([df_ref[...], dfn_ref[...] * keep_bot], axis=0)

        def back(dc, e, w, du_ref, sl):
            up1 = pltpu.roll(dc, rows - 1, 0)
            up2 = pltpu.roll(dc, rows - 2, 0)
            du = (w[2:3, :] * dc + w[1:2, :] * up1 + w[0:1, :] * up2)[lo:hi, :].astype(BF16)
            du_ref[:, sl] = du
            p1, p2 = up1 * e, up2 * e
            colsum = lambda p: jnp.sum(p[lo:hi, :], axis=0, keepdims=True)
            row = lambda p, t: p[t:t + 1, :]
            d_w1 = colsum(p1) + row(p1, lo - 1) - row(p1, hi - 1)
            d_w0 = colsum(p2) + row(p2, lo - 2) + row(p2, lo - 1) - row(p2, hi - 2) - row(p2, hi - 1)
            sums = [d_w0, d_w1, colsum(dc * e), colsum(dc), jnp.zeros((HALO - 4, FF_SUB), F32)]
            return du, jnp.concatenate(sums, axis=0)

        dh = None
        n_sub = FF_TC // FF_SUB
        cols = [slice(c * FF_SUB, (c + 1) * FF_SUB) for c in range(n_sub)]
        d_act = lambda c: _dotg(dfe, dw_ref[cols[c], :], 1, 1)[:FF_TS + HALO, :]
        ahead = d_act(0)
        for c in range(n_sub):
            sl, dy = cols[c], ahead
            if c + 1 < n_sub:
                ahead = d_act(c + 1)
            dye = jnp.concatenate([jnp.zeros((HALO, FF_SUB), F32), dy], axis=0)
            eg = jnp.concatenate([hg_ref[:, sl] * keep_top, ug_ref[:, sl], ng_ref[:, sl]], axis=0)
            ev = jnp.concatenate([hv_ref[:, sl] * keep_top, uv_ref[:, sl], nv_ref[:, sl]], axis=0)
            wg_, wv_ = wg_ref[:, sl], wv_ref[:, sl]
            gate = _conv3(eg, wg_, bg_ref[:, sl])
            val = _conv3(ev, wv_, bv_ref[:, sl])
            act, slope = _gelu_tanh_and_slope(gate)
            dug, dwg_ref[:, sl] = back((dye * val) * slope, eg, wg_, dug_ref, sl)
            duv, dwv_ref[:, sl] = back(dye * act, ev, wv_, duv_ref, sl)
            term = _dotg(dug, ugw_ref[:, sl], 1, 1) + _dotg(duv, uvw_ref[:, sl], 1, 1)
            dh = term if dh is None else dh + term

        @pl.when(j == 0)
        def _():
            dh_ref[...] = dh

        @pl.when(j > 0)
        def _():
            dh_ref[...] += dh

    parts = jax.ShapeDtypeStruct((nrb * HALO, D_FF), F32)
    dug, duv, dh, pg, pv = pl.pallas_call(
        body, name=name, grid=(nrb, sp.ncb),
        in_specs=[sp.row, sp.row, sp.before, sp.before, sp.after, sp.after, sp.act, sp.act_after, sp.down,
                  sp.up_gate, sp.up_val, sp.w, sp.w, sp.b, sp.b],
        out_specs=[sp.row, sp.row, sp.act, sp.part, sp.part],
        out_shape=[jax.ShapeDtypeStruct((s, D_FF), BF16), jax.ShapeDtypeStruct((s, D_FF), BF16),
                   jax.ShapeDtypeStruct((s, D_MODEL), F32), parts, parts],
        compiler_params=_params("parallel", "arbitrary"),
    )(ug, uv, ug, uv, ug, uv, df, df, w_down, w_up, w_up, wg, wv, bg, bv)
    return dug, duv, dh, _sum_parts(pg, name=name + "_sum_gate"), _sum_parts(pv, name=name + "_sum_val")


def _rope_tables(s):
    inv = ROPE_THETA ** (-jnp.arange(0, ROT_DIM, 2, dtype=F32) / ROT_DIM)
    ang = jnp.arange(s, dtype=F32)[:, None] * inv[None, :]
    cos8, sin8 = jnp.cos(ang), jnp.sin(ang)
    rest = HEAD_DIM - ROT_DIM
    cos_h = jnp.concatenate([cos8, cos8, jnp.ones((s, rest), F32)], axis=1)
    sin_h = jnp.concatenate([sin8, sin8, jnp.zeros((s, rest), F32)], axis=1)
    return jnp.tile(cos_h, (1, LANES // HEAD_DIM)), jnp.tile(sin_h, (1, LANES // HEAD_DIM))


def _all_heads(table):
    return jnp.concatenate([table] * (B_WIDTH // LANES), axis=1)


def _layer_fwd(x, w, cos, sin):
    sv = types.SimpleNamespace(x=x)
    (sv.h1,), _ = _rowwise(lambda xb, g: ((_rms(xb, g),), ()), [_full(x)], [w.g_pre], [(D_MODEL, BF16)], [],
                           ts=512, name="pre_mix_norm")
    sv.proj = _matmul(sv.h1, w.big("w_in", sv.h1), mode="nt", out_dtype=F32, name="proj")

    gate_consts = [w.vg, w.vb, *w.ws, w.bfull, w.ga]
    (na,), _ = _rowwise(lambda *a: ((_gate_fn(*a),), ()), [(sv.proj, A_WIDTH, 0), (sv.proj, A_WIDTH, 1)], gate_consts,
                        [(A_WIDTH, BF16)], [], ts=CHUNK, name="gate_fwd")

    def rope_fn(qr, kr, vr, cs, sn):
        cs, sn = _all_heads(cs), _all_heads(sn)
        return (qr * cs + _rot_half(qr) * sn, kr * cs + _rot_half(kr) * sn, vr), ()

    def rope_all(qr, kr, vr, cs, sn):
        return rope_fn(qr, kr, vr, cs, sn)[0] * len(DILATIONS), ()

    qkv, _ = _rowwise(
        rope_all, [(sv.proj, B_WIDTH, 2), (sv.proj, B_WIDTH, 3), (sv.proj, B_WIDTH, 4), _full(cos), _full(sin)], [],
        [(B_WIDTH, BF16, d) for d in DILATIONS for _ in range(3)], [], ts=512, name="rope_fwd")
    sv.qkv = {d: qkv[3 * i:3 * i + 3] for i, d in enumerate(DILATIONS)}

    branch = []
    for d in DILATIONS:
        o, l = _attn_fwd(*sv.qkv[d], d, name=f"attn_fwd_d{d}")
        branch += [_full(o, d), _full(l, d)]

    def combine_fn(o1, l1, o2, l2, o3, l3, nab, gb):
        m = jnp.maximum(jnp.maximum(l1, l2), l3)
        e1, e2, e3 = jnp.exp(l1 - m), jnp.exp(l2 - m), jnp.exp(l3 - m)
        den = e1 + e2 + e3
        ob = (e1 / den) * o1 + (e2 / den) * o2 + (e3 / den) * o3
        mixed = jnp.concatenate([nab, _rms(ob, gb).astype(BF16)], axis=1)
        lse = m + jnp.log(den)
        return (mixed, ob, lse) + (lse,) * len(DILATIONS), ()

    (sv.mixed, sv.ob, sv.lse_rows, *lses), _ = _rowwise(
        combine_fn, branch + [_full(na)], [w.gb],
        [(D_MODEL, BF16), (B_WIDTH, F32), (B_WIDTH, F32, HEAD_ROWS)] + [(B_WIDTH, F32, d) for d in DILATIONS], [],
        ts=512, name="combine")
    sv.lse = dict(zip(DILATIONS, lses))
    sv.y = _matmul(sv.mixed, w.big("w_out", sv.mixed), mode="nn", out_dtype=F32, name="mix_out")

    def mid_fn(xb, yb, g1, g2):
        x1 = xb + _rms(yb, g1)
        return (x1, _rms(x1, g2)), ()

    (sv.x1, sv.h2), _ = _rowwise(mid_fn, [_full(x), _full(sv.y)], [w.g_pm, w.g_pf], [(D_MODEL, F32), (D_MODEL, BF16)], [],
                                 ts=512, name="post_mix_norm")
    conv_w = w.big("conv_w", sv.h2)
    sv.ug, sv.uv, sv.yff = _ffn_up_geglu(sv.h2, w.big("w_up", sv.h2), conv_w[:, :D_FF], conv_w[:, D_FF:],
                                         w.cb_g, w.cb_v, name="ffn_up_geglu")
    sv.f = _matmul(sv.yff, w.big("w_down", sv.yff), mode="nn", out_dtype=F32, name="ffn_down")
    (x2,), _ = _rowwise(lambda xb, fb, g: ((xb + _rms(fb, g),), ()), [_full(sv.x1), _full(sv.f)], [w.g_post],
                        [(D_MODEL, F32)], [], ts=512, name="post_ffn_norm")
    return x2, sv


def _layer_bwd(dx2, sv, w, cos, sin, emit):
    g = {}

    def post_fn(fb, dxb, gp):
        _, vjp = jax.vjp(_rms, fb, gp)
        df, dg = vjp(dxb)
        return (df,), (dg,)

    (df,), (g["post_ffn_norm"],) = _rowwise(post_fn, [_full(sv.f), _full(dx2)], [w.g_post], [(D_MODEL, BF16)],
                                            [(1, D_MODEL)], ts=512, name="post_ffn_norm_bwd")
    big = {"w_down": _matmul(sv.yff, df, mode="tn", out_dtype=BF16, name="ffn_down_dw").reshape(N_DEV, -1, D_MODEL)}
    conv_w = w.big("conv_w", df)
    dug, duv, dh2, dwg, dwv = _ffn_geglu_bwd(sv.ug, sv.uv, df, w.big("w_down", df), w.big("w_up", df),
                                             conv_w[:, :D_FF], conv_w[:, D_FF:], w.cb_g, w.cb_v, name="ffn_geglu_bwd")
    big["conv_w"] = jnp.concatenate([dwg[0:3], dwv[0:3]], axis=1).reshape(3, N_DEV, D_MODEL).transpose(1, 0, 2)
    g["conv_b"] = jnp.concatenate([dwg[3], dwv[3]], axis=0)
    big["w_up"] = _matmul_by_destination(sv.h2, dug, duv, name="ffn_up_dw")
    g_pm = w.g_pm + emit(big)

    def mid_fn(x1b, yb, dhb, dxb, g1, g2):
        _, vjp2 = jax.vjp(_rms, x1b, g2)
        dx1h, dg2 = vjp2(dhb)
        dx1 = dxb + dx1h
        _, vjp1 = jax.vjp(_rms, yb, g1)
        dy, dg1 = vjp1(dx1)
        return (dx1, dy), (dg1, dg2)

    (dx1, dy), (g["post_mix_norm"], g["pre_ffn_norm"]) = _rowwise(
        mid_fn, [_full(sv.x1), _full(sv.y), _full(dh2), _full(dx2)], [g_pm, w.g_pf],
        [(D_MODEL, F32), (D_MODEL, BF16)], [(1, D_MODEL), (1, D_MODEL)], ts=256, name="post_mix_norm_bwd")
    dmixed = _matmul(dy, w.big("w_out", dy), mode="nt", out_dtype=F32, name="mix_out_dx")
    big = {"w_out": _matmul(sv.mixed, dy, mode="tn", out_dtype=BF16, name="mix_out_dw").reshape(N_DEV, -1, D_MODEL)}

    def attn_out_fn(obb, dmb, gb):
        _, vjp = jax.vjp(_rms, obb, gb)
        do, dgb = vjp(dmb)
        delta = _head_sum(do * obb)
        return (delta,) + (do,) * len(DILATIONS) + (delta,) * len(DILATIONS), (dgb,)

    (delta_rows, *outs), (g["out_norm_b"],) = _rowwise(
        attn_out_fn, [_full(sv.ob), (dmixed, B_WIDTH, 1)], [w.gb],
        [(B_WIDTH, F32, HEAD_ROWS)] + [(B_WIDTH, BF16, d) for d in DILATIONS] + [(B_WIDTH, F32, d) for d in DILATIONS],
        [(1, B_WIDTH)], ts=512, name="attn_out_bwd")
    do = dict(zip(DILATIONS, outs[:len(DILATIONS)]))
    delta = dict(zip(DILATIONS, outs[len(DILATIONS):]))
    parts = {"q": [], "k": [], "v": []}
    for d in DILATIONS:
        qv, kv, vv = sv.qkv[d]
        dq = _attn_bwd_q(qv, kv, vv, do[d], sv.lse[d], delta[d], d, name=f"attn_bwd_q_d{d}")
        dk, dv = _attn_bwd_kv(qv, kv, vv, do[d], _spread_rows(sv.lse_rows, d), _spread_rows(delta_rows, d), d,
                              name=f"attn_bwd_kv_d{d}")
        parts["q"].append(_full(dq, d))
        parts["k"].append(_full(dk, d))
        parts["v"].append(_full(dv, d))

    def rope_bwd_fn(q1, q2, q3, k1, k2, k3, v1, v2, v3, cs, sn):
        cs, sn = _all_heads(cs), _all_heads(sn)

        def back(t):
            return t * cs - _rot_half(t * sn)
        return (jnp.concatenate([back(q1 + q2 + q3), back(k1 + k2 + k3), v1 + v2 + v3], axis=1),), ()

    (dzb,), _ = _rowwise(rope_bwd_fn, parts["q"] + parts["k"] + parts["v"] + [_full(cos), _full(sin)], [],
                         [(3 * B_WIDTH, BF16)], [], ts=256, name="rope_bwd")

    gate_consts = [w.vg, w.vb, *w.ws, w.bfull, w.ga]

    def gate_bwd_fn(zu, zv, dna, *consts):
        _, vjp = jax.vjp(_gate_fn, zu, zv, *consts)
        grads = vjp(dna)
        return (jnp.concatenate([grads[0], grads[1]], axis=1),), tuple(grads[2:])

    (dza,), gsmall = _rowwise(
        gate_bwd_fn, [(sv.proj, A_WIDTH, 0), (sv.proj, A_WIDTH, 1), (dmixed, A_WIDTH, 0)], gate_consts,
        [(2 * A_WIDTH, BF16)], [c.shape for c in gate_consts], ts=CHUNK, name="gate_bwd")
    g["v_norm_g"], g["v_norm_b"] = gsmall[0], gsmall[1]
    g["w_spatial"] = jnp.stack(gsmall[2:6])
    g["b_spatial"] = _bias_reduce(gsmall[6], name="bias_reduce")[:, :A_GROUPS].T
    g["out_norm_a"] = gsmall[7]

    dproj = jnp.concatenate([dza, dzb], axis=1)
    dh1 = _matmul(dproj, w.big("w_in", dproj), mode="nn", out_dtype=F32, name="proj_dx")
    big["w_in"] = _matmul(dproj, sv.h1, mode="tn", out_dtype=BF16, name="proj_dw").reshape(N_DEV, -1, D_MODEL)
    g_pre = w.g_pre + emit(big)

    def pre_fn(xb, dhb, dxb, gp):
        _, vjp = jax.vjp(_rms, xb, gp)
        dxh, dg = vjp(dhb)
        return (dxb + dxh,), (dg,)

    (dx,), (g["pre_mix_norm"],) = _rowwise(pre_fn, [_full(sv.x), _full(dh1), _full(dx1)], [g_pre], [(D_MODEL, F32)],
                                           [(1, D_MODEL)], ts=512, name="pre_mix_norm_bwd")
    return dx, g


def _layer_weights(l, full, small):
    row = lambda a: a[l].reshape(1, -1)
    return types.SimpleNamespace(
        big=functools.partial(full, l),
        g_pre=row(small["pre_mix_norm"]), vg=row(small["v_norm_g"]), vb=row(small["v_norm_b"]),
        ws=[small["w_spatial"][l, gi] for gi in range(A_GROUPS)],
        bfull=jnp.repeat(small["b_spatial"][l].T, CHUNK, axis=1),
        ga=row(small["out_norm_a"]), gb=row(small["out_norm_b"]),
        g_pm=row(small["post_mix_norm"]), g_pf=row(small["pre_ffn_norm"]),
        cb_g=small["conv_b"][l][:D_FF].reshape(1, -1), cb_v=small["conv_b"][l][D_FF:].reshape(1, -1),
        g_post=row(small["post_ffn_norm"]))


def _local_step(x, target, full, small, emit, started):
    s = x.shape[0]
    cos, sin = _rope_tables(s)
    ws = [_layer_weights(l, full, small) for l in range(N_LAYERS)]
    ws[0].g_pre = ws[0].g_pre + started
    saved = []
    h = x
    for l in range(N_LAYERS):
        h, sv = _layer_fwd(h, ws[l], cos, sin)
        saved.append(sv)

    def loss_fn(yb, tb):
        diff = yb - tb
        return (diff * (1.0 / D_MODEL),), (jnp.sum(diff * diff, axis=0, keepdims=True),)

    (dh,), (sq,) = _rowwise(loss_fn, [_full(h), _full(target)], [], [(D_MODEL, F32)], [(1, D_MODEL)], ts=512, name="loss")
    loss = 0.5 * jnp.sum(sq) * (1.0 / D_MODEL)
    grads = [None] * N_LAYERS
    for l in reversed(range(N_LAYERS)):
        dh, grads[l] = _layer_bwd(dh, saved[l], ws[l], cos, sin, functools.partial(emit, l))
    return loss, dh, grads


def _place():
    return lax.axis_index("x"), lax.axis_index("y"), lax.axis_index("c")


def _all_gather(x, after, name):
    def body(x_ref, after_ref, out_ref, send_sems, recv_sems, local_sem):
        mx, my, mc = _place()
        me, sibling = (mx, my, mc), (mx, my, 1 - mc)
        chips = [(1 - mx, my), (mx, 1 - my), (1 - mx, 1 - my)]

        def slot(px, py, pc):
            return out_ref.at[4 * px + 2 * py + pc]

        def copy(k, block, to, src=None):
            return pltpu.make_async_remote_copy(
                src_ref=slot(*block) if src is None else src, dst_ref=slot(*block),
                send_sem=send_sems.at[k], recv_sem=recv_sems.at[k], device_id=to, device_id_type=MESH_ID)

        mine = pltpu.make_async_copy(x_ref, slot(*me), local_sem)
        mine.start()
        first = [copy(0, me, sibling, src=x_ref)]
        first += [copy(1 + j, me, (*chip, mc), src=x_ref) for j, chip in enumerate(chips)]
        for cp in first:
            cp.start()
        passed = [copy(4 + j, (*chip, mc), sibling) for j, chip in enumerate(chips)]
        for j, chip in enumerate(chips):
            copy(1 + j, (*chip, mc), me).wait_recv()
            passed[j].start()
        copy(0, sibling, me).wait_recv()
        for j, chip in enumerate(chips):
            copy(4 + j, (*chip, 1 - mc), me).wait_recv()
        for cp in first + passed:
            cp.wait_send()
        mine.wait()

    return pl.pallas_call(
        body, name=name, out_shape=jax.ShapeDtypeStruct((N_DEV,) + x.shape, x.dtype), in_specs=[ANY, ANY],
        out_specs=ANY,
        scratch_shapes=[pltpu.SemaphoreType.DMA((7,)), pltpu.SemaphoreType.DMA((7,)), pltpu.SemaphoreType.DMA],
    )(x, after)


FLIPS = ((1, 0, 0), (0, 1, 0), (1, 1, 0), (0, 0, 1), (1, 0, 1), (0, 1, 1), (1, 1, 1))
HBM_SPEC = pl.BlockSpec(memory_space=pltpu.HBM)
SEM_SPEC = pl.BlockSpec(memory_space=pltpu.SEMAPHORE)
SPLIT_COPY = pltpu.CompilerParams(has_side_effects=pltpu.SideEffectType.DATAFLOW_SIDE_EFFECTING)


def _peers():
    mx, my, mc = _place()
    out = []
    for fx, fy, fc in FLIPS:
        px, py, pc = (1 - mx if fx else mx), (1 - my if fy else my), (1 - mc if fc else mc)
        out.append(((px, py, pc), 4 * px + 2 * py + pc))
    return out


def _flat_copies(scatter, src_refs, land_refs, send_sems, recv_sems):
    mx, my, mc = _place()
    me = 4 * mx + 2 * my + mc
    n = len(src_refs)
    copies = []
    for t in range(n):
        for i, (peer, number) in enumerate(_peers()):
            copies.append(pltpu.make_async_remote_copy(
                src_ref=src_refs[t].at[number] if scatter else src_refs[t],
                dst_ref=land_refs[t].at[i] if scatter else land_refs[t].at[me],
                send_sem=send_sems.at[t * len(FLIPS) + i], recv_sem=recv_sems.at[t * len(FLIPS) + i],
                device_id=peer, device_id_type=MESH_ID))
    return copies


def _flat_start(arrays, scatter, name):
    n = len(arrays)
    slots = len(FLIPS) if scatter else N_DEV
    lands = [lax.empty((slots,) + (a.shape[1:] if scatter else a.shape), a.dtype) for a in arrays]

    def body(*refs):
        src, land, (send_sems, recv_sems), token = refs[:n], refs[n:2 * n], refs[2 * n:2 * n + 2], refs[-1]
        for cp in _flat_copies(scatter, src, land, send_sems, recv_sems):
            cp.start()
        token[...] = jnp.zeros_like(token)

    hbm = [pltpu.HBM(a.shape, a.dtype) for a in arrays] + [pltpu.HBM(a.shape, a.dtype) for a in lands]
    sems = pltpu.SemaphoreType.DMA((n * len(FLIPS),))
    outs = pl.pallas_call(
        body, name=name, out_shape=(sems, sems, *hbm, jax.ShapeDtypeStruct((8, 128), F32)),
        in_specs=[HBM_SPEC] * (2 * n),
        out_specs=(SEM_SPEC, SEM_SPEC, *([HBM_SPEC] * (2 * n)), pl.BlockSpec(memory_space=pltpu.VMEM)),
        input_output_aliases={t: 2 + t for t in range(2 * n)}, compiler_params=SPLIT_COPY,
    )(*[pltpu.with_memory_space_constraint(a, pltpu.HBM) for a in (*arrays, *lands)])
    return types.SimpleNamespace(sems=outs[:2], thru=outs[2:2 + 2 * n], scatter=scatter, n=n), outs[-1][0:1, 0:1]


def _flat_wait(handle, after, name):
    n = handle.n

    def body(*refs):
        src, land, (send_sems, recv_sems) = refs[:n], refs[n:2 * n], refs[2 * n:2 * n + 2]
        for cp in _flat_copies(handle.scatter, src, land, send_sems, recv_sems):
            cp.wait_send()
            cp.wait_recv()

    outs = pl.pallas_call(
        body, name=name, out_shape=tuple(pltpu.HBM(a.shape, a.dtype) for a in handle.thru),
        in_specs=[HBM_SPEC] * (2 * n) + [SEM_SPEC, SEM_SPEC, ANY], out_specs=tuple([HBM_SPEC] * (2 * n)),
        input_output_aliases={t: t for t in range(2 * n)}, compiler_params=SPLIT_COPY,
    )(*handle.thru, *handle.sems, after)
    return outs[:n], outs[n:]


def _adamw(w, g, m, v):
    m2 = ADAM_B1 * m + (1.0 - ADAM_B1) * g
    v2 = ADAM_B2 * v + (1.0 - ADAM_B2) * (g * g)
    m_hat = m2 / (1.0 - ADAM_B1 ** ADAM_STEP)
    v_hat = v2 / (1.0 - ADAM_B2 ** ADAM_STEP)
    return -ADAM_LR * (m_hat / (jnp.sqrt(v_hat) + ADAM_EPS) + ADAM_WD * w), m2, v2


def _adamw_sharded(me, mine, landed, w, m, v, tr, name):
    _, r, c = w.shape
    nt = r // tr
    assert r % tr == 0 and len(mine) == len(landed) == N_LAYERS == 2, name
    per_layer = 1 + len(FLIPS)

    def body(me_ref, *refs):
        terms, (w_ref, m_ref, v_ref), outs = refs[:2 * per_layer], refs[2 * per_layer:2 * per_layer + 3], refs[-4:]
        layer = pl.program_id(0)

        def total(group):
            g = group[0][0].astype(F32)
            for t in group[1:]:
                g = g + t[0].astype(F32)
            return g

        g = jnp.where(layer == 0, total(terms[:per_layer]), total(terms[per_layer:]))
        d, m2, v2 = _adamw(w_ref[0], g, m_ref[0], v_ref[0])
        for o, val in zip(outs, (g, d, m2, v2)):
            o[0] = val

    def held(l):
        return lambda layer, i: jnp.where(layer == l, i, nt - 1 if l == 0 else 0)

    in_specs = []
    for l in range(N_LAYERS):
        rows = held(l)
        in_specs.append(pl.BlockSpec((1, tr, c), functools.partial(lambda layer, i, me_ref, rows: (me_ref[0], rows(layer, i), 0), rows=rows)))
        for k in range(len(FLIPS)):
            in_specs.append(pl.BlockSpec(
                (1, tr, c), functools.partial(lambda layer, i, me_ref, rows, k: (k, rows(layer, i), 0), rows=rows, k=k)))
    tile = pl.BlockSpec((1, tr, c), lambda layer, i, me_ref: (layer, i, 0))
    operands = []
    for l in range(N_LAYERS):
        operands += [mine[l]] + [landed[l]] * len(FLIPS)
    return pl.pallas_call(
        body, name=name, out_shape=[jax.ShapeDtypeStruct(w.shape, F32)] * 4,
        grid_spec=pltpu.PrefetchScalarGridSpec(
            num_scalar_prefetch=1, grid=(N_LAYERS, nt), in_specs=in_specs + [tile] * 3, out_specs=[tile] * 4),
        compiler_params=_params("arbitrary", "arbitrary"),
    )(me, *operands, w, m, v)


def _adamw_replicated(parts, w, m, v, name):
    def body(p_ref, w_ref, m_ref, v_ref, g_ref, d_ref, m2_ref, v2_ref):
        g = p_ref[0]
        for j in range(1, N_DEV):
            g = g + p_ref[j]
        d, m2, v2 = _adamw(w_ref[...], g, m_ref[...], v_ref[...])
        g_ref[...], d_ref[...], m2_ref[...], v2_ref[...] = g, d, m2, v2

    return pl.pallas_call(body, name=name, out_shape=[jax.ShapeDtypeStruct(w.shape, F32)] * 4,
                          compiler_params=pltpu.CompilerParams(vmem_limit_bytes=VMEM_LIMIT_BYTES))(parts, w, m, v)


def _pack_small(vals):
    flat = jnp.concatenate([vals[n].reshape(-1) for n in SMALL_NAMES])
    return jnp.concatenate([flat, jnp.zeros((SMALL_ROWS * D_MODEL - flat.shape[0],), F32)]).reshape(SMALL_ROWS, D_MODEL)


def _unpack_small(packed, shapes):
    flat, out, at = packed.reshape(-1), {}, 0
    for n in SMALL_NAMES:
        size = math.prod(shapes[n])
        out[n] = flat[at:at + size].reshape(shapes[n])
        at += size
    return out


GATHER_GROUPS = ((0, ("w_in",)), (0, ("w_out", "w_up", "conv_w")), (0, ("w_down",)),
                 (1, ("w_in", "w_out", "w_up", "conv_w")), (1, ("w_down",)))
ADAMW_TILE_ROWS = {"w_in": 320, "w_out": 128, "w_up": 256, "w_down": 256, "conv_w": 3}


def _assemble(name, land):
    if name == "w_in":
        return land.reshape(IN_COLS, D_MODEL)
    if name == "conv_w":
        return land.transpose(1, 0, 2).reshape(3, 2 * D_FF)
    if name == "w_up":
        return land
    return land.reshape(-1, D_MODEL)


def _start_gathers(wts, me):
    started, groups = jnp.zeros((1, 1), F32), []
    for gi, (l, names) in enumerate(GATHER_GROUPS):
        local = {"conv_w": lambda a: a, "w_in": lambda a: a.T.astype(BF16)}
        blocks = [local.get(n, lambda a: a.astype(BF16))(wts[n][l]) for n in names]
        handle, token = _flat_start(blocks, False, name=f"gather_start_{gi}")
        groups.append(types.SimpleNamespace(layer=l, names=names, blocks=blocks, handle=handle, got=None, index=gi))
        started = started + token

    def fetch(l, name, after):
        grp = next(gr for gr in groups if gr.layer == l and name in gr.names)
        if grp.got is None:
            lands = _flat_wait(grp.handle, after, name=f"gather_wait_{grp.index}")[1]
            grp.got = {}
            for n, blk, land in zip(grp.names, grp.blocks, lands):
                own = (me,) + (0,) * blk.ndim
                grp.got[n] = _assemble(n, lax.dynamic_update_slice(land, blk[None], own))
        return grp.got[name]

    return fetch, started


def kernel(x, pre_mix_norm, w_in, v_norm_g, v_norm_b, w_spatial, b_spatial, out_norm_a, out_norm_b, w_out, post_mix_norm, pre_ffn_norm, w_up, conv_w, conv_b, w_down, post_ffn_norm, loss_target, m_pre_mix_norm, m_w_in, m_v_norm_g, m_v_norm_b, m_w_spatial, m_b_spatial, m_out_norm_a, m_out_norm_b, m_w_out, m_post_mix_norm, m_pre_ffn_norm, m_w_up, m_conv_w, m_conv_b, m_w_down, m_post_ffn_norm, v_pre_mix_norm, v_w_in, v_v_norm_g, v_v_norm_b, v_w_spatial, v_b_spatial, v_out_norm_a, v_out_norm_b, v_w_out, v_post_mix_norm, v_pre_ffn_norm, v_w_up, v_conv_w, v_conv_b, v_w_down, v_post_ffn_norm):
    wts = dict(zip(WEIGHT_NAMES, (pre_mix_norm, w_in, v_norm_g, v_norm_b, w_spatial, b_spatial, out_norm_a, out_norm_b,
                                  w_out, post_mix_norm, pre_ffn_norm, w_up, conv_w, conv_b, w_down, post_ffn_norm)))
    mom1 = dict(zip(WEIGHT_NAMES, (m_pre_mix_norm, m_w_in, m_v_norm_g, m_v_norm_b, m_w_spatial, m_b_spatial, m_out_norm_a,
                                   m_out_norm_b, m_w_out, m_post_mix_norm, m_pre_ffn_norm, m_w_up, m_conv_w, m_conv_b,
                                   m_w_down, m_post_ffn_norm)))
    mom2 = dict(zip(WEIGHT_NAMES, (v_pre_mix_norm, v_w_in, v_v_norm_g, v_v_norm_b, v_w_spatial, v_b_spatial, v_out_norm_a,
                                   v_out_norm_b, v_w_out, v_post_mix_norm, v_pre_ffn_norm, v_w_up, v_conv_w, v_conv_b,
                                   v_w_down, v_post_ffn_norm)))
    mx, my, mc = _place()
    me = 4 * mx + 2 * my + mc

    fetch, started = _start_gathers(wts, me)
    scatters = []

    def emit(l, blocks):
        names = tuple(blocks)
        handle, token = _flat_start([blocks[n] for n in names], True, name=f"scatter_start_{l}_{len(scatters) % 2}")
        scatters.append((l, names, handle))
        return token

    loss_local, dx, grads = _local_step(x[0], loss_target[0], fetch, wts, emit, started)
    loss = lax.psum(loss_local, AXES)

    me_arr = jnp.reshape(me, (1,)).astype(jnp.int32)
    big_out = [{}, {}, {}, {}]

    def finish(group, after):
        mine, landed = {}, {}
        for l, names, handle in scatters:
            if names == group:
                sent, lands = _flat_wait(handle, after, name=f"scatter_wait_{l}_{'_'.join(names)}")
                for n, a, b in zip(names, sent, lands):
                    mine[l, n], landed[l, n] = a, b
        for n in group:
            flip = (lambda a: a.transpose(0, 2, 1)) if n == "w_in" else (lambda a: a)
            res = _adamw_sharded(me_arr, [mine[l, n] for l in range(N_LAYERS)], [landed[l, n] for l in range(N_LAYERS)],
                                 flip(wts[n]), flip(mom1[n]), flip(mom2[n]), ADAMW_TILE_ROWS[n], name=f"adamw_{n}")
            for kind in range(4):
                big_out[kind][n] = flip(res[kind])
        return res[0]

    early, late = scatters[0][1], scatters[1][1]
    done_early = finish(early, dx)
    small_grads = {n: jnp.stack([g[n].reshape(wts[n].shape[1:]) for g in grads]) for n in SMALL_NAMES}
    everyone = _all_gather(_pack_small(small_grads), done_early, name="gather_small_grads")
    small = [_pack_small({n: t[n] for n in SMALL_NAMES}) for t in (wts, mom1, mom2)]
    small_out = _adamw_replicated(everyone, *small, name="adamw_replicated")
    finish(late, small_out[0])
    small_shapes = {n: wts[n].shape for n in SMALL_NAMES}
    small_out = [_unpack_small(o, small_shapes) for o in small_out]

    outs = [loss, dx[None]]
    for kind in range(4):
        outs += [big_out[kind][n] if n in BIG_NAMES else small_out[kind][n] for n in WEIGHT_NAMES]
    return tuple(outs)
```

```python
import functools
import math
import types

import jax
import jax.numpy as jnp
from jax import lax
from jax.experimental import pallas as pl
from jax.experimental.pallas import tpu as pltpu

F32 = jnp.float32
BF16 = jnp.bfloat16

D_MODEL = 1024
A_WIDTH = 512
A_GROUPS = 4
CHUNK = 128
B_WIDTH = 512
HEAD_DIM = 64
N_HEADS = B_WIDTH // HEAD_DIM
ROT_DIM = 16
ROPE_THETA = 500000.0
BAND = 128
DILATIONS = (1, 4, 16)
IN_COLS = 2560
D_FF = 4096
EPS = 1e-6
NEG_INF = -1e30
N_DEV = 8
N_LAYERS = 2

ADAM_LR = 0.001
ADAM_B1 = 0.9
ADAM_B2 = 0.999
ADAM_EPS = 1e-08
ADAM_WD = 0.01
ADAM_STEP = 10

VMEM_LIMIT_BYTES = 56 * 1024 * 1024
MESH_ID = pl.DeviceIdType.MESH
ANY = pl.BlockSpec(memory_space=pl.ANY)
AXES = ("x", "y", "c")

WEIGHT_NAMES = ("pre_mix_norm", "w_in", "v_norm_g", "v_norm_b", "w_spatial", "b_spatial", "out_norm_a", "out_norm_b",
                "w_out", "post_mix_norm", "pre_ffn_norm", "w_up", "conv_w", "conv_b", "w_down", "post_ffn_norm")
BIG_NAMES = ("w_in", "w_out", "w_up", "w_down", "conv_w")
SMALL_NAMES = tuple(n for n in WEIGHT_NAMES if n not in BIG_NAMES)

PACK_ROWS = {"w_in": 640, "w_out": 256, "w_up": 2048, "w_down": 1024, "conv_w": 6}
CONV_W_PAD = 2
PACKED_F32_ROWS = 4096
PACKED_BF16_ROWS = 3984
SMALL_ROWS = 160


def _params(*sem):
    return pltpu.CompilerParams(dimension_semantics=sem, vmem_limit_bytes=VMEM_LIMIT_BYTES)


def _dotg(a, b, ca, cb):
    return lax.dot_general(a.astype(BF16), b.astype(BF16), (((ca,), (cb,)), ((), ())), preferred_element_type=F32)


@jax.custom_vjp
def _bdot(a, b):
    return _dotg(a, b, 1, 0)


def _bdot_fwd(a, b):
    return _dotg(a, b, 1, 0), (a, b)


def _bdot_bwd(res, g):
    a, b = res
    return _dotg(g, b, 1, 1), _dotg(a, g, 0, 0)


_bdot.defvjp(_bdot_fwd, _bdot_bwd)


def _rms(x, g):
    return x * lax.rsqrt(jnp.mean(x * x, axis=-1, keepdims=True) + EPS) * g


def _layernorm(x, g, b):
    mu = jnp.mean(x, axis=-1, keepdims=True)
    xc = x - mu
    return xc * lax.rsqrt(jnp.mean(xc * xc, axis=-1, keepdims=True) + EPS) * g + b


def _gelu_erf(x):
    return x * (lax.erf(x * (1.0 / math.sqrt(2.0))) + 1.0) * 0.5


def _gelu_tanh(x):
    c = math.sqrt(2.0 / math.pi)
    return 0.5 * x * (1.0 + jnp.tanh(c * (x + 0.044715 * (x * x * x))))


def _gelu_tanh_and_slope(x):
    c, k = math.sqrt(2.0 / math.pi), 0.044715
    x2 = x * x
    t = jnp.tanh(c * (x + k * (x2 * x)))
    half_x, one_t = 0.5 * x, 1.0 + t
    return half_x * one_t, 0.5 * one_t + (half_x * (1.0 - t * t)) * (c + (3.0 * k * c) * x2)


def _rot_half(x):
    width = x.shape[1]
    lane = lax.broadcasted_iota(jnp.int32, x.shape, 1) % HEAD_DIM
    back = pltpu.roll(x, ROT_DIM // 2, 1)
    fwd = pltpu.roll(x, width - ROT_DIM // 2, 1)
    return jnp.where(lane < ROT_DIM // 2, -fwd, jnp.where(lane < ROT_DIM, back, 0.0))


def _split3(z):
    h0 = z.astype(BF16)
    r1 = z - h0.astype(F32)
    h1 = r1.astype(BF16)
    h2 = (r1 - h1.astype(F32)).astype(BF16)
    return h0, h1, h2


def _head_sum(z):
    width = z.shape[1]
    a = lax.broadcasted_iota(jnp.int32, (width, width), 0) // HEAD_DIM
    b = lax.broadcasted_iota(jnp.int32, (width, width), 1) // HEAD_DIM
    ones = jnp.where(a == b, 1.0, 0.0).astype(BF16)
    out = None
    for part in _split3(z):
        t = lax.dot_general(part, ones, (((1,), (0,)), ((), ())), preferred_element_type=F32)
        out = t if out is None else out + t
    return out


MATMUL_VMEM_BUDGET = 40 * 1024 * 1024


def _matmul_tiles(m, n, k, out_bytes):
    tn = n if n <= 1024 else (1280 if n % 1280 == 0 and n % 1024 else 1024)
    tk = k if k <= 1024 else (1280 if k % 1280 == 0 and k % 1024 else 1024)
    tm = m
    while tm > 256:
        blocks = 2 * 2 * (tm * tk + tk * tn) + 2 * out_bytes * tm * tn + (4 * tm * tn if k > tk else 0)
        if blocks <= MATMUL_VMEM_BUDGET and m % tm == 0:
            break
        tm //= 2
    return tm, tn, tk


def _matmul(a, b, *, mode, out_dtype, name, cols=None):
    wide = D_MODEL if cols is not None else None
    if mode == "nn":
        (m, k), (_, n) = a.shape, (b.shape if cols is None else (b.shape[1], cols[1] * wide))
    elif mode == "nt":
        (m, k), (n, _) = a.shape, (b.shape if cols is None else (b.shape[1], cols[1] * wide))
    else:
        (k, m), (_, n) = a.shape, b.shape
    tm, tn, tk = _matmul_tiles(m, n, k, jnp.dtype(out_dtype).itemsize)
    assert m % tm == 0 and n % tn == 0 and k % tk == 0, (name, m, n, k)
    nk = k // tk
    if mode == "nn":
        a_spec = pl.BlockSpec((tm, tk), lambda i, j, kk: (i, kk))
        b_spec = pl.BlockSpec((tk, tn), lambda i, j, kk: (kk, j))
        if cols is not None:
            assert tn == wide
            b_spec = pl.BlockSpec((None, tk, tn), lambda i, j, kk: (cols[0] + j, kk, 0))
        ca, cb = 1, 0
    elif mode == "nt":
        a_spec = pl.BlockSpec((tm, tk), lambda i, j, kk: (i, kk))
        b_spec = pl.BlockSpec((tn, tk), lambda i, j, kk: (j, kk))
        if cols is not None:
            assert tk == wide
            b_spec = pl.BlockSpec((None, tn, tk), lambda i, j, kk: (cols[0] + kk, j, 0))
        ca, cb = 1, 1
    else:
        a_spec = pl.BlockSpec((tk, tm), lambda i, j, kk: (kk, i))
        b_spec = pl.BlockSpec((tk, tn), lambda i, j, kk: (kk, j))
        ca, cb = 0, 0

    def body(a_ref, b_ref, o_ref, *acc):
        kk = pl.program_id(2)
        part = lax.dot_general(a_ref[...], b_ref[...], (((ca,), (cb,)), ((), ())), preferred_element_type=F32)
        if nk == 1:
            o_ref[...] = part.astype(o_ref.dtype)
            return
        acc_ref, = acc

        @pl.when(kk == 0)
        def _():
            acc_ref[...] = part

        @pl.when(kk > 0)
        def _():
            acc_ref[...] += part

        @pl.when(kk == nk - 1)
        def _():
            o_ref[...] = acc_ref[...].astype(o_ref.dtype)

    return pl.pallas_call(
        body, name=name, grid=(m // tm, n // tn, nk),
        in_specs=[a_spec, b_spec], out_specs=pl.BlockSpec((tm, tn), lambda i, j, kk: (i, j)),
        out_shape=jax.ShapeDtypeStruct((m, n), out_dtype),
        scratch_shapes=[pltpu.VMEM((tm, tn), F32)] if nk > 1 else [],
        compiler_params=_params("parallel", "parallel", "arbitrary"),
    )(a, b)


def _matmul_by_destination(a, b_lo, b_hi, *, name, tm=1024, tk=2048):
    (k, m), half = a.shape, N_DEV // 2
    assert b_lo.shape == b_hi.shape == (k, half * D_MODEL) and m % tm == 0 and k % tk == 0, name
    nk = k // tk

    def body(a_ref, lo_ref, hi_ref, o_ref, acc_ref):
        j, kk = pl.program_id(1), pl.program_id(2)

        def step(b_ref):
            part = lax.dot_general(a_ref[...], b_ref[...], (((0,), (0,)), ((), ())), preferred_element_type=F32)

            @pl.when(kk == 0)
            def _():
                acc_ref[...] = part

            @pl.when(kk > 0)
            def _():
                acc_ref[...] += part

        pl.when(j < half)(lambda: step(lo_ref))
        pl.when(j >= half)(lambda: step(hi_ref))

        @pl.when(kk == nk - 1)
        def _():
            o_ref[...] = acc_ref[...].astype(o_ref.dtype)

    lo_spec = pl.BlockSpec((tk, D_MODEL), lambda i, j, kk: (jnp.where(j < half, kk, nk - 1), jnp.minimum(j, half - 1)))
    hi_spec = pl.BlockSpec((tk, D_MODEL), lambda i, j, kk: (jnp.where(j >= half, kk, 0), jnp.maximum(j - half, 0)))
    return pl.pallas_call(
        body, name=name, grid=(m // tm, N_DEV, nk),
        in_specs=[pl.BlockSpec((tk, tm), lambda i, j, kk: (kk, i)), lo_spec, hi_spec],
        out_specs=pl.BlockSpec((None, tm, D_MODEL), lambda i, j, kk: (j, i, 0)),
        out_shape=jax.ShapeDtypeStruct((N_DEV, m, D_MODEL), BF16),
        scratch_shapes=[pltpu.VMEM((tm, D_MODEL), F32)],
        compiler_params=_params("parallel", "parallel", "arbitrary"),
    )(a, b_lo, b_hi)


LANES = 128


def _residues_to_rows(ref, scr, d):
    w = ref.shape[1] // d
    n = ref.shape[0]
    for r in range(d):
        for c in range(w // LANES):
            scr[c, pl.ds(r, n, stride=d), :] = ref[:, r * w + c * LANES:r * w + (c + 1) * LANES].astype(F32)
    return jnp.concatenate([scr[c] for c in range(w // LANES)], axis=1)


def _rows_to_residues(val, ref, scr, d):
    w = val.shape[1]
    n = ref.shape[0]
    for c in range(w // LANES):
        scr[c] = val[:, c * LANES:(c + 1) * LANES].astype(F32)
    for r in range(d):
        for c in range(w // LANES):
            ref[:, r * w + c * LANES:r * w + (c + 1) * LANES] = scr[c, pl.ds(r, n, stride=d), :].astype(ref.dtype)


HEAD_ROWS = 0


def _head_cols(z):
    width = z.shape[1]
    a = lax.broadcasted_iota(jnp.int32, (width, LANES), 0)
    b = lax.broadcasted_iota(jnp.int32, (width, LANES), 1)
    pick = jnp.where(a == b * HEAD_DIM, 1.0, 0.0).astype(BF16)
    out = None
    for part in _split3(z):
        t = lax.dot_general(part, pick, (((1,), (0,)), ((), ())), preferred_element_type=F32)
        out = t if out is None else out + t
    return out


def _head_rows_block(cols):
    return cols.T[:N_HEADS, :]


def _rowwise(fn, rows, consts, out_rows, out_acc, *, ts, name):
    rows = [tuple(r) + (1,) * (4 - len(r)) for r in rows]
    out_rows = [tuple(o) + (1,) * (3 - len(o)) for o in out_rows]
    s = rows[0][0].shape[0] * rows[0][3]
    assert s % ts == 0, (name, s, ts)
    n_rows, n_in = len(rows), len(rows) + len(consts)
    n_row = len(out_rows)
    n_out = n_row + len(out_acc)
    moved = [(idx, w) for idx, (_, w, _, d) in enumerate(rows) if d > 1]
    moved += [(n_rows + idx, w) for idx, (w, _, d) in enumerate(out_rows) if d > 1]

    def body(*refs):
        scratch = dict(zip([key for key, _ in moved], refs[n_in + n_out:]))
        vals = []
        for idx, r in enumerate(refs[:n_in]):
            d = rows[idx][3] if idx < n_rows else 1
            vals.append(r[...] if d == 1 else _residues_to_rows(r, scratch[idx], d))
        row_vals, acc_vals = fn(*vals)
        for idx, (r, v) in enumerate(zip(refs[n_in:n_in + n_row], row_vals)):
            d = out_rows[idx][2]
            if d == 1:
                r[...] = v.astype(r.dtype)
            elif d == HEAD_ROWS:
                r[...] = _head_rows_block(v)
            else:
                _rows_to_residues(v, r, scratch[n_rows + idx], d)
        first = pl.program_id(0) == 0
        for r, v in zip(refs[n_in + n_row:n_in + n_out], acc_vals):
            @pl.when(first)
            def _(r=r, v=v):
                r[...] = v

            @pl.when(jnp.logical_not(first))
            def _(r=r, v=v):
                r[...] += v

    in_specs = [pl.BlockSpec((ts // d, d * w), functools.partial(lambda i, cb: (i, cb), cb=cb)) for _, w, cb, d in rows]
    in_specs += [pl.BlockSpec(c.shape, lambda i: (0, 0)) for c in consts]
    out_specs = [pl.BlockSpec((N_HEADS, ts), lambda i: (0, i)) if d == HEAD_ROWS else
                 pl.BlockSpec((ts // d, d * w), lambda i: (i, 0)) for w, _, d in out_rows]
    out_specs += [pl.BlockSpec(sh, lambda i: (0, 0)) for sh in out_acc]
    out_shape = [jax.ShapeDtypeStruct((N_HEADS, s) if d == HEAD_ROWS else (s // d, d * w), dt) for w, dt, d in out_rows]
    out_shape += [jax.ShapeDtypeStruct(sh, F32) for sh in out_acc]
    outs = pl.pallas_call(
        body, name=name, grid=(s // ts,), in_specs=in_specs, out_specs=out_specs, out_shape=out_shape,
        scratch_shapes=[pltpu.VMEM((w // LANES, ts, LANES), F32) for _, w in moved],
        compiler_params=_params("arbitrary" if out_acc else "parallel"),
    )(*[a for a, _, _, _ in rows], *consts)
    return outs[:n_row], outs[n_row:]


def _full(a, d=1):
    return (a, a.shape[1] // d, 0, d)


def _gate_fn(zu, zv, vg, vb, ws0, ws1, ws2, ws3, bfull, ga):
    u = _gelu_erf(zu)
    vn = _layernorm(_gelu_erf(zv), vg, vb)
    p = lax.broadcasted_iota(jnp.int32, (CHUNK, CHUNK), 0)
    q = lax.broadcasted_iota(jnp.int32, (CHUNK, CHUNK), 1)
    tril = jnp.where(q <= p, 1.0, 0.0)
    group = lax.broadcasted_iota(jnp.int32, (1, A_WIDTH), 1) // CHUNK
    sg = bfull
    for g, w in enumerate((ws0, ws1, ws2, ws3)):
        sg = sg + _bdot(w * tril, jnp.where(group == g, vn, 0.0))
    return _rms(u * sg, ga)


def _bias_reduce(dbf, name):
    def body(x_ref, o_ref):
        lane = lax.broadcasted_iota(jnp.int32, (CHUNK, CHUNK), 1)
        out = jnp.zeros((CHUNK, CHUNK), F32)
        for g in range(A_GROUPS):
            out = jnp.where(lane == g, jnp.sum(x_ref[:, g * CHUNK:(g + 1) * CHUNK], axis=1, keepdims=True), out)
        o_ref[...] = out

    return pl.pallas_call(body, name=name, out_shape=jax.ShapeDtypeStruct((CHUNK, CHUNK), F32))(dbf)


def _pair_mask(hh):
    lane = lax.broadcasted_iota(jnp.int32, (1, 2 * HEAD_DIM), 1)
    return (lane >= HEAD_DIM * hh) & (lane < HEAD_DIM * (hh + 1))


def _lane_pick(x2, lm):
    return jnp.max(jnp.where(lm, x2, -jnp.inf), axis=1, keepdims=True)


ATTN_BLOCKS = 2
ATTN_UNITS = [(b, hp, hh) for b in range(ATTN_BLOCKS) for hp in range(N_HEADS // 2) for hh in range(2)]
SCALE = HEAD_DIM ** -0.5


def _attn_specs(nb):
    cur = pl.BlockSpec((ATTN_BLOCKS * BAND, B_WIDTH), lambda r, j: (j, r))
    prev = pl.BlockSpec((BAND, B_WIDTH), lambda r, j: (jnp.maximum(ATTN_BLOCKS * j - 1, 0), r))
    nxt = pl.BlockSpec((BAND, B_WIDTH), lambda r, j: (jnp.minimum(ATTN_BLOCKS * (j + 1), nb - 1), r))
    return cur, prev, nxt


def _pair_cols(hp):
    return slice(2 * HEAD_DIM * hp, 2 * HEAD_DIM * (hp + 1))


def _rows(b):
    return slice(b * BAND, (b + 1) * BAND)


def _with_prev(cur_ref, prev_ref, b, sl):
    if b == 0:
        return jnp.concatenate([prev_ref[:, sl], cur_ref[_rows(0), sl]], axis=0)
    return cur_ref[(b - 1) * BAND:(b + 1) * BAND, sl]


def _with_next(cur_ref, next_ref, b, sl):
    if b == ATTN_BLOCKS - 1:
        return jnp.concatenate([cur_ref[_rows(b), sl], next_ref[:, sl]], axis=0)
    return cur_ref[b * BAND:(b + 2) * BAND, sl]


def _band_valid(other_exists):
    row = lax.broadcasted_iota(jnp.int32, (BAND, 2 * BAND), 0)
    col = lax.broadcasted_iota(jnp.int32, (BAND, 2 * BAND), 1)
    return (col >= row) & (col <= row + BAND), other_exists


def _masked(lm, x):
    return jnp.where(lm, x, jnp.zeros_like(x))


def _attn_fwd(q, k, v, d, name):
    nb = q.shape[0] // BAND
    cur, prev, _ = _attn_specs(nb)

    def body(q_ref, kc_ref, kp_ref, vc_ref, vp_ref, o_ref, l_ref):
        band, has_prev = _band_valid(pl.program_id(1) > 0)
        col = lax.broadcasted_iota(jnp.int32, (BAND, 2 * BAND), 1)
        valid = [band & ((col >= BAND) | has_prev)] + [band] * (ATTN_BLOCKS - 1)

        def scores(unit):
            b, hp, hh = unit
            sl = _pair_cols(hp)
            return _dotg(_masked(_pair_mask(hh), q_ref[_rows(b), sl]), _with_prev(kc_ref, kp_ref, b, sl), 1, 1)

        ahead, half = scores(ATTN_UNITS[0]), None
        for idx, (b, hp, hh) in enumerate(ATTN_UNITS):
            raw = ahead
            if idx + 1 < len(ATTN_UNITS):
                ahead = scores(ATTN_UNITS[idx + 1])
            sl, lm = _pair_cols(hp), _pair_mask(hh)
            s = jnp.where(valid[b], raw * SCALE, NEG_INF)
            m = jnp.max(s, axis=1, keepdims=True)
            p = jnp.exp(s - m)
            den = jnp.sum(p, axis=1, keepdims=True)
            o = _dotg(p, _with_prev(vc_ref, vp_ref, b, sl), 1, 0) / den
            lse = m + jnp.log(den)
            if hh == 0:
                half = (o, lse)
            else:
                o_ref[_rows(b), sl] = jnp.where(lm, o, half[0])
                l_ref[_rows(b), sl] = jnp.where(lm, lse, half[1])

    return pl.pallas_call(
        body, name=name, grid=(d, nb // ATTN_BLOCKS), in_specs=[cur, cur, prev, cur, prev], out_specs=[cur, cur],
        out_shape=[jax.ShapeDtypeStruct(q.shape, F32), jax.ShapeDtypeStruct(q.shape, F32)],
        compiler_params=_params("parallel", "parallel"),
    )(q, k, k, v, v)


def _attn_bwd_q(q, k, v, do, lse, delta, d, name):
    nb = q.shape[0] // BAND
    cur, prev, _ = _attn_specs(nb)
    per_head = pl.BlockSpec((ATTN_BLOCKS * BAND, LANES), lambda r, j: (j, r))

    def body(q_ref, kc_ref, kp_ref, vc_ref, vp_ref, do_ref, l_ref, dl_ref, dq_ref):
        band, has_prev = _band_valid(pl.program_id(1) > 0)
        col = lax.broadcasted_iota(jnp.int32, (BAND, 2 * BAND), 1)
        valid = [band & ((col >= BAND) | has_prev)] + [band] * (ATTN_BLOCKS - 1)

        def products(unit):
            b, hp, hh = unit
            sl, lm = _pair_cols(hp), _pair_mask(hh)
            return (_dotg(_masked(lm, q_ref[_rows(b), sl]), _with_prev(kc_ref, kp_ref, b, sl), 1, 1),
                    _dotg(_masked(lm, do_ref[_rows(b), sl]), _with_prev(vc_ref, vp_ref, b, sl), 1, 1))

        ahead, half = products(ATTN_UNITS[0]), None
        for idx, (b, hp, hh) in enumerate(ATTN_UNITS):
            raw, dp = ahead
            if idx + 1 < len(ATTN_UNITS):
                ahead = products(ATTN_UNITS[idx + 1])
            sl, lm = _pair_cols(hp), _pair_mask(hh)
            s = jnp.where(valid[b], raw * SCALE, NEG_INF)
            head = lax.broadcasted_iota(jnp.int32, (1, LANES), 1) == 2 * hp + hh
            p = jnp.exp(s - _lane_pick(l_ref[_rows(b), :], head))
            ds = p * (dp - _lane_pick(dl_ref[_rows(b), :], head))
            dq = _dotg(ds, _with_prev(kc_ref, kp_ref, b, sl), 1, 0) * SCALE
            if hh == 0:
                half = dq
            else:
                dq_ref[_rows(b), sl] = jnp.where(lm, dq, half)

    return pl.pallas_call(
        body, name=name, grid=(d, nb // ATTN_BLOCKS),
        in_specs=[cur, cur, prev, cur, prev, cur, per_head, per_head], out_specs=cur,
        out_shape=jax.ShapeDtypeStruct(q.shape, F32),
        compiler_params=_params("parallel", "parallel"),
    )(q, k, k, v, v, do, lse, delta)


def _attn_bwd_kv(q, k, v, do, lse_t, delta_t, d, name):
    nb = q.shape[0] // BAND
    cur, _, nxt = _attn_specs(nb)
    t_cur = pl.BlockSpec((1, N_HEADS, ATTN_BLOCKS * BAND), lambda r, j: (r, 0, j))
    t_nxt = pl.BlockSpec((1, N_HEADS, BAND), lambda r, j: (r, 0, jnp.minimum(ATTN_BLOCKS * (j + 1), nb - 1)))

    def body(k_ref, v_ref, qc_ref, qn_ref, doc_ref, don_ref, lc_ref, ln_ref, dlc_ref, dln_ref, dk_ref, dv_ref):
        band, has_next = _band_valid(pl.program_id(1) < nb // ATTN_BLOCKS - 1)
        col = lax.broadcasted_iota(jnp.int32, (BAND, 2 * BAND), 1)
        valid = [band] * (ATTN_BLOCKS - 1) + [band & ((col < BAND) | has_next)]

        def head_row(c_ref, n_ref, b, h):
            if b == ATTN_BLOCKS - 1:
                return jnp.concatenate([c_ref[0, h:h + 1, b * BAND:(b + 1) * BAND], n_ref[0, h:h + 1, :]], axis=1)
            return c_ref[0, h:h + 1, b * BAND:(b + 2) * BAND]

        def products(unit):
            b, hp, hh = unit
            sl, lm = _pair_cols(hp), _pair_mask(hh)
            return (_dotg(_masked(lm, k_ref[_rows(b), sl]), _with_next(qc_ref, qn_ref, b, sl), 1, 1),
                    _dotg(_masked(lm, v_ref[_rows(b), sl]), _with_next(doc_ref, don_ref, b, sl), 1, 1))

        ahead, half = products(ATTN_UNITS[0]), None
        for idx, (b, hp, hh) in enumerate(ATTN_UNITS):
            raw, dpt = ahead
            if idx + 1 < len(ATTN_UNITS):
                ahead = products(ATTN_UNITS[idx + 1])
            sl, lm, h = _pair_cols(hp), _pair_mask(hh), 2 * hp + hh
            st = jnp.where(valid[b], raw * SCALE, NEG_INF)
            pt = jnp.exp(st - head_row(lc_ref, ln_ref, b, h))
            dv = _dotg(pt, _with_next(doc_ref, don_ref, b, sl), 1, 0)
            dst = pt * (dpt - head_row(dlc_ref, dln_ref, b, h))
            dk = _dotg(dst, _with_next(qc_ref, qn_ref, b, sl), 1, 0) * SCALE
            if hh == 0:
                half = (dk, dv)
            else:
                dk_ref[_rows(b), sl] = jnp.where(lm, dk, half[0])
                dv_ref[_rows(b), sl] = jnp.where(lm, dv, half[1])

    return pl.pallas_call(
        body, name=name, grid=(d, nb // ATTN_BLOCKS),
        in_specs=[cur, cur, cur, nxt, cur, nxt, t_cur, t_nxt, t_cur, t_nxt], out_specs=[cur, cur],
        out_shape=[jax.ShapeDtypeStruct(q.shape, F32), jax.ShapeDtypeStruct(q.shape, F32)],
        compiler_params=_params("parallel", "parallel"),
    )(k, v, q, q, do, do, lse_t, lse_t, delta_t, delta_t)


def _spread_rows(a, d):
    return a.reshape(N_HEADS, a.shape[1] // d, d).transpose(2, 0, 1)


FF_TS = 512
FF_TC = 1024
FF_SUB = 256
HALO = 8
HALO_BF16 = 16
UP_BLOCKS = D_MODEL // FF_TC


def _conv3(ext, w, b):
    return b + w[0:1, :] * pltpu.roll(ext, 2, 0) + w[1:2, :] * pltpu.roll(ext, 1, 0) + w[2:3, :] * ext


def _ffn_specs(s, cols_first):
    nrb = s // FF_TS
    per, per16 = FF_TS // HALO, FF_TS // HALO_BF16

    def mk(block, fn):
        if cols_first:
            return pl.BlockSpec(block, lambda j, i: fn(i, j))
        return pl.BlockSpec(block, lambda i, j: fn(i, j))

    specs = types.SimpleNamespace(
        nrb=nrb, ncb=D_FF // FF_TC,
        row=mk((FF_TS, FF_TC), lambda i, j: (i, j)),
        before=mk((HALO, FF_TC), lambda i, j: (jnp.maximum(i * per - 1, 0), j)),
        after=mk((HALO, FF_TC), lambda i, j: (jnp.minimum((i + 1) * per, nrb * per - 1), j)),
        w=mk((3, FF_TC), lambda i, j: (0, j)),
        b=mk((1, FF_TC), lambda i, j: (0, j)),
        part=mk((HALO, FF_TC), lambda i, j: (i, j)),
        act=mk((FF_TS, D_MODEL), lambda i, j: (i, 0)),
        act_before=mk((HALO_BF16, D_MODEL), lambda i, j: (jnp.maximum(i * per16 - 1, 0), 0)),
        act_after=mk((HALO_BF16, D_MODEL), lambda i, j: (jnp.minimum((i + 1) * per16, nrb * per16 - 1), 0)),
        up_gate=mk((None, D_MODEL, FF_TC), lambda i, j: (j // UP_BLOCKS, 0, j % UP_BLOCKS)),
        up_val=mk((None, D_MODEL, FF_TC), lambda i, j: (N_DEV // 2 + j // UP_BLOCKS, 0, j % UP_BLOCKS)),
        down=mk((FF_TC, D_MODEL), lambda i, j: (j, 0)),
    )
    return specs


def _ffn_up_geglu(h, w_up, wg, wv, bg, bv, name):
    s = h.shape[0]
    sp = _ffn_specs(s, True)

    def body(h_ref, hb_ref, ugw_ref, uvw_ref, wg_ref, wv_ref, bg_ref, bv_ref, ug_ref, uv_ref, y_ref):
        keep = jnp.where(pl.program_id(1) > 0, 1.0, 0.0).astype(BF16)
        hext = jnp.concatenate([hb_ref[...] * keep, h_ref[...]], axis=0)
        n_sub = FF_TC // FF_SUB
        cols = [slice(c * FF_SUB, (c + 1) * FF_SUB) for c in range(n_sub)]
        up_gate = lambda c: _dotg(hext, ugw_ref[:, cols[c]], 1, 0)
        up_val = lambda c: _dotg(hext, uvw_ref[:, cols[c]], 1, 0)
        eg, ev = up_gate(0), up_val(0)
        for c in range(n_sub):
            sl, more = cols[c], c + 1 < n_sub
            eg_next = up_gate(c + 1) if more else None
            ug_ref[:, sl] = eg[HALO_BF16:, :]
            act = _gelu_tanh(_conv3(eg, wg_ref[:, sl], bg_ref[:, sl])[HALO_BF16:, :])
            ev_next = up_val(c + 1) if more else None
            uv_ref[:, sl] = ev[HALO_BF16:, :]
            y_ref[:, sl] = (act * _conv3(ev, wv_ref[:, sl], bv_ref[:, sl])[HALO_BF16:, :]).astype(y_ref.dtype)
            eg, ev = eg_next, ev_next

    return pl.pallas_call(
        body, name=name, grid=(sp.ncb, sp.nrb),
        in_specs=[sp.act, sp.act_before, sp.up_gate, sp.up_val, sp.w, sp.w, sp.b, sp.b],
        out_specs=[sp.row, sp.row, sp.row],
        out_shape=[jax.ShapeDtypeStruct((s, D_FF), F32), jax.ShapeDtypeStruct((s, D_FF), F32),
                   jax.ShapeDtypeStruct((s, D_FF), BF16)],
        compiler_params=_params("parallel", "parallel"),
    )(h, h, w_up, w_up, wg, wv, bg, bv)


def _sum_parts(parts, name):
    n = parts.shape[0] // HALO

    def body(p_ref, o_ref):
        acc = p_ref[0:HALO, :]
        for t in range(1, n):
            acc = acc + p_ref[t * HALO:(t + 1) * HALO, :]
        o_ref[...] = acc

    return pl.pallas_call(body, name=name, out_shape=jax.ShapeDtypeStruct((HALO, parts.shape[1]), F32))(parts)


def _ffn_geglu_bwd(ug, uv, df, w_down, w_up, wg, wv, bg, bv, name):
    s = ug.shape[0]
    sp = _ffn_specs(s, False)
    nrb = sp.nrb
    rows = FF_TS + 2 * HALO
    lo, hi = HALO, HALO + FF_TS

    def body(ug_ref, uv_ref, hg_ref, hv_ref, ng_ref, nv_ref, df_ref, dfn_ref, dw_ref, ugw_ref, uvw_ref,
             wg_ref, wv_ref, bg_ref, bv_ref, dug_ref, duv_ref, dh_ref, dwg_ref, dwv_ref):
        i, j = pl.program_id(0), pl.program_id(1)
        keep_top = jnp.where(i > 0, 1.0, 0.0)
        keep_bot = jnp.where(i < nrb - 1, 1.0, 0.0).astype(BF16)
        dfe = jnp.concatenate([df_ref[...], dfn_ref[...] * keep_bot], axis=0)

        def back(dc, e, w, du_ref, sl):
            up1 = pltpu.roll(dc, rows - 1, 0)
            up2 = pltpu.roll(dc, rows - 2, 0)
            du = (w[2:3, :] * dc + w[1:2, :] * up1 + w[0:1, :] * up2)[lo:hi, :].astype(BF16)
            du_ref[:, sl] = du
            p1, p2 = up1 * e, up2 * e
            colsum = lambda p: jnp.sum(p[lo:hi, :], axis=0, keepdims=True)
            row = lambda p, t: p[t:t + 1, :]
            d_w1 = colsum(p1) + row(p1, lo - 1) - row(p1, hi - 1)
            d_w0 = colsum(p2) + row(p2, lo - 2) + row(p2, lo - 1) - row(p2, hi - 2) - row(p2, hi - 1)
            sums = [d_w0, d_w1, colsum(dc * e), colsum(dc), jnp.zeros((HALO - 4, FF_SUB), F32)]
            return du, jnp.concatenate(sums, axis=0)

        dh = None
        n_sub = FF_TC // FF_SUB
        cols = [slice(c * FF_SUB, (c + 1) * FF_SUB) for c in range(n_sub)]
        d_act = lambda c: _dotg(dfe, dw_ref[cols[c], :], 1, 1)[:FF_TS + HALO, :]
        ahead = d_act(0)
        for c in range(n_sub):
            sl, dy = cols[c], ahead
            if c + 1 < n_sub:
                ahead = d_act(c + 1)
            dye = jnp.concatenate([jnp.zeros((HALO, FF_SUB), F32), dy], axis=0)
            eg = jnp.concatenate([hg_ref[:, sl] * keep_top, ug_ref[:, sl], ng_ref[:, sl]], axis=0)
            ev = jnp.concatenate([hv_ref[:, sl] * keep_top, uv_ref[:, sl], nv_ref[:, sl]], axis=0)
            wg_, wv_ = wg_ref[:, sl], wv_ref[:, sl]
            gate = _conv3(eg, wg_, bg_ref[:, sl])
            val = _conv3(ev, wv_, bv_ref[:, sl])
            act, slope = _gelu_tanh_and_slope(gate)
            dug, dwg_ref[:, sl] = back((dye * val) * slope, eg, wg_, dug_ref, sl)
            duv, dwv_ref[:, sl] = back(dye * act, ev, wv_, duv_ref, sl)
            term = _dotg(dug, ugw_ref[:, sl], 1, 1) + _dotg(duv, uvw_ref[:, sl], 1, 1)
            dh = term if dh is None else dh + term

        @pl.when(j == 0)
        def _():
            dh_ref[...] = dh

        @pl.when(j > 0)
        def _():
            dh_ref[...] += dh

    parts = jax.ShapeDtypeStruct((nrb * HALO, D_FF), F32)
    dug, duv, dh, pg, pv = pl.pallas_call(
        body, name=name, grid=(nrb, sp.ncb),
        in_specs=[sp.row, sp.row, sp.before, sp.before, sp.after, sp.after, sp.act, sp.act_after, sp.down,
                  sp.up_gate, sp.up_val, sp.w, sp.w, sp.b, sp.b],
        out_specs=[sp.row, sp.row, sp.act, sp.part, sp.part],
        out_shape=[jax.ShapeDtypeStruct((s, D_FF), BF16), jax.ShapeDtypeStruct((s, D_FF), BF16),
                   jax.ShapeDtypeStruct((s, D_MODEL), F32), parts, parts],
        compiler_params=_params("parallel", "arbitrary"),
    )(ug, uv, ug, uv, ug, uv, df, df, w_down, w_up, w_up, wg, wv, bg, bv)
    return dug, duv, dh, _sum_parts(pg, name=name + "_sum_gate"), _sum_parts(pv, name=name + "_sum_val")


def _rope_tables(s):
    inv = ROPE_THETA ** (-jnp.arange(0, ROT_DIM, 2, dtype=F32) / ROT_DIM)
    ang = jnp.arange(s, dtype=F32)[:, None] * inv[None, :]
    cos8, sin8 = jnp.cos(ang), jnp.sin(ang)
    rest = HEAD_DIM - ROT_DIM
    cos_h = jnp.concatenate([cos8, cos8, jnp.ones((s, rest), F32)], axis=1)
    sin_h = jnp.concatenate([sin8, sin8, jnp.zeros((s, rest), F32)], axis=1)
    return jnp.tile(cos_h, (1, LANES // HEAD_DIM)), jnp.tile(sin_h, (1, LANES // HEAD_DIM))


def _all_heads(table):
    return jnp.concatenate([table] * (B_WIDTH // LANES), axis=1)


def _layer_fwd(x, w, cos, sin):
    sv = types.SimpleNamespace(x=x)
    (sv.h1,), _ = _rowwise(lambda xb, g: ((_rms(xb, g),), ()), [_full(x)], [w.g_pre], [(D_MODEL, BF16)], [],
                           ts=512, name="pre_mix_norm")
    sv.proj = _matmul(sv.h1, w.big("w_in", sv.h1), mode="nt", out_dtype=F32, name="proj")

    gate_consts = [w.vg, w.vb, *w.ws, w.bfull, w.ga]
    (na,), _ = _rowwise(lambda *a: ((_gate_fn(*a),), ()), [(sv.proj, A_WIDTH, 0), (sv.proj, A_WIDTH, 1)], gate_consts,
                        [(A_WIDTH, BF16)], [], ts=CHUNK, name="gate_fwd")

    def rope_fn(qr, kr, vr, cs, sn):
        cs, sn = _all_heads(cs), _all_heads(sn)
        return (qr * cs + _rot_half(qr) * sn, kr * cs + _rot_half(kr) * sn, vr), ()

    def rope_all(qr, kr, vr, cs, sn):
        return rope_fn(qr, kr, vr, cs, sn)[0] * len(DILATIONS), ()

    qkv, _ = _rowwise(
        rope_all, [(sv.proj, B_WIDTH, 2), (sv.proj, B_WIDTH, 3), (sv.proj, B_WIDTH, 4), _full(cos), _full(sin)], [],
        [(B_WIDTH, BF16, d) for d in DILATIONS for _ in range(3)], [], ts=512, name="rope_fwd")
    sv.qkv = {d: qkv[3 * i:3 * i + 3] for i, d in enumerate(DILATIONS)}

    branch = []
    for d in DILATIONS:
        o, l = _attn_fwd(*sv.qkv[d], d, name=f"attn_fwd_d{d}")
        branch += [_full(o, d), _full(l, d)]

    def combine_fn(o1, l1, o2, l2, o3, l3, nab, gb):
        m = jnp.maximum(jnp.maximum(l1, l2), l3)
        e1, e2, e3 = jnp.exp(l1 - m), jnp.exp(l2 - m), jnp.exp(l3 - m)
        den = e1 + e2 + e3
        ob = (e1 / den) * o1 + (e2 / den) * o2 + (e3 / den) * o3
        mixed = jnp.concatenate([nab, _rms(ob, gb).astype(BF16)], axis=1)
        lse = _head_cols(m + jnp.log(den))
        return (mixed, ob, lse) + (lse,) * len(DILATIONS), ()

    (sv.mixed, sv.ob, sv.lse_rows, *lses), _ = _rowwise(
        combine_fn, branch + [_full(na)], [w.gb],
        [(D_MODEL, BF16), (B_WIDTH, F32), (LANES, F32, HEAD_ROWS)] + [(LANES, F32, d) for d in DILATIONS], [],
        ts=512, name="combine")
    sv.lse = dict(zip(DILATIONS, lses))
    sv.y = _matmul(sv.mixed, w.big("w_out", sv.mixed), mode="nn", out_dtype=F32, name="mix_out")

    def mid_fn(xb, yb, g1, g2):
        x1 = xb + _rms(yb, g1)
        return (x1, _rms(x1, g2)), ()

    (sv.x1, sv.h2), _ = _rowwise(mid_fn, [_full(x), _full(sv.y)], [w.g_pm, w.g_pf], [(D_MODEL, F32), (D_MODEL, BF16)], [],
                                 ts=512, name="post_mix_norm")
    conv_w = w.big("conv_w", sv.h2)
    sv.ug, sv.uv, sv.yff = _ffn_up_geglu(sv.h2, w.big("w_up", sv.h2), conv_w[:, :D_FF], conv_w[:, D_FF:],
                                         w.cb_g, w.cb_v, name="ffn_up_geglu")
    sv.f = _matmul(sv.yff, w.big("w_down", sv.yff), mode="nn", out_dtype=F32, name="ffn_down")
    (x2,), _ = _rowwise(lambda xb, fb, g: ((xb + _rms(fb, g),), ()), [_full(sv.x1), _full(sv.f)], [w.g_post],
                        [(D_MODEL, F32)], [], ts=512, name="post_ffn_norm")
    return x2, sv


def _layer_bwd(dx2, sv, w, cos, sin, emit):
    g = {}

    def post_fn(fb, dxb, gp):
        _, vjp = jax.vjp(_rms, fb, gp)
        df, dg = vjp(dxb)
        return (df,), (dg,)

    (df,), (g["post_ffn_norm"],) = _rowwise(post_fn, [_full(sv.f), _full(dx2)], [w.g_post], [(D_MODEL, BF16)],
                                            [(1, D_MODEL)], ts=512, name="post_ffn_norm_bwd")
    big = {"w_down": _matmul(sv.yff, df, mode="tn", out_dtype=BF16, name="ffn_down_dw").reshape(N_DEV, -1, D_MODEL)}
    conv_w = w.big("conv_w", df)
    dug, duv, dh2, dwg, dwv = _ffn_geglu_bwd(sv.ug, sv.uv, df, w.big("w_down", df), w.big("w_up", df),
                                             conv_w[:, :D_FF], conv_w[:, D_FF:], w.cb_g, w.cb_v, name="ffn_geglu_bwd")
    big["conv_w"] = jnp.concatenate([dwg[0:3], dwv[0:3]], axis=1).reshape(3, N_DEV, D_MODEL).transpose(1, 0, 2)
    g["conv_b"] = jnp.concatenate([dwg[3], dwv[3]], axis=0)
    big["w_up"] = _matmul_by_destination(sv.h2, dug, duv, name="ffn_up_dw")
    g_pm = w.g_pm + emit(big)

    def mid_fn(x1b, yb, dhb, dxb, g1, g2):
        _, vjp2 = jax.vjp(_rms, x1b, g2)
        dx1h, dg2 = vjp2(dhb)
        dx1 = dxb + dx1h
        _, vjp1 = jax.vjp(_rms, yb, g1)
        dy, dg1 = vjp1(dx1)
        return (dx1, dy), (dg1, dg2)

    (dx1, dy), (g["post_mix_norm"], g["pre_ffn_norm"]) = _rowwise(
        mid_fn, [_full(sv.x1), _full(sv.y), _full(dh2), _full(dx2)], [g_pm, w.g_pf],
        [(D_MODEL, F32), (D_MODEL, BF16)], [(1, D_MODEL), (1, D_MODEL)], ts=256, name="post_mix_norm_bwd")
    dmixed = _matmul(dy, w.big("w_out", dy), mode="nt", out_dtype=F32, name="mix_out_dx")
    big = {"w_out": _matmul(sv.mixed, dy, mode="tn", out_dtype=BF16, name="mix_out_dw").reshape(N_DEV, -1, D_MODEL)}

    def attn_out_fn(obb, dmb, gb):
        _, vjp = jax.vjp(_rms, obb, gb)
        do, dgb = vjp(dmb)
        delta = _head_cols(_head_sum(do * obb))
        return (delta,) + (do,) * len(DILATIONS) + (delta,) * len(DILATIONS), (dgb,)

    (delta_rows, *outs), (g["out_norm_b"],) = _rowwise(
        attn_out_fn, [_full(sv.ob), (dmixed, B_WIDTH, 1)], [w.gb],
        [(LANES, F32, HEAD_ROWS)] + [(B_WIDTH, BF16, d) for d in DILATIONS] + [(LANES, F32, d) for d in DILATIONS],
        [(1, B_WIDTH)], ts=512, name="attn_out_bwd")
    do = dict(zip(DILATIONS, outs[:len(DILATIONS)]))
    delta = dict(zip(DILATIONS, outs[len(DILATIONS):]))
    parts = {"q": [], "k": [], "v": []}
    for d in DILATIONS:
        qv, kv, vv = sv.qkv[d]
        dq = _attn_bwd_q(qv, kv, vv, do[d], sv.lse[d], delta[d], d, name=f"attn_bwd_q_d{d}")
        dk, dv = _attn_bwd_kv(qv, kv, vv, do[d], _spread_rows(sv.lse_rows, d), _spread_rows(delta_rows, d), d,
                              name=f"attn_bwd_kv_d{d}")
        parts["q"].append(_full(dq, d))
        parts["k"].append(_full(dk, d))
        parts["v"].append(_full(dv, d))

    def rope_bwd_fn(q1, q2, q3, k1, k2, k3, v1, v2, v3, cs, sn):
        cs, sn = _all_heads(cs), _all_heads(sn)

        def back(t):
            return t * cs - _rot_half(t * sn)
        return (jnp.concatenate([back(q1 + q2 + q3), back(k1 + k2 + k3), v1 + v2 + v3], axis=1),), ()

    (dzb,), _ = _rowwise(rope_bwd_fn, parts["q"] + parts["k"] + parts["v"] + [_full(cos), _full(sin)], [],
                         [(3 * B_WIDTH, BF16)], [], ts=256, name="rope_bwd")

    gate_consts = [w.vg, w.vb, *w.ws, w.bfull, w.ga]

    def gate_bwd_fn(zu, zv, dna, *consts):
        _, vjp = jax.vjp(_gate_fn, zu, zv, *consts)
        grads = vjp(dna)
        return (jnp.concatenate([grads[0], grads[1]], axis=1),), tuple(grads[2:])

    (dza,), gsmall = _rowwise(
        gate_bwd_fn, [(sv.proj, A_WIDTH, 0), (sv.proj, A_WIDTH, 1), (dmixed, A_WIDTH, 0)], gate_consts,
        [(2 * A_WIDTH, BF16)], [c.shape for c in gate_consts], ts=CHUNK, name="gate_bwd")
    g["v_norm_g"], g["v_norm_b"] = gsmall[0], gsmall[1]
    g["w_spatial"] = jnp.stack(gsmall[2:6])
    g["b_spatial"] = _bias_reduce(gsmall[6], name="bias_reduce")[:, :A_GROUPS].T
    g["out_norm_a"] = gsmall[7]

    dproj = jnp.concatenate([dza, dzb], axis=1)
    dh1 = _matmul(dproj, w.big("w_in", dproj), mode="nn", out_dtype=F32, name="proj_dx")
    big["w_in"] = _matmul(dproj, sv.h1, mode="tn", out_dtype=BF16, name="proj_dw").reshape(N_DEV, -1, D_MODEL)
    g_pre = w.g_pre + emit(big)

    def pre_fn(xb, dhb, dxb, gp):
        _, vjp = jax.vjp(_rms, xb, gp)
        dxh, dg = vjp(dhb)
        return (dxb + dxh,), (dg,)

    (dx,), (g["pre_mix_norm"],) = _rowwise(pre_fn, [_full(sv.x), _full(dh1), _full(dx1)], [g_pre], [(D_MODEL, F32)],
                                           [(1, D_MODEL)], ts=512, name="pre_mix_norm_bwd")
    return dx, g


def _layer_weights(l, full, small):
    row = lambda a: a[l].reshape(1, -1)
    return types.SimpleNamespace(
        big=functools.partial(full, l),
        g_pre=row(small["pre_mix_norm"]), vg=row(small["v_norm_g"]), vb=row(small["v_norm_b"]),
        ws=[small["w_spatial"][l, gi] for gi in range(A_GROUPS)],
        bfull=jnp.repeat(small["b_spatial"][l].T, CHUNK, axis=1),
        ga=row(small["out_norm_a"]), gb=row(small["out_norm_b"]),
        g_pm=row(small["post_mix_norm"]), g_pf=row(small["pre_ffn_norm"]),
        cb_g=small["conv_b"][l][:D_FF].reshape(1, -1), cb_v=small["conv_b"][l][D_FF:].reshape(1, -1),
        g_post=row(small["post_ffn_norm"]))


def _local_step(x, target, full, small, emit, started):
    s = x.shape[0]
    cos, sin = _rope_tables(s)
    ws = [_layer_weights(l, full, small) for l in range(N_LAYERS)]
    ws[0].g_pre = ws[0].g_pre + started
    saved = []
    h = x
    for l in range(N_LAYERS):
        h, sv = _layer_fwd(h, ws[l], cos, sin)
        saved.append(sv)

    def loss_fn(yb, tb):
        diff = yb - tb
        return (diff * (1.0 / D_MODEL),), (jnp.sum(diff * diff, axis=0, keepdims=True),)

    (dh,), (sq,) = _rowwise(loss_fn, [_full(h), _full(target)], [], [(D_MODEL, F32)], [(1, D_MODEL)], ts=512, name="loss")
    loss = 0.5 * jnp.sum(sq) * (1.0 / D_MODEL)
    grads = [None] * N_LAYERS
    for l in reversed(range(N_LAYERS)):
        dh, grads[l] = _layer_bwd(dh, saved[l], ws[l], cos, sin, functools.partial(emit, l))
    return loss, dh, grads


def _place():
    return lax.axis_index("x"), lax.axis_index("y"), lax.axis_index("c")


def _all_gather(x, after, name):
    def body(x_ref, after_ref, out_ref, send_sems, recv_sems, local_sem):
        mx, my, mc = _place()
        me, sibling = (mx, my, mc), (mx, my, 1 - mc)
        chips = [(1 - mx, my), (mx, 1 - my), (1 - mx, 1 - my)]

        def slot(px, py, pc):
            return out_ref.at[4 * px + 2 * py + pc]

        def copy(k, block, to, src=None):
            return pltpu.make_async_remote_copy(
                src_ref=slot(*block) if src is None else src, dst_ref=slot(*block),
                send_sem=send_sems.at[k], recv_sem=recv_sems.at[k], device_id=to, device_id_type=MESH_ID)

        mine = pltpu.make_async_copy(x_ref, slot(*me), local_sem)
        mine.start()
        first = [copy(0, me, sibling, src=x_ref)]
        first += [copy(1 + j, me, (*chip, mc), src=x_ref) for j, chip in enumerate(chips)]
        for cp in first:
            cp.start()
        passed = [copy(4 + j, (*chip, mc), sibling) for j, chip in enumerate(chips)]
        for j, chip in enumerate(chips):
            copy(1 + j, (*chip, mc), me).wait_recv()
            passed[j].start()
        copy(0, sibling, me).wait_recv()
        for j, chip in enumerate(chips):
            copy(4 + j, (*chip, 1 - mc), me).wait_recv()
        for cp in first + passed:
            cp.wait_send()
        mine.wait()

    return pl.pallas_call(
        body, name=name, out_shape=jax.ShapeDtypeStruct((N_DEV,) + x.shape, x.dtype), in_specs=[ANY, ANY],
        out_specs=ANY,
        scratch_shapes=[pltpu.SemaphoreType.DMA((7,)), pltpu.SemaphoreType.DMA((7,)), pltpu.SemaphoreType.DMA],
    )(x, after)


FLIPS = ((1, 0, 0), (0, 1, 0), (1, 1, 0), (0, 0, 1), (1, 0, 1), (0, 1, 1), (1, 1, 1))
HBM_SPEC = pl.BlockSpec(memory_space=pltpu.HBM)
SEM_SPEC = pl.BlockSpec(memory_space=pltpu.SEMAPHORE)
SPLIT_COPY = pltpu.CompilerParams(has_side_effects=pltpu.SideEffectType.DATAFLOW_SIDE_EFFECTING)


def _peers():
    mx, my, mc = _place()
    out = []
    for fx, fy, fc in FLIPS:
        px, py, pc = (1 - mx if fx else mx), (1 - my if fy else my), (1 - mc if fc else mc)
        out.append(((px, py, pc), 4 * px + 2 * py + pc))
    return out


def _flat_copies(scatter, src_refs, land_refs, send_sems, recv_sems):
    mx, my, mc = _place()
    me = 4 * mx + 2 * my + mc
    n = len(src_refs)
    copies = []
    for t in range(n):
        for i, (peer, number) in enumerate(_peers()):
            copies.append(pltpu.make_async_remote_copy(
                src_ref=src_refs[t].at[number] if scatter else src_refs[t],
                dst_ref=land_refs[t].at[i] if scatter else land_refs[t].at[me],
                send_sem=send_sems.at[t * len(FLIPS) + i], recv_sem=recv_sems.at[t * len(FLIPS) + i],
                device_id=peer, device_id_type=MESH_ID))
    return copies


def _flat_start(arrays, scatter, name):
    n = len(arrays)
    slots = len(FLIPS) if scatter else N_DEV
    lands = [lax.empty((slots,) + (a.shape[1:] if scatter else a.shape), a.dtype) for a in arrays]

    def body(*refs):
        src, land, (send_sems, recv_sems), token = refs[:n], refs[n:2 * n], refs[2 * n:2 * n + 2], refs[-1]
        for cp in _flat_copies(scatter, src, land, send_sems, recv_sems):
            cp.start()
        token[...] = jnp.zeros_like(token)

    hbm = [pltpu.HBM(a.shape, a.dtype) for a in arrays] + [pltpu.HBM(a.shape, a.dtype) for a in lands]
    sems = pltpu.SemaphoreType.DMA((n * len(FLIPS),))
    outs = pl.pallas_call(
        body, name=name, out_shape=(sems, sems, *hbm, jax.ShapeDtypeStruct((8, 128), F32)),
        in_specs=[HBM_SPEC] * (2 * n),
        out_specs=(SEM_SPEC, SEM_SPEC, *([HBM_SPEC] * (2 * n)), pl.BlockSpec(memory_space=pltpu.VMEM)),
        input_output_aliases={t: 2 + t for t in range(2 * n)}, compiler_params=SPLIT_COPY,
    )(*[pltpu.with_memory_space_constraint(a, pltpu.HBM) for a in (*arrays, *lands)])
    return types.SimpleNamespace(sems=outs[:2], thru=outs[2:2 + 2 * n], scatter=scatter, n=n), outs[-1][0:1, 0:1]


def _flat_wait(handle, after, name):
    n = handle.n

    def body(*refs):
        src, land, (send_sems, recv_sems) = refs[:n], refs[n:2 * n], refs[2 * n:2 * n + 2]
        for cp in _flat_copies(handle.scatter, src, land, send_sems, recv_sems):
            cp.wait_send()
            cp.wait_recv()

    outs = pl.pallas_call(
        body, name=name, out_shape=tuple(pltpu.HBM(a.shape, a.dtype) for a in handle.thru),
        in_specs=[HBM_SPEC] * (2 * n) + [SEM_SPEC, SEM_SPEC, ANY], out_specs=tuple([HBM_SPEC] * (2 * n)),
        input_output_aliases={t: t for t in range(2 * n)}, compiler_params=SPLIT_COPY,
    )(*handle.thru, *handle.sems, after)
    return outs[:n], outs[n:]


def _adamw(w, g, m, v):
    m2 = ADAM_B1 * m + (1.0 - ADAM_B1) * g
    v2 = ADAM_B2 * v + (1.0 - ADAM_B2) * (g * g)
    m_hat = m2 / (1.0 - ADAM_B1 ** ADAM_STEP)
    v_hat = v2 / (1.0 - ADAM_B2 ** ADAM_STEP)
    return -ADAM_LR * (m_hat / (jnp.sqrt(v_hat) + ADAM_EPS) + ADAM_WD * w), m2, v2


def _adamw_sharded(me, mine, landed, w, m, v, tr, name):
    _, r, c = w.shape
    nt = r // tr
    assert r % tr == 0 and len(mine) == len(landed) == N_LAYERS == 2, name
    per_layer = 1 + len(FLIPS)

    def body(me_ref, *refs):
        terms, (w_ref, m_ref, v_ref), outs = refs[:2 * per_layer], refs[2 * per_layer:2 * per_layer + 3], refs[-4:]
        layer = pl.program_id(0)

        def total(group):
            g = group[0][0].astype(F32)
            for t in group[1:]:
                g = g + t[0].astype(F32)
            return g

        g = jnp.where(layer == 0, total(terms[:per_layer]), total(terms[per_layer:]))
        d, m2, v2 = _adamw(w_ref[0], g, m_ref[0], v_ref[0])
        for o, val in zip(outs, (g, d, m2, v2)):
            o[0] = val

    def held(l):
        return lambda layer, i: jnp.where(layer == l, i, nt - 1 if l == 0 else 0)

    in_specs = []
    for l in range(N_LAYERS):
        rows = held(l)
        in_specs.append(pl.BlockSpec((1, tr, c), functools.partial(lambda layer, i, me_ref, rows: (me_ref[0], rows(layer, i), 0), rows=rows)))
        for k in range(len(FLIPS)):
            in_specs.append(pl.BlockSpec(
                (1, tr, c), functools.partial(lambda layer, i, me_ref, rows, k: (k, rows(layer, i), 0), rows=rows, k=k)))
    tile = pl.BlockSpec((1, tr, c), lambda layer, i, me_ref: (layer, i, 0))
    operands = []
    for l in range(N_LAYERS):
        operands += [mine[l]] + [landed[l]] * len(FLIPS)
    return pl.pallas_call(
        body, name=name, out_shape=[jax.ShapeDtypeStruct(w.shape, F32)] * 4,
        grid_spec=pltpu.PrefetchScalarGridSpec(
            num_scalar_prefetch=1, grid=(N_LAYERS, nt), in_specs=in_specs + [tile] * 3, out_specs=[tile] * 4),
        compiler_params=_params("arbitrary", "arbitrary"),
    )(me, *operands, w, m, v)


def _adamw_replicated(parts, w, m, v, name):
    def body(p_ref, w_ref, m_ref, v_ref, g_ref, d_ref, m2_ref, v2_ref):
        g = p_ref[0]
        for j in range(1, N_DEV):
            g = g + p_ref[j]
        d, m2, v2 = _adamw(w_ref[...], g, m_ref[...], v_ref[...])
        g_ref[...], d_ref[...], m2_ref[...], v2_ref[...] = g, d, m2, v2

    return pl.pallas_call(body, name=name, out_shape=[jax.ShapeDtypeStruct(w.shape, F32)] * 4,
                          compiler_params=pltpu.CompilerParams(vmem_limit_bytes=VMEM_LIMIT_BYTES))(parts, w, m, v)


def _pack_small(vals):
    flat = jnp.concatenate([vals[n].reshape(-1) for n in SMALL_NAMES])
    return jnp.concatenate([flat, jnp.zeros((SMALL_ROWS * D_MODEL - flat.shape[0],), F32)]).reshape(SMALL_ROWS, D_MODEL)


def _unpack_small(packed, shapes):
    flat, out, at = packed.reshape(-1), {}, 0
    for n in SMALL_NAMES:
        size = math.prod(shapes[n])
        out[n] = flat[at:at + size].reshape(shapes[n])
        at += size
    return out


GATHER_GROUPS = ((0, ("w_in",)), (0, ("w_out", "w_up", "conv_w")), (0, ("w_down",)),
                 (1, ("w_in", "w_out", "w_up", "conv_w")), (1, ("w_down",)))
ADAMW_TILE_ROWS = {"w_in": 320, "w_out": 128, "w_up": 256, "w_down": 256, "conv_w": 3}


def _assemble(name, land):
    if name == "w_in":
        return land.reshape(IN_COLS, D_MODEL)
    if name == "conv_w":
        return land.transpose(1, 0, 2).reshape(3, 2 * D_FF)
    if name == "w_up":
        return land
    return land.reshape(-1, D_MODEL)


def _start_gathers(wts, me):
    started, groups = jnp.zeros((1, 1), F32), []
    for gi, (l, names) in enumerate(GATHER_GROUPS):
        local = {"conv_w": lambda a: a, "w_in": lambda a: a.T.astype(BF16)}
        blocks = [local.get(n, lambda a: a.astype(BF16))(wts[n][l]) for n in names]
        handle, token = _flat_start(blocks, False, name=f"gather_start_{gi}")
        groups.append(types.SimpleNamespace(layer=l, names=names, blocks=blocks, handle=handle, got=None, index=gi))
        started = started + token

    def fetch(l, name, after):
        grp = next(gr for gr in groups if gr.layer == l and name in gr.names)
        if grp.got is None:
            lands = _flat_wait(grp.handle, after, name=f"gather_wait_{grp.index}")[1]
            grp.got = {}
            for n, blk, land in zip(grp.names, grp.blocks, lands):
                own = (me,) + (0,) * blk.ndim
                grp.got[n] = _assemble(n, lax.dynamic_update_slice(land, blk[None], own))
        return grp.got[name]

    return fetch, started


def kernel(x, pre_mix_norm, w_in, v_norm_g, v_norm_b, w_spatial, b_spatial, out_norm_a, out_norm_b, w_out, post_mix_norm, pre_ffn_norm, w_up, conv_w, conv_b, w_down, post_ffn_norm, loss_target, m_pre_mix_norm, m_w_in, m_v_norm_g, m_v_norm_b, m_w_spatial, m_b_spatial, m_out_norm_a, m_out_norm_b, m_w_out, m_post_mix_norm, m_pre_ffn_norm, m_w_up, m_conv_w, m_conv_b, m_w_down, m_post_ffn_norm, v_pre_mix_norm, v_w_in, v_v_norm_g, v_v_norm_b, v_w_spatial, v_b_spatial, v_out_norm_a, v_out_norm_b, v_w_out, v_post_mix_norm, v_pre_ffn_norm, v_w_up, v_conv_w, v_conv_b, v_w_down, v_post_ffn_norm):
    wts = dict(zip(WEIGHT_NAMES, (pre_mix_norm, w_in, v_norm_g, v_norm_b, w_spatial, b_spatial, out_norm_a, out_norm_b,
                                  w_out, post_mix_norm, pre_ffn_norm, w_up, conv_w, conv_b, w_down, post_ffn_norm)))
    mom1 = dict(zip(WEIGHT_NAMES, (m_pre_mix_norm, m_w_in, m_v_norm_g, m_v_norm_b, m_w_spatial, m_b_spatial, m_out_norm_a,
                                   m_out_norm_b, m_w_out, m_post_mix_norm, m_pre_ffn_norm, m_w_up, m_conv_w, m_conv_b,
                                   m_w_down, m_post_ffn_norm)))
    mom2 = dict(zip(WEIGHT_NAMES, (v_pre_mix_norm, v_w_in, v_v_norm_g, v_v_norm_b, v_w_spatial, v_b_spatial, v_out_norm_a,
                                   v_out_norm_b, v_w_out, v_post_mix_norm, v_pre_ffn_norm, v_w_up, v_conv_w, v_conv_b,
                                   v_w_down, v_post_ffn_norm)))
    mx, my, mc = _place()
    me = 4 * mx + 2 * my + mc

    fetch, started = _start_gathers(wts, me)
    scatters = []

    def emit(l, blocks):
        names = tuple(blocks)
        handle, token = _flat_start([blocks[n] for n in names], True, name=f"scatter_start_{l}_{len(scatters) % 2}")
        scatters.append((l, names, handle))
        return token

    loss_local, dx, grads = _local_step(x[0], loss_target[0], fetch, wts, emit, started)
    loss = lax.psum(loss_local, AXES)

    me_arr = jnp.reshape(me, (1,)).astype(jnp.int32)
    big_out = [{}, {}, {}, {}]

    def finish(group, after):
        mine, landed = {}, {}
        for l, names, handle in scatters:
            if names == group:
                sent, lands = _flat_wait(handle, after, name=f"scatter_wait_{l}_{'_'.join(names)}")
                for n, a, b in zip(names, sent, lands):
                    mine[l, n], landed[l, n] = a, b
        for n in group:
            flip = (lambda a: a.transpose(0, 2, 1)) if n == "w_in" else (lambda a: a)
            res = _adamw_sharded(me_arr, [mine[l, n] for l in range(N_LAYERS)], [landed[l, n] for l in range(N_LAYERS)],
                                 flip(wts[n]), flip(mom1[n]), flip(mom2[n]), ADAMW_TILE_ROWS[n], name=f"adamw_{n}")
            for kind in range(4):
                big_out[kind][n] = flip(res[kind])
        return res[0]

    early, late = scatters[0][1], scatters[1][1]
    done_early = finish(early, dx)
    small_grads = {n: jnp.stack([g[n].reshape(wts[n].shape[1:]) for g in grads]) for n in SMALL_NAMES}
    everyone = _all_gather(_pack_small(small_grads), done_early, name="gather_small_grads")
    small = [_pack_small({n: t[n] for n in SMALL_NAMES}) for t in (wts, mom1, mom2)]
    small_out = _adamw_replicated(everyone, *small, name="adamw_replicated")
    finish(late, small_out[0])
    small_shapes = {n: wts[n].shape for n in SMALL_NAMES}
    small_out = [_unpack_small(o, small_shapes) for o in small_out]

    outs = [loss, dx[None]]
    for kind in range(4):
        outs += [big_out[kind][n] if n in BIG_NAMES else small_out[kind][n] for n in WEIGHT_NAMES]
    return tuple(outs)
```

```python
import functools
import math
import types

import jax
import jax.numpy as jnp
from jax import lax
from jax.experimental import pallas as pl
from jax.experimental.pallas import tpu as pltpu

F32 = jnp.float32
BF16 = jnp.bfloat16

D_MODEL = 1024
A_WIDTH = 512
A_GROUPS = 4
CHUNK = 128
B_WIDTH = 512
HEAD_DIM = 64
N_HEADS = B_WIDTH // HEAD_DIM
ROT_DIM = 16
ROPE_THETA = 500000.0
BAND = 128
DILATIONS = (1, 4, 16)
IN_COLS = 2560
D_FF = 4096
EPS = 1e-6
NEG_INF = -1e30
N_DEV = 8
N_LAYERS = 2

ADAM_LR = 0.001
ADAM_B1 = 0.9
ADAM_B2 = 0.999
ADAM_EPS = 1e-08
ADAM_WD = 0.01
ADAM_STEP = 10

VMEM_LIMIT_BYTES = 56 * 1024 * 1024
MESH_ID = pl.DeviceIdType.MESH
ANY = pl.BlockSpec(memory_space=pl.ANY)
AXES = ("x", "y", "c")

WEIGHT_NAMES = ("pre_mix_norm", "w_in", "v_norm_g", "v_norm_b", "w_spatial", "b_spatial", "out_norm_a", "out_norm_b",
                "w_out", "post_mix_norm", "pre_ffn_norm", "w_up", "conv_w", "conv_b", "w_down", "post_ffn_norm")
BIG_NAMES = ("w_in", "w_out", "w_up", "w_down", "conv_w")
SMALL_NAMES = tuple(n for n in WEIGHT_NAMES if n not in BIG_NAMES)

PACK_ROWS = {"w_in": 640, "w_out": 256, "w_up": 2048, "w_down": 1024, "conv_w": 6}
CONV_W_PAD = 2
PACKED_F32_ROWS = 4096
PACKED_BF16_ROWS = 3984
SMALL_ROWS = 160


def _params(*sem):
    return pltpu.CompilerParams(dimension_semantics=sem, vmem_limit_bytes=VMEM_LIMIT_BYTES)


def _dotg(a, b, ca, cb):
    return lax.dot_general(a.astype(BF16), b.astype(BF16), (((ca,), (cb,)), ((), ())), preferred_element_type=F32)


@jax.custom_vjp
def _bdot(a, b):
    return _dotg(a, b, 1, 0)


def _bdot_fwd(a, b):
    return _dotg(a, b, 1, 0), (a, b)


def _bdot_bwd(res, g):
    a, b = res
    return _dotg(g, b, 1, 1), _dotg(a, g, 0, 0)


_bdot.defvjp(_bdot_fwd, _bdot_bwd)


def _rms(x, g):
    return x * lax.rsqrt(jnp.mean(x * x, axis=-1, keepdims=True) + EPS) * g


def _layernorm(x, g, b):
    mu = jnp.mean(x, axis=-1, keepdims=True)
    xc = x - mu
    return xc * lax.rsqrt(jnp.mean(xc * xc, axis=-1, keepdims=True) + EPS) * g + b


def _gelu_erf(x):
    return x * (lax.erf(x * (1.0 / math.sqrt(2.0))) + 1.0) * 0.5


def _gelu_tanh(x):
    c = math.sqrt(2.0 / math.pi)
    return 0.5 * x * (1.0 + jnp.tanh(c * (x + 0.044715 * (x * x * x))))


def _gelu_tanh_and_slope(x):
    c, k = math.sqrt(2.0 / math.pi), 0.044715
    x2 = x * x
    t = jnp.tanh(c * (x + k * (x2 * x)))
    half_x, one_t = 0.5 * x, 1.0 + t
    return half_x * one_t, 0.5 * one_t + (half_x * (1.0 - t * t)) * (c + (3.0 * k * c) * x2)


def _rot_half(x):
    width = x.shape[1]
    lane = lax.broadcasted_iota(jnp.int32, x.shape, 1) % HEAD_DIM
    back = pltpu.roll(x, ROT_DIM // 2, 1)
    fwd = pltpu.roll(x, width - ROT_DIM // 2, 1)
    return jnp.where(lane < ROT_DIM // 2, -fwd, jnp.where(lane < ROT_DIM, back, 0.0))


def _split3(z):
    h0 = z.astype(BF16)
    r1 = z - h0.astype(F32)
    h1 = r1.astype(BF16)
    h2 = (r1 - h1.astype(F32)).astype(BF16)
    return h0, h1, h2


def _head_sum(z):
    width = z.shape[1]
    a = lax.broadcasted_iota(jnp.int32, (width, width), 0) // HEAD_DIM
    b = lax.broadcasted_iota(jnp.int32, (width, width), 1) // HEAD_DIM
    ones = jnp.where(a == b, 1.0, 0.0).astype(BF16)
    out = None
    for part in _split3(z):
        t = lax.dot_general(part, ones, (((1,), (0,)), ((), ())), preferred_element_type=F32)
        out = t if out is None else out + t
    return out


MATMUL_VMEM_BUDGET = 40 * 1024 * 1024


def _matmul_tiles(m, n, k, out_bytes):
    tn = n if n <= 1024 else (1280 if n % 1280 == 0 and n % 1024 else 1024)
    tk = k if k <= 1024 else (1280 if k % 1280 == 0 and k % 1024 else 1024)
    tm = m
    while tm > 256:
        blocks = 2 * 2 * (tm * tk + tk * tn) + 2 * out_bytes * tm * tn + (4 * tm * tn if k > tk else 0)
        if blocks <= MATMUL_VMEM_BUDGET and m % tm == 0:
            break
        tm //= 2
    return tm, tn, tk


def _matmul(a, b, *, mode, out_dtype, name, cols=None):
    wide = D_MODEL if cols is not None else None
    if mode == "nn":
        (m, k), (_, n) = a.shape, (b.shape if cols is None else (b.shape[1], cols[1] * wide))
    elif mode == "nt":
        (m, k), (n, _) = a.shape, (b.shape if cols is None else (b.shape[1], cols[1] * wide))
    else:
        (k, m), (_, n) = a.shape, b.shape
    tm, tn, tk = _matmul_tiles(m, n, k, jnp.dtype(out_dtype).itemsize)
    assert m % tm == 0 and n % tn == 0 and k % tk == 0, (name, m, n, k)
    nk = k // tk
    if mode == "nn":
        a_spec = pl.BlockSpec((tm, tk), lambda i, j, kk: (i, kk))
        b_spec = pl.BlockSpec((tk, tn), lambda i, j, kk: (kk, j))
        if cols is not None:
            assert tn == wide
            b_spec = pl.BlockSpec((None, tk, tn), lambda i, j, kk: (cols[0] + j, kk, 0))
        ca, cb = 1, 0
    elif mode == "nt":
        a_spec = pl.BlockSpec((tm, tk), lambda i, j, kk: (i, kk))
        b_spec = pl.BlockSpec((tn, tk), lambda i, j, kk: (j, kk))
        if cols is not None:
            assert tk == wide
            b_spec = pl.BlockSpec((None, tn, tk), lambda i, j, kk: (cols[0] + kk, j, 0))
        ca, cb = 1, 1
    else:
        a_spec = pl.BlockSpec((tk, tm), lambda i, j, kk: (kk, i))
        b_spec = pl.BlockSpec((tk, tn), lambda i, j, kk: (kk, j))
        ca, cb = 0, 0

    def body(a_ref, b_ref, o_ref, *acc):
        kk = pl.program_id(2)
        part = lax.dot_general(a_ref[...], b_ref[...], (((ca,), (cb,)), ((), ())), preferred_element_type=F32)
        if nk == 1:
            o_ref[...] = part.astype(o_ref.dtype)
            return
        acc_ref, = acc

        @pl.when(kk == 0)
        def _():
            acc_ref[...] = part

        @pl.when(kk > 0)
        def _():
            acc_ref[...] += part

        @pl.when(kk == nk - 1)
        def _():
            o_ref[...] = acc_ref[...].astype(o_ref.dtype)

    return pl.pallas_call(
        body, name=name, grid=(m // tm, n // tn, nk),
        in_specs=[a_spec, b_spec], out_specs=pl.BlockSpec((tm, tn), lambda i, j, kk: (i, j)),
        out_shape=jax.ShapeDtypeStruct((m, n), out_dtype),
        scratch_shapes=[pltpu.VMEM((tm, tn), F32)] if nk > 1 else [],
        compiler_params=_params("parallel", "parallel", "arbitrary"),
    )(a, b)


def _matmul_by_destination(a, b_lo, b_hi, *, name, tm=1024, tk=2048):
    (k, m), half = a.shape, N_DEV // 2
    assert b_lo.shape == b_hi.shape == (k, half * D_MODEL) and m % tm == 0 and k % tk == 0, name
    nk = k // tk

    def body(a_ref, lo_ref, hi_ref, o_ref, acc_ref):
        j, kk = pl.program_id(1), pl.program_id(2)

        def step(b_ref):
            part = lax.dot_general(a_ref[...], b_ref[...], (((0,), (0,)), ((), ())), preferred_element_type=F32)

            @pl.when(kk == 0)
            def _():
                acc_ref[...] = part

            @pl.when(kk > 0)
            def _():
                acc_ref[...] += part

        pl.when(j < half)(lambda: step(lo_ref))
        pl.when(j >= half)(lambda: step(hi_ref))

        @pl.when(kk == nk - 1)
        def _():
            o_ref[...] = acc_ref[...].astype(o_ref.dtype)

    lo_spec = pl.BlockSpec((tk, D_MODEL), lambda i, j, kk: (jnp.where(j < half, kk, nk - 1), jnp.minimum(j, half - 1)))
    hi_spec = pl.BlockSpec((tk, D_MODEL), lambda i, j, kk: (jnp.where(j >= half, kk, 0), jnp.maximum(j - half, 0)))
    return pl.pallas_call(
        body, name=name, grid=(m // tm, N_DEV, nk),
        in_specs=[pl.BlockSpec((tk, tm), lambda i, j, kk: (kk, i)), lo_spec, hi_spec],
        out_specs=pl.BlockSpec((None, tm, D_MODEL), lambda i, j, kk: (j, i, 0)),
        out_shape=jax.ShapeDtypeStruct((N_DEV, m, D_MODEL), BF16),
        scratch_shapes=[pltpu.VMEM((tm, D_MODEL), F32)],
        compiler_params=_params("parallel", "parallel", "arbitrary"),
    )(a, b_lo, b_hi)


LANES = 128


def _residues_to_rows(ref, scr, d):
    w = ref.shape[1] // d
    n = ref.shape[0]
    for r in range(d):
        for c in range(w // LANES):
            scr[c, pl.ds(r, n, stride=d), :] = ref[:, r * w + c * LANES:r * w + (c + 1) * LANES].astype(F32)
    return jnp.concatenate([scr[c] for c in range(w // LANES)], axis=1)


def _rows_to_residues(val, ref, scr, d):
    w = val.shape[1]
    n = ref.shape[0]
    for c in range(w // LANES):
        scr[c] = val[:, c * LANES:(c + 1) * LANES].astype(F32)
    for r in range(d):
        for c in range(w // LANES):
            ref[:, r * w + c * LANES:r * w + (c + 1) * LANES] = scr[c, pl.ds(r, n, stride=d), :].astype(ref.dtype)


HEAD_ROWS = 0


def _head_cols(z):
    width = z.shape[1]
    a = lax.broadcasted_iota(jnp.int32, (width, LANES), 0)
    b = lax.broadcasted_iota(jnp.int32, (width, LANES), 1)
    pick = jnp.where(a == b * HEAD_DIM, 1.0, 0.0).astype(BF16)
    out = None
    for part in _split3(z):
        t = lax.dot_general(part, pick, (((1,), (0,)), ((), ())), preferred_element_type=F32)
        out = t if out is None else out + t
    return out


def _head_rows_block(cols):
    return cols.T[:N_HEADS, :]


def _rowwise(fn, rows, consts, out_rows, out_acc, *, ts, name):
    rows = [tuple(r) + (1,) * (4 - len(r)) for r in rows]
    out_rows = [tuple(o) + (1,) * (3 - len(o)) for o in out_rows]
    s = rows[0][0].shape[0] * rows[0][3]
    assert s % ts == 0, (name, s, ts)
    n_rows, n_in = len(rows), len(rows) + len(consts)
    n_row = len(out_rows)
    n_out = n_row + len(out_acc)
    moved = [(idx, w) for idx, (_, w, _, d) in enumerate(rows) if d > 1]
    moved += [(n_rows + idx, w) for idx, (w, _, d) in enumerate(out_rows) if d > 1]

    def body(*refs):
        scratch = dict(zip([key for key, _ in moved], refs[n_in + n_out:]))
        vals = []
        for idx, r in enumerate(refs[:n_in]):
            d = rows[idx][3] if idx < n_rows else 1
            vals.append(r[...] if d == 1 else _residues_to_rows(r, scratch[idx], d))
        row_vals, acc_vals = fn(*vals)
        for idx, (r, v) in enumerate(zip(refs[n_in:n_in + n_row], row_vals)):
            d = out_rows[idx][2]
            if d == 1:
                r[...] = v.astype(r.dtype)
            elif d == HEAD_ROWS:
                r[...] = _head_rows_block(v)
            else:
                _rows_to_residues(v, r, scratch[n_rows + idx], d)
        first = pl.program_id(0) == 0
        for r, v in zip(refs[n_in + n_row:n_in + n_out], acc_vals):
            @pl.when(first)
            def _(r=r, v=v):
                r[...] = v

            @pl.when(jnp.logical_not(first))
            def _(r=r, v=v):
                r[...] += v

    in_specs = [pl.BlockSpec((ts // d, d * w), functools.partial(lambda i, cb: (i, cb), cb=cb)) for _, w, cb, d in rows]
    in_specs += [pl.BlockSpec(c.shape, lambda i: (0, 0)) for c in consts]
    out_specs = [pl.BlockSpec((N_HEADS, ts), lambda i: (0, i)) if d == HEAD_ROWS else
                 pl.BlockSpec((ts // d, d * w), lambda i: (i, 0)) for w, _, d in out_rows]
    out_specs += [pl.BlockSpec(sh, lambda i: (0, 0)) for sh in out_acc]
    out_shape = [jax.ShapeDtypeStruct((N_HEADS, s) if d == HEAD_ROWS else (s // d, d * w), dt) for w, dt, d in out_rows]
    out_shape += [jax.ShapeDtypeStruct(sh, F32) for sh in out_acc]
    outs = pl.pallas_call(
        body, name=name, grid=(s // ts,), in_specs=in_specs, out_specs=out_specs, out_shape=out_shape,
        scratch_shapes=[pltpu.VMEM((w // LANES, ts, LANES), F32) for _, w in moved],
        compiler_params=_params("arbitrary" if out_acc else "parallel"),
    )(*[a for a, _, _, _ in rows], *consts)
    return outs[:n_row], outs[n_row:]


def _full(a, d=1):
    return (a, a.shape[1] // d, 0, d)


def _gate_fn(zu, zv, vg, vb, ws0, ws1, ws2, ws3, bfull, ga):
    u = _gelu_erf(zu)
    vn = _layernorm(_gelu_erf(zv), vg, vb)
    p = lax.broadcasted_iota(jnp.int32, (CHUNK, CHUNK), 0)
    q = lax.broadcasted_iota(jnp.int32, (CHUNK, CHUNK), 1)
    tril = jnp.where(q <= p, 1.0, 0.0)
    group = lax.broadcasted_iota(jnp.int32, (1, A_WIDTH), 1) // CHUNK
    sg = bfull
    for g, w in enumerate((ws0, ws1, ws2, ws3)):
        sg = sg + _bdot(w * tril, jnp.where(group == g, vn, 0.0))
    return _rms(u * sg, ga)


GATE_TS = 2 * CHUNK


def _chunk_rows(rows):
    return [slice(c * CHUNK, (c + 1) * CHUNK) for c in range(rows // CHUNK)]


def _bias_reduce(dbf, name):
    def body(x_ref, o_ref):
        lane = lax.broadcasted_iota(jnp.int32, (CHUNK, CHUNK), 1)
        out = jnp.zeros((CHUNK, CHUNK), F32)
        for g in range(A_GROUPS):
            out = jnp.where(lane == g, jnp.sum(x_ref[:, g * CHUNK:(g + 1) * CHUNK], axis=1, keepdims=True), out)
        o_ref[...] = out

    return pl.pallas_call(body, name=name, out_shape=jax.ShapeDtypeStruct((CHUNK, CHUNK), F32))(dbf)


def _pair_mask(hh):
    lane = lax.broadcasted_iota(jnp.int32, (1, 2 * HEAD_DIM), 1)
    return (lane >= HEAD_DIM * hh) & (lane < HEAD_DIM * (hh + 1))


def _lane_pick(x2, lm):
    return jnp.max(jnp.where(lm, x2, -jnp.inf), axis=1, keepdims=True)


SCALE = HEAD_DIM ** -0.5


def _blocks_per_step(nb):
    return next(n for n in (4, 2, 1) if nb % n == 0)


def _units(nblk):
    return [(b, hp, hh) for b in range(nblk) for hp in range(N_HEADS // 2) for hh in range(2)]


def _attn_specs(nb, nblk):
    cur = pl.BlockSpec((nblk * BAND, B_WIDTH), lambda r, j: (j, r))
    prev = pl.BlockSpec((BAND, B_WIDTH), lambda r, j: (jnp.maximum(nblk * j - 1, 0), r))
    nxt = pl.BlockSpec((BAND, B_WIDTH), lambda r, j: (jnp.minimum(nblk * (j + 1), nb - 1), r))
    return cur, prev, nxt


def _pair_cols(hp):
    return slice(2 * HEAD_DIM * hp, 2 * HEAD_DIM * (hp + 1))


def _rows(b):
    return slice(b * BAND, (b + 1) * BAND)


def _with_prev(cur_ref, prev_ref, b, sl):
    if b == 0:
        return jnp.concatenate([prev_ref[:, sl], cur_ref[_rows(0), sl]], axis=0)
    return cur_ref[(b - 1) * BAND:(b + 1) * BAND, sl]


def _with_next(cur_ref, next_ref, b, sl, nblk):
    if b == nblk - 1:
        return jnp.concatenate([cur_ref[_rows(b), sl], next_ref[:, sl]], axis=0)
    return cur_ref[b * BAND:(b + 2) * BAND, sl]


def _band_valid(other_exists):
    row = lax.broadcasted_iota(jnp.int32, (BAND, 2 * BAND), 0)
    col = lax.broadcasted_iota(jnp.int32, (BAND, 2 * BAND), 1)
    return (col >= row) & (col <= row + BAND), other_exists


def _masked(lm, x):
    return jnp.where(lm, x, jnp.zeros_like(x))


def _attn_fwd(q, k, v, d, name):
    nb = q.shape[0] // BAND
    nblk = _blocks_per_step(nb)
    units = _units(nblk)
    cur, prev, _ = _attn_specs(nb, nblk)

    def body(q_ref, kc_ref, kp_ref, vc_ref, vp_ref, o_ref, l_ref):
        band, has_prev = _band_valid(pl.program_id(1) > 0)
        col = lax.broadcasted_iota(jnp.int32, (BAND, 2 * BAND), 1)
        valid = [band & ((col >= BAND) | has_prev)] + [band] * (nblk - 1)

        def scores(unit):
            b, hp, hh = unit
            sl = _pair_cols(hp)
            return _dotg(_masked(_pair_mask(hh), q_ref[_rows(b), sl]), _with_prev(kc_ref, kp_ref, b, sl), 1, 1)

        ahead, half = scores(units[0]), None
        for idx, (b, hp, hh) in enumerate(units):
            raw = ahead
            if idx + 1 < len(units):
                ahead = scores(units[idx + 1])
            sl, lm = _pair_cols(hp), _pair_mask(hh)
            s = jnp.where(valid[b], raw * SCALE, NEG_INF)
            m = jnp.max(s, axis=1, keepdims=True)
            p = jnp.exp(s - m)
            den = jnp.sum(p, axis=1, keepdims=True)
            o = _dotg(p, _with_prev(vc_ref, vp_ref, b, sl), 1, 0) / den
            lse = m + jnp.log(den)
            if hh == 0:
                half = (o, lse)
            else:
                o_ref[_rows(b), sl] = jnp.where(lm, o, half[0])
                l_ref[_rows(b), sl] = jnp.where(lm, lse, half[1])

    return pl.pallas_call(
        body, name=name, grid=(d, nb // nblk), in_specs=[cur, cur, prev, cur, prev], out_specs=[cur, cur],
        out_shape=[jax.ShapeDtypeStruct(q.shape, F32), jax.ShapeDtypeStruct(q.shape, F32)],
        compiler_params=_params("parallel", "parallel"),
    )(q, k, k, v, v)


def _attn_bwd_q(q, k, v, do, lse, delta, d, name):
    nb = q.shape[0] // BAND
    nblk = _blocks_per_step(nb)
    units = _units(nblk)
    cur, prev, _ = _attn_specs(nb, nblk)
    per_head = pl.BlockSpec((nblk * BAND, LANES), lambda r, j: (j, r))

    def body(q_ref, kc_ref, kp_ref, vc_ref, vp_ref, do_ref, l_ref, dl_ref, dq_ref):
        band, has_prev = _band_valid(pl.program_id(1) > 0)
        col = lax.broadcasted_iota(jnp.int32, (BAND, 2 * BAND), 1)
        valid = [band & ((col >= BAND) | has_prev)] + [band] * (nblk - 1)

        def products(unit):
            b, hp, hh = unit
            sl, lm = _pair_cols(hp), _pair_mask(hh)
            return (_dotg(_masked(lm, q_ref[_rows(b), sl]), _with_prev(kc_ref, kp_ref, b, sl), 1, 1),
                    _dotg(_masked(lm, do_ref[_rows(b), sl]), _with_prev(vc_ref, vp_ref, b, sl), 1, 1))

        ahead, half = products(units[0]), None
        for idx, (b, hp, hh) in enumerate(units):
            raw, dp = ahead
            if idx + 1 < len(units):
                ahead = products(units[idx + 1])
            sl, lm = _pair_cols(hp), _pair_mask(hh)
            s = jnp.where(valid[b], raw * SCALE, NEG_INF)
            head = lax.broadcasted_iota(jnp.int32, (1, LANES), 1) == 2 * hp + hh
            p = jnp.exp(s - _lane_pick(l_ref[_rows(b), :], head))
            ds = p * (dp - _lane_pick(dl_ref[_rows(b), :], head))
            dq = _dotg(ds, _with_prev(kc_ref, kp_ref, b, sl), 1, 0) * SCALE
            if hh == 0:
                half = dq
            else:
                dq_ref[_rows(b), sl] = jnp.where(lm, dq, half)

    return pl.pallas_call(
        body, name=name, grid=(d, nb // nblk),
        in_specs=[cur, cur, prev, cur, prev, cur, per_head, per_head], out_specs=cur,
        out_shape=jax.ShapeDtypeStruct(q.shape, F32),
        compiler_params=_params("parallel", "parallel"),
    )(q, k, k, v, v, do, lse, delta)


def _attn_bwd_kv(q, k, v, do, lse_t, delta_t, d, name):
    nb = q.shape[0] // BAND
    nblk = _blocks_per_step(nb)
    units = _units(nblk)
    cur, _, nxt = _attn_specs(nb, nblk)
    t_cur = pl.BlockSpec((1, N_HEADS, nblk * BAND), lambda r, j: (r, 0, j))
    t_nxt = pl.BlockSpec((1, N_HEADS, BAND), lambda r, j: (r, 0, jnp.minimum(nblk * (j + 1), nb - 1)))

    def body(k_ref, v_ref, qc_ref, qn_ref, doc_ref, don_ref, lc_ref, ln_ref, dlc_ref, dln_ref, dk_ref, dv_ref):
        band, has_next = _band_valid(pl.program_id(1) < nb // nblk - 1)
        col = lax.broadcasted_iota(jnp.int32, (BAND, 2 * BAND), 1)
        valid = [band] * (nblk - 1) + [band & ((col < BAND) | has_next)]

        def head_row(c_ref, n_ref, b, h):
            if b == nblk - 1:
                return jnp.concatenate([c_ref[0, h:h + 1, b * BAND:(b + 1) * BAND], n_ref[0, h:h + 1, :]], axis=1)
            return c_ref[0, h:h + 1, b * BAND:(b + 2) * BAND]

        def products(unit):
            b, hp, hh = unit
            sl, lm = _pair_cols(hp), _pair_mask(hh)
            return (_dotg(_masked(lm, k_ref[_rows(b), sl]), _with_next(qc_ref, qn_ref, b, sl, nblk), 1, 1),
                    _dotg(_masked(lm, v_ref[_rows(b), sl]), _with_next(doc_ref, don_ref, b, sl, nblk), 1, 1))

        ahead, half = products(units[0]), None
        for idx, (b, hp, hh) in enumerate(units):
            raw, dpt = ahead
            if idx + 1 < len(units):
                ahead = products(units[idx + 1])
            sl, lm, h = _pair_cols(hp), _pair_mask(hh), 2 * hp + hh
            st = jnp.where(valid[b], raw * SCALE, NEG_INF)
            pt = jnp.exp(st - head_row(lc_ref, ln_ref, b, h))
            dv = _dotg(pt, _with_next(doc_ref, don_ref, b, sl, nblk), 1, 0)
            dst = pt * (dpt - head_row(dlc_ref, dln_ref, b, h))
            dk = _dotg(dst, _with_next(qc_ref, qn_ref, b, sl, nblk), 1, 0) * SCALE
            if hh == 0:
                half = (dk, dv)
            else:
                dk_ref[_rows(b), sl] = jnp.where(lm, dk, half[0])
                dv_ref[_rows(b), sl] = jnp.where(lm, dv, half[1])

    return pl.pallas_call(
        body, name=name, grid=(d, nb // nblk),
        in_specs=[cur, cur, cur, nxt, cur, nxt, t_cur, t_nxt, t_cur, t_nxt], out_specs=[cur, cur],
        out_shape=[jax.ShapeDtypeStruct(q.shape, F32), jax.ShapeDtypeStruct(q.shape, F32)],
        compiler_params=_params("parallel", "parallel"),
    )(k, v, q, q, do, do, lse_t, lse_t, delta_t, delta_t)


def _spread_rows(a, d):
    return a.reshape(N_HEADS, a.shape[1] // d, d).transpose(2, 0, 1)


FF_TS = 512
FF_TC = 1024
FF_SUB = 256
HALO = 8
HALO_BF16 = 16
UP_BLOCKS = D_MODEL // FF_TC


def _conv3(ext, w, b):
    return b + w[0:1, :] * pltpu.roll(ext, 2, 0) + w[1:2, :] * pltpu.roll(ext, 1, 0) + w[2:3, :] * ext


def _ffn_specs(s, cols_first):
    nrb = s // FF_TS
    per, per16 = FF_TS // HALO, FF_TS // HALO_BF16

    def mk(block, fn):
        if cols_first:
            return pl.BlockSpec(block, lambda j, i: fn(i, j))
        return pl.BlockSpec(block, lambda i, j: fn(i, j))

    specs = types.SimpleNamespace(
        nrb=nrb, ncb=D_FF // FF_TC,
        row=mk((FF_TS, FF_TC), lambda i, j: (i, j)),
        before=mk((HALO, FF_TC), lambda i, j: (jnp.maximum(i * per - 1, 0), j)),
        after=mk((HALO, FF_TC), lambda i, j: (jnp.minimum((i + 1) * per, nrb * per - 1), j)),
        w=mk((3, FF_TC), lambda i, j: (0, j)),
        b=mk((1, FF_TC), lambda i, j: (0, j)),
        part=mk((HALO, FF_TC), lambda i, j: (i, j)),
        act=mk((FF_TS, D_MODEL), lambda i, j: (i, 0)),
        act_before=mk((HALO_BF16, D_MODEL), lambda i, j: (jnp.maximum(i * per16 - 1, 0), 0)),
        act_after=mk((HALO_BF16, D_MODEL), lambda i, j: (jnp.minimum((i + 1) * per16, nrb * per16 - 1), 0)),
        up_gate=mk((None, D_MODEL, FF_TC), lambda i, j: (j // UP_BLOCKS, 0, j % UP_BLOCKS)),
        up_val=mk((None, D_MODEL, FF_TC), lambda i, j: (N_DEV // 2 + j // UP_BLOCKS, 0, j % UP_BLOCKS)),
        down=mk((FF_TC, D_MODEL), lambda i, j: (j, 0)),
    )
    return specs


def _ffn_up_geglu(h, w_up, wg, wv, bg, bv, name):
    s = h.shape[0]
    sp = _ffn_specs(s, True)

    def body(h_ref, hb_ref, ugw_ref, uvw_ref, wg_ref, wv_ref, bg_ref, bv_ref, ug_ref, uv_ref, y_ref):
        keep = jnp.where(pl.program_id(1) > 0, 1.0, 0.0).astype(BF16)
        hext = jnp.concatenate([hb_ref[...] * keep, h_ref[...]], axis=0)
        n_sub = FF_TC // FF_SUB
        cols = [slice(c * FF_SUB, (c + 1) * FF_SUB) for c in range(n_sub)]
        up_gate = lambda c: _dotg(hext, ugw_ref[:, cols[c]], 1, 0)
        up_val = lambda c: _dotg(hext, uvw_ref[:, cols[c]], 1, 0)
        eg, ev = up_gate(0), up_val(0)
        for c in range(n_sub):
            sl, more = cols[c], c + 1 < n_sub
            eg_next = up_gate(c + 1) if more else None
            ug_ref[:, sl] = eg[HALO_BF16:, :]
            act = _gelu_tanh(_conv3(eg, wg_ref[:, sl], bg_ref[:, sl])[HALO_BF16:, :])
            ev_next = up_val(c + 1) if more else None
            uv_ref[:, sl] = ev[HALO_BF16:, :]
            y_ref[:, sl] = (act * _conv3(ev, wv_ref[:, sl], bv_ref[:, sl])[HALO_BF16:, :]).astype(y_ref.dtype)
            eg, ev = eg_next, ev_next

    return pl.pallas_call(
        body, name=name, grid=(sp.ncb, sp.nrb),
        in_specs=[sp.act, sp.act_before, sp.up_gate, sp.up_val, sp.w, sp.w, sp.b, sp.b],
        out_specs=[sp.row, sp.row, sp.row],
        out_shape=[jax.ShapeDtypeStruct((s, D_FF), F32), jax.ShapeDtypeStruct((s, D_FF), F32),
                   jax.ShapeDtypeStruct((s, D_FF), BF16)],
        compiler_params=_params("parallel", "parallel"),
    )(h, h, w_up, w_up, wg, wv, bg, bv)


def _sum_parts(parts, name):
    n = parts.shape[0] // HALO

    def body(p_ref, o_ref):
        acc = p_ref[0:HALO, :]
        for t in range(1, n):
            acc = acc + p_ref[t * HALO:(t + 1) * HALO, :]
        o_ref[...] = acc

    return pl.pallas_call(body, name=name, out_shape=jax.ShapeDtypeStruct((HALO, parts.shape[1]), F32))(parts)


def _ffn_geglu_bwd(ug, uv, df, w_down, w_up, wg, wv, bg, bv, name):
    s = ug.shape[0]
    sp = _ffn_specs(s, False)
    nrb = sp.nrb
    rows = FF_TS + 2 * HALO
    lo, hi = HALO, HALO + FF_TS

    def body(ug_ref, uv_ref, hg_ref, hv_ref, ng_ref, nv_ref, df_ref, dfn_ref, dw_ref, ugw_ref, uvw_ref,
             wg_ref, wv_ref, bg_ref, bv_ref, dug_ref, duv_ref, dh_ref, dwg_ref, dwv_ref):
        i, j = pl.program_id(0), pl.program_id(1)
        keep_top = jnp.where(i > 0, 1.0, 0.0)
        keep_bot = jnp.where(i < nrb - 1, 1.0, 0.0).astype(BF16)
        dfe = jnp.concatenate([df_ref[...], dfn_ref[...] * keep_bot], axis=0)

        def back(dc, e, w, du_ref, sl):
            up1 = pltpu.roll(dc, rows - 1, 0)
            up2 = pltpu.roll(dc, rows - 2, 0)
            du = (w[2:3, :] * dc + w[1:2, :] * up1 + w[0:1, :] * up2)[lo:hi, :].astype(BF16)
            du_ref[:, sl] = du
            p1, p2 = up1 * e, up2 * e
            colsum = lambda p: jnp.sum(p[lo:hi, :], axis=0, keepdims=True)
            row = lambda p, t: p[t:t + 1, :]
            d_w1 = colsum(p1) + row(p1, lo - 1) - row(p1, hi - 1)
            d_w0 = colsum(p2) + row(p2, lo - 2) + row(p2, lo - 1) - row(p2, hi - 2) - row(p2, hi - 1)
            sums = [d_w0, d_w1, colsum(dc * e), colsum(dc), jnp.zeros((HALO - 4, FF_SUB), F32)]
            return du, jnp.concatenate(sums, axis=0)

        dh = None
        n_sub = FF_TC // FF_SUB
        cols = [slice(c * FF_SUB, (c + 1) * FF_SUB) for c in range(n_sub)]
        d_act = lambda c: _dotg(dfe, dw_ref[cols[c], :], 1, 1)[:FF_TS + HALO, :]
        ahead = d_act(0)
        for c in range(n_sub):
            sl, dy = cols[c], ahead
            if c + 1 < n_sub:
                ahead = d_act(c + 1)
            dye = jnp.concatenate([jnp.zeros((HALO, FF_SUB), F32), dy], axis=0)
            eg = jnp.concatenate([hg_ref[:, sl] * keep_top, ug_ref[:, sl], ng_ref[:, sl]], axis=0)
            ev = jnp.concatenate([hv_ref[:, sl] * keep_top, uv_ref[:, sl], nv_ref[:, sl]], axis=0)
            wg_, wv_ = wg_ref[:, sl], wv_ref[:, sl]
            gate = _conv3(eg, wg_, bg_ref[:, sl])
            val = _conv3(ev, wv_, bv_ref[:, sl])
            act, slope = _gelu_tanh_and_slope(gate)
            dug, dwg_ref[:, sl] = back((dye * val) * slope, eg, wg_, dug_ref, sl)
            duv, dwv_ref[:, sl] = back(dye * act, ev, wv_, duv_ref, sl)
            term = _dotg(dug, ugw_ref[:, sl], 1, 1) + _dotg(duv, uvw_ref[:, sl], 1, 1)
            dh = term if dh is None else dh + term

        @pl.when(j == 0)
        def _():
            dh_ref[...] = dh

        @pl.when(j > 0)
        def _():
            dh_ref[...] += dh

    parts = jax.ShapeDtypeStruct((nrb * HALO, D_FF), F32)
    dug, duv, dh, pg, pv = pl.pallas_call(
        body, name=name, grid=(nrb, sp.ncb),
        in_specs=[sp.row, sp.row, sp.before, sp.before, sp.after, sp.after, sp.act, sp.act_after, sp.down,
                  sp.up_gate, sp.up_val, sp.w, sp.w, sp.b, sp.b],
        out_specs=[sp.row, sp.row, sp.act, sp.part, sp.part],
        out_shape=[jax.ShapeDtypeStruct((s, D_FF), BF16), jax.ShapeDtypeStruct((s, D_FF), BF16),
                   jax.ShapeDtypeStruct((s, D_MODEL), F32), parts, parts],
        compiler_params=_params("parallel", "arbitrary"),
    )(ug, uv, ug, uv, ug, uv, df, df, w_down, w_up, w_up, wg, wv, bg, bv)
    return dug, duv, dh, _sum_parts(pg, name=name + "_sum_gate"), _sum_parts(pv, name=name + "_sum_val")


def _rope_tables(s):
    inv = ROPE_THETA ** (-jnp.arange(0, ROT_DIM, 2, dtype=F32) / ROT_DIM)
    ang = jnp.arange(s, dtype=F32)[:, None] * inv[None, :]
    cos8, sin8 = jnp.cos(ang), jnp.sin(ang)
    rest = HEAD_DIM - ROT_DIM
    cos_h = jnp.concatenate([cos8, cos8, jnp.ones((s, rest), F32)], axis=1)
    sin_h = jnp.concatenate([sin8, sin8, jnp.zeros((s, rest), F32)], axis=1)
    return jnp.tile(cos_h, (1, LANES // HEAD_DIM)), jnp.tile(sin_h, (1, LANES // HEAD_DIM))


def _all_heads(table):
    return jnp.concatenate([table] * (B_WIDTH // LANES), axis=1)


def _layer_fwd(x, w, cos, sin, last):
    sv = types.SimpleNamespace(x=x)
    (sv.h1,), _ = _rowwise(lambda xb, g: ((_rms(xb, g),), ()), [_full(x)], [w.g_pre], [(D_MODEL, BF16)], [],
                           ts=512, name="pre_mix_norm")
    sv.proj = _matmul(sv.h1, w.big("w_in", sv.h1), mode="nt", out_dtype=F32, name="proj")

    gate_consts = [w.vg, w.vb, *w.ws, w.bfull, w.ga]
    def gate_fwd_fn(zu, zv, *consts):
        chunks = [_gate_fn(zu[r], zv[r], *consts) for r in _chunk_rows(zu.shape[0])]
        return (jnp.concatenate(chunks, axis=0),), ()

    (na,), _ = _rowwise(gate_fwd_fn, [(sv.proj, A_WIDTH, 0), (sv.proj, A_WIDTH, 1)], gate_consts,
                        [(A_WIDTH, BF16)], [], ts=GATE_TS, name="gate_fwd")

    def rope_fn(qr, kr, vr, cs, sn):
        cs, sn = _all_heads(cs), _all_heads(sn)
        return (qr * cs + _rot_half(qr) * sn, kr * cs + _rot_half(kr) * sn, vr), ()

    def rope_all(qr, kr, vr, cs, sn):
        return rope_fn(qr, kr, vr, cs, sn)[0] * len(DILATIONS), ()

    qkv, _ = _rowwise(
        rope_all, [(sv.proj, B_WIDTH, 2), (sv.proj, B_WIDTH, 3), (sv.proj, B_WIDTH, 4), _full(cos), _full(sin)], [],
        [(B_WIDTH, BF16, d) for d in DILATIONS for _ in range(3)], [], ts=512, name="rope_fwd")
    sv.qkv = {d: qkv[3 * i:3 * i + 3] for i, d in enumerate(DILATIONS)}

    branch = []
    for d in DILATIONS:
        o, l = _attn_fwd(*sv.qkv[d], d, name=f"attn_fwd_d{d}")
        branch += [_full(o, d), _full(l, d)]

    def combine_fn(o1, l1, o2, l2, o3, l3, nab, gb):
        m = jnp.maximum(jnp.maximum(l1, l2), l3)
        e1, e2, e3 = jnp.exp(l1 - m), jnp.exp(l2 - m), jnp.exp(l3 - m)
        den = e1 + e2 + e3
        ob = (e1 / den) * o1 + (e2 / den) * o2 + (e3 / den) * o3
        mixed = jnp.concatenate([nab, _rms(ob, gb).astype(BF16)], axis=1)
        lse = _head_cols(m + jnp.log(den))
        return (mixed, ob, lse) + (lse,) * len(DILATIONS), ()

    (sv.mixed, sv.ob, sv.lse_rows, *lses), _ = _rowwise(
        combine_fn, branch + [_full(na)], [w.gb],
        [(D_MODEL, BF16), (B_WIDTH, F32), (LANES, F32, HEAD_ROWS)] + [(LANES, F32, d) for d in DILATIONS], [],
        ts=512, name="combine")
    sv.lse = dict(zip(DILATIONS, lses))
    sv.y = _matmul(sv.mixed, w.big("w_out", sv.mixed), mode="nn", out_dtype=F32, name="mix_out")

    def mid_fn(xb, yb, g1, g2):
        x1 = xb + _rms(yb, g1)
        return (x1, _rms(x1, g2)), ()

    (sv.x1, sv.h2), _ = _rowwise(mid_fn, [_full(x), _full(sv.y)], [w.g_pm, w.g_pf], [(D_MODEL, F32), (D_MODEL, BF16)], [],
                                 ts=512, name="post_mix_norm")
    conv_w = w.big("conv_w", sv.h2)
    sv.ug, sv.uv, sv.yff = _ffn_up_geglu(sv.h2, w.big("w_up", sv.h2), conv_w[:, :D_FF], conv_w[:, D_FF:],
                                         w.cb_g, w.cb_v, name="ffn_up_geglu")
    sv.f = _matmul(sv.yff, w.big("w_down", sv.yff), mode="nn", out_dtype=F32, name="ffn_down")
    if last:
        return None, sv
    (x2,), _ = _rowwise(lambda xb, fb, g: ((xb + _rms(fb, g),), ()), [_full(sv.x1), _full(sv.f)], [w.g_post],
                        [(D_MODEL, F32)], [], ts=512, name="post_ffn_norm")
    return x2, sv


def _layer_bwd(dx2, sv, w, cos, sin, emit):
    g = {}

    def post_fn(fb, dxb, gp):
        _, vjp = jax.vjp(_rms, fb, gp)
        df, dg = vjp(dxb)
        return (df,), (dg,)

    (df,), (g["post_ffn_norm"],) = _rowwise(post_fn, [_full(sv.f), _full(dx2)], [w.g_post], [(D_MODEL, BF16)],
                                            [(1, D_MODEL)], ts=512, name="post_ffn_norm_bwd")
    big = {"w_down": _matmul(sv.yff, df, mode="tn", out_dtype=BF16, name="ffn_down_dw").reshape(N_DEV, -1, D_MODEL)}
    conv_w = w.big("conv_w", df)
    dug, duv, dh2, dwg, dwv = _ffn_geglu_bwd(sv.ug, sv.uv, df, w.big("w_down", df), w.big("w_up", df),
                                             conv_w[:, :D_FF], conv_w[:, D_FF:], w.cb_g, w.cb_v, name="ffn_geglu_bwd")
    big["conv_w"] = jnp.concatenate([dwg[0:3], dwv[0:3]], axis=1).reshape(3, N_DEV, D_MODEL).transpose(1, 0, 2)
    g["conv_b"] = jnp.concatenate([dwg[3], dwv[3]], axis=0)
    big["w_up"] = _matmul_by_destination(sv.h2, dug, duv, name="ffn_up_dw")
    g_pm = w.g_pm + emit(big)

    def mid_fn(x1b, yb, dhb, dxb, g1, g2):
        _, vjp2 = jax.vjp(_rms, x1b, g2)
        dx1h, dg2 = vjp2(dhb)
        dx1 = dxb + dx1h
        _, vjp1 = jax.vjp(_rms, yb, g1)
        dy, dg1 = vjp1(dx1)
        return (dx1, dy), (dg1, dg2)

    (dx1, dy), (g["post_mix_norm"], g["pre_ffn_norm"]) = _rowwise(
        mid_fn, [_full(sv.x1), _full(sv.y), _full(dh2), _full(dx2)], [g_pm, w.g_pf],
        [(D_MODEL, F32), (D_MODEL, BF16)], [(1, D_MODEL), (1, D_MODEL)], ts=256, name="post_mix_norm_bwd")
    dmixed = _matmul(dy, w.big("w_out", dy), mode="nt", out_dtype=F32, name="mix_out_dx")
    big = {"w_out": _matmul(sv.mixed, dy, mode="tn", out_dtype=BF16, name="mix_out_dw").reshape(N_DEV, -1, D_MODEL)}

    def attn_out_fn(obb, dmb, gb):
        _, vjp = jax.vjp(_rms, obb, gb)
        do, dgb = vjp(dmb)
        delta = _head_cols(_head_sum(do * obb))
        return (delta,) + (do,) * len(DILATIONS) + (delta,) * len(DILATIONS), (dgb,)

    (delta_rows, *outs), (g["out_norm_b"],) = _rowwise(
        attn_out_fn, [_full(sv.ob), (dmixed, B_WIDTH, 1)], [w.gb],
        [(LANES, F32, HEAD_ROWS)] + [(B_WIDTH, BF16, d) for d in DILATIONS] + [(LANES, F32, d) for d in DILATIONS],
        [(1, B_WIDTH)], ts=512, name="attn_out_bwd")
    do = dict(zip(DILATIONS, outs[:len(DILATIONS)]))
    delta = dict(zip(DILATIONS, outs[len(DILATIONS):]))
    parts = {"q": [], "k": [], "v": []}
    for d in DILATIONS:
        qv, kv, vv = sv.qkv[d]
        dq = _attn_bwd_q(qv, kv, vv, do[d], sv.lse[d], delta[d], d, name=f"attn_bwd_q_d{d}")
        dk, dv = _attn_bwd_kv(qv, kv, vv, do[d], _spread_rows(sv.lse_rows, d), _spread_rows(delta_rows, d), d,
                              name=f"attn_bwd_kv_d{d}")
        parts["q"].append(_full(dq, d))
        parts["k"].append(_full(dk, d))
        parts["v"].append(_full(dv, d))

    def rope_bwd_fn(q1, q2, q3, k1, k2, k3, v1, v2, v3, cs, sn):
        cs, sn = _all_heads(cs), _all_heads(sn)

        def back(t):
            return t * cs - _rot_half(t * sn)
        return (jnp.concatenate([back(q1 + q2 + q3), back(k1 + k2 + k3), v1 + v2 + v3], axis=1),), ()

    (dzb,), _ = _rowwise(rope_bwd_fn, parts["q"] + parts["k"] + parts["v"] + [_full(cos), _full(sin)], [],
                         [(3 * B_WIDTH, BF16)], [], ts=256, name="rope_bwd")

    gate_consts = [w.vg, w.vb, *w.ws, w.bfull, w.ga]

    def gate_bwd_fn(zu, zv, dna, *consts):
        dz, sums = [], None
        for r in _chunk_rows(zu.shape[0]):
            _, vjp = jax.vjp(_gate_fn, zu[r], zv[r], *consts)
            grads = vjp(dna[r])
            dz.append(jnp.concatenate([grads[0], grads[1]], axis=1))
            sums = grads[2:] if sums is None else tuple(a + b for a, b in zip(sums, grads[2:]))
        return (jnp.concatenate(dz, axis=0),), tuple(sums)

    (dza,), gsmall = _rowwise(
        gate_bwd_fn, [(sv.proj, A_WIDTH, 0), (sv.proj, A_WIDTH, 1), (dmixed, A_WIDTH, 0)], gate_consts,
        [(2 * A_WIDTH, BF16)], [c.shape for c in gate_consts], ts=GATE_TS, name="gate_bwd")
    g["v_norm_g"], g["v_norm_b"] = gsmall[0], gsmall[1]
    g["w_spatial"] = jnp.stack(gsmall[2:6])
    g["b_spatial"] = _bias_reduce(gsmall[6], name="bias_reduce")[:, :A_GROUPS].T
    g["out_norm_a"] = gsmall[7]

    dproj = jnp.concatenate([dza, dzb], axis=1)
    dh1 = _matmul(dproj, w.big("w_in", dproj), mode="nn", out_dtype=F32, name="proj_dx")
    big["w_in"] = _matmul(dproj, sv.h1, mode="tn", out_dtype=BF16, name="proj_dw").reshape(N_DEV, -1, D_MODEL)
    g_pre = w.g_pre + emit(big)

    def pre_fn(xb, dhb, dxb, gp):
        _, vjp = jax.vjp(_rms, xb, gp)
        dxh, dg = vjp(dhb)
        return (dxb + dxh,), (dg,)

    (dx,), (g["pre_mix_norm"],) = _rowwise(pre_fn, [_full(sv.x), _full(dh1), _full(dx1)], [g_pre], [(D_MODEL, F32)],
                                           [(1, D_MODEL)], ts=512, name="pre_mix_norm_bwd")
    return dx, g


def _layer_weights(l, full, small):
    row = lambda a: a[l].reshape(1, -1)
    return types.SimpleNamespace(
        big=functools.partial(full, l),
        g_pre=row(small["pre_mix_norm"]), vg=row(small["v_norm_g"]), vb=row(small["v_norm_b"]),
        ws=[small["w_spatial"][l, gi] for gi in range(A_GROUPS)],
        bfull=jnp.repeat(small["b_spatial"][l].T, CHUNK, axis=1),
        ga=row(small["out_norm_a"]), gb=row(small["out_norm_b"]),
        g_pm=row(small["post_mix_norm"]), g_pf=row(small["pre_ffn_norm"]),
        cb_g=small["conv_b"][l][:D_FF].reshape(1, -1), cb_v=small["conv_b"][l][D_FF:].reshape(1, -1),
        g_post=row(small["post_ffn_norm"]))


def _local_step(x, target, full, small, emit, started):
    s = x.shape[0]
    cos, sin = _rope_tables(s)
    ws = [_layer_weights(l, full, small) for l in range(N_LAYERS)]
    ws[0].g_pre = ws[0].g_pre + started
    saved = []
    h = x
    for l in range(N_LAYERS):
        h, sv = _layer_fwd(h, ws[l], cos, sin, last=l == N_LAYERS - 1)
        saved.append(sv)

    def loss_fn(xb, fb, tb, g):
        diff = (xb + _rms(fb, g)) - tb
        return (diff * (1.0 / D_MODEL),), (jnp.sum(diff * diff, axis=0, keepdims=True),)

    (dh,), (sq,) = _rowwise(loss_fn, [_full(saved[-1].x1), _full(saved[-1].f), _full(target)], [ws[-1].g_post],
                            [(D_MODEL, F32)], [(1, D_MODEL)], ts=512, name="loss")
    loss = 0.5 * jnp.sum(sq) * (1.0 / D_MODEL)
    grads = [None] * N_LAYERS
    for l in reversed(range(N_LAYERS)):
        dh, grads[l] = _layer_bwd(dh, saved[l], ws[l], cos, sin, functools.partial(emit, l))
    return loss, dh, grads


def _place():
    return lax.axis_index("x"), lax.axis_index("y"), lax.axis_index("c")


FLIPS = ((1, 0, 0), (0, 1, 0), (1, 1, 0), (0, 0, 1), (1, 0, 1), (0, 1, 1), (1, 1, 1))
HBM_SPEC = pl.BlockSpec(memory_space=pltpu.HBM)
SEM_SPEC = pl.BlockSpec(memory_space=pltpu.SEMAPHORE)
SPLIT_COPY = pltpu.CompilerParams(has_side_effects=pltpu.SideEffectType.DATAFLOW_SIDE_EFFECTING)


def _peers():
    mx, my, mc = _place()
    out = []
    for fx, fy, fc in FLIPS:
        px, py, pc = (1 - mx if fx else mx), (1 - my if fy else my), (1 - mc if fc else mc)
        out.append(((px, py, pc), 4 * px + 2 * py + pc))
    return out


def _flat_copies(scatter, src_refs, land_refs, send_sems, recv_sems):
    mx, my, mc = _place()
    me = 4 * mx + 2 * my + mc
    n = len(src_refs)
    copies = []
    for t in range(n):
        for i, (peer, number) in enumerate(_peers()):
            copies.append(pltpu.make_async_remote_copy(
                src_ref=src_refs[t].at[number] if scatter else src_refs[t],
                dst_ref=land_refs[t].at[i] if scatter else land_refs[t].at[me],
                send_sem=send_sems.at[t * len(FLIPS) + i], recv_sem=recv_sems.at[t * len(FLIPS) + i],
                device_id=peer, device_id_type=MESH_ID))
    return copies


def _flat_start(arrays, scatter, name):
    n = len(arrays)
    slots = len(FLIPS) if scatter else N_DEV
    lands = [lax.empty((slots,) + (a.shape[1:] if scatter else a.shape), a.dtype) for a in arrays]

    def body(*refs):
        src, land, (send_sems, recv_sems), token = refs[:n], refs[n:2 * n], refs[2 * n:2 * n + 2], refs[-1]
        for cp in _flat_copies(scatter, src, land, send_sems, recv_sems):
            cp.start()
        token[...] = jnp.zeros_like(token)

    hbm = [pltpu.HBM(a.shape, a.dtype) for a in arrays] + [pltpu.HBM(a.shape, a.dtype) for a in lands]
    sems = pltpu.SemaphoreType.DMA((n * len(FLIPS),))
    outs = pl.pallas_call(
        body, name=name, out_shape=(sems, sems, *hbm, jax.ShapeDtypeStruct((8, 128), F32)),
        in_specs=[HBM_SPEC] * (2 * n),
        out_specs=(SEM_SPEC, SEM_SPEC, *([HBM_SPEC] * (2 * n)), pl.BlockSpec(memory_space=pltpu.VMEM)),
        input_output_aliases={t: 2 + t for t in range(2 * n)}, compiler_params=SPLIT_COPY,
    )(*[pltpu.with_memory_space_constraint(a, pltpu.HBM) for a in (*arrays, *lands)])
    return types.SimpleNamespace(sems=outs[:2], thru=outs[2:2 + 2 * n], scatter=scatter, n=n), outs[-1][0:1, 0:1]


def _flat_wait(handle, after, name):
    n = handle.n

    def body(*refs):
        src, land, (send_sems, recv_sems) = refs[:n], refs[n:2 * n], refs[2 * n:2 * n + 2]
        for cp in _flat_copies(handle.scatter, src, land, send_sems, recv_sems):
            cp.wait_send()
            cp.wait_recv()

    outs = pl.pallas_call(
        body, name=name, out_shape=tuple(pltpu.HBM(a.shape, a.dtype) for a in handle.thru),
        in_specs=[HBM_SPEC] * (2 * n) + [SEM_SPEC, SEM_SPEC, ANY], out_specs=tuple([HBM_SPEC] * (2 * n)),
        input_output_aliases={t: t for t in range(2 * n)}, compiler_params=SPLIT_COPY,
    )(*handle.thru, *handle.sems, after)
    return outs[:n], outs[n:]


def _adamw(w, g, m, v):
    m2 = ADAM_B1 * m + (1.0 - ADAM_B1) * g
    v2 = ADAM_B2 * v + (1.0 - ADAM_B2) * (g * g)
    m_hat = m2 / (1.0 - ADAM_B1 ** ADAM_STEP)
    v_hat = v2 / (1.0 - ADAM_B2 ** ADAM_STEP)
    return -ADAM_LR * (m_hat / (jnp.sqrt(v_hat) + ADAM_EPS) + ADAM_WD * w), m2, v2


def _adamw_sharded(me, mine, landed, w, m, v, tr, name):
    _, r, c = w.shape
    nt = r // tr
    assert r % tr == 0 and len(mine) == len(landed) == N_LAYERS == 2, name
    per_layer = 1 + len(FLIPS)

    def body(me_ref, *refs):
        terms, (w_ref, m_ref, v_ref), outs = refs[:2 * per_layer], refs[2 * per_layer:2 * per_layer + 3], refs[-4:]
        layer = pl.program_id(0)

        def total(group):
            g = group[0][0].astype(F32)
            for t in group[1:]:
                g = g + t[0].astype(F32)
            return g

        g = jnp.where(layer == 0, total(terms[:per_layer]), total(terms[per_layer:]))
        d, m2, v2 = _adamw(w_ref[0], g, m_ref[0], v_ref[0])
        for o, val in zip(outs, (g, d, m2, v2)):
            o[0] = val

    def held(l):
        return lambda layer, i: jnp.where(layer == l, i, nt - 1 if l == 0 else 0)

    in_specs = []
    for l in range(N_LAYERS):
        rows = held(l)
        in_specs.append(pl.BlockSpec((1, tr, c), functools.partial(lambda layer, i, me_ref, rows: (me_ref[0], rows(layer, i), 0), rows=rows)))
        for k in range(len(FLIPS)):
            in_specs.append(pl.BlockSpec(
                (1, tr, c), functools.partial(lambda layer, i, me_ref, rows, k: (k, rows(layer, i), 0), rows=rows, k=k)))
    tile = pl.BlockSpec((1, tr, c), lambda layer, i, me_ref: (layer, i, 0))
    operands = []
    for l in range(N_LAYERS):
        operands += [mine[l]] + [landed[l]] * len(FLIPS)
    return pl.pallas_call(
        body, name=name, out_shape=[jax.ShapeDtypeStruct(w.shape, F32)] * 4,
        grid_spec=pltpu.PrefetchScalarGridSpec(
            num_scalar_prefetch=1, grid=(N_LAYERS, nt), in_specs=in_specs + [tile] * 3, out_specs=[tile] * 4),
        compiler_params=_params("arbitrary", "arbitrary"),
    )(me, *operands, w, m, v)


def _adamw_replicated(parts, w, m, v, name):
    def body(p_ref, w_ref, m_ref, v_ref, g_ref, d_ref, m2_ref, v2_ref):
        g = p_ref[0]
        for j in range(1, N_DEV):
            g = g + p_ref[j]
        d, m2, v2 = _adamw(w_ref[...], g, m_ref[...], v_ref[...])
        g_ref[...], d_ref[...], m2_ref[...], v2_ref[...] = g, d, m2, v2

    return pl.pallas_call(body, name=name, out_shape=[jax.ShapeDtypeStruct(w.shape, F32)] * 4,
                          compiler_params=pltpu.CompilerParams(vmem_limit_bytes=VMEM_LIMIT_BYTES))(parts, w, m, v)


def _pack_small(vals):
    flat = jnp.concatenate([vals[n].reshape(-1) for n in SMALL_NAMES])
    return jnp.concatenate([flat, jnp.zeros((SMALL_ROWS * D_MODEL - flat.shape[0],), F32)]).reshape(SMALL_ROWS, D_MODEL)


def _unpack_small(packed, shapes):
    flat, out, at = packed.reshape(-1), {}, 0
    for n in SMALL_NAMES:
        size = math.prod(shapes[n])
        out[n] = flat[at:at + size].reshape(shapes[n])
        at += size
    return out


GATHER_GROUPS = ((0, ("w_in",)), (0, ("w_out", "w_up", "conv_w")), (0, ("w_down",)),
                 (1, ("w_in", "w_out", "w_up", "conv_w")), (1, ("w_down",)))
ADAMW_TILE_ROWS = {"w_in": 320, "w_out": 128, "w_up": 256, "w_down": 256, "conv_w": 3}


def _assemble(name, land):
    if name == "w_in":
        return land.reshape(IN_COLS, D_MODEL)
    if name == "conv_w":
        return land.transpose(1, 0, 2).reshape(3, 2 * D_FF)
    if name == "w_up":
        return land
    return land.reshape(-1, D_MODEL)


def _start_gathers(wts, me):
    started, groups = jnp.zeros((1, 1), F32), []
    for gi, (l, names) in enumerate(GATHER_GROUPS):
        local = {"conv_w": lambda a: a, "w_in": lambda a: a.T.astype(BF16)}
        blocks = [local.get(n, lambda a: a.astype(BF16))(wts[n][l]) for n in names]
        handle, token = _flat_start(blocks, False, name=f"gather_start_{gi}")
        groups.append(types.SimpleNamespace(layer=l, names=names, blocks=blocks, handle=handle, got=None, index=gi))
        started = started + token

    def fetch(l, name, after):
        grp = next(gr for gr in groups if gr.layer == l and name in gr.names)
        if grp.got is None:
            lands = _flat_wait(grp.handle, after, name=f"gather_wait_{grp.index}")[1]
            grp.got = {}
            for n, blk, land in zip(grp.names, grp.blocks, lands):
                own = (me,) + (0,) * blk.ndim
                grp.got[n] = _assemble(n, lax.dynamic_update_slice(land, blk[None], own))
        return grp.got[name]

    return fetch, started


def kernel(x, pre_mix_norm, w_in, v_norm_g, v_norm_b, w_spatial, b_spatial, out_norm_a, out_norm_b, w_out, post_mix_norm, pre_ffn_norm, w_up, conv_w, conv_b, w_down, post_ffn_norm, loss_target, m_pre_mix_norm, m_w_in, m_v_norm_g, m_v_norm_b, m_w_spatial, m_b_spatial, m_out_norm_a, m_out_norm_b, m_w_out, m_post_mix_norm, m_pre_ffn_norm, m_w_up, m_conv_w, m_conv_b, m_w_down, m_post_ffn_norm, v_pre_mix_norm, v_w_in, v_v_norm_g, v_v_norm_b, v_w_spatial, v_b_spatial, v_out_norm_a, v_out_norm_b, v_w_out, v_post_mix_norm, v_pre_ffn_norm, v_w_up, v_conv_w, v_conv_b, v_w_down, v_post_ffn_norm):
    wts = dict(zip(WEIGHT_NAMES, (pre_mix_norm, w_in, v_norm_g, v_norm_b, w_spatial, b_spatial, out_norm_a, out_norm_b,
                                  w_out, post_mix_norm, pre_ffn_norm, w_up, conv_w, conv_b, w_down, post_ffn_norm)))
    mom1 = dict(zip(WEIGHT_NAMES, (m_pre_mix_norm, m_w_in, m_v_norm_g, m_v_norm_b, m_w_spatial, m_b_spatial, m_out_norm_a,
                                   m_out_norm_b, m_w_out, m_post_mix_norm, m_pre_ffn_norm, m_w_up, m_conv_w, m_conv_b,
                                   m_w_down, m_post_ffn_norm)))
    mom2 = dict(zip(WEIGHT_NAMES, (v_pre_mix_norm, v_w_in, v_v_norm_g, v_v_norm_b, v_w_spatial, v_b_spatial, v_out_norm_a,
                                   v_out_norm_b, v_w_out, v_post_mix_norm, v_pre_ffn_norm, v_w_up, v_conv_w, v_conv_b,
                                   v_w_down, v_post_ffn_norm)))
    mx, my, mc = _place()
    me = 4 * mx + 2 * my + mc

    fetch, started = _start_gathers(wts, me)
    scatters = []

    def emit(l, blocks):
        names = tuple(blocks)
        handle, token = _flat_start([blocks[n] for n in names], True, name=f"scatter_start_{l}_{len(scatters) % 2}")
        scatters.append((l, names, handle))
        return token

    loss_local, dx, grads = _local_step(x[0], loss_target[0], fetch, wts, emit, started)
    loss = lax.psum(loss_local, AXES)

    me_arr = jnp.reshape(me, (1,)).astype(jnp.int32)
    big_out = [{}, {}, {}, {}]

    def finish(group, after):
        mine, landed = {}, {}
        for l, names, handle in scatters:
            if names == group:
                sent, lands = _flat_wait(handle, after, name=f"scatter_wait_{l}_{'_'.join(names)}")
                for n, a, b in zip(names, sent, lands):
                    mine[l, n], landed[l, n] = a, b
        for n in group:
            flip = (lambda a: a.transpose(0, 2, 1)) if n == "w_in" else (lambda a: a)
            res = _adamw_sharded(me_arr, [mine[l, n] for l in range(N_LAYERS)], [landed[l, n] for l in range(N_LAYERS)],
                                 flip(wts[n]), flip(mom1[n]), flip(mom2[n]), ADAMW_TILE_ROWS[n], name=f"adamw_{n}")
            for kind in range(4):
                big_out[kind][n] = flip(res[kind])
        return res[0]

    early, late = scatters[0][1], scatters[1][1]
    small_grads = {n: jnp.stack([g[n].reshape(wts[n].shape[1:]) for g in grads]) for n in SMALL_NAMES}
    small_pack = _pack_small(small_grads)
    small_handle, _ = _flat_start([small_pack], False, name="small_grads_start")
    done_early = finish(early, small_handle.thru[0])
    (small_landed,) = _flat_wait(small_handle, done_early, name="small_grads_wait")[1]
    everyone = lax.dynamic_update_slice(small_landed, small_pack[None], (me, 0, 0))
    small = [_pack_small({n: t[n] for n in SMALL_NAMES}) for t in (wts, mom1, mom2)]
    small_out = _adamw_replicated(everyone, *small, name="adamw_replicated")
    finish(late, small_out[0])
    small_shapes = {n: wts[n].shape for n in SMALL_NAMES}
    small_out = [_unpack_small(o, small_shapes) for o in small_out]

    outs = [loss, dx[None]]
    for kind in range(4):
        outs += [big_out[kind][n] if n in BIG_NAMES else small_out[kind][n] for n in WEIGHT_NAMES]
    return tuple(outs)
```

```python
import functools
import math
import types

import jax
import jax.numpy as jnp
from jax import lax
from jax.experimental import pallas as pl
from jax.experimental.pallas import tpu as pltpu

F32 = jnp.float32
BF16 = jnp.bfloat16

D_MODEL = 1024
A_WIDTH = 512
A_GROUPS = 4
CHUNK = 128
B_WIDTH = 512
HEAD_DIM = 64
N_HEADS = B_WIDTH // HEAD_DIM
ROT_DIM = 16
ROPE_THETA = 500000.0
BAND = 128
DILATIONS = (1, 4, 16)
IN_COLS = 2560
D_FF = 4096
EPS = 1e-6
NEG_INF = -1e30
N_DEV = 8
N_LAYERS = 2

ADAM_LR = 0.001
ADAM_B1 = 0.9
ADAM_B2 = 0.999
ADAM_EPS = 1e-08
ADAM_WD = 0.01
ADAM_STEP = 10

VMEM_LIMIT_BYTES = 56 * 1024 * 1024
MESH_ID = pl.DeviceIdType.MESH
ANY = pl.BlockSpec(memory_space=pl.ANY)
AXES = ("x", "y", "c")

WEIGHT_NAMES = ("pre_mix_norm", "w_in", "v_norm_g", "v_norm_b", "w_spatial", "b_spatial", "out_norm_a", "out_norm_b",
                "w_out", "post_mix_norm", "pre_ffn_norm", "w_up", "conv_w", "conv_b", "w_down", "post_ffn_norm")
BIG_NAMES = ("w_in", "w_out", "w_up", "w_down", "conv_w")
SMALL_NAMES = tuple(n for n in WEIGHT_NAMES if n not in BIG_NAMES)

PACK_ROWS = {"w_in": 640, "w_out": 256, "w_up": 2048, "w_down": 1024, "conv_w": 6}
CONV_W_PAD = 2
PACKED_F32_ROWS = 4096
PACKED_BF16_ROWS = 3984
SMALL_ROWS = 80


def _params(*sem):
    return pltpu.CompilerParams(dimension_semantics=sem, vmem_limit_bytes=VMEM_LIMIT_BYTES)


def _dotg(a, b, ca, cb):
    return lax.dot_general(a.astype(BF16), b.astype(BF16), (((ca,), (cb,)), ((), ())), preferred_element_type=F32)


@jax.custom_vjp
def _bdot(a, b):
    return _dotg(a, b, 1, 0)


def _bdot_fwd(a, b):
    return _dotg(a, b, 1, 0), (a, b)


def _bdot_bwd(res, g):
    a, b = res
    return _dotg(g, b, 1, 1), _dotg(a, g, 0, 0)


_bdot.defvjp(_bdot_fwd, _bdot_bwd)


def _rms(x, g):
    return x * lax.rsqrt(jnp.mean(x * x, axis=-1, keepdims=True) + EPS) * g


def _layernorm(x, g, b):
    mu = jnp.mean(x, axis=-1, keepdims=True)
    xc = x - mu
    return xc * lax.rsqrt(jnp.mean(xc * xc, axis=-1, keepdims=True) + EPS) * g + b


def _gelu_erf(x):
    return x * (lax.erf(x * (1.0 / math.sqrt(2.0))) + 1.0) * 0.5


def _gelu_tanh(x):
    c = math.sqrt(2.0 / math.pi)
    return 0.5 * x * (1.0 + jnp.tanh(c * (x + 0.044715 * (x * x * x))))


def _gelu_tanh_and_slope(x):
    c, k = math.sqrt(2.0 / math.pi), 0.044715
    x2 = x * x
    t = jnp.tanh(c * (x + k * (x2 * x)))
    half_x, one_t = 0.5 * x, 1.0 + t
    return half_x * one_t, 0.5 * one_t + (half_x * (1.0 - t * t)) * (c + (3.0 * k * c) * x2)


def _rot_half(x):
    width = x.shape[1]
    lane = lax.broadcasted_iota(jnp.int32, x.shape, 1) % HEAD_DIM
    back = pltpu.roll(x, ROT_DIM // 2, 1)
    fwd = pltpu.roll(x, width - ROT_DIM // 2, 1)
    return jnp.where(lane < ROT_DIM // 2, -fwd, jnp.where(lane < ROT_DIM, back, 0.0))


def _split3(z):
    h0 = z.astype(BF16)
    r1 = z - h0.astype(F32)
    h1 = r1.astype(BF16)
    h2 = (r1 - h1.astype(F32)).astype(BF16)
    return h0, h1, h2


def _head_sum(z):
    width = z.shape[1]
    a = lax.broadcasted_iota(jnp.int32, (width, width), 0) // HEAD_DIM
    b = lax.broadcasted_iota(jnp.int32, (width, width), 1) // HEAD_DIM
    ones = jnp.where(a == b, 1.0, 0.0).astype(BF16)
    out = None
    for part in _split3(z):
        t = lax.dot_general(part, ones, (((1,), (0,)), ((), ())), preferred_element_type=F32)
        out = t if out is None else out + t
    return out


MATMUL_VMEM_BUDGET = 40 * 1024 * 1024


def _matmul_tiles(m, n, k, out_bytes):
    tn = n if n <= 1024 else (1280 if n % 1280 == 0 and n % 1024 else 1024)
    tk = k if k <= 1024 else (1280 if k % 1280 == 0 and k % 1024 else 1024)
    tm = m
    while tm > 256:
        blocks = 2 * 2 * (tm * tk + tk * tn) + 2 * out_bytes * tm * tn + (4 * tm * tn if k > tk else 0)
        if blocks <= MATMUL_VMEM_BUDGET and m % tm == 0:
            break
        tm //= 2
    return tm, tn, tk


def _matmul(a, b, *, mode, out_dtype, name, cols=None):
    wide = D_MODEL if cols is not None else None
    if mode == "nn":
        (m, k), (_, n) = a.shape, (b.shape if cols is None else (b.shape[1], cols[1] * wide))
    elif mode == "nt":
        (m, k), (n, _) = a.shape, (b.shape if cols is None else (b.shape[1], cols[1] * wide))
    else:
        (k, m), (_, n) = a.shape, b.shape
    tm, tn, tk = _matmul_tiles(m, n, k, jnp.dtype(out_dtype).itemsize)
    assert m % tm == 0 and n % tn == 0 and k % tk == 0, (name, m, n, k)
    nk = k // tk
    if mode == "nn":
        a_spec = pl.BlockSpec((tm, tk), lambda i, j, kk: (i, kk))
        b_spec = pl.BlockSpec((tk, tn), lambda i, j, kk: (kk, j))
        if cols is not None:
            assert tn == wide
            b_spec = pl.BlockSpec((None, tk, tn), lambda i, j, kk: (cols[0] + j, kk, 0))
        ca, cb = 1, 0
    elif mode == "nt":
        a_spec = pl.BlockSpec((tm, tk), lambda i, j, kk: (i, kk))
        b_spec = pl.BlockSpec((tn, tk), lambda i, j, kk: (j, kk))
        if cols is not None:
            assert tk == wide
            b_spec = pl.BlockSpec((None, tn, tk), lambda i, j, kk: (cols[0] + kk, j, 0))
        ca, cb = 1, 1
    else:
        a_spec = pl.BlockSpec((tk, tm), lambda i, j, kk: (kk, i))
        b_spec = pl.BlockSpec((tk, tn), lambda i, j, kk: (kk, j))
        ca, cb = 0, 0

    def body(a_ref, b_ref, o_ref, *acc):
        kk = pl.program_id(2)
        part = lax.dot_general(a_ref[...], b_ref[...], (((ca,), (cb,)), ((), ())), preferred_element_type=F32)
        if nk == 1:
            o_ref[...] = part.astype(o_ref.dtype)
            return
        acc_ref, = acc

        @pl.when(kk == 0)
        def _():
            acc_ref[...] = part

        @pl.when(kk > 0)
        def _():
            acc_ref[...] += part

        @pl.when(kk == nk - 1)
        def _():
            o_ref[...] = acc_ref[...].astype(o_ref.dtype)

    return pl.pallas_call(
        body, name=name, grid=(m // tm, n // tn, nk),
        in_specs=[a_spec, b_spec], out_specs=pl.BlockSpec((tm, tn), lambda i, j, kk: (i, j)),
        out_shape=jax.ShapeDtypeStruct((m, n), out_dtype),
        scratch_shapes=[pltpu.VMEM((tm, tn), F32)] if nk > 1 else [],
        compiler_params=_params("parallel", "parallel", "arbitrary"),
    )(a, b)


def _matmul_by_destination(a, b_lo, b_hi, *, name, tm=1024, tk=2048):
    (k, m), half = a.shape, N_DEV // 2
    assert b_lo.shape == b_hi.shape == (k, half * D_MODEL) and m % tm == 0 and k % tk == 0, name
    nk = k // tk

    def body(a_ref, lo_ref, hi_ref, o_ref, acc_ref):
        j, kk = pl.program_id(1), pl.program_id(2)

        def step(b_ref):
            part = lax.dot_general(a_ref[...], b_ref[...], (((0,), (0,)), ((), ())), preferred_element_type=F32)

            @pl.when(kk == 0)
            def _():
                acc_ref[...] = part

            @pl.when(kk > 0)
            def _():
                acc_ref[...] += part

        pl.when(j < half)(lambda: step(lo_ref))
        pl.when(j >= half)(lambda: step(hi_ref))

        @pl.when(kk == nk - 1)
        def _():
            o_ref[...] = acc_ref[...].astype(o_ref.dtype)

    lo_spec = pl.BlockSpec((tk, D_MODEL), lambda i, j, kk: (jnp.where(j < half, kk, nk - 1), jnp.minimum(j, half - 1)))
    hi_spec = pl.BlockSpec((tk, D_MODEL), lambda i, j, kk: (jnp.where(j >= half, kk, 0), jnp.maximum(j - half, 0)))
    return pl.pallas_call(
        body, name=name, grid=(m // tm, N_DEV, nk),
        in_specs=[pl.BlockSpec((tk, tm), lambda i, j, kk: (kk, i)), lo_spec, hi_spec],
        out_specs=pl.BlockSpec((None, tm, D_MODEL), lambda i, j, kk: (j, i, 0)),
        out_shape=jax.ShapeDtypeStruct((N_DEV, m, D_MODEL), BF16),
        scratch_shapes=[pltpu.VMEM((tm, D_MODEL), F32)],
        compiler_params=_params("parallel", "parallel", "arbitrary"),
    )(a, b_lo, b_hi)


LANES = 128


def _residues_to_rows(ref, scr, d):
    w = ref.shape[1] // d
    n = ref.shape[0]
    for r in range(d):
        for c in range(w // LANES):
            scr[c, pl.ds(r, n, stride=d), :] = ref[:, r * w + c * LANES:r * w + (c + 1) * LANES].astype(F32)
    return jnp.concatenate([scr[c] for c in range(w // LANES)], axis=1)


def _rows_to_residues(val, ref, scr, d):
    w = val.shape[1]
    n = ref.shape[0]
    for c in range(w // LANES):
        scr[c] = val[:, c * LANES:(c + 1) * LANES].astype(F32)
    for r in range(d):
        for c in range(w // LANES):
            ref[:, r * w + c * LANES:r * w + (c + 1) * LANES] = scr[c, pl.ds(r, n, stride=d), :].astype(ref.dtype)


HEAD_ROWS = 0


def _head_cols(z):
    width = z.shape[1]
    a = lax.broadcasted_iota(jnp.int32, (width, LANES), 0)
    b = lax.broadcasted_iota(jnp.int32, (width, LANES), 1)
    pick = jnp.where(a == b * HEAD_DIM, 1.0, 0.0).astype(BF16)
    out = None
    for part in _split3(z):
        t = lax.dot_general(part, pick, (((1,), (0,)), ((), ())), preferred_element_type=F32)
        out = t if out is None else out + t
    return out


def _head_rows_block(cols):
    return cols.T[:N_HEADS, :]


def _rowwise(fn, rows, consts, out_rows, out_acc, *, ts, name):
    rows = [tuple(r) + (1,) * (4 - len(r)) for r in rows]
    out_rows = [tuple(o) + (1,) * (3 - len(o)) for o in out_rows]
    s = rows[0][0].shape[0] * rows[0][3]
    assert s % ts == 0, (name, s, ts)
    n_rows, n_in = len(rows), len(rows) + len(consts)
    n_row = len(out_rows)
    n_out = n_row + len(out_acc)
    moved = [(idx, w) for idx, (_, w, _, d) in enumerate(rows) if d > 1]
    moved += [(n_rows + idx, w) for idx, (w, _, d) in enumerate(out_rows) if d > 1]

    def body(*refs):
        scratch = dict(zip([key for key, _ in moved], refs[n_in + n_out:]))
        vals = []
        for idx, r in enumerate(refs[:n_in]):
            d = rows[idx][3] if idx < n_rows else 1
            vals.append(r[...] if d == 1 else _residues_to_rows(r, scratch[idx], d))
        row_vals, acc_vals = fn(*vals)
        for idx, (r, v) in enumerate(zip(refs[n_in:n_in + n_row], row_vals)):
            d = out_rows[idx][2]
            if d == 1:
                r[...] = v.astype(r.dtype)
            elif d == HEAD_ROWS:
                r[...] = _head_rows_block(v)
            else:
                _rows_to_residues(v, r, scratch[n_rows + idx], d)
        first = pl.program_id(0) == 0
        for r, v in zip(refs[n_in + n_row:n_in + n_out], acc_vals):
            @pl.when(first)
            def _(r=r, v=v):
                r[...] = v

            @pl.when(jnp.logical_not(first))
            def _(r=r, v=v):
                r[...] += v

    in_specs = [pl.BlockSpec((ts // d, d * w), functools.partial(lambda i, cb: (i, cb), cb=cb)) for _, w, cb, d in rows]
    in_specs += [pl.BlockSpec(c.shape, lambda i: (0, 0)) for c in consts]
    out_specs = [pl.BlockSpec((N_HEADS, ts), lambda i: (0, i)) if d == HEAD_ROWS else
                 pl.BlockSpec((ts // d, d * w), lambda i: (i, 0)) for w, _, d in out_rows]
    out_specs += [pl.BlockSpec(sh, lambda i: (0, 0)) for sh in out_acc]
    out_shape = [jax.ShapeDtypeStruct((N_HEADS, s) if d == HEAD_ROWS else (s // d, d * w), dt) for w, dt, d in out_rows]
    out_shape += [jax.ShapeDtypeStruct(sh, F32) for sh in out_acc]
    outs = pl.pallas_call(
        body, name=name, grid=(s // ts,), in_specs=in_specs, out_specs=out_specs, out_shape=out_shape,
        scratch_shapes=[pltpu.VMEM((w // LANES, ts, LANES), F32) for _, w in moved],
        compiler_params=_params("arbitrary" if out_acc else "parallel"),
    )(*[a for a, _, _, _ in rows], *consts)
    return outs[:n_row], outs[n_row:]


def _full(a, d=1):
    return (a, a.shape[1] // d, 0, d)


def _gate_fn(zu, zv, vg, vb, ws0, ws1, ws2, ws3, bfull, ga):
    u = _gelu_erf(zu)
    vn = _layernorm(_gelu_erf(zv), vg, vb)
    p = lax.broadcasted_iota(jnp.int32, (CHUNK, CHUNK), 0)
    q = lax.broadcasted_iota(jnp.int32, (CHUNK, CHUNK), 1)
    tril = jnp.where(q <= p, 1.0, 0.0)
    group = lax.broadcasted_iota(jnp.int32, (1, A_WIDTH), 1) // CHUNK
    sg = bfull
    for g, w in enumerate((ws0, ws1, ws2, ws3)):
        sg = sg + _bdot(w * tril, jnp.where(group == g, vn, 0.0))
    return _rms(u * sg, ga)


GATE_TS = 2 * CHUNK


def _chunk_rows(rows):
    return [slice(c * CHUNK, (c + 1) * CHUNK) for c in range(rows // CHUNK)]


def _bias_reduce(dbf, name):
    def body(x_ref, o_ref):
        lane = lax.broadcasted_iota(jnp.int32, (CHUNK, CHUNK), 1)
        out = jnp.zeros((CHUNK, CHUNK), F32)
        for g in range(A_GROUPS):
            out = jnp.where(lane == g, jnp.sum(x_ref[:, g * CHUNK:(g + 1) * CHUNK], axis=1, keepdims=True), out)
        o_ref[...] = out

    return pl.pallas_call(body, name=name, out_shape=jax.ShapeDtypeStruct((CHUNK, CHUNK), F32))(dbf)


def _pair_mask(hh):
    lane = lax.broadcasted_iota(jnp.int32, (1, 2 * HEAD_DIM), 1)
    return (lane >= HEAD_DIM * hh) & (lane < HEAD_DIM * (hh + 1))


def _lane_pick(x2, lm):
    return jnp.max(jnp.where(lm, x2, -jnp.inf), axis=1, keepdims=True)


SCALE = HEAD_DIM ** -0.5


def _blocks_per_step(nb):
    return next(n for n in (4, 2, 1) if nb % n == 0)


def _units(nblk):
    return [(b, hp, hh) for b in range(nblk) for hp in range(N_HEADS // 2) for hh in range(2)]


def _attn_specs(nb, nblk):
    cur = pl.BlockSpec((nblk * BAND, B_WIDTH), lambda r, j: (j, r))
    prev = pl.BlockSpec((BAND, B_WIDTH), lambda r, j: (jnp.maximum(nblk * j - 1, 0), r))
    nxt = pl.BlockSpec((BAND, B_WIDTH), lambda r, j: (jnp.minimum(nblk * (j + 1), nb - 1), r))
    return cur, prev, nxt


def _pair_cols(hp):
    return slice(2 * HEAD_DIM * hp, 2 * HEAD_DIM * (hp + 1))


def _rows(b):
    return slice(b * BAND, (b + 1) * BAND)


def _with_prev(cur_ref, prev_ref, b, sl):
    if b == 0:
        return jnp.concatenate([prev_ref[:, sl], cur_ref[_rows(0), sl]], axis=0)
    return cur_ref[(b - 1) * BAND:(b + 1) * BAND, sl]


def _with_next(cur_ref, next_ref, b, sl, nblk):
    if b == nblk - 1:
        return jnp.concatenate([cur_ref[_rows(b), sl], next_ref[:, sl]], axis=0)
    return cur_ref[b * BAND:(b + 2) * BAND, sl]


def _band_valid(other_exists):
    row = lax.broadcasted_iota(jnp.int32, (BAND, 2 * BAND), 0)
    col = lax.broadcasted_iota(jnp.int32, (BAND, 2 * BAND), 1)
    return (col >= row) & (col <= row + BAND), other_exists


def _masked(lm, x):
    return jnp.where(lm, x, jnp.zeros_like(x))


def _attn_fwd(q, k, v, d, name):
    nb = q.shape[0] // BAND
    nblk = _blocks_per_step(nb)
    units = _units(nblk)
    cur, prev, _ = _attn_specs(nb, nblk)

    def body(q_ref, kc_ref, kp_ref, vc_ref, vp_ref, o_ref, l_ref):
        band, has_prev = _band_valid(pl.program_id(1) > 0)
        col = lax.broadcasted_iota(jnp.int32, (BAND, 2 * BAND), 1)
        valid = [band & ((col >= BAND) | has_prev)] + [band] * (nblk - 1)

        def scores(unit):
            b, hp, hh = unit
            sl = _pair_cols(hp)
            return _dotg(_masked(_pair_mask(hh), q_ref[_rows(b), sl]), _with_prev(kc_ref, kp_ref, b, sl), 1, 1)

        ahead, half = scores(units[0]), None
        for idx, (b, hp, hh) in enumerate(units):
            raw = ahead
            if idx + 1 < len(units):
                ahead = scores(units[idx + 1])
            sl, lm = _pair_cols(hp), _pair_mask(hh)
            s = jnp.where(valid[b], raw * SCALE, NEG_INF)
            m = jnp.max(s, axis=1, keepdims=True)
            p = jnp.exp(s - m)
            den = jnp.sum(p, axis=1, keepdims=True)
            o = _dotg(p, _with_prev(vc_ref, vp_ref, b, sl), 1, 0) / den
            lse = m + jnp.log(den)
            if hh == 0:
                half = (o, lse)
            else:
                o_ref[_rows(b), sl] = jnp.where(lm, o, half[0])
                l_ref[_rows(b), sl] = jnp.where(lm, lse, half[1])

    return pl.pallas_call(
        body, name=name, grid=(d, nb // nblk), in_specs=[cur, cur, prev, cur, prev], out_specs=[cur, cur],
        out_shape=[jax.ShapeDtypeStruct(q.shape, F32), jax.ShapeDtypeStruct(q.shape, F32)],
        compiler_params=_params("parallel", "parallel"),
    )(q, k, k, v, v)


def _attn_bwd_q(q, k, v, do, lse, delta, d, name):
    nb = q.shape[0] // BAND
    nblk = _blocks_per_step(nb)
    units = _units(nblk)
    cur, prev, _ = _attn_specs(nb, nblk)
    per_head = pl.BlockSpec((nblk * BAND, LANES), lambda r, j: (j, r))

    def body(q_ref, kc_ref, kp_ref, vc_ref, vp_ref, do_ref, l_ref, dl_ref, dq_ref):
        band, has_prev = _band_valid(pl.program_id(1) > 0)
        col = lax.broadcasted_iota(jnp.int32, (BAND, 2 * BAND), 1)
        valid = [band & ((col >= BAND) | has_prev)] + [band] * (nblk - 1)

        def products(unit):
            b, hp, hh = unit
            sl, lm = _pair_cols(hp), _pair_mask(hh)
            return (_dotg(_masked(lm, q_ref[_rows(b), sl]), _with_prev(kc_ref, kp_ref, b, sl), 1, 1),
                    _dotg(_masked(lm, do_ref[_rows(b), sl]), _with_prev(vc_ref, vp_ref, b, sl), 1, 1))

        ahead, half = products(units[0]), None
        for idx, (b, hp, hh) in enumerate(units):
            raw, dp = ahead
            if idx + 1 < len(units):
                ahead = products(units[idx + 1])
            sl, lm = _pair_cols(hp), _pair_mask(hh)
            s = jnp.where(valid[b], raw * SCALE, NEG_INF)
            head = lax.broadcasted_iota(jnp.int32, (1, LANES), 1) == 2 * hp + hh
            p = jnp.exp(s - _lane_pick(l_ref[_rows(b), :], head))
            ds = p * (dp - _lane_pick(dl_ref[_rows(b), :], head))
            dq = _dotg(ds, _with_prev(kc_ref, kp_ref, b, sl), 1, 0) * SCALE
            if hh == 0:
                half = dq
            else:
                dq_ref[_rows(b), sl] = jnp.where(lm, dq, half)

    return pl.pallas_call(
        body, name=name, grid=(d, nb // nblk),
        in_specs=[cur, cur, prev, cur, prev, cur, per_head, per_head], out_specs=cur,
        out_shape=jax.ShapeDtypeStruct(q.shape, F32),
        compiler_params=_params("parallel", "parallel"),
    )(q, k, k, v, v, do, lse, delta)


def _attn_bwd_kv(q, k, v, do, lse_t, delta_t, d, name):
    nb = q.shape[0] // BAND
    nblk = _blocks_per_step(nb)
    units = _units(nblk)
    cur, _, nxt = _attn_specs(nb, nblk)
    t_cur = pl.BlockSpec((1, N_HEADS, nblk * BAND), lambda r, j: (r, 0, j))
    t_nxt = pl.BlockSpec((1, N_HEADS, BAND), lambda r, j: (r, 0, jnp.minimum(nblk * (j + 1), nb - 1)))

    def body(k_ref, v_ref, qc_ref, qn_ref, doc_ref, don_ref, lc_ref, ln_ref, dlc_ref, dln_ref, dk_ref, dv_ref):
        band, has_next = _band_valid(pl.program_id(1) < nb // nblk - 1)
        col = lax.broadcasted_iota(jnp.int32, (BAND, 2 * BAND), 1)
        valid = [band] * (nblk - 1) + [band & ((col < BAND) | has_next)]

        def head_row(c_ref, n_ref, b, h):
            if b == nblk - 1:
                return jnp.concatenate([c_ref[0, h:h + 1, b * BAND:(b + 1) * BAND], n_ref[0, h:h + 1, :]], axis=1)
            return c_ref[0, h:h + 1, b * BAND:(b + 2) * BAND]

        def products(unit):
            b, hp, hh = unit
            sl, lm = _pair_cols(hp), _pair_mask(hh)
            return (_dotg(_masked(lm, k_ref[_rows(b), sl]), _with_next(qc_ref, qn_ref, b, sl, nblk), 1, 1),
                    _dotg(_masked(lm, v_ref[_rows(b), sl]), _with_next(doc_ref, don_ref, b, sl, nblk), 1, 1))

        ahead, half = products(units[0]), None
        for idx, (b, hp, hh) in enumerate(units):
            raw, dpt = ahead
            if idx + 1 < len(units):
                ahead = products(units[idx + 1])
            sl, lm, h = _pair_cols(hp), _pair_mask(hh), 2 * hp + hh
            st = jnp.where(valid[b], raw * SCALE, NEG_INF)
            pt = jnp.exp(st - head_row(lc_ref, ln_ref, b, h))
            dv = _dotg(pt, _with_next(doc_ref, don_ref, b, sl, nblk), 1, 0)
            dst = pt * (dpt - head_row(dlc_ref, dln_ref, b, h))
            dk = _dotg(dst, _with_next(qc_ref, qn_ref, b, sl, nblk), 1, 0) * SCALE
            if hh == 0:
                half = (dk, dv)
            else:
                dk_ref[_rows(b), sl] = jnp.where(lm, dk, half[0])
                dv_ref[_rows(b), sl] = jnp.where(lm, dv, half[1])

    return pl.pallas_call(
        body, name=name, grid=(d, nb // nblk),
        in_specs=[cur, cur, cur, nxt, cur, nxt, t_cur, t_nxt, t_cur, t_nxt], out_specs=[cur, cur],
        out_shape=[jax.ShapeDtypeStruct(q.shape, F32), jax.ShapeDtypeStruct(q.shape, F32)],
        compiler_params=_params("parallel", "parallel"),
    )(k, v, q, q, do, do, lse_t, lse_t, delta_t, delta_t)


def _spread_rows(a, d):
    return a.reshape(N_HEADS, a.shape[1] // d, d).transpose(2, 0, 1)


FF_TS = 512
FF_TC = 1024
FF_SUB = 256
HALO = 8
HALO_BF16 = 16
UP_BLOCKS = D_MODEL // FF_TC


def _conv3(ext, w, b):
    return b + w[0:1, :] * pltpu.roll(ext, 2, 0) + w[1:2, :] * pltpu.roll(ext, 1, 0) + w[2:3, :] * ext


def _ffn_specs(s, cols_first):
    nrb = s // FF_TS
    per, per16 = FF_TS // HALO, FF_TS // HALO_BF16

    def mk(block, fn):
        if cols_first:
            return pl.BlockSpec(block, lambda j, i: fn(i, j))
        return pl.BlockSpec(block, lambda i, j: fn(i, j))

    specs = types.SimpleNamespace(
        nrb=nrb, ncb=D_FF // FF_TC,
        row=mk((FF_TS, FF_TC), lambda i, j: (i, j)),
        before=mk((HALO, FF_TC), lambda i, j: (jnp.maximum(i * per - 1, 0), j)),
        after=mk((HALO, FF_TC), lambda i, j: (jnp.minimum((i + 1) * per, nrb * per - 1), j)),
        w=mk((3, FF_TC), lambda i, j: (0, j)),
        b=mk((1, FF_TC), lambda i, j: (0, j)),
        part=mk((HALO, FF_TC), lambda i, j: (i, j)),
        act=mk((FF_TS, D_MODEL), lambda i, j: (i, 0)),
        act_before=mk((HALO_BF16, D_MODEL), lambda i, j: (jnp.maximum(i * per16 - 1, 0), 0)),
        act_after=mk((HALO_BF16, D_MODEL), lambda i, j: (jnp.minimum((i + 1) * per16, nrb * per16 - 1), 0)),
        up_gate=mk((None, D_MODEL, FF_TC), lambda i, j: (j // UP_BLOCKS, 0, j % UP_BLOCKS)),
        up_val=mk((None, D_MODEL, FF_TC), lambda i, j: (N_DEV // 2 + j // UP_BLOCKS, 0, j % UP_BLOCKS)),
        down=mk((FF_TC, D_MODEL), lambda i, j: (j, 0)),
    )
    return specs


def _ffn_up_geglu(h, w_up, wg, wv, bg, bv, name):
    s = h.shape[0]
    sp = _ffn_specs(s, True)

    def body(h_ref, hb_ref, ugw_ref, uvw_ref, wg_ref, wv_ref, bg_ref, bv_ref, ug_ref, uv_ref, y_ref):
        keep = jnp.where(pl.program_id(1) > 0, 1.0, 0.0).astype(BF16)
        hext = jnp.concatenate([hb_ref[...] * keep, h_ref[...]], axis=0)
        n_sub = FF_TC // FF_SUB
        cols = [slice(c * FF_SUB, (c + 1) * FF_SUB) for c in range(n_sub)]
        up_gate = lambda c: _dotg(hext, ugw_ref[:, cols[c]], 1, 0)
        up_val = lambda c: _dotg(hext, uvw_ref[:, cols[c]], 1, 0)
        eg, ev = up_gate(0), up_val(0)
        for c in range(n_sub):
            sl, more = cols[c], c + 1 < n_sub
            eg_next = up_gate(c + 1) if more else None
            ug_ref[:, sl] = eg[HALO_BF16:, :]
            act = _gelu_tanh(_conv3(eg, wg_ref[:, sl], bg_ref[:, sl])[HALO_BF16:, :])
            ev_next = up_val(c + 1) if more else None
            uv_ref[:, sl] = ev[HALO_BF16:, :]
            y_ref[:, sl] = (act * _conv3(ev, wv_ref[:, sl], bv_ref[:, sl])[HALO_BF16:, :]).astype(y_ref.dtype)
            eg, ev = eg_next, ev_next

    return pl.pallas_call(
        body, name=name, grid=(sp.ncb, sp.nrb),
        in_specs=[sp.act, sp.act_before, sp.up_gate, sp.up_val, sp.w, sp.w, sp.b, sp.b],
        out_specs=[sp.row, sp.row, sp.row],
        out_shape=[jax.ShapeDtypeStruct((s, D_FF), F32), jax.ShapeDtypeStruct((s, D_FF), F32),
                   jax.ShapeDtypeStruct((s, D_FF), BF16)],
        compiler_params=_params("parallel", "parallel"),
    )(h, h, w_up, w_up, wg, wv, bg, bv)


def _sum_parts(parts, name):
    n = parts.shape[0] // HALO

    def body(p_ref, o_ref):
        acc = p_ref[0:HALO, :]
        for t in range(1, n):
            acc = acc + p_ref[t * HALO:(t + 1) * HALO, :]
        o_ref[...] = acc

    return pl.pallas_call(body, name=name, out_shape=jax.ShapeDtypeStruct((HALO, parts.shape[1]), F32))(parts)


def _ffn_geglu_bwd(ug, uv, df, w_down, w_up, wg, wv, bg, bv, name):
    s = ug.shape[0]
    sp = _ffn_specs(s, False)
    nrb = sp.nrb
    rows = FF_TS + 2 * HALO
    lo, hi = HALO, HALO + FF_TS

    def body(ug_ref, uv_ref, hg_ref, hv_ref, ng_ref, nv_ref, df_ref, dfn_ref, dw_ref, ugw_ref, uvw_ref,
             wg_ref, wv_ref, bg_ref, bv_ref, dug_ref, duv_ref, dh_ref, dwg_ref, dwv_ref):
        i, j = pl.program_id(0), pl.program_id(1)
        keep_top = jnp.where(i > 0, 1.0, 0.0)
        keep_bot = jnp.where(i < nrb - 1, 1.0, 0.0).astype(BF16)
        dfe = jnp.concatenate([df_ref[...], dfn_ref[...] * keep_bot], axis=0)

        def back(dc, e, w, du_ref, sl):
            up1 = pltpu.roll(dc, rows - 1, 0)
            up2 = pltpu.roll(dc, rows - 2, 0)
            du = (w[2:3, :] * dc + w[1:2, :] * up1 + w[0:1, :] * up2)[lo:hi, :].astype(BF16)
            du_ref[:, sl] = du
            p1, p2 = up1 * e, up2 * e
            colsum = lambda p: jnp.sum(p[lo:hi, :], axis=0, keepdims=True)
            row = lambda p, t: p[t:t + 1, :]
            d_w1 = colsum(p1) + row(p1, lo - 1) - row(p1, hi - 1)
            d_w0 = colsum(p2) + row(p2, lo - 2) + row(p2, lo - 1) - row(p2, hi - 2) - row(p2, hi - 1)
            sums = [d_w0, d_w1, colsum(dc * e), colsum(dc), jnp.zeros((HALO - 4, FF_SUB), F32)]
            return du, jnp.concatenate(sums, axis=0)

        dh = None
        n_sub = FF_TC // FF_SUB
        cols = [slice(c * FF_SUB, (c + 1) * FF_SUB) for c in range(n_sub)]
        d_act = lambda c: _dotg(dfe, dw_ref[cols[c], :], 1, 1)[:FF_TS + HALO, :]
        ahead = d_act(0)
        for c in range(n_sub):
            sl, dy = cols[c], ahead
            if c + 1 < n_sub:
                ahead = d_act(c + 1)
            dye = jnp.concatenate([jnp.zeros((HALO, FF_SUB), F32), dy], axis=0)
            eg = jnp.concatenate([hg_ref[:, sl] * keep_top, ug_ref[:, sl], ng_ref[:, sl]], axis=0)
            ev = jnp.concatenate([hv_ref[:, sl] * keep_top, uv_ref[:, sl], nv_ref[:, sl]], axis=0)
            wg_, wv_ = wg_ref[:, sl], wv_ref[:, sl]
            gate = _conv3(eg, wg_, bg_ref[:, sl])
            val = _conv3(ev, wv_, bv_ref[:, sl])
            act, slope = _gelu_tanh_and_slope(gate)
            dug, dwg_ref[:, sl] = back((dye * val) * slope, eg, wg_, dug_ref, sl)
            duv, dwv_ref[:, sl] = back(dye * act, ev, wv_, duv_ref, sl)
            term = _dotg(dug, ugw_ref[:, sl], 1, 1) + _dotg(duv, uvw_ref[:, sl], 1, 1)
            dh = term if dh is None else dh + term

        @pl.when(j == 0)
        def _():
            dh_ref[...] = dh

        @pl.when(j > 0)
        def _():
            dh_ref[...] += dh

    parts = jax.ShapeDtypeStruct((nrb * HALO, D_FF), F32)
    dug, duv, dh, pg, pv = pl.pallas_call(
        body, name=name, grid=(nrb, sp.ncb),
        in_specs=[sp.row, sp.row, sp.before, sp.before, sp.after, sp.after, sp.act, sp.act_after, sp.down,
                  sp.up_gate, sp.up_val, sp.w, sp.w, sp.b, sp.b],
        out_specs=[sp.row, sp.row, sp.act, sp.part, sp.part],
        out_shape=[jax.ShapeDtypeStruct((s, D_FF), BF16), jax.ShapeDtypeStruct((s, D_FF), BF16),
                   jax.ShapeDtypeStruct((s, D_MODEL), F32), parts, parts],
        compiler_params=_params("parallel", "arbitrary"),
    )(ug, uv, ug, uv, ug, uv, df, df, w_down, w_up, w_up, wg, wv, bg, bv)
    return dug, duv, dh, _sum_parts(pg, name=name + "_sum_gate"), _sum_parts(pv, name=name + "_sum_val")


def _rope_tables(s):
    inv = ROPE_THETA ** (-jnp.arange(0, ROT_DIM, 2, dtype=F32) / ROT_DIM)
    ang = jnp.arange(s, dtype=F32)[:, None] * inv[None, :]
    cos8, sin8 = jnp.cos(ang), jnp.sin(ang)
    rest = HEAD_DIM - ROT_DIM
    cos_h = jnp.concatenate([cos8, cos8, jnp.ones((s, rest), F32)], axis=1)
    sin_h = jnp.concatenate([sin8, sin8, jnp.zeros((s, rest), F32)], axis=1)
    return jnp.tile(cos_h, (1, LANES // HEAD_DIM)), jnp.tile(sin_h, (1, LANES // HEAD_DIM))


def _all_heads(table):
    return jnp.concatenate([table] * (B_WIDTH // LANES), axis=1)


def _layer_fwd(x, w, cos, sin, last):
    sv = types.SimpleNamespace(x=x)
    (sv.h1,), _ = _rowwise(lambda xb, g: ((_rms(xb, g),), ()), [_full(x)], [w.g_pre], [(D_MODEL, BF16)], [],
                           ts=512, name="pre_mix_norm")
    sv.proj = _matmul(sv.h1, w.big("w_in", sv.h1), mode="nt", out_dtype=F32, name="proj")

    gate_consts = [w.vg, w.vb, *w.ws, w.bfull, w.ga]
    def gate_fwd_fn(zu, zv, *consts):
        chunks = [_gate_fn(zu[r], zv[r], *consts) for r in _chunk_rows(zu.shape[0])]
        return (jnp.concatenate(chunks, axis=0),), ()

    (na,), _ = _rowwise(gate_fwd_fn, [(sv.proj, A_WIDTH, 0), (sv.proj, A_WIDTH, 1)], gate_consts,
                        [(A_WIDTH, BF16)], [], ts=GATE_TS, name="gate_fwd")

    def rope_fn(qr, kr, vr, cs, sn):
        cs, sn = _all_heads(cs), _all_heads(sn)
        return (qr * cs + _rot_half(qr) * sn, kr * cs + _rot_half(kr) * sn, vr), ()

    def rope_all(qr, kr, vr, cs, sn):
        return rope_fn(qr, kr, vr, cs, sn)[0] * len(DILATIONS), ()

    qkv, _ = _rowwise(
        rope_all, [(sv.proj, B_WIDTH, 2), (sv.proj, B_WIDTH, 3), (sv.proj, B_WIDTH, 4), _full(cos), _full(sin)], [],
        [(B_WIDTH, BF16, d) for d in DILATIONS for _ in range(3)], [], ts=512, name="rope_fwd")
    sv.qkv = {d: qkv[3 * i:3 * i + 3] for i, d in enumerate(DILATIONS)}

    branch = []
    for d in DILATIONS:
        o, l = _attn_fwd(*sv.qkv[d], d, name=f"attn_fwd_d{d}")
        branch += [_full(o, d), _full(l, d)]

    def combine_fn(o1, l1, o2, l2, o3, l3, nab, gb):
        m = jnp.maximum(jnp.maximum(l1, l2), l3)
        e1, e2, e3 = jnp.exp(l1 - m), jnp.exp(l2 - m), jnp.exp(l3 - m)
        den = e1 + e2 + e3
        ob = (e1 / den) * o1 + (e2 / den) * o2 + (e3 / den) * o3
        mixed = jnp.concatenate([nab, _rms(ob, gb).astype(BF16)], axis=1)
        lse = _head_cols(m + jnp.log(den))
        return (mixed, ob, lse) + (lse,) * len(DILATIONS), ()

    (sv.mixed, sv.ob, sv.lse_rows, *lses), _ = _rowwise(
        combine_fn, branch + [_full(na)], [w.gb],
        [(D_MODEL, BF16), (B_WIDTH, F32), (LANES, F32, HEAD_ROWS)] + [(LANES, F32, d) for d in DILATIONS], [],
        ts=512, name="combine")
    sv.lse = dict(zip(DILATIONS, lses))
    sv.y = _matmul(sv.mixed, w.big("w_out", sv.mixed), mode="nn", out_dtype=F32, name="mix_out")

    def mid_fn(xb, yb, g1, g2):
        x1 = xb + _rms(yb, g1)
        return (x1, _rms(x1, g2)), ()

    (sv.x1, sv.h2), _ = _rowwise(mid_fn, [_full(x), _full(sv.y)], [w.g_pm, w.g_pf], [(D_MODEL, F32), (D_MODEL, BF16)], [],
                                 ts=512, name="post_mix_norm")
    conv_w = w.big("conv_w", sv.h2)
    sv.ug, sv.uv, sv.yff = _ffn_up_geglu(sv.h2, w.big("w_up", sv.h2), conv_w[:, :D_FF], conv_w[:, D_FF:],
                                         w.cb_g, w.cb_v, name="ffn_up_geglu")
    sv.f = _matmul(sv.yff, w.big("w_down", sv.yff), mode="nn", out_dtype=F32, name="ffn_down")
    if last:
        return None, sv
    (x2,), _ = _rowwise(lambda xb, fb, g: ((xb + _rms(fb, g),), ()), [_full(sv.x1), _full(sv.f)], [w.g_post],
                        [(D_MODEL, F32)], [], ts=512, name="post_ffn_norm")
    return x2, sv


def _layer_bwd(dx2, sv, w, cos, sin, emit):
    g = {}

    def post_fn(fb, dxb, gp):
        _, vjp = jax.vjp(_rms, fb, gp)
        df, dg = vjp(dxb)
        return (df,), (dg,)

    (df,), (g["post_ffn_norm"],) = _rowwise(post_fn, [_full(sv.f), _full(dx2)], [w.g_post], [(D_MODEL, BF16)],
                                            [(1, D_MODEL)], ts=512, name="post_ffn_norm_bwd")
    big = {"w_down": _matmul(sv.yff, df, mode="tn", out_dtype=BF16, name="ffn_down_dw").reshape(N_DEV, -1, D_MODEL)}
    conv_w = w.big("conv_w", df)
    dug, duv, dh2, dwg, dwv = _ffn_geglu_bwd(sv.ug, sv.uv, df, w.big("w_down", df), w.big("w_up", df),
                                             conv_w[:, :D_FF], conv_w[:, D_FF:], w.cb_g, w.cb_v, name="ffn_geglu_bwd")
    big["conv_w"] = jnp.concatenate([dwg[0:3], dwv[0:3]], axis=1).reshape(3, N_DEV, D_MODEL).transpose(1, 0, 2)
    g["conv_b"] = jnp.concatenate([dwg[3], dwv[3]], axis=0)
    big["w_up"] = _matmul_by_destination(sv.h2, dug, duv, name="ffn_up_dw")
    g_pm = w.g_pm + emit(big)

    def mid_fn(x1b, yb, dhb, dxb, g1, g2):
        _, vjp2 = jax.vjp(_rms, x1b, g2)
        dx1h, dg2 = vjp2(dhb)
        dx1 = dxb + dx1h
        _, vjp1 = jax.vjp(_rms, yb, g1)
        dy, dg1 = vjp1(dx1)
        return (dx1, dy), (dg1, dg2)

    (dx1, dy), (g["post_mix_norm"], g["pre_ffn_norm"]) = _rowwise(
        mid_fn, [_full(sv.x1), _full(sv.y), _full(dh2), _full(dx2)], [g_pm, w.g_pf],
        [(D_MODEL, F32), (D_MODEL, BF16)], [(1, D_MODEL), (1, D_MODEL)], ts=256, name="post_mix_norm_bwd")
    dmixed = _matmul(dy, w.big("w_out", dy), mode="nt", out_dtype=F32, name="mix_out_dx")
    dw_out = _matmul(sv.mixed, dy, mode="tn", out_dtype=BF16, name="mix_out_dw").reshape(N_DEV, -1, D_MODEL)
    g_b = w.gb + emit({"w_out": dw_out})

    def attn_out_fn(obb, dmb, gb):
        _, vjp = jax.vjp(_rms, obb, gb)
        do, dgb = vjp(dmb)
        delta = _head_cols(_head_sum(do * obb))
        return (delta,) + (do,) * len(DILATIONS) + (delta,) * len(DILATIONS), (dgb,)

    (delta_rows, *outs), (g["out_norm_b"],) = _rowwise(
        attn_out_fn, [_full(sv.ob), (dmixed, B_WIDTH, 1)], [g_b],
        [(LANES, F32, HEAD_ROWS)] + [(B_WIDTH, BF16, d) for d in DILATIONS] + [(LANES, F32, d) for d in DILATIONS],
        [(1, B_WIDTH)], ts=512, name="attn_out_bwd")
    do = dict(zip(DILATIONS, outs[:len(DILATIONS)]))
    delta = dict(zip(DILATIONS, outs[len(DILATIONS):]))
    parts = {"q": [], "k": [], "v": []}
    for d in DILATIONS:
        qv, kv, vv = sv.qkv[d]
        dq = _attn_bwd_q(qv, kv, vv, do[d], sv.lse[d], delta[d], d, name=f"attn_bwd_q_d{d}")
        dk, dv = _attn_bwd_kv(qv, kv, vv, do[d], _spread_rows(sv.lse_rows, d), _spread_rows(delta_rows, d), d,
                              name=f"attn_bwd_kv_d{d}")
        parts["q"].append(_full(dq, d))
        parts["k"].append(_full(dk, d))
        parts["v"].append(_full(dv, d))

    def rope_bwd_fn(q1, q2, q3, k1, k2, k3, v1, v2, v3, cs, sn):
        cs, sn = _all_heads(cs), _all_heads(sn)

        def back(t):
            return t * cs - _rot_half(t * sn)
        return (jnp.concatenate([back(q1 + q2 + q3), back(k1 + k2 + k3), v1 + v2 + v3], axis=1),), ()

    (dzb,), _ = _rowwise(rope_bwd_fn, parts["q"] + parts["k"] + parts["v"] + [_full(cos), _full(sin)], [],
                         [(3 * B_WIDTH, BF16)], [], ts=256, name="rope_bwd")

    gate_consts = [w.vg, w.vb, *w.ws, w.bfull, w.ga]

    def gate_bwd_fn(zu, zv, dna, *consts):
        dz, sums = [], None
        for r in _chunk_rows(zu.shape[0]):
            _, vjp = jax.vjp(_gate_fn, zu[r], zv[r], *consts)
            grads = vjp(dna[r])
            dz.append(jnp.concatenate([grads[0], grads[1]], axis=1))
            sums = grads[2:] if sums is None else tuple(a + b for a, b in zip(sums, grads[2:]))
        return (jnp.concatenate(dz, axis=0),), tuple(sums)

    (dza,), gsmall = _rowwise(
        gate_bwd_fn, [(sv.proj, A_WIDTH, 0), (sv.proj, A_WIDTH, 1), (dmixed, A_WIDTH, 0)], gate_consts,
        [(2 * A_WIDTH, BF16)], [c.shape for c in gate_consts], ts=GATE_TS, name="gate_bwd")
    g["v_norm_g"], g["v_norm_b"] = gsmall[0], gsmall[1]
    g["w_spatial"] = jnp.stack(gsmall[2:6])
    g["b_spatial"] = _bias_reduce(gsmall[6], name="bias_reduce")[:, :A_GROUPS].T
    g["out_norm_a"] = gsmall[7]

    dproj = jnp.concatenate([dza, dzb], axis=1)
    dh1 = _matmul(dproj, w.big("w_in", dproj), mode="nn", out_dtype=F32, name="proj_dx")
    dw_in = _matmul(dproj, sv.h1, mode="tn", out_dtype=BF16, name="proj_dw").reshape(N_DEV, -1, D_MODEL)
    g_pre = w.g_pre + emit({"w_in": dw_in})

    def pre_fn(xb, dhb, dxb, gp):
        _, vjp = jax.vjp(_rms, xb, gp)
        dxh, dg = vjp(dhb)
        return (dxb + dxh,), (dg,)

    (dx,), (g["pre_mix_norm"],) = _rowwise(pre_fn, [_full(sv.x), _full(dh1), _full(dx1)], [g_pre], [(D_MODEL, F32)],
                                           [(1, D_MODEL)], ts=512, name="pre_mix_norm_bwd")
    return dx, g


def _layer_weights(l, full, small):
    row = lambda a: a[l].reshape(1, -1)
    return types.SimpleNamespace(
        big=functools.partial(full, l),
        g_pre=row(small["pre_mix_norm"]), vg=row(small["v_norm_g"]), vb=row(small["v_norm_b"]),
        ws=[small["w_spatial"][l, gi] for gi in range(A_GROUPS)],
        bfull=jnp.repeat(small["b_spatial"][l].T, CHUNK, axis=1),
        ga=row(small["out_norm_a"]), gb=row(small["out_norm_b"]),
        g_pm=row(small["post_mix_norm"]), g_pf=row(small["pre_ffn_norm"]),
        cb_g=small["conv_b"][l][:D_FF].reshape(1, -1), cb_v=small["conv_b"][l][D_FF:].reshape(1, -1),
        g_post=row(small["post_ffn_norm"]))


def _local_step(x, target, full, small, emit, emit_small, started):
    s = x.shape[0]
    cos, sin = _rope_tables(s)
    ws = [_layer_weights(l, full, small) for l in range(N_LAYERS)]
    ws[0].g_pre = ws[0].g_pre + started
    saved = []
    h = x
    for l in range(N_LAYERS):
        h, sv = _layer_fwd(h, ws[l], cos, sin, last=l == N_LAYERS - 1)
        saved.append(sv)

    def loss_fn(xb, fb, tb, g):
        diff = (xb + _rms(fb, g)) - tb
        return (diff * (1.0 / D_MODEL),), (jnp.sum(diff * diff, axis=0, keepdims=True),)

    (dh,), (sq,) = _rowwise(loss_fn, [_full(saved[-1].x1), _full(saved[-1].f), _full(target)], [ws[-1].g_post],
                            [(D_MODEL, F32)], [(1, D_MODEL)], ts=512, name="loss")
    loss = 0.5 * jnp.sum(sq) * (1.0 / D_MODEL)
    grads = [None] * N_LAYERS
    for l in reversed(range(N_LAYERS)):
        dh, grads[l] = _layer_bwd(dh, saved[l], ws[l], cos, sin, functools.partial(emit, l))
        token = emit_small(l, grads[l])
        if l > 0:
            ws[l - 1].g_post = ws[l - 1].g_post + token
    return loss, dh, grads


def _place():
    return lax.axis_index("x"), lax.axis_index("y"), lax.axis_index("c")


FLIPS = ((1, 0, 0), (0, 1, 0), (1, 1, 0), (0, 0, 1), (1, 0, 1), (0, 1, 1), (1, 1, 1))
HBM_SPEC = pl.BlockSpec(memory_space=pltpu.HBM)
SEM_SPEC = pl.BlockSpec(memory_space=pltpu.SEMAPHORE)
SPLIT_COPY = pltpu.CompilerParams(has_side_effects=pltpu.SideEffectType.DATAFLOW_SIDE_EFFECTING)


def _peers():
    mx, my, mc = _place()
    out = []
    for fx, fy, fc in FLIPS:
        px, py, pc = (1 - mx if fx else mx), (1 - my if fy else my), (1 - mc if fc else mc)
        out.append(((px, py, pc), 4 * px + 2 * py + pc))
    return out


def _flat_copies(scatter, src_refs, land_refs, send_sems, recv_sems):
    mx, my, mc = _place()
    me = 4 * mx + 2 * my + mc
    n = len(src_refs)
    copies = []
    for t in range(n):
        for i, (peer, number) in enumerate(_peers()):
            copies.append(pltpu.make_async_remote_copy(
                src_ref=src_refs[t].at[number] if scatter else src_refs[t],
                dst_ref=land_refs[t].at[i] if scatter else land_refs[t].at[me],
                send_sem=send_sems.at[t * len(FLIPS) + i], recv_sem=recv_sems.at[t * len(FLIPS) + i],
                device_id=peer, device_id_type=MESH_ID))
    return copies


def _flat_start(arrays, scatter, name):
    n = len(arrays)
    slots = len(FLIPS) if scatter else N_DEV
    lands = [lax.empty((slots,) + (a.shape[1:] if scatter else a.shape), a.dtype) for a in arrays]

    def body(*refs):
        src, land, (send_sems, recv_sems), token = refs[:n], refs[n:2 * n], refs[2 * n:2 * n + 2], refs[-1]
        for cp in _flat_copies(scatter, src, land, send_sems, recv_sems):
            cp.start()
        token[...] = jnp.zeros_like(token)

    hbm = [pltpu.HBM(a.shape, a.dtype) for a in arrays] + [pltpu.HBM(a.shape, a.dtype) for a in lands]
    sems = pltpu.SemaphoreType.DMA((n * len(FLIPS),))
    outs = pl.pallas_call(
        body, name=name, out_shape=(sems, sems, *hbm, jax.ShapeDtypeStruct((8, 128), F32)),
        in_specs=[HBM_SPEC] * (2 * n),
        out_specs=(SEM_SPEC, SEM_SPEC, *([HBM_SPEC] * (2 * n)), pl.BlockSpec(memory_space=pltpu.VMEM)),
        input_output_aliases={t: 2 + t for t in range(2 * n)}, compiler_params=SPLIT_COPY,
    )(*[pltpu.with_memory_space_constraint(a, pltpu.HBM) for a in (*arrays, *lands)])
    return types.SimpleNamespace(sems=outs[:2], thru=outs[2:2 + 2 * n], scatter=scatter, n=n), outs[-1][0:1, 0:1]


def _flat_wait(handle, after, name):
    n = handle.n

    def body(*refs):
        src, land, (send_sems, recv_sems) = refs[:n], refs[n:2 * n], refs[2 * n:2 * n + 2]
        for cp in _flat_copies(handle.scatter, src, land, send_sems, recv_sems):
            cp.wait_send()
            cp.wait_recv()

    outs = pl.pallas_call(
        body, name=name, out_shape=tuple(pltpu.HBM(a.shape, a.dtype) for a in handle.thru),
        in_specs=[HBM_SPEC] * (2 * n) + [SEM_SPEC, SEM_SPEC, ANY], out_specs=tuple([HBM_SPEC] * (2 * n)),
        input_output_aliases={t: t for t in range(2 * n)}, compiler_params=SPLIT_COPY,
    )(*handle.thru, *handle.sems, after)
    return outs[:n], outs[n:]


def _adamw(w, g, m, v):
    m2 = ADAM_B1 * m + (1.0 - ADAM_B1) * g
    v2 = ADAM_B2 * v + (1.0 - ADAM_B2) * (g * g)
    m_hat = m2 / (1.0 - ADAM_B1 ** ADAM_STEP)
    v_hat = v2 / (1.0 - ADAM_B2 ** ADAM_STEP)
    return -ADAM_LR * (m_hat / (jnp.sqrt(v_hat) + ADAM_EPS) + ADAM_WD * w), m2, v2


def _adamw_sharded(me, mine, landed, w, m, v, tr, name):
    _, r, c = w.shape
    nt = r // tr
    assert r % tr == 0 and len(mine) == len(landed) == N_LAYERS == 2, name
    per_layer = 1 + len(FLIPS)

    def body(me_ref, *refs):
        terms, (w_ref, m_ref, v_ref), outs = refs[:2 * per_layer], refs[2 * per_layer:2 * per_layer + 3], refs[-4:]
        layer = pl.program_id(0)

        def total(group):
            g = group[0][0].astype(F32)
            for t in group[1:]:
                g = g + t[0].astype(F32)
            return g

        g = jnp.where(layer == 0, total(terms[:per_layer]), total(terms[per_layer:]))
        d, m2, v2 = _adamw(w_ref[0], g, m_ref[0], v_ref[0])
        for o, val in zip(outs, (g, d, m2, v2)):
            o[0] = val

    def held(l):
        return lambda layer, i: jnp.where(layer == l, i, nt - 1 if l == 0 else 0)

    in_specs = []
    for l in range(N_LAYERS):
        rows = held(l)
        in_specs.append(pl.BlockSpec((1, tr, c), functools.partial(lambda layer, i, me_ref, rows: (me_ref[0], rows(layer, i), 0), rows=rows)))
        for k in range(len(FLIPS)):
            in_specs.append(pl.BlockSpec(
                (1, tr, c), functools.partial(lambda layer, i, me_ref, rows, k: (k, rows(layer, i), 0), rows=rows, k=k)))
    tile = pl.BlockSpec((1, tr, c), lambda layer, i, me_ref: (layer, i, 0))
    operands = []
    for l in range(N_LAYERS):
        operands += [mine[l]] + [landed[l]] * len(FLIPS)
    return pl.pallas_call(
        body, name=name, out_shape=[jax.ShapeDtypeStruct(w.shape, F32)] * 4,
        grid_spec=pltpu.PrefetchScalarGridSpec(
            num_scalar_prefetch=1, grid=(N_LAYERS, nt), in_specs=in_specs + [tile] * 3, out_specs=[tile] * 4),
        compiler_params=_params("arbitrary", "arbitrary"),
    )(me, *operands, w, m, v)


def _adamw_replicated(parts, w, m, v, name):
    def body(p_ref, w_ref, m_ref, v_ref, g_ref, d_ref, m2_ref, v2_ref):
        g = p_ref[0]
        for j in range(1, N_DEV):
            g = g + p_ref[j]
        d, m2, v2 = _adamw(w_ref[...], g, m_ref[...], v_ref[...])
        g_ref[...], d_ref[...], m2_ref[...], v2_ref[...] = g, d, m2, v2

    return pl.pallas_call(body, name=name, out_shape=[jax.ShapeDtypeStruct(w.shape, F32)] * 4,
                          compiler_params=pltpu.CompilerParams(vmem_limit_bytes=VMEM_LIMIT_BYTES))(parts, w, m, v)


def _pack_small(vals):
    flat = jnp.concatenate([vals[n].reshape(-1) for n in SMALL_NAMES])
    return jnp.concatenate([flat, jnp.zeros((SMALL_ROWS * D_MODEL - flat.shape[0],), F32)]).reshape(SMALL_ROWS, D_MODEL)


def _unpack_small(packed, shapes):
    flat, out, at = packed.reshape(-1), {}, 0
    for n in SMALL_NAMES:
        size = math.prod(shapes[n])
        out[n] = flat[at:at + size].reshape(shapes[n])
        at += size
    return out


GATHER_GROUPS = ((0, ("w_in",)), (0, ("w_out", "w_up", "conv_w")), (0, ("w_down",)),
                 (1, ("w_in",)), (1, ("w_out", "w_up", "conv_w")), (1, ("w_down",)))
ADAMW_TILE_ROWS = {"w_in": 320, "w_out": 128, "w_up": 256, "w_down": 256, "conv_w": 3}


def _assemble(name, land):
    if name == "w_in":
        return land.reshape(IN_COLS, D_MODEL)
    if name == "conv_w":
        return land.transpose(1, 0, 2).reshape(3, 2 * D_FF)
    if name == "w_up":
        return land
    return land.reshape(-1, D_MODEL)


def _start_gathers(wts, me):
    started, groups = jnp.zeros((1, 1), F32), []
    for gi, (l, names) in enumerate(GATHER_GROUPS):
        local = {"conv_w": lambda a: a, "w_in": lambda a: a.T.astype(BF16)}
        blocks = [local.get(n, lambda a: a.astype(BF16))(wts[n][l]) for n in names]
        handle, token = _flat_start(blocks, False, name=f"gather_start_{gi}")
        groups.append(types.SimpleNamespace(layer=l, names=names, blocks=blocks, handle=handle, got=None, index=gi))
        started = started + token

    def fetch(l, name, after):
        grp = next(gr for gr in groups if gr.layer == l and name in gr.names)
        if grp.got is None:
            lands = _flat_wait(grp.handle, after, name=f"gather_wait_{grp.index}")[1]
            grp.got = {}
            for n, blk, land in zip(grp.names, grp.blocks, lands):
                own = (me,) + (0,) * blk.ndim
                grp.got[n] = _assemble(n, lax.dynamic_update_slice(land, blk[None], own))
        return grp.got[name]

    return fetch, started


def kernel(x, pre_mix_norm, w_in, v_norm_g, v_norm_b, w_spatial, b_spatial, out_norm_a, out_norm_b, w_out, post_mix_norm, pre_ffn_norm, w_up, conv_w, conv_b, w_down, post_ffn_norm, loss_target, m_pre_mix_norm, m_w_in, m_v_norm_g, m_v_norm_b, m_w_spatial, m_b_spatial, m_out_norm_a, m_out_norm_b, m_w_out, m_post_mix_norm, m_pre_ffn_norm, m_w_up, m_conv_w, m_conv_b, m_w_down, m_post_ffn_norm, v_pre_mix_norm, v_w_in, v_v_norm_g, v_v_norm_b, v_w_spatial, v_b_spatial, v_out_norm_a, v_out_norm_b, v_w_out, v_post_mix_norm, v_pre_ffn_norm, v_w_up, v_conv_w, v_conv_b, v_w_down, v_post_ffn_norm):
    wts = dict(zip(WEIGHT_NAMES, (pre_mix_norm, w_in, v_norm_g, v_norm_b, w_spatial, b_spatial, out_norm_a, out_norm_b,
                                  w_out, post_mix_norm, pre_ffn_norm, w_up, conv_w, conv_b, w_down, post_ffn_norm)))
    mom1 = dict(zip(WEIGHT_NAMES, (m_pre_mix_norm, m_w_in, m_v_norm_g, m_v_norm_b, m_w_spatial, m_b_spatial, m_out_norm_a,
                                   m_out_norm_b, m_w_out, m_post_mix_norm, m_pre_ffn_norm, m_w_up, m_conv_w, m_conv_b,
                                   m_w_down, m_post_ffn_norm)))
    mom2 = dict(zip(WEIGHT_NAMES, (v_pre_mix_norm, v_w_in, v_v_norm_g, v_v_norm_b, v_w_spatial, v_b_spatial, v_out_norm_a,
                                   v_out_norm_b, v_w_out, v_post_mix_norm, v_pre_ffn_norm, v_w_up, v_conv_w, v_conv_b,
                                   v_w_down, v_post_ffn_norm)))
    mx, my, mc = _place()
    me = 4 * mx + 2 * my + mc

    fetch, started = _start_gathers(wts, me)
    scatters = []

    def emit(l, blocks):
        names = tuple(blocks)
        handle, token = _flat_start([blocks[n] for n in names], True, name=f"scatter_start_{l}_{'_'.join(names)}")
        scatters.append((l, names, handle))
        return token

    smalls = {}

    def emit_small(l, g):
        pack = _pack_small({n: g[n] for n in SMALL_NAMES})
        handle, token = _flat_start([pack], False, name=f"small_grads_start_{l}")
        smalls[l] = (pack, handle)
        return token

    loss_local, dx, _ = _local_step(x[0], loss_target[0], fetch, wts, emit, emit_small, started)
    loss = lax.psum(loss_local, AXES)

    me_arr = jnp.reshape(me, (1,)).astype(jnp.int32)
    big_out = [{}, {}, {}, {}]

    def finish(group, after):
        mine, landed = {}, {}
        for l, names, handle in scatters:
            if names == group:
                sent, lands = _flat_wait(handle, after, name=f"scatter_wait_{l}_{'_'.join(names)}")
                for n, a, b in zip(names, sent, lands):
                    mine[l, n], landed[l, n] = a, b
        for n in group:
            flip = (lambda a: a.transpose(0, 2, 1)) if n == "w_in" else (lambda a: a)
            res = _adamw_sharded(me_arr, [mine[l, n] for l in range(N_LAYERS)], [landed[l, n] for l in range(N_LAYERS)],
                                 flip(wts[n]), flip(mom1[n]), flip(mom2[n]), ADAMW_TILE_ROWS[n], name=f"adamw_{n}")
            for kind in range(4):
                big_out[kind][n] = flip(res[kind])
        return res[0]

    done = smalls[0][1].thru[0]
    for group in [names for l, names, _ in scatters if l == 0][:-1]:
        done = finish(group, done)
    small_shapes = {n: wts[n].shape[1:] for n in SMALL_NAMES}
    per_layer = {}
    for l in reversed(range(N_LAYERS)):
        pack, handle = smalls[l]
        (landed,) = _flat_wait(handle, done, name=f"small_grads_wait_{l}")[1]
        everyone = lax.dynamic_update_slice(landed, pack[None], (me, 0, 0))
        packs = [_pack_small({n: t[n][l] for n in SMALL_NAMES}) for t in (wts, mom1, mom2)]
        res = _adamw_replicated(everyone, *packs, name=f"adamw_replicated_{l}")
        per_layer[l] = [_unpack_small(o, small_shapes) for o in res]
        done = res[0]
    finish(scatters[-1][1], done)
    small_out = [{n: jnp.stack([per_layer[l][kind][n] for l in range(N_LAYERS)]) for n in SMALL_NAMES}
                 for kind in range(4)]

    outs = [loss, dx[None]]
    for kind in range(4):
        outs += [big_out[kind][n] if n in BIG_NAMES else small_out[kind][n] for n in WEIGHT_NAMES]
    return tuple(outs)
```

```python
import functools
import math
import types

import jax
import jax.numpy as jnp
from jax import lax
from jax.experimental import pallas as pl
from jax.experimental.pallas import tpu as pltpu

F32 = jnp.float32
BF16 = jnp.bfloat16

D_MODEL = 1024
A_WIDTH = 512
A_GROUPS = 4
CHUNK = 128
B_WIDTH = 512
HEAD_DIM = 64
N_HEADS = B_WIDTH // HEAD_DIM
ROT_DIM = 16
ROPE_THETA = 500000.0
BAND = 128
DILATIONS = (1, 4, 16)
IN_COLS = 2560
D_FF = 4096
EPS = 1e-6
NEG_INF = -1e30
N_DEV = 8
N_LAYERS = 2

ADAM_LR = 0.001
ADAM_B1 = 0.9
ADAM_B2 = 0.999
ADAM_EPS = 1e-08
ADAM_WD = 0.01
ADAM_STEP = 10

VMEM_LIMIT_BYTES = 56 * 1024 * 1024
MESH_ID = pl.DeviceIdType.MESH
ANY = pl.BlockSpec(memory_space=pl.ANY)
AXES = ("x", "y", "c")

WEIGHT_NAMES = ("pre_mix_norm", "w_in", "v_norm_g", "v_norm_b", "w_spatial", "b_spatial", "out_norm_a", "out_norm_b",
                "w_out", "post_mix_norm", "pre_ffn_norm", "w_up", "conv_w", "conv_b", "w_down", "post_ffn_norm")
BIG_NAMES = ("w_in", "w_out", "w_up", "w_down", "conv_w")
SMALL_NAMES = tuple(n for n in WEIGHT_NAMES if n not in BIG_NAMES)

SMALL_ROWS = 80


def _params(*sem):
    return pltpu.CompilerParams(dimension_semantics=sem, vmem_limit_bytes=VMEM_LIMIT_BYTES)


def _dotg(a, b, ca, cb):
    return lax.dot_general(a.astype(BF16), b.astype(BF16), (((ca,), (cb,)), ((), ())), preferred_element_type=F32)


@jax.custom_vjp
def _bdot(a, b):
    return _dotg(a, b, 1, 0)


def _bdot_fwd(a, b):
    return _dotg(a, b, 1, 0), (a, b)


def _bdot_bwd(res, g):
    a, b = res
    return _dotg(g, b, 1, 1), _dotg(a, g, 0, 0)


_bdot.defvjp(_bdot_fwd, _bdot_bwd)


def _rms(x, g):
    return x * lax.rsqrt(jnp.mean(x * x, axis=-1, keepdims=True) + EPS) * g


def _layernorm(x, g, b):
    mu = jnp.mean(x, axis=-1, keepdims=True)
    xc = x - mu
    return xc * lax.rsqrt(jnp.mean(xc * xc, axis=-1, keepdims=True) + EPS) * g + b


def _gelu_erf(x):
    return x * (lax.erf(x * (1.0 / math.sqrt(2.0))) + 1.0) * 0.5


def _gelu_tanh(x):
    c = math.sqrt(2.0 / math.pi)
    return 0.5 * x * (1.0 + jnp.tanh(c * (x + 0.044715 * (x * x * x))))


def _gelu_tanh_and_slope(x):
    c, k = math.sqrt(2.0 / math.pi), 0.044715
    x2 = x * x
    t = jnp.tanh(c * (x + k * (x2 * x)))
    half_x, one_t = 0.5 * x, 1.0 + t
    return half_x * one_t, 0.5 * one_t + (half_x * (1.0 - t * t)) * (c + (3.0 * k * c) * x2)


def _rot_half(x):
    width = x.shape[1]
    lane = lax.broadcasted_iota(jnp.int32, x.shape, 1) % HEAD_DIM
    back = pltpu.roll(x, ROT_DIM // 2, 1)
    fwd = pltpu.roll(x, width - ROT_DIM // 2, 1)
    return jnp.where(lane < ROT_DIM // 2, -fwd, jnp.where(lane < ROT_DIM, back, 0.0))


def _split3(z):
    h0 = z.astype(BF16)
    r1 = z - h0.astype(F32)
    h1 = r1.astype(BF16)
    h2 = (r1 - h1.astype(F32)).astype(BF16)
    return h0, h1, h2


def _head_sum(z):
    width = z.shape[1]
    a = lax.broadcasted_iota(jnp.int32, (width, width), 0) // HEAD_DIM
    b = lax.broadcasted_iota(jnp.int32, (width, width), 1) // HEAD_DIM
    ones = jnp.where(a == b, 1.0, 0.0).astype(BF16)
    out = None
    for part in _split3(z):
        t = lax.dot_general(part, ones, (((1,), (0,)), ((), ())), preferred_element_type=F32)
        out = t if out is None else out + t
    return out


MATMUL_VMEM_BUDGET = 40 * 1024 * 1024


def _matmul_tiles(m, n, k, out_bytes):
    tn = n if n <= 1024 else (1280 if n % 1280 == 0 and n % 1024 else 1024)
    tk = k if k <= 1024 else (1280 if k % 1280 == 0 and k % 1024 else 1024)
    tm = m
    while tm > 256:
        blocks = 2 * 2 * (tm * tk + tk * tn) + 2 * out_bytes * tm * tn + (4 * tm * tn if k > tk else 0)
        if blocks <= MATMUL_VMEM_BUDGET and m % tm == 0:
            break
        tm //= 2
    return tm, tn, tk


def _matmul(a, b, *, mode, out_dtype, name, cols=None):
    wide = D_MODEL if cols is not None else None
    if mode == "nn":
        (m, k), (_, n) = a.shape, (b.shape if cols is None else (b.shape[1], cols[1] * wide))
    elif mode == "nt":
        (m, k), (n, _) = a.shape, (b.shape if cols is None else (b.shape[1], cols[1] * wide))
    else:
        (k, m), (_, n) = a.shape, b.shape
    tm, tn, tk = _matmul_tiles(m, n, k, jnp.dtype(out_dtype).itemsize)
    assert m % tm == 0 and n % tn == 0 and k % tk == 0, (name, m, n, k)
    nk = k // tk
    if mode == "nn":
        a_spec = pl.BlockSpec((tm, tk), lambda i, j, kk: (i, kk))
        b_spec = pl.BlockSpec((tk, tn), lambda i, j, kk: (kk, j))
        if cols is not None:
            assert tn == wide
            b_spec = pl.BlockSpec((None, tk, tn), lambda i, j, kk: (cols[0] + j, kk, 0))
        ca, cb = 1, 0
    elif mode == "nt":
        a_spec = pl.BlockSpec((tm, tk), lambda i, j, kk: (i, kk))
        b_spec = pl.BlockSpec((tn, tk), lambda i, j, kk: (j, kk))
        if cols is not None:
            assert tk == wide
            b_spec = pl.BlockSpec((None, tn, tk), lambda i, j, kk: (cols[0] + kk, j, 0))
        ca, cb = 1, 1
    else:
        a_spec = pl.BlockSpec((tk, tm), lambda i, j, kk: (kk, i))
        b_spec = pl.BlockSpec((tk, tn), lambda i, j, kk: (kk, j))
        ca, cb = 0, 0

    def body(a_ref, b_ref, o_ref, *acc):
        kk = pl.program_id(2)
        part = lax.dot_general(a_ref[...], b_ref[...], (((ca,), (cb,)), ((), ())), preferred_element_type=F32)
        if nk == 1:
            o_ref[...] = part.astype(o_ref.dtype)
            return
        acc_ref, = acc

        @pl.when(kk == 0)
        def _():
            acc_ref[...] = part

        @pl.when(kk > 0)
        def _():
            acc_ref[...] += part

        @pl.when(kk == nk - 1)
        def _():
            o_ref[...] = acc_ref[...].astype(o_ref.dtype)

    return pl.pallas_call(
        body, name=name, grid=(m // tm, n // tn, nk),
        in_specs=[a_spec, b_spec], out_specs=pl.BlockSpec((tm, tn), lambda i, j, kk: (i, j)),
        out_shape=jax.ShapeDtypeStruct((m, n), out_dtype),
        scratch_shapes=[pltpu.VMEM((tm, tn), F32)] if nk > 1 else [],
        compiler_params=_params("parallel", "parallel", "arbitrary"),
    )(a, b)


def _matmul_by_destination(a, b_lo, b_hi, *, name, tm=1024, tk=2048):
    (k, m), half = a.shape, N_DEV // 2
    assert b_lo.shape == b_hi.shape == (k, half * D_MODEL) and m % tm == 0 and k % tk == 0, name
    nk = k // tk

    def body(a_ref, lo_ref, hi_ref, o_ref, acc_ref):
        j, kk = pl.program_id(1), pl.program_id(2)

        def step(b_ref):
            part = lax.dot_general(a_ref[...], b_ref[...], (((0,), (0,)), ((), ())), preferred_element_type=F32)

            @pl.when(kk == 0)
            def _():
                acc_ref[...] = part

            @pl.when(kk > 0)
            def _():
                acc_ref[...] += part

        pl.when(j < half)(lambda: step(lo_ref))
        pl.when(j >= half)(lambda: step(hi_ref))

        @pl.when(kk == nk - 1)
        def _():
            o_ref[...] = acc_ref[...].astype(o_ref.dtype)

    lo_spec = pl.BlockSpec((tk, D_MODEL), lambda i, j, kk: (jnp.where(j < half, kk, nk - 1), jnp.minimum(j, half - 1)))
    hi_spec = pl.BlockSpec((tk, D_MODEL), lambda i, j, kk: (jnp.where(j >= half, kk, 0), jnp.maximum(j - half, 0)))
    return pl.pallas_call(
        body, name=name, grid=(m // tm, N_DEV, nk),
        in_specs=[pl.BlockSpec((tk, tm), lambda i, j, kk: (kk, i)), lo_spec, hi_spec],
        out_specs=pl.BlockSpec((None, tm, D_MODEL), lambda i, j, kk: (j, i, 0)),
        out_shape=jax.ShapeDtypeStruct((N_DEV, m, D_MODEL), BF16),
        scratch_shapes=[pltpu.VMEM((tm, D_MODEL), F32)],
        compiler_params=_params("parallel", "parallel", "arbitrary"),
    )(a, b_lo, b_hi)


LANES = 128


def _residues_to_rows(ref, scr, d):
    w = ref.shape[1] // d
    n = ref.shape[0]
    for r in range(d):
        for c in range(w // LANES):
            scr[c, pl.ds(r, n, stride=d), :] = ref[:, r * w + c * LANES:r * w + (c + 1) * LANES].astype(F32)
    return jnp.concatenate([scr[c] for c in range(w // LANES)], axis=1)


def _rows_to_residues(val, ref, scr, d):
    w = val.shape[1]
    n = ref.shape[0]
    for c in range(w // LANES):
        scr[c] = val[:, c * LANES:(c + 1) * LANES].astype(F32)
    for r in range(d):
        for c in range(w // LANES):
            ref[:, r * w + c * LANES:r * w + (c + 1) * LANES] = scr[c, pl.ds(r, n, stride=d), :].astype(ref.dtype)


HEAD_ROWS = 0


def _head_cols(z):
    width = z.shape[1]
    a = lax.broadcasted_iota(jnp.int32, (width, LANES), 0)
    b = lax.broadcasted_iota(jnp.int32, (width, LANES), 1)
    pick = jnp.where(a == b * HEAD_DIM, 1.0, 0.0).astype(BF16)
    out = None
    for part in _split3(z):
        t = lax.dot_general(part, pick, (((1,), (0,)), ((), ())), preferred_element_type=F32)
        out = t if out is None else out + t
    return out


def _head_rows_block(cols):
    return cols.T[:N_HEADS, :]


def _rowwise(fn, rows, consts, out_rows, out_acc, *, ts, name):
    rows = [tuple(r) + (1,) * (4 - len(r)) for r in rows]
    out_rows = [tuple(o) + (1,) * (3 - len(o)) for o in out_rows]
    s = rows[0][0].shape[0] * rows[0][3]
    assert s % ts == 0, (name, s, ts)
    n_rows, n_in = len(rows), len(rows) + len(consts)
    n_row = len(out_rows)
    n_out = n_row + len(out_acc)
    moved = [(idx, w) for idx, (_, w, _, d) in enumerate(rows) if d > 1]
    moved += [(n_rows + idx, w) for idx, (w, _, d) in enumerate(out_rows) if d > 1]

    def body(*refs):
        scratch = dict(zip([key for key, _ in moved], refs[n_in + n_out:]))
        vals = []
        for idx, r in enumerate(refs[:n_in]):
            d = rows[idx][3] if idx < n_rows else 1
            vals.append(r[...] if d == 1 else _residues_to_rows(r, scratch[idx], d))
        row_vals, acc_vals = fn(*vals)
        for idx, (r, v) in enumerate(zip(refs[n_in:n_in + n_row], row_vals)):
            d = out_rows[idx][2]
            if d == 1:
                r[...] = v.astype(r.dtype)
            elif d == HEAD_ROWS:
                r[...] = _head_rows_block(v)
            else:
                _rows_to_residues(v, r, scratch[n_rows + idx], d)
        first = pl.program_id(0) == 0
        for r, v in zip(refs[n_in + n_row:n_in + n_out], acc_vals):
            @pl.when(first)
            def _(r=r, v=v):
                r[...] = v

            @pl.when(jnp.logical_not(first))
            def _(r=r, v=v):
                r[...] += v

    in_specs = [pl.BlockSpec((ts // d, d * w), functools.partial(lambda i, cb: (i, cb), cb=cb)) for _, w, cb, d in rows]
    in_specs += [pl.BlockSpec(c.shape, lambda i: (0, 0)) for c in consts]
    out_specs = [pl.BlockSpec((N_HEADS, ts), lambda i: (0, i)) if d == HEAD_ROWS else
                 pl.BlockSpec((ts // d, d * w), lambda i: (i, 0)) for w, _, d in out_rows]
    out_specs += [pl.BlockSpec(sh, lambda i: (0, 0)) for sh in out_acc]
    out_shape = [jax.ShapeDtypeStruct((N_HEADS, s) if d == HEAD_ROWS else (s // d, d * w), dt) for w, dt, d in out_rows]
    out_shape += [jax.ShapeDtypeStruct(sh, F32) for sh in out_acc]
    outs = pl.pallas_call(
        body, name=name, grid=(s // ts,), in_specs=in_specs, out_specs=out_specs, out_shape=out_shape,
        scratch_shapes=[pltpu.VMEM((w // LANES, ts, LANES), F32) for _, w in moved],
        compiler_params=_params("arbitrary" if out_acc else "parallel"),
    )(*[a for a, _, _, _ in rows], *consts)
    return outs[:n_row], outs[n_row:]


def _full(a, d=1):
    return (a, a.shape[1] // d, 0, d)


def _gate_fn(zu, zv, vg, vb, ws0, ws1, ws2, ws3, bfull, ga):
    u = _gelu_erf(zu)
    vn = _layernorm(_gelu_erf(zv), vg, vb)
    p = lax.broadcasted_iota(jnp.int32, (CHUNK, CHUNK), 0)
    q = lax.broadcasted_iota(jnp.int32, (CHUNK, CHUNK), 1)
    tril = jnp.where(q <= p, 1.0, 0.0)
    group = lax.broadcasted_iota(jnp.int32, (1, A_WIDTH), 1) // CHUNK
    sg = bfull
    for g, w in enumerate((ws0, ws1, ws2, ws3)):
        sg = sg + _bdot(w * tril, jnp.where(group == g, vn, 0.0))
    return _rms(u * sg, ga)


GATE_TS = 2 * CHUNK


def _chunk_rows(rows):
    return [slice(c * CHUNK, (c + 1) * CHUNK) for c in range(rows // CHUNK)]


def _bias_reduce(dbf, name):
    def body(x_ref, o_ref):
        lane = lax.broadcasted_iota(jnp.int32, (CHUNK, CHUNK), 1)
        out = jnp.zeros((CHUNK, CHUNK), F32)
        for g in range(A_GROUPS):
            out = jnp.where(lane == g, jnp.sum(x_ref[:, g * CHUNK:(g + 1) * CHUNK], axis=1, keepdims=True), out)
        o_ref[...] = out

    return pl.pallas_call(body, name=name, out_shape=jax.ShapeDtypeStruct((CHUNK, CHUNK), F32))(dbf)


def _pair_mask(hh):
    lane = lax.broadcasted_iota(jnp.int32, (1, 2 * HEAD_DIM), 1)
    return (lane >= HEAD_DIM * hh) & (lane < HEAD_DIM * (hh + 1))


def _lane_pick(x2, lm):
    return jnp.max(jnp.where(lm, x2, -jnp.inf), axis=1, keepdims=True)


SCALE = HEAD_DIM ** -0.5


def _blocks_per_step(nb):
    return next(n for n in (4, 2, 1) if nb % n == 0)


def _units(nblk):
    return [(b, hp, hh) for b in range(nblk) for hp in range(N_HEADS // 2) for hh in range(2)]


def _attn_specs(nb, nblk):
    cur = pl.BlockSpec((nblk * BAND, B_WIDTH), lambda r, j: (j, r))
    prev = pl.BlockSpec((BAND, B_WIDTH), lambda r, j: (jnp.maximum(nblk * j - 1, 0), r))
    nxt = pl.BlockSpec((BAND, B_WIDTH), lambda r, j: (jnp.minimum(nblk * (j + 1), nb - 1), r))
    return cur, prev, nxt


def _pair_cols(hp):
    return slice(2 * HEAD_DIM * hp, 2 * HEAD_DIM * (hp + 1))


def _rows(b):
    return slice(b * BAND, (b + 1) * BAND)


def _with_prev(cur_ref, prev_ref, b, sl):
    if b == 0:
        return jnp.concatenate([prev_ref[:, sl], cur_ref[_rows(0), sl]], axis=0)
    return cur_ref[(b - 1) * BAND:(b + 1) * BAND, sl]


def _with_next(cur_ref, next_ref, b, sl, nblk):
    if b == nblk - 1:
        return jnp.concatenate([cur_ref[_rows(b), sl], next_ref[:, sl]], axis=0)
    return cur_ref[b * BAND:(b + 2) * BAND, sl]


def _band_valid(other_exists):
    row = lax.broadcasted_iota(jnp.int32, (BAND, 2 * BAND), 0)
    col = lax.broadcasted_iota(jnp.int32, (BAND, 2 * BAND), 1)
    return (col >= row) & (col <= row + BAND), other_exists


def _masked(lm, x):
    return jnp.where(lm, x, jnp.zeros_like(x))


def _attn_fwd(q, k, v, d, name):
    nb = q.shape[0] // BAND
    nblk = _blocks_per_step(nb)
    units = _units(nblk)
    cur, prev, _ = _attn_specs(nb, nblk)

    def body(q_ref, kc_ref, kp_ref, vc_ref, vp_ref, o_ref, l_ref):
        band, has_prev = _band_valid(pl.program_id(1) > 0)
        col = lax.broadcasted_iota(jnp.int32, (BAND, 2 * BAND), 1)
        valid = [band & ((col >= BAND) | has_prev)] + [band] * (nblk - 1)

        def scores(unit):
            b, hp, hh = unit
            sl = _pair_cols(hp)
            return _dotg(_masked(_pair_mask(hh), q_ref[_rows(b), sl]), _with_prev(kc_ref, kp_ref, b, sl), 1, 1)

        ahead, half = scores(units[0]), None
        for idx, (b, hp, hh) in enumerate(units):
            raw = ahead
            if idx + 1 < len(units):
                ahead = scores(units[idx + 1])
            sl, lm = _pair_cols(hp), _pair_mask(hh)
            s = jnp.where(valid[b], raw * SCALE, NEG_INF)
            m = jnp.max(s, axis=1, keepdims=True)
            p = jnp.exp(s - m)
            den = jnp.sum(p, axis=1, keepdims=True)
            o = _dotg(p, _with_prev(vc_ref, vp_ref, b, sl), 1, 0) / den
            lse = m + jnp.log(den)
            if hh == 0:
                half = (o, lse)
            else:
                o_ref[_rows(b), sl] = jnp.where(lm, o, half[0])
                l_ref[_rows(b), sl] = jnp.where(lm, lse, half[1])

    return pl.pallas_call(
        body, name=name, grid=(d, nb // nblk), in_specs=[cur, cur, prev, cur, prev], out_specs=[cur, cur],
        out_shape=[jax.ShapeDtypeStruct(q.shape, F32), jax.ShapeDtypeStruct(q.shape, F32)],
        compiler_params=_params("parallel", "parallel"),
    )(q, k, k, v, v)


def _attn_bwd_q(q, k, v, do, lse, delta, d, name):
    nb = q.shape[0] // BAND
    nblk = _blocks_per_step(nb)
    units = _units(nblk)
    cur, prev, _ = _attn_specs(nb, nblk)
    per_head = pl.BlockSpec((nblk * BAND, LANES), lambda r, j: (j, r))

    def body(q_ref, kc_ref, kp_ref, vc_ref, vp_ref, do_ref, l_ref, dl_ref, dq_ref):
        band, has_prev = _band_valid(pl.program_id(1) > 0)
        col = lax.broadcasted_iota(jnp.int32, (BAND, 2 * BAND), 1)
        valid = [band & ((col >= BAND) | has_prev)] + [band] * (nblk - 1)

        def products(unit):
            b, hp, hh = unit
            sl, lm = _pair_cols(hp), _pair_mask(hh)
            return (_dotg(_masked(lm, q_ref[_rows(b), sl]), _with_prev(kc_ref, kp_ref, b, sl), 1, 1),
                    _dotg(_masked(lm, do_ref[_rows(b), sl]), _with_prev(vc_ref, vp_ref, b, sl), 1, 1))

        ahead, half = products(units[0]), None
        for idx, (b, hp, hh) in enumerate(units):
            raw, dp = ahead
            if idx + 1 < len(units):
                ahead = products(units[idx + 1])
            sl, lm = _pair_cols(hp), _pair_mask(hh)
            s = jnp.where(valid[b], raw * SCALE, NEG_INF)
            head = lax.broadcasted_iota(jnp.int32, (1, LANES), 1) == 2 * hp + hh
            p = jnp.exp(s - _lane_pick(l_ref[_rows(b), :], head))
            ds = p * (dp - _lane_pick(dl_ref[_rows(b), :], head))
            dq = _dotg(ds, _with_prev(kc_ref, kp_ref, b, sl), 1, 0) * SCALE
            if hh == 0:
                half = dq
            else:
                dq_ref[_rows(b), sl] = jnp.where(lm, dq, half).astype(dq_ref.dtype)

    return pl.pallas_call(
        body, name=name, grid=(d, nb // nblk),
        in_specs=[cur, cur, prev, cur, prev, cur, per_head, per_head], out_specs=cur,
        out_shape=jax.ShapeDtypeStruct(q.shape, BF16),
        compiler_params=_params("parallel", "parallel"),
    )(q, k, k, v, v, do, lse, delta)


def _attn_bwd_kv(q, k, v, do, lse_t, delta_t, d, name):
    nb = q.shape[0] // BAND
    nblk = _blocks_per_step(nb)
    units = _units(nblk)
    cur, _, nxt = _attn_specs(nb, nblk)
    t_cur = pl.BlockSpec((1, N_HEADS, nblk * BAND), lambda r, j: (r, 0, j))
    t_nxt = pl.BlockSpec((1, N_HEADS, BAND), lambda r, j: (r, 0, jnp.minimum(nblk * (j + 1), nb - 1)))

    def body(k_ref, v_ref, qc_ref, qn_ref, doc_ref, don_ref, lc_ref, ln_ref, dlc_ref, dln_ref, dk_ref, dv_ref):
        band, has_next = _band_valid(pl.program_id(1) < nb // nblk - 1)
        col = lax.broadcasted_iota(jnp.int32, (BAND, 2 * BAND), 1)
        valid = [band] * (nblk - 1) + [band & ((col < BAND) | has_next)]

        def head_row(c_ref, n_ref, b, h):
            if b == nblk - 1:
                return jnp.concatenate([c_ref[0, h:h + 1, b * BAND:(b + 1) * BAND], n_ref[0, h:h + 1, :]], axis=1)
            return c_ref[0, h:h + 1, b * BAND:(b + 2) * BAND]

        def products(unit):
            b, hp, hh = unit
            sl, lm = _pair_cols(hp), _pair_mask(hh)
            return (_dotg(_masked(lm, k_ref[_rows(b), sl]), _with_next(qc_ref, qn_ref, b, sl, nblk), 1, 1),
                    _dotg(_masked(lm, v_ref[_rows(b), sl]), _with_next(doc_ref, don_ref, b, sl, nblk), 1, 1))

        ahead, half = products(units[0]), None
        for idx, (b, hp, hh) in enumerate(units):
            raw, dpt = ahead
            if idx + 1 < len(units):
                ahead = products(units[idx + 1])
            sl, lm, h = _pair_cols(hp), _pair_mask(hh), 2 * hp + hh
            st = jnp.where(valid[b], raw * SCALE, NEG_INF)
            pt = jnp.exp(st - head_row(lc_ref, ln_ref, b, h))
            dv = _dotg(pt, _with_next(doc_ref, don_ref, b, sl, nblk), 1, 0)
            dst = pt * (dpt - head_row(dlc_ref, dln_ref, b, h))
            dk = _dotg(dst, _with_next(qc_ref, qn_ref, b, sl, nblk), 1, 0) * SCALE
            if hh == 0:
                half = (dk, dv)
            else:
                dk_ref[_rows(b), sl] = jnp.where(lm, dk, half[0]).astype(dk_ref.dtype)
                dv_ref[_rows(b), sl] = jnp.where(lm, dv, half[1]).astype(dv_ref.dtype)

    return pl.pallas_call(
        body, name=name, grid=(d, nb // nblk),
        in_specs=[cur, cur, cur, nxt, cur, nxt, t_cur, t_nxt, t_cur, t_nxt], out_specs=[cur, cur],
        out_shape=[jax.ShapeDtypeStruct(q.shape, BF16), jax.ShapeDtypeStruct(q.shape, BF16)],
        compiler_params=_params("parallel", "parallel"),
    )(k, v, q, q, do, do, lse_t, lse_t, delta_t, delta_t)


def _spread_rows(a, d):
    return a.reshape(N_HEADS, a.shape[1] // d, d).transpose(2, 0, 1)


FF_TS = 512
FF_TC = 1024
FF_SUB = 256
HALO = 8
HALO_BF16 = 16
UP_BLOCKS = D_MODEL // FF_TC


def _conv3(ext, w, b):
    return b + w[0:1, :] * pltpu.roll(ext, 2, 0) + w[1:2, :] * pltpu.roll(ext, 1, 0) + w[2:3, :] * ext


def _ffn_specs(s, cols_first):
    nrb = s // FF_TS
    per, per16 = FF_TS // HALO, FF_TS // HALO_BF16

    def mk(block, fn):
        if cols_first:
            return pl.BlockSpec(block, lambda j, i: fn(i, j))
        return pl.BlockSpec(block, lambda i, j: fn(i, j))

    specs = types.SimpleNamespace(
        nrb=nrb, ncb=D_FF // FF_TC,
        row=mk((FF_TS, FF_TC), lambda i, j: (i, j)),
        before=mk((HALO, FF_TC), lambda i, j: (jnp.maximum(i * per - 1, 0), j)),
        after=mk((HALO, FF_TC), lambda i, j: (jnp.minimum((i + 1) * per, nrb * per - 1), j)),
        w=mk((3, FF_TC), lambda i, j: (0, j)),
        b=mk((1, FF_TC), lambda i, j: (0, j)),
        part=mk((HALO, FF_TC), lambda i, j: (i, j)),
        act=mk((FF_TS, D_MODEL), lambda i, j: (i, 0)),
        act_before=mk((HALO_BF16, D_MODEL), lambda i, j: (jnp.maximum(i * per16 - 1, 0), 0)),
        act_after=mk((HALO_BF16, D_MODEL), lambda i, j: (jnp.minimum((i + 1) * per16, nrb * per16 - 1), 0)),
        up_gate=mk((None, D_MODEL, FF_TC), lambda i, j: (j // UP_BLOCKS, 0, j % UP_BLOCKS)),
        up_val=mk((None, D_MODEL, FF_TC), lambda i, j: (N_DEV // 2 + j // UP_BLOCKS, 0, j % UP_BLOCKS)),
        down=mk((FF_TC, D_MODEL), lambda i, j: (j, 0)),
    )
    return specs


def _ffn_up_geglu(h, w_up, wg, wv, bg, bv, name):
    s = h.shape[0]
    sp = _ffn_specs(s, True)

    def body(h_ref, hb_ref, ugw_ref, uvw_ref, wg_ref, wv_ref, bg_ref, bv_ref, ug_ref, uv_ref, y_ref):
        keep = jnp.where(pl.program_id(1) > 0, 1.0, 0.0).astype(BF16)
        hext = jnp.concatenate([hb_ref[...] * keep, h_ref[...]], axis=0)
        n_sub = FF_TC // FF_SUB
        cols = [slice(c * FF_SUB, (c + 1) * FF_SUB) for c in range(n_sub)]
        up_gate = lambda c: _dotg(hext, ugw_ref[:, cols[c]], 1, 0)
        up_val = lambda c: _dotg(hext, uvw_ref[:, cols[c]], 1, 0)
        eg, ev = up_gate(0), up_val(0)
        for c in range(n_sub):
            sl, more = cols[c], c + 1 < n_sub
            eg_next = up_gate(c + 1) if more else None
            ug_ref[:, sl] = eg[HALO_BF16:, :]
            act = _gelu_tanh(_conv3(eg, wg_ref[:, sl], bg_ref[:, sl])[HALO_BF16:, :])
            ev_next = up_val(c + 1) if more else None
            uv_ref[:, sl] = ev[HALO_BF16:, :]
            y_ref[:, sl] = (act * _conv3(ev, wv_ref[:, sl], bv_ref[:, sl])[HALO_BF16:, :]).astype(y_ref.dtype)
            eg, ev = eg_next, ev_next

    return pl.pallas_call(
        body, name=name, grid=(sp.ncb, sp.nrb),
        in_specs=[sp.act, sp.act_before, sp.up_gate, sp.up_val, sp.w, sp.w, sp.b, sp.b],
        out_specs=[sp.row, sp.row, sp.row],
        out_shape=[jax.ShapeDtypeStruct((s, D_FF), F32), jax.ShapeDtypeStruct((s, D_FF), F32),
                   jax.ShapeDtypeStruct((s, D_FF), BF16)],
        compiler_params=_params("parallel", "parallel"),
    )(h, h, w_up, w_up, wg, wv, bg, bv)


def _sum_parts(parts, name):
    n = parts.shape[0] // HALO

    def body(p_ref, o_ref):
        acc = p_ref[0:HALO, :]
        for t in range(1, n):
            acc = acc + p_ref[t * HALO:(t + 1) * HALO, :]
        o_ref[...] = acc

    return pl.pallas_call(body, name=name, out_shape=jax.ShapeDtypeStruct((HALO, parts.shape[1]), F32))(parts)


def _ffn_geglu_bwd(ug, uv, df, w_down, w_up, wg, wv, bg, bv, name):
    s = ug.shape[0]
    sp = _ffn_specs(s, False)
    nrb = sp.nrb
    rows = FF_TS + 2 * HALO
    lo, hi = HALO, HALO + FF_TS

    def body(ug_ref, uv_ref, hg_ref, hv_ref, ng_ref, nv_ref, df_ref, dfn_ref, dw_ref, ugw_ref, uvw_ref,
             wg_ref, wv_ref, bg_ref, bv_ref, dug_ref, duv_ref, dh_ref, dwg_ref, dwv_ref):
        i, j = pl.program_id(0), pl.program_id(1)
        keep_top = jnp.where(i > 0, 1.0, 0.0)
        keep_bot = jnp.where(i < nrb - 1, 1.0, 0.0).astype(BF16)
        dfe = jnp.concatenate([df_ref[...], dfn_ref[...] * keep_bot], axis=0)

        def back(dc, e, w, du_ref, sl):
            up1 = pltpu.roll(dc, rows - 1, 0)
            up2 = pltpu.roll(dc, rows - 2, 0)
            du = (w[2:3, :] * dc + w[1:2, :] * up1 + w[0:1, :] * up2)[lo:hi, :].astype(BF16)
            du_ref[:, sl] = du
            p1, p2 = up1 * e, up2 * e
            colsum = lambda p: jnp.sum(p[lo:hi, :], axis=0, keepdims=True)
            row = lambda p, t: p[t:t + 1, :]
            d_w1 = colsum(p1) + row(p1, lo - 1) - row(p1, hi - 1)
            d_w0 = colsum(p2) + row(p2, lo - 2) + row(p2, lo - 1) - row(p2, hi - 2) - row(p2, hi - 1)
            sums = [d_w0, d_w1, colsum(dc * e), colsum(dc), jnp.zeros((HALO - 4, FF_SUB), F32)]
            return du, jnp.concatenate(sums, axis=0)

        dh = None
        n_sub = FF_TC // FF_SUB
        cols = [slice(c * FF_SUB, (c + 1) * FF_SUB) for c in range(n_sub)]
        d_act = lambda c: _dotg(dfe, dw_ref[cols[c], :], 1, 1)[:FF_TS + HALO, :]
        ahead = d_act(0)
        for c in range(n_sub):
            sl, dy = cols[c], ahead
            if c + 1 < n_sub:
                ahead = d_act(c + 1)
            dye = jnp.concatenate([jnp.zeros((HALO, FF_SUB), F32), dy], axis=0)
            eg = jnp.concatenate([hg_ref[:, sl] * keep_top, ug_ref[:, sl], ng_ref[:, sl]], axis=0)
            ev = jnp.concatenate([hv_ref[:, sl] * keep_top, uv_ref[:, sl], nv_ref[:, sl]], axis=0)
            wg_, wv_ = wg_ref[:, sl], wv_ref[:, sl]
            gate = _conv3(eg, wg_, bg_ref[:, sl])
            val = _conv3(ev, wv_, bv_ref[:, sl])
            act, slope = _gelu_tanh_and_slope(gate)
            dug, dwg_ref[:, sl] = back((dye * val) * slope, eg, wg_, dug_ref, sl)
            duv, dwv_ref[:, sl] = back(dye * act, ev, wv_, duv_ref, sl)
            term = _dotg(dug, ugw_ref[:, sl], 1, 1) + _dotg(duv, uvw_ref[:, sl], 1, 1)
            dh = term if dh is None else dh + term

        @pl.when(j == 0)
        def _():
            dh_ref[...] = dh

        @pl.when(j > 0)
        def _():
            dh_ref[...] += dh

    parts = jax.ShapeDtypeStruct((nrb * HALO, D_FF), F32)
    dug, duv, dh, pg, pv = pl.pallas_call(
        body, name=name, grid=(nrb, sp.ncb),
        in_specs=[sp.row, sp.row, sp.before, sp.before, sp.after, sp.after, sp.act, sp.act_after, sp.down,
                  sp.up_gate, sp.up_val, sp.w, sp.w, sp.b, sp.b],
        out_specs=[sp.row, sp.row, sp.act, sp.part, sp.part],
        out_shape=[jax.ShapeDtypeStruct((s, D_FF), BF16), jax.ShapeDtypeStruct((s, D_FF), BF16),
                   jax.ShapeDtypeStruct((s, D_MODEL), F32), parts, parts],
        compiler_params=_params("parallel", "arbitrary"),
    )(ug, uv, ug, uv, ug, uv, df, df, w_down, w_up, w_up, wg, wv, bg, bv)
    return dug, duv, dh, _sum_parts(pg, name=name + "_sum_gate"), _sum_parts(pv, name=name + "_sum_val")


def _rope_tables(s):
    inv = ROPE_THETA ** (-jnp.arange(0, ROT_DIM, 2, dtype=F32) / ROT_DIM)
    ang = jnp.arange(s, dtype=F32)[:, None] * inv[None, :]
    cos8, sin8 = jnp.cos(ang), jnp.sin(ang)
    rest = HEAD_DIM - ROT_DIM
    cos_h = jnp.concatenate([cos8, cos8, jnp.ones((s, rest), F32)], axis=1)
    sin_h = jnp.concatenate([sin8, sin8, jnp.zeros((s, rest), F32)], axis=1)
    return jnp.tile(cos_h, (1, LANES // HEAD_DIM)), jnp.tile(sin_h, (1, LANES // HEAD_DIM))


def _all_heads(table):
    return jnp.concatenate([table] * (B_WIDTH // LANES), axis=1)


def _layer_fwd(x, w, cos, sin, last):
    sv = types.SimpleNamespace(x=x)
    (sv.h1,), _ = _rowwise(lambda xb, g: ((_rms(xb, g),), ()), [_full(x)], [w.g_pre], [(D_MODEL, BF16)], [],
                           ts=512, name="pre_mix_norm")
    sv.proj = _matmul(sv.h1, w.big("w_in", sv.h1), mode="nt", out_dtype=F32, name="proj")

    gate_consts = [w.vg, w.vb, *w.ws, w.bfull, w.ga]
    def gate_fwd_fn(zu, zv, *consts):
        chunks = [_gate_fn(zu[r], zv[r], *consts) for r in _chunk_rows(zu.shape[0])]
        return (jnp.concatenate(chunks, axis=0),), ()

    (na,), _ = _rowwise(gate_fwd_fn, [(sv.proj, A_WIDTH, 0), (sv.proj, A_WIDTH, 1)], gate_consts,
                        [(A_WIDTH, BF16)], [], ts=GATE_TS, name="gate_fwd")

    def rope_fn(qr, kr, vr, cs, sn):
        cs, sn = _all_heads(cs), _all_heads(sn)
        return (qr * cs + _rot_half(qr) * sn, kr * cs + _rot_half(kr) * sn, vr), ()

    def rope_all(qr, kr, vr, cs, sn):
        return rope_fn(qr, kr, vr, cs, sn)[0] * len(DILATIONS), ()

    qkv, _ = _rowwise(
        rope_all, [(sv.proj, B_WIDTH, 2), (sv.proj, B_WIDTH, 3), (sv.proj, B_WIDTH, 4), _full(cos), _full(sin)], [],
        [(B_WIDTH, BF16, d) for d in DILATIONS for _ in range(3)], [], ts=512, name="rope_fwd")
    sv.qkv = {d: qkv[3 * i:3 * i + 3] for i, d in enumerate(DILATIONS)}

    branch = []
    for d in DILATIONS:
        o, l = _attn_fwd(*sv.qkv[d], d, name=f"attn_fwd_d{d}")
        branch += [_full(o, d), _full(l, d)]

    def combine_fn(o1, l1, o2, l2, o3, l3, nab, gb):
        m = jnp.maximum(jnp.maximum(l1, l2), l3)
        e1, e2, e3 = jnp.exp(l1 - m), jnp.exp(l2 - m), jnp.exp(l3 - m)
        den = e1 + e2 + e3
        ob = (e1 / den) * o1 + (e2 / den) * o2 + (e3 / den) * o3
        mixed = jnp.concatenate([nab, _rms(ob, gb).astype(BF16)], axis=1)
        lse = _head_cols(m + jnp.log(den))
        return (mixed, ob, lse) + (lse,) * len(DILATIONS), ()

    (sv.mixed, sv.ob, sv.lse_rows, *lses), _ = _rowwise(
        combine_fn, branch + [_full(na)], [w.gb],
        [(D_MODEL, BF16), (B_WIDTH, F32), (LANES, F32, HEAD_ROWS)] + [(LANES, F32, d) for d in DILATIONS], [],
        ts=512, name="combine")
    sv.lse = dict(zip(DILATIONS, lses))
    sv.y = _matmul(sv.mixed, w.big("w_out", sv.mixed), mode="nn", out_dtype=F32, name="mix_out")

    def mid_fn(xb, yb, g1, g2):
        x1 = xb + _rms(yb, g1)
        return (x1, _rms(x1, g2)), ()

    (sv.x1, sv.h2), _ = _rowwise(mid_fn, [_full(x), _full(sv.y)], [w.g_pm, w.g_pf], [(D_MODEL, F32), (D_MODEL, BF16)], [],
                                 ts=512, name="post_mix_norm")
    conv_w = w.big("conv_w", sv.h2)
    sv.ug, sv.uv, sv.yff = _ffn_up_geglu(sv.h2, w.big("w_up", sv.h2), conv_w[:, :D_FF], conv_w[:, D_FF:],
                                         w.cb_g, w.cb_v, name="ffn_up_geglu")
    sv.f = _matmul(sv.yff, w.big("w_down", sv.yff), mode="nn", out_dtype=F32, name="ffn_down")
    if last:
        return None, sv
    (x2,), _ = _rowwise(lambda xb, fb, g: ((xb + _rms(fb, g),), ()), [_full(sv.x1), _full(sv.f)], [w.g_post],
                        [(D_MODEL, F32)], [], ts=512, name="post_ffn_norm")
    return x2, sv


def _layer_bwd(dx2, sv, w, cos, sin, emit):
    g = {}

    def post_fn(fb, dxb, gp):
        _, vjp = jax.vjp(_rms, fb, gp)
        df, dg = vjp(dxb)
        return (df,), (dg,)

    (df,), (g["post_ffn_norm"],) = _rowwise(post_fn, [_full(sv.f), _full(dx2)], [w.g_post], [(D_MODEL, BF16)],
                                            [(1, D_MODEL)], ts=512, name="post_ffn_norm_bwd")
    big = {"w_down": _matmul(sv.yff, df, mode="tn", out_dtype=BF16, name="ffn_down_dw").reshape(N_DEV, -1, D_MODEL)}
    conv_w = w.big("conv_w", df)
    dug, duv, dh2, dwg, dwv = _ffn_geglu_bwd(sv.ug, sv.uv, df, w.big("w_down", df), w.big("w_up", df),
                                             conv_w[:, :D_FF], conv_w[:, D_FF:], w.cb_g, w.cb_v, name="ffn_geglu_bwd")
    big["conv_w"] = jnp.concatenate([dwg[0:3], dwv[0:3]], axis=1).reshape(3, N_DEV, D_MODEL).transpose(1, 0, 2)
    g["conv_b"] = jnp.concatenate([dwg[3], dwv[3]], axis=0)
    big["w_up"] = _matmul_by_destination(sv.h2, dug, duv, name="ffn_up_dw")
    g_pm = w.g_pm + emit(big)

    def mid_fn(x1b, yb, dhb, dxb, g1, g2):
        _, vjp2 = jax.vjp(_rms, x1b, g2)
        dx1h, dg2 = vjp2(dhb)
        dx1 = dxb + dx1h
        _, vjp1 = jax.vjp(_rms, yb, g1)
        dy, dg1 = vjp1(dx1)
        return (dx1, dy), (dg1, dg2)

    (dx1, dy), (g["post_mix_norm"], g["pre_ffn_norm"]) = _rowwise(
        mid_fn, [_full(sv.x1), _full(sv.y), _full(dh2), _full(dx2)], [g_pm, w.g_pf],
        [(D_MODEL, F32), (D_MODEL, BF16)], [(1, D_MODEL), (1, D_MODEL)], ts=256, name="post_mix_norm_bwd")
    dmixed = _matmul(dy, w.big("w_out", dy), mode="nt", out_dtype=F32, name="mix_out_dx")
    dw_out = _matmul(sv.mixed, dy, mode="tn", out_dtype=BF16, name="mix_out_dw").reshape(N_DEV, -1, D_MODEL)
    g_b = w.gb + emit({"w_out": dw_out})

    def attn_out_fn(obb, dmb, gb):
        _, vjp = jax.vjp(_rms, obb, gb)
        do, dgb = vjp(dmb)
        delta = _head_cols(_head_sum(do * obb))
        return (delta,) + (do,) * len(DILATIONS) + (delta,) * len(DILATIONS), (dgb,)

    (delta_rows, *outs), (g["out_norm_b"],) = _rowwise(
        attn_out_fn, [_full(sv.ob), (dmixed, B_WIDTH, 1)], [g_b],
        [(LANES, F32, HEAD_ROWS)] + [(B_WIDTH, BF16, d) for d in DILATIONS] + [(LANES, F32, d) for d in DILATIONS],
        [(1, B_WIDTH)], ts=512, name="attn_out_bwd")
    do = dict(zip(DILATIONS, outs[:len(DILATIONS)]))
    delta = dict(zip(DILATIONS, outs[len(DILATIONS):]))
    parts = {"q": [], "k": [], "v": []}
    for d in DILATIONS:
        qv, kv, vv = sv.qkv[d]
        dq = _attn_bwd_q(qv, kv, vv, do[d], sv.lse[d], delta[d], d, name=f"attn_bwd_q_d{d}")
        dk, dv = _attn_bwd_kv(qv, kv, vv, do[d], _spread_rows(sv.lse_rows, d), _spread_rows(delta_rows, d), d,
                              name=f"attn_bwd_kv_d{d}")
        parts["q"].append(_full(dq, d))
        parts["k"].append(_full(dk, d))
        parts["v"].append(_full(dv, d))

    def rope_bwd_fn(q1, q2, q3, k1, k2, k3, v1, v2, v3, cs, sn):
        cs, sn = _all_heads(cs), _all_heads(sn)

        def back(t):
            return t * cs - _rot_half(t * sn)
        return (jnp.concatenate([back(q1 + q2 + q3), back(k1 + k2 + k3), v1 + v2 + v3], axis=1),), ()

    (dzb,), _ = _rowwise(rope_bwd_fn, parts["q"] + parts["k"] + parts["v"] + [_full(cos), _full(sin)], [],
                         [(3 * B_WIDTH, BF16)], [], ts=256, name="rope_bwd")

    gate_consts = [w.vg, w.vb, *w.ws, w.bfull, w.ga]

    def gate_bwd_fn(zu, zv, dna, *consts):
        dz, sums = [], None
        for r in _chunk_rows(zu.shape[0]):
            _, vjp = jax.vjp(_gate_fn, zu[r], zv[r], *consts)
            grads = vjp(dna[r])
            dz.append(jnp.concatenate([grads[0], grads[1]], axis=1))
            sums = grads[2:] if sums is None else tuple(a + b for a, b in zip(sums, grads[2:]))
        return (jnp.concatenate(dz, axis=0),), tuple(sums)

    (dza,), gsmall = _rowwise(
        gate_bwd_fn, [(sv.proj, A_WIDTH, 0), (sv.proj, A_WIDTH, 1), (dmixed, A_WIDTH, 0)], gate_consts,
        [(2 * A_WIDTH, BF16)], [c.shape for c in gate_consts], ts=GATE_TS, name="gate_bwd")
    g["v_norm_g"], g["v_norm_b"] = gsmall[0], gsmall[1]
    g["w_spatial"] = jnp.stack(gsmall[2:6])
    g["b_spatial"] = _bias_reduce(gsmall[6], name="bias_reduce")[:, :A_GROUPS].T
    g["out_norm_a"] = gsmall[7]

    dproj = jnp.concatenate([dza, dzb], axis=1)
    dh1 = _matmul(dproj, w.big("w_in", dproj), mode="nn", out_dtype=F32, name="proj_dx")
    dw_in = _matmul(dproj, sv.h1, mode="tn", out_dtype=BF16, name="proj_dw").reshape(N_DEV, -1, D_MODEL)
    g_pre = w.g_pre + emit({"w_in": dw_in})

    def pre_fn(xb, dhb, dxb, gp):
        _, vjp = jax.vjp(_rms, xb, gp)
        dxh, dg = vjp(dhb)
        return (dxb + dxh,), (dg,)

    (dx,), (g["pre_mix_norm"],) = _rowwise(pre_fn, [_full(sv.x), _full(dh1), _full(dx1)], [g_pre], [(D_MODEL, F32)],
                                           [(1, D_MODEL)], ts=512, name="pre_mix_norm_bwd")
    return dx, g


def _layer_weights(l, full, small):
    row = lambda a: a[l].reshape(1, -1)
    return types.SimpleNamespace(
        big=functools.partial(full, l),
        g_pre=row(small["pre_mix_norm"]), vg=row(small["v_norm_g"]), vb=row(small["v_norm_b"]),
        ws=[small["w_spatial"][l, gi] for gi in range(A_GROUPS)],
        bfull=jnp.repeat(small["b_spatial"][l].T, CHUNK, axis=1),
        ga=row(small["out_norm_a"]), gb=row(small["out_norm_b"]),
        g_pm=row(small["post_mix_norm"]), g_pf=row(small["pre_ffn_norm"]),
        cb_g=small["conv_b"][l][:D_FF].reshape(1, -1), cb_v=small["conv_b"][l][D_FF:].reshape(1, -1),
        g_post=row(small["post_ffn_norm"]))


def _local_step(x, target, full, small, emit, emit_small, started):
    s = x.shape[0]
    cos, sin = _rope_tables(s)
    ws = [_layer_weights(l, full, small) for l in range(N_LAYERS)]
    ws[0].g_pre = ws[0].g_pre + started
    saved = []
    h = x
    for l in range(N_LAYERS):
        h, sv = _layer_fwd(h, ws[l], cos, sin, last=l == N_LAYERS - 1)
        saved.append(sv)

    def loss_fn(xb, fb, tb, g):
        diff = (xb + _rms(fb, g)) - tb
        return (diff * (1.0 / D_MODEL),), (jnp.sum(diff * diff, axis=0, keepdims=True),)

    (dh,), (sq,) = _rowwise(loss_fn, [_full(saved[-1].x1), _full(saved[-1].f), _full(target)], [ws[-1].g_post],
                            [(D_MODEL, F32)], [(1, D_MODEL)], ts=512, name="loss")
    loss = 0.5 * jnp.sum(sq) * (1.0 / D_MODEL)
    grads = [None] * N_LAYERS
    for l in reversed(range(N_LAYERS)):
        dh, grads[l] = _layer_bwd(dh, saved[l], ws[l], cos, sin, functools.partial(emit, l))
        token = emit_small(l, grads[l])
        if l > 0:
            ws[l - 1].g_post = ws[l - 1].g_post + token
    return loss, dh, grads


def _place():
    return lax.axis_index("x"), lax.axis_index("y"), lax.axis_index("c")


FLIPS = ((1, 0, 0), (0, 1, 0), (1, 1, 0), (0, 0, 1), (1, 0, 1), (0, 1, 1), (1, 1, 1))
HBM_SPEC = pl.BlockSpec(memory_space=pltpu.HBM)
SEM_SPEC = pl.BlockSpec(memory_space=pltpu.SEMAPHORE)
SPLIT_COPY = pltpu.CompilerParams(has_side_effects=pltpu.SideEffectType.DATAFLOW_SIDE_EFFECTING)


def _peers():
    mx, my, mc = _place()
    out = []
    for fx, fy, fc in FLIPS:
        px, py, pc = (1 - mx if fx else mx), (1 - my if fy else my), (1 - mc if fc else mc)
        out.append(((px, py, pc), 4 * px + 2 * py + pc))
    return out


def _flat_copies(scatter, src_refs, land_refs, send_sems, recv_sems):
    mx, my, mc = _place()
    me = 4 * mx + 2 * my + mc
    n = len(src_refs)
    copies = []
    for t in range(n):
        for i, (peer, number) in enumerate(_peers()):
            copies.append(pltpu.make_async_remote_copy(
                src_ref=src_refs[t].at[number] if scatter else src_refs[t],
                dst_ref=land_refs[t].at[i] if scatter else land_refs[t].at[me],
                send_sem=send_sems.at[t * len(FLIPS) + i], recv_sem=recv_sems.at[t * len(FLIPS) + i],
                device_id=peer, device_id_type=MESH_ID))
    return copies


def _flat_start(arrays, scatter, name):
    n = len(arrays)
    slots = len(FLIPS) if scatter else N_DEV
    lands = [lax.empty((slots,) + (a.shape[1:] if scatter else a.shape), a.dtype) for a in arrays]

    def body(*refs):
        src, land, (send_sems, recv_sems), token = refs[:n], refs[n:2 * n], refs[2 * n:2 * n + 2], refs[-1]
        for cp in _flat_copies(scatter, src, land, send_sems, recv_sems):
            cp.start()
        token[...] = jnp.zeros_like(token)

    hbm = [pltpu.HBM(a.shape, a.dtype) for a in arrays] + [pltpu.HBM(a.shape, a.dtype) for a in lands]
    sems = pltpu.SemaphoreType.DMA((n * len(FLIPS),))
    outs = pl.pallas_call(
        body, name=name, out_shape=(sems, sems, *hbm, jax.ShapeDtypeStruct((8, 128), F32)),
        in_specs=[HBM_SPEC] * (2 * n),
        out_specs=(SEM_SPEC, SEM_SPEC, *([HBM_SPEC] * (2 * n)), pl.BlockSpec(memory_space=pltpu.VMEM)),
        input_output_aliases={t: 2 + t for t in range(2 * n)}, compiler_params=SPLIT_COPY,
    )(*[pltpu.with_memory_space_constraint(a, pltpu.HBM) for a in (*arrays, *lands)])
    return types.SimpleNamespace(sems=outs[:2], thru=outs[2:2 + 2 * n], scatter=scatter, n=n), outs[-1][0:1, 0:1]


def _flat_wait(handle, after, name):
    n = handle.n

    def body(*refs):
        src, land, (send_sems, recv_sems) = refs[:n], refs[n:2 * n], refs[2 * n:2 * n + 2]
        for cp in _flat_copies(handle.scatter, src, land, send_sems, recv_sems):
            cp.wait_send()
            cp.wait_recv()

    outs = pl.pallas_call(
        body, name=name, out_shape=tuple(pltpu.HBM(a.shape, a.dtype) for a in handle.thru),
        in_specs=[HBM_SPEC] * (2 * n) + [SEM_SPEC, SEM_SPEC, ANY], out_specs=tuple([HBM_SPEC] * (2 * n)),
        input_output_aliases={t: t for t in range(2 * n)}, compiler_params=SPLIT_COPY,
    )(*handle.thru, *handle.sems, after)
    return outs[:n], outs[n:]


def _adamw(w, g, m, v):
    m2 = ADAM_B1 * m + (1.0 - ADAM_B1) * g
    v2 = ADAM_B2 * v + (1.0 - ADAM_B2) * (g * g)
    m_hat = m2 / (1.0 - ADAM_B1 ** ADAM_STEP)
    v_hat = v2 / (1.0 - ADAM_B2 ** ADAM_STEP)
    return -ADAM_LR * (m_hat / (jnp.sqrt(v_hat) + ADAM_EPS) + ADAM_WD * w), m2, v2


def _adamw_sharded(me, mine, landed, w, m, v, tr, name):
    _, r, c = w.shape
    nt = r // tr
    assert r % tr == 0 and len(mine) == len(landed) == N_LAYERS == 2, name
    per_layer = 1 + len(FLIPS)

    def body(me_ref, *refs):
        terms, (w_ref, m_ref, v_ref), outs = refs[:2 * per_layer], refs[2 * per_layer:2 * per_layer + 3], refs[-4:]
        layer = pl.program_id(0)

        def total(group):
            g = group[0][0].astype(F32)
            for t in group[1:]:
                g = g + t[0].astype(F32)
            return g

        g = jnp.where(layer == 0, total(terms[:per_layer]), total(terms[per_layer:]))
        d, m2, v2 = _adamw(w_ref[0], g, m_ref[0], v_ref[0])
        for o, val in zip(outs, (g, d, m2, v2)):
            o[0] = val

    def held(l):
        return lambda layer, i: jnp.where(layer == l, i, nt - 1 if l == 0 else 0)

    in_specs = []
    for l in range(N_LAYERS):
        rows = held(l)
        in_specs.append(pl.BlockSpec((1, tr, c), functools.partial(lambda layer, i, me_ref, rows: (me_ref[0], rows(layer, i), 0), rows=rows)))
        for k in range(len(FLIPS)):
            in_specs.append(pl.BlockSpec(
                (1, tr, c), functools.partial(lambda layer, i, me_ref, rows, k: (k, rows(layer, i), 0), rows=rows, k=k)))
    tile = pl.BlockSpec((1, tr, c), lambda layer, i, me_ref: (layer, i, 0))
    operands = []
    for l in range(N_LAYERS):
        operands += [mine[l]] + [landed[l]] * len(FLIPS)
    return pl.pallas_call(
        body, name=name, out_shape=[jax.ShapeDtypeStruct(w.shape, F32)] * 4,
        grid_spec=pltpu.PrefetchScalarGridSpec(
            num_scalar_prefetch=1, grid=(N_LAYERS, nt), in_specs=in_specs + [tile] * 3, out_specs=[tile] * 4),
        compiler_params=_params("arbitrary", "arbitrary"),
    )(me, *operands, w, m, v)


def _adamw_replicated(parts, w, m, v, name):
    def body(p_ref, w_ref, m_ref, v_ref, g_ref, d_ref, m2_ref, v2_ref):
        g = p_ref[0]
        for j in range(1, N_DEV):
            g = g + p_ref[j]
        d, m2, v2 = _adamw(w_ref[...], g, m_ref[...], v_ref[...])
        g_ref[...], d_ref[...], m2_ref[...], v2_ref[...] = g, d, m2, v2

    return pl.pallas_call(body, name=name, out_shape=[jax.ShapeDtypeStruct(w.shape, F32)] * 4,
                          compiler_params=pltpu.CompilerParams(vmem_limit_bytes=VMEM_LIMIT_BYTES))(parts, w, m, v)


def _pack_small(vals):
    flat = jnp.concatenate([vals[n].reshape(-1) for n in SMALL_NAMES])
    return jnp.concatenate([flat, jnp.zeros((SMALL_ROWS * D_MODEL - flat.shape[0],), F32)]).reshape(SMALL_ROWS, D_MODEL)


def _unpack_small(packed, shapes):
    flat, out, at = packed.reshape(-1), {}, 0
    for n in SMALL_NAMES:
        size = math.prod(shapes[n])
        out[n] = flat[at:at + size].reshape(shapes[n])
        at += size
    return out


GATHER_GROUPS = ((0, ("w_in",)), (0, ("w_out", "w_up", "conv_w")), (0, ("w_down",)),
                 (1, ("w_in",)), (1, ("w_out", "w_up", "conv_w")), (1, ("w_down",)))
ADAMW_TILE_ROWS = {"w_in": 320, "w_out": 128, "w_up": 256, "w_down": 256, "conv_w": 3}


def _assemble(name, land):
    if name == "w_in":
        return land.reshape(IN_COLS, D_MODEL)
    if name == "conv_w":
        return land.transpose(1, 0, 2).reshape(3, 2 * D_FF)
    if name == "w_up":
        return land
    return land.reshape(-1, D_MODEL)


def _start_gathers(wts, me):
    started, groups = jnp.zeros((1, 1), F32), []
    for gi, (l, names) in enumerate(GATHER_GROUPS):
        local = {"conv_w": lambda a: a, "w_in": lambda a: a.T.astype(BF16)}
        blocks = [local.get(n, lambda a: a.astype(BF16))(wts[n][l]) for n in names]
        handle, token = _flat_start(blocks, False, name=f"gather_start_{gi}")
        groups.append(types.SimpleNamespace(layer=l, names=names, blocks=blocks, handle=handle, got=None, index=gi))
        started = started + token

    def fetch(l, name, after):
        grp = next(gr for gr in groups if gr.layer == l and name in gr.names)
        if grp.got is None:
            lands = _flat_wait(grp.handle, after, name=f"gather_wait_{grp.index}")[1]
            grp.got = {}
            for n, blk, land in zip(grp.names, grp.blocks, lands):
                own = (me,) + (0,) * blk.ndim
                grp.got[n] = _assemble(n, lax.dynamic_update_slice(land, blk[None], own))
        return grp.got[name]

    return fetch, started


def kernel(x, pre_mix_norm, w_in, v_norm_g, v_norm_b, w_spatial, b_spatial, out_norm_a, out_norm_b, w_out, post_mix_norm, pre_ffn_norm, w_up, conv_w, conv_b, w_down, post_ffn_norm, loss_target, m_pre_mix_norm, m_w_in, m_v_norm_g, m_v_norm_b, m_w_spatial, m_b_spatial, m_out_norm_a, m_out_norm_b, m_w_out, m_post_mix_norm, m_pre_ffn_norm, m_w_up, m_conv_w, m_conv_b, m_w_down, m_post_ffn_norm, v_pre_mix_norm, v_w_in, v_v_norm_g, v_v_norm_b, v_w_spatial, v_b_spatial, v_out_norm_a, v_out_norm_b, v_w_out, v_post_mix_norm, v_pre_ffn_norm, v_w_up, v_conv_w, v_conv_b, v_w_down, v_post_ffn_norm):
    wts = dict(zip(WEIGHT_NAMES, (pre_mix_norm, w_in, v_norm_g, v_norm_b, w_spatial, b_spatial, out_norm_a, out_norm_b,
                                  w_out, post_mix_norm, pre_ffn_norm, w_up, conv_w, conv_b, w_down, post_ffn_norm)))
    mom1 = dict(zip(WEIGHT_NAMES, (m_pre_mix_norm, m_w_in, m_v_norm_g, m_v_norm_b, m_w_spatial, m_b_spatial, m_out_norm_a,
                                   m_out_norm_b, m_w_out, m_post_mix_norm, m_pre_ffn_norm, m_w_up, m_conv_w, m_conv_b,
                                   m_w_down, m_post_ffn_norm)))
    mom2 = dict(zip(WEIGHT_NAMES, (v_pre_mix_norm, v_w_in, v_v_norm_g, v_v_norm_b, v_w_spatial, v_b_spatial, v_out_norm_a,
                                   v_out_norm_b, v_w_out, v_post_mix_norm, v_pre_ffn_norm, v_w_up, v_conv_w, v_conv_b,
                                   v_w_down, v_post_ffn_norm)))
    mx, my, mc = _place()
    me = 4 * mx + 2 * my + mc

    fetch, started = _start_gathers(wts, me)
    scatters = []

    def emit(l, blocks):
        names = tuple(blocks)
        handle, token = _flat_start([blocks[n] for n in names], True, name=f"scatter_start_{l}_{'_'.join(names)}")
        scatters.append((l, names, handle))
        return token

    smalls = {}

    def emit_small(l, g):
        pack = _pack_small({n: g[n] for n in SMALL_NAMES})
        handle, token = _flat_start([pack], False, name=f"small_grads_start_{l}")
        smalls[l] = (pack, handle)
        return token

    loss_local, dx, _ = _local_step(x[0], loss_target[0], fetch, wts, emit, emit_small, started)
    loss = lax.psum(loss_local, AXES)

    me_arr = jnp.reshape(me, (1,)).astype(jnp.int32)
    big_out = [{}, {}, {}, {}]

    def finish(group, after):
        mine, landed = {}, {}
        for l, names, handle in scatters:
            if names == group:
                sent, lands = _flat_wait(handle, after, name=f"scatter_wait_{l}_{'_'.join(names)}")
                for n, a, b in zip(names, sent, lands):
                    mine[l, n], landed[l, n] = a, b
        for n in group:
            flip = (lambda a: a.transpose(0, 2, 1)) if n == "w_in" else (lambda a: a)
            res = _adamw_sharded(me_arr, [mine[l, n] for l in range(N_LAYERS)], [landed[l, n] for l in range(N_LAYERS)],
                                 flip(wts[n]), flip(mom1[n]), flip(mom2[n]), ADAMW_TILE_ROWS[n], name=f"adamw_{n}")
            for kind in range(4):
                big_out[kind][n] = flip(res[kind])
        return res[0]

    done = smalls[0][1].thru[0]
    for group in [names for l, names, _ in scatters if l == 0][:-1]:
        done = finish(group, done)
    small_shapes = {n: wts[n].shape[1:] for n in SMALL_NAMES}
    per_layer = {}
    for l in reversed(range(N_LAYERS)):
        pack, handle = smalls[l]
        (landed,) = _flat_wait(handle, done, name=f"small_grads_wait_{l}")[1]
        everyone = lax.dynamic_update_slice(landed, pack[None], (me, 0, 0))
        packs = [_pack_small({n: t[n][l] for n in SMALL_NAMES}) for t in (wts, mom1, mom2)]
        res = _adamw_replicated(everyone, *packs, name=f"adamw_replicated_{l}")
        per_layer[l] = [_unpack_small(o, small_shapes) for o in res]
        done = res[0]
    finish(scatters[-1][1], done)
    small_out = [{n: jnp.stack([per_layer[l][kind][n] for l in range(N_LAYERS)]) for n in SMALL_NAMES}
                 for kind in range(4)]

    outs = [loss, dx[None]]
    for kind in range(4):
        outs += [big_out[kind][n] if n in BIG_NAMES else small_out[kind][n] for n in WEIGHT_NAMES]
    return tuple(outs)
```

```python
import functools
import math
import types

import jax
import jax.numpy as jnp
from jax import lax
from jax.experimental import pallas as pl
from jax.experimental.pallas import tpu as pltpu

F32 = jnp.float32
BF16 = jnp.bfloat16

D_MODEL = 1024
A_WIDTH = 512
A_GROUPS = 4
CHUNK = 128
B_WIDTH = 512
HEAD_DIM = 64
N_HEADS = B_WIDTH // HEAD_DIM
ROT_DIM = 16
ROPE_THETA = 500000.0
BAND = 128
DILATIONS = (1, 4, 16)
IN_COLS = 2560
D_FF = 4096
EPS = 1e-6
NEG_INF = -1e30
N_DEV = 8
N_LAYERS = 2

ADAM_LR = 0.001
ADAM_B1 = 0.9
ADAM_B2 = 0.999
ADAM_EPS = 1e-08
ADAM_WD = 0.01
ADAM_STEP = 10

VMEM_LIMIT_BYTES = 56 * 1024 * 1024
MESH_ID = pl.DeviceIdType.MESH
ANY = pl.BlockSpec(memory_space=pl.ANY)
AXES = ("x", "y", "c")

WEIGHT_NAMES = ("pre_mix_norm", "w_in", "v_norm_g", "v_norm_b", "w_spatial", "b_spatial", "out_norm_a", "out_norm_b",
                "w_out", "post_mix_norm", "pre_ffn_norm", "w_up", "conv_w", "conv_b", "w_down", "post_ffn_norm")
BIG_NAMES = ("w_in", "w_out", "w_up", "w_down", "conv_w")
SMALL_NAMES = tuple(n for n in WEIGHT_NAMES if n not in BIG_NAMES)

SMALL_ROWS = 80


def _params(*sem):
    return pltpu.CompilerParams(dimension_semantics=sem, vmem_limit_bytes=VMEM_LIMIT_BYTES)


def _dotg(a, b, ca, cb):
    return lax.dot_general(a.astype(BF16), b.astype(BF16), (((ca,), (cb,)), ((), ())), preferred_element_type=F32)


@jax.custom_vjp
def _bdot(a, b):
    return _dotg(a, b, 1, 0)


def _bdot_fwd(a, b):
    return _dotg(a, b, 1, 0), (a, b)


def _bdot_bwd(res, g):
    a, b = res
    return _dotg(g, b, 1, 1), _dotg(a, g, 0, 0)


_bdot.defvjp(_bdot_fwd, _bdot_bwd)


def _rms(x, g):
    return x * lax.rsqrt(jnp.mean(x * x, axis=-1, keepdims=True) + EPS) * g


def _layernorm(x, g, b):
    mu = jnp.mean(x, axis=-1, keepdims=True)
    xc = x - mu
    return xc * lax.rsqrt(jnp.mean(xc * xc, axis=-1, keepdims=True) + EPS) * g + b


def _gelu_erf(x):
    return x * (lax.erf(x * (1.0 / math.sqrt(2.0))) + 1.0) * 0.5


def _gelu_tanh(x):
    c = math.sqrt(2.0 / math.pi)
    return 0.5 * x * (1.0 + jnp.tanh(c * (x + 0.044715 * (x * x * x))))


def _gelu_tanh_and_slope(x):
    c, k = math.sqrt(2.0 / math.pi), 0.044715
    x2 = x * x
    t = jnp.tanh(c * (x + k * (x2 * x)))
    half_x, one_t = 0.5 * x, 1.0 + t
    return half_x * one_t, 0.5 * one_t + (half_x * (1.0 - t * t)) * (c + (3.0 * k * c) * x2)


def _rot_half(x):
    width = x.shape[1]
    lane = lax.broadcasted_iota(jnp.int32, x.shape, 1) % HEAD_DIM
    back = pltpu.roll(x, ROT_DIM // 2, 1)
    fwd = pltpu.roll(x, width - ROT_DIM // 2, 1)
    return jnp.where(lane < ROT_DIM // 2, -fwd, jnp.where(lane < ROT_DIM, back, 0.0))


def _split3(z):
    h0 = z.astype(BF16)
    r1 = z - h0.astype(F32)
    h1 = r1.astype(BF16)
    h2 = (r1 - h1.astype(F32)).astype(BF16)
    return h0, h1, h2


def _head_sum(z):
    width = z.shape[1]
    a = lax.broadcasted_iota(jnp.int32, (width, width), 0) // HEAD_DIM
    b = lax.broadcasted_iota(jnp.int32, (width, width), 1) // HEAD_DIM
    ones = jnp.where(a == b, 1.0, 0.0).astype(BF16)
    out = None
    for part in _split3(z):
        t = lax.dot_general(part, ones, (((1,), (0,)), ((), ())), preferred_element_type=F32)
        out = t if out is None else out + t
    return out


MATMUL_VMEM_BUDGET = 40 * 1024 * 1024


def _matmul_tiles(m, n, k, out_bytes):
    tn = n if n <= 1024 else (1280 if n % 1280 == 0 and n % 1024 else 1024)
    tk = k if k <= 1024 else (1280 if k % 1280 == 0 and k % 1024 else 1024)
    tm = m
    while tm > 256:
        blocks = 2 * 2 * (tm * tk + tk * tn) + 2 * out_bytes * tm * tn + (4 * tm * tn if k > tk else 0)
        if blocks <= MATMUL_VMEM_BUDGET and m % tm == 0:
            break
        tm //= 2
    return tm, tn, tk


def _matmul(a, b, *, mode, out_dtype, name, cols=None):
    wide = D_MODEL if cols is not None else None
    if mode == "nn":
        (m, k), (_, n) = a.shape, (b.shape if cols is None else (b.shape[1], cols[1] * wide))
    elif mode == "nt":
        (m, k), (n, _) = a.shape, (b.shape if cols is None else (b.shape[1], cols[1] * wide))
    else:
        (k, m), (_, n) = a.shape, b.shape
    tm, tn, tk = _matmul_tiles(m, n, k, jnp.dtype(out_dtype).itemsize)
    assert m % tm == 0 and n % tn == 0 and k % tk == 0, (name, m, n, k)
    nk = k // tk
    if mode == "nn":
        a_spec = pl.BlockSpec((tm, tk), lambda i, j, kk: (i, kk))
        b_spec = pl.BlockSpec((tk, tn), lambda i, j, kk: (kk, j))
        if cols is not None:
            assert tn == wide
            b_spec = pl.BlockSpec((None, tk, tn), lambda i, j, kk: (cols[0] + j, kk, 0))
        ca, cb = 1, 0
    elif mode == "nt":
        a_spec = pl.BlockSpec((tm, tk), lambda i, j, kk: (i, kk))
        b_spec = pl.BlockSpec((tn, tk), lambda i, j, kk: (j, kk))
        if cols is not None:
            assert tk == wide
            b_spec = pl.BlockSpec((None, tn, tk), lambda i, j, kk: (cols[0] + kk, j, 0))
        ca, cb = 1, 1
    else:
        a_spec = pl.BlockSpec((tk, tm), lambda i, j, kk: (kk, i))
        b_spec = pl.BlockSpec((tk, tn), lambda i, j, kk: (kk, j))
        ca, cb = 0, 0

    def body(a_ref, b_ref, o_ref, *acc):
        kk = pl.program_id(2)
        part = lax.dot_general(a_ref[...], b_ref[...], (((ca,), (cb,)), ((), ())), preferred_element_type=F32)
        if nk == 1:
            o_ref[...] = part.astype(o_ref.dtype)
            return
        acc_ref, = acc

        @pl.when(kk == 0)
        def _():
            acc_ref[...] = part

        @pl.when(kk > 0)
        def _():
            acc_ref[...] += part

        @pl.when(kk == nk - 1)
        def _():
            o_ref[...] = acc_ref[...].astype(o_ref.dtype)

    return pl.pallas_call(
        body, name=name, grid=(m // tm, n // tn, nk),
        in_specs=[a_spec, b_spec], out_specs=pl.BlockSpec((tm, tn), lambda i, j, kk: (i, j)),
        out_shape=jax.ShapeDtypeStruct((m, n), out_dtype),
        scratch_shapes=[pltpu.VMEM((tm, tn), F32)] if nk > 1 else [],
        compiler_params=_params("parallel", "parallel", "arbitrary"),
    )(a, b)


def _matmul_by_destination(a, b_lo, b_hi, *, name, tm=1024, tk=2048):
    (k, m), half = a.shape, N_DEV // 2
    assert b_lo.shape == b_hi.shape == (k, half * D_MODEL) and m % tm == 0 and k % tk == 0, name
    nk = k // tk

    def body(a_ref, lo_ref, hi_ref, o_ref, acc_ref):
        j, kk = pl.program_id(1), pl.program_id(2)

        def step(b_ref):
            part = lax.dot_general(a_ref[...], b_ref[...], (((0,), (0,)), ((), ())), preferred_element_type=F32)

            @pl.when(kk == 0)
            def _():
                acc_ref[...] = part

            @pl.when(kk > 0)
            def _():
                acc_ref[...] += part

        pl.when(j < half)(lambda: step(lo_ref))
        pl.when(j >= half)(lambda: step(hi_ref))

        @pl.when(kk == nk - 1)
        def _():
            o_ref[...] = acc_ref[...].astype(o_ref.dtype)

    lo_spec = pl.BlockSpec((tk, D_MODEL), lambda i, j, kk: (jnp.where(j < half, kk, nk - 1), jnp.minimum(j, half - 1)))
    hi_spec = pl.BlockSpec((tk, D_MODEL), lambda i, j, kk: (jnp.where(j >= half, kk, 0), jnp.maximum(j - half, 0)))
    return pl.pallas_call(
        body, name=name, grid=(m // tm, N_DEV, nk),
        in_specs=[pl.BlockSpec((tk, tm), lambda i, j, kk: (kk, i)), lo_spec, hi_spec],
        out_specs=pl.BlockSpec((None, tm, D_MODEL), lambda i, j, kk: (j, i, 0)),
        out_shape=jax.ShapeDtypeStruct((N_DEV, m, D_MODEL), BF16),
        scratch_shapes=[pltpu.VMEM((tm, D_MODEL), F32)],
        compiler_params=_params("parallel", "parallel", "arbitrary"),
    )(a, b_lo, b_hi)


LANES = 128


def _residues_to_rows(ref, scr, d):
    w = ref.shape[1] // d
    n = ref.shape[0]
    for r in range(d):
        for c in range(w // LANES):
            scr[c, pl.ds(r, n, stride=d), :] = ref[:, r * w + c * LANES:r * w + (c + 1) * LANES].astype(F32)
    return jnp.concatenate([scr[c] for c in range(w // LANES)], axis=1)


def _rows_to_residues(val, ref, scr, d):
    w = val.shape[1]
    n = ref.shape[0]
    for c in range(w // LANES):
        scr[c] = val[:, c * LANES:(c + 1) * LANES].astype(F32)
    for r in range(d):
        for c in range(w // LANES):
            ref[:, r * w + c * LANES:r * w + (c + 1) * LANES] = scr[c, pl.ds(r, n, stride=d), :].astype(ref.dtype)


HEAD_ROWS = 0


def _head_cols(z):
    width = z.shape[1]
    a = lax.broadcasted_iota(jnp.int32, (width, LANES), 0)
    b = lax.broadcasted_iota(jnp.int32, (width, LANES), 1)
    pick = jnp.where(a == b * HEAD_DIM, 1.0, 0.0).astype(BF16)
    out = None
    for part in _split3(z):
        t = lax.dot_general(part, pick, (((1,), (0,)), ((), ())), preferred_element_type=F32)
        out = t if out is None else out + t
    return out


def _head_rows_block(cols):
    return cols.T[:N_HEADS, :]


def _rowwise(fn, rows, consts, out_rows, out_acc, *, ts, name):
    rows = [tuple(r) + (1,) * (4 - len(r)) for r in rows]
    out_rows = [tuple(o) + (1,) * (3 - len(o)) for o in out_rows]
    s = rows[0][0].shape[0] * rows[0][3]
    assert s % ts == 0, (name, s, ts)
    n_rows, n_in = len(rows), len(rows) + len(consts)
    n_row = len(out_rows)
    n_out = n_row + len(out_acc)
    moved = [(idx, w) for idx, (_, w, _, d) in enumerate(rows) if d > 1]
    moved += [(n_rows + idx, w) for idx, (w, _, d) in enumerate(out_rows) if d > 1]

    def body(*refs):
        scratch = dict(zip([key for key, _ in moved], refs[n_in + n_out:]))
        vals = []
        for idx, r in enumerate(refs[:n_in]):
            d = rows[idx][3] if idx < n_rows else 1
            vals.append(r[...] if d == 1 else _residues_to_rows(r, scratch[idx], d))
        row_vals, acc_vals = fn(*vals)
        for idx, (r, v) in enumerate(zip(refs[n_in:n_in + n_row], row_vals)):
            d = out_rows[idx][2]
            if d == 1:
                r[...] = v.astype(r.dtype)
            elif d == HEAD_ROWS:
                r[...] = _head_rows_block(v)
            else:
                _rows_to_residues(v, r, scratch[n_rows + idx], d)
        first = pl.program_id(0) == 0
        for r, v in zip(refs[n_in + n_row:n_in + n_out], acc_vals):
            @pl.when(first)
            def _(r=r, v=v):
                r[...] = v

            @pl.when(jnp.logical_not(first))
            def _(r=r, v=v):
                r[...] += v

    in_specs = [pl.BlockSpec((ts // d, d * w), functools.partial(lambda i, cb: (i, cb), cb=cb)) for _, w, cb, d in rows]
    in_specs += [pl.BlockSpec(c.shape, lambda i: (0, 0)) for c in consts]
    out_specs = [pl.BlockSpec((N_HEADS, ts), lambda i: (0, i)) if d == HEAD_ROWS else
                 pl.BlockSpec((ts // d, d * w), lambda i: (i, 0)) for w, _, d in out_rows]
    out_specs += [pl.BlockSpec(sh, lambda i: (0, 0)) for sh in out_acc]
    out_shape = [jax.ShapeDtypeStruct((N_HEADS, s) if d == HEAD_ROWS else (s // d, d * w), dt) for w, dt, d in out_rows]
    out_shape += [jax.ShapeDtypeStruct(sh, F32) for sh in out_acc]
    outs = pl.pallas_call(
        body, name=name, grid=(s // ts,), in_specs=in_specs, out_specs=out_specs, out_shape=out_shape,
        scratch_shapes=[pltpu.VMEM((w // LANES, ts, LANES), F32) for _, w in moved],
        compiler_params=_params("arbitrary" if out_acc else "parallel"),
    )(*[a for a, _, _, _ in rows], *consts)
    return outs[:n_row], outs[n_row:]


def _full(a, d=1):
    return (a, a.shape[1] // d, 0, d)


def _gate_fn(zu, zv, vg, vb, ws0, ws1, ws2, ws3, bfull, ga):
    u = _gelu_erf(zu)
    vn = _layernorm(_gelu_erf(zv), vg, vb)
    p = lax.broadcasted_iota(jnp.int32, (CHUNK, CHUNK), 0)
    q = lax.broadcasted_iota(jnp.int32, (CHUNK, CHUNK), 1)
    tril = jnp.where(q <= p, 1.0, 0.0)
    group = lax.broadcasted_iota(jnp.int32, (1, A_WIDTH), 1) // CHUNK
    sg = bfull
    for g, w in enumerate((ws0, ws1, ws2, ws3)):
        sg = sg + _bdot(w * tril, jnp.where(group == g, vn, 0.0))
    return _rms(u * sg, ga)


GATE_TS = 2 * CHUNK


def _chunk_rows(rows):
    return [slice(c * CHUNK, (c + 1) * CHUNK) for c in range(rows // CHUNK)]


def _bias_reduce(dbf, name):
    def body(x_ref, o_ref):
        lane = lax.broadcasted_iota(jnp.int32, (CHUNK, CHUNK), 1)
        out = jnp.zeros((CHUNK, CHUNK), F32)
        for g in range(A_GROUPS):
            out = jnp.where(lane == g, jnp.sum(x_ref[:, g * CHUNK:(g + 1) * CHUNK], axis=1, keepdims=True), out)
        o_ref[...] = out

    return pl.pallas_call(body, name=name, out_shape=jax.ShapeDtypeStruct((CHUNK, CHUNK), F32))(dbf)


def _pair_mask(hh):
    lane = lax.broadcasted_iota(jnp.int32, (1, 2 * HEAD_DIM), 1)
    return (lane >= HEAD_DIM * hh) & (lane < HEAD_DIM * (hh + 1))


def _lane_pick(x2, lm):
    return jnp.max(jnp.where(lm, x2, -jnp.inf), axis=1, keepdims=True)


SCALE = HEAD_DIM ** -0.5


def _blocks_per_step(nb):
    return next(n for n in (4, 2, 1) if nb % n == 0)


def _units(nblk):
    return [(b, hp, hh) for b in range(nblk) for hp in range(N_HEADS // 2) for hh in range(2)]


def _attn_specs(nb, nblk):
    cur = pl.BlockSpec((nblk * BAND, B_WIDTH), lambda r, j: (j, r))
    prev = pl.BlockSpec((BAND, B_WIDTH), lambda r, j: (jnp.maximum(nblk * j - 1, 0), r))
    nxt = pl.BlockSpec((BAND, B_WIDTH), lambda r, j: (jnp.minimum(nblk * (j + 1), nb - 1), r))
    return cur, prev, nxt


def _pair_cols(hp):
    return slice(2 * HEAD_DIM * hp, 2 * HEAD_DIM * (hp + 1))


def _rows(b):
    return slice(b * BAND, (b + 1) * BAND)


def _with_prev(cur_ref, prev_ref, b, sl):
    if b == 0:
        return jnp.concatenate([prev_ref[:, sl], cur_ref[_rows(0), sl]], axis=0)
    return cur_ref[(b - 1) * BAND:(b + 1) * BAND, sl]


def _with_next(cur_ref, next_ref, b, sl, nblk):
    if b == nblk - 1:
        return jnp.concatenate([cur_ref[_rows(b), sl], next_ref[:, sl]], axis=0)
    return cur_ref[b * BAND:(b + 2) * BAND, sl]


def _band_valid(other_exists):
    row = lax.broadcasted_iota(jnp.int32, (BAND, 2 * BAND), 0)
    col = lax.broadcasted_iota(jnp.int32, (BAND, 2 * BAND), 1)
    return (col >= row) & (col <= row + BAND), other_exists


def _masked(lm, x):
    return jnp.where(lm, x, jnp.zeros_like(x))


def _attn_fwd(q, k, v, d, name):
    nb = q.shape[0] // BAND
    nblk = _blocks_per_step(nb)
    units = _units(nblk)
    cur, prev, _ = _attn_specs(nb, nblk)

    def body(q_ref, kc_ref, kp_ref, vc_ref, vp_ref, o_ref, l_ref):
        band, has_prev = _band_valid(pl.program_id(1) > 0)
        col = lax.broadcasted_iota(jnp.int32, (BAND, 2 * BAND), 1)
        valid = [band & ((col >= BAND) | has_prev)] + [band] * (nblk - 1)

        def scores(unit):
            b, hp, hh = unit
            sl = _pair_cols(hp)
            return _dotg(_masked(_pair_mask(hh), q_ref[_rows(b), sl]), _with_prev(kc_ref, kp_ref, b, sl), 1, 1)

        ahead, half = scores(units[0]), None
        for idx, (b, hp, hh) in enumerate(units):
            raw = ahead
            if idx + 1 < len(units):
                ahead = scores(units[idx + 1])
            sl, lm = _pair_cols(hp), _pair_mask(hh)
            s = jnp.where(valid[b], raw * SCALE, NEG_INF)
            m = jnp.max(s, axis=1, keepdims=True)
            p = jnp.exp(s - m)
            den = jnp.sum(p, axis=1, keepdims=True)
            o = _dotg(p, _with_prev(vc_ref, vp_ref, b, sl), 1, 0) / den
            lse = m + jnp.log(den)
            if hh == 0:
                half = (o, lse)
            else:
                o_ref[_rows(b), sl] = jnp.where(lm, o, half[0])
                l_ref[_rows(b), sl] = jnp.where(lm, lse, half[1])

    return pl.pallas_call(
        body, name=name, grid=(d, nb // nblk), in_specs=[cur, cur, prev, cur, prev], out_specs=[cur, cur],
        out_shape=[jax.ShapeDtypeStruct(q.shape, F32), jax.ShapeDtypeStruct(q.shape, F32)],
        compiler_params=_params("parallel", "parallel"),
    )(q, k, k, v, v)


def _attn_bwd_q(q, k, v, do, lse, delta, d, name):
    nb = q.shape[0] // BAND
    nblk = _blocks_per_step(nb)
    units = _units(nblk)
    cur, prev, _ = _attn_specs(nb, nblk)
    per_head = pl.BlockSpec((nblk * BAND, LANES), lambda r, j: (j, r))

    def body(q_ref, kc_ref, kp_ref, vc_ref, vp_ref, do_ref, l_ref, dl_ref, dq_ref):
        band, has_prev = _band_valid(pl.program_id(1) > 0)
        col = lax.broadcasted_iota(jnp.int32, (BAND, 2 * BAND), 1)
        valid = [band & ((col >= BAND) | has_prev)] + [band] * (nblk - 1)

        def products(unit):
            b, hp, hh = unit
            sl, lm = _pair_cols(hp), _pair_mask(hh)
            return (_dotg(_masked(lm, q_ref[_rows(b), sl]), _with_prev(kc_ref, kp_ref, b, sl), 1, 1),
                    _dotg(_masked(lm, do_ref[_rows(b), sl]), _with_prev(vc_ref, vp_ref, b, sl), 1, 1))

        ahead, half = products(units[0]), None
        for idx, (b, hp, hh) in enumerate(units):
            raw, dp = ahead
            if idx + 1 < len(units):
                ahead = products(units[idx + 1])
            sl, lm = _pair_cols(hp), _pair_mask(hh)
            s = jnp.where(valid[b], raw * SCALE, NEG_INF)
            head = lax.broadcasted_iota(jnp.int32, (1, LANES), 1) == 2 * hp + hh
            p = jnp.exp(s - _lane_pick(l_ref[_rows(b), :], head))
            ds = p * (dp - _lane_pick(dl_ref[_rows(b), :], head))
            dq = _dotg(ds, _with_prev(kc_ref, kp_ref, b, sl), 1, 0) * SCALE
            if hh == 0:
                half = dq
            else:
                dq_ref[_rows(b), sl] = jnp.where(lm, dq, half).astype(dq_ref.dtype)

    return pl.pallas_call(
        body, name=name, grid=(d, nb // nblk),
        in_specs=[cur, cur, prev, cur, prev, cur, per_head, per_head], out_specs=cur,
        out_shape=jax.ShapeDtypeStruct(q.shape, BF16),
        compiler_params=_params("parallel", "parallel"),
    )(q, k, k, v, v, do, lse, delta)


def _attn_bwd_kv(q, k, v, do, lse_t, delta_t, d, name):
    nb = q.shape[0] // BAND
    nblk = _blocks_per_step(nb)
    units = _units(nblk)
    cur, _, nxt = _attn_specs(nb, nblk)
    t_cur = pl.BlockSpec((1, N_HEADS, nblk * BAND), lambda r, j: (r, 0, j))
    t_nxt = pl.BlockSpec((1, N_HEADS, BAND), lambda r, j: (r, 0, jnp.minimum(nblk * (j + 1), nb - 1)))

    def body(k_ref, v_ref, qc_ref, qn_ref, doc_ref, don_ref, lc_ref, ln_ref, dlc_ref, dln_ref, dk_ref, dv_ref):
        band, has_next = _band_valid(pl.program_id(1) < nb // nblk - 1)
        col = lax.broadcasted_iota(jnp.int32, (BAND, 2 * BAND), 1)
        valid = [band] * (nblk - 1) + [band & ((col < BAND) | has_next)]

        def head_row(c_ref, n_ref, b, h):
            if b == nblk - 1:
                return jnp.concatenate([c_ref[0, h:h + 1, b * BAND:(b + 1) * BAND], n_ref[0, h:h + 1, :]], axis=1)
            return c_ref[0, h:h + 1, b * BAND:(b + 2) * BAND]

        def products(unit):
            b, hp, hh = unit
            sl, lm = _pair_cols(hp), _pair_mask(hh)
            return (_dotg(_masked(lm, k_ref[_rows(b), sl]), _with_next(qc_ref, qn_ref, b, sl, nblk), 1, 1),
                    _dotg(_masked(lm, v_ref[_rows(b), sl]), _with_next(doc_ref, don_ref, b, sl, nblk), 1, 1))

        ahead, half = products(units[0]), None
        for idx, (b, hp, hh) in enumerate(units):
            raw, dpt = ahead
            if idx + 1 < len(units):
                ahead = products(units[idx + 1])
            sl, lm, h = _pair_cols(hp), _pair_mask(hh), 2 * hp + hh
            st = jnp.where(valid[b], raw * SCALE, NEG_INF)
            pt = jnp.exp(st - head_row(lc_ref, ln_ref, b, h))
            dv = _dotg(pt, _with_next(doc_ref, don_ref, b, sl, nblk), 1, 0)
            dst = pt * (dpt - head_row(dlc_ref, dln_ref, b, h))
            dk = _dotg(dst, _with_next(qc_ref, qn_ref, b, sl, nblk), 1, 0) * SCALE
            if hh == 0:
                half = (dk, dv)
            else:
                dk_ref[_rows(b), sl] = jnp.where(lm, dk, half[0]).astype(dk_ref.dtype)
                dv_ref[_rows(b), sl] = jnp.where(lm, dv, half[1]).astype(dv_ref.dtype)

    return pl.pallas_call(
        body, name=name, grid=(d, nb // nblk),
        in_specs=[cur, cur, cur, nxt, cur, nxt, t_cur, t_nxt, t_cur, t_nxt], out_specs=[cur, cur],
        out_shape=[jax.ShapeDtypeStruct(q.shape, BF16), jax.ShapeDtypeStruct(q.shape, BF16)],
        compiler_params=_params("parallel", "parallel"),
    )(k, v, q, q, do, do, lse_t, lse_t, delta_t, delta_t)


def _spread_rows(a, d):
    return a.reshape(N_HEADS, a.shape[1] // d, d).transpose(2, 0, 1)


FF_TS = 512
FF_TC = 1024
FF_SUB = 256
HALO = 8
HALO_BF16 = 16
UP_BLOCKS = D_MODEL // FF_TC


def _conv3(ext, w, b):
    return b + w[0:1, :] * pltpu.roll(ext, 2, 0) + w[1:2, :] * pltpu.roll(ext, 1, 0) + w[2:3, :] * ext


def _ffn_specs(s, cols_first):
    nrb = s // FF_TS
    per, per16 = FF_TS // HALO, FF_TS // HALO_BF16

    def mk(block, fn):
        if cols_first:
            return pl.BlockSpec(block, lambda j, i: fn(i, j))
        return pl.BlockSpec(block, lambda i, j: fn(i, j))

    specs = types.SimpleNamespace(
        nrb=nrb, ncb=D_FF // FF_TC,
        row=mk((FF_TS, FF_TC), lambda i, j: (i, j)),
        before=mk((HALO, FF_TC), lambda i, j: (jnp.maximum(i * per - 1, 0), j)),
        after=mk((HALO, FF_TC), lambda i, j: (jnp.minimum((i + 1) * per, nrb * per - 1), j)),
        w=mk((3, FF_TC), lambda i, j: (0, j)),
        b=mk((1, FF_TC), lambda i, j: (0, j)),
        part=mk((HALO, FF_TC), lambda i, j: (i, j)),
        act=mk((FF_TS, D_MODEL), lambda i, j: (i, 0)),
        act_before=mk((HALO_BF16, D_MODEL), lambda i, j: (jnp.maximum(i * per16 - 1, 0), 0)),
        act_after=mk((HALO_BF16, D_MODEL), lambda i, j: (jnp.minimum((i + 1) * per16, nrb * per16 - 1), 0)),
        up_gate=mk((None, D_MODEL, FF_TC), lambda i, j: (j // UP_BLOCKS, 0, j % UP_BLOCKS)),
        up_val=mk((None, D_MODEL, FF_TC), lambda i, j: (N_DEV // 2 + j // UP_BLOCKS, 0, j % UP_BLOCKS)),
        down=mk((FF_TC, D_MODEL), lambda i, j: (j, 0)),
    )
    return specs


def _ffn_up_geglu(h, w_up, wg, wv, bg, bv, name):
    s = h.shape[0]
    sp = _ffn_specs(s, True)

    def body(h_ref, hb_ref, ugw_ref, uvw_ref, wg_ref, wv_ref, bg_ref, bv_ref, ug_ref, uv_ref, y_ref):
        keep = jnp.where(pl.program_id(1) > 0, 1.0, 0.0).astype(BF16)
        hext = jnp.concatenate([hb_ref[...] * keep, h_ref[...]], axis=0)
        eg = _dotg(hext, ugw_ref[...], 1, 0)
        ug_ref[...] = eg[HALO_BF16:, :]
        act = _gelu_tanh(_conv3(eg, wg_ref[...], bg_ref[...])[HALO_BF16:, :])
        ev = _dotg(hext, uvw_ref[...], 1, 0)
        uv_ref[...] = ev[HALO_BF16:, :]
        y_ref[...] = (act * _conv3(ev, wv_ref[...], bv_ref[...])[HALO_BF16:, :]).astype(y_ref.dtype)

    return pl.pallas_call(
        body, name=name, grid=(sp.ncb, sp.nrb),
        in_specs=[sp.act, sp.act_before, sp.up_gate, sp.up_val, sp.w, sp.w, sp.b, sp.b],
        out_specs=[sp.row, sp.row, sp.row],
        out_shape=[jax.ShapeDtypeStruct((s, D_FF), F32), jax.ShapeDtypeStruct((s, D_FF), F32),
                   jax.ShapeDtypeStruct((s, D_FF), BF16)],
        compiler_params=_params("parallel", "parallel"),
    )(h, h, w_up, w_up, wg, wv, bg, bv)


def _sum_parts(parts, name):
    n = parts.shape[0] // HALO

    def body(p_ref, o_ref):
        acc = p_ref[0:HALO, :]
        for t in range(1, n):
            acc = acc + p_ref[t * HALO:(t + 1) * HALO, :]
        o_ref[...] = acc

    return pl.pallas_call(body, name=name, out_shape=jax.ShapeDtypeStruct((HALO, parts.shape[1]), F32))(parts)


def _ffn_geglu_bwd(ug, uv, df, w_down, w_up, wg, wv, bg, bv, name):
    s = ug.shape[0]
    sp = _ffn_specs(s, False)
    nrb = sp.nrb
    rows = FF_TS + 2 * HALO
    lo, hi = HALO, HALO + FF_TS

    def body(ug_ref, uv_ref, hg_ref, hv_ref, ng_ref, nv_ref, df_ref, dfn_ref, dw_ref, ugw_ref, uvw_ref,
             wg_ref, wv_ref, bg_ref, bv_ref, dug_ref, duv_ref, dh_ref, dwg_ref, dwv_ref):
        i, j = pl.program_id(0), pl.program_id(1)
        keep_top = jnp.where(i > 0, 1.0, 0.0)
        keep_bot = jnp.where(i < nrb - 1, 1.0, 0.0).astype(BF16)
        dfe = jnp.concatenate([df_ref[...], dfn_ref[...] * keep_bot], axis=0)

        def back(dc, e, w, du_ref, sl):
            up1 = pltpu.roll(dc, rows - 1, 0)
            up2 = pltpu.roll(dc, rows - 2, 0)
            du = (w[2:3, :] * dc + w[1:2, :] * up1 + w[0:1, :] * up2)[lo:hi, :].astype(BF16)
            du_ref[:, sl] = du
            p1, p2 = up1 * e, up2 * e
            colsum = lambda p: jnp.sum(p[lo:hi, :], axis=0, keepdims=True)
            row = lambda p, t: p[t:t + 1, :]
            d_w1 = colsum(p1) + row(p1, lo - 1) - row(p1, hi - 1)
            d_w0 = colsum(p2) + row(p2, lo - 2) + row(p2, lo - 1) - row(p2, hi - 2) - row(p2, hi - 1)
            sums = [d_w0, d_w1, colsum(dc * e), colsum(dc), jnp.zeros((HALO - 4, FF_SUB), F32)]
            return du, jnp.concatenate(sums, axis=0)

        dh = None
        n_sub = FF_TC // FF_SUB
        cols = [slice(c * FF_SUB, (c + 1) * FF_SUB) for c in range(n_sub)]
        d_act = lambda c: _dotg(dfe, dw_ref[cols[c], :], 1, 1)[:FF_TS + HALO, :]
        ahead = d_act(0)
        for c in range(n_sub):
            sl, dy = cols[c], ahead
            if c + 1 < n_sub:
                ahead = d_act(c + 1)
            dye = jnp.concatenate([jnp.zeros((HALO, FF_SUB), F32), dy], axis=0)
            eg = jnp.concatenate([hg_ref[:, sl] * keep_top, ug_ref[:, sl], ng_ref[:, sl]], axis=0)
            ev = jnp.concatenate([hv_ref[:, sl] * keep_top, uv_ref[:, sl], nv_ref[:, sl]], axis=0)
            wg_, wv_ = wg_ref[:, sl], wv_ref[:, sl]
            gate = _conv3(eg, wg_, bg_ref[:, sl])
            val = _conv3(ev, wv_, bv_ref[:, sl])
            act, slope = _gelu_tanh_and_slope(gate)
            dug, dwg_ref[:, sl] = back((dye * val) * slope, eg, wg_, dug_ref, sl)
            duv, dwv_ref[:, sl] = back(dye * act, ev, wv_, duv_ref, sl)
            term = _dotg(dug, ugw_ref[:, sl], 1, 1) + _dotg(duv, uvw_ref[:, sl], 1, 1)
            dh = term if dh is None else dh + term

        @pl.when(j == 0)
        def _():
            dh_ref[...] = dh

        @pl.when(j > 0)
        def _():
            dh_ref[...] += dh

    parts = jax.ShapeDtypeStruct((nrb * HALO, D_FF), F32)
    dug, duv, dh, pg, pv = pl.pallas_call(
        body, name=name, grid=(nrb, sp.ncb),
        in_specs=[sp.row, sp.row, sp.before, sp.before, sp.after, sp.after, sp.act, sp.act_after, sp.down,
                  sp.up_gate, sp.up_val, sp.w, sp.w, sp.b, sp.b],
        out_specs=[sp.row, sp.row, sp.act, sp.part, sp.part],
        out_shape=[jax.ShapeDtypeStruct((s, D_FF), BF16), jax.ShapeDtypeStruct((s, D_FF), BF16),
                   jax.ShapeDtypeStruct((s, D_MODEL), F32), parts, parts],
        compiler_params=_params("parallel", "arbitrary"),
    )(ug, uv, ug, uv, ug, uv, df, df, w_down, w_up, w_up, wg, wv, bg, bv)
    return dug, duv, dh, _sum_parts(pg, name=name + "_sum_gate"), _sum_parts(pv, name=name + "_sum_val")


def _rope_tables(s):
    inv = ROPE_THETA ** (-jnp.arange(0, ROT_DIM, 2, dtype=F32) / ROT_DIM)
    ang = jnp.arange(s, dtype=F32)[:, None] * inv[None, :]
    cos8, sin8 = jnp.cos(ang), jnp.sin(ang)
    rest = HEAD_DIM - ROT_DIM
    cos_h = jnp.concatenate([cos8, cos8, jnp.ones((s, rest), F32)], axis=1)
    sin_h = jnp.concatenate([sin8, sin8, jnp.zeros((s, rest), F32)], axis=1)
    return jnp.tile(cos_h, (1, LANES // HEAD_DIM)), jnp.tile(sin_h, (1, LANES // HEAD_DIM))


def _all_heads(table):
    return jnp.concatenate([table] * (B_WIDTH // LANES), axis=1)


def _layer_fwd(x, w, cos, sin, last):
    sv = types.SimpleNamespace(x=x)
    (sv.h1,), _ = _rowwise(lambda xb, g: ((_rms(xb, g),), ()), [_full(x)], [w.g_pre], [(D_MODEL, BF16)], [],
                           ts=512, name="pre_mix_norm")
    sv.proj = _matmul(sv.h1, w.big("w_in", sv.h1), mode="nt", out_dtype=F32, name="proj")

    gate_consts = [w.vg, w.vb, *w.ws, w.bfull, w.ga]
    def gate_fwd_fn(zu, zv, *consts):
        chunks = [_gate_fn(zu[r], zv[r], *consts) for r in _chunk_rows(zu.shape[0])]
        return (jnp.concatenate(chunks, axis=0),), ()

    (na,), _ = _rowwise(gate_fwd_fn, [(sv.proj, A_WIDTH, 0), (sv.proj, A_WIDTH, 1)], gate_consts,
                        [(A_WIDTH, BF16)], [], ts=GATE_TS, name="gate_fwd")

    def rope_fn(qr, kr, vr, cs, sn):
        cs, sn = _all_heads(cs), _all_heads(sn)
        return (qr * cs + _rot_half(qr) * sn, kr * cs + _rot_half(kr) * sn, vr), ()

    def rope_all(qr, kr, vr, cs, sn):
        return rope_fn(qr, kr, vr, cs, sn)[0] * len(DILATIONS), ()

    qkv, _ = _rowwise(
        rope_all, [(sv.proj, B_WIDTH, 2), (sv.proj, B_WIDTH, 3), (sv.proj, B_WIDTH, 4), _full(cos), _full(sin)], [],
        [(B_WIDTH, BF16, d) for d in DILATIONS for _ in range(3)], [], ts=512, name="rope_fwd")
    sv.qkv = {d: qkv[3 * i:3 * i + 3] for i, d in enumerate(DILATIONS)}

    branch = []
    for d in DILATIONS:
        o, l = _attn_fwd(*sv.qkv[d], d, name=f"attn_fwd_d{d}")
        branch += [_full(o, d), _full(l, d)]

    def combine_fn(o1, l1, o2, l2, o3, l3, nab, gb):
        m = jnp.maximum(jnp.maximum(l1, l2), l3)
        e1, e2, e3 = jnp.exp(l1 - m), jnp.exp(l2 - m), jnp.exp(l3 - m)
        den = e1 + e2 + e3
        ob = (e1 / den) * o1 + (e2 / den) * o2 + (e3 / den) * o3
        mixed = jnp.concatenate([nab, _rms(ob, gb).astype(BF16)], axis=1)
        lse = _head_cols(m + jnp.log(den))
        return (mixed, ob, lse) + (lse,) * len(DILATIONS), ()

    (sv.mixed, sv.ob, sv.lse_rows, *lses), _ = _rowwise(
        combine_fn, branch + [_full(na)], [w.gb],
        [(D_MODEL, BF16), (B_WIDTH, F32), (LANES, F32, HEAD_ROWS)] + [(LANES, F32, d) for d in DILATIONS], [],
        ts=512, name="combine")
    sv.lse = dict(zip(DILATIONS, lses))
    sv.y = _matmul(sv.mixed, w.big("w_out", sv.mixed), mode="nn", out_dtype=F32, name="mix_out")

    def mid_fn(xb, yb, g1, g2):
        x1 = xb + _rms(yb, g1)
        return (x1, _rms(x1, g2)), ()

    (sv.x1, sv.h2), _ = _rowwise(mid_fn, [_full(x), _full(sv.y)], [w.g_pm, w.g_pf], [(D_MODEL, F32), (D_MODEL, BF16)], [],
                                 ts=512, name="post_mix_norm")
    conv_w = w.big("conv_w", sv.h2)
    sv.ug, sv.uv, sv.yff = _ffn_up_geglu(sv.h2, w.big("w_up", sv.h2), conv_w[:, :D_FF], conv_w[:, D_FF:],
                                         w.cb_g, w.cb_v, name="ffn_up_geglu")
    sv.f = _matmul(sv.yff, w.big("w_down", sv.yff), mode="nn", out_dtype=F32, name="ffn_down")
    if last:
        return None, sv
    (x2,), _ = _rowwise(lambda xb, fb, g: ((xb + _rms(fb, g),), ()), [_full(sv.x1), _full(sv.f)], [w.g_post],
                        [(D_MODEL, F32)], [], ts=512, name="post_ffn_norm")
    return x2, sv


def _layer_bwd(dx2, sv, w, cos, sin, emit):
    g = {}

    def post_fn(fb, dxb, gp):
        _, vjp = jax.vjp(_rms, fb, gp)
        df, dg = vjp(dxb)
        return (df,), (dg,)

    (df,), (g["post_ffn_norm"],) = _rowwise(post_fn, [_full(sv.f), _full(dx2)], [w.g_post], [(D_MODEL, BF16)],
                                            [(1, D_MODEL)], ts=512, name="post_ffn_norm_bwd")
    big = {"w_down": _matmul(sv.yff, df, mode="tn", out_dtype=BF16, name="ffn_down_dw").reshape(N_DEV, -1, D_MODEL)}
    conv_w = w.big("conv_w", df)
    dug, duv, dh2, dwg, dwv = _ffn_geglu_bwd(sv.ug, sv.uv, df, w.big("w_down", df), w.big("w_up", df),
                                             conv_w[:, :D_FF], conv_w[:, D_FF:], w.cb_g, w.cb_v, name="ffn_geglu_bwd")
    big["conv_w"] = jnp.concatenate([dwg[0:3], dwv[0:3]], axis=1).reshape(3, N_DEV, D_MODEL).transpose(1, 0, 2)
    g["conv_b"] = jnp.concatenate([dwg[3], dwv[3]], axis=0)
    big["w_up"] = _matmul_by_destination(sv.h2, dug, duv, name="ffn_up_dw")
    g_pm = w.g_pm + emit(big)

    def mid_fn(x1b, yb, dhb, dxb, g1, g2):
        _, vjp2 = jax.vjp(_rms, x1b, g2)
        dx1h, dg2 = vjp2(dhb)
        dx1 = dxb + dx1h
        _, vjp1 = jax.vjp(_rms, yb, g1)
        dy, dg1 = vjp1(dx1)
        return (dx1, dy), (dg1, dg2)

    (dx1, dy), (g["post_mix_norm"], g["pre_ffn_norm"]) = _rowwise(
        mid_fn, [_full(sv.x1), _full(sv.y), _full(dh2), _full(dx2)], [g_pm, w.g_pf],
        [(D_MODEL, F32), (D_MODEL, BF16)], [(1, D_MODEL), (1, D_MODEL)], ts=256, name="post_mix_norm_bwd")
    dmixed = _matmul(dy, w.big("w_out", dy), mode="nt", out_dtype=F32, name="mix_out_dx")
    dw_out = _matmul(sv.mixed, dy, mode="tn", out_dtype=BF16, name="mix_out_dw").reshape(N_DEV, -1, D_MODEL)
    g_b = w.gb + emit({"w_out": dw_out})

    def attn_out_fn(obb, dmb, gb):
        _, vjp = jax.vjp(_rms, obb, gb)
        do, dgb = vjp(dmb)
        delta = _head_cols(_head_sum(do * obb))
        return (delta,) + (do,) * len(DILATIONS) + (delta,) * len(DILATIONS), (dgb,)

    (delta_rows, *outs), (g["out_norm_b"],) = _rowwise(
        attn_out_fn, [_full(sv.ob), (dmixed, B_WIDTH, 1)], [g_b],
        [(LANES, F32, HEAD_ROWS)] + [(B_WIDTH, BF16, d) for d in DILATIONS] + [(LANES, F32, d) for d in DILATIONS],
        [(1, B_WIDTH)], ts=512, name="attn_out_bwd")
    do = dict(zip(DILATIONS, outs[:len(DILATIONS)]))
    delta = dict(zip(DILATIONS, outs[len(DILATIONS):]))
    parts = {"q": [], "k": [], "v": []}
    for d in DILATIONS:
        qv, kv, vv = sv.qkv[d]
        dq = _attn_bwd_q(qv, kv, vv, do[d], sv.lse[d], delta[d], d, name=f"attn_bwd_q_d{d}")
        dk, dv = _attn_bwd_kv(qv, kv, vv, do[d], _spread_rows(sv.lse_rows, d), _spread_rows(delta_rows, d), d,
                              name=f"attn_bwd_kv_d{d}")
        parts["q"].append(_full(dq, d))
        parts["k"].append(_full(dk, d))
        parts["v"].append(_full(dv, d))

    def rope_bwd_fn(q1, q2, q3, k1, k2, k3, v1, v2, v3, cs, sn):
        cs, sn = _all_heads(cs), _all_heads(sn)

        def back(t):
            return t * cs - _rot_half(t * sn)
        return (jnp.concatenate([back(q1 + q2 + q3), back(k1 + k2 + k3), v1 + v2 + v3], axis=1),), ()

    (dzb,), _ = _rowwise(rope_bwd_fn, parts["q"] + parts["k"] + parts["v"] + [_full(cos), _full(sin)], [],
                         [(3 * B_WIDTH, BF16)], [], ts=256, name="rope_bwd")

    gate_consts = [w.vg, w.vb, *w.ws, w.bfull, w.ga]

    def gate_bwd_fn(zu, zv, dna, *consts):
        dz, sums = [], None
        for r in _chunk_rows(zu.shape[0]):
            _, vjp = jax.vjp(_gate_fn, zu[r], zv[r], *consts)
            grads = vjp(dna[r])
            dz.append(jnp.concatenate([grads[0], grads[1]], axis=1))
            sums = grads[2:] if sums is None else tuple(a + b for a, b in zip(sums, grads[2:]))
        return (jnp.concatenate(dz, axis=0),), tuple(sums)

    (dza,), gsmall = _rowwise(
        gate_bwd_fn, [(sv.proj, A_WIDTH, 0), (sv.proj, A_WIDTH, 1), (dmixed, A_WIDTH, 0)], gate_consts,
        [(2 * A_WIDTH, BF16)], [c.shape for c in gate_consts], ts=GATE_TS, name="gate_bwd")
    g["v_norm_g"], g["v_norm_b"] = gsmall[0], gsmall[1]
    g["w_spatial"] = jnp.stack(gsmall[2:6])
    g["b_spatial"] = _bias_reduce(gsmall[6], name="bias_reduce")[:, :A_GROUPS].T
    g["out_norm_a"] = gsmall[7]

    dproj = jnp.concatenate([dza, dzb], axis=1)
    dh1 = _matmul(dproj, w.big("w_in", dproj), mode="nn", out_dtype=F32, name="proj_dx")
    dw_in = _matmul(dproj, sv.h1, mode="tn", out_dtype=BF16, name="proj_dw").reshape(N_DEV, -1, D_MODEL)
    g_pre = w.g_pre + emit({"w_in": dw_in})

    def pre_fn(xb, dhb, dxb, gp):
        _, vjp = jax.vjp(_rms, xb, gp)
        dxh, dg = vjp(dhb)
        return (dxb + dxh,), (dg,)

    (dx,), (g["pre_mix_norm"],) = _rowwise(pre_fn, [_full(sv.x), _full(dh1), _full(dx1)], [g_pre], [(D_MODEL, F32)],
                                           [(1, D_MODEL)], ts=512, name="pre_mix_norm_bwd")
    return dx, g


def _layer_weights(l, full, small):
    row = lambda a: a[l].reshape(1, -1)
    return types.SimpleNamespace(
        big=functools.partial(full, l),
        g_pre=row(small["pre_mix_norm"]), vg=row(small["v_norm_g"]), vb=row(small["v_norm_b"]),
        ws=[small["w_spatial"][l, gi] for gi in range(A_GROUPS)],
        bfull=jnp.repeat(small["b_spatial"][l].T, CHUNK, axis=1),
        ga=row(small["out_norm_a"]), gb=row(small["out_norm_b"]),
        g_pm=row(small["post_mix_norm"]), g_pf=row(small["pre_ffn_norm"]),
        cb_g=small["conv_b"][l][:D_FF].reshape(1, -1), cb_v=small["conv_b"][l][D_FF:].reshape(1, -1),
        g_post=row(small["post_ffn_norm"]))


def _local_step(x, target, full, small, emit, emit_small, started):
    s = x.shape[0]
    cos, sin = _rope_tables(s)
    ws = [_layer_weights(l, full, small) for l in range(N_LAYERS)]
    ws[0].g_pre = ws[0].g_pre + started
    saved = []
    h = x
    for l in range(N_LAYERS):
        h, sv = _layer_fwd(h, ws[l], cos, sin, last=l == N_LAYERS - 1)
        saved.append(sv)

    def loss_fn(xb, fb, tb, g):
        diff = (xb + _rms(fb, g)) - tb
        return (diff * (1.0 / D_MODEL),), (jnp.sum(diff * diff, axis=0, keepdims=True),)

    (dh,), (sq,) = _rowwise(loss_fn, [_full(saved[-1].x1), _full(saved[-1].f), _full(target)], [ws[-1].g_post],
                            [(D_MODEL, F32)], [(1, D_MODEL)], ts=512, name="loss")
    loss = 0.5 * jnp.sum(sq) * (1.0 / D_MODEL)
    grads = [None] * N_LAYERS
    for l in reversed(range(N_LAYERS)):
        dh, grads[l] = _layer_bwd(dh, saved[l], ws[l], cos, sin, functools.partial(emit, l))
        token = emit_small(l, grads[l])
        if l > 0:
            ws[l - 1].g_post = ws[l - 1].g_post + token
    return loss, dh, grads


def _place():
    return lax.axis_index("x"), lax.axis_index("y"), lax.axis_index("c")


FLIPS = ((1, 0, 0), (0, 1, 0), (1, 1, 0), (0, 0, 1), (1, 0, 1), (0, 1, 1), (1, 1, 1))
HBM_SPEC = pl.BlockSpec(memory_space=pltpu.HBM)
SEM_SPEC = pl.BlockSpec(memory_space=pltpu.SEMAPHORE)
SPLIT_COPY = pltpu.CompilerParams(has_side_effects=pltpu.SideEffectType.DATAFLOW_SIDE_EFFECTING)


def _peers():
    mx, my, mc = _place()
    out = []
    for fx, fy, fc in FLIPS:
        px, py, pc = (1 - mx if fx else mx), (1 - my if fy else my), (1 - mc if fc else mc)
        out.append(((px, py, pc), 4 * px + 2 * py + pc))
    return out


def _flat_copies(scatter, src_refs, land_refs, send_sems, recv_sems):
    mx, my, mc = _place()
    me = 4 * mx + 2 * my + mc
    n = len(src_refs)
    copies = []
    for t in range(n):
        for i, (peer, number) in enumerate(_peers()):
            copies.append(pltpu.make_async_remote_copy(
                src_ref=src_refs[t].at[number] if scatter else src_refs[t],
                dst_ref=land_refs[t].at[i] if scatter else land_refs[t].at[me],
                send_sem=send_sems.at[t * len(FLIPS) + i], recv_sem=recv_sems.at[t * len(FLIPS) + i],
                device_id=peer, device_id_type=MESH_ID))
    return copies


def _flat_start(arrays, scatter, name):
    n = len(arrays)
    slots = len(FLIPS) if scatter else N_DEV
    lands = [lax.empty((slots,) + (a.shape[1:] if scatter else a.shape), a.dtype) for a in arrays]

    def body(*refs):
        src, land, (send_sems, recv_sems), token = refs[:n], refs[n:2 * n], refs[2 * n:2 * n + 2], refs[-1]
        for cp in _flat_copies(scatter, src, land, send_sems, recv_sems):
            cp.start()
        token[...] = jnp.zeros_like(token)

    hbm = [pltpu.HBM(a.shape, a.dtype) for a in arrays] + [pltpu.HBM(a.shape, a.dtype) for a in lands]
    sems = pltpu.SemaphoreType.DMA((n * len(FLIPS),))
    outs = pl.pallas_call(
        body, name=name, out_shape=(sems, sems, *hbm, jax.ShapeDtypeStruct((8, 128), F32)),
        in_specs=[HBM_SPEC] * (2 * n),
        out_specs=(SEM_SPEC, SEM_SPEC, *([HBM_SPEC] * (2 * n)), pl.BlockSpec(memory_space=pltpu.VMEM)),
        input_output_aliases={t: 2 + t for t in range(2 * n)}, compiler_params=SPLIT_COPY,
    )(*[pltpu.with_memory_space_constraint(a, pltpu.HBM) for a in (*arrays, *lands)])
    return types.SimpleNamespace(sems=outs[:2], thru=outs[2:2 + 2 * n], scatter=scatter, n=n), outs[-1][0:1, 0:1]


def _flat_wait(handle, after, name):
    n = handle.n

    def body(*refs):
        src, land, (send_sems, recv_sems) = refs[:n], refs[n:2 * n], refs[2 * n:2 * n + 2]
        for cp in _flat_copies(handle.scatter, src, land, send_sems, recv_sems):
            cp.wait_send()
            cp.wait_recv()

    outs = pl.pallas_call(
        body, name=name, out_shape=tuple(pltpu.HBM(a.shape, a.dtype) for a in handle.thru),
        in_specs=[HBM_SPEC] * (2 * n) + [SEM_SPEC, SEM_SPEC, ANY], out_specs=tuple([HBM_SPEC] * (2 * n)),
        input_output_aliases={t: t for t in range(2 * n)}, compiler_params=SPLIT_COPY,
    )(*handle.thru, *handle.sems, after)
    return outs[:n], outs[n:]


def _adamw(w, g, m, v):
    m2 = ADAM_B1 * m + (1.0 - ADAM_B1) * g
    v2 = ADAM_B2 * v + (1.0 - ADAM_B2) * (g * g)
    m_hat = m2 / (1.0 - ADAM_B1 ** ADAM_STEP)
    v_hat = v2 / (1.0 - ADAM_B2 ** ADAM_STEP)
    return -ADAM_LR * (m_hat / (jnp.sqrt(v_hat) + ADAM_EPS) + ADAM_WD * w), m2, v2


def _adamw_sharded(me, mine, landed, w, m, v, tr, name):
    _, r, c = w.shape
    nt = r // tr
    assert r % tr == 0 and len(mine) == len(landed) == N_LAYERS == 2, name
    per_layer = 1 + len(FLIPS)

    def body(me_ref, *refs):
        terms, (w_ref, m_ref, v_ref), outs = refs[:2 * per_layer], refs[2 * per_layer:2 * per_layer + 3], refs[-4:]
        layer = pl.program_id(0)

        def total(group):
            g = group[0][0].astype(F32)
            for t in group[1:]:
                g = g + t[0].astype(F32)
            return g

        g = jnp.where(layer == 0, total(terms[:per_layer]), total(terms[per_layer:]))
        d, m2, v2 = _adamw(w_ref[0], g, m_ref[0], v_ref[0])
        for o, val in zip(outs, (g, d, m2, v2)):
            o[0] = val

    def held(l):
        return lambda layer, i: jnp.where(layer == l, i, nt - 1 if l == 0 else 0)

    in_specs = []
    for l in range(N_LAYERS):
        rows = held(l)
        in_specs.append(pl.BlockSpec((1, tr, c), functools.partial(lambda layer, i, me_ref, rows: (me_ref[0], rows(layer, i), 0), rows=rows)))
        for k in range(len(FLIPS)):
            in_specs.append(pl.BlockSpec(
                (1, tr, c), functools.partial(lambda layer, i, me_ref, rows, k: (k, rows(layer, i), 0), rows=rows, k=k)))
    tile = pl.BlockSpec((1, tr, c), lambda layer, i, me_ref: (layer, i, 0))
    operands = []
    for l in range(N_LAYERS):
        operands += [mine[l]] + [landed[l]] * len(FLIPS)
    return pl.pallas_call(
        body, name=name, out_shape=[jax.ShapeDtypeStruct(w.shape, F32)] * 4,
        grid_spec=pltpu.PrefetchScalarGridSpec(
            num_scalar_prefetch=1, grid=(N_LAYERS, nt), in_specs=in_specs + [tile] * 3, out_specs=[tile] * 4),
        compiler_params=_params("arbitrary", "arbitrary"),
    )(me, *operands, w, m, v)


def _adamw_replicated(parts, w, m, v, name):
    def body(p_ref, w_ref, m_ref, v_ref, g_ref, d_ref, m2_ref, v2_ref):
        g = p_ref[0]
        for j in range(1, N_DEV):
            g = g + p_ref[j]
        d, m2, v2 = _adamw(w_ref[...], g, m_ref[...], v_ref[...])
        g_ref[...], d_ref[...], m2_ref[...], v2_ref[...] = g, d, m2, v2

    return pl.pallas_call(body, name=name, out_shape=[jax.ShapeDtypeStruct(w.shape, F32)] * 4,
                          compiler_params=pltpu.CompilerParams(vmem_limit_bytes=VMEM_LIMIT_BYTES))(parts, w, m, v)


def _pack_small(vals):
    flat = jnp.concatenate([vals[n].reshape(-1) for n in SMALL_NAMES])
    return jnp.concatenate([flat, jnp.zeros((SMALL_ROWS * D_MODEL - flat.shape[0],), F32)]).reshape(SMALL_ROWS, D_MODEL)


def _unpack_small(packed, shapes):
    flat, out, at = packed.reshape(-1), {}, 0
    for n in SMALL_NAMES:
        size = math.prod(shapes[n])
        out[n] = flat[at:at + size].reshape(shapes[n])
        at += size
    return out


GATHER_GROUPS = ((0, ("w_in",)), (0, ("w_out", "w_up", "conv_w")), (0, ("w_down",)),
                 (1, ("w_in",)), (1, ("w_out", "w_up", "conv_w")), (1, ("w_down",)))
ADAMW_TILE_ROWS = {"w_in": 320, "w_out": 128, "w_up": 256, "w_down": 256, "conv_w": 3}


def _assemble(name, land):
    if name == "w_in":
        return land.reshape(IN_COLS, D_MODEL)
    if name == "conv_w":
        return land.transpose(1, 0, 2).reshape(3, 2 * D_FF)
    if name == "w_up":
        return land
    return land.reshape(-1, D_MODEL)


def _start_gathers(wts, me):
    started, groups = jnp.zeros((1, 1), F32), []
    for gi, (l, names) in enumerate(GATHER_GROUPS):
        local = {"conv_w": lambda a: a, "w_in": lambda a: a.T.astype(BF16)}
        blocks = [local.get(n, lambda a: a.astype(BF16))(wts[n][l]) for n in names]
        handle, token = _flat_start(blocks, False, name=f"gather_start_{gi}")
        groups.append(types.SimpleNamespace(layer=l, names=names, blocks=blocks, handle=handle, got=None, index=gi))
        started = started + token

    def fetch(l, name, after):
        grp = next(gr for gr in groups if gr.layer == l and name in gr.names)
        if grp.got is None:
            lands = _flat_wait(grp.handle, after, name=f"gather_wait_{grp.index}")[1]
            grp.got = {}
            for n, blk, land in zip(grp.names, grp.blocks, lands):
                own = (me,) + (0,) * blk.ndim
                grp.got[n] = _assemble(n, lax.dynamic_update_slice(land, blk[None], own))
        return grp.got[name]

    return fetch, started


def kernel(x, pre_mix_norm, w_in, v_norm_g, v_norm_b, w_spatial, b_spatial, out_norm_a, out_norm_b, w_out, post_mix_norm, pre_ffn_norm, w_up, conv_w, conv_b, w_down, post_ffn_norm, loss_target, m_pre_mix_norm, m_w_in, m_v_norm_g, m_v_norm_b, m_w_spatial, m_b_spatial, m_out_norm_a, m_out_norm_b, m_w_out, m_post_mix_norm, m_pre_ffn_norm, m_w_up, m_conv_w, m_conv_b, m_w_down, m_post_ffn_norm, v_pre_mix_norm, v_w_in, v_v_norm_g, v_v_norm_b, v_w_spatial, v_b_spatial, v_out_norm_a, v_out_norm_b, v_w_out, v_post_mix_norm, v_pre_ffn_norm, v_w_up, v_conv_w, v_conv_b, v_w_down, v_post_ffn_norm):
    wts = dict(zip(WEIGHT_NAMES, (pre_mix_norm, w_in, v_norm_g, v_norm_b, w_spatial, b_spatial, out_norm_a, out_norm_b,
                                  w_out, post_mix_norm, pre_ffn_norm, w_up, conv_w, conv_b, w_down, post_ffn_norm)))
    mom1 = dict(zip(WEIGHT_NAMES, (m_pre_mix_norm, m_w_in, m_v_norm_g, m_v_norm_b, m_w_spatial, m_b_spatial, m_out_norm_a,
                                   m_out_norm_b, m_w_out, m_post_mix_norm, m_pre_ffn_norm, m_w_up, m_conv_w, m_conv_b,
                                   m_w_down, m_post_ffn_norm)))
    mom2 = dict(zip(WEIGHT_NAMES, (v_pre_mix_norm, v_w_in, v_v_norm_g, v_v_norm_b, v_w_spatial, v_b_spatial, v_out_norm_a,
                                   v_out_norm_b, v_w_out, v_post_mix_norm, v_pre_ffn_norm, v_w_up, v_conv_w, v_conv_b,
                                   v_w_down, v_post_ffn_norm)))
    mx, my, mc = _place()
    me = 4 * mx + 2 * my + mc

    fetch, started = _start_gathers(wts, me)
    scatters = []

    def emit(l, blocks):
        names = tuple(blocks)
        handle, token = _flat_start([blocks[n] for n in names], True, name=f"scatter_start_{l}_{'_'.join(names)}")
        scatters.append((l, names, handle))
        return token

    smalls = {}

    def emit_small(l, g):
        pack = _pack_small({n: g[n] for n in SMALL_NAMES})
        handle, token = _flat_start([pack], False, name=f"small_grads_start_{l}")
        smalls[l] = (pack, handle)
        return token

    loss_local, dx, _ = _local_step(x[0], loss_target[0], fetch, wts, emit, emit_small, started)
    loss = lax.psum(loss_local, AXES)

    me_arr = jnp.reshape(me, (1,)).astype(jnp.int32)
    big_out = [{}, {}, {}, {}]

    def finish(group, after):
        mine, landed = {}, {}
        for l, names, handle in scatters:
            if names == group:
                sent, lands = _flat_wait(handle, after, name=f"scatter_wait_{l}_{'_'.join(names)}")
                for n, a, b in zip(names, sent, lands):
                    mine[l, n], landed[l, n] = a, b
        for n in group:
            flip = (lambda a: a.transpose(0, 2, 1)) if n == "w_in" else (lambda a: a)
            res = _adamw_sharded(me_arr, [mine[l, n] for l in range(N_LAYERS)], [landed[l, n] for l in range(N_LAYERS)],
                                 flip(wts[n]), flip(mom1[n]), flip(mom2[n]), ADAMW_TILE_ROWS[n], name=f"adamw_{n}")
            for kind in range(4):
                big_out[kind][n] = flip(res[kind])
        return res[0]

    done = smalls[0][1].thru[0]
    for group in [names for l, names, _ in scatters if l == 0][:-1]:
        done = finish(group, done)
    small_shapes = {n: wts[n].shape[1:] for n in SMALL_NAMES}
    per_layer = {}
    for l in reversed(range(N_LAYERS)):
        pack, handle = smalls[l]
        (landed,) = _flat_wait(handle, done, name=f"small_grads_wait_{l}")[1]
        everyone = lax.dynamic_update_slice(landed, pack[None], (me, 0, 0))
        packs = [_pack_small({n: t[n][l] for n in SMALL_NAMES}) for t in (wts, mom1, mom2)]
        res = _adamw_replicated(everyone, *packs, name=f"adamw_replicated_{l}")
        per_layer[l] = [_unpack_small(o, small_shapes) for o in res]
        done = res[0]
    finish(scatters[-1][1], done)
    small_out = [{n: jnp.stack([per_layer[l][kind][n] for l in range(N_LAYERS)]) for n in SMALL_NAMES}
                 for kind in range(4)]

    outs = [loss, dx[None]]
    for kind in range(4):
        outs += [big_out[kind][n] if n in BIG_NAMES else small_out[kind][n] for n in WEIGHT_NAMES]
    return tuple(outs)
```

```python
import functools
import math
import types

import jax
import jax.numpy as jnp
from jax import lax
from jax.experimental import pallas as pl
from jax.experimental.pallas import tpu as pltpu

F32 = jnp.float32
BF16 = jnp.bfloat16

D_MODEL = 1024
A_WIDTH = 512
A_GROUPS = 4
CHUNK = 128
B_WIDTH = 512
HEAD_DIM = 64
N_HEADS = B_WIDTH // HEAD_DIM
ROT_DIM = 16
ROPE_THETA = 500000.0
BAND = 128
DILATIONS = (1, 4, 16)
IN_COLS = 2560
D_FF = 4096
EPS = 1e-6
NEG_INF = -1e30
N_DEV = 8
N_LAYERS = 2

ADAM_LR = 0.001
ADAM_B1 = 0.9
ADAM_B2 = 0.999
ADAM_EPS = 1e-08
ADAM_WD = 0.01
ADAM_STEP = 10

VMEM_LIMIT_BYTES = 56 * 1024 * 1024
MESH_ID = pl.DeviceIdType.MESH
ANY = pl.BlockSpec(memory_space=pl.ANY)

WEIGHT_NAMES = ("pre_mix_norm", "w_in", "v_norm_g", "v_norm_b", "w_spatial", "b_spatial", "out_norm_a", "out_norm_b",
                "w_out", "post_mix_norm", "pre_ffn_norm", "w_up", "conv_w", "conv_b", "w_down", "post_ffn_norm")
BIG_NAMES = ("w_in", "w_out", "w_up", "w_down", "conv_w")
SMALL_NAMES = tuple(n for n in WEIGHT_NAMES if n not in BIG_NAMES)

SMALL_ROWS = 80


def _params(*sem):
    return pltpu.CompilerParams(dimension_semantics=sem, vmem_limit_bytes=VMEM_LIMIT_BYTES)


def _dotg(a, b, ca, cb):
    return lax.dot_general(a.astype(BF16), b.astype(BF16), (((ca,), (cb,)), ((), ())), preferred_element_type=F32)


@jax.custom_vjp
def _bdot(a, b):
    return _dotg(a, b, 1, 0)


def _bdot_fwd(a, b):
    return _dotg(a, b, 1, 0), (a, b)


def _bdot_bwd(res, g):
    a, b = res
    return _dotg(g, b, 1, 1), _dotg(a, g, 0, 0)


_bdot.defvjp(_bdot_fwd, _bdot_bwd)


def _rms(x, g):
    return x * lax.rsqrt(jnp.mean(x * x, axis=-1, keepdims=True) + EPS) * g


def _layernorm(x, g, b):
    mu = jnp.mean(x, axis=-1, keepdims=True)
    xc = x - mu
    return xc * lax.rsqrt(jnp.mean(xc * xc, axis=-1, keepdims=True) + EPS) * g + b


def _gelu_erf(x):
    return x * (lax.erf(x * (1.0 / math.sqrt(2.0))) + 1.0) * 0.5


def _gelu_tanh(x):
    c = math.sqrt(2.0 / math.pi)
    return 0.5 * x * (1.0 + jnp.tanh(c * (x + 0.044715 * (x * x * x))))


def _gelu_tanh_and_slope(x):
    c, k = math.sqrt(2.0 / math.pi), 0.044715
    x2 = x * x
    t = jnp.tanh(c * (x + k * (x2 * x)))
    half_x, one_t = 0.5 * x, 1.0 + t
    return half_x * one_t, 0.5 * one_t + (half_x * (1.0 - t * t)) * (c + (3.0 * k * c) * x2)


def _rot_half(x):
    width = x.shape[1]
    lane = lax.broadcasted_iota(jnp.int32, x.shape, 1) % HEAD_DIM
    back = pltpu.roll(x, ROT_DIM // 2, 1)
    fwd = pltpu.roll(x, width - ROT_DIM // 2, 1)
    return jnp.where(lane < ROT_DIM // 2, -fwd, jnp.where(lane < ROT_DIM, back, 0.0))


def _split3(z):
    h0 = z.astype(BF16)
    r1 = z - h0.astype(F32)
    h1 = r1.astype(BF16)
    h2 = (r1 - h1.astype(F32)).astype(BF16)
    return h0, h1, h2


MATMUL_VMEM_BUDGET = 40 * 1024 * 1024


def _matmul_tiles(m, n, k, out_bytes):
    tn = n if n <= 1024 else (1280 if n % 1280 == 0 and n % 1024 else 1024)
    tk = k if k <= 1024 else (1280 if k % 1280 == 0 and k % 1024 else 1024)
    tm = m
    while tm > 256:
        blocks = 2 * 2 * (tm * tk + tk * tn) + 2 * out_bytes * tm * tn + (4 * tm * tn if k > tk else 0)
        if blocks <= MATMUL_VMEM_BUDGET and m % tm == 0:
            break
        tm //= 2
    return tm, tn, tk


def _matmul(a, b, *, mode, out_dtype, name, cols=None):
    wide = D_MODEL if cols is not None else None
    if mode == "nn":
        (m, k), (_, n) = a.shape, (b.shape if cols is None else (b.shape[1], cols[1] * wide))
    elif mode == "nt":
        (m, k), (n, _) = a.shape, (b.shape if cols is None else (b.shape[1], cols[1] * wide))
    else:
        (k, m), (_, n) = a.shape, b.shape
    tm, tn, tk = _matmul_tiles(m, n, k, jnp.dtype(out_dtype).itemsize)
    assert m % tm == 0 and n % tn == 0 and k % tk == 0, (name, m, n, k)
    nk = k // tk
    if mode == "nn":
        a_spec = pl.BlockSpec((tm, tk), lambda i, j, kk: (i, kk))
        b_spec = pl.BlockSpec((tk, tn), lambda i, j, kk: (kk, j))
        if cols is not None:
            assert tn == wide
            b_spec = pl.BlockSpec((None, tk, tn), lambda i, j, kk: (cols[0] + j, kk, 0))
        ca, cb = 1, 0
    elif mode == "nt":
        a_spec = pl.BlockSpec((tm, tk), lambda i, j, kk: (i, kk))
        b_spec = pl.BlockSpec((tn, tk), lambda i, j, kk: (j, kk))
        if cols is not None:
            assert tk == wide
            b_spec = pl.BlockSpec((None, tn, tk), lambda i, j, kk: (cols[0] + kk, j, 0))
        ca, cb = 1, 1
    else:
        a_spec = pl.BlockSpec((tk, tm), lambda i, j, kk: (kk, i))
        b_spec = pl.BlockSpec((tk, tn), lambda i, j, kk: (kk, j))
        ca, cb = 0, 0

    def body(a_ref, b_ref, o_ref, *acc):
        kk = pl.program_id(2)
        part = lax.dot_general(a_ref[...], b_ref[...], (((ca,), (cb,)), ((), ())), preferred_element_type=F32)
        if nk == 1:
            o_ref[...] = part.astype(o_ref.dtype)
            return
        acc_ref, = acc

        @pl.when(kk == 0)
        def _():
            acc_ref[...] = part

        @pl.when(kk > 0)
        def _():
            acc_ref[...] += part

        @pl.when(kk == nk - 1)
        def _():
            o_ref[...] = acc_ref[...].astype(o_ref.dtype)

    return pl.pallas_call(
        body, name=name, grid=(m // tm, n // tn, nk),
        in_specs=[a_spec, b_spec], out_specs=pl.BlockSpec((tm, tn), lambda i, j, kk: (i, j)),
        out_shape=jax.ShapeDtypeStruct((m, n), out_dtype),
        scratch_shapes=[pltpu.VMEM((tm, tn), F32)] if nk > 1 else [],
        compiler_params=_params("parallel", "parallel", "arbitrary"),
    )(a, b)


def _matmul_by_destination(a, b_lo, b_hi, *, name, tm=1024, tk=2048):
    (k, m), half = a.shape, N_DEV // 2
    assert b_lo.shape == b_hi.shape == (k, half * D_MODEL) and m % tm == 0 and k % tk == 0, name
    nk = k // tk

    def body(a_ref, lo_ref, hi_ref, o_ref, acc_ref):
        j, kk = pl.program_id(1), pl.program_id(2)

        def step(b_ref):
            part = lax.dot_general(a_ref[...], b_ref[...], (((0,), (0,)), ((), ())), preferred_element_type=F32)

            @pl.when(kk == 0)
            def _():
                acc_ref[...] = part

            @pl.when(kk > 0)
            def _():
                acc_ref[...] += part

        pl.when(j < half)(lambda: step(lo_ref))
        pl.when(j >= half)(lambda: step(hi_ref))

        @pl.when(kk == nk - 1)
        def _():
            o_ref[...] = acc_ref[...].astype(o_ref.dtype)

    lo_spec = pl.BlockSpec((tk, D_MODEL), lambda i, j, kk: (jnp.where(j < half, kk, nk - 1), jnp.minimum(j, half - 1)))
    hi_spec = pl.BlockSpec((tk, D_MODEL), lambda i, j, kk: (jnp.where(j >= half, kk, 0), jnp.maximum(j - half, 0)))
    return pl.pallas_call(
        body, name=name, grid=(m // tm, N_DEV, nk),
        in_specs=[pl.BlockSpec((tk, tm), lambda i, j, kk: (kk, i)), lo_spec, hi_spec],
        out_specs=pl.BlockSpec((None, tm, D_MODEL), lambda i, j, kk: (j, i, 0)),
        out_shape=jax.ShapeDtypeStruct((N_DEV, m, D_MODEL), BF16),
        scratch_shapes=[pltpu.VMEM((tm, D_MODEL), F32)],
        compiler_params=_params("parallel", "parallel", "arbitrary"),
    )(a, b_lo, b_hi)


LANES = 128


def _residues_to_rows(ref, scr, d):
    w = ref.shape[1] // d
    n = ref.shape[0]
    for r in range(d):
        for c in range(w // LANES):
            scr[c, pl.ds(r, n, stride=d), :] = ref[:, r * w + c * LANES:r * w + (c + 1) * LANES].astype(F32)
    return jnp.concatenate([scr[c] for c in range(w // LANES)], axis=1)


def _rows_to_residues(val, ref, scr, d):
    w = val.shape[1]
    n = ref.shape[0]
    for c in range(w // LANES):
        scr[c] = val[:, c * LANES:(c + 1) * LANES].astype(F32)
    for r in range(d):
        for c in range(w // LANES):
            ref[:, r * w + c * LANES:r * w + (c + 1) * LANES] = scr[c, pl.ds(r, n, stride=d), :].astype(ref.dtype)


HEAD_ROWS = 0


def _head_cols(z, whole_head=False):
    width = z.shape[1]
    a = lax.broadcasted_iota(jnp.int32, (width, LANES), 0)
    b = lax.broadcasted_iota(jnp.int32, (width, LANES), 1)
    pick = jnp.where((a // HEAD_DIM == b) if whole_head else (a == b * HEAD_DIM), 1.0, 0.0).astype(BF16)
    out = None
    for part in _split3(z):
        t = lax.dot_general(part, pick, (((1,), (0,)), ((), ())), preferred_element_type=F32)
        out = t if out is None else out + t
    return out


def _head_rows_block(cols):
    return cols.T[:N_HEADS, :]


def _rowwise(fn, rows, consts, out_rows, out_acc, *, ts, name):
    rows = [tuple(r) + (1,) * (4 - len(r)) for r in rows]
    out_rows = [tuple(o) + (1,) * (3 - len(o)) for o in out_rows]
    s = rows[0][0].shape[0] * rows[0][3]
    assert s % ts == 0, (name, s, ts)
    n_rows, n_in = len(rows), len(rows) + len(consts)
    n_row = len(out_rows)
    n_out = n_row + len(out_acc)
    moved = [(idx, w) for idx, (_, w, _, d) in enumerate(rows) if d > 1]
    moved += [(n_rows + idx, w) for idx, (w, _, d) in enumerate(out_rows) if d > 1]

    def body(*refs):
        scratch = dict(zip([key for key, _ in moved], refs[n_in + n_out:]))
        vals = []
        for idx, r in enumerate(refs[:n_in]):
            d = rows[idx][3] if idx < n_rows else 1
            vals.append(r[...] if d == 1 else _residues_to_rows(r, scratch[idx], d))
        row_vals, acc_vals = fn(*vals)
        for idx, (r, v) in enumerate(zip(refs[n_in:n_in + n_row], row_vals)):
            d = out_rows[idx][2]
            if d == 1:
                r[...] = v.astype(r.dtype)
            elif d == HEAD_ROWS:
                r[...] = _head_rows_block(v)
            else:
                _rows_to_residues(v, r, scratch[n_rows + idx], d)
        first = pl.program_id(0) == 0
        for r, v in zip(refs[n_in + n_row:n_in + n_out], acc_vals):
            @pl.when(first)
            def _(r=r, v=v):
                r[...] = v

            @pl.when(jnp.logical_not(first))
            def _(r=r, v=v):
                r[...] += v

    in_specs = [pl.BlockSpec((ts // d, d * w), functools.partial(lambda i, cb: (i, cb), cb=cb)) for _, w, cb, d in rows]
    in_specs += [pl.BlockSpec(c.shape, lambda i: (0, 0)) for c in consts]
    out_specs = [pl.BlockSpec((N_HEADS, ts), lambda i: (0, i)) if d == HEAD_ROWS else
                 pl.BlockSpec((ts // d, d * w), lambda i: (i, 0)) for w, _, d in out_rows]
    out_specs += [pl.BlockSpec(sh, lambda i: (0, 0)) for sh in out_acc]
    out_shape = [jax.ShapeDtypeStruct((N_HEADS, s) if d == HEAD_ROWS else (s // d, d * w), dt) for w, dt, d in out_rows]
    out_shape += [jax.ShapeDtypeStruct(sh, F32) for sh in out_acc]
    outs = pl.pallas_call(
        body, name=name, grid=(s // ts,), in_specs=in_specs, out_specs=out_specs, out_shape=out_shape,
        scratch_shapes=[pltpu.VMEM((w // LANES, ts, LANES), F32) for _, w in moved],
        compiler_params=_params("arbitrary" if out_acc else "parallel"),
    )(*[a for a, _, _, _ in rows], *consts)
    return outs[:n_row], outs[n_row:]


def _full(a, d=1):
    return (a, a.shape[1] // d, 0, d)


def _gate_fn(zu, zv, vg, vb, ws0, ws1, ws2, ws3, bfull, ga):
    u = _gelu_erf(zu)
    vn = _layernorm(_gelu_erf(zv), vg, vb)
    p = lax.broadcasted_iota(jnp.int32, (CHUNK, CHUNK), 0)
    q = lax.broadcasted_iota(jnp.int32, (CHUNK, CHUNK), 1)
    tril = jnp.where(q <= p, 1.0, 0.0)
    group = lax.broadcasted_iota(jnp.int32, (1, A_WIDTH), 1) // CHUNK
    sg = bfull
    for g, w in enumerate((ws0, ws1, ws2, ws3)):
        sg = sg + _bdot(w * tril, jnp.where(group == g, vn, 0.0))
    return _rms(u * sg, ga)


GATE_TS = 2 * CHUNK


def _chunk_rows(rows):
    return [slice(c * CHUNK, (c + 1) * CHUNK) for c in range(rows // CHUNK)]


def _bias_reduce(dbf, name):
    def body(x_ref, o_ref):
        lane = lax.broadcasted_iota(jnp.int32, (CHUNK, CHUNK), 1)
        out = jnp.zeros((CHUNK, CHUNK), F32)
        for g in range(A_GROUPS):
            out = jnp.where(lane == g, jnp.sum(x_ref[:, g * CHUNK:(g + 1) * CHUNK], axis=1, keepdims=True), out)
        o_ref[...] = out

    return pl.pallas_call(body, name=name, out_shape=jax.ShapeDtypeStruct((CHUNK, CHUNK), F32))(dbf)


def _pair_mask(hh):
    lane = lax.broadcasted_iota(jnp.int32, (1, 2 * HEAD_DIM), 1)
    return (lane >= HEAD_DIM * hh) & (lane < HEAD_DIM * (hh + 1))


def _lane_pick(x2, lm):
    return jnp.max(jnp.where(lm, x2, -jnp.inf), axis=1, keepdims=True)


SCALE = HEAD_DIM ** -0.5


def _blocks_per_step(nb):
    return next(n for n in (4, 2, 1) if nb % n == 0)


def _units(nblk):
    return [(b, hp, hh) for b in range(nblk) for hp in range(N_HEADS // 2) for hh in range(2)]


def _attn_specs(nb, nblk):
    cur = pl.BlockSpec((nblk * BAND, B_WIDTH), lambda r, j: (j, r))
    prev = pl.BlockSpec((BAND, B_WIDTH), lambda r, j: (jnp.maximum(nblk * j - 1, 0), r))
    nxt = pl.BlockSpec((BAND, B_WIDTH), lambda r, j: (jnp.minimum(nblk * (j + 1), nb - 1), r))
    return cur, prev, nxt


def _pair_cols(hp):
    return slice(2 * HEAD_DIM * hp, 2 * HEAD_DIM * (hp + 1))


def _rows(b):
    return slice(b * BAND, (b + 1) * BAND)


def _with_prev(cur_ref, prev_ref, b, sl):
    if b == 0:
        return jnp.concatenate([prev_ref[:, sl], cur_ref[_rows(0), sl]], axis=0)
    return cur_ref[(b - 1) * BAND:(b + 1) * BAND, sl]


def _with_next(cur_ref, next_ref, b, sl, nblk):
    if b == nblk - 1:
        return jnp.concatenate([cur_ref[_rows(b), sl], next_ref[:, sl]], axis=0)
    return cur_ref[b * BAND:(b + 2) * BAND, sl]


def _band_valid(other_exists):
    row = lax.broadcasted_iota(jnp.int32, (BAND, 2 * BAND), 0)
    col = lax.broadcasted_iota(jnp.int32, (BAND, 2 * BAND), 1)
    return (col >= row) & (col <= row + BAND), other_exists


def _masked(lm, x):
    return jnp.where(lm, x, jnp.zeros_like(x))


def _attn_fwd(q, k, v, d, name):
    nb = q.shape[0] // BAND
    nblk = _blocks_per_step(nb)
    units = _units(nblk)
    cur, prev, _ = _attn_specs(nb, nblk)

    def body(q_ref, kc_ref, kp_ref, vc_ref, vp_ref, o_ref, l_ref):
        band, has_prev = _band_valid(pl.program_id(1) > 0)
        col = lax.broadcasted_iota(jnp.int32, (BAND, 2 * BAND), 1)
        valid = [band & ((col >= BAND) | has_prev)] + [band] * (nblk - 1)

        def scores(unit):
            b, hp, hh = unit
            sl = _pair_cols(hp)
            return _dotg(_masked(_pair_mask(hh), q_ref[_rows(b), sl]), _with_prev(kc_ref, kp_ref, b, sl), 1, 1)

        ahead, half = scores(units[0]), None
        for idx, (b, hp, hh) in enumerate(units):
            raw = ahead
            if idx + 1 < len(units):
                ahead = scores(units[idx + 1])
            sl, lm = _pair_cols(hp), _pair_mask(hh)
            s = jnp.where(valid[b], raw * SCALE, NEG_INF)
            m = jnp.max(s, axis=1, keepdims=True)
            p = jnp.exp(s - m)
            den = jnp.sum(p, axis=1, keepdims=True)
            o = _dotg(p, _with_prev(vc_ref, vp_ref, b, sl), 1, 0) / den
            lse = m + jnp.log(den)
            if hh == 0:
                half = (o, lse)
            else:
                o_ref[_rows(b), sl] = jnp.where(lm, o, half[0])
                l_ref[_rows(b), sl] = jnp.where(lm, lse, half[1])

    return pl.pallas_call(
        body, name=name, grid=(d, nb // nblk), in_specs=[cur, cur, prev, cur, prev], out_specs=[cur, cur],
        out_shape=[jax.ShapeDtypeStruct(q.shape, F32), jax.ShapeDtypeStruct(q.shape, F32)],
        compiler_params=_params("parallel", "parallel"),
    )(q, k, k, v, v)


def _attn_bwd_q(q, k, v, do, lse, delta, d, name):
    nb = q.shape[0] // BAND
    nblk = _blocks_per_step(nb)
    units = _units(nblk)
    cur, prev, _ = _attn_specs(nb, nblk)
    per_head = pl.BlockSpec((nblk * BAND, LANES), lambda r, j: (j, r))

    def body(q_ref, kc_ref, kp_ref, vc_ref, vp_ref, do_ref, l_ref, dl_ref, dq_ref):
        band, has_prev = _band_valid(pl.program_id(1) > 0)
        col = lax.broadcasted_iota(jnp.int32, (BAND, 2 * BAND), 1)
        valid = [band & ((col >= BAND) | has_prev)] + [band] * (nblk - 1)

        def products(unit):
            b, hp, hh = unit
            sl, lm = _pair_cols(hp), _pair_mask(hh)
            return (_dotg(_masked(lm, q_ref[_rows(b), sl]), _with_prev(kc_ref, kp_ref, b, sl), 1, 1),
                    _dotg(_masked(lm, do_ref[_rows(b), sl]), _with_prev(vc_ref, vp_ref, b, sl), 1, 1))

        ahead, half = products(units[0]), None
        for idx, (b, hp, hh) in enumerate(units):
            raw, dp = ahead
            if idx + 1 < len(units):
                ahead = products(units[idx + 1])
            sl, lm = _pair_cols(hp), _pair_mask(hh)
            s = jnp.where(valid[b], raw * SCALE, NEG_INF)
            head = lax.broadcasted_iota(jnp.int32, (1, LANES), 1) == 2 * hp + hh
            p = jnp.exp(s - _lane_pick(l_ref[_rows(b), :], head))
            ds = p * (dp - _lane_pick(dl_ref[_rows(b), :], head))
            dq = _dotg(ds, _with_prev(kc_ref, kp_ref, b, sl), 1, 0) * SCALE
            if hh == 0:
                half = dq
            else:
                dq_ref[_rows(b), sl] = jnp.where(lm, dq, half).astype(dq_ref.dtype)

    return pl.pallas_call(
        body, name=name, grid=(d, nb // nblk),
        in_specs=[cur, cur, prev, cur, prev, cur, per_head, per_head], out_specs=cur,
        out_shape=jax.ShapeDtypeStruct(q.shape, BF16),
        compiler_params=_params("parallel", "parallel"),
    )(q, k, k, v, v, do, lse, delta)


def _attn_bwd_kv(q, k, v, do, lse_t, delta_t, d, name):
    nb = q.shape[0] // BAND
    nblk = _blocks_per_step(nb)
    units = _units(nblk)
    cur, _, nxt = _attn_specs(nb, nblk)
    t_cur = pl.BlockSpec((1, N_HEADS, nblk * BAND), lambda r, j: (r, 0, j))
    t_nxt = pl.BlockSpec((1, N_HEADS, BAND), lambda r, j: (r, 0, jnp.minimum(nblk * (j + 1), nb - 1)))

    def body(k_ref, v_ref, qc_ref, qn_ref, doc_ref, don_ref, lc_ref, ln_ref, dlc_ref, dln_ref, dk_ref, dv_ref):
        band, has_next = _band_valid(pl.program_id(1) < nb // nblk - 1)
        col = lax.broadcasted_iota(jnp.int32, (BAND, 2 * BAND), 1)
        valid = [band] * (nblk - 1) + [band & ((col < BAND) | has_next)]

        def head_row(c_ref, n_ref, b, h):
            if b == nblk - 1:
                return jnp.concatenate([c_ref[0, h:h + 1, b * BAND:(b + 1) * BAND], n_ref[0, h:h + 1, :]], axis=1)
            return c_ref[0, h:h + 1, b * BAND:(b + 2) * BAND]

        def products(unit):
            b, hp, hh = unit
            sl, lm = _pair_cols(hp), _pair_mask(hh)
            return (_dotg(_masked(lm, k_ref[_rows(b), sl]), _with_next(qc_ref, qn_ref, b, sl, nblk), 1, 1),
                    _dotg(_masked(lm, v_ref[_rows(b), sl]), _with_next(doc_ref, don_ref, b, sl, nblk), 1, 1))

        ahead, half = products(units[0]), None
        for idx, (b, hp, hh) in enumerate(units):
            raw, dpt = ahead
            if idx + 1 < len(units):
                ahead = products(units[idx + 1])
            sl, lm, h = _pair_cols(hp), _pair_mask(hh), 2 * hp + hh
            st = jnp.where(valid[b], raw * SCALE, NEG_INF)
            pt = jnp.exp(st - head_row(lc_ref, ln_ref, b, h))
            dv = _dotg(pt, _with_next(doc_ref, don_ref, b, sl, nblk), 1, 0)
            dst = pt * (dpt - head_row(dlc_ref, dln_ref, b, h))
            dk = _dotg(dst, _with_next(qc_ref, qn_ref, b, sl, nblk), 1, 0) * SCALE
            if hh == 0:
                half = (dk, dv)
            else:
                dk_ref[_rows(b), sl] = jnp.where(lm, dk, half[0]).astype(dk_ref.dtype)
                dv_ref[_rows(b), sl] = jnp.where(lm, dv, half[1]).astype(dv_ref.dtype)

    return pl.pallas_call(
        body, name=name, grid=(d, nb // nblk),
        in_specs=[cur, cur, cur, nxt, cur, nxt, t_cur, t_nxt, t_cur, t_nxt], out_specs=[cur, cur],
        out_shape=[jax.ShapeDtypeStruct(q.shape, BF16), jax.ShapeDtypeStruct(q.shape, BF16)],
        compiler_params=_params("parallel", "parallel"),
    )(k, v, q, q, do, do, lse_t, lse_t, delta_t, delta_t)


def _spread_rows(a, d):
    return a.reshape(N_HEADS, a.shape[1] // d, d).transpose(2, 0, 1)


FF_TS = 512
FF_TC = 1024
FF_SUB = 256
HALO = 8
HALO_BF16 = 16
UP_BLOCKS = D_MODEL // FF_TC


def _conv3(ext, w, b):
    return b + w[0:1, :] * pltpu.roll(ext, 2, 0) + w[1:2, :] * pltpu.roll(ext, 1, 0) + w[2:3, :] * ext


def _ffn_specs(s, cols_first):
    nrb = s // FF_TS
    per, per16 = FF_TS // HALO, FF_TS // HALO_BF16

    def mk(block, fn):
        if cols_first:
            return pl.BlockSpec(block, lambda j, i: fn(i, j))
        return pl.BlockSpec(block, lambda i, j: fn(i, j))

    specs = types.SimpleNamespace(
        nrb=nrb, ncb=D_FF // FF_TC,
        row=mk((FF_TS, FF_TC), lambda i, j: (i, j)),
        before=mk((HALO, FF_TC), lambda i, j: (jnp.maximum(i * per - 1, 0), j)),
        after=mk((HALO, FF_TC), lambda i, j: (jnp.minimum((i + 1) * per, nrb * per - 1), j)),
        w=mk((3, FF_TC), lambda i, j: (0, j)),
        b=mk((1, FF_TC), lambda i, j: (0, j)),
        part=mk((HALO, FF_TC), lambda i, j: (i, j)),
        act=mk((FF_TS, D_MODEL), lambda i, j: (i, 0)),
        act_before=mk((HALO_BF16, D_MODEL), lambda i, j: (jnp.maximum(i * per16 - 1, 0), 0)),
        act_after=mk((HALO_BF16, D_MODEL), lambda i, j: (jnp.minimum((i + 1) * per16, nrb * per16 - 1), 0)),
        up_gate=mk((None, D_MODEL, FF_TC), lambda i, j: (j // UP_BLOCKS, 0, j % UP_BLOCKS)),
        up_val=mk((None, D_MODEL, FF_TC), lambda i, j: (N_DEV // 2 + j // UP_BLOCKS, 0, j % UP_BLOCKS)),
        down=mk((FF_TC, D_MODEL), lambda i, j: (j, 0)),
    )
    return specs


def _ffn_up_geglu(h, w_up, wg, wv, bg, bv, name):
    s = h.shape[0]
    sp = _ffn_specs(s, True)

    def body(h_ref, hb_ref, ugw_ref, uvw_ref, wg_ref, wv_ref, bg_ref, bv_ref, ug_ref, uv_ref, y_ref):
        keep = jnp.where(pl.program_id(1) > 0, 1.0, 0.0).astype(BF16)
        hext = jnp.concatenate([hb_ref[...] * keep, h_ref[...]], axis=0)
        eg = _dotg(hext, ugw_ref[...], 1, 0)
        ug_ref[...] = eg[HALO_BF16:, :]
        act = _gelu_tanh(_conv3(eg, wg_ref[...], bg_ref[...])[HALO_BF16:, :])
        ev = _dotg(hext, uvw_ref[...], 1, 0)
        uv_ref[...] = ev[HALO_BF16:, :]
        y_ref[...] = (act * _conv3(ev, wv_ref[...], bv_ref[...])[HALO_BF16:, :]).astype(y_ref.dtype)

    return pl.pallas_call(
        body, name=name, grid=(sp.ncb, sp.nrb),
        in_specs=[sp.act, sp.act_before, sp.up_gate, sp.up_val, sp.w, sp.w, sp.b, sp.b],
        out_specs=[sp.row, sp.row, sp.row],
        out_shape=[jax.ShapeDtypeStruct((s, D_FF), F32), jax.ShapeDtypeStruct((s, D_FF), F32),
                   jax.ShapeDtypeStruct((s, D_FF), BF16)],
        compiler_params=_params("parallel", "parallel"),
    )(h, h, w_up, w_up, wg, wv, bg, bv)


def _sum_parts(parts, name):
    n = parts.shape[0] // HALO

    def body(p_ref, o_ref):
        acc = p_ref[0:HALO, :]
        for t in range(1, n):
            acc = acc + p_ref[t * HALO:(t + 1) * HALO, :]
        o_ref[...] = acc

    return pl.pallas_call(body, name=name, out_shape=jax.ShapeDtypeStruct((HALO, parts.shape[1]), F32))(parts)


def _ffn_geglu_bwd(ug, uv, df, w_down, w_up, wg, wv, bg, bv, name):
    s = ug.shape[0]
    sp = _ffn_specs(s, False)
    nrb = sp.nrb
    rows = FF_TS + 2 * HALO
    lo, hi = HALO, HALO + FF_TS

    def body(ug_ref, uv_ref, hg_ref, hv_ref, ng_ref, nv_ref, df_ref, dfn_ref, dw_ref, ugw_ref, uvw_ref,
             wg_ref, wv_ref, bg_ref, bv_ref, dug_ref, duv_ref, dh_ref, dwg_ref, dwv_ref):
        i, j = pl.program_id(0), pl.program_id(1)
        keep_top = jnp.where(i > 0, 1.0, 0.0)
        keep_bot = jnp.where(i < nrb - 1, 1.0, 0.0).astype(BF16)
        dfe = jnp.concatenate([df_ref[...], dfn_ref[...] * keep_bot], axis=0)

        def back(dc, e, w, du_ref, sl):
            up1 = pltpu.roll(dc, rows - 1, 0)
            up2 = pltpu.roll(dc, rows - 2, 0)
            du = (w[2:3, :] * dc + w[1:2, :] * up1 + w[0:1, :] * up2)[lo:hi, :].astype(BF16)
            du_ref[:, sl] = du
            p1, p2 = up1 * e, up2 * e
            colsum = lambda p: jnp.sum(p[lo:hi, :], axis=0, keepdims=True)
            row = lambda p, t: p[t:t + 1, :]
            d_w1 = colsum(p1) + row(p1, lo - 1) - row(p1, hi - 1)
            d_w0 = colsum(p2) + row(p2, lo - 2) + row(p2, lo - 1) - row(p2, hi - 2) - row(p2, hi - 1)
            sums = [d_w0, d_w1, colsum(dc * e), colsum(dc), jnp.zeros((HALO - 4, FF_SUB), F32)]
            return du, jnp.concatenate(sums, axis=0)

        dh = None
        n_sub = FF_TC // FF_SUB
        cols = [slice(c * FF_SUB, (c + 1) * FF_SUB) for c in range(n_sub)]
        d_act = lambda c: _dotg(dfe, dw_ref[cols[c], :], 1, 1)[:FF_TS + HALO, :]
        ahead = d_act(0)
        for c in range(n_sub):
            sl, dy = cols[c], ahead
            if c + 1 < n_sub:
                ahead = d_act(c + 1)
            dye = jnp.concatenate([jnp.zeros((HALO, FF_SUB), F32), dy], axis=0)
            eg = jnp.concatenate([hg_ref[:, sl] * keep_top, ug_ref[:, sl], ng_ref[:, sl]], axis=0)
            ev = jnp.concatenate([hv_ref[:, sl] * keep_top, uv_ref[:, sl], nv_ref[:, sl]], axis=0)
            wg_, wv_ = wg_ref[:, sl], wv_ref[:, sl]
            gate = _conv3(eg, wg_, bg_ref[:, sl])
            val = _conv3(ev, wv_, bv_ref[:, sl])
            act, slope = _gelu_tanh_and_slope(gate)
            dug, dwg_ref[:, sl] = back((dye * val) * slope, eg, wg_, dug_ref, sl)
            duv, dwv_ref[:, sl] = back(dye * act, ev, wv_, duv_ref, sl)
            term = _dotg(dug, ugw_ref[:, sl], 1, 1) + _dotg(duv, uvw_ref[:, sl], 1, 1)
            dh = term if dh is None else dh + term

        @pl.when(j == 0)
        def _():
            dh_ref[...] = dh

        @pl.when(j > 0)
        def _():
            dh_ref[...] += dh

    parts = jax.ShapeDtypeStruct((nrb * HALO, D_FF), F32)
    dug, duv, dh, pg, pv = pl.pallas_call(
        body, name=name, grid=(nrb, sp.ncb),
        in_specs=[sp.row, sp.row, sp.before, sp.before, sp.after, sp.after, sp.act, sp.act_after, sp.down,
                  sp.up_gate, sp.up_val, sp.w, sp.w, sp.b, sp.b],
        out_specs=[sp.row, sp.row, sp.act, sp.part, sp.part],
        out_shape=[jax.ShapeDtypeStruct((s, D_FF), BF16), jax.ShapeDtypeStruct((s, D_FF), BF16),
                   jax.ShapeDtypeStruct((s, D_MODEL), F32), parts, parts],
        compiler_params=_params("parallel", "arbitrary"),
    )(ug, uv, ug, uv, ug, uv, df, df, w_down, w_up, w_up, wg, wv, bg, bv)
    return dug, duv, dh, _sum_parts(pg, name=name + "_sum_gate"), _sum_parts(pv, name=name + "_sum_val")


def _rope_tables(s):
    inv = ROPE_THETA ** (-jnp.arange(0, ROT_DIM, 2, dtype=F32) / ROT_DIM)
    ang = jnp.arange(s, dtype=F32)[:, None] * inv[None, :]
    cos8, sin8 = jnp.cos(ang), jnp.sin(ang)
    rest = HEAD_DIM - ROT_DIM
    cos_h = jnp.concatenate([cos8, cos8, jnp.ones((s, rest), F32)], axis=1)
    sin_h = jnp.concatenate([sin8, sin8, jnp.zeros((s, rest), F32)], axis=1)
    return jnp.tile(cos_h, (1, LANES // HEAD_DIM)), jnp.tile(sin_h, (1, LANES // HEAD_DIM))


def _all_heads(table):
    return jnp.concatenate([table] * (B_WIDTH // LANES), axis=1)


def _layer_fwd(x, w, cos, sin, last):
    sv = types.SimpleNamespace(x=x)
    (sv.h1,), _ = _rowwise(lambda xb, g: ((_rms(xb, g),), ()), [_full(x)], [w.g_pre], [(D_MODEL, BF16)], [],
                           ts=512, name="pre_mix_norm")
    sv.proj = _matmul(sv.h1, w.big("w_in", sv.h1), mode="nt", out_dtype=F32, name="proj")

    gate_consts = [w.vg, w.vb, *w.ws, w.bfull, w.ga]
    def gate_fwd_fn(zu, zv, *consts):
        chunks = [_gate_fn(zu[r], zv[r], *consts) for r in _chunk_rows(zu.shape[0])]
        return (jnp.concatenate(chunks, axis=0),), ()

    (na,), _ = _rowwise(gate_fwd_fn, [(sv.proj, A_WIDTH, 0), (sv.proj, A_WIDTH, 1)], gate_consts,
                        [(A_WIDTH, BF16)], [], ts=GATE_TS, name="gate_fwd")

    def rope_fn(qr, kr, vr, cs, sn):
        cs, sn = _all_heads(cs), _all_heads(sn)
        return (qr * cs + _rot_half(qr) * sn, kr * cs + _rot_half(kr) * sn, vr), ()

    def rope_all(qr, kr, vr, cs, sn):
        return rope_fn(qr, kr, vr, cs, sn)[0] * len(DILATIONS), ()

    qkv, _ = _rowwise(
        rope_all, [(sv.proj, B_WIDTH, 2), (sv.proj, B_WIDTH, 3), (sv.proj, B_WIDTH, 4), _full(cos), _full(sin)], [],
        [(B_WIDTH, BF16, d) for d in DILATIONS for _ in range(3)], [], ts=512, name="rope_fwd")
    sv.qkv = {d: qkv[3 * i:3 * i + 3] for i, d in enumerate(DILATIONS)}

    branch = []
    for d in DILATIONS:
        o, l = _attn_fwd(*sv.qkv[d], d, name=f"attn_fwd_d{d}")
        branch += [_full(o, d), _full(l, d)]

    def combine_fn(o1, l1, o2, l2, o3, l3, nab, gb):
        m = jnp.maximum(jnp.maximum(l1, l2), l3)
        e1, e2, e3 = jnp.exp(l1 - m), jnp.exp(l2 - m), jnp.exp(l3 - m)
        den = e1 + e2 + e3
        ob = (e1 / den) * o1 + (e2 / den) * o2 + (e3 / den) * o3
        mixed = jnp.concatenate([nab, _rms(ob, gb).astype(BF16)], axis=1)
        lse = _head_cols(m + jnp.log(den))
        return (mixed, ob, lse) + (lse,) * len(DILATIONS), ()

    (sv.mixed, sv.ob, sv.lse_rows, *lses), _ = _rowwise(
        combine_fn, branch + [_full(na)], [w.gb],
        [(D_MODEL, BF16), (B_WIDTH, F32), (LANES, F32, HEAD_ROWS)] + [(LANES, F32, d) for d in DILATIONS], [],
        ts=512, name="combine")
    sv.lse = dict(zip(DILATIONS, lses))
    sv.y = _matmul(sv.mixed, w.big("w_out", sv.mixed), mode="nn", out_dtype=F32, name="mix_out")

    def mid_fn(xb, yb, g1, g2):
        x1 = xb + _rms(yb, g1)
        return (x1, _rms(x1, g2)), ()

    (sv.x1, sv.h2), _ = _rowwise(mid_fn, [_full(x), _full(sv.y)], [w.g_pm, w.g_pf], [(D_MODEL, F32), (D_MODEL, BF16)], [],
                                 ts=512, name="post_mix_norm")
    conv_w = w.big("conv_w", sv.h2)
    sv.ug, sv.uv, sv.yff = _ffn_up_geglu(sv.h2, w.big("w_up", sv.h2), conv_w[:, :D_FF], conv_w[:, D_FF:],
                                         w.cb_g, w.cb_v, name="ffn_up_geglu")
    sv.f = _matmul(sv.yff, w.big("w_down", sv.yff), mode="nn", out_dtype=F32, name="ffn_down")
    if last:
        return None, sv
    (x2,), _ = _rowwise(lambda xb, fb, g: ((xb + _rms(fb, g),), ()), [_full(sv.x1), _full(sv.f)], [w.g_post],
                        [(D_MODEL, F32)], [], ts=512, name="post_ffn_norm")
    return x2, sv


def _layer_bwd(dx2, sv, w, cos, sin, emit):
    g = {}

    def post_fn(fb, dxb, gp):
        _, vjp = jax.vjp(_rms, fb, gp)
        df, dg = vjp(dxb)
        return (df,), (dg,)

    (df,), (g["post_ffn_norm"],) = _rowwise(post_fn, [_full(sv.f), _full(dx2)], [w.g_post], [(D_MODEL, BF16)],
                                            [(1, D_MODEL)], ts=512, name="post_ffn_norm_bwd")
    big = {"w_down": _matmul(sv.yff, df, mode="tn", out_dtype=BF16, name="ffn_down_dw").reshape(N_DEV, -1, D_MODEL)}
    conv_w = w.big("conv_w", df)
    dug, duv, dh2, dwg, dwv = _ffn_geglu_bwd(sv.ug, sv.uv, df, w.big("w_down", df), w.big("w_up", df),
                                             conv_w[:, :D_FF], conv_w[:, D_FF:], w.cb_g, w.cb_v, name="ffn_geglu_bwd")
    big["conv_w"] = jnp.concatenate([dwg[0:3], dwv[0:3]], axis=1).reshape(3, N_DEV, D_MODEL).transpose(1, 0, 2)
    g["conv_b"] = jnp.concatenate([dwg[3], dwv[3]], axis=0)
    big["w_up"] = _matmul_by_destination(sv.h2, dug, duv, name="ffn_up_dw")
    g_pm = w.g_pm + emit(big)

    def mid_fn(x1b, yb, dhb, dxb, g1, g2):
        _, vjp2 = jax.vjp(_rms, x1b, g2)
        dx1h, dg2 = vjp2(dhb)
        dx1 = dxb + dx1h
        _, vjp1 = jax.vjp(_rms, yb, g1)
        dy, dg1 = vjp1(dx1)
        return (dx1, dy), (dg1, dg2)

    (dx1, dy), (g["post_mix_norm"], g["pre_ffn_norm"]) = _rowwise(
        mid_fn, [_full(sv.x1), _full(sv.y), _full(dh2), _full(dx2)], [g_pm, w.g_pf],
        [(D_MODEL, F32), (D_MODEL, BF16)], [(1, D_MODEL), (1, D_MODEL)], ts=256, name="post_mix_norm_bwd")
    dmixed = _matmul(dy, w.big("w_out", dy), mode="nt", out_dtype=F32, name="mix_out_dx")
    dw_out = _matmul(sv.mixed, dy, mode="tn", out_dtype=BF16, name="mix_out_dw").reshape(N_DEV, -1, D_MODEL)
    g_b = w.gb + emit({"w_out": dw_out})

    def attn_out_fn(obb, dmb, gb):
        _, vjp = jax.vjp(_rms, obb, gb)
        do, dgb = vjp(dmb)
        delta = _head_cols(do * obb, whole_head=True)
        return (delta,) + (do,) * len(DILATIONS) + (delta,) * len(DILATIONS), (dgb,)

    (delta_rows, *outs), (g["out_norm_b"],) = _rowwise(
        attn_out_fn, [_full(sv.ob), (dmixed, B_WIDTH, 1)], [g_b],
        [(LANES, F32, HEAD_ROWS)] + [(B_WIDTH, BF16, d) for d in DILATIONS] + [(LANES, F32, d) for d in DILATIONS],
        [(1, B_WIDTH)], ts=512, name="attn_out_bwd")
    do = dict(zip(DILATIONS, outs[:len(DILATIONS)]))
    delta = dict(zip(DILATIONS, outs[len(DILATIONS):]))
    parts = {"q": [], "k": [], "v": []}
    for d in DILATIONS:
        qv, kv, vv = sv.qkv[d]
        dq = _attn_bwd_q(qv, kv, vv, do[d], sv.lse[d], delta[d], d, name=f"attn_bwd_q_d{d}")
        dk, dv = _attn_bwd_kv(qv, kv, vv, do[d], _spread_rows(sv.lse_rows, d), _spread_rows(delta_rows, d), d,
                              name=f"attn_bwd_kv_d{d}")
        parts["q"].append(_full(dq, d))
        parts["k"].append(_full(dk, d))
        parts["v"].append(_full(dv, d))

    def rope_bwd_fn(q1, q2, q3, k1, k2, k3, v1, v2, v3, cs, sn):
        cs, sn = _all_heads(cs), _all_heads(sn)

        def back(t):
            return t * cs - _rot_half(t * sn)
        return (jnp.concatenate([back(q1 + q2 + q3), back(k1 + k2 + k3), v1 + v2 + v3], axis=1),), ()

    (dzb,), _ = _rowwise(rope_bwd_fn, parts["q"] + parts["k"] + parts["v"] + [_full(cos), _full(sin)], [],
                         [(3 * B_WIDTH, BF16)], [], ts=256, name="rope_bwd")

    gate_consts = [w.vg, w.vb, *w.ws, w.bfull, w.ga]

    def gate_bwd_fn(zu, zv, dna, dzb_rows, *consts):
        dz, sums = [], None
        for r in _chunk_rows(zu.shape[0]):
            _, vjp = jax.vjp(_gate_fn, zu[r], zv[r], *consts)
            grads = vjp(dna[r])
            dz.append(jnp.concatenate([grads[0].astype(BF16), grads[1].astype(BF16), dzb_rows[r]], axis=1))
            sums = grads[2:] if sums is None else tuple(a + b for a, b in zip(sums, grads[2:]))
        return (jnp.concatenate(dz, axis=0),), tuple(sums)

    (dproj,), gsmall = _rowwise(
        gate_bwd_fn, [(sv.proj, A_WIDTH, 0), (sv.proj, A_WIDTH, 1), (dmixed, A_WIDTH, 0), _full(dzb)], gate_consts,
        [(IN_COLS, BF16)], [c.shape for c in gate_consts], ts=GATE_TS, name="gate_bwd")
    g["v_norm_g"], g["v_norm_b"] = gsmall[0], gsmall[1]
    g["w_spatial"] = jnp.stack(gsmall[2:6])
    g["b_spatial"] = _bias_reduce(gsmall[6], name="bias_reduce")[:, :A_GROUPS].T
    g["out_norm_a"] = gsmall[7]

    dh1 = _matmul(dproj, w.big("w_in", dproj), mode="nn", out_dtype=F32, name="proj_dx")
    dw_in = _matmul(dproj, sv.h1, mode="tn", out_dtype=BF16, name="proj_dw").reshape(N_DEV, -1, D_MODEL)
    g_pre = w.g_pre + emit({"w_in": dw_in})

    def pre_fn(xb, dhb, dxb, gp):
        _, vjp = jax.vjp(_rms, xb, gp)
        dxh, dg = vjp(dhb)
        return (dxb + dxh,), (dg,)

    (dx,), (g["pre_mix_norm"],) = _rowwise(pre_fn, [_full(sv.x), _full(dh1), _full(dx1)], [g_pre], [(D_MODEL, F32)],
                                           [(1, D_MODEL)], ts=512, name="pre_mix_norm_bwd")
    return dx, g


def _layer_weights(l, full, small):
    row = lambda a: a[l].reshape(1, -1)
    return types.SimpleNamespace(
        big=functools.partial(full, l),
        g_pre=row(small["pre_mix_norm"]), vg=row(small["v_norm_g"]), vb=row(small["v_norm_b"]),
        ws=[small["w_spatial"][l, gi] for gi in range(A_GROUPS)],
        bfull=jnp.repeat(small["b_spatial"][l].T, CHUNK, axis=1),
        ga=row(small["out_norm_a"]), gb=row(small["out_norm_b"]),
        g_pm=row(small["post_mix_norm"]), g_pf=row(small["pre_ffn_norm"]),
        cb_g=small["conv_b"][l][:D_FF].reshape(1, -1), cb_v=small["conv_b"][l][D_FF:].reshape(1, -1),
        g_post=row(small["post_ffn_norm"]))


def _local_step(x, target, full, small, emit, emit_small, started):
    s = x.shape[0]
    cos, sin = _rope_tables(s)
    ws = [_layer_weights(l, full, small) for l in range(N_LAYERS)]
    ws[0].g_pre = ws[0].g_pre + started
    saved = []
    h = x
    for l in range(N_LAYERS):
        h, sv = _layer_fwd(h, ws[l], cos, sin, last=l == N_LAYERS - 1)
        saved.append(sv)

    def loss_fn(xb, fb, tb, g):
        diff = (xb + _rms(fb, g)) - tb
        return (diff * (1.0 / D_MODEL),), (jnp.sum(diff * diff, axis=0, keepdims=True),)

    (dh,), (sq,) = _rowwise(loss_fn, [_full(saved[-1].x1), _full(saved[-1].f), _full(target)], [ws[-1].g_post],
                            [(D_MODEL, F32)], [(1, D_MODEL)], ts=512, name="loss")
    loss = 0.5 * jnp.sum(sq) * (1.0 / D_MODEL)
    grads = [None] * N_LAYERS
    for l in reversed(range(N_LAYERS)):
        dh, grads[l] = _layer_bwd(dh, saved[l], ws[l], cos, sin, functools.partial(emit, l))
        token = emit_small(l, grads[l], loss)
        if l > 0:
            ws[l - 1].g_post = ws[l - 1].g_post + token
    return loss, dh, grads


def _place():
    return lax.axis_index("x"), lax.axis_index("y"), lax.axis_index("c")


FLIPS = ((1, 0, 0), (0, 1, 0), (1, 1, 0), (0, 0, 1), (1, 0, 1), (0, 1, 1), (1, 1, 1))
HBM_SPEC = pl.BlockSpec(memory_space=pltpu.HBM)
SEM_SPEC = pl.BlockSpec(memory_space=pltpu.SEMAPHORE)
SPLIT_COPY = pltpu.CompilerParams(has_side_effects=pltpu.SideEffectType.DATAFLOW_SIDE_EFFECTING)


def _peers():
    mx, my, mc = _place()
    out = []
    for fx, fy, fc in FLIPS:
        px, py, pc = (1 - mx if fx else mx), (1 - my if fy else my), (1 - mc if fc else mc)
        out.append(((px, py, pc), 4 * px + 2 * py + pc))
    return out


def _flat_copies(scatter, src_refs, land_refs, send_sems, recv_sems):
    mx, my, mc = _place()
    me = 4 * mx + 2 * my + mc
    n = len(src_refs)
    copies = []
    for t in range(n):
        for i, (peer, number) in enumerate(_peers()):
            copies.append(pltpu.make_async_remote_copy(
                src_ref=src_refs[t].at[number] if scatter else src_refs[t],
                dst_ref=land_refs[t].at[i] if scatter else land_refs[t].at[me],
                send_sem=send_sems.at[t * len(FLIPS) + i], recv_sem=recv_sems.at[t * len(FLIPS) + i],
                device_id=peer, device_id_type=MESH_ID))
    return copies


def _flat_start(arrays, scatter, name):
    n = len(arrays)
    slots = len(FLIPS) if scatter else N_DEV
    lands = [lax.empty((slots,) + (a.shape[1:] if scatter else a.shape), a.dtype) for a in arrays]

    def body(*refs):
        src, land, (send_sems, recv_sems), token = refs[:n], refs[n:2 * n], refs[2 * n:2 * n + 2], refs[-1]
        for cp in _flat_copies(scatter, src, land, send_sems, recv_sems):
            cp.start()
        token[...] = jnp.zeros_like(token)

    hbm = [pltpu.HBM(a.shape, a.dtype) for a in arrays] + [pltpu.HBM(a.shape, a.dtype) for a in lands]
    sems = pltpu.SemaphoreType.DMA((n * len(FLIPS),))
    outs = pl.pallas_call(
        body, name=name, out_shape=(sems, sems, *hbm, jax.ShapeDtypeStruct((8, 128), F32)),
        in_specs=[HBM_SPEC] * (2 * n),
        out_specs=(SEM_SPEC, SEM_SPEC, *([HBM_SPEC] * (2 * n)), pl.BlockSpec(memory_space=pltpu.VMEM)),
        input_output_aliases={t: 2 + t for t in range(2 * n)}, compiler_params=SPLIT_COPY,
    )(*[pltpu.with_memory_space_constraint(a, pltpu.HBM) for a in (*arrays, *lands)])
    return types.SimpleNamespace(sems=outs[:2], thru=outs[2:2 + 2 * n], scatter=scatter, n=n), outs[-1][0:1, 0:1]


def _flat_wait(handle, after, name):
    n = handle.n

    def body(*refs):
        src, land, (send_sems, recv_sems) = refs[:n], refs[n:2 * n], refs[2 * n:2 * n + 2]
        for cp in _flat_copies(handle.scatter, src, land, send_sems, recv_sems):
            cp.wait_send()
            cp.wait_recv()

    outs = pl.pallas_call(
        body, name=name, out_shape=tuple(pltpu.HBM(a.shape, a.dtype) for a in handle.thru),
        in_specs=[HBM_SPEC] * (2 * n) + [SEM_SPEC, SEM_SPEC, ANY], out_specs=tuple([HBM_SPEC] * (2 * n)),
        input_output_aliases={t: t for t in range(2 * n)}, compiler_params=SPLIT_COPY,
    )(*handle.thru, *handle.sems, after)
    return outs[:n], outs[n:]


def _adamw(w, g, m, v):
    m2 = ADAM_B1 * m + (1.0 - ADAM_B1) * g
    v2 = ADAM_B2 * v + (1.0 - ADAM_B2) * (g * g)
    m_hat = m2 / (1.0 - ADAM_B1 ** ADAM_STEP)
    v_hat = v2 / (1.0 - ADAM_B2 ** ADAM_STEP)
    return -ADAM_LR * (m_hat / (jnp.sqrt(v_hat) + ADAM_EPS) + ADAM_WD * w), m2, v2


def _adamw_sharded(me, mine, landed, w, m, v, tr, name):
    _, r, c = w.shape
    nt = r // tr
    assert r % tr == 0 and len(mine) == len(landed) == N_LAYERS == 2, name
    per_layer = 1 + len(FLIPS)

    def body(me_ref, *refs):
        terms, (w_ref, m_ref, v_ref), outs = refs[:2 * per_layer], refs[2 * per_layer:2 * per_layer + 3], refs[-4:]
        layer = pl.program_id(0)

        def total(group):
            g = group[0][0].astype(F32)
            for t in group[1:]:
                g = g + t[0].astype(F32)
            return g

        g = jnp.where(layer == 0, total(terms[:per_layer]), total(terms[per_layer:]))
        d, m2, v2 = _adamw(w_ref[0], g, m_ref[0], v_ref[0])
        for o, val in zip(outs, (g, d, m2, v2)):
            o[0] = val

    def held(l):
        return lambda layer, i: jnp.where(layer == l, i, nt - 1 if l == 0 else 0)

    in_specs = []
    for l in range(N_LAYERS):
        rows = held(l)
        in_specs.append(pl.BlockSpec((1, tr, c), functools.partial(lambda layer, i, me_ref, rows: (me_ref[0], rows(layer, i), 0), rows=rows)))
        for k in range(len(FLIPS)):
            in_specs.append(pl.BlockSpec(
                (1, tr, c), functools.partial(lambda layer, i, me_ref, rows, k: (k, rows(layer, i), 0), rows=rows, k=k)))
    tile = pl.BlockSpec((1, tr, c), lambda layer, i, me_ref: (layer, i, 0))
    operands = []
    for l in range(N_LAYERS):
        operands += [mine[l]] + [landed[l]] * len(FLIPS)
    return pl.pallas_call(
        body, name=name, out_shape=[jax.ShapeDtypeStruct(w.shape, F32)] * 4,
        grid_spec=pltpu.PrefetchScalarGridSpec(
            num_scalar_prefetch=1, grid=(N_LAYERS, nt), in_specs=in_specs + [tile] * 3, out_specs=[tile] * 4),
        compiler_params=_params("arbitrary", "arbitrary"),
    )(me, *operands, w, m, v)


def _adamw_replicated(parts, w, m, v, name):
    def body(p_ref, w_ref, m_ref, v_ref, g_ref, d_ref, m2_ref, v2_ref):
        g = p_ref[0]
        for j in range(1, N_DEV):
            g = g + p_ref[j]
        d, m2, v2 = _adamw(w_ref[...], g, m_ref[...], v_ref[...])
        g_ref[...], d_ref[...], m2_ref[...], v2_ref[...] = g, d, m2, v2

    return pl.pallas_call(body, name=name, out_shape=[jax.ShapeDtypeStruct(w.shape, F32)] * 4,
                          compiler_params=pltpu.CompilerParams(vmem_limit_bytes=VMEM_LIMIT_BYTES))(parts, w, m, v)


SMALL_USED = 80384


def _pack_small(vals, rider=None):
    pieces = [vals[n].reshape(-1) for n in SMALL_NAMES] + ([] if rider is None else [rider.reshape(1)])
    flat = jnp.concatenate(pieces)
    assert flat.shape[0] == SMALL_USED + (rider is not None)
    return jnp.concatenate([flat, jnp.zeros((SMALL_ROWS * D_MODEL - flat.shape[0],), F32)]).reshape(SMALL_ROWS, D_MODEL)


def _unpack_small(packed, shapes):
    flat, out, at = packed.reshape(-1), {}, 0
    for n in SMALL_NAMES:
        size = math.prod(shapes[n])
        out[n] = flat[at:at + size].reshape(shapes[n])
        at += size
    return out


GATHER_GROUPS = ((0, ("w_in",)), (0, ("w_out", "w_up", "conv_w")), (0, ("w_down",)),
                 (1, ("w_in",)), (1, ("w_out", "w_up", "conv_w")), (1, ("w_down",)))
ADAMW_TILE_ROWS = {"w_in": 320, "w_out": 128, "w_up": 256, "w_down": 256, "conv_w": 3}


def _assemble(name, land):
    if name == "w_in":
        return land.reshape(IN_COLS, D_MODEL)
    if name == "conv_w":
        return land.transpose(1, 0, 2).reshape(3, 2 * D_FF)
    if name == "w_up":
        return land
    return land.reshape(-1, D_MODEL)


def _start_gathers(wts, me):
    started, groups = jnp.zeros((1, 1), F32), []
    for gi, (l, names) in enumerate(GATHER_GROUPS):
        local = {"conv_w": lambda a: a, "w_in": lambda a: a.T.astype(BF16)}
        blocks = [local.get(n, lambda a: a.astype(BF16))(wts[n][l]) for n in names]
        handle, token = _flat_start(blocks, False, name=f"gather_start_{gi}")
        groups.append(types.SimpleNamespace(layer=l, names=names, blocks=blocks, handle=handle, got=None, index=gi))
        started = started + token

    def fetch(l, name, after):
        grp = next(gr for gr in groups if gr.layer == l and name in gr.names)
        if grp.got is None:
            lands = _flat_wait(grp.handle, after, name=f"gather_wait_{grp.index}")[1]
            grp.got = {}
            for n, blk, land in zip(grp.names, grp.blocks, lands):
                own = (me,) + (0,) * blk.ndim
                grp.got[n] = _assemble(n, lax.dynamic_update_slice(land, blk[None], own))
        return grp.got[name]

    return fetch, started


def kernel(x, pre_mix_norm, w_in, v_norm_g, v_norm_b, w_spatial, b_spatial, out_norm_a, out_norm_b, w_out, post_mix_norm, pre_ffn_norm, w_up, conv_w, conv_b, w_down, post_ffn_norm, loss_target, m_pre_mix_norm, m_w_in, m_v_norm_g, m_v_norm_b, m_w_spatial, m_b_spatial, m_out_norm_a, m_out_norm_b, m_w_out, m_post_mix_norm, m_pre_ffn_norm, m_w_up, m_conv_w, m_conv_b, m_w_down, m_post_ffn_norm, v_pre_mix_norm, v_w_in, v_v_norm_g, v_v_norm_b, v_w_spatial, v_b_spatial, v_out_norm_a, v_out_norm_b, v_w_out, v_post_mix_norm, v_pre_ffn_norm, v_w_up, v_conv_w, v_conv_b, v_w_down, v_post_ffn_norm):
    wts = dict(zip(WEIGHT_NAMES, (pre_mix_norm, w_in, v_norm_g, v_norm_b, w_spatial, b_spatial, out_norm_a, out_norm_b,
                                  w_out, post_mix_norm, pre_ffn_norm, w_up, conv_w, conv_b, w_down, post_ffn_norm)))
    mom1 = dict(zip(WEIGHT_NAMES, (m_pre_mix_norm, m_w_in, m_v_norm_g, m_v_norm_b, m_w_spatial, m_b_spatial, m_out_norm_a,
                                   m_out_norm_b, m_w_out, m_post_mix_norm, m_pre_ffn_norm, m_w_up, m_conv_w, m_conv_b,
                                   m_w_down, m_post_ffn_norm)))
    mom2 = dict(zip(WEIGHT_NAMES, (v_pre_mix_norm, v_w_in, v_v_norm_g, v_v_norm_b, v_w_spatial, v_b_spatial, v_out_norm_a,
                                   v_out_norm_b, v_w_out, v_post_mix_norm, v_pre_ffn_norm, v_w_up, v_conv_w, v_conv_b,
                                   v_w_down, v_post_ffn_norm)))
    mx, my, mc = _place()
    me = 4 * mx + 2 * my + mc

    fetch, started = _start_gathers(wts, me)
    scatters = []

    def emit(l, blocks):
        names = tuple(blocks)
        handle, token = _flat_start([blocks[n] for n in names], True, name=f"scatter_start_{l}_{'_'.join(names)}")
        scatters.append((l, names, handle))
        return token

    smalls = {}

    def emit_small(l, g, loss_local):
        pack = _pack_small({n: g[n] for n in SMALL_NAMES}, rider=loss_local if l == 0 else None)
        handle, token = _flat_start([pack], False, name=f"small_grads_start_{l}")
        smalls[l] = (pack, handle)
        return token

    _, dx, _ = _local_step(x[0], loss_target[0], fetch, wts, emit, emit_small, started)

    me_arr = jnp.reshape(me, (1,)).astype(jnp.int32)
    big_out = [{}, {}, {}, {}]

    def finish(group, after):
        mine, landed = {}, {}
        for l, names, handle in scatters:
            if names == group:
                sent, lands = _flat_wait(handle, after, name=f"scatter_wait_{l}_{'_'.join(names)}")
                for n, a, b in zip(names, sent, lands):
                    mine[l, n], landed[l, n] = a, b
        for n in group:
            flip = (lambda a: a.transpose(0, 2, 1)) if n == "w_in" else (lambda a: a)
            res = _adamw_sharded(me_arr, [mine[l, n] for l in range(N_LAYERS)], [landed[l, n] for l in range(N_LAYERS)],
                                 flip(wts[n]), flip(mom1[n]), flip(mom2[n]), ADAMW_TILE_ROWS[n], name=f"adamw_{n}")
            for kind in range(4):
                big_out[kind][n] = flip(res[kind])
        return res[0]

    done = smalls[0][1].thru[0]
    for group in [names for l, names, _ in scatters if l == 0][:-1]:
        done = finish(group, done)
    small_shapes = {n: wts[n].shape[1:] for n in SMALL_NAMES}
    per_layer = {}
    for l in reversed(range(N_LAYERS)):
        pack, handle = smalls[l]
        (landed,) = _flat_wait(handle, done, name=f"small_grads_wait_{l}")[1]
        everyone = lax.dynamic_update_slice(landed, pack[None], (me, 0, 0))
        packs = [_pack_small({n: t[n][l] for n in SMALL_NAMES}) for t in (wts, mom1, mom2)]
        res = _adamw_replicated(everyone, *packs, name=f"adamw_replicated_{l}")
        per_layer[l] = [_unpack_small(o, small_shapes) for o in res]
        done = res[0]
    loss = done.reshape(-1)[SMALL_USED]
    finish(scatters[-1][1], done)
    small_out = [{n: jnp.stack([per_layer[l][kind][n] for l in range(N_LAYERS)]) for n in SMALL_NAMES}
                 for kind in range(4)]

    outs = [loss, dx[None]]
    for kind in range(4):
        outs += [big_out[kind][n] if n in BIG_NAMES else small_out[kind][n] for n in WEIGHT_NAMES]
    return tuple(outs)
```

```python
import functools
import math
import types

import jax
import jax.numpy as jnp
from jax import lax
from jax.experimental import pallas as pl
from jax.experimental.pallas import tpu as pltpu

F32 = jnp.float32
BF16 = jnp.bfloat16

D_MODEL = 1024
A_WIDTH = 512
A_GROUPS = 4
CHUNK = 128
B_WIDTH = 512
HEAD_DIM = 64
N_HEADS = B_WIDTH // HEAD_DIM
ROT_DIM = 16
ROPE_THETA = 500000.0
BAND = 128
DILATIONS = (1, 4, 16)
IN_COLS = 2560
D_FF = 4096
EPS = 1e-6
NEG_INF = -1e30
N_DEV = 8
N_LAYERS = 2

ADAM_LR = 0.001
ADAM_B1 = 0.9
ADAM_B2 = 0.999
ADAM_EPS = 1e-08
ADAM_WD = 0.01
ADAM_STEP = 10

VMEM_LIMIT_BYTES = 56 * 1024 * 1024
MESH_ID = pl.DeviceIdType.MESH
ANY = pl.BlockSpec(memory_space=pl.ANY)

WEIGHT_NAMES = ("pre_mix_norm", "w_in", "v_norm_g", "v_norm_b", "w_spatial", "b_spatial", "out_norm_a", "out_norm_b",
                "w_out", "post_mix_norm", "pre_ffn_norm", "w_up", "conv_w", "conv_b", "w_down", "post_ffn_norm")
BIG_NAMES = ("w_in", "w_out", "w_up", "w_down", "conv_w")
SMALL_NAMES = tuple(n for n in WEIGHT_NAMES if n not in BIG_NAMES)

SMALL_ROWS = 80


def _params(*sem):
    return pltpu.CompilerParams(dimension_semantics=sem, vmem_limit_bytes=VMEM_LIMIT_BYTES)


def _dotg(a, b, ca, cb):
    return lax.dot_general(a.astype(BF16), b.astype(BF16), (((ca,), (cb,)), ((), ())), preferred_element_type=F32)


@jax.custom_vjp
def _bdot(a, b):
    return _dotg(a, b, 1, 0)


def _bdot_fwd(a, b):
    return _dotg(a, b, 1, 0), (a, b)


def _bdot_bwd(res, g):
    a, b = res
    return _dotg(g, b, 1, 1), _dotg(a, g, 0, 0)


_bdot.defvjp(_bdot_fwd, _bdot_bwd)


def _rms(x, g):
    return x * lax.rsqrt(jnp.mean(x * x, axis=-1, keepdims=True) + EPS) * g


def _layernorm(x, g, b):
    mu = jnp.mean(x, axis=-1, keepdims=True)
    xc = x - mu
    return xc * lax.rsqrt(jnp.mean(xc * xc, axis=-1, keepdims=True) + EPS) * g + b


def _gelu_erf(x):
    return x * (lax.erf(x * (1.0 / math.sqrt(2.0))) + 1.0) * 0.5


def _gelu_tanh(x):
    c = math.sqrt(2.0 / math.pi)
    return 0.5 * x * (1.0 + jnp.tanh(c * (x + 0.044715 * (x * x * x))))


def _gelu_tanh_and_slope(x):
    c, k = math.sqrt(2.0 / math.pi), 0.044715
    x2 = x * x
    t = jnp.tanh(c * (x + k * (x2 * x)))
    half_x, one_t = 0.5 * x, 1.0 + t
    return half_x * one_t, 0.5 * one_t + (half_x * (1.0 - t * t)) * (c + (3.0 * k * c) * x2)


def _rot_half(x):
    width = x.shape[1]
    lane = lax.broadcasted_iota(jnp.int32, x.shape, 1) % HEAD_DIM
    back = pltpu.roll(x, ROT_DIM // 2, 1)
    fwd = pltpu.roll(x, width - ROT_DIM // 2, 1)
    return jnp.where(lane < ROT_DIM // 2, -fwd, jnp.where(lane < ROT_DIM, back, 0.0))


def _split3(z):
    h0 = z.astype(BF16)
    r1 = z - h0.astype(F32)
    h1 = r1.astype(BF16)
    h2 = (r1 - h1.astype(F32)).astype(BF16)
    return h0, h1, h2


MATMUL_VMEM_BUDGET = 40 * 1024 * 1024


def _matmul_tiles(m, n, k, out_bytes):
    tn = n if n <= 1024 else (1280 if n % 1280 == 0 and n % 1024 else 1024)
    tk = k if k <= 1024 else (1280 if k % 1280 == 0 and k % 1024 else 1024)
    tm = m
    while tm > 256:
        blocks = 2 * 2 * (tm * tk + tk * tn) + 2 * out_bytes * tm * tn + (4 * tm * tn if k > tk else 0)
        if blocks <= MATMUL_VMEM_BUDGET and m % tm == 0:
            break
        tm //= 2
    return tm, tn, tk


def _matmul(a, b, *, mode, out_dtype, name, cols=None):
    wide = D_MODEL if cols is not None else None
    if mode == "nn":
        (m, k), (_, n) = a.shape, (b.shape if cols is None else (b.shape[1], cols[1] * wide))
    elif mode == "nt":
        (m, k), (n, _) = a.shape, (b.shape if cols is None else (b.shape[1], cols[1] * wide))
    else:
        (k, m), (_, n) = a.shape, b.shape
    tm, tn, tk = _matmul_tiles(m, n, k, jnp.dtype(out_dtype).itemsize)
    assert m % tm == 0 and n % tn == 0 and k % tk == 0, (name, m, n, k)
    nk = k // tk
    if mode == "nn":
        a_spec = pl.BlockSpec((tm, tk), lambda i, j, kk: (i, kk))
        b_spec = pl.BlockSpec((tk, tn), lambda i, j, kk: (kk, j))
        if cols is not None:
            assert tn == wide
            b_spec = pl.BlockSpec((None, tk, tn), lambda i, j, kk: (cols[0] + j, kk, 0))
        ca, cb = 1, 0
    elif mode == "nt":
        a_spec = pl.BlockSpec((tm, tk), lambda i, j, kk: (i, kk))
        b_spec = pl.BlockSpec((tn, tk), lambda i, j, kk: (j, kk))
        if cols is not None:
            assert tk == wide
            b_spec = pl.BlockSpec((None, tn, tk), lambda i, j, kk: (cols[0] + kk, j, 0))
        ca, cb = 1, 1
    else:
        a_spec = pl.BlockSpec((tk, tm), lambda i, j, kk: (kk, i))
        b_spec = pl.BlockSpec((tk, tn), lambda i, j, kk: (kk, j))
        ca, cb = 0, 0

    def body(a_ref, b_ref, o_ref, *acc):
        kk = pl.program_id(2)
        part = lax.dot_general(a_ref[...], b_ref[...], (((ca,), (cb,)), ((), ())), preferred_element_type=F32)
        if nk == 1:
            o_ref[...] = part.astype(o_ref.dtype)
            return
        acc_ref, = acc

        @pl.when(kk == 0)
        def _():
            acc_ref[...] = part

        @pl.when(kk > 0)
        def _():
            acc_ref[...] += part

        @pl.when(kk == nk - 1)
        def _():
            o_ref[...] = acc_ref[...].astype(o_ref.dtype)

    return pl.pallas_call(
        body, name=name, grid=(m // tm, n // tn, nk),
        in_specs=[a_spec, b_spec], out_specs=pl.BlockSpec((tm, tn), lambda i, j, kk: (i, j)),
        out_shape=jax.ShapeDtypeStruct((m, n), out_dtype),
        scratch_shapes=[pltpu.VMEM((tm, tn), F32)] if nk > 1 else [],
        compiler_params=_params("parallel", "parallel", "arbitrary"),
    )(a, b)


def _matmul_by_destination(a, b_lo, b_hi, *, name, tm=1024, tk=2048):
    (k, m), half = a.shape, N_DEV // 2
    assert b_lo.shape == b_hi.shape == (k, half * D_MODEL) and m % tm == 0 and k % tk == 0, name
    nk = k // tk

    def body(a_ref, lo_ref, hi_ref, o_ref, acc_ref):
        j, kk = pl.program_id(1), pl.program_id(2)

        def step(b_ref):
            part = lax.dot_general(a_ref[...], b_ref[...], (((0,), (0,)), ((), ())), preferred_element_type=F32)

            @pl.when(kk == 0)
            def _():
                acc_ref[...] = part

            @pl.when(kk > 0)
            def _():
                acc_ref[...] += part

        pl.when(j < half)(lambda: step(lo_ref))
        pl.when(j >= half)(lambda: step(hi_ref))

        @pl.when(kk == nk - 1)
        def _():
            o_ref[...] = acc_ref[...].astype(o_ref.dtype)

    lo_spec = pl.BlockSpec((tk, D_MODEL), lambda i, j, kk: (jnp.where(j < half, kk, nk - 1), jnp.minimum(j, half - 1)))
    hi_spec = pl.BlockSpec((tk, D_MODEL), lambda i, j, kk: (jnp.where(j >= half, kk, 0), jnp.maximum(j - half, 0)))
    return pl.pallas_call(
        body, name=name, grid=(m // tm, N_DEV, nk),
        in_specs=[pl.BlockSpec((tk, tm), lambda i, j, kk: (kk, i)), lo_spec, hi_spec],
        out_specs=pl.BlockSpec((None, tm, D_MODEL), lambda i, j, kk: (j, i, 0)),
        out_shape=jax.ShapeDtypeStruct((N_DEV, m, D_MODEL), BF16),
        scratch_shapes=[pltpu.VMEM((tm, D_MODEL), F32)],
        compiler_params=_params("parallel", "parallel", "arbitrary"),
    )(a, b_lo, b_hi)


LANES = 128


def _residues_to_rows(ref, scr, d):
    w = ref.shape[1] // d
    n = ref.shape[0]
    for r in range(d):
        for c in range(w // LANES):
            scr[c, pl.ds(r, n, stride=d), :] = ref[:, r * w + c * LANES:r * w + (c + 1) * LANES].astype(F32)
    return jnp.concatenate([scr[c] for c in range(w // LANES)], axis=1)


def _rows_to_residues(val, ref, scr, d):
    w = val.shape[1]
    n = ref.shape[0]
    for c in range(w // LANES):
        scr[c] = val[:, c * LANES:(c + 1) * LANES].astype(F32)
    for r in range(d):
        for c in range(w // LANES):
            ref[:, r * w + c * LANES:r * w + (c + 1) * LANES] = scr[c, pl.ds(r, n, stride=d), :].astype(ref.dtype)


HEAD_ROWS = 0


def _head_cols(z, whole_head=False):
    width = z.shape[1]
    a = lax.broadcasted_iota(jnp.int32, (width, LANES), 0)
    b = lax.broadcasted_iota(jnp.int32, (width, LANES), 1)
    pick = jnp.where((a // HEAD_DIM == b) if whole_head else (a == b * HEAD_DIM), 1.0, 0.0).astype(BF16)
    out = None
    for part in _split3(z):
        t = lax.dot_general(part, pick, (((1,), (0,)), ((), ())), preferred_element_type=F32)
        out = t if out is None else out + t
    return out


def _head_rows_block(cols):
    return cols.T[:N_HEADS, :]


def _rowwise(fn, rows, consts, out_rows, out_acc, *, ts, name):
    rows = [tuple(r) + (1,) * (4 - len(r)) for r in rows]
    out_rows = [tuple(o) + (1,) * (3 - len(o)) for o in out_rows]
    s = rows[0][0].shape[0] * rows[0][3]
    assert s % ts == 0, (name, s, ts)
    n_rows, n_in = len(rows), len(rows) + len(consts)
    n_row = len(out_rows)
    n_out = n_row + len(out_acc)
    moved = [(idx, w) for idx, (_, w, _, d) in enumerate(rows) if d > 1]
    moved += [(n_rows + idx, w) for idx, (w, _, d) in enumerate(out_rows) if d > 1]

    def body(*refs):
        scratch = dict(zip([key for key, _ in moved], refs[n_in + n_out:]))
        vals = []
        for idx, r in enumerate(refs[:n_in]):
            d = rows[idx][3] if idx < n_rows else 1
            vals.append(r[...] if d == 1 else _residues_to_rows(r, scratch[idx], d))
        row_vals, acc_vals = fn(*vals)
        for idx, (r, v) in enumerate(zip(refs[n_in:n_in + n_row], row_vals)):
            d = out_rows[idx][2]
            if d == 1:
                r[...] = v.astype(r.dtype)
            elif d == HEAD_ROWS:
                r[...] = _head_rows_block(v)
            else:
                _rows_to_residues(v, r, scratch[n_rows + idx], d)
        first = pl.program_id(0) == 0
        for r, v in zip(refs[n_in + n_row:n_in + n_out], acc_vals):
            @pl.when(first)
            def _(r=r, v=v):
                r[...] = v

            @pl.when(jnp.logical_not(first))
            def _(r=r, v=v):
                r[...] += v

    in_specs = [pl.BlockSpec((ts // d, d * w), functools.partial(lambda i, cb: (i, cb), cb=cb)) for _, w, cb, d in rows]
    in_specs += [pl.BlockSpec(c.shape, lambda i: (0, 0)) for c in consts]
    out_specs = [pl.BlockSpec((N_HEADS, ts), lambda i: (0, i)) if d == HEAD_ROWS else
                 pl.BlockSpec((ts // d, d * w), lambda i: (i, 0)) for w, _, d in out_rows]
    out_specs += [pl.BlockSpec(sh, lambda i: (0, 0)) for sh in out_acc]
    out_shape = [jax.ShapeDtypeStruct((N_HEADS, s) if d == HEAD_ROWS else (s // d, d * w), dt) for w, dt, d in out_rows]
    out_shape += [jax.ShapeDtypeStruct(sh, F32) for sh in out_acc]
    outs = pl.pallas_call(
        body, name=name, grid=(s // ts,), in_specs=in_specs, out_specs=out_specs, out_shape=out_shape,
        scratch_shapes=[pltpu.VMEM((w // LANES, ts, LANES), F32) for _, w in moved],
        compiler_params=_params("arbitrary" if out_acc else "parallel"),
    )(*[a for a, _, _, _ in rows], *consts)
    return outs[:n_row], outs[n_row:]


def _full(a, d=1):
    return (a, a.shape[1] // d, 0, d)


def _gate_fn(zu, zv, vg, vb, ws0, ws1, ws2, ws3, bfull, ga):
    u = _gelu_erf(zu)
    vn = _layernorm(_gelu_erf(zv), vg, vb)
    p = lax.broadcasted_iota(jnp.int32, (CHUNK, CHUNK), 0)
    q = lax.broadcasted_iota(jnp.int32, (CHUNK, CHUNK), 1)
    tril = jnp.where(q <= p, 1.0, 0.0)
    group = lax.broadcasted_iota(jnp.int32, (1, A_WIDTH), 1) // CHUNK
    sg = bfull
    for g, w in enumerate((ws0, ws1, ws2, ws3)):
        sg = sg + _bdot(w * tril, jnp.where(group == g, vn, 0.0))
    return _rms(u * sg, ga)


GATE_TS = 2 * CHUNK


def _chunk_rows(rows):
    return [slice(c * CHUNK, (c + 1) * CHUNK) for c in range(rows // CHUNK)]


def _bias_reduce(dbf, name):
    def body(x_ref, o_ref):
        lane = lax.broadcasted_iota(jnp.int32, (CHUNK, CHUNK), 1)
        out = jnp.zeros((CHUNK, CHUNK), F32)
        for g in range(A_GROUPS):
            out = jnp.where(lane == g, jnp.sum(x_ref[:, g * CHUNK:(g + 1) * CHUNK], axis=1, keepdims=True), out)
        o_ref[...] = out

    return pl.pallas_call(body, name=name, out_shape=jax.ShapeDtypeStruct((CHUNK, CHUNK), F32))(dbf)


def _pair_mask(hh):
    lane = lax.broadcasted_iota(jnp.int32, (1, 2 * HEAD_DIM), 1)
    return (lane >= HEAD_DIM * hh) & (lane < HEAD_DIM * (hh + 1))


def _lane_pick(x2, lm):
    return jnp.max(jnp.where(lm, x2, -jnp.inf), axis=1, keepdims=True)


SCALE = HEAD_DIM ** -0.5


def _blocks_per_step(nb):
    return next(n for n in (4, 2, 1) if nb % n == 0)


def _units(nblk):
    return [(b, hp, hh) for b in range(nblk) for hp in range(N_HEADS // 2) for hh in range(2)]


def _attn_specs(nb, nblk):
    cur = pl.BlockSpec((nblk * BAND, B_WIDTH), lambda r, j: (j, r))
    prev = pl.BlockSpec((BAND, B_WIDTH), lambda r, j: (jnp.maximum(nblk * j - 1, 0), r))
    nxt = pl.BlockSpec((BAND, B_WIDTH), lambda r, j: (jnp.minimum(nblk * (j + 1), nb - 1), r))
    return cur, prev, nxt


def _pair_cols(hp):
    return slice(2 * HEAD_DIM * hp, 2 * HEAD_DIM * (hp + 1))


def _rows(b):
    return slice(b * BAND, (b + 1) * BAND)


def _with_prev(cur_ref, prev_ref, b, sl):
    if b == 0:
        return jnp.concatenate([prev_ref[:, sl], cur_ref[_rows(0), sl]], axis=0)
    return cur_ref[(b - 1) * BAND:(b + 1) * BAND, sl]


def _with_next(cur_ref, next_ref, b, sl, nblk):
    if b == nblk - 1:
        return jnp.concatenate([cur_ref[_rows(b), sl], next_ref[:, sl]], axis=0)
    return cur_ref[b * BAND:(b + 2) * BAND, sl]


def _band_valid(other_exists):
    row = lax.broadcasted_iota(jnp.int32, (BAND, 2 * BAND), 0)
    col = lax.broadcasted_iota(jnp.int32, (BAND, 2 * BAND), 1)
    return (col >= row) & (col <= row + BAND), other_exists


def _masked(lm, x):
    return jnp.where(lm, x, jnp.zeros_like(x))


def _attn_fwd(q, k, v, d, name):
    nb = q.shape[0] // BAND
    nblk = _blocks_per_step(nb)
    units = _units(nblk)
    cur, prev, _ = _attn_specs(nb, nblk)

    def body(q_ref, kc_ref, kp_ref, vc_ref, vp_ref, o_ref, l_ref):
        band, has_prev = _band_valid(pl.program_id(1) > 0)
        col = lax.broadcasted_iota(jnp.int32, (BAND, 2 * BAND), 1)
        valid = [band & ((col >= BAND) | has_prev)] + [band] * (nblk - 1)

        def scores(unit):
            b, hp, hh = unit
            sl = _pair_cols(hp)
            return _dotg(_masked(_pair_mask(hh), q_ref[_rows(b), sl]), _with_prev(kc_ref, kp_ref, b, sl), 1, 1)

        ahead, half = scores(units[0]), None
        for idx, (b, hp, hh) in enumerate(units):
            raw = ahead
            if idx + 1 < len(units):
                ahead = scores(units[idx + 1])
            sl, lm = _pair_cols(hp), _pair_mask(hh)
            s = jnp.where(valid[b], raw * SCALE, NEG_INF)
            m = jnp.max(s, axis=1, keepdims=True)
            p = jnp.exp(s - m)
            den = jnp.sum(p, axis=1, keepdims=True)
            o = _dotg(p, _with_prev(vc_ref, vp_ref, b, sl), 1, 0) / den
            lse = m + jnp.log(den)
            if hh == 0:
                half = (o, lse)
            else:
                o_ref[_rows(b), sl] = jnp.where(lm, o, half[0])
                l_ref[_rows(b), sl] = jnp.where(lm, lse, half[1])

    return pl.pallas_call(
        body, name=name, grid=(d, nb // nblk), in_specs=[cur, cur, prev, cur, prev], out_specs=[cur, cur],
        out_shape=[jax.ShapeDtypeStruct(q.shape, F32), jax.ShapeDtypeStruct(q.shape, F32)],
        compiler_params=_params("parallel", "parallel"),
    )(q, k, k, v, v)


def _attn_bwd_q(q, k, v, do, lse, delta, d, name):
    nb = q.shape[0] // BAND
    nblk = _blocks_per_step(nb)
    units = _units(nblk)
    cur, prev, _ = _attn_specs(nb, nblk)
    per_head = pl.BlockSpec((nblk * BAND, LANES), lambda r, j: (j, r))

    def body(q_ref, kc_ref, kp_ref, vc_ref, vp_ref, do_ref, l_ref, dl_ref, dq_ref):
        band, has_prev = _band_valid(pl.program_id(1) > 0)
        col = lax.broadcasted_iota(jnp.int32, (BAND, 2 * BAND), 1)
        valid = [band & ((col >= BAND) | has_prev)] + [band] * (nblk - 1)

        def products(unit):
            b, hp, hh = unit
            sl, lm = _pair_cols(hp), _pair_mask(hh)
            return (_dotg(_masked(lm, q_ref[_rows(b), sl]), _with_prev(kc_ref, kp_ref, b, sl), 1, 1),
                    _dotg(_masked(lm, do_ref[_rows(b), sl]), _with_prev(vc_ref, vp_ref, b, sl), 1, 1))

        ahead, half = products(units[0]), None
        for idx, (b, hp, hh) in enumerate(units):
            raw, dp = ahead
            if idx + 1 < len(units):
                ahead = products(units[idx + 1])
            sl, lm = _pair_cols(hp), _pair_mask(hh)
            s = jnp.where(valid[b], raw * SCALE, NEG_INF)
            head = lax.broadcasted_iota(jnp.int32, (1, LANES), 1) == 2 * hp + hh
            p = jnp.exp(s - _lane_pick(l_ref[_rows(b), :], head))
            ds = p * (dp - _lane_pick(dl_ref[_rows(b), :], head))
            dq = _dotg(ds, _with_prev(kc_ref, kp_ref, b, sl), 1, 0) * SCALE
            if hh == 0:
                half = dq
            else:
                dq_ref[_rows(b), sl] = jnp.where(lm, dq, half).astype(dq_ref.dtype)

    return pl.pallas_call(
        body, name=name, grid=(d, nb // nblk),
        in_specs=[cur, cur, prev, cur, prev, cur, per_head, per_head], out_specs=cur,
        out_shape=jax.ShapeDtypeStruct(q.shape, BF16),
        compiler_params=_params("parallel", "parallel"),
    )(q, k, k, v, v, do, lse, delta)


def _attn_bwd_kv(q, k, v, do, lse_t, delta_t, d, name):
    nb = q.shape[0] // BAND
    nblk = _blocks_per_step(nb)
    units = _units(nblk)
    cur, _, nxt = _attn_specs(nb, nblk)
    t_cur = pl.BlockSpec((1, N_HEADS, nblk * BAND), lambda r, j: (r, 0, j))
    t_nxt = pl.BlockSpec((1, N_HEADS, BAND), lambda r, j: (r, 0, jnp.minimum(nblk * (j + 1), nb - 1)))

    def body(k_ref, v_ref, qc_ref, qn_ref, doc_ref, don_ref, lc_ref, ln_ref, dlc_ref, dln_ref, dk_ref, dv_ref):
        band, has_next = _band_valid(pl.program_id(1) < nb // nblk - 1)
        col = lax.broadcasted_iota(jnp.int32, (BAND, 2 * BAND), 1)
        valid = [band] * (nblk - 1) + [band & ((col < BAND) | has_next)]

        def head_row(c_ref, n_ref, b, h):
            if b == nblk - 1:
                return jnp.concatenate([c_ref[0, h:h + 1, b * BAND:(b + 1) * BAND], n_ref[0, h:h + 1, :]], axis=1)
            return c_ref[0, h:h + 1, b * BAND:(b + 2) * BAND]

        def products(unit):
            b, hp, hh = unit
            sl, lm = _pair_cols(hp), _pair_mask(hh)
            return (_dotg(_masked(lm, k_ref[_rows(b), sl]), _with_next(qc_ref, qn_ref, b, sl, nblk), 1, 1),
                    _dotg(_masked(lm, v_ref[_rows(b), sl]), _with_next(doc_ref, don_ref, b, sl, nblk), 1, 1))

        ahead, half = products(units[0]), None
        for idx, (b, hp, hh) in enumerate(units):
            raw, dpt = ahead
            if idx + 1 < len(units):
                ahead = products(units[idx + 1])
            sl, lm, h = _pair_cols(hp), _pair_mask(hh), 2 * hp + hh
            st = jnp.where(valid[b], raw * SCALE, NEG_INF)
            pt = jnp.exp(st - head_row(lc_ref, ln_ref, b, h))
            dv = _dotg(pt, _with_next(doc_ref, don_ref, b, sl, nblk), 1, 0)
            dst = pt * (dpt - head_row(dlc_ref, dln_ref, b, h))
            dk = _dotg(dst, _with_next(qc_ref, qn_ref, b, sl, nblk), 1, 0) * SCALE
            if hh == 0:
                half = (dk, dv)
            else:
                dk_ref[_rows(b), sl] = jnp.where(lm, dk, half[0]).astype(dk_ref.dtype)
                dv_ref[_rows(b), sl] = jnp.where(lm, dv, half[1]).astype(dv_ref.dtype)

    return pl.pallas_call(
        body, name=name, grid=(d, nb // nblk),
        in_specs=[cur, cur, cur, nxt, cur, nxt, t_cur, t_nxt, t_cur, t_nxt], out_specs=[cur, cur],
        out_shape=[jax.ShapeDtypeStruct(q.shape, BF16), jax.ShapeDtypeStruct(q.shape, BF16)],
        compiler_params=_params("parallel", "parallel"),
    )(k, v, q, q, do, do, lse_t, lse_t, delta_t, delta_t)


def _spread_rows(a, d):
    return a.reshape(N_HEADS, a.shape[1] // d, d).transpose(2, 0, 1)


FF_TS = 512
FF_TC = 1024
FF_SUB = 256
HALO = 8
HALO_BF16 = 16
UP_BLOCKS = D_MODEL // FF_TC


def _conv3(ext, w, b):
    return b + w[0:1, :] * pltpu.roll(ext, 2, 0) + w[1:2, :] * pltpu.roll(ext, 1, 0) + w[2:3, :] * ext


def _ffn_specs(s, cols_first):
    nrb = s // FF_TS
    per, per16 = FF_TS // HALO, FF_TS // HALO_BF16

    def mk(block, fn):
        if cols_first:
            return pl.BlockSpec(block, lambda j, i: fn(i, j))
        return pl.BlockSpec(block, lambda i, j: fn(i, j))

    specs = types.SimpleNamespace(
        nrb=nrb, ncb=D_FF // FF_TC,
        row=mk((FF_TS, FF_TC), lambda i, j: (i, j)),
        before=mk((HALO, FF_TC), lambda i, j: (jnp.maximum(i * per - 1, 0), j)),
        after=mk((HALO, FF_TC), lambda i, j: (jnp.minimum((i + 1) * per, nrb * per - 1), j)),
        w=mk((3, FF_TC), lambda i, j: (0, j)),
        b=mk((1, FF_TC), lambda i, j: (0, j)),
        part=mk((HALO, FF_TC), lambda i, j: (i, j)),
        act=mk((FF_TS, D_MODEL), lambda i, j: (i, 0)),
        act_before=mk((HALO_BF16, D_MODEL), lambda i, j: (jnp.maximum(i * per16 - 1, 0), 0)),
        act_after=mk((HALO_BF16, D_MODEL), lambda i, j: (jnp.minimum((i + 1) * per16, nrb * per16 - 1), 0)),
        up_gate=mk((None, D_MODEL, FF_TC), lambda i, j: (j // UP_BLOCKS, 0, j % UP_BLOCKS)),
        up_val=mk((None, D_MODEL, FF_TC), lambda i, j: (N_DEV // 2 + j // UP_BLOCKS, 0, j % UP_BLOCKS)),
        down=mk((FF_TC, D_MODEL), lambda i, j: (j, 0)),
    )
    return specs


def _ffn_up_geglu(h, w_up, wg, wv, bg, bv, name):
    s = h.shape[0]
    sp = _ffn_specs(s, True)

    def body(h_ref, hb_ref, ugw_ref, uvw_ref, wg_ref, wv_ref, bg_ref, bv_ref, ug_ref, uv_ref, y_ref):
        keep = jnp.where(pl.program_id(1) > 0, 1.0, 0.0).astype(BF16)
        hext = jnp.concatenate([hb_ref[...] * keep, h_ref[...]], axis=0)
        eg = _dotg(hext, ugw_ref[...], 1, 0)
        ug_ref[...] = eg[HALO_BF16:, :]
        act = _gelu_tanh(_conv3(eg, wg_ref[...], bg_ref[...])[HALO_BF16:, :])
        ev = _dotg(hext, uvw_ref[...], 1, 0)
        uv_ref[...] = ev[HALO_BF16:, :]
        y_ref[...] = (act * _conv3(ev, wv_ref[...], bv_ref[...])[HALO_BF16:, :]).astype(y_ref.dtype)

    return pl.pallas_call(
        body, name=name, grid=(sp.ncb, sp.nrb),
        in_specs=[sp.act, sp.act_before, sp.up_gate, sp.up_val, sp.w, sp.w, sp.b, sp.b],
        out_specs=[sp.row, sp.row, sp.row],
        out_shape=[jax.ShapeDtypeStruct((s, D_FF), F32), jax.ShapeDtypeStruct((s, D_FF), F32),
                   jax.ShapeDtypeStruct((s, D_FF), BF16)],
        compiler_params=_params("parallel", "parallel"),
    )(h, h, w_up, w_up, wg, wv, bg, bv)


def _sum_parts(parts, name):
    n = parts.shape[0] // HALO

    def body(p_ref, o_ref):
        acc = p_ref[0:HALO, :]
        for t in range(1, n):
            acc = acc + p_ref[t * HALO:(t + 1) * HALO, :]
        o_ref[...] = acc

    return pl.pallas_call(body, name=name, out_shape=jax.ShapeDtypeStruct((HALO, parts.shape[1]), F32))(parts)


def _ffn_geglu_bwd(ug, uv, df, w_down, w_up, wg, wv, bg, bv, name):
    s = ug.shape[0]
    sp = _ffn_specs(s, False)
    nrb = sp.nrb
    rows = FF_TS + 2 * HALO
    lo, hi = HALO, HALO + FF_TS

    def body(ug_ref, uv_ref, hg_ref, hv_ref, ng_ref, nv_ref, df_ref, dfn_ref, dw_ref, ugw_ref, uvw_ref,
             wg_ref, wv_ref, bg_ref, bv_ref, dug_ref, duv_ref, dh_ref, dwg_ref, dwv_ref):
        i, j = pl.program_id(0), pl.program_id(1)
        keep_top = jnp.where(i > 0, 1.0, 0.0)
        keep_bot = jnp.where(i < nrb - 1, 1.0, 0.0).astype(BF16)
        dfe = jnp.concatenate([df_ref[...], dfn_ref[...] * keep_bot], axis=0)

        def back(dc, e, w, du_ref, sl):
            up1 = pltpu.roll(dc, rows - 1, 0)
            up2 = pltpu.roll(dc, rows - 2, 0)
            du = (w[2:3, :] * dc + w[1:2, :] * up1 + w[0:1, :] * up2)[lo:hi, :].astype(BF16)
            du_ref[:, sl] = du
            p1, p2 = up1 * e, up2 * e
            colsum = lambda p: jnp.sum(p[lo:hi, :], axis=0, keepdims=True)
            row = lambda p, t: p[t:t + 1, :]
            d_w1 = colsum(p1) + row(p1, lo - 1) - row(p1, hi - 1)
            d_w0 = colsum(p2) + row(p2, lo - 2) + row(p2, lo - 1) - row(p2, hi - 2) - row(p2, hi - 1)
            sums = [d_w0, d_w1, colsum(dc * e), colsum(dc), jnp.zeros((HALO - 4, FF_SUB), F32)]
            return du, jnp.concatenate(sums, axis=0)

        dh = None
        n_sub = FF_TC // FF_SUB
        cols = [slice(c * FF_SUB, (c + 1) * FF_SUB) for c in range(n_sub)]
        d_act = lambda c: _dotg(dfe, dw_ref[cols[c], :], 1, 1)[:FF_TS + HALO, :]
        ahead = d_act(0)
        for c in range(n_sub):
            sl, dy = cols[c], ahead
            if c + 1 < n_sub:
                ahead = d_act(c + 1)
            dye = jnp.concatenate([jnp.zeros((HALO, FF_SUB), F32), dy], axis=0)
            eg = jnp.concatenate([hg_ref[:, sl] * keep_top, ug_ref[:, sl], ng_ref[:, sl]], axis=0)
            ev = jnp.concatenate([hv_ref[:, sl] * keep_top, uv_ref[:, sl], nv_ref[:, sl]], axis=0)
            wg_, wv_ = wg_ref[:, sl], wv_ref[:, sl]
            gate = _conv3(eg, wg_, bg_ref[:, sl])
            val = _conv3(ev, wv_, bv_ref[:, sl])
            act, slope = _gelu_tanh_and_slope(gate)
            dug, dwg_ref[:, sl] = back((dye * val) * slope, eg, wg_, dug_ref, sl)
            duv, dwv_ref[:, sl] = back(dye * act, ev, wv_, duv_ref, sl)
            term = _dotg(dug, ugw_ref[:, sl], 1, 1) + _dotg(duv, uvw_ref[:, sl], 1, 1)
            dh = term if dh is None else dh + term

        @pl.when(j == 0)
        def _():
            dh_ref[...] = dh

        @pl.when(j > 0)
        def _():
            dh_ref[...] += dh

    parts = jax.ShapeDtypeStruct((nrb * HALO, D_FF), F32)
    dug, duv, dh, pg, pv = pl.pallas_call(
        body, name=name, grid=(nrb, sp.ncb),
        in_specs=[sp.row, sp.row, sp.before, sp.before, sp.after, sp.after, sp.act, sp.act_after, sp.down,
                  sp.up_gate, sp.up_val, sp.w, sp.w, sp.b, sp.b],
        out_specs=[sp.row, sp.row, sp.act, sp.part, sp.part],
        out_shape=[jax.ShapeDtypeStruct((s, D_FF), BF16), jax.ShapeDtypeStruct((s, D_FF), BF16),
                   jax.ShapeDtypeStruct((s, D_MODEL), F32), parts, parts],
        compiler_params=_params("parallel", "arbitrary"),
    )(ug, uv, ug, uv, ug, uv, df, df, w_down, w_up, w_up, wg, wv, bg, bv)
    return dug, duv, dh, _sum_parts(pg, name=name + "_sum_gate"), _sum_parts(pv, name=name + "_sum_val")


def _rope_tables(s):
    inv = ROPE_THETA ** (-jnp.arange(0, ROT_DIM, 2, dtype=F32) / ROT_DIM)
    ang = jnp.arange(s, dtype=F32)[:, None] * inv[None, :]
    cos8, sin8 = jnp.cos(ang), jnp.sin(ang)
    rest = HEAD_DIM - ROT_DIM
    cos_h = jnp.concatenate([cos8, cos8, jnp.ones((s, rest), F32)], axis=1)
    sin_h = jnp.concatenate([sin8, sin8, jnp.zeros((s, rest), F32)], axis=1)
    return jnp.tile(cos_h, (1, LANES // HEAD_DIM)), jnp.tile(sin_h, (1, LANES // HEAD_DIM))


def _all_heads(table):
    return jnp.concatenate([table] * (B_WIDTH // LANES), axis=1)


def _layer_fwd(x, w, cos, sin, last):
    sv = types.SimpleNamespace(x=x)
    (sv.h1,), _ = _rowwise(lambda xb, g: ((_rms(xb, g),), ()), [_full(x)], [w.g_pre], [(D_MODEL, BF16)], [],
                           ts=512, name="pre_mix_norm")
    sv.proj = _matmul(sv.h1, w.big("w_in", sv.h1), mode="nt", out_dtype=F32, name="proj")

    gate_consts = [w.vg, w.vb, *w.ws, w.bfull, w.ga]
    def gate_fwd_fn(zu, zv, *consts):
        chunks = [_gate_fn(zu[r], zv[r], *consts) for r in _chunk_rows(zu.shape[0])]
        return (jnp.concatenate(chunks, axis=0),), ()

    (na,), _ = _rowwise(gate_fwd_fn, [(sv.proj, A_WIDTH, 0), (sv.proj, A_WIDTH, 1)], gate_consts,
                        [(A_WIDTH, BF16)], [], ts=GATE_TS, name="gate_fwd")

    def rope_fn(qr, kr, vr, cs, sn):
        cs, sn = _all_heads(cs), _all_heads(sn)
        return (qr * cs + _rot_half(qr) * sn, kr * cs + _rot_half(kr) * sn, vr), ()

    def rope_all(qr, kr, vr, cs, sn):
        return rope_fn(qr, kr, vr, cs, sn)[0] * len(DILATIONS), ()

    qkv, _ = _rowwise(
        rope_all, [(sv.proj, B_WIDTH, 2), (sv.proj, B_WIDTH, 3), (sv.proj, B_WIDTH, 4), _full(cos), _full(sin)], [],
        [(B_WIDTH, BF16, d) for d in DILATIONS for _ in range(3)], [], ts=512, name="rope_fwd")
    sv.qkv = {d: qkv[3 * i:3 * i + 3] for i, d in enumerate(DILATIONS)}

    branch = []
    for d in DILATIONS:
        o, l = _attn_fwd(*sv.qkv[d], d, name=f"attn_fwd_d{d}")
        branch += [_full(o, d), _full(l, d)]

    def combine_fn(o1, l1, o2, l2, o3, l3, nab, gb):
        m = jnp.maximum(jnp.maximum(l1, l2), l3)
        e1, e2, e3 = jnp.exp(l1 - m), jnp.exp(l2 - m), jnp.exp(l3 - m)
        den = e1 + e2 + e3
        ob = (e1 / den) * o1 + (e2 / den) * o2 + (e3 / den) * o3
        mixed = jnp.concatenate([nab, _rms(ob, gb).astype(BF16)], axis=1)
        lse = _head_cols(m + jnp.log(den))
        return (mixed, ob, lse) + (lse,) * len(DILATIONS), ()

    (sv.mixed, sv.ob, sv.lse_rows, *lses), _ = _rowwise(
        combine_fn, branch + [_full(na)], [w.gb],
        [(D_MODEL, BF16), (B_WIDTH, F32), (LANES, F32, HEAD_ROWS)] + [(LANES, F32, d) for d in DILATIONS], [],
        ts=512, name="combine")
    sv.lse = dict(zip(DILATIONS, lses))
    sv.y = _matmul(sv.mixed, w.big("w_out", sv.mixed), mode="nn", out_dtype=F32, name="mix_out")

    def mid_fn(xb, yb, g1, g2):
        x1 = xb + _rms(yb, g1)
        return (x1, _rms(x1, g2)), ()

    (sv.x1, sv.h2), _ = _rowwise(mid_fn, [_full(x), _full(sv.y)], [w.g_pm, w.g_pf], [(D_MODEL, F32), (D_MODEL, BF16)], [],
                                 ts=512, name="post_mix_norm")
    conv_w = w.big("conv_w", sv.h2)
    sv.ug, sv.uv, sv.yff = _ffn_up_geglu(sv.h2, w.big("w_up", sv.h2), conv_w[:, :D_FF], conv_w[:, D_FF:],
                                         w.cb_g, w.cb_v, name="ffn_up_geglu")
    sv.f = _matmul(sv.yff, w.big("w_down", sv.yff), mode="nn", out_dtype=F32, name="ffn_down")
    if last:
        return None, sv
    (x2,), _ = _rowwise(lambda xb, fb, g: ((xb + _rms(fb, g),), ()), [_full(sv.x1), _full(sv.f)], [w.g_post],
                        [(D_MODEL, F32)], [], ts=512, name="post_ffn_norm")
    return x2, sv


def _layer_bwd(dx2, sv, w, cos, sin, emit):
    g = {}

    def post_fn(fb, dxb, gp):
        _, vjp = jax.vjp(_rms, fb, gp)
        df, dg = vjp(dxb)
        return (df,), (dg,)

    (df,), (g["post_ffn_norm"],) = _rowwise(post_fn, [_full(sv.f), _full(dx2)], [w.g_post], [(D_MODEL, BF16)],
                                            [(1, D_MODEL)], ts=512, name="post_ffn_norm_bwd")
    big = {"w_down": _matmul(sv.yff, df, mode="tn", out_dtype=BF16, name="ffn_down_dw").reshape(N_DEV, -1, D_MODEL)}
    conv_w = w.big("conv_w", df)
    dug, duv, dh2, dwg, dwv = _ffn_geglu_bwd(sv.ug, sv.uv, df, w.big("w_down", df), w.big("w_up", df),
                                             conv_w[:, :D_FF], conv_w[:, D_FF:], w.cb_g, w.cb_v, name="ffn_geglu_bwd")
    big["conv_w"] = jnp.concatenate([dwg[0:3], dwv[0:3]], axis=1).reshape(3, N_DEV, D_MODEL).transpose(1, 0, 2)
    g["conv_b"] = jnp.concatenate([dwg[3], dwv[3]], axis=0)
    big["w_up"] = _matmul_by_destination(sv.h2, dug, duv, name="ffn_up_dw")
    g_pm = w.g_pm + emit(big)

    def mid_fn(x1b, yb, dhb, dxb, g1, g2):
        _, vjp2 = jax.vjp(_rms, x1b, g2)
        dx1h, dg2 = vjp2(dhb)
        dx1 = dxb + dx1h
        _, vjp1 = jax.vjp(_rms, yb, g1)
        dy, dg1 = vjp1(dx1)
        return (dx1, dy), (dg1, dg2)

    (dx1, dy), (g["post_mix_norm"], g["pre_ffn_norm"]) = _rowwise(
        mid_fn, [_full(sv.x1), _full(sv.y), _full(dh2), _full(dx2)], [g_pm, w.g_pf],
        [(D_MODEL, F32), (D_MODEL, BF16)], [(1, D_MODEL), (1, D_MODEL)], ts=256, name="post_mix_norm_bwd")
    dmixed = _matmul(dy, w.big("w_out", dy), mode="nt", out_dtype=F32, name="mix_out_dx")
    dw_out = _matmul(sv.mixed, dy, mode="tn", out_dtype=BF16, name="mix_out_dw").reshape(N_DEV, -1, D_MODEL)
    g_b = w.gb + emit({"w_out": dw_out})

    def attn_out_fn(obb, dmb, gb):
        _, vjp = jax.vjp(_rms, obb, gb)
        do, dgb = vjp(dmb)
        delta = _head_cols(do * obb, whole_head=True)
        return (delta,) + (do,) * len(DILATIONS) + (delta,) * len(DILATIONS), (dgb,)

    (delta_rows, *outs), (g["out_norm_b"],) = _rowwise(
        attn_out_fn, [_full(sv.ob), (dmixed, B_WIDTH, 1)], [g_b],
        [(LANES, F32, HEAD_ROWS)] + [(B_WIDTH, BF16, d) for d in DILATIONS] + [(LANES, F32, d) for d in DILATIONS],
        [(1, B_WIDTH)], ts=512, name="attn_out_bwd")
    do = dict(zip(DILATIONS, outs[:len(DILATIONS)]))
    delta = dict(zip(DILATIONS, outs[len(DILATIONS):]))
    parts = {"q": [], "k": [], "v": []}
    for d in DILATIONS:
        qv, kv, vv = sv.qkv[d]
        dq = _attn_bwd_q(qv, kv, vv, do[d], sv.lse[d], delta[d], d, name=f"attn_bwd_q_d{d}")
        dk, dv = _attn_bwd_kv(qv, kv, vv, do[d], _spread_rows(sv.lse_rows, d), _spread_rows(delta_rows, d), d,
                              name=f"attn_bwd_kv_d{d}")
        parts["q"].append(_full(dq, d))
        parts["k"].append(_full(dk, d))
        parts["v"].append(_full(dv, d))

    def rope_bwd_fn(q1, q2, q3, k1, k2, k3, v1, v2, v3, cs, sn):
        cs, sn = _all_heads(cs), _all_heads(sn)

        def back(t):
            return t * cs - _rot_half(t * sn)
        return (jnp.concatenate([back(q1 + q2 + q3), back(k1 + k2 + k3), v1 + v2 + v3], axis=1),), ()

    (dzb,), _ = _rowwise(rope_bwd_fn, parts["q"] + parts["k"] + parts["v"] + [_full(cos), _full(sin)], [],
                         [(3 * B_WIDTH, BF16)], [], ts=256, name="rope_bwd")

    gate_consts = [w.vg, w.vb, *w.ws, w.bfull, w.ga]

    def gate_bwd_fn(zu, zv, dna, dzb_rows, *consts):
        dz, sums = [], None
        for r in _chunk_rows(zu.shape[0]):
            _, vjp = jax.vjp(_gate_fn, zu[r], zv[r], *consts)
            grads = vjp(dna[r])
            dz.append(jnp.concatenate([grads[0].astype(BF16), grads[1].astype(BF16), dzb_rows[r]], axis=1))
            sums = grads[2:] if sums is None else tuple(a + b for a, b in zip(sums, grads[2:]))
        return (jnp.concatenate(dz, axis=0),), tuple(sums)

    (dproj,), gsmall = _rowwise(
        gate_bwd_fn, [(sv.proj, A_WIDTH, 0), (sv.proj, A_WIDTH, 1), (dmixed, A_WIDTH, 0), _full(dzb)], gate_consts,
        [(IN_COLS, BF16)], [c.shape for c in gate_consts], ts=GATE_TS, name="gate_bwd")
    g["v_norm_g"], g["v_norm_b"] = gsmall[0], gsmall[1]
    g["w_spatial"] = jnp.stack(gsmall[2:6])
    g["b_spatial"] = _bias_reduce(gsmall[6], name="bias_reduce")[:, :A_GROUPS].T
    g["out_norm_a"] = gsmall[7]

    dh1 = _matmul(dproj, w.big("w_in", dproj), mode="nn", out_dtype=F32, name="proj_dx")
    dw_in = _matmul(dproj, sv.h1, mode="tn", out_dtype=BF16, name="proj_dw").reshape(N_DEV, -1, D_MODEL)
    g_pre = w.g_pre + emit({"w_in": dw_in})

    def pre_fn(xb, dhb, dxb, gp):
        _, vjp = jax.vjp(_rms, xb, gp)
        dxh, dg = vjp(dhb)
        return (dxb + dxh,), (dg,)

    (dx,), (g["pre_mix_norm"],) = _rowwise(pre_fn, [_full(sv.x), _full(dh1), _full(dx1)], [g_pre], [(D_MODEL, F32)],
                                           [(1, D_MODEL)], ts=512, name="pre_mix_norm_bwd")
    return dx, g


def _layer_weights(l, full, small):
    row = lambda a: a[l].reshape(1, -1)
    return types.SimpleNamespace(
        big=functools.partial(full, l),
        g_pre=row(small["pre_mix_norm"]), vg=row(small["v_norm_g"]), vb=row(small["v_norm_b"]),
        ws=[small["w_spatial"][l, gi] for gi in range(A_GROUPS)],
        bfull=jnp.repeat(small["b_spatial"][l].T, CHUNK, axis=1),
        ga=row(small["out_norm_a"]), gb=row(small["out_norm_b"]),
        g_pm=row(small["post_mix_norm"]), g_pf=row(small["pre_ffn_norm"]),
        cb_g=small["conv_b"][l][:D_FF].reshape(1, -1), cb_v=small["conv_b"][l][D_FF:].reshape(1, -1),
        g_post=row(small["post_ffn_norm"]))


def _local_step(x, target, full, small, emit, emit_small, started):
    s = x.shape[0]
    cos, sin = _rope_tables(s)
    ws = [_layer_weights(l, full, small) for l in range(N_LAYERS)]
    ws[0].g_pre = ws[0].g_pre + started
    saved = []
    h = x
    for l in range(N_LAYERS):
        h, sv = _layer_fwd(h, ws[l], cos, sin, last=l == N_LAYERS - 1)
        saved.append(sv)

    def loss_fn(xb, fb, tb, g):
        diff = (xb + _rms(fb, g)) - tb
        return (diff * (1.0 / D_MODEL),), (jnp.sum(diff * diff, axis=0, keepdims=True),)

    (dh,), (sq,) = _rowwise(loss_fn, [_full(saved[-1].x1), _full(saved[-1].f), _full(target)], [ws[-1].g_post],
                            [(D_MODEL, F32)], [(1, D_MODEL)], ts=512, name="loss")
    loss = 0.5 * jnp.sum(sq) * (1.0 / D_MODEL)
    grads = [None] * N_LAYERS
    for l in reversed(range(N_LAYERS)):
        dh, grads[l] = _layer_bwd(dh, saved[l], ws[l], cos, sin, functools.partial(emit, l))
        token = emit_small(l, grads[l], loss)
        if l > 0:
            ws[l - 1].g_post = ws[l - 1].g_post + token
    return loss, dh, grads


def _place():
    return lax.axis_index("x"), lax.axis_index("y"), lax.axis_index("c")


FLIPS = ((1, 0, 0), (0, 1, 0), (1, 1, 0), (0, 0, 1), (1, 0, 1), (0, 1, 1), (1, 1, 1))
HBM_SPEC = pl.BlockSpec(memory_space=pltpu.HBM)
SEM_SPEC = pl.BlockSpec(memory_space=pltpu.SEMAPHORE)
SPLIT_COPY = pltpu.CompilerParams(has_side_effects=pltpu.SideEffectType.DATAFLOW_SIDE_EFFECTING)


def _peers():
    mx, my, mc = _place()
    out = []
    for fx, fy, fc in FLIPS:
        px, py, pc = (1 - mx if fx else mx), (1 - my if fy else my), (1 - mc if fc else mc)
        out.append(((px, py, pc), 4 * px + 2 * py + pc))
    return out


def _flat_copies(scatter, src_refs, land_refs, send_sems, recv_sems):
    mx, my, mc = _place()
    me = 4 * mx + 2 * my + mc
    n = len(src_refs)
    copies = []
    for t in range(n):
        for i, (peer, number) in enumerate(_peers()):
            copies.append(pltpu.make_async_remote_copy(
                src_ref=src_refs[t].at[number] if scatter else src_refs[t],
                dst_ref=land_refs[t].at[i] if scatter else land_refs[t].at[me],
                send_sem=send_sems.at[t * len(FLIPS) + i], recv_sem=recv_sems.at[t * len(FLIPS) + i],
                device_id=peer, device_id_type=MESH_ID))
    return copies


def _flat_start(arrays, scatter, name):
    n = len(arrays)
    slots = len(FLIPS) if scatter else N_DEV
    lands = [lax.empty((slots,) + (a.shape[1:] if scatter else a.shape), a.dtype) for a in arrays]

    def body(*refs):
        src, land, (send_sems, recv_sems), token = refs[:n], refs[n:2 * n], refs[2 * n:2 * n + 2], refs[-1]
        for cp in _flat_copies(scatter, src, land, send_sems, recv_sems):
            cp.start()
        token[...] = jnp.zeros_like(token)

    hbm = [pltpu.HBM(a.shape, a.dtype) for a in arrays] + [pltpu.HBM(a.shape, a.dtype) for a in lands]
    sems = pltpu.SemaphoreType.DMA((n * len(FLIPS),))
    outs = pl.pallas_call(
        body, name=name, out_shape=(sems, sems, *hbm, jax.ShapeDtypeStruct((8, 128), F32)),
        in_specs=[HBM_SPEC] * (2 * n),
        out_specs=(SEM_SPEC, SEM_SPEC, *([HBM_SPEC] * (2 * n)), pl.BlockSpec(memory_space=pltpu.VMEM)),
        input_output_aliases={t: 2 + t for t in range(2 * n)}, compiler_params=SPLIT_COPY,
    )(*[pltpu.with_memory_space_constraint(a, pltpu.HBM) for a in (*arrays, *lands)])
    return types.SimpleNamespace(sems=outs[:2], thru=outs[2:2 + 2 * n], scatter=scatter, n=n), outs[-1][0:1, 0:1]


def _flat_wait(handle, after, name):
    n = handle.n

    def body(*refs):
        src, land, (send_sems, recv_sems) = refs[:n], refs[n:2 * n], refs[2 * n:2 * n + 2]
        for cp in _flat_copies(handle.scatter, src, land, send_sems, recv_sems):
            cp.wait_send()
            cp.wait_recv()

    outs = pl.pallas_call(
        body, name=name, out_shape=tuple(pltpu.HBM(a.shape, a.dtype) for a in handle.thru),
        in_specs=[HBM_SPEC] * (2 * n) + [SEM_SPEC, SEM_SPEC, ANY], out_specs=tuple([HBM_SPEC] * (2 * n)),
        input_output_aliases={t: t for t in range(2 * n)}, compiler_params=SPLIT_COPY,
    )(*handle.thru, *handle.sems, after)
    return outs[:n], outs[n:]


def _adamw(w, g, m, v):
    m2 = ADAM_B1 * m + (1.0 - ADAM_B1) * g
    v2 = ADAM_B2 * v + (1.0 - ADAM_B2) * (g * g)
    m_hat = m2 / (1.0 - ADAM_B1 ** ADAM_STEP)
    v_hat = v2 / (1.0 - ADAM_B2 ** ADAM_STEP)
    return -ADAM_LR * (m_hat / (jnp.sqrt(v_hat) + ADAM_EPS) + ADAM_WD * w), m2, v2


def _adamw_sharded(me, mine, landed, w, m, v, tr, name):
    _, r, c = w.shape
    nt = r // tr
    assert r % tr == 0 and len(mine) == len(landed) == N_LAYERS == 2, name
    per_layer = 1 + len(FLIPS)

    def body(me_ref, *refs):
        terms, (w_ref, m_ref, v_ref), outs = refs[:2 * per_layer], refs[2 * per_layer:2 * per_layer + 3], refs[-4:]
        layer = pl.program_id(0)

        def total(group):
            g = group[0][0].astype(F32)
            for t in group[1:]:
                g = g + t[0].astype(F32)
            return g

        g = jnp.where(layer == 0, total(terms[:per_layer]), total(terms[per_layer:]))
        d, m2, v2 = _adamw(w_ref[0], g, m_ref[0], v_ref[0])
        for o, val in zip(outs, (g, d, m2, v2)):
            o[0] = val

    def held(l):
        return lambda layer, i: jnp.where(layer == l, i, nt - 1 if l == 0 else 0)

    in_specs = []
    for l in range(N_LAYERS):
        rows = held(l)
        in_specs.append(pl.BlockSpec((1, tr, c), functools.partial(lambda layer, i, me_ref, rows: (me_ref[0], rows(layer, i), 0), rows=rows)))
        for k in range(len(FLIPS)):
            in_specs.append(pl.BlockSpec(
                (1, tr, c), functools.partial(lambda layer, i, me_ref, rows, k: (k, rows(layer, i), 0), rows=rows, k=k)))
    tile = pl.BlockSpec((1, tr, c), lambda layer, i, me_ref: (layer, i, 0))
    operands = []
    for l in range(N_LAYERS):
        operands += [mine[l]] + [landed[l]] * len(FLIPS)
    return pl.pallas_call(
        body, name=name, out_shape=[jax.ShapeDtypeStruct(w.shape, F32)] * 4,
        grid_spec=pltpu.PrefetchScalarGridSpec(
            num_scalar_prefetch=1, grid=(N_LAYERS, nt), in_specs=in_specs + [tile] * 3, out_specs=[tile] * 4),
        compiler_params=_params("arbitrary", "arbitrary"),
    )(me, *operands, w, m, v)


def _adamw_replicated(parts, w, m, v, name):
    n_layers = len(parts)

    def body(*refs):
        p_refs, (w_ref, m_ref, v_ref), outs = refs[:n_layers], refs[n_layers:n_layers + 3], refs[n_layers + 3:]
        for l, p_ref in enumerate(p_refs):
            g = p_ref[0]
            for j in range(1, N_DEV):
                g = g + p_ref[j]
            d, m2, v2 = _adamw(w_ref[l], g, m_ref[l], v_ref[l])
            for o, val in zip(outs, (g, d, m2, v2)):
                o[l] = val

    return pl.pallas_call(body, name=name, out_shape=[jax.ShapeDtypeStruct(w.shape, F32)] * 4,
                          compiler_params=pltpu.CompilerParams(vmem_limit_bytes=VMEM_LIMIT_BYTES))(*parts, w, m, v)


SMALL_USED = 80384


def _pack_small(vals, rider=None):
    pieces = [vals[n].reshape(-1) for n in SMALL_NAMES] + ([] if rider is None else [rider.reshape(1)])
    flat = jnp.concatenate(pieces)
    assert flat.shape[0] == SMALL_USED + (rider is not None)
    return jnp.concatenate([flat, jnp.zeros((SMALL_ROWS * D_MODEL - flat.shape[0],), F32)]).reshape(SMALL_ROWS, D_MODEL)


def _pack_small_layers(vals):
    flat = jnp.concatenate([vals[n].reshape(N_LAYERS, -1) for n in SMALL_NAMES], axis=1)
    return jnp.pad(flat, ((0, 0), (0, SMALL_ROWS * D_MODEL - SMALL_USED))).reshape(N_LAYERS, SMALL_ROWS, D_MODEL)


def _unpack_small_layers(packed, shapes):
    flat, out, at = packed.reshape(N_LAYERS, -1), {}, 0
    for n in SMALL_NAMES:
        size = math.prod(shapes[n][1:])
        out[n] = flat[:, at:at + size].reshape(shapes[n])
        at += size
    return out


GATHER_GROUPS = ((0, ("w_in",)), (0, ("w_out", "w_up", "conv_w")), (0, ("w_down",)),
                 (1, ("w_in",)), (1, ("w_out", "w_up", "conv_w")), (1, ("w_down",)))
ADAMW_TILE_ROWS = {"w_in": 320, "w_out": 128, "w_up": 256, "w_down": 256, "conv_w": 3}


def _assemble(name, land):
    if name == "w_in":
        return land.reshape(IN_COLS, D_MODEL)
    if name == "conv_w":
        return land.transpose(1, 0, 2).reshape(3, 2 * D_FF)
    if name == "w_up":
        return land
    return land.reshape(-1, D_MODEL)


def _start_gathers(wts, me):
    started, groups = jnp.zeros((1, 1), F32), []
    for gi, (l, names) in enumerate(GATHER_GROUPS):
        local = {"conv_w": lambda a: a, "w_in": lambda a: a.T.astype(BF16)}
        blocks = [local.get(n, lambda a: a.astype(BF16))(wts[n][l]) for n in names]
        handle, token = _flat_start(blocks, False, name=f"gather_start_{gi}")
        groups.append(types.SimpleNamespace(layer=l, names=names, blocks=blocks, handle=handle, got=None, index=gi))
        started = started + token

    def fetch(l, name, after):
        grp = next(gr for gr in groups if gr.layer == l and name in gr.names)
        if grp.got is None:
            lands = _flat_wait(grp.handle, after, name=f"gather_wait_{grp.index}")[1]
            grp.got = {}
            for n, blk, land in zip(grp.names, grp.blocks, lands):
                own = (me,) + (0,) * blk.ndim
                grp.got[n] = _assemble(n, lax.dynamic_update_slice(land, blk[None], own))
        return grp.got[name]

    return fetch, started


def kernel(x, pre_mix_norm, w_in, v_norm_g, v_norm_b, w_spatial, b_spatial, out_norm_a, out_norm_b, w_out, post_mix_norm, pre_ffn_norm, w_up, conv_w, conv_b, w_down, post_ffn_norm, loss_target, m_pre_mix_norm, m_w_in, m_v_norm_g, m_v_norm_b, m_w_spatial, m_b_spatial, m_out_norm_a, m_out_norm_b, m_w_out, m_post_mix_norm, m_pre_ffn_norm, m_w_up, m_conv_w, m_conv_b, m_w_down, m_post_ffn_norm, v_pre_mix_norm, v_w_in, v_v_norm_g, v_v_norm_b, v_w_spatial, v_b_spatial, v_out_norm_a, v_out_norm_b, v_w_out, v_post_mix_norm, v_pre_ffn_norm, v_w_up, v_conv_w, v_conv_b, v_w_down, v_post_ffn_norm):
    wts = dict(zip(WEIGHT_NAMES, (pre_mix_norm, w_in, v_norm_g, v_norm_b, w_spatial, b_spatial, out_norm_a, out_norm_b,
                                  w_out, post_mix_norm, pre_ffn_norm, w_up, conv_w, conv_b, w_down, post_ffn_norm)))
    mom1 = dict(zip(WEIGHT_NAMES, (m_pre_mix_norm, m_w_in, m_v_norm_g, m_v_norm_b, m_w_spatial, m_b_spatial, m_out_norm_a,
                                   m_out_norm_b, m_w_out, m_post_mix_norm, m_pre_ffn_norm, m_w_up, m_conv_w, m_conv_b,
                                   m_w_down, m_post_ffn_norm)))
    mom2 = dict(zip(WEIGHT_NAMES, (v_pre_mix_norm, v_w_in, v_v_norm_g, v_v_norm_b, v_w_spatial, v_b_spatial, v_out_norm_a,
                                   v_out_norm_b, v_w_out, v_post_mix_norm, v_pre_ffn_norm, v_w_up, v_conv_w, v_conv_b,
                                   v_w_down, v_post_ffn_norm)))
    mx, my, mc = _place()
    me = 4 * mx + 2 * my + mc

    fetch, started = _start_gathers(wts, me)
    scatters = []

    def emit(l, blocks):
        names = tuple(blocks)
        handle, token = _flat_start([blocks[n] for n in names], True, name=f"scatter_start_{l}_{'_'.join(names)}")
        scatters.append((l, names, handle))
        return token

    smalls = {}

    def emit_small(l, g, loss_local):
        pack = _pack_small({n: g[n] for n in SMALL_NAMES}, rider=loss_local if l == 0 else None)
        handle, token = _flat_start([pack], False, name=f"small_grads_start_{l}")
        smalls[l] = (pack, handle)
        return token

    _, dx, _ = _local_step(x[0], loss_target[0], fetch, wts, emit, emit_small, started)

    me_arr = jnp.reshape(me, (1,)).astype(jnp.int32)
    big_out = [{}, {}, {}, {}]

    def finish(group, after):
        mine, landed = {}, {}
        for l, names, handle in scatters:
            if names == group:
                sent, lands = _flat_wait(handle, after, name=f"scatter_wait_{l}_{'_'.join(names)}")
                for n, a, b in zip(names, sent, lands):
                    mine[l, n], landed[l, n] = a, b
        for n in group:
            flip = (lambda a: a.transpose(0, 2, 1)) if n == "w_in" else (lambda a: a)
            res = _adamw_sharded(me_arr, [mine[l, n] for l in range(N_LAYERS)], [landed[l, n] for l in range(N_LAYERS)],
                                 flip(wts[n]), flip(mom1[n]), flip(mom2[n]), ADAMW_TILE_ROWS[n], name=f"adamw_{n}")
            for kind in range(4):
                big_out[kind][n] = flip(res[kind])
        return res[0]

    done = smalls[0][1].thru[0]
    for group in [names for l, names, _ in scatters if l == 0][:-1]:
        done = finish(group, done)
    everyone = [None] * N_LAYERS
    for l in reversed(range(N_LAYERS)):
        pack, handle = smalls[l]
        (landed,) = _flat_wait(handle, done, name=f"small_grads_wait_{l}")[1]
        everyone[l] = lax.dynamic_update_slice(landed, pack[None], (me, 0, 0))
    packs = [_pack_small_layers({n: t[n] for n in SMALL_NAMES}) for t in (wts, mom1, mom2)]
    res = _adamw_replicated(everyone, *packs, name="adamw_replicated")
    loss = res[0][0].reshape(-1)[SMALL_USED]
    finish(scatters[-1][1], res[0])
    small_shapes = {n: wts[n].shape for n in SMALL_NAMES}
    small_out = [_unpack_small_layers(o, small_shapes) for o in res]

    outs = [loss, dx[None]]
    for kind in range(4):
        outs += [big_out[kind][n] if n in BIG_NAMES else small_out[kind][n] for n in WEIGHT_NAMES]
    return tuple(outs)
```

```python
import functools
import math
import types

import jax
import jax.numpy as jnp
from jax import lax
from jax.experimental import pallas as pl
from jax.experimental.pallas import tpu as pltpu

F32 = jnp.float32
BF16 = jnp.bfloat16

D_MODEL = 1024
A_WIDTH = 512
A_GROUPS = 4
CHUNK = 128
B_WIDTH = 512
HEAD_DIM = 64
N_HEADS = B_WIDTH // HEAD_DIM
ROT_DIM = 16
ROPE_THETA = 500000.0
BAND = 128
DILATIONS = (1, 4, 16)
IN_COLS = 2560
D_FF = 4096
EPS = 1e-6
NEG_INF = -1e30
N_DEV = 8
N_LAYERS = 2

ADAM_LR = 0.001
ADAM_B1 = 0.9
ADAM_B2 = 0.999
ADAM_EPS = 1e-08
ADAM_WD = 0.01
ADAM_STEP = 10

VMEM_LIMIT_BYTES = 56 * 1024 * 1024
MESH_ID = pl.DeviceIdType.MESH
ANY = pl.BlockSpec(memory_space=pl.ANY)

WEIGHT_NAMES = ("pre_mix_norm", "w_in", "v_norm_g", "v_norm_b", "w_spatial", "b_spatial", "out_norm_a", "out_norm_b",
                "w_out", "post_mix_norm", "pre_ffn_norm", "w_up", "conv_w", "conv_b", "w_down", "post_ffn_norm")
BIG_NAMES = ("w_in", "w_out", "w_up", "w_down", "conv_w")
SMALL_NAMES = tuple(n for n in WEIGHT_NAMES if n not in BIG_NAMES)

SMALL_ROWS = 80


def _params(*sem):
    return pltpu.CompilerParams(dimension_semantics=sem, vmem_limit_bytes=VMEM_LIMIT_BYTES)


def _dotg(a, b, ca, cb):
    return lax.dot_general(a.astype(BF16), b.astype(BF16), (((ca,), (cb,)), ((), ())), preferred_element_type=F32)


@jax.custom_vjp
def _bdot(a, b):
    return _dotg(a, b, 1, 0)


def _bdot_fwd(a, b):
    return _dotg(a, b, 1, 0), (a, b)


def _bdot_bwd(res, g):
    a, b = res
    return _dotg(g, b, 1, 1), _dotg(a, g, 0, 0)


_bdot.defvjp(_bdot_fwd, _bdot_bwd)


def _rms(x, g):
    return x * lax.rsqrt(jnp.mean(x * x, axis=-1, keepdims=True) + EPS) * g


def _layernorm(x, g, b):
    mu = jnp.mean(x, axis=-1, keepdims=True)
    xc = x - mu
    return xc * lax.rsqrt(jnp.mean(xc * xc, axis=-1, keepdims=True) + EPS) * g + b


def _gelu_erf(x):
    return x * (lax.erf(x * (1.0 / math.sqrt(2.0))) + 1.0) * 0.5


def _gelu_tanh(x):
    c = math.sqrt(2.0 / math.pi)
    return 0.5 * x * (1.0 + jnp.tanh(c * (x + 0.044715 * (x * x * x))))


def _gelu_tanh_and_slope(x):
    c, k = math.sqrt(2.0 / math.pi), 0.044715
    x2 = x * x
    t = jnp.tanh(c * (x + k * (x2 * x)))
    half_x, one_t = 0.5 * x, 1.0 + t
    return half_x * one_t, 0.5 * one_t + (half_x * (1.0 - t * t)) * (c + (3.0 * k * c) * x2)


def _rot_half(x):
    width = x.shape[1]
    lane = lax.broadcasted_iota(jnp.int32, x.shape, 1) % HEAD_DIM
    back = pltpu.roll(x, ROT_DIM // 2, 1)
    fwd = pltpu.roll(x, width - ROT_DIM // 2, 1)
    return jnp.where(lane < ROT_DIM // 2, -fwd, jnp.where(lane < ROT_DIM, back, 0.0))


def _split3(z):
    h0 = z.astype(BF16)
    r1 = z - h0.astype(F32)
    h1 = r1.astype(BF16)
    h2 = (r1 - h1.astype(F32)).astype(BF16)
    return h0, h1, h2


MATMUL_VMEM_BUDGET = 40 * 1024 * 1024


def _matmul_tiles(m, n, k, out_bytes):
    tn = n if n <= 1024 else (1280 if n % 1280 == 0 and n % 1024 else 1024)
    tk = k if k <= 1024 else (1280 if k % 1280 == 0 and k % 1024 else 1024)
    tm = m
    while tm > 256:
        blocks = 2 * 2 * (tm * tk + tk * tn) + 2 * out_bytes * tm * tn + (4 * tm * tn if k > tk else 0)
        if blocks <= MATMUL_VMEM_BUDGET and m % tm == 0:
            break
        tm //= 2
    return tm, tn, tk


def _matmul(a, b, *, mode, out_dtype, name, cols=None):
    wide = D_MODEL if cols is not None else None
    if mode == "nn":
        (m, k), (_, n) = a.shape, (b.shape if cols is None else (b.shape[1], cols[1] * wide))
    elif mode == "nt":
        (m, k), (n, _) = a.shape, (b.shape if cols is None else (b.shape[1], cols[1] * wide))
    else:
        (k, m), (_, n) = a.shape, b.shape
    tm, tn, tk = _matmul_tiles(m, n, k, jnp.dtype(out_dtype).itemsize)
    assert m % tm == 0 and n % tn == 0 and k % tk == 0, (name, m, n, k)
    nk = k // tk
    if mode == "nn":
        a_spec = pl.BlockSpec((tm, tk), lambda i, j, kk: (i, kk))
        b_spec = pl.BlockSpec((tk, tn), lambda i, j, kk: (kk, j))
        if cols is not None:
            assert tn == wide
            b_spec = pl.BlockSpec((None, tk, tn), lambda i, j, kk: (cols[0] + j, kk, 0))
        ca, cb = 1, 0
    elif mode == "nt":
        a_spec = pl.BlockSpec((tm, tk), lambda i, j, kk: (i, kk))
        b_spec = pl.BlockSpec((tn, tk), lambda i, j, kk: (j, kk))
        if cols is not None:
            assert tk == wide
            b_spec = pl.BlockSpec((None, tn, tk), lambda i, j, kk: (cols[0] + kk, j, 0))
        ca, cb = 1, 1
    else:
        a_spec = pl.BlockSpec((tk, tm), lambda i, j, kk: (kk, i))
        b_spec = pl.BlockSpec((tk, tn), lambda i, j, kk: (kk, j))
        ca, cb = 0, 0

    def body(a_ref, b_ref, o_ref, *acc):
        kk = pl.program_id(2)
        part = lax.dot_general(a_ref[...], b_ref[...], (((ca,), (cb,)), ((), ())), preferred_element_type=F32)
        if nk == 1:
            o_ref[...] = part.astype(o_ref.dtype)
            return
        acc_ref, = acc

        @pl.when(kk == 0)
        def _():
            acc_ref[...] = part

        @pl.when(kk > 0)
        def _():
            acc_ref[...] += part

        @pl.when(kk == nk - 1)
        def _():
            o_ref[...] = acc_ref[...].astype(o_ref.dtype)

    return pl.pallas_call(
        body, name=name, grid=(m // tm, n // tn, nk),
        in_specs=[a_spec, b_spec], out_specs=pl.BlockSpec((tm, tn), lambda i, j, kk: (i, j)),
        out_shape=jax.ShapeDtypeStruct((m, n), out_dtype),
        scratch_shapes=[pltpu.VMEM((tm, tn), F32)] if nk > 1 else [],
        compiler_params=_params("parallel", "parallel", "arbitrary"),
    )(a, b)


def _matmul_by_destination(a, b_lo, b_hi, *, name, tm=1024, tk=2048):
    (k, m), half = a.shape, N_DEV // 2
    assert b_lo.shape == b_hi.shape == (k, half * D_MODEL) and m % tm == 0 and k % tk == 0, name
    nk = k // tk

    def body(a_ref, lo_ref, hi_ref, o_ref, acc_ref):
        j, kk = pl.program_id(1), pl.program_id(2)

        def step(b_ref):
            part = lax.dot_general(a_ref[...], b_ref[...], (((0,), (0,)), ((), ())), preferred_element_type=F32)

            @pl.when(kk == 0)
            def _():
                acc_ref[...] = part

            @pl.when(kk > 0)
            def _():
                acc_ref[...] += part

        pl.when(j < half)(lambda: step(lo_ref))
        pl.when(j >= half)(lambda: step(hi_ref))

        @pl.when(kk == nk - 1)
        def _():
            o_ref[...] = acc_ref[...].astype(o_ref.dtype)

    lo_spec = pl.BlockSpec((tk, D_MODEL), lambda i, j, kk: (jnp.where(j < half, kk, nk - 1), jnp.minimum(j, half - 1)))
    hi_spec = pl.BlockSpec((tk, D_MODEL), lambda i, j, kk: (jnp.where(j >= half, kk, 0), jnp.maximum(j - half, 0)))
    return pl.pallas_call(
        body, name=name, grid=(m // tm, N_DEV, nk),
        in_specs=[pl.BlockSpec((tk, tm), lambda i, j, kk: (kk, i)), lo_spec, hi_spec],
        out_specs=pl.BlockSpec((None, tm, D_MODEL), lambda i, j, kk: (j, i, 0)),
        out_shape=jax.ShapeDtypeStruct((N_DEV, m, D_MODEL), BF16),
        scratch_shapes=[pltpu.VMEM((tm, D_MODEL), F32)],
        compiler_params=_params("parallel", "parallel", "arbitrary"),
    )(a, b_lo, b_hi)


LANES = 128


def _residues_to_rows(ref, scr, d):
    w = ref.shape[1] // d
    n = ref.shape[0]
    for r in range(d):
        for c in range(w // LANES):
            scr[c, pl.ds(r, n, stride=d), :] = ref[:, r * w + c * LANES:r * w + (c + 1) * LANES].astype(F32)
    return jnp.concatenate([scr[c] for c in range(w // LANES)], axis=1)


def _rows_to_residues(val, ref, scr, d):
    w = val.shape[1]
    n = ref.shape[0]
    for c in range(w // LANES):
        scr[c] = val[:, c * LANES:(c + 1) * LANES].astype(F32)
    for r in range(d):
        for c in range(w // LANES):
            ref[:, r * w + c * LANES:r * w + (c + 1) * LANES] = scr[c, pl.ds(r, n, stride=d), :].astype(ref.dtype)


HEAD_ROWS = 0


def _head_cols(z, whole_head=False):
    width = z.shape[1]
    a = lax.broadcasted_iota(jnp.int32, (width, LANES), 0)
    b = lax.broadcasted_iota(jnp.int32, (width, LANES), 1)
    pick = jnp.where((a // HEAD_DIM == b) if whole_head else (a == b * HEAD_DIM), 1.0, 0.0).astype(BF16)
    out = None
    for part in _split3(z):
        t = lax.dot_general(part, pick, (((1,), (0,)), ((), ())), preferred_element_type=F32)
        out = t if out is None else out + t
    return out


def _head_rows_block(cols):
    return cols.T[:N_HEADS, :]


def _rowwise(fn, rows, consts, out_rows, out_acc, *, ts, name):
    rows = [tuple(r) + (1,) * (4 - len(r)) for r in rows]
    out_rows = [tuple(o) + (1,) * (3 - len(o)) for o in out_rows]
    s = rows[0][0].shape[0] * rows[0][3]
    assert s % ts == 0, (name, s, ts)
    n_rows, n_in = len(rows), len(rows) + len(consts)
    n_row = len(out_rows)
    n_out = n_row + len(out_acc)
    moved = [(idx, w) for idx, (_, w, _, d) in enumerate(rows) if d > 1]
    moved += [(n_rows + idx, w) for idx, (w, _, d) in enumerate(out_rows) if d > 1]

    def body(*refs):
        scratch = dict(zip([key for key, _ in moved], refs[n_in + n_out:]))
        vals = []
        for idx, r in enumerate(refs[:n_in]):
            d = rows[idx][3] if idx < n_rows else 1
            vals.append(r[...] if d == 1 else _residues_to_rows(r, scratch[idx], d))
        row_vals, acc_vals = fn(*vals)
        for idx, (r, v) in enumerate(zip(refs[n_in:n_in + n_row], row_vals)):
            d = out_rows[idx][2]
            if d == 1:
                r[...] = v.astype(r.dtype)
            elif d == HEAD_ROWS:
                r[...] = _head_rows_block(v)
            else:
                _rows_to_residues(v, r, scratch[n_rows + idx], d)
        first = pl.program_id(0) == 0
        for r, v in zip(refs[n_in + n_row:n_in + n_out], acc_vals):
            @pl.when(first)
            def _(r=r, v=v):
                r[...] = v

            @pl.when(jnp.logical_not(first))
            def _(r=r, v=v):
                r[...] += v

    in_specs = [pl.BlockSpec((ts // d, d * w), functools.partial(lambda i, cb: (i, cb), cb=cb)) for _, w, cb, d in rows]
    in_specs += [pl.BlockSpec(c.shape, lambda i: (0, 0)) for c in consts]
    out_specs = [pl.BlockSpec((N_HEADS, ts), lambda i: (0, i)) if d == HEAD_ROWS else
                 pl.BlockSpec((ts // d, d * w), lambda i: (i, 0)) for w, _, d in out_rows]
    out_specs += [pl.BlockSpec(sh, lambda i: (0, 0)) for sh in out_acc]
    out_shape = [jax.ShapeDtypeStruct((N_HEADS, s) if d == HEAD_ROWS else (s // d, d * w), dt) for w, dt, d in out_rows]
    out_shape += [jax.ShapeDtypeStruct(sh, F32) for sh in out_acc]
    outs = pl.pallas_call(
        body, name=name, grid=(s // ts,), in_specs=in_specs, out_specs=out_specs, out_shape=out_shape,
        scratch_shapes=[pltpu.VMEM((w // LANES, ts, LANES), F32) for _, w in moved],
        compiler_params=_params("arbitrary" if out_acc else "parallel"),
    )(*[a for a, _, _, _ in rows], *consts)
    return outs[:n_row], outs[n_row:]


def _full(a, d=1):
    return (a, a.shape[1] // d, 0, d)


def _gate_fn(zu, zv, vg, vb, ws0, ws1, ws2, ws3, bfull, ga):
    u = _gelu_erf(zu)
    vn = _layernorm(_gelu_erf(zv), vg, vb)
    p = lax.broadcasted_iota(jnp.int32, (CHUNK, CHUNK), 0)
    q = lax.broadcasted_iota(jnp.int32, (CHUNK, CHUNK), 1)
    tril = jnp.where(q <= p, 1.0, 0.0)
    group = lax.broadcasted_iota(jnp.int32, (1, A_WIDTH), 1) // CHUNK
    sg = bfull
    for g, w in enumerate((ws0, ws1, ws2, ws3)):
        sg = sg + _bdot(w * tril, jnp.where(group == g, vn, 0.0))
    return _rms(u * sg, ga)


GATE_TS = 2 * CHUNK


def _chunk_rows(rows):
    return [slice(c * CHUNK, (c + 1) * CHUNK) for c in range(rows // CHUNK)]


def _bias_reduce(dbf, name):
    def body(x_ref, o_ref):
        lane = lax.broadcasted_iota(jnp.int32, (CHUNK, CHUNK), 1)
        out = jnp.zeros((CHUNK, CHUNK), F32)
        for g in range(A_GROUPS):
            out = jnp.where(lane == g, jnp.sum(x_ref[:, g * CHUNK:(g + 1) * CHUNK], axis=1, keepdims=True), out)
        o_ref[...] = out

    return pl.pallas_call(body, name=name, out_shape=jax.ShapeDtypeStruct((CHUNK, CHUNK), F32))(dbf)


def _pair_mask(hh):
    lane = lax.broadcasted_iota(jnp.int32, (1, 2 * HEAD_DIM), 1)
    return (lane >= HEAD_DIM * hh) & (lane < HEAD_DIM * (hh + 1))


def _lane_pick(x2, lm):
    return jnp.max(jnp.where(lm, x2, -jnp.inf), axis=1, keepdims=True)


SCALE = HEAD_DIM ** -0.5


def _blocks_per_step(nb):
    return next(n for n in (4, 2, 1) if nb % n == 0)


def _units(nblk):
    return [(b, hp, hh) for b in range(nblk) for hp in range(N_HEADS // 2) for hh in range(2)]


def _attn_specs(nb, nblk):
    cur = pl.BlockSpec((nblk * BAND, B_WIDTH), lambda r, j: (j, r))
    prev = pl.BlockSpec((BAND, B_WIDTH), lambda r, j: (jnp.maximum(nblk * j - 1, 0), r))
    nxt = pl.BlockSpec((BAND, B_WIDTH), lambda r, j: (jnp.minimum(nblk * (j + 1), nb - 1), r))
    return cur, prev, nxt


def _pair_cols(hp):
    return slice(2 * HEAD_DIM * hp, 2 * HEAD_DIM * (hp + 1))


def _rows(b):
    return slice(b * BAND, (b + 1) * BAND)


def _with_prev(cur_ref, prev_ref, b, sl):
    if b == 0:
        return jnp.concatenate([prev_ref[:, sl], cur_ref[_rows(0), sl]], axis=0)
    return cur_ref[(b - 1) * BAND:(b + 1) * BAND, sl]


def _with_next(cur_ref, next_ref, b, sl, nblk):
    if b == nblk - 1:
        return jnp.concatenate([cur_ref[_rows(b), sl], next_ref[:, sl]], axis=0)
    return cur_ref[b * BAND:(b + 2) * BAND, sl]


def _band_valid(other_exists):
    row = lax.broadcasted_iota(jnp.int32, (BAND, 2 * BAND), 0)
    col = lax.broadcasted_iota(jnp.int32, (BAND, 2 * BAND), 1)
    return (col >= row) & (col <= row + BAND), other_exists


def _masked(lm, x):
    return jnp.where(lm, x, jnp.zeros_like(x))


def _attn_fwd(q, k, v, d, name):
    nb = q.shape[0] // BAND
    nblk = _blocks_per_step(nb)
    units = _units(nblk)
    cur, prev, _ = _attn_specs(nb, nblk)

    def body(q_ref, kc_ref, kp_ref, vc_ref, vp_ref, o_ref, l_ref):
        band, has_prev = _band_valid(pl.program_id(1) > 0)
        col = lax.broadcasted_iota(jnp.int32, (BAND, 2 * BAND), 1)
        valid = [band & ((col >= BAND) | has_prev)] + [band] * (nblk - 1)

        def scores(unit):
            b, hp, hh = unit
            sl = _pair_cols(hp)
            return _dotg(_masked(_pair_mask(hh), q_ref[_rows(b), sl]), _with_prev(kc_ref, kp_ref, b, sl), 1, 1)

        ahead, half = scores(units[0]), None
        for idx, (b, hp, hh) in enumerate(units):
            raw = ahead
            if idx + 1 < len(units):
                ahead = scores(units[idx + 1])
            sl, lm = _pair_cols(hp), _pair_mask(hh)
            s = jnp.where(valid[b], raw * SCALE, NEG_INF)
            m = jnp.max(s, axis=1, keepdims=True)
            p = jnp.exp(s - m)
            den = jnp.sum(p, axis=1, keepdims=True)
            o = _dotg(p, _with_prev(vc_ref, vp_ref, b, sl), 1, 0) / den
            lse = m + jnp.log(den)
            if hh == 0:
                half = (o, lse)
            else:
                o_ref[_rows(b), sl] = jnp.where(lm, o, half[0])
                l_ref[_rows(b), sl] = jnp.where(lm, lse, half[1])

    return pl.pallas_call(
        body, name=name, grid=(d, nb // nblk), in_specs=[cur, cur, prev, cur, prev], out_specs=[cur, cur],
        out_shape=[jax.ShapeDtypeStruct(q.shape, F32), jax.ShapeDtypeStruct(q.shape, F32)],
        compiler_params=_params("parallel", "parallel"),
    )(q, k, k, v, v)


def _attn_bwd_q(q, k, v, do, lse, delta, d, name):
    nb = q.shape[0] // BAND
    nblk = _blocks_per_step(nb)
    units = _units(nblk)
    cur, prev, _ = _attn_specs(nb, nblk)
    per_head = pl.BlockSpec((nblk * BAND, LANES), lambda r, j: (j, r))

    def body(q_ref, kc_ref, kp_ref, vc_ref, vp_ref, do_ref, l_ref, dl_ref, dq_ref):
        band, has_prev = _band_valid(pl.program_id(1) > 0)
        col = lax.broadcasted_iota(jnp.int32, (BAND, 2 * BAND), 1)
        valid = [band & ((col >= BAND) | has_prev)] + [band] * (nblk - 1)

        def products(unit):
            b, hp, hh = unit
            sl, lm = _pair_cols(hp), _pair_mask(hh)
            return (_dotg(_masked(lm, q_ref[_rows(b), sl]), _with_prev(kc_ref, kp_ref, b, sl), 1, 1),
                    _dotg(_masked(lm, do_ref[_rows(b), sl]), _with_prev(vc_ref, vp_ref, b, sl), 1, 1))

        ahead, half = products(units[0]), None
        for idx, (b, hp, hh) in enumerate(units):
            raw, dp = ahead
            if idx + 1 < len(units):
                ahead = products(units[idx + 1])
            sl, lm = _pair_cols(hp), _pair_mask(hh)
            s = jnp.where(valid[b], raw * SCALE, NEG_INF)
            head = lax.broadcasted_iota(jnp.int32, (1, LANES), 1) == 2 * hp + hh
            p = jnp.exp(s - _lane_pick(l_ref[_rows(b), :], head))
            ds = p * (dp - _lane_pick(dl_ref[_rows(b), :], head))
            dq = _dotg(ds, _with_prev(kc_ref, kp_ref, b, sl), 1, 0) * SCALE
            if hh == 0:
                half = dq
            else:
                dq_ref[_rows(b), sl] = jnp.where(lm, dq, half).astype(dq_ref.dtype)

    return pl.pallas_call(
        body, name=name, grid=(d, nb // nblk),
        in_specs=[cur, cur, prev, cur, prev, cur, per_head, per_head], out_specs=cur,
        out_shape=jax.ShapeDtypeStruct(q.shape, BF16),
        compiler_params=_params("parallel", "parallel"),
    )(q, k, k, v, v, do, lse, delta)


def _attn_bwd_kv(q, k, v, do, lse_t, delta_t, d, name):
    nb = q.shape[0] // BAND
    nblk = _blocks_per_step(nb)
    units = _units(nblk)
    cur, _, nxt = _attn_specs(nb, nblk)
    t_cur = pl.BlockSpec((1, N_HEADS, nblk * BAND), lambda r, j: (r, 0, j))
    t_nxt = pl.BlockSpec((1, N_HEADS, BAND), lambda r, j: (r, 0, jnp.minimum(nblk * (j + 1), nb - 1)))

    def body(k_ref, v_ref, qc_ref, qn_ref, doc_ref, don_ref, lc_ref, ln_ref, dlc_ref, dln_ref, dk_ref, dv_ref):
        band, has_next = _band_valid(pl.program_id(1) < nb // nblk - 1)
        col = lax.broadcasted_iota(jnp.int32, (BAND, 2 * BAND), 1)
        valid = [band] * (nblk - 1) + [band & ((col < BAND) | has_next)]

        def head_row(c_ref, n_ref, b, h):
            if b == nblk - 1:
                return jnp.concatenate([c_ref[0, h:h + 1, b * BAND:(b + 1) * BAND], n_ref[0, h:h + 1, :]], axis=1)
            return c_ref[0, h:h + 1, b * BAND:(b + 2) * BAND]

        def products(unit):
            b, hp, hh = unit
            sl, lm = _pair_cols(hp), _pair_mask(hh)
            return (_dotg(_masked(lm, k_ref[_rows(b), sl]), _with_next(qc_ref, qn_ref, b, sl, nblk), 1, 1),
                    _dotg(_masked(lm, v_ref[_rows(b), sl]), _with_next(doc_ref, don_ref, b, sl, nblk), 1, 1))

        ahead, half = products(units[0]), None
        for idx, (b, hp, hh) in enumerate(units):
            raw, dpt = ahead
            if idx + 1 < len(units):
                ahead = products(units[idx + 1])
            sl, lm, h = _pair_cols(hp), _pair_mask(hh), 2 * hp + hh
            st = jnp.where(valid[b], raw * SCALE, NEG_INF)
            pt = jnp.exp(st - head_row(lc_ref, ln_ref, b, h))
            dv = _dotg(pt, _with_next(doc_ref, don_ref, b, sl, nblk), 1, 0)
            dst = pt * (dpt - head_row(dlc_ref, dln_ref, b, h))
            dk = _dotg(dst, _with_next(qc_ref, qn_ref, b, sl, nblk), 1, 0) * SCALE
            if hh == 0:
                half = (dk, dv)
            else:
                dk_ref[_rows(b), sl] = jnp.where(lm, dk, half[0]).astype(dk_ref.dtype)
                dv_ref[_rows(b), sl] = jnp.where(lm, dv, half[1]).astype(dv_ref.dtype)

    return pl.pallas_call(
        body, name=name, grid=(d, nb // nblk),
        in_specs=[cur, cur, cur, nxt, cur, nxt, t_cur, t_nxt, t_cur, t_nxt], out_specs=[cur, cur],
        out_shape=[jax.ShapeDtypeStruct(q.shape, BF16), jax.ShapeDtypeStruct(q.shape, BF16)],
        compiler_params=_params("parallel", "parallel"),
    )(k, v, q, q, do, do, lse_t, lse_t, delta_t, delta_t)


def _spread_rows(a, d):
    return a.reshape(N_HEADS, a.shape[1] // d, d).transpose(2, 0, 1)


FF_TS = 512
FF_TC = 1024
FF_SUB = 256
HALO = 8
HALO_BF16 = 16
UP_BLOCKS = D_MODEL // FF_TC


def _conv3(ext, w, b):
    return b + w[0:1, :] * pltpu.roll(ext, 2, 0) + w[1:2, :] * pltpu.roll(ext, 1, 0) + w[2:3, :] * ext


def _ffn_specs(s, cols_first):
    nrb = s // FF_TS
    per, per16 = FF_TS // HALO, FF_TS // HALO_BF16

    def mk(block, fn):
        if cols_first:
            return pl.BlockSpec(block, lambda j, i: fn(i, j))
        return pl.BlockSpec(block, lambda i, j: fn(i, j))

    specs = types.SimpleNamespace(
        nrb=nrb, ncb=D_FF // FF_TC,
        row=mk((FF_TS, FF_TC), lambda i, j: (i, j)),
        before=mk((HALO, FF_TC), lambda i, j: (jnp.maximum(i * per - 1, 0), j)),
        after=mk((HALO, FF_TC), lambda i, j: (jnp.minimum((i + 1) * per, nrb * per - 1), j)),
        w=mk((3, FF_TC), lambda i, j: (0, j)),
        b=mk((1, FF_TC), lambda i, j: (0, j)),
        part=mk((HALO, FF_TC), lambda i, j: (i, j)),
        act=mk((FF_TS, D_MODEL), lambda i, j: (i, 0)),
        act_before=mk((HALO_BF16, D_MODEL), lambda i, j: (jnp.maximum(i * per16 - 1, 0), 0)),
        act_after=mk((HALO_BF16, D_MODEL), lambda i, j: (jnp.minimum((i + 1) * per16, nrb * per16 - 1), 0)),
        up_gate=mk((None, D_MODEL, FF_TC), lambda i, j: (j // UP_BLOCKS, 0, j % UP_BLOCKS)),
        up_val=mk((None, D_MODEL, FF_TC), lambda i, j: (N_DEV // 2 + j // UP_BLOCKS, 0, j % UP_BLOCKS)),
        down=mk((FF_TC, D_MODEL), lambda i, j: (j, 0)),
    )
    return specs


def _ffn_up_geglu(h, w_up, wg, wv, bg, bv, name):
    s = h.shape[0]
    sp = _ffn_specs(s, True)

    def body(h_ref, hb_ref, ugw_ref, uvw_ref, wg_ref, wv_ref, bg_ref, bv_ref, ug_ref, uv_ref, y_ref):
        keep = jnp.where(pl.program_id(1) > 0, 1.0, 0.0).astype(BF16)
        hext = jnp.concatenate([hb_ref[...] * keep, h_ref[...]], axis=0)
        eg = _dotg(hext, ugw_ref[...], 1, 0)
        ug_ref[...] = eg[HALO_BF16:, :]
        act = _gelu_tanh(_conv3(eg, wg_ref[...], bg_ref[...])[HALO_BF16:, :])
        ev = _dotg(hext, uvw_ref[...], 1, 0)
        uv_ref[...] = ev[HALO_BF16:, :]
        y_ref[...] = (act * _conv3(ev, wv_ref[...], bv_ref[...])[HALO_BF16:, :]).astype(y_ref.dtype)

    return pl.pallas_call(
        body, name=name, grid=(sp.ncb, sp.nrb),
        in_specs=[sp.act, sp.act_before, sp.up_gate, sp.up_val, sp.w, sp.w, sp.b, sp.b],
        out_specs=[sp.row, sp.row, sp.row],
        out_shape=[jax.ShapeDtypeStruct((s, D_FF), F32), jax.ShapeDtypeStruct((s, D_FF), F32),
                   jax.ShapeDtypeStruct((s, D_FF), BF16)],
        compiler_params=_params("parallel", "parallel"),
    )(h, h, w_up, w_up, wg, wv, bg, bv)


def _sum_parts(parts, name):
    n = parts.shape[0] // HALO

    def body(p_ref, o_ref):
        acc = p_ref[0:HALO, :]
        for t in range(1, n):
            acc = acc + p_ref[t * HALO:(t + 1) * HALO, :]
        o_ref[...] = acc

    return pl.pallas_call(body, name=name, out_shape=jax.ShapeDtypeStruct((HALO, parts.shape[1]), F32))(parts)


def _ffn_geglu_bwd(ug, uv, df, w_down, w_up, wg, wv, bg, bv, name):
    s = ug.shape[0]
    sp = _ffn_specs(s, False)
    nrb = sp.nrb
    rows = FF_TS + 2 * HALO
    lo, hi = HALO, HALO + FF_TS

    def body(ug_ref, uv_ref, hg_ref, hv_ref, ng_ref, nv_ref, df_ref, dfn_ref, dw_ref, ugw_ref, uvw_ref,
             wg_ref, wv_ref, bg_ref, bv_ref, dug_ref, duv_ref, dh_ref, dwg_ref, dwv_ref):
        i, j = pl.program_id(0), pl.program_id(1)
        keep_top = jnp.where(i > 0, 1.0, 0.0)
        keep_bot = jnp.where(i < nrb - 1, 1.0, 0.0).astype(BF16)
        dfe = jnp.concatenate([df_ref[...], dfn_ref[...] * keep_bot], axis=0)

        def back(dc, e, w, du_ref, sl):
            up1 = pltpu.roll(dc, rows - 1, 0)
            up2 = pltpu.roll(dc, rows - 2, 0)
            du = (w[2:3, :] * dc + w[1:2, :] * up1 + w[0:1, :] * up2)[lo:hi, :].astype(BF16)
            du_ref[:, sl] = du
            p1, p2 = up1 * e, up2 * e
            colsum = lambda p: jnp.sum(p[lo:hi, :], axis=0, keepdims=True)
            row = lambda p, t: p[t:t + 1, :]
            d_w1 = colsum(p1) + row(p1, lo - 1) - row(p1, hi - 1)
            d_w0 = colsum(p2) + row(p2, lo - 2) + row(p2, lo - 1) - row(p2, hi - 2) - row(p2, hi - 1)
            sums = [d_w0, d_w1, colsum(dc * e), colsum(dc), jnp.zeros((HALO - 4, FF_SUB), F32)]
            return du, jnp.concatenate(sums, axis=0)

        dh = None
        n_sub = FF_TC // FF_SUB
        cols = [slice(c * FF_SUB, (c + 1) * FF_SUB) for c in range(n_sub)]
        d_act = lambda c: _dotg(dfe, dw_ref[cols[c], :], 1, 1)[:FF_TS + HALO, :]
        ahead = d_act(0)
        for c in range(n_sub):
            sl, dy = cols[c], ahead
            if c + 1 < n_sub:
                ahead = d_act(c + 1)
            dye = jnp.concatenate([jnp.zeros((HALO, FF_SUB), F32), dy], axis=0)
            eg = jnp.concatenate([hg_ref[:, sl] * keep_top, ug_ref[:, sl], ng_ref[:, sl]], axis=0)
            ev = jnp.concatenate([hv_ref[:, sl] * keep_top, uv_ref[:, sl], nv_ref[:, sl]], axis=0)
            wg_, wv_ = wg_ref[:, sl], wv_ref[:, sl]
            gate = _conv3(eg, wg_, bg_ref[:, sl])
            val = _conv3(ev, wv_, bv_ref[:, sl])
            act, slope = _gelu_tanh_and_slope(gate)
            dug, dwg_ref[:, sl] = back((dye * val) * slope, eg, wg_, dug_ref, sl)
            duv, dwv_ref[:, sl] = back(dye * act, ev, wv_, duv_ref, sl)
            term = _dotg(dug, ugw_ref[:, sl], 1, 1) + _dotg(duv, uvw_ref[:, sl], 1, 1)
            dh = term if dh is None else dh + term

        @pl.when(j == 0)
        def _():
            dh_ref[...] = dh

        @pl.when(j > 0)
        def _():
            dh_ref[...] += dh

    parts = jax.ShapeDtypeStruct((nrb * HALO, D_FF), F32)
    dug, duv, dh, pg, pv = pl.pallas_call(
        body, name=name, grid=(nrb, sp.ncb),
        in_specs=[sp.row, sp.row, sp.before, sp.before, sp.after, sp.after, sp.act, sp.act_after, sp.down,
                  sp.up_gate, sp.up_val, sp.w, sp.w, sp.b, sp.b],
        out_specs=[sp.row, sp.row, sp.act, sp.part, sp.part],
        out_shape=[jax.ShapeDtypeStruct((s, D_FF), BF16), jax.ShapeDtypeStruct((s, D_FF), BF16),
                   jax.ShapeDtypeStruct((s, D_MODEL), F32), parts, parts],
        compiler_params=_params("parallel", "arbitrary"),
    )(ug, uv, ug, uv, ug, uv, df, df, w_down, w_up, w_up, wg, wv, bg, bv)
    return dug, duv, dh, _sum_parts(pg, name=name + "_sum_gate"), _sum_parts(pv, name=name + "_sum_val")


def _rope_tables(s):
    inv = ROPE_THETA ** (-jnp.arange(0, ROT_DIM, 2, dtype=F32) / ROT_DIM)
    ang = jnp.arange(s, dtype=F32)[:, None] * inv[None, :]
    cos8, sin8 = jnp.cos(ang), jnp.sin(ang)
    rest = HEAD_DIM - ROT_DIM
    cos_h = jnp.concatenate([cos8, cos8, jnp.ones((s, rest), F32)], axis=1)
    sin_h = jnp.concatenate([sin8, sin8, jnp.zeros((s, rest), F32)], axis=1)
    return jnp.tile(cos_h, (1, LANES // HEAD_DIM)), jnp.tile(sin_h, (1, LANES // HEAD_DIM))


def _all_heads(table):
    return jnp.concatenate([table] * (B_WIDTH // LANES), axis=1)


def _layer_fwd(x, w, cos, sin, last):
    sv = types.SimpleNamespace(x=x)
    (sv.h1,), _ = _rowwise(lambda xb, g: ((_rms(xb, g),), ()), [_full(x)], [w.g_pre], [(D_MODEL, BF16)], [],
                           ts=512, name="pre_mix_norm")
    sv.proj = _matmul(sv.h1, w.big("w_in", sv.h1), mode="nt", out_dtype=F32, name="proj")

    gate_consts = [w.vg, w.vb, *w.ws, w.bfull, w.ga]
    def gate_fwd_fn(zu, zv, *consts):
        chunks = [_gate_fn(zu[r], zv[r], *consts) for r in _chunk_rows(zu.shape[0])]
        return (jnp.concatenate(chunks, axis=0),), ()

    (na,), _ = _rowwise(gate_fwd_fn, [(sv.proj, A_WIDTH, 0), (sv.proj, A_WIDTH, 1)], gate_consts,
                        [(A_WIDTH, BF16)], [], ts=GATE_TS, name="gate_fwd")

    def rope_fn(qr, kr, vr, cs, sn):
        cs, sn = _all_heads(cs), _all_heads(sn)
        return (qr * cs + _rot_half(qr) * sn, kr * cs + _rot_half(kr) * sn, vr), ()

    def rope_all(qr, kr, vr, cs, sn):
        return rope_fn(qr, kr, vr, cs, sn)[0] * len(DILATIONS), ()

    qkv, _ = _rowwise(
        rope_all, [(sv.proj, B_WIDTH, 2), (sv.proj, B_WIDTH, 3), (sv.proj, B_WIDTH, 4), _full(cos), _full(sin)], [],
        [(B_WIDTH, BF16, d) for d in DILATIONS for _ in range(3)], [], ts=512, name="rope_fwd")
    sv.qkv = {d: qkv[3 * i:3 * i + 3] for i, d in enumerate(DILATIONS)}

    branch = []
    for d in DILATIONS:
        o, l = _attn_fwd(*sv.qkv[d], d, name=f"attn_fwd_d{d}")
        branch += [_full(o, d), _full(l, d)]

    def combine_fn(o1, l1, o2, l2, o3, l3, nab, gb):
        m = jnp.maximum(jnp.maximum(l1, l2), l3)
        e1, e2, e3 = jnp.exp(l1 - m), jnp.exp(l2 - m), jnp.exp(l3 - m)
        den = e1 + e2 + e3
        ob = (e1 / den) * o1 + (e2 / den) * o2 + (e3 / den) * o3
        mixed = jnp.concatenate([nab, _rms(ob, gb).astype(BF16)], axis=1)
        lse = _head_cols(m + jnp.log(den))
        return (mixed, ob, lse) + (lse,) * len(DILATIONS), ()

    (sv.mixed, sv.ob, sv.lse_rows, *lses), _ = _rowwise(
        combine_fn, branch + [_full(na)], [w.gb],
        [(D_MODEL, BF16), (B_WIDTH, F32), (LANES, F32, HEAD_ROWS)] + [(LANES, F32, d) for d in DILATIONS], [],
        ts=512, name="combine")
    sv.lse = dict(zip(DILATIONS, lses))
    sv.y = _matmul(sv.mixed, w.big("w_out", sv.mixed), mode="nn", out_dtype=F32, name="mix_out")

    def mid_fn(xb, yb, g1, g2):
        x1 = xb + _rms(yb, g1)
        return (x1, _rms(x1, g2)), ()

    (sv.x1, sv.h2), _ = _rowwise(mid_fn, [_full(x), _full(sv.y)], [w.g_pm, w.g_pf], [(D_MODEL, F32), (D_MODEL, BF16)], [],
                                 ts=512, name="post_mix_norm")
    conv_w = w.big("conv_w", sv.h2)
    sv.ug, sv.uv, sv.yff = _ffn_up_geglu(sv.h2, w.big("w_up", sv.h2), conv_w[:, :D_FF], conv_w[:, D_FF:],
                                         w.cb_g, w.cb_v, name="ffn_up_geglu")
    sv.f = _matmul(sv.yff, w.big("w_down", sv.yff), mode="nn", out_dtype=F32, name="ffn_down")
    if last:
        return None, sv
    (x2,), _ = _rowwise(lambda xb, fb, g: ((xb + _rms(fb, g),), ()), [_full(sv.x1), _full(sv.f)], [w.g_post],
                        [(D_MODEL, F32)], [], ts=512, name="post_ffn_norm")
    return x2, sv


def _layer_bwd(dx2, sv, w, cos, sin, emit):
    g = {}

    def post_fn(fb, dxb, gp):
        _, vjp = jax.vjp(_rms, fb, gp)
        df, dg = vjp(dxb)
        return (df,), (dg,)

    (df,), (g["post_ffn_norm"],) = _rowwise(post_fn, [_full(sv.f), _full(dx2)], [w.g_post], [(D_MODEL, BF16)],
                                            [(1, D_MODEL)], ts=512, name="post_ffn_norm_bwd")
    big = {"w_down": _matmul(sv.yff, df, mode="tn", out_dtype=BF16, name="ffn_down_dw").reshape(N_DEV, -1, D_MODEL)}
    conv_w = w.big("conv_w", df)
    dug, duv, dh2, dwg, dwv = _ffn_geglu_bwd(sv.ug, sv.uv, df, w.big("w_down", df), w.big("w_up", df),
                                             conv_w[:, :D_FF], conv_w[:, D_FF:], w.cb_g, w.cb_v, name="ffn_geglu_bwd")
    big["conv_w"] = jnp.concatenate([dwg[0:3], dwv[0:3]], axis=1).reshape(3, N_DEV, D_MODEL).transpose(1, 0, 2)
    g["conv_b"] = jnp.concatenate([dwg[3], dwv[3]], axis=0)
    big["w_up"] = _matmul_by_destination(sv.h2, dug, duv, name="ffn_up_dw")
    g_pm = w.g_pm + emit(big)

    def mid_fn(x1b, yb, dhb, dxb, g1, g2):
        _, vjp2 = jax.vjp(_rms, x1b, g2)
        dx1h, dg2 = vjp2(dhb)
        dx1 = dxb + dx1h
        _, vjp1 = jax.vjp(_rms, yb, g1)
        dy, dg1 = vjp1(dx1)
        return (dx1, dy), (dg1, dg2)

    (dx1, dy), (g["post_mix_norm"], g["pre_ffn_norm"]) = _rowwise(
        mid_fn, [_full(sv.x1), _full(sv.y), _full(dh2), _full(dx2)], [g_pm, w.g_pf],
        [(D_MODEL, F32), (D_MODEL, BF16)], [(1, D_MODEL), (1, D_MODEL)], ts=256, name="post_mix_norm_bwd")
    dmixed = _matmul(dy, w.big("w_out", dy), mode="nt", out_dtype=F32, name="mix_out_dx")
    dw_out = _matmul(sv.mixed, dy, mode="tn", out_dtype=BF16, name="mix_out_dw").reshape(N_DEV, -1, D_MODEL)
    g_b = w.gb + emit({"w_out": dw_out})

    def attn_out_fn(obb, dmb, gb):
        _, vjp = jax.vjp(_rms, obb, gb)
        do, dgb = vjp(dmb)
        delta = _head_cols(do * obb, whole_head=True)
        return (delta,) + (do,) * len(DILATIONS) + (delta,) * len(DILATIONS), (dgb,)

    (delta_rows, *outs), (g["out_norm_b"],) = _rowwise(
        attn_out_fn, [_full(sv.ob), (dmixed, B_WIDTH, 1)], [g_b],
        [(LANES, F32, HEAD_ROWS)] + [(B_WIDTH, BF16, d) for d in DILATIONS] + [(LANES, F32, d) for d in DILATIONS],
        [(1, B_WIDTH)], ts=512, name="attn_out_bwd")
    do = dict(zip(DILATIONS, outs[:len(DILATIONS)]))
    delta = dict(zip(DILATIONS, outs[len(DILATIONS):]))
    parts = {"q": [], "k": [], "v": []}
    for d in DILATIONS:
        qv, kv, vv = sv.qkv[d]
        dq = _attn_bwd_q(qv, kv, vv, do[d], sv.lse[d], delta[d], d, name=f"attn_bwd_q_d{d}")
        dk, dv = _attn_bwd_kv(qv, kv, vv, do[d], _spread_rows(sv.lse_rows, d), _spread_rows(delta_rows, d), d,
                              name=f"attn_bwd_kv_d{d}")
        parts["q"].append(_full(dq, d))
        parts["k"].append(_full(dk, d))
        parts["v"].append(_full(dv, d))

    def rope_bwd_fn(q1, q2, q3, k1, k2, k3, v1, v2, v3, cs, sn):
        cs, sn = _all_heads(cs), _all_heads(sn)

        def back(t):
            return t * cs - _rot_half(t * sn)
        return (jnp.concatenate([back(q1 + q2 + q3), back(k1 + k2 + k3), v1 + v2 + v3], axis=1),), ()

    (dzb,), _ = _rowwise(rope_bwd_fn, parts["q"] + parts["k"] + parts["v"] + [_full(cos), _full(sin)], [],
                         [(3 * B_WIDTH, BF16)], [], ts=256, name="rope_bwd")

    gate_consts = [w.vg, w.vb, *w.ws, w.bfull, w.ga]

    def gate_bwd_fn(zu, zv, dna, dzb_rows, *consts):
        dz, sums = [], None
        for r in _chunk_rows(zu.shape[0]):
            _, vjp = jax.vjp(_gate_fn, zu[r], zv[r], *consts)
            grads = vjp(dna[r])
            dz.append(jnp.concatenate([grads[0].astype(BF16), grads[1].astype(BF16), dzb_rows[r]], axis=1))
            sums = grads[2:] if sums is None else tuple(a + b for a, b in zip(sums, grads[2:]))
        return (jnp.concatenate(dz, axis=0),), tuple(sums)

    (dproj,), gsmall = _rowwise(
        gate_bwd_fn, [(sv.proj, A_WIDTH, 0), (sv.proj, A_WIDTH, 1), (dmixed, A_WIDTH, 0), _full(dzb)], gate_consts,
        [(IN_COLS, BF16)], [c.shape for c in gate_consts], ts=GATE_TS, name="gate_bwd")
    g["v_norm_g"], g["v_norm_b"] = gsmall[0], gsmall[1]
    g["w_spatial"] = jnp.stack(gsmall[2:6])
    g["b_spatial"] = _bias_reduce(gsmall[6], name="bias_reduce")[:, :A_GROUPS].T
    g["out_norm_a"] = gsmall[7]

    dh1 = _matmul(dproj, w.big("w_in", dproj), mode="nn", out_dtype=F32, name="proj_dx")
    dw_in = _matmul(dproj, sv.h1, mode="tn", out_dtype=BF16, name="proj_dw").reshape(N_DEV, -1, D_MODEL)
    g_pre = w.g_pre + emit({"w_in": dw_in})

    def pre_fn(xb, dhb, dxb, gp):
        _, vjp = jax.vjp(_rms, xb, gp)
        dxh, dg = vjp(dhb)
        return (dxb + dxh,), (dg,)

    (dx,), (g["pre_mix_norm"],) = _rowwise(pre_fn, [_full(sv.x), _full(dh1), _full(dx1)], [g_pre], [(D_MODEL, F32)],
                                           [(1, D_MODEL)], ts=512, name="pre_mix_norm_bwd")
    return dx, g


def _layer_weights(l, full, small):
    row = lambda a: a[l].reshape(1, -1)
    return types.SimpleNamespace(
        big=functools.partial(full, l),
        g_pre=row(small["pre_mix_norm"]), vg=row(small["v_norm_g"]), vb=row(small["v_norm_b"]),
        ws=[small["w_spatial"][l, gi] for gi in range(A_GROUPS)],
        bfull=jnp.repeat(small["b_spatial"][l].T, CHUNK, axis=1),
        ga=row(small["out_norm_a"]), gb=row(small["out_norm_b"]),
        g_pm=row(small["post_mix_norm"]), g_pf=row(small["pre_ffn_norm"]),
        cb_g=small["conv_b"][l][:D_FF].reshape(1, -1), cb_v=small["conv_b"][l][D_FF:].reshape(1, -1),
        g_post=row(small["post_ffn_norm"]))


def _local_step(x, target, full, small, emit, emit_small, started):
    s = x.shape[0]
    cos, sin = _rope_tables(s)
    ws = [_layer_weights(l, full, small) for l in range(N_LAYERS)]
    ws[0].g_pre = ws[0].g_pre + started
    saved = []
    h = x
    for l in range(N_LAYERS):
        h, sv = _layer_fwd(h, ws[l], cos, sin, last=l == N_LAYERS - 1)
        saved.append(sv)

    def loss_fn(xb, fb, tb, g):
        diff = (xb + _rms(fb, g)) - tb
        return (diff * (1.0 / D_MODEL),), (jnp.sum(diff * diff, axis=0, keepdims=True),)

    (dh,), (sq,) = _rowwise(loss_fn, [_full(saved[-1].x1), _full(saved[-1].f), _full(target)], [ws[-1].g_post],
                            [(D_MODEL, F32)], [(1, D_MODEL)], ts=512, name="loss")
    loss = 0.5 * jnp.sum(sq) * (1.0 / D_MODEL)
    grads = [None] * N_LAYERS
    for l in reversed(range(N_LAYERS)):
        dh, grads[l] = _layer_bwd(dh, saved[l], ws[l], cos, sin, functools.partial(emit, l))
        token = emit_small(l, grads[l], loss)
        if l > 0:
            ws[l - 1].g_post = ws[l - 1].g_post + token
    return loss, dh, grads


def _place():
    return lax.axis_index("x"), lax.axis_index("y"), lax.axis_index("c")


FLIPS = ((1, 0, 0), (0, 1, 0), (1, 1, 0), (0, 0, 1), (1, 0, 1), (0, 1, 1), (1, 1, 1))
HBM_SPEC = pl.BlockSpec(memory_space=pltpu.HBM)
SEM_SPEC = pl.BlockSpec(memory_space=pltpu.SEMAPHORE)
SPLIT_COPY = pltpu.CompilerParams(has_side_effects=pltpu.SideEffectType.DATAFLOW_SIDE_EFFECTING)


def _peers():
    mx, my, mc = _place()
    out = []
    for fx, fy, fc in FLIPS:
        px, py, pc = (1 - mx if fx else mx), (1 - my if fy else my), (1 - mc if fc else mc)
        out.append(((px, py, pc), 4 * px + 2 * py + pc))
    return out


def _flat_copies(scatter, src_refs, land_refs, send_sems, recv_sems):
    mx, my, mc = _place()
    me = 4 * mx + 2 * my + mc
    n = len(src_refs)
    copies = []
    for t in range(n):
        for i, (peer, number) in enumerate(_peers()):
            copies.append(pltpu.make_async_remote_copy(
                src_ref=src_refs[t].at[number] if scatter else src_refs[t],
                dst_ref=land_refs[t].at[i] if scatter else land_refs[t].at[me],
                send_sem=send_sems.at[t * len(FLIPS) + i], recv_sem=recv_sems.at[t * len(FLIPS) + i],
                device_id=peer, device_id_type=MESH_ID))
    return copies


def _flat_start(arrays, scatter, name):
    n = len(arrays)
    slots = len(FLIPS) if scatter else N_DEV
    lands = [lax.empty((slots,) + (a.shape[1:] if scatter else a.shape), a.dtype) for a in arrays]

    def body(*refs):
        src, land, (send_sems, recv_sems), token = refs[:n], refs[n:2 * n], refs[2 * n:2 * n + 2], refs[-1]
        for cp in _flat_copies(scatter, src, land, send_sems, recv_sems):
            cp.start()
        token[...] = jnp.zeros_like(token)

    hbm = [pltpu.HBM(a.shape, a.dtype) for a in arrays] + [pltpu.HBM(a.shape, a.dtype) for a in lands]
    sems = pltpu.SemaphoreType.DMA((n * len(FLIPS),))
    outs = pl.pallas_call(
        body, name=name, out_shape=(sems, sems, *hbm, jax.ShapeDtypeStruct((8, 128), F32)),
        in_specs=[HBM_SPEC] * (2 * n),
        out_specs=(SEM_SPEC, SEM_SPEC, *([HBM_SPEC] * (2 * n)), pl.BlockSpec(memory_space=pltpu.VMEM)),
        input_output_aliases={t: 2 + t for t in range(2 * n)}, compiler_params=SPLIT_COPY,
    )(*[pltpu.with_memory_space_constraint(a, pltpu.HBM) for a in (*arrays, *lands)])
    return types.SimpleNamespace(sems=outs[:2], thru=outs[2:2 + 2 * n], scatter=scatter, n=n), outs[-1][0:1, 0:1]


def _flat_wait(handle, after, name):
    n = handle.n

    def body(*refs):
        src, land, (send_sems, recv_sems) = refs[:n], refs[n:2 * n], refs[2 * n:2 * n + 2]
        for cp in _flat_copies(handle.scatter, src, land, send_sems, recv_sems):
            cp.wait_send()
            cp.wait_recv()

    outs = pl.pallas_call(
        body, name=name, out_shape=tuple(pltpu.HBM(a.shape, a.dtype) for a in handle.thru),
        in_specs=[HBM_SPEC] * (2 * n) + [SEM_SPEC, SEM_SPEC, ANY], out_specs=tuple([HBM_SPEC] * (2 * n)),
        input_output_aliases={t: t for t in range(2 * n)}, compiler_params=SPLIT_COPY,
    )(*handle.thru, *handle.sems, after)
    return outs[:n], outs[n:]


def _adamw(w, g, m, v):
    m2 = ADAM_B1 * m + (1.0 - ADAM_B1) * g
    v2 = ADAM_B2 * v + (1.0 - ADAM_B2) * (g * g)
    m_hat = m2 / (1.0 - ADAM_B1 ** ADAM_STEP)
    v_hat = v2 / (1.0 - ADAM_B2 ** ADAM_STEP)
    return -ADAM_LR * (m_hat / (jnp.sqrt(v_hat) + ADAM_EPS) + ADAM_WD * w), m2, v2


def _adamw_sharded(me, mine, landed, w, m, v, tr, name):
    _, r, c = w.shape
    nt = r // tr
    assert r % tr == 0 and len(mine) == len(landed) == N_LAYERS == 2, name
    per_layer = 1 + len(FLIPS)

    def body(me_ref, *refs):
        terms, (w_ref, m_ref, v_ref), outs = refs[:2 * per_layer], refs[2 * per_layer:2 * per_layer + 3], refs[-4:]
        layer = pl.program_id(0)

        def total(group):
            g = group[0][0].astype(F32)
            for t in group[1:]:
                g = g + t[0].astype(F32)
            return g

        g = jnp.where(layer == 0, total(terms[:per_layer]), total(terms[per_layer:]))
        d, m2, v2 = _adamw(w_ref[0], g, m_ref[0], v_ref[0])
        for o, val in zip(outs, (g, d, m2, v2)):
            o[0] = val

    def held(l):
        return lambda layer, i: jnp.where(layer == l, i, nt - 1 if l == 0 else 0)

    in_specs = []
    for l in range(N_LAYERS):
        rows = held(l)
        in_specs.append(pl.BlockSpec((1, tr, c), functools.partial(lambda layer, i, me_ref, rows: (me_ref[0], rows(layer, i), 0), rows=rows)))
        for k in range(len(FLIPS)):
            in_specs.append(pl.BlockSpec(
                (1, tr, c), functools.partial(lambda layer, i, me_ref, rows, k: (k, rows(layer, i), 0), rows=rows, k=k)))
    tile = pl.BlockSpec((1, tr, c), lambda layer, i, me_ref: (layer, i, 0))
    operands = []
    for l in range(N_LAYERS):
        operands += [mine[l]] + [landed[l]] * len(FLIPS)
    return pl.pallas_call(
        body, name=name, out_shape=[jax.ShapeDtypeStruct(w.shape, F32)] * 4,
        grid_spec=pltpu.PrefetchScalarGridSpec(
            num_scalar_prefetch=1, grid=(N_LAYERS, nt), in_specs=in_specs + [tile] * 3, out_specs=[tile] * 4),
        compiler_params=_params("arbitrary", "arbitrary"),
    )(me, *operands, w, m, v)


def _adamw_replicated(parts, w, m, v, name):
    n_layers = len(parts)

    def body(*refs):
        p_refs, (w_ref, m_ref, v_ref), outs = refs[:n_layers], refs[n_layers:n_layers + 3], refs[n_layers + 3:]
        for l, p_ref in enumerate(p_refs):
            g = p_ref[0]
            for j in range(1, N_DEV):
                g = g + p_ref[j]
            d, m2, v2 = _adamw(w_ref[l], g, m_ref[l], v_ref[l])
            for o, val in zip(outs, (g, d, m2, v2)):
                o[l] = val

    return pl.pallas_call(body, name=name, out_shape=[jax.ShapeDtypeStruct(w.shape, F32)] * 4,
                          compiler_params=pltpu.CompilerParams(vmem_limit_bytes=VMEM_LIMIT_BYTES))(*parts, w, m, v)


SMALL_USED = 80384


def _pack_small(vals, rider=None):
    pieces = [vals[n].reshape(-1) for n in SMALL_NAMES] + ([] if rider is None else [rider.reshape(1)])
    flat = jnp.concatenate(pieces)
    assert flat.shape[0] == SMALL_USED + (rider is not None)
    return jnp.concatenate([flat, jnp.zeros((SMALL_ROWS * D_MODEL - flat.shape[0],), F32)]).reshape(SMALL_ROWS, D_MODEL)


def _pack_small_layers(vals):
    flat = jnp.concatenate([vals[n].reshape(N_LAYERS, -1) for n in SMALL_NAMES], axis=1)
    return jnp.pad(flat, ((0, 0), (0, SMALL_ROWS * D_MODEL - SMALL_USED))).reshape(N_LAYERS, SMALL_ROWS, D_MODEL)


def _unpack_small_layers(packed, shapes):
    flat, out, at = packed.reshape(N_LAYERS, -1), {}, 0
    for n in SMALL_NAMES:
        size = math.prod(shapes[n][1:])
        out[n] = flat[:, at:at + size].reshape(shapes[n])
        at += size
    return out


GATHER_GROUPS = ((0, ("w_in",)), (0, ("w_out", "w_up", "conv_w")), (0, ("w_down",)),
                 (1, ("w_in",)), (1, ("w_out", "w_up", "conv_w")), (1, ("w_down",)))
ADAMW_TILE_ROWS = {"w_in": 320, "w_out": 128, "w_up": 256, "w_down": 256, "conv_w": 3}


def _assemble(name, land):
    if name == "w_in":
        return land.reshape(IN_COLS, D_MODEL)
    if name == "conv_w":
        return land.transpose(1, 0, 2).reshape(3, 2 * D_FF)
    if name == "w_up":
        return land
    return land.reshape(-1, D_MODEL)


def _start_gathers(wts, me):
    started, groups = jnp.zeros((1, 1), F32), []
    for gi, (l, names) in enumerate(GATHER_GROUPS):
        local = {"conv_w": lambda a: a, "w_in": lambda a: a.T.astype(BF16)}
        blocks = [local.get(n, lambda a: a.astype(BF16))(wts[n][l]) for n in names]
        if gi > 0:
            blocks[0] = blocks[0] + token.astype(blocks[0].dtype)
        handle, token = _flat_start(blocks, False, name=f"gather_start_{gi}")
        groups.append(types.SimpleNamespace(layer=l, names=names, blocks=blocks, handle=handle, got=None, index=gi))
        started = started + token

    def fetch(l, name, after):
        grp = next(gr for gr in groups if gr.layer == l and name in gr.names)
        if grp.got is None:
            lands = _flat_wait(grp.handle, after, name=f"gather_wait_{grp.index}")[1]
            grp.got = {}
            for n, blk, land in zip(grp.names, grp.blocks, lands):
                own = (me,) + (0,) * blk.ndim
                grp.got[n] = _assemble(n, lax.dynamic_update_slice(land, blk[None], own))
        return grp.got[name]

    return fetch, started


def kernel(x, pre_mix_norm, w_in, v_norm_g, v_norm_b, w_spatial, b_spatial, out_norm_a, out_norm_b, w_out, post_mix_norm, pre_ffn_norm, w_up, conv_w, conv_b, w_down, post_ffn_norm, loss_target, m_pre_mix_norm, m_w_in, m_v_norm_g, m_v_norm_b, m_w_spatial, m_b_spatial, m_out_norm_a, m_out_norm_b, m_w_out, m_post_mix_norm, m_pre_ffn_norm, m_w_up, m_conv_w, m_conv_b, m_w_down, m_post_ffn_norm, v_pre_mix_norm, v_w_in, v_v_norm_g, v_v_norm_b, v_w_spatial, v_b_spatial, v_out_norm_a, v_out_norm_b, v_w_out, v_post_mix_norm, v_pre_ffn_norm, v_w_up, v_conv_w, v_conv_b, v_w_down, v_post_ffn_norm):
    wts = dict(zip(WEIGHT_NAMES, (pre_mix_norm, w_in, v_norm_g, v_norm_b, w_spatial, b_spatial, out_norm_a, out_norm_b,
                                  w_out, post_mix_norm, pre_ffn_norm, w_up, conv_w, conv_b, w_down, post_ffn_norm)))
    mom1 = dict(zip(WEIGHT_NAMES, (m_pre_mix_norm, m_w_in, m_v_norm_g, m_v_norm_b, m_w_spatial, m_b_spatial, m_out_norm_a,
                                   m_out_norm_b, m_w_out, m_post_mix_norm, m_pre_ffn_norm, m_w_up, m_conv_w, m_conv_b,
                                   m_w_down, m_post_ffn_norm)))
    mom2 = dict(zip(WEIGHT_NAMES, (v_pre_mix_norm, v_w_in, v_v_norm_g, v_v_norm_b, v_w_spatial, v_b_spatial, v_out_norm_a,
                                   v_out_norm_b, v_w_out, v_post_mix_norm, v_pre_ffn_norm, v_w_up, v_conv_w, v_conv_b,
                                   v_w_down, v_post_ffn_norm)))
    mx, my, mc = _place()
    me = 4 * mx + 2 * my + mc

    fetch, started = _start_gathers(wts, me)
    scatters = []

    def emit(l, blocks):
        names = tuple(blocks)
        handle, token = _flat_start([blocks[n] for n in names], True, name=f"scatter_start_{l}_{'_'.join(names)}")
        scatters.append((l, names, handle))
        return token

    smalls = {}

    def emit_small(l, g, loss_local):
        pack = _pack_small({n: g[n] for n in SMALL_NAMES}, rider=loss_local if l == 0 else None)
        handle, token = _flat_start([pack], False, name=f"small_grads_start_{l}")
        smalls[l] = (pack, handle)
        return token

    _, dx, _ = _local_step(x[0], loss_target[0], fetch, wts, emit, emit_small, started)

    me_arr = jnp.reshape(me, (1,)).astype(jnp.int32)
    big_out = [{}, {}, {}, {}]

    def finish(group, after):
        mine, landed = {}, {}
        for l, names, handle in scatters:
            if names == group:
                sent, lands = _flat_wait(handle, after, name=f"scatter_wait_{l}_{'_'.join(names)}")
                for n, a, b in zip(names, sent, lands):
                    mine[l, n], landed[l, n] = a, b
        for n in group:
            flip = (lambda a: a.transpose(0, 2, 1)) if n == "w_in" else (lambda a: a)
            res = _adamw_sharded(me_arr, [mine[l, n] for l in range(N_LAYERS)], [landed[l, n] for l in range(N_LAYERS)],
                                 flip(wts[n]), flip(mom1[n]), flip(mom2[n]), ADAMW_TILE_ROWS[n], name=f"adamw_{n}")
            for kind in range(4):
                big_out[kind][n] = flip(res[kind])
        return res[0]

    done = smalls[0][1].thru[0]
    for group in [names for l, names, _ in scatters if l == 0][:-1]:
        done = finish(group, done)
    everyone = [None] * N_LAYERS
    for l in reversed(range(N_LAYERS)):
        pack, handle = smalls[l]
        (landed,) = _flat_wait(handle, done, name=f"small_grads_wait_{l}")[1]
        everyone[l] = lax.dynamic_update_slice(landed, pack[None], (me, 0, 0))
    packs = [_pack_small_layers({n: t[n] for n in SMALL_NAMES}) for t in (wts, mom1, mom2)]
    res = _adamw_replicated(everyone, *packs, name="adamw_replicated")
    loss = res[0][0].reshape(-1)[SMALL_USED]
    finish(scatters[-1][1], res[0])
    small_shapes = {n: wts[n].shape for n in SMALL_NAMES}
    small_out = [_unpack_small_layers(o, small_shapes) for o in res]

    outs = [loss, dx[None]]
    for kind in range(4):
        outs += [big_out[kind][n] if n in BIG_NAMES else small_out[kind][n] for n in WEIGHT_NAMES]
    return tuple(outs)
```

```python
import functools
import math
import types

import jax
import jax.numpy as jnp
from jax import lax
from jax.experimental import pallas as pl
from jax.experimental.pallas import tpu as pltpu

F32 = jnp.float32
BF16 = jnp.bfloat16

D_MODEL = 1024
A_WIDTH = 512
A_GROUPS = 4
CHUNK = 128
B_WIDTH = 512
HEAD_DIM = 64
N_HEADS = B_WIDTH // HEAD_DIM
ROT_DIM = 16
ROPE_THETA = 500000.0
BAND = 128
DILATIONS = (1, 4, 16)
IN_COLS = 2560
D_FF = 4096
EPS = 1e-6
NEG_INF = -1e30
N_DEV = 8
N_LAYERS = 2

ADAM_LR = 0.001
ADAM_B1 = 0.9
ADAM_B2 = 0.999
ADAM_EPS = 1e-08
ADAM_WD = 0.01
ADAM_STEP = 10

VMEM_LIMIT_BYTES = 56 * 1024 * 1024
MESH_ID = pl.DeviceIdType.MESH
ANY = pl.BlockSpec(memory_space=pl.ANY)

WEIGHT_NAMES = ("pre_mix_norm", "w_in", "v_norm_g", "v_norm_b", "w_spatial", "b_spatial", "out_norm_a", "out_norm_b",
                "w_out", "post_mix_norm", "pre_ffn_norm", "w_up", "conv_w", "conv_b", "w_down", "post_ffn_norm")
BIG_NAMES = ("w_in", "w_out", "w_up", "w_down", "conv_w")
SMALL_NAMES = tuple(n for n in WEIGHT_NAMES if n not in BIG_NAMES)

SMALL_ROWS = 80


def _params(*sem):
    return pltpu.CompilerParams(dimension_semantics=sem, vmem_limit_bytes=VMEM_LIMIT_BYTES)


def _dotg(a, b, ca, cb):
    return lax.dot_general(a.astype(BF16), b.astype(BF16), (((ca,), (cb,)), ((), ())), preferred_element_type=F32)


@jax.custom_vjp
def _bdot(a, b):
    return _dotg(a, b, 1, 0)


def _bdot_fwd(a, b):
    return _dotg(a, b, 1, 0), (a, b)


def _bdot_bwd(res, g):
    a, b = res
    return _dotg(g, b, 1, 1), _dotg(a, g, 0, 0)


_bdot.defvjp(_bdot_fwd, _bdot_bwd)


def _rms(x, g):
    return x * lax.rsqrt(jnp.mean(x * x, axis=-1, keepdims=True) + EPS) * g


def _layernorm(x, g, b):
    mu = jnp.mean(x, axis=-1, keepdims=True)
    xc = x - mu
    return xc * lax.rsqrt(jnp.mean(xc * xc, axis=-1, keepdims=True) + EPS) * g + b


def _gelu_erf(x):
    return x * (lax.erf(x * (1.0 / math.sqrt(2.0))) + 1.0) * 0.5


def _gelu_tanh(x):
    c = math.sqrt(2.0 / math.pi)
    return 0.5 * x * (1.0 + jnp.tanh(c * (x + 0.044715 * (x * x * x))))


def _gelu_tanh_and_slope(x):
    c, k = math.sqrt(2.0 / math.pi), 0.044715
    x2 = x * x
    t = jnp.tanh(c * (x + k * (x2 * x)))
    half_x, one_t = 0.5 * x, 1.0 + t
    return half_x * one_t, 0.5 * one_t + (half_x * (1.0 - t * t)) * (c + (3.0 * k * c) * x2)


def _rot_half(x):
    width = x.shape[1]
    lane = lax.broadcasted_iota(jnp.int32, x.shape, 1) % HEAD_DIM
    back = pltpu.roll(x, ROT_DIM // 2, 1)
    fwd = pltpu.roll(x, width - ROT_DIM // 2, 1)
    return jnp.where(lane < ROT_DIM // 2, -fwd, jnp.where(lane < ROT_DIM, back, 0.0))


def _split3(z):
    h0 = z.astype(BF16)
    r1 = z - h0.astype(F32)
    h1 = r1.astype(BF16)
    h2 = (r1 - h1.astype(F32)).astype(BF16)
    return h0, h1, h2


MATMUL_VMEM_BUDGET = 40 * 1024 * 1024


def _matmul_tiles(m, n, k, out_bytes):
    tn = n if n <= 1024 else (1280 if n % 1280 == 0 and n % 1024 else 1024)
    tk = k if k <= 1024 else (1280 if k % 1280 == 0 and k % 1024 else 1024)
    tm = m
    while tm > 256:
        blocks = 2 * 2 * (tm * tk + tk * tn) + 2 * out_bytes * tm * tn + (4 * tm * tn if k > tk else 0)
        if blocks <= MATMUL_VMEM_BUDGET and m % tm == 0:
            break
        tm //= 2
    return tm, tn, tk


def _matmul(a, b, *, mode, out_dtype, name, cols=None):
    wide = D_MODEL if cols is not None else None
    if mode == "nn":
        (m, k), (_, n) = a.shape, (b.shape if cols is None else (b.shape[1], cols[1] * wide))
    elif mode == "nt":
        (m, k), (n, _) = a.shape, (b.shape if cols is None else (b.shape[1], cols[1] * wide))
    else:
        (k, m), (_, n) = a.shape, b.shape
    tm, tn, tk = _matmul_tiles(m, n, k, jnp.dtype(out_dtype).itemsize)
    assert m % tm == 0 and n % tn == 0 and k % tk == 0, (name, m, n, k)
    nk = k // tk
    if mode == "nn":
        a_spec = pl.BlockSpec((tm, tk), lambda i, j, kk: (i, kk))
        b_spec = pl.BlockSpec((tk, tn), lambda i, j, kk: (kk, j))
        if cols is not None:
            assert tn == wide
            b_spec = pl.BlockSpec((None, tk, tn), lambda i, j, kk: (cols[0] + j, kk, 0))
        ca, cb = 1, 0
    elif mode == "nt":
        a_spec = pl.BlockSpec((tm, tk), lambda i, j, kk: (i, kk))
        b_spec = pl.BlockSpec((tn, tk), lambda i, j, kk: (j, kk))
        if cols is not None:
            assert tk == wide
            b_spec = pl.BlockSpec((None, tn, tk), lambda i, j, kk: (cols[0] + kk, j, 0))
        ca, cb = 1, 1
    else:
        a_spec = pl.BlockSpec((tk, tm), lambda i, j, kk: (kk, i))
        b_spec = pl.BlockSpec((tk, tn), lambda i, j, kk: (kk, j))
        ca, cb = 0, 0

    def body(a_ref, b_ref, o_ref, *acc):
        kk = pl.program_id(2)
        part = lax.dot_general(a_ref[...], b_ref[...], (((ca,), (cb,)), ((), ())), preferred_element_type=F32)
        if nk == 1:
            o_ref[...] = part.astype(o_ref.dtype)
            return
        acc_ref, = acc

        @pl.when(kk == 0)
        def _():
            acc_ref[...] = part

        @pl.when(kk > 0)
        def _():
            acc_ref[...] += part

        @pl.when(kk == nk - 1)
        def _():
            o_ref[...] = acc_ref[...].astype(o_ref.dtype)

    return pl.pallas_call(
        body, name=name, grid=(m // tm, n // tn, nk),
        in_specs=[a_spec, b_spec], out_specs=pl.BlockSpec((tm, tn), lambda i, j, kk: (i, j)),
        out_shape=jax.ShapeDtypeStruct((m, n), out_dtype),
        scratch_shapes=[pltpu.VMEM((tm, tn), F32)] if nk > 1 else [],
        compiler_params=_params("parallel", "parallel", "arbitrary"),
    )(a, b)


def _matmul_by_destination(a, b_lo, b_hi, *, name, tm=1024, tk=2048):
    (k, m), half = a.shape, N_DEV // 2
    assert b_lo.shape == b_hi.shape == (k, half * D_MODEL) and m % tm == 0 and k % tk == 0, name
    nk = k // tk

    def body(a_ref, lo_ref, hi_ref, o_ref, acc_ref):
        j, kk = pl.program_id(1), pl.program_id(2)

        def step(b_ref):
            a_rows = a_ref[pl.ds(pl.multiple_of(kk * tk, tk), tk), :]
            part = lax.dot_general(a_rows, b_ref[...], (((0,), (0,)), ((), ())), preferred_element_type=F32)

            @pl.when(kk == 0)
            def _():
                acc_ref[...] = part

            @pl.when(kk > 0)
            def _():
                acc_ref[...] += part

        pl.when(j < half)(lambda: step(lo_ref))
        pl.when(j >= half)(lambda: step(hi_ref))

        @pl.when(kk == nk - 1)
        def _():
            o_ref[...] = acc_ref[...].astype(o_ref.dtype)

    lo_spec = pl.BlockSpec((tk, D_MODEL), lambda i, j, kk: (jnp.where(j < half, kk, nk - 1), jnp.minimum(j, half - 1)))
    hi_spec = pl.BlockSpec((tk, D_MODEL), lambda i, j, kk: (jnp.where(j >= half, kk, 0), jnp.maximum(j - half, 0)))
    return pl.pallas_call(
        body, name=name, grid=(m // tm, N_DEV, nk),
        in_specs=[pl.BlockSpec((k, tm), lambda i, j, kk: (0, i)), lo_spec, hi_spec],
        out_specs=pl.BlockSpec((None, tm, D_MODEL), lambda i, j, kk: (j, i, 0)),
        out_shape=jax.ShapeDtypeStruct((N_DEV, m, D_MODEL), BF16),
        scratch_shapes=[pltpu.VMEM((tm, D_MODEL), F32)],
        compiler_params=_params("parallel", "parallel", "arbitrary"),
    )(a, b_lo, b_hi)


LANES = 128


def _residues_to_rows(ref, scr, d):
    w = ref.shape[1] // d
    n = ref.shape[0]
    for r in range(d):
        for c in range(w // LANES):
            scr[c, pl.ds(r, n, stride=d), :] = ref[:, r * w + c * LANES:r * w + (c + 1) * LANES].astype(F32)
    return jnp.concatenate([scr[c] for c in range(w // LANES)], axis=1)


def _rows_to_residues(val, ref, scr, d):
    w = val.shape[1]
    n = ref.shape[0]
    for c in range(w // LANES):
        scr[c] = val[:, c * LANES:(c + 1) * LANES].astype(F32)
    for r in range(d):
        for c in range(w // LANES):
            ref[:, r * w + c * LANES:r * w + (c + 1) * LANES] = scr[c, pl.ds(r, n, stride=d), :].astype(ref.dtype)


HEAD_ROWS = 0


def _head_cols(z, whole_head=False):
    width = z.shape[1]
    a = lax.broadcasted_iota(jnp.int32, (width, LANES), 0)
    b = lax.broadcasted_iota(jnp.int32, (width, LANES), 1)
    pick = jnp.where((a // HEAD_DIM == b) if whole_head else (a == b * HEAD_DIM), 1.0, 0.0).astype(BF16)
    out = None
    for part in _split3(z):
        t = lax.dot_general(part, pick, (((1,), (0,)), ((), ())), preferred_element_type=F32)
        out = t if out is None else out + t
    return out


def _head_rows_block(cols):
    return cols.T[:N_HEADS, :]


def _rowwise(fn, rows, consts, out_rows, out_acc, *, ts, name):
    rows = [tuple(r) + (1,) * (4 - len(r)) for r in rows]
    out_rows = [tuple(o) + (1,) * (3 - len(o)) for o in out_rows]
    s = rows[0][0].shape[0] * rows[0][3]
    assert s % ts == 0, (name, s, ts)
    n_rows, n_in = len(rows), len(rows) + len(consts)
    n_row = len(out_rows)
    n_out = n_row + len(out_acc)
    moved = [(idx, w) for idx, (_, w, _, d) in enumerate(rows) if d > 1]
    moved += [(n_rows + idx, w) for idx, (w, _, d) in enumerate(out_rows) if d > 1]

    def body(*refs):
        scratch = dict(zip([key for key, _ in moved], refs[n_in + n_out:]))
        vals = []
        for idx, r in enumerate(refs[:n_in]):
            d = rows[idx][3] if idx < n_rows else 1
            vals.append(r[...] if d == 1 else _residues_to_rows(r, scratch[idx], d))
        row_vals, acc_vals = fn(*vals)
        for idx, (r, v) in enumerate(zip(refs[n_in:n_in + n_row], row_vals)):
            d = out_rows[idx][2]
            if d == 1:
                r[...] = v.astype(r.dtype)
            elif d == HEAD_ROWS:
                r[...] = _head_rows_block(v)
            else:
                _rows_to_residues(v, r, scratch[n_rows + idx], d)
        first = pl.program_id(0) == 0
        for r, v in zip(refs[n_in + n_row:n_in + n_out], acc_vals):
            @pl.when(first)
            def _(r=r, v=v):
                r[...] = v

            @pl.when(jnp.logical_not(first))
            def _(r=r, v=v):
                r[...] += v

    in_specs = [pl.BlockSpec((ts // d, d * w), functools.partial(lambda i, cb: (i, cb), cb=cb)) for _, w, cb, d in rows]
    in_specs += [pl.BlockSpec(c.shape, lambda i: (0, 0)) for c in consts]
    out_specs = [pl.BlockSpec((N_HEADS, ts), lambda i: (0, i)) if d == HEAD_ROWS else
                 pl.BlockSpec((ts // d, d * w), lambda i: (i, 0)) for w, _, d in out_rows]
    out_specs += [pl.BlockSpec(sh, lambda i: (0, 0)) for sh in out_acc]
    out_shape = [jax.ShapeDtypeStruct((N_HEADS, s) if d == HEAD_ROWS else (s // d, d * w), dt) for w, dt, d in out_rows]
    out_shape += [jax.ShapeDtypeStruct(sh, F32) for sh in out_acc]
    outs = pl.pallas_call(
        body, name=name, grid=(s // ts,), in_specs=in_specs, out_specs=out_specs, out_shape=out_shape,
        scratch_shapes=[pltpu.VMEM((w // LANES, ts, LANES), F32) for _, w in moved],
        compiler_params=_params("arbitrary" if out_acc else "parallel"),
    )(*[a for a, _, _, _ in rows], *consts)
    return outs[:n_row], outs[n_row:]


def _full(a, d=1):
    return (a, a.shape[1] // d, 0, d)


def _gate_fn(zu, zv, vg, vb, ws0, ws1, ws2, ws3, bfull, ga):
    u = _gelu_erf(zu)
    vn = _layernorm(_gelu_erf(zv), vg, vb)
    p = lax.broadcasted_iota(jnp.int32, (CHUNK, CHUNK), 0)
    q = lax.broadcasted_iota(jnp.int32, (CHUNK, CHUNK), 1)
    tril = jnp.where(q <= p, 1.0, 0.0)
    group = lax.broadcasted_iota(jnp.int32, (1, A_WIDTH), 1) // CHUNK
    sg = bfull
    for g, w in enumerate((ws0, ws1, ws2, ws3)):
        sg = sg + _bdot(w * tril, jnp.where(group == g, vn, 0.0))
    return _rms(u * sg, ga)


GATE_TS = 2 * CHUNK


def _chunk_rows(rows):
    return [slice(c * CHUNK, (c + 1) * CHUNK) for c in range(rows // CHUNK)]


def _bias_reduce(dbf, name):
    def body(x_ref, o_ref):
        lane = lax.broadcasted_iota(jnp.int32, (CHUNK, CHUNK), 1)
        out = jnp.zeros((CHUNK, CHUNK), F32)
        for g in range(A_GROUPS):
            out = jnp.where(lane == g, jnp.sum(x_ref[:, g * CHUNK:(g + 1) * CHUNK], axis=1, keepdims=True), out)
        o_ref[...] = out

    return pl.pallas_call(body, name=name, out_shape=jax.ShapeDtypeStruct((CHUNK, CHUNK), F32))(dbf)


def _pair_mask(hh):
    lane = lax.broadcasted_iota(jnp.int32, (1, 2 * HEAD_DIM), 1)
    return (lane >= HEAD_DIM * hh) & (lane < HEAD_DIM * (hh + 1))


def _lane_pick(x2, lm):
    return jnp.max(jnp.where(lm, x2, -jnp.inf), axis=1, keepdims=True)


SCALE = HEAD_DIM ** -0.5


def _blocks_per_step(nb):
    return next(n for n in (4, 2, 1) if nb % n == 0)


def _units(nblk):
    return [(b, hp, hh) for b in range(nblk) for hp in range(N_HEADS // 2) for hh in range(2)]


def _attn_specs(nb, nblk):
    cur = pl.BlockSpec((nblk * BAND, B_WIDTH), lambda r, j: (j, r))
    prev = pl.BlockSpec((BAND, B_WIDTH), lambda r, j: (jnp.maximum(nblk * j - 1, 0), r))
    nxt = pl.BlockSpec((BAND, B_WIDTH), lambda r, j: (jnp.minimum(nblk * (j + 1), nb - 1), r))
    return cur, prev, nxt


def _pair_cols(hp):
    return slice(2 * HEAD_DIM * hp, 2 * HEAD_DIM * (hp + 1))


def _rows(b):
    return slice(b * BAND, (b + 1) * BAND)


def _with_prev(cur_ref, prev_ref, b, sl):
    if b == 0:
        return jnp.concatenate([prev_ref[:, sl], cur_ref[_rows(0), sl]], axis=0)
    return cur_ref[(b - 1) * BAND:(b + 1) * BAND, sl]


def _with_next(cur_ref, next_ref, b, sl, nblk):
    if b == nblk - 1:
        return jnp.concatenate([cur_ref[_rows(b), sl], next_ref[:, sl]], axis=0)
    return cur_ref[b * BAND:(b + 2) * BAND, sl]


def _band_valid(other_exists):
    row = lax.broadcasted_iota(jnp.int32, (BAND, 2 * BAND), 0)
    col = lax.broadcasted_iota(jnp.int32, (BAND, 2 * BAND), 1)
    return (col >= row) & (col <= row + BAND), other_exists


def _masked(lm, x):
    return jnp.where(lm, x, jnp.zeros_like(x))


def _attn_fwd(q, k, v, d, name):
    nb = q.shape[0] // BAND
    nblk = _blocks_per_step(nb)
    units = _units(nblk)
    cur, prev, _ = _attn_specs(nb, nblk)

    def body(q_ref, kc_ref, kp_ref, vc_ref, vp_ref, o_ref, l_ref):
        band, has_prev = _band_valid(pl.program_id(1) > 0)
        col = lax.broadcasted_iota(jnp.int32, (BAND, 2 * BAND), 1)
        valid = [band & ((col >= BAND) | has_prev)] + [band] * (nblk - 1)

        def scores(unit):
            b, hp, hh = unit
            sl = _pair_cols(hp)
            return _dotg(_masked(_pair_mask(hh), q_ref[_rows(b), sl]), _with_prev(kc_ref, kp_ref, b, sl), 1, 1)

        ahead, half = scores(units[0]), None
        for idx, (b, hp, hh) in enumerate(units):
            raw = ahead
            if idx + 1 < len(units):
                ahead = scores(units[idx + 1])
            sl, lm = _pair_cols(hp), _pair_mask(hh)
            s = jnp.where(valid[b], raw * SCALE, NEG_INF)
            m = jnp.max(s, axis=1, keepdims=True)
            p = jnp.exp(s - m)
            den = jnp.sum(p, axis=1, keepdims=True)
            o = _dotg(p, _with_prev(vc_ref, vp_ref, b, sl), 1, 0) / den
            lse = m + jnp.log(den)
            if hh == 0:
                half = (o, lse)
            else:
                o_ref[_rows(b), sl] = jnp.where(lm, o, half[0])
                l_ref[_rows(b), sl] = jnp.where(lm, lse, half[1])

    return pl.pallas_call(
        body, name=name, grid=(d, nb // nblk), in_specs=[cur, cur, prev, cur, prev], out_specs=[cur, cur],
        out_shape=[jax.ShapeDtypeStruct(q.shape, F32), jax.ShapeDtypeStruct(q.shape, F32)],
        compiler_params=_params("parallel", "parallel"),
    )(q, k, k, v, v)


def _attn_bwd_q(q, k, v, do, lse, delta, d, name):
    nb = q.shape[0] // BAND
    nblk = _blocks_per_step(nb)
    units = _units(nblk)
    cur, prev, _ = _attn_specs(nb, nblk)
    per_head = pl.BlockSpec((nblk * BAND, LANES), lambda r, j: (j, r))

    def body(q_ref, kc_ref, kp_ref, vc_ref, vp_ref, do_ref, l_ref, dl_ref, dq_ref):
        band, has_prev = _band_valid(pl.program_id(1) > 0)
        col = lax.broadcasted_iota(jnp.int32, (BAND, 2 * BAND), 1)
        valid = [band & ((col >= BAND) | has_prev)] + [band] * (nblk - 1)

        def products(unit):
            b, hp, hh = unit
            sl, lm = _pair_cols(hp), _pair_mask(hh)
            return (_dotg(_masked(lm, q_ref[_rows(b), sl]), _with_prev(kc_ref, kp_ref, b, sl), 1, 1),
                    _dotg(_masked(lm, do_ref[_rows(b), sl]), _with_prev(vc_ref, vp_ref, b, sl), 1, 1))

        ahead, half = products(units[0]), None
        for idx, (b, hp, hh) in enumerate(units):
            raw, dp = ahead
            if idx + 1 < len(units):
                ahead = products(units[idx + 1])
            sl, lm = _pair_cols(hp), _pair_mask(hh)
            s = jnp.where(valid[b], raw * SCALE, NEG_INF)
            head = lax.broadcasted_iota(jnp.int32, (1, LANES), 1) == 2 * hp + hh
            p = jnp.exp(s - _lane_pick(l_ref[_rows(b), :], head))
            ds = p * (dp - _lane_pick(dl_ref[_rows(b), :], head))
            dq = _dotg(ds, _with_prev(kc_ref, kp_ref, b, sl), 1, 0) * SCALE
            if hh == 0:
                half = dq
            else:
                dq_ref[_rows(b), sl] = jnp.where(lm, dq, half).astype(dq_ref.dtype)

    return pl.pallas_call(
        body, name=name, grid=(d, nb // nblk),
        in_specs=[cur, cur, prev, cur, prev, cur, per_head, per_head], out_specs=cur,
        out_shape=jax.ShapeDtypeStruct(q.shape, BF16),
        compiler_params=_params("parallel", "parallel"),
    )(q, k, k, v, v, do, lse, delta)


def _attn_bwd_kv(q, k, v, do, lse_t, delta_t, d, name):
    nb = q.shape[0] // BAND
    nblk = _blocks_per_step(nb)
    units = _units(nblk)
    cur, _, nxt = _attn_specs(nb, nblk)
    t_cur = pl.BlockSpec((1, N_HEADS, nblk * BAND), lambda r, j: (r, 0, j))
    t_nxt = pl.BlockSpec((1, N_HEADS, BAND), lambda r, j: (r, 0, jnp.minimum(nblk * (j + 1), nb - 1)))

    def body(k_ref, v_ref, qc_ref, qn_ref, doc_ref, don_ref, lc_ref, ln_ref, dlc_ref, dln_ref, dk_ref, dv_ref):
        band, has_next = _band_valid(pl.program_id(1) < nb // nblk - 1)
        col = lax.broadcasted_iota(jnp.int32, (BAND, 2 * BAND), 1)
        valid = [band] * (nblk - 1) + [band & ((col < BAND) | has_next)]

        def head_row(c_ref, n_ref, b, h):
            if b == nblk - 1:
                return jnp.concatenate([c_ref[0, h:h + 1, b * BAND:(b + 1) * BAND], n_ref[0, h:h + 1, :]], axis=1)
            return c_ref[0, h:h + 1, b * BAND:(b + 2) * BAND]

        def products(unit):
            b, hp, hh = unit
            sl, lm = _pair_cols(hp), _pair_mask(hh)
            return (_dotg(_masked(lm, k_ref[_rows(b), sl]), _with_next(qc_ref, qn_ref, b, sl, nblk), 1, 1),
                    _dotg(_masked(lm, v_ref[_rows(b), sl]), _with_next(doc_ref, don_ref, b, sl, nblk), 1, 1))

        ahead, half = products(units[0]), None
        for idx, (b, hp, hh) in enumerate(units):
            raw, dpt = ahead
            if idx + 1 < len(units):
                ahead = products(units[idx + 1])
            sl, lm, h = _pair_cols(hp), _pair_mask(hh), 2 * hp + hh
            st = jnp.where(valid[b], raw * SCALE, NEG_INF)
            pt = jnp.exp(st - head_row(lc_ref, ln_ref, b, h))
            dv = _dotg(pt, _with_next(doc_ref, don_ref, b, sl, nblk), 1, 0)
            dst = pt * (dpt - head_row(dlc_ref, dln_ref, b, h))
            dk = _dotg(dst, _with_next(qc_ref, qn_ref, b, sl, nblk), 1, 0) * SCALE
            if hh == 0:
                half = (dk, dv)
            else:
                dk_ref[_rows(b), sl] = jnp.where(lm, dk, half[0]).astype(dk_ref.dtype)
                dv_ref[_rows(b), sl] = jnp.where(lm, dv, half[1]).astype(dv_ref.dtype)

    return pl.pallas_call(
        body, name=name, grid=(d, nb // nblk),
        in_specs=[cur, cur, cur, nxt, cur, nxt, t_cur, t_nxt, t_cur, t_nxt], out_specs=[cur, cur],
        out_shape=[jax.ShapeDtypeStruct(q.shape, BF16), jax.ShapeDtypeStruct(q.shape, BF16)],
        compiler_params=_params("parallel", "parallel"),
    )(k, v, q, q, do, do, lse_t, lse_t, delta_t, delta_t)


def _spread_rows(a, d):
    return a.reshape(N_HEADS, a.shape[1] // d, d).transpose(2, 0, 1)


FF_TS = 512
FF_TC = 1024
FF_SUB = 256
HALO = 8
HALO_BF16 = 16
UP_BLOCKS = D_MODEL // FF_TC


def _conv3(ext, w, b):
    return b + w[0:1, :] * pltpu.roll(ext, 2, 0) + w[1:2, :] * pltpu.roll(ext, 1, 0) + w[2:3, :] * ext


def _ffn_specs(s, cols_first):
    nrb = s // FF_TS
    per, per16 = FF_TS // HALO, FF_TS // HALO_BF16

    def mk(block, fn):
        if cols_first:
            return pl.BlockSpec(block, lambda j, i: fn(i, j))
        return pl.BlockSpec(block, lambda i, j: fn(i, j))

    specs = types.SimpleNamespace(
        nrb=nrb, ncb=D_FF // FF_TC,
        row=mk((FF_TS, FF_TC), lambda i, j: (i, j)),
        before=mk((HALO, FF_TC), lambda i, j: (jnp.maximum(i * per - 1, 0), j)),
        after=mk((HALO, FF_TC), lambda i, j: (jnp.minimum((i + 1) * per, nrb * per - 1), j)),
        w=mk((3, FF_TC), lambda i, j: (0, j)),
        b=mk((1, FF_TC), lambda i, j: (0, j)),
        part=mk((HALO, FF_TC), lambda i, j: (i, j)),
        act=mk((FF_TS, D_MODEL), lambda i, j: (i, 0)),
        act_before=mk((HALO_BF16, D_MODEL), lambda i, j: (jnp.maximum(i * per16 - 1, 0), 0)),
        act_after=mk((HALO_BF16, D_MODEL), lambda i, j: (jnp.minimum((i + 1) * per16, nrb * per16 - 1), 0)),
        up_gate=mk((None, D_MODEL, FF_TC), lambda i, j: (j // UP_BLOCKS, 0, j % UP_BLOCKS)),
        up_val=mk((None, D_MODEL, FF_TC), lambda i, j: (N_DEV // 2 + j // UP_BLOCKS, 0, j % UP_BLOCKS)),
        down=mk((FF_TC, D_MODEL), lambda i, j: (j, 0)),
    )
    return specs


def _ffn_up_geglu(h, w_up, wg, wv, bg, bv, name):
    s = h.shape[0]
    sp = _ffn_specs(s, True)

    def body(h_ref, hb_ref, ugw_ref, uvw_ref, wg_ref, wv_ref, bg_ref, bv_ref, ug_ref, uv_ref, y_ref):
        keep = jnp.where(pl.program_id(1) > 0, 1.0, 0.0).astype(BF16)
        hext = jnp.concatenate([hb_ref[...] * keep, h_ref[...]], axis=0)
        eg = _dotg(hext, ugw_ref[...], 1, 0)
        ug_ref[...] = eg[HALO_BF16:, :]
        act = _gelu_tanh(_conv3(eg, wg_ref[...], bg_ref[...])[HALO_BF16:, :])
        ev = _dotg(hext, uvw_ref[...], 1, 0)
        uv_ref[...] = ev[HALO_BF16:, :]
        y_ref[...] = (act * _conv3(ev, wv_ref[...], bv_ref[...])[HALO_BF16:, :]).astype(y_ref.dtype)

    return pl.pallas_call(
        body, name=name, grid=(sp.ncb, sp.nrb),
        in_specs=[sp.act, sp.act_before, sp.up_gate, sp.up_val, sp.w, sp.w, sp.b, sp.b],
        out_specs=[sp.row, sp.row, sp.row],
        out_shape=[jax.ShapeDtypeStruct((s, D_FF), F32), jax.ShapeDtypeStruct((s, D_FF), F32),
                   jax.ShapeDtypeStruct((s, D_FF), BF16)],
        compiler_params=_params("parallel", "parallel"),
    )(h, h, w_up, w_up, wg, wv, bg, bv)


def _sum_parts(parts, name):
    n = parts.shape[0] // HALO

    def body(p_ref, o_ref):
        acc = p_ref[0:HALO, :]
        for t in range(1, n):
            acc = acc + p_ref[t * HALO:(t + 1) * HALO, :]
        o_ref[...] = acc

    return pl.pallas_call(body, name=name, out_shape=jax.ShapeDtypeStruct((HALO, parts.shape[1]), F32))(parts)


def _ffn_geglu_bwd(ug, uv, df, w_down, w_up, wg, wv, bg, bv, name):
    s = ug.shape[0]
    sp = _ffn_specs(s, False)
    nrb = sp.nrb
    rows = FF_TS + 2 * HALO
    lo, hi = HALO, HALO + FF_TS

    def body(ug_ref, uv_ref, hg_ref, hv_ref, ng_ref, nv_ref, df_ref, dfn_ref, dw_ref, ugw_ref, uvw_ref,
             wg_ref, wv_ref, bg_ref, bv_ref, dug_ref, duv_ref, dh_ref, dwg_ref, dwv_ref):
        i, j = pl.program_id(0), pl.program_id(1)
        keep_top = jnp.where(i > 0, 1.0, 0.0)
        keep_bot = jnp.where(i < nrb - 1, 1.0, 0.0).astype(BF16)
        dfe = jnp.concatenate([df_ref[...], dfn_ref[...] * keep_bot], axis=0)

        def back(dc, e, w, du_ref, sl):
            up1 = pltpu.roll(dc, rows - 1, 0)
            up2 = pltpu.roll(dc, rows - 2, 0)
            du = (w[2:3, :] * dc + w[1:2, :] * up1 + w[0:1, :] * up2)[lo:hi, :].astype(BF16)
            du_ref[:, sl] = du
            p1, p2 = up1 * e, up2 * e
            colsum = lambda p: jnp.sum(p[lo:hi, :], axis=0, keepdims=True)
            row = lambda p, t: p[t:t + 1, :]
            d_w1 = colsum(p1) + row(p1, lo - 1) - row(p1, hi - 1)
            d_w0 = colsum(p2) + row(p2, lo - 2) + row(p2, lo - 1) - row(p2, hi - 2) - row(p2, hi - 1)
            sums = [d_w0, d_w1, colsum(dc * e), colsum(dc), jnp.zeros((HALO - 4, FF_SUB), F32)]
            return du, jnp.concatenate(sums, axis=0)

        dh = None
        n_sub = FF_TC // FF_SUB
        cols = [slice(c * FF_SUB, (c + 1) * FF_SUB) for c in range(n_sub)]
        d_act = lambda c: _dotg(dfe, dw_ref[cols[c], :], 1, 1)[:FF_TS + HALO, :]
        ahead = d_act(0)
        for c in range(n_sub):
            sl, dy = cols[c], ahead
            if c + 1 < n_sub:
                ahead = d_act(c + 1)
            dye = jnp.concatenate([jnp.zeros((HALO, FF_SUB), F32), dy], axis=0)
            eg = jnp.concatenate([hg_ref[:, sl] * keep_top, ug_ref[:, sl], ng_ref[:, sl]], axis=0)
            ev = jnp.concatenate([hv_ref[:, sl] * keep_top, uv_ref[:, sl], nv_ref[:, sl]], axis=0)
            wg_, wv_ = wg_ref[:, sl], wv_ref[:, sl]
            gate = _conv3(eg, wg_, bg_ref[:, sl])
            val = _conv3(ev, wv_, bv_ref[:, sl])
            act, slope = _gelu_tanh_and_slope(gate)
            dug, dwg_ref[:, sl] = back((dye * val) * slope, eg, wg_, dug_ref, sl)
            duv, dwv_ref[:, sl] = back(dye * act, ev, wv_, duv_ref, sl)
            term = _dotg(dug, ugw_ref[:, sl], 1, 1) + _dotg(duv, uvw_ref[:, sl], 1, 1)
            dh = term if dh is None else dh + term

        @pl.when(j == 0)
        def _():
            dh_ref[...] = dh

        @pl.when(j > 0)
        def _():
            dh_ref[...] += dh

    parts = jax.ShapeDtypeStruct((nrb * HALO, D_FF), F32)
    dug, duv, dh, pg, pv = pl.pallas_call(
        body, name=name, grid=(nrb, sp.ncb),
        in_specs=[sp.row, sp.row, sp.before, sp.before, sp.after, sp.after, sp.act, sp.act_after, sp.down,
                  sp.up_gate, sp.up_val, sp.w, sp.w, sp.b, sp.b],
        out_specs=[sp.row, sp.row, sp.act, sp.part, sp.part],
        out_shape=[jax.ShapeDtypeStruct((s, D_FF), BF16), jax.ShapeDtypeStruct((s, D_FF), BF16),
                   jax.ShapeDtypeStruct((s, D_MODEL), F32), parts, parts],
        compiler_params=_params("parallel", "arbitrary"),
    )(ug, uv, ug, uv, ug, uv, df, df, w_down, w_up, w_up, wg, wv, bg, bv)
    return dug, duv, dh, _sum_parts(pg, name=name + "_sum_gate"), _sum_parts(pv, name=name + "_sum_val")


def _rope_tables(s):
    inv = ROPE_THETA ** (-jnp.arange(0, ROT_DIM, 2, dtype=F32) / ROT_DIM)
    ang = jnp.arange(s, dtype=F32)[:, None] * inv[None, :]
    cos8, sin8 = jnp.cos(ang), jnp.sin(ang)
    rest = HEAD_DIM - ROT_DIM
    cos_h = jnp.concatenate([cos8, cos8, jnp.ones((s, rest), F32)], axis=1)
    sin_h = jnp.concatenate([sin8, sin8, jnp.zeros((s, rest), F32)], axis=1)
    return jnp.tile(cos_h, (1, LANES // HEAD_DIM)), jnp.tile(sin_h, (1, LANES // HEAD_DIM))


def _all_heads(table):
    return jnp.concatenate([table] * (B_WIDTH // LANES), axis=1)


def _layer_fwd(x, w, cos, sin, last):
    sv = types.SimpleNamespace(x=x)
    (sv.h1,), _ = _rowwise(lambda xb, g: ((_rms(xb, g),), ()), [_full(x)], [w.g_pre], [(D_MODEL, BF16)], [],
                           ts=512, name="pre_mix_norm")
    sv.proj = _matmul(sv.h1, w.big("w_in", sv.h1), mode="nt", out_dtype=F32, name="proj")

    gate_consts = [w.vg, w.vb, *w.ws, w.bfull, w.ga]
    def gate_fwd_fn(zu, zv, *consts):
        chunks = [_gate_fn(zu[r], zv[r], *consts) for r in _chunk_rows(zu.shape[0])]
        return (jnp.concatenate(chunks, axis=0),), ()

    (na,), _ = _rowwise(gate_fwd_fn, [(sv.proj, A_WIDTH, 0), (sv.proj, A_WIDTH, 1)], gate_consts,
                        [(A_WIDTH, BF16)], [], ts=GATE_TS, name="gate_fwd")

    def rope_fn(qr, kr, vr, cs, sn):
        cs, sn = _all_heads(cs), _all_heads(sn)
        return (qr * cs + _rot_half(qr) * sn, kr * cs + _rot_half(kr) * sn, vr), ()

    def rope_all(qr, kr, vr, cs, sn):
        return rope_fn(qr, kr, vr, cs, sn)[0] * len(DILATIONS), ()

    qkv, _ = _rowwise(
        rope_all, [(sv.proj, B_WIDTH, 2), (sv.proj, B_WIDTH, 3), (sv.proj, B_WIDTH, 4), _full(cos), _full(sin)], [],
        [(B_WIDTH, BF16, d) for d in DILATIONS for _ in range(3)], [], ts=512, name="rope_fwd")
    sv.qkv = {d: qkv[3 * i:3 * i + 3] for i, d in enumerate(DILATIONS)}

    branch = []
    for d in DILATIONS:
        o, l = _attn_fwd(*sv.qkv[d], d, name=f"attn_fwd_d{d}")
        branch += [_full(o, d), _full(l, d)]

    def combine_fn(o1, l1, o2, l2, o3, l3, nab, gb):
        m = jnp.maximum(jnp.maximum(l1, l2), l3)
        e1, e2, e3 = jnp.exp(l1 - m), jnp.exp(l2 - m), jnp.exp(l3 - m)
        den = e1 + e2 + e3
        ob = (e1 / den) * o1 + (e2 / den) * o2 + (e3 / den) * o3
        mixed = jnp.concatenate([nab, _rms(ob, gb).astype(BF16)], axis=1)
        lse = _head_cols(m + jnp.log(den))
        return (mixed, ob, lse) + (lse,) * len(DILATIONS), ()

    (sv.mixed, sv.ob, sv.lse_rows, *lses), _ = _rowwise(
        combine_fn, branch + [_full(na)], [w.gb],
        [(D_MODEL, BF16), (B_WIDTH, F32), (LANES, F32, HEAD_ROWS)] + [(LANES, F32, d) for d in DILATIONS], [],
        ts=512, name="combine")
    sv.lse = dict(zip(DILATIONS, lses))
    sv.y = _matmul(sv.mixed, w.big("w_out", sv.mixed), mode="nn", out_dtype=F32, name="mix_out")

    def mid_fn(xb, yb, g1, g2):
        x1 = xb + _rms(yb, g1)
        return (x1, _rms(x1, g2)), ()

    (sv.x1, sv.h2), _ = _rowwise(mid_fn, [_full(x), _full(sv.y)], [w.g_pm, w.g_pf], [(D_MODEL, F32), (D_MODEL, BF16)], [],
                                 ts=512, name="post_mix_norm")
    conv_w = w.big("conv_w", sv.h2)
    sv.ug, sv.uv, sv.yff = _ffn_up_geglu(sv.h2, w.big("w_up", sv.h2), conv_w[:, :D_FF], conv_w[:, D_FF:],
                                         w.cb_g, w.cb_v, name="ffn_up_geglu")
    sv.f = _matmul(sv.yff, w.big("w_down", sv.yff), mode="nn", out_dtype=F32, name="ffn_down")
    if last:
        return None, sv
    (x2,), _ = _rowwise(lambda xb, fb, g: ((xb + _rms(fb, g),), ()), [_full(sv.x1), _full(sv.f)], [w.g_post],
                        [(D_MODEL, F32)], [], ts=512, name="post_ffn_norm")
    return x2, sv


def _layer_bwd(dx2, sv, w, cos, sin, emit):
    g = {}

    def post_fn(fb, dxb, gp):
        _, vjp = jax.vjp(_rms, fb, gp)
        df, dg = vjp(dxb)
        return (df,), (dg,)

    (df,), (g["post_ffn_norm"],) = _rowwise(post_fn, [_full(sv.f), _full(dx2)], [w.g_post], [(D_MODEL, BF16)],
                                            [(1, D_MODEL)], ts=512, name="post_ffn_norm_bwd")
    big = {"w_down": _matmul(sv.yff, df, mode="tn", out_dtype=BF16, name="ffn_down_dw").reshape(N_DEV, -1, D_MODEL)}
    conv_w = w.big("conv_w", df)
    dug, duv, dh2, dwg, dwv = _ffn_geglu_bwd(sv.ug, sv.uv, df, w.big("w_down", df), w.big("w_up", df),
                                             conv_w[:, :D_FF], conv_w[:, D_FF:], w.cb_g, w.cb_v, name="ffn_geglu_bwd")
    big["conv_w"] = jnp.concatenate([dwg[0:3], dwv[0:3]], axis=1).reshape(3, N_DEV, D_MODEL).transpose(1, 0, 2)
    g["conv_b"] = jnp.concatenate([dwg[3], dwv[3]], axis=0)
    big["w_up"] = _matmul_by_destination(sv.h2, dug, duv, name="ffn_up_dw")
    g_pm = w.g_pm + emit(big)

    def mid_fn(x1b, yb, dhb, dxb, g1, g2):
        _, vjp2 = jax.vjp(_rms, x1b, g2)
        dx1h, dg2 = vjp2(dhb)
        dx1 = dxb + dx1h
        _, vjp1 = jax.vjp(_rms, yb, g1)
        dy, dg1 = vjp1(dx1)
        return (dx1, dy), (dg1, dg2)

    (dx1, dy), (g["post_mix_norm"], g["pre_ffn_norm"]) = _rowwise(
        mid_fn, [_full(sv.x1), _full(sv.y), _full(dh2), _full(dx2)], [g_pm, w.g_pf],
        [(D_MODEL, F32), (D_MODEL, BF16)], [(1, D_MODEL), (1, D_MODEL)], ts=256, name="post_mix_norm_bwd")
    dmixed = _matmul(dy, w.big("w_out", dy), mode="nt", out_dtype=F32, name="mix_out_dx")
    dw_out = _matmul(sv.mixed, dy, mode="tn", out_dtype=BF16, name="mix_out_dw").reshape(N_DEV, -1, D_MODEL)
    g_b = w.gb + emit({"w_out": dw_out})

    def attn_out_fn(obb, dmb, gb):
        _, vjp = jax.vjp(_rms, obb, gb)
        do, dgb = vjp(dmb)
        delta = _head_cols(do * obb, whole_head=True)
        return (delta,) + (do,) * len(DILATIONS) + (delta,) * len(DILATIONS), (dgb,)

    (delta_rows, *outs), (g["out_norm_b"],) = _rowwise(
        attn_out_fn, [_full(sv.ob), (dmixed, B_WIDTH, 1)], [g_b],
        [(LANES, F32, HEAD_ROWS)] + [(B_WIDTH, BF16, d) for d in DILATIONS] + [(LANES, F32, d) for d in DILATIONS],
        [(1, B_WIDTH)], ts=512, name="attn_out_bwd")
    do = dict(zip(DILATIONS, outs[:len(DILATIONS)]))
    delta = dict(zip(DILATIONS, outs[len(DILATIONS):]))
    parts = {"q": [], "k": [], "v": []}
    for d in DILATIONS:
        qv, kv, vv = sv.qkv[d]
        dq = _attn_bwd_q(qv, kv, vv, do[d], sv.lse[d], delta[d], d, name=f"attn_bwd_q_d{d}")
        dk, dv = _attn_bwd_kv(qv, kv, vv, do[d], _spread_rows(sv.lse_rows, d), _spread_rows(delta_rows, d), d,
                              name=f"attn_bwd_kv_d{d}")
        parts["q"].append(_full(dq, d))
        parts["k"].append(_full(dk, d))
        parts["v"].append(_full(dv, d))

    def rope_bwd_fn(q1, q2, q3, k1, k2, k3, v1, v2, v3, cs, sn):
        cs, sn = _all_heads(cs), _all_heads(sn)

        def back(t):
            return t * cs - _rot_half(t * sn)
        return (jnp.concatenate([back(q1 + q2 + q3), back(k1 + k2 + k3), v1 + v2 + v3], axis=1),), ()

    (dzb,), _ = _rowwise(rope_bwd_fn, parts["q"] + parts["k"] + parts["v"] + [_full(cos), _full(sin)], [],
                         [(3 * B_WIDTH, BF16)], [], ts=256, name="rope_bwd")

    gate_consts = [w.vg, w.vb, *w.ws, w.bfull, w.ga]

    def gate_bwd_fn(zu, zv, dna, dzb_rows, *consts):
        dz, sums = [], None
        for r in _chunk_rows(zu.shape[0]):
            _, vjp = jax.vjp(_gate_fn, zu[r], zv[r], *consts)
            grads = vjp(dna[r])
            dz.append(jnp.concatenate([grads[0].astype(BF16), grads[1].astype(BF16), dzb_rows[r]], axis=1))
            sums = grads[2:] if sums is None else tuple(a + b for a, b in zip(sums, grads[2:]))
        return (jnp.concatenate(dz, axis=0),), tuple(sums)

    (dproj,), gsmall = _rowwise(
        gate_bwd_fn, [(sv.proj, A_WIDTH, 0), (sv.proj, A_WIDTH, 1), (dmixed, A_WIDTH, 0), _full(dzb)], gate_consts,
        [(IN_COLS, BF16)], [c.shape for c in gate_consts], ts=GATE_TS, name="gate_bwd")
    g["v_norm_g"], g["v_norm_b"] = gsmall[0], gsmall[1]
    g["w_spatial"] = jnp.stack(gsmall[2:6])
    g["b_spatial"] = _bias_reduce(gsmall[6], name="bias_reduce")[:, :A_GROUPS].T
    g["out_norm_a"] = gsmall[7]

    dh1 = _matmul(dproj, w.big("w_in", dproj), mode="nn", out_dtype=F32, name="proj_dx")
    dw_in = _matmul(dproj, sv.h1, mode="tn", out_dtype=BF16, name="proj_dw").reshape(N_DEV, -1, D_MODEL)
    g_pre = w.g_pre + emit({"w_in": dw_in})

    def pre_fn(xb, dhb, dxb, gp):
        _, vjp = jax.vjp(_rms, xb, gp)
        dxh, dg = vjp(dhb)
        return (dxb + dxh,), (dg,)

    (dx,), (g["pre_mix_norm"],) = _rowwise(pre_fn, [_full(sv.x), _full(dh1), _full(dx1)], [g_pre], [(D_MODEL, F32)],
                                           [(1, D_MODEL)], ts=512, name="pre_mix_norm_bwd")
    return dx, g


def _layer_weights(l, full, small):
    row = lambda a: a[l].reshape(1, -1)
    return types.SimpleNamespace(
        big=functools.partial(full, l),
        g_pre=row(small["pre_mix_norm"]), vg=row(small["v_norm_g"]), vb=row(small["v_norm_b"]),
        ws=[small["w_spatial"][l, gi] for gi in range(A_GROUPS)],
        bfull=jnp.repeat(small["b_spatial"][l].T, CHUNK, axis=1),
        ga=row(small["out_norm_a"]), gb=row(small["out_norm_b"]),
        g_pm=row(small["post_mix_norm"]), g_pf=row(small["pre_ffn_norm"]),
        cb_g=small["conv_b"][l][:D_FF].reshape(1, -1), cb_v=small["conv_b"][l][D_FF:].reshape(1, -1),
        g_post=row(small["post_ffn_norm"]))


def _local_step(x, target, full, small, emit, emit_small, started):
    s = x.shape[0]
    cos, sin = _rope_tables(s)
    ws = [_layer_weights(l, full, small) for l in range(N_LAYERS)]
    ws[0].g_pre = ws[0].g_pre + started
    saved = []
    h = x
    for l in range(N_LAYERS):
        h, sv = _layer_fwd(h, ws[l], cos, sin, last=l == N_LAYERS - 1)
        saved.append(sv)

    def loss_fn(xb, fb, tb, g):
        diff = (xb + _rms(fb, g)) - tb
        return (diff * (1.0 / D_MODEL),), (jnp.sum(diff * diff, axis=0, keepdims=True),)

    (dh,), (sq,) = _rowwise(loss_fn, [_full(saved[-1].x1), _full(saved[-1].f), _full(target)], [ws[-1].g_post],
                            [(D_MODEL, F32)], [(1, D_MODEL)], ts=512, name="loss")
    loss = 0.5 * jnp.sum(sq) * (1.0 / D_MODEL)
    grads = [None] * N_LAYERS
    for l in reversed(range(N_LAYERS)):
        dh, grads[l] = _layer_bwd(dh, saved[l], ws[l], cos, sin, functools.partial(emit, l))
        token = emit_small(l, grads[l], loss)
        if l > 0:
            ws[l - 1].g_post = ws[l - 1].g_post + token
    return loss, dh, grads


def _place():
    return lax.axis_index("x"), lax.axis_index("y"), lax.axis_index("c")


FLIPS = ((1, 0, 0), (0, 1, 0), (1, 1, 0), (0, 0, 1), (1, 0, 1), (0, 1, 1), (1, 1, 1))
HBM_SPEC = pl.BlockSpec(memory_space=pltpu.HBM)
SEM_SPEC = pl.BlockSpec(memory_space=pltpu.SEMAPHORE)
SPLIT_COPY = pltpu.CompilerParams(has_side_effects=pltpu.SideEffectType.DATAFLOW_SIDE_EFFECTING)


def _peers():
    mx, my, mc = _place()
    out = []
    for fx, fy, fc in FLIPS:
        px, py, pc = (1 - mx if fx else mx), (1 - my if fy else my), (1 - mc if fc else mc)
        out.append(((px, py, pc), 4 * px + 2 * py + pc))
    return out


def _flat_copies(scatter, src_refs, land_refs, send_sems, recv_sems):
    mx, my, mc = _place()
    me = 4 * mx + 2 * my + mc
    n = len(src_refs)
    copies = []
    for t in range(n):
        for i, (peer, number) in enumerate(_peers()):
            copies.append(pltpu.make_async_remote_copy(
                src_ref=src_refs[t].at[number] if scatter else src_refs[t],
                dst_ref=land_refs[t].at[i] if scatter else land_refs[t].at[me],
                send_sem=send_sems.at[t * len(FLIPS) + i], recv_sem=recv_sems.at[t * len(FLIPS) + i],
                device_id=peer, device_id_type=MESH_ID))
    return copies


def _flat_start(arrays, scatter, name):
    n = len(arrays)
    slots = len(FLIPS) if scatter else N_DEV
    lands = [lax.empty((slots,) + (a.shape[1:] if scatter else a.shape), a.dtype) for a in arrays]

    def body(*refs):
        src, land, (send_sems, recv_sems), token = refs[:n], refs[n:2 * n], refs[2 * n:2 * n + 2], refs[-1]
        for cp in _flat_copies(scatter, src, land, send_sems, recv_sems):
            cp.start()
        token[...] = jnp.zeros_like(token)

    hbm = [pltpu.HBM(a.shape, a.dtype) for a in arrays] + [pltpu.HBM(a.shape, a.dtype) for a in lands]
    sems = pltpu.SemaphoreType.DMA((n * len(FLIPS),))
    outs = pl.pallas_call(
        body, name=name, out_shape=(sems, sems, *hbm, jax.ShapeDtypeStruct((8, 128), F32)),
        in_specs=[HBM_SPEC] * (2 * n),
        out_specs=(SEM_SPEC, SEM_SPEC, *([HBM_SPEC] * (2 * n)), pl.BlockSpec(memory_space=pltpu.VMEM)),
        input_output_aliases={t: 2 + t for t in range(2 * n)}, compiler_params=SPLIT_COPY,
    )(*[pltpu.with_memory_space_constraint(a, pltpu.HBM) for a in (*arrays, *lands)])
    return types.SimpleNamespace(sems=outs[:2], thru=outs[2:2 + 2 * n], scatter=scatter, n=n), outs[-1][0:1, 0:1]


def _flat_wait(handle, after, name):
    n = handle.n

    def body(*refs):
        src, land, (send_sems, recv_sems) = refs[:n], refs[n:2 * n], refs[2 * n:2 * n + 2]
        for cp in _flat_copies(handle.scatter, src, land, send_sems, recv_sems):
            cp.wait_send()
            cp.wait_recv()

    outs = pl.pallas_call(
        body, name=name, out_shape=tuple(pltpu.HBM(a.shape, a.dtype) for a in handle.thru),
        in_specs=[HBM_SPEC] * (2 * n) + [SEM_SPEC, SEM_SPEC, ANY], out_specs=tuple([HBM_SPEC] * (2 * n)),
        input_output_aliases={t: t for t in range(2 * n)}, compiler_params=SPLIT_COPY,
    )(*handle.thru, *handle.sems, after)
    return outs[:n], outs[n:]


def _adamw(w, g, m, v):
    m2 = ADAM_B1 * m + (1.0 - ADAM_B1) * g
    v2 = ADAM_B2 * v + (1.0 - ADAM_B2) * (g * g)
    m_hat = m2 / (1.0 - ADAM_B1 ** ADAM_STEP)
    v_hat = v2 / (1.0 - ADAM_B2 ** ADAM_STEP)
    return -ADAM_LR * (m_hat / (jnp.sqrt(v_hat) + ADAM_EPS) + ADAM_WD * w), m2, v2


def _adamw_sharded(me, mine, landed, w, m, v, tr, name):
    _, r, c = w.shape
    nt = r // tr
    assert r % tr == 0 and len(mine) == len(landed) == N_LAYERS == 2, name
    per_layer = 1 + len(FLIPS)

    def body(me_ref, *refs):
        terms, (w_ref, m_ref, v_ref), outs = refs[:2 * per_layer], refs[2 * per_layer:2 * per_layer + 3], refs[-4:]
        layer = pl.program_id(0)

        def total(group):
            g = group[0][0].astype(F32)
            for t in group[1:]:
                g = g + t[0].astype(F32)
            return g

        g = jnp.where(layer == 0, total(terms[:per_layer]), total(terms[per_layer:]))
        d, m2, v2 = _adamw(w_ref[0], g, m_ref[0], v_ref[0])
        for o, val in zip(outs, (g, d, m2, v2)):
            o[0] = val

    def held(l):
        return lambda layer, i: jnp.where(layer == l, i, nt - 1 if l == 0 else 0)

    in_specs = []
    for l in range(N_LAYERS):
        rows = held(l)
        in_specs.append(pl.BlockSpec((1, tr, c), functools.partial(lambda layer, i, me_ref, rows: (me_ref[0], rows(layer, i), 0), rows=rows)))
        for k in range(len(FLIPS)):
            in_specs.append(pl.BlockSpec(
                (1, tr, c), functools.partial(lambda layer, i, me_ref, rows, k: (k, rows(layer, i), 0), rows=rows, k=k)))
    tile = pl.BlockSpec((1, tr, c), lambda layer, i, me_ref: (layer, i, 0))
    operands = []
    for l in range(N_LAYERS):
        operands += [mine[l]] + [landed[l]] * len(FLIPS)
    return pl.pallas_call(
        body, name=name, out_shape=[jax.ShapeDtypeStruct(w.shape, F32)] * 4,
        grid_spec=pltpu.PrefetchScalarGridSpec(
            num_scalar_prefetch=1, grid=(N_LAYERS, nt), in_specs=in_specs + [tile] * 3, out_specs=[tile] * 4),
        compiler_params=_params("arbitrary", "arbitrary"),
    )(me, *operands, w, m, v)


def _adamw_replicated(parts, w, m, v, name):
    n_layers = len(parts)

    def body(*refs):
        p_refs, (w_ref, m_ref, v_ref), outs = refs[:n_layers], refs[n_layers:n_layers + 3], refs[n_layers + 3:]
        for l, p_ref in enumerate(p_refs):
            g = p_ref[0]
            for j in range(1, N_DEV):
                g = g + p_ref[j]
            d, m2, v2 = _adamw(w_ref[l], g, m_ref[l], v_ref[l])
            for o, val in zip(outs, (g, d, m2, v2)):
                o[l] = val

    return pl.pallas_call(body, name=name, out_shape=[jax.ShapeDtypeStruct(w.shape, F32)] * 4,
                          compiler_params=pltpu.CompilerParams(vmem_limit_bytes=VMEM_LIMIT_BYTES))(*parts, w, m, v)


SMALL_USED = 80384


def _pack_small(vals, rider=None):
    pieces = [vals[n].reshape(-1) for n in SMALL_NAMES] + ([] if rider is None else [rider.reshape(1)])
    flat = jnp.concatenate(pieces)
    assert flat.shape[0] == SMALL_USED + (rider is not None)
    return jnp.concatenate([flat, jnp.zeros((SMALL_ROWS * D_MODEL - flat.shape[0],), F32)]).reshape(SMALL_ROWS, D_MODEL)


def _pack_small_layers(vals):
    flat = jnp.concatenate([vals[n].reshape(N_LAYERS, -1) for n in SMALL_NAMES], axis=1)
    return jnp.pad(flat, ((0, 0), (0, SMALL_ROWS * D_MODEL - SMALL_USED))).reshape(N_LAYERS, SMALL_ROWS, D_MODEL)


def _unpack_small_layers(packed, shapes):
    flat, out, at = packed.reshape(N_LAYERS, -1), {}, 0
    for n in SMALL_NAMES:
        size = math.prod(shapes[n][1:])
        out[n] = flat[:, at:at + size].reshape(shapes[n])
        at += size
    return out


GATHER_GROUPS = ((0, ("w_in",)), (0, ("w_out", "w_up", "conv_w")), (0, ("w_down",)),
                 (1, ("w_in",)), (1, ("w_out", "w_up", "conv_w")), (1, ("w_down",)))
ADAMW_TILE_ROWS = {"w_in": 320, "w_out": 128, "w_up": 256, "w_down": 256, "conv_w": 3}


def _assemble(name, land):
    if name == "w_in":
        return land.reshape(IN_COLS, D_MODEL)
    if name == "conv_w":
        return land.transpose(1, 0, 2).reshape(3, 2 * D_FF)
    if name == "w_up":
        return land
    return land.reshape(-1, D_MODEL)


def _start_gathers(wts, me):
    started, groups = jnp.zeros((1, 1), F32), []
    for gi, (l, names) in enumerate(GATHER_GROUPS):
        local = {"conv_w": lambda a: a, "w_in": lambda a: a.T.astype(BF16)}
        blocks = [local.get(n, lambda a: a.astype(BF16))(wts[n][l]) for n in names]
        if gi > 0:
            blocks[0] = blocks[0] + token.astype(blocks[0].dtype)
        handle, token = _flat_start(blocks, False, name=f"gather_start_{gi}")
        groups.append(types.SimpleNamespace(layer=l, names=names, blocks=blocks, handle=handle, got=None, index=gi))
        started = started + token

    def fetch(l, name, after):
        grp = next(gr for gr in groups if gr.layer == l and name in gr.names)
        if grp.got is None:
            lands = _flat_wait(grp.handle, after, name=f"gather_wait_{grp.index}")[1]
            grp.got = {}
            for n, blk, land in zip(grp.names, grp.blocks, lands):
                own = (me,) + (0,) * blk.ndim
                grp.got[n] = _assemble(n, lax.dynamic_update_slice(land, blk[None], own))
        return grp.got[name]

    return fetch, started


def kernel(x, pre_mix_norm, w_in, v_norm_g, v_norm_b, w_spatial, b_spatial, out_norm_a, out_norm_b, w_out, post_mix_norm, pre_ffn_norm, w_up, conv_w, conv_b, w_down, post_ffn_norm, loss_target, m_pre_mix_norm, m_w_in, m_v_norm_g, m_v_norm_b, m_w_spatial, m_b_spatial, m_out_norm_a, m_out_norm_b, m_w_out, m_post_mix_norm, m_pre_ffn_norm, m_w_up, m_conv_w, m_conv_b, m_w_down, m_post_ffn_norm, v_pre_mix_norm, v_w_in, v_v_norm_g, v_v_norm_b, v_w_spatial, v_b_spatial, v_out_norm_a, v_out_norm_b, v_w_out, v_post_mix_norm, v_pre_ffn_norm, v_w_up, v_conv_w, v_conv_b, v_w_down, v_post_ffn_norm):
    wts = dict(zip(WEIGHT_NAMES, (pre_mix_norm, w_in, v_norm_g, v_norm_b, w_spatial, b_spatial, out_norm_a, out_norm_b,
                                  w_out, post_mix_norm, pre_ffn_norm, w_up, conv_w, conv_b, w_down, post_ffn_norm)))
    mom1 = dict(zip(WEIGHT_NAMES, (m_pre_mix_norm, m_w_in, m_v_norm_g, m_v_norm_b, m_w_spatial, m_b_spatial, m_out_norm_a,
                                   m_out_norm_b, m_w_out, m_post_mix_norm, m_pre_ffn_norm, m_w_up, m_conv_w, m_conv_b,
                                   m_w_down, m_post_ffn_norm)))
    mom2 = dict(zip(WEIGHT_NAMES, (v_pre_mix_norm, v_w_in, v_v_norm_g, v_v_norm_b, v_w_spatial, v_b_spatial, v_out_norm_a,
                                   v_out_norm_b, v_w_out, v_post_mix_norm, v_pre_ffn_norm, v_w_up, v_conv_w, v_conv_b,
                                   v_w_down, v_post_ffn_norm)))
    mx, my, mc = _place()
    me = 4 * mx + 2 * my + mc

    fetch, started = _start_gathers(wts, me)
    scatters = []

    def emit(l, blocks):
        names = tuple(blocks)
        handle, token = _flat_start([blocks[n] for n in names], True, name=f"scatter_start_{l}_{'_'.join(names)}")
        scatters.append((l, names, handle))
        return token

    smalls = {}

    def emit_small(l, g, loss_local):
        pack = _pack_small({n: g[n] for n in SMALL_NAMES}, rider=loss_local if l == 0 else None)
        handle, token = _flat_start([pack], False, name=f"small_grads_start_{l}")
        smalls[l] = (pack, handle)
        return token

    _, dx, _ = _local_step(x[0], loss_target[0], fetch, wts, emit, emit_small, started)

    me_arr = jnp.reshape(me, (1,)).astype(jnp.int32)
    big_out = [{}, {}, {}, {}]

    def finish(group, after):
        mine, landed = {}, {}
        for l, names, handle in scatters:
            if names == group:
                sent, lands = _flat_wait(handle, after, name=f"scatter_wait_{l}_{'_'.join(names)}")
                for n, a, b in zip(names, sent, lands):
                    mine[l, n], landed[l, n] = a, b
        for n in group:
            flip = (lambda a: a.transpose(0, 2, 1)) if n == "w_in" else (lambda a: a)
            res = _adamw_sharded(me_arr, [mine[l, n] for l in range(N_LAYERS)], [landed[l, n] for l in range(N_LAYERS)],
                                 flip(wts[n]), flip(mom1[n]), flip(mom2[n]), ADAMW_TILE_ROWS[n], name=f"adamw_{n}")
            for kind in range(4):
                big_out[kind][n] = flip(res[kind])
        return res[0]

    done = smalls[0][1].thru[0]
    for group in [names for l, names, _ in scatters if l == 0][:-1]:
        done = finish(group, done)
    everyone = [None] * N_LAYERS
    for l in reversed(range(N_LAYERS)):
        pack, handle = smalls[l]
        (landed,) = _flat_wait(handle, done, name=f"small_grads_wait_{l}")[1]
        everyone[l] = lax.dynamic_update_slice(landed, pack[None], (me, 0, 0))
    packs = [_pack_small_layers({n: t[n] for n in SMALL_NAMES}) for t in (wts, mom1, mom2)]
    res = _adamw_replicated(everyone, *packs, name="adamw_replicated")
    loss = res[0][0].reshape(-1)[SMALL_USED]
    finish(scatters[-1][1], res[0])
    small_shapes = {n: wts[n].shape for n in SMALL_NAMES}
    small_out = [_unpack_small_layers(o, small_shapes) for o in res]

    outs = [loss, dx[None]]
    for kind in range(4):
        outs += [big_out[kind][n] if n in BIG_NAMES else small_out[kind][n] for n in WEIGHT_NAMES]
    return tuple(outs)
```

```python
import functools
import math
import types

import jax
import jax.numpy as jnp
from jax import lax
from jax.experimental import pallas as pl
from jax.experimental.pallas import tpu as pltpu

F32 = jnp.float32
BF16 = jnp.bfloat16

D_MODEL = 1024
A_WIDTH = 512
A_GROUPS = 4
CHUNK = 128
B_WIDTH = 512
HEAD_DIM = 64
N_HEADS = B_WIDTH // HEAD_DIM
ROT_DIM = 16
ROPE_THETA = 500000.0
BAND = 128
DILATIONS = (1, 4, 16)
IN_COLS = 2560
D_FF = 4096
EPS = 1e-6
NEG_INF = -1e30
N_DEV = 8
N_LAYERS = 2

ADAM_LR = 0.001
ADAM_B1 = 0.9
ADAM_B2 = 0.999
ADAM_EPS = 1e-08
ADAM_WD = 0.01
ADAM_STEP = 10

VMEM_LIMIT_BYTES = 56 * 1024 * 1024
MESH_ID = pl.DeviceIdType.MESH
ANY = pl.BlockSpec(memory_space=pl.ANY)

WEIGHT_NAMES = ("pre_mix_norm", "w_in", "v_norm_g", "v_norm_b", "w_spatial", "b_spatial", "out_norm_a", "out_norm_b",
                "w_out", "post_mix_norm", "pre_ffn_norm", "w_up", "conv_w", "conv_b", "w_down", "post_ffn_norm")
BIG_NAMES = ("w_in", "w_out", "w_up", "w_down", "conv_w")
SMALL_NAMES = tuple(n for n in WEIGHT_NAMES if n not in BIG_NAMES)

SMALL_ROWS = 80


def _params(*sem):
    return pltpu.CompilerParams(dimension_semantics=sem, vmem_limit_bytes=VMEM_LIMIT_BYTES)


def _dotg(a, b, ca, cb):
    return lax.dot_general(a.astype(BF16), b.astype(BF16), (((ca,), (cb,)), ((), ())), preferred_element_type=F32)


@jax.custom_vjp
def _bdot(a, b):
    return _dotg(a, b, 1, 0)


def _bdot_fwd(a, b):
    return _dotg(a, b, 1, 0), (a, b)


def _bdot_bwd(res, g):
    a, b = res
    return _dotg(g, b, 1, 1), _dotg(a, g, 0, 0)


_bdot.defvjp(_bdot_fwd, _bdot_bwd)


def _rms(x, g):
    return x * lax.rsqrt(jnp.mean(x * x, axis=-1, keepdims=True) + EPS) * g


def _layernorm(x, g, b):
    mu = jnp.mean(x, axis=-1, keepdims=True)
    xc = x - mu
    return xc * lax.rsqrt(jnp.mean(xc * xc, axis=-1, keepdims=True) + EPS) * g + b


def _gelu_erf(x):
    return x * (lax.erf(x * (1.0 / math.sqrt(2.0))) + 1.0) * 0.5


def _gelu_tanh(x):
    c = math.sqrt(2.0 / math.pi)
    return 0.5 * x * (1.0 + jnp.tanh(c * (x + 0.044715 * (x * x * x))))


def _gelu_tanh_and_slope(x):
    c, k = math.sqrt(2.0 / math.pi), 0.044715
    x2 = x * x
    t = jnp.tanh(c * (x + k * (x2 * x)))
    half_x, one_t = 0.5 * x, 1.0 + t
    return half_x * one_t, 0.5 * one_t + (half_x * (1.0 - t * t)) * (c + (3.0 * k * c) * x2)


def _rot_half(x):
    width = x.shape[1]
    lane = lax.broadcasted_iota(jnp.int32, x.shape, 1) % HEAD_DIM
    back = pltpu.roll(x, ROT_DIM // 2, 1)
    fwd = pltpu.roll(x, width - ROT_DIM // 2, 1)
    return jnp.where(lane < ROT_DIM // 2, -fwd, jnp.where(lane < ROT_DIM, back, 0.0))


def _split3(z):
    h0 = z.astype(BF16)
    r1 = z - h0.astype(F32)
    h1 = r1.astype(BF16)
    h2 = (r1 - h1.astype(F32)).astype(BF16)
    return h0, h1, h2


MATMUL_VMEM_BUDGET = 40 * 1024 * 1024


def _matmul_tiles(m, n, k, out_bytes):
    tn = n if n <= 1024 else (1280 if n % 1280 == 0 and n % 1024 else 1024)
    tk = k if k <= 1024 else (1280 if k % 1280 == 0 and k % 1024 else 1024)
    tm = m
    while tm > 256:
        blocks = 2 * 2 * (tm * tk + tk * tn) + 2 * out_bytes * tm * tn + (4 * tm * tn if k > tk else 0)
        if blocks <= MATMUL_VMEM_BUDGET and m % tm == 0:
            break
        tm //= 2
    return tm, tn, tk


def _matmul(a, b, *, mode, out_dtype, name, cols=None):
    wide = D_MODEL if cols is not None else None
    if mode == "nn":
        (m, k), (_, n) = a.shape, (b.shape if cols is None else (b.shape[1], cols[1] * wide))
    elif mode == "nt":
        (m, k), (n, _) = a.shape, (b.shape if cols is None else (b.shape[1], cols[1] * wide))
    else:
        (k, m), (_, n) = a.shape, b.shape
    tm, tn, tk = _matmul_tiles(m, n, k, jnp.dtype(out_dtype).itemsize)
    assert m % tm == 0 and n % tn == 0 and k % tk == 0, (name, m, n, k)
    nk = k // tk
    if mode == "nn":
        a_spec = pl.BlockSpec((tm, tk), lambda i, j, kk: (i, kk))
        b_spec = pl.BlockSpec((tk, tn), lambda i, j, kk: (kk, j))
        if cols is not None:
            assert tn == wide
            b_spec = pl.BlockSpec((None, tk, tn), lambda i, j, kk: (cols[0] + j, kk, 0))
        ca, cb = 1, 0
    elif mode == "nt":
        a_spec = pl.BlockSpec((tm, tk), lambda i, j, kk: (i, kk))
        b_spec = pl.BlockSpec((tn, tk), lambda i, j, kk: (j, kk))
        if cols is not None:
            assert tk == wide
            b_spec = pl.BlockSpec((None, tn, tk), lambda i, j, kk: (cols[0] + kk, j, 0))
        ca, cb = 1, 1
    else:
        a_spec = pl.BlockSpec((tk, tm), lambda i, j, kk: (kk, i))
        b_spec = pl.BlockSpec((tk, tn), lambda i, j, kk: (kk, j))
        ca, cb = 0, 0

    def body(a_ref, b_ref, o_ref, *acc):
        kk = pl.program_id(2)
        part = lax.dot_general(a_ref[...], b_ref[...], (((ca,), (cb,)), ((), ())), preferred_element_type=F32)
        if nk == 1:
            o_ref[...] = part.astype(o_ref.dtype)
            return
        acc_ref, = acc

        @pl.when(kk == 0)
        def _():
            acc_ref[...] = part

        @pl.when(kk > 0)
        def _():
            acc_ref[...] += part

        @pl.when(kk == nk - 1)
        def _():
            o_ref[...] = acc_ref[...].astype(o_ref.dtype)

    return pl.pallas_call(
        body, name=name, grid=(m // tm, n // tn, nk),
        in_specs=[a_spec, b_spec], out_specs=pl.BlockSpec((tm, tn), lambda i, j, kk: (i, j)),
        out_shape=jax.ShapeDtypeStruct((m, n), out_dtype),
        scratch_shapes=[pltpu.VMEM((tm, tn), F32)] if nk > 1 else [],
        compiler_params=_params("parallel", "parallel", "arbitrary"),
    )(a, b)


def _matmul_by_destination(a, b_lo, b_hi, *, name, tm=1024, tk=2048):
    (k, m), half = a.shape, N_DEV // 2
    assert b_lo.shape == b_hi.shape == (k, half * D_MODEL) and m % tm == 0 and k % tk == 0, name
    nk = k // tk

    def body(a_ref, lo_ref, hi_ref, o_ref, acc_ref):
        j, kk = pl.program_id(1), pl.program_id(2)

        def step(b_ref):
            part = lax.dot_general(a_ref[...], b_ref[...], (((0,), (0,)), ((), ())), preferred_element_type=F32)

            @pl.when(kk == 0)
            def _():
                acc_ref[...] = part

            @pl.when(kk > 0)
            def _():
                acc_ref[...] += part

        pl.when(j < half)(lambda: step(lo_ref))
        pl.when(j >= half)(lambda: step(hi_ref))

        @pl.when(kk == nk - 1)
        def _():
            o_ref[...] = acc_ref[...].astype(o_ref.dtype)

    lo_spec = pl.BlockSpec((tk, D_MODEL), lambda i, j, kk: (jnp.where(j < half, kk, nk - 1), jnp.minimum(j, half - 1)))
    hi_spec = pl.BlockSpec((tk, D_MODEL), lambda i, j, kk: (jnp.where(j >= half, kk, 0), jnp.maximum(j - half, 0)))
    return pl.pallas_call(
        body, name=name, grid=(m // tm, N_DEV, nk),
        in_specs=[pl.BlockSpec((tk, tm), lambda i, j, kk: (kk, i)), lo_spec, hi_spec],
        out_specs=pl.BlockSpec((None, tm, D_MODEL), lambda i, j, kk: (j, i, 0)),
        out_shape=jax.ShapeDtypeStruct((N_DEV, m, D_MODEL), BF16),
        scratch_shapes=[pltpu.VMEM((tm, D_MODEL), F32)],
        compiler_params=_params("parallel", "parallel", "arbitrary"),
    )(a, b_lo, b_hi)


LANES = 128


def _residues_to_rows(ref, scr, d):
    w = ref.shape[1] // d
    n = ref.shape[0]
    for r in range(d):
        for c in range(w // LANES):
            scr[c, pl.ds(r, n, stride=d), :] = ref[:, r * w + c * LANES:r * w + (c + 1) * LANES].astype(F32)
    return jnp.concatenate([scr[c] for c in range(w // LANES)], axis=1)


def _rows_to_residues(val, ref, scr, d):
    w = val.shape[1]
    n = ref.shape[0]
    for c in range(w // LANES):
        scr[c] = val[:, c * LANES:(c + 1) * LANES].astype(F32)
    for r in range(d):
        for c in range(w // LANES):
            ref[:, r * w + c * LANES:r * w + (c + 1) * LANES] = scr[c, pl.ds(r, n, stride=d), :].astype(ref.dtype)


HEAD_ROWS = 0


def _head_cols(z, whole_head=False):
    width = z.shape[1]
    a = lax.broadcasted_iota(jnp.int32, (width, LANES), 0)
    b = lax.broadcasted_iota(jnp.int32, (width, LANES), 1)
    pick = jnp.where((a // HEAD_DIM == b) if whole_head else (a == b * HEAD_DIM), 1.0, 0.0).astype(BF16)
    out = None
    for part in _split3(z):
        t = lax.dot_general(part, pick, (((1,), (0,)), ((), ())), preferred_element_type=F32)
        out = t if out is None else out + t
    return out


def _head_rows_block(cols):
    return cols.T[:N_HEADS, :]


def _rowwise(fn, rows, consts, out_rows, out_acc, *, ts, name):
    rows = [tuple(r) + (1,) * (4 - len(r)) for r in rows]
    out_rows = [tuple(o) + (1,) * (3 - len(o)) for o in out_rows]
    s = rows[0][0].shape[0] * rows[0][3]
    assert s % ts == 0, (name, s, ts)
    n_rows, n_in = len(rows), len(rows) + len(consts)
    n_row = len(out_rows)
    n_out = n_row + len(out_acc)
    moved = [(idx, w) for idx, (_, w, _, d) in enumerate(rows) if d > 1]
    moved += [(n_rows + idx, w) for idx, (w, _, d) in enumerate(out_rows) if d > 1]

    def body(*refs):
        scratch = dict(zip([key for key, _ in moved], refs[n_in + n_out:]))
        vals = []
        for idx, r in enumerate(refs[:n_in]):
            d = rows[idx][3] if idx < n_rows else 1
            vals.append(r[...] if d == 1 else _residues_to_rows(r, scratch[idx], d))
        row_vals, acc_vals = fn(*vals)
        for idx, (r, v) in enumerate(zip(refs[n_in:n_in + n_row], row_vals)):
            d = out_rows[idx][2]
            if d == 1:
                r[...] = v.astype(r.dtype)
            elif d == HEAD_ROWS:
                r[...] = _head_rows_block(v)
            else:
                _rows_to_residues(v, r, scratch[n_rows + idx], d)
        first = pl.program_id(0) == 0
        for r, v in zip(refs[n_in + n_row:n_in + n_out], acc_vals):
            @pl.when(first)
            def _(r=r, v=v):
                r[...] = v

            @pl.when(jnp.logical_not(first))
            def _(r=r, v=v):
                r[...] += v

    in_specs = [pl.BlockSpec((ts // d, d * w), functools.partial(lambda i, cb: (i, cb), cb=cb)) for _, w, cb, d in rows]
    in_specs += [pl.BlockSpec(c.shape, lambda i: (0, 0)) for c in consts]
    out_specs = [pl.BlockSpec((N_HEADS, ts), lambda i: (0, i)) if d == HEAD_ROWS else
                 pl.BlockSpec((ts // d, d * w), lambda i: (i, 0)) for w, _, d in out_rows]
    out_specs += [pl.BlockSpec(sh, lambda i: (0, 0)) for sh in out_acc]
    out_shape = [jax.ShapeDtypeStruct((N_HEADS, s) if d == HEAD_ROWS else (s // d, d * w), dt) for w, dt, d in out_rows]
    out_shape += [jax.ShapeDtypeStruct(sh, F32) for sh in out_acc]
    outs = pl.pallas_call(
        body, name=name, grid=(s // ts,), in_specs=in_specs, out_specs=out_specs, out_shape=out_shape,
        scratch_shapes=[pltpu.VMEM((w // LANES, ts, LANES), F32) for _, w in moved],
        compiler_params=_params("arbitrary" if out_acc else "parallel"),
    )(*[a for a, _, _, _ in rows], *consts)
    return outs[:n_row], outs[n_row:]


def _full(a, d=1):
    return (a, a.shape[1] // d, 0, d)


def _gate_fn(zu, zv, vg, vb, ws0, ws1, ws2, ws3, bfull, ga):
    u = _gelu_erf(zu)
    vn = _layernorm(_gelu_erf(zv), vg, vb)
    p = lax.broadcasted_iota(jnp.int32, (CHUNK, CHUNK), 0)
    q = lax.broadcasted_iota(jnp.int32, (CHUNK, CHUNK), 1)
    tril = jnp.where(q <= p, 1.0, 0.0)
    group = lax.broadcasted_iota(jnp.int32, (1, A_WIDTH), 1) // CHUNK
    sg = bfull
    for g, w in enumerate((ws0, ws1, ws2, ws3)):
        sg = sg + _bdot(w * tril, jnp.where(group == g, vn, 0.0))
    return _rms(u * sg, ga)


GATE_TS = 4 * CHUNK


def _chunk_rows(rows):
    return [slice(c * CHUNK, (c + 1) * CHUNK) for c in range(rows // CHUNK)]


def _bias_reduce(dbf, name):
    def body(x_ref, o_ref):
        lane = lax.broadcasted_iota(jnp.int32, (CHUNK, CHUNK), 1)
        out = jnp.zeros((CHUNK, CHUNK), F32)
        for g in range(A_GROUPS):
            out = jnp.where(lane == g, jnp.sum(x_ref[:, g * CHUNK:(g + 1) * CHUNK], axis=1, keepdims=True), out)
        o_ref[...] = out

    return pl.pallas_call(body, name=name, out_shape=jax.ShapeDtypeStruct((CHUNK, CHUNK), F32))(dbf)


def _pair_mask(hh):
    lane = lax.broadcasted_iota(jnp.int32, (1, 2 * HEAD_DIM), 1)
    return (lane >= HEAD_DIM * hh) & (lane < HEAD_DIM * (hh + 1))


def _lane_pick(x2, lm):
    return jnp.max(jnp.where(lm, x2, -jnp.inf), axis=1, keepdims=True)


SCALE = HEAD_DIM ** -0.5


def _blocks_per_step(nb):
    return next(n for n in (4, 2, 1) if nb % n == 0)


def _units(nblk):
    return [(b, hp, hh) for b in range(nblk) for hp in range(N_HEADS // 2) for hh in range(2)]


def _attn_specs(nb, nblk):
    cur = pl.BlockSpec((nblk * BAND, B_WIDTH), lambda r, j: (j, r))
    prev = pl.BlockSpec((BAND, B_WIDTH), lambda r, j: (jnp.maximum(nblk * j - 1, 0), r))
    nxt = pl.BlockSpec((BAND, B_WIDTH), lambda r, j: (jnp.minimum(nblk * (j + 1), nb - 1), r))
    return cur, prev, nxt


def _pair_cols(hp):
    return slice(2 * HEAD_DIM * hp, 2 * HEAD_DIM * (hp + 1))


def _rows(b):
    return slice(b * BAND, (b + 1) * BAND)


def _with_prev(cur_ref, prev_ref, b, sl):
    if b == 0:
        return jnp.concatenate([prev_ref[:, sl], cur_ref[_rows(0), sl]], axis=0)
    return cur_ref[(b - 1) * BAND:(b + 1) * BAND, sl]


def _with_next(cur_ref, next_ref, b, sl, nblk):
    if b == nblk - 1:
        return jnp.concatenate([cur_ref[_rows(b), sl], next_ref[:, sl]], axis=0)
    return cur_ref[b * BAND:(b + 2) * BAND, sl]


def _band_valid(other_exists):
    row = lax.broadcasted_iota(jnp.int32, (BAND, 2 * BAND), 0)
    col = lax.broadcasted_iota(jnp.int32, (BAND, 2 * BAND), 1)
    return (col >= row) & (col <= row + BAND), other_exists


def _masked(lm, x):
    return jnp.where(lm, x, jnp.zeros_like(x))


def _attn_fwd(q, k, v, d, name):
    nb = q.shape[0] // BAND
    nblk = _blocks_per_step(nb)
    units = _units(nblk)
    cur, prev, _ = _attn_specs(nb, nblk)

    def body(q_ref, kc_ref, kp_ref, vc_ref, vp_ref, o_ref, l_ref):
        band, has_prev = _band_valid(pl.program_id(1) > 0)
        col = lax.broadcasted_iota(jnp.int32, (BAND, 2 * BAND), 1)
        valid = [band & ((col >= BAND) | has_prev)] + [band] * (nblk - 1)

        def scores(unit):
            b, hp, hh = unit
            sl = _pair_cols(hp)
            return _dotg(_masked(_pair_mask(hh), q_ref[_rows(b), sl]), _with_prev(kc_ref, kp_ref, b, sl), 1, 1)

        ahead, half = scores(units[0]), None
        for idx, (b, hp, hh) in enumerate(units):
            raw = ahead
            if idx + 1 < len(units):
                ahead = scores(units[idx + 1])
            sl, lm = _pair_cols(hp), _pair_mask(hh)
            s = jnp.where(valid[b], raw * SCALE, NEG_INF)
            m = jnp.max(s, axis=1, keepdims=True)
            p = jnp.exp(s - m)
            den = jnp.sum(p, axis=1, keepdims=True)
            o = _dotg(p, _with_prev(vc_ref, vp_ref, b, sl), 1, 0) / den
            lse = m + jnp.log(den)
            if hh == 0:
                half = (o, lse)
            else:
                o_ref[_rows(b), sl] = jnp.where(lm, o, half[0])
                l_ref[_rows(b), sl] = jnp.where(lm, lse, half[1])

    return pl.pallas_call(
        body, name=name, grid=(d, nb // nblk), in_specs=[cur, cur, prev, cur, prev], out_specs=[cur, cur],
        out_shape=[jax.ShapeDtypeStruct(q.shape, F32), jax.ShapeDtypeStruct(q.shape, F32)],
        compiler_params=_params("parallel", "parallel"),
    )(q, k, k, v, v)


def _attn_bwd_q(q, k, v, do, lse, delta, d, name):
    nb = q.shape[0] // BAND
    nblk = _blocks_per_step(nb)
    units = _units(nblk)
    cur, prev, _ = _attn_specs(nb, nblk)
    per_head = pl.BlockSpec((nblk * BAND, LANES), lambda r, j: (j, r))

    def body(q_ref, kc_ref, kp_ref, vc_ref, vp_ref, do_ref, l_ref, dl_ref, dq_ref):
        band, has_prev = _band_valid(pl.program_id(1) > 0)
        col = lax.broadcasted_iota(jnp.int32, (BAND, 2 * BAND), 1)
        valid = [band & ((col >= BAND) | has_prev)] + [band] * (nblk - 1)

        def products(unit):
            b, hp, hh = unit
            sl, lm = _pair_cols(hp), _pair_mask(hh)
            return (_dotg(_masked(lm, q_ref[_rows(b), sl]), _with_prev(kc_ref, kp_ref, b, sl), 1, 1),
                    _dotg(_masked(lm, do_ref[_rows(b), sl]), _with_prev(vc_ref, vp_ref, b, sl), 1, 1))

        ahead, half = products(units[0]), None
        for idx, (b, hp, hh) in enumerate(units):
            raw, dp = ahead
            if idx + 1 < len(units):
                ahead = products(units[idx + 1])
            sl, lm = _pair_cols(hp), _pair_mask(hh)
            s = jnp.where(valid[b], raw * SCALE, NEG_INF)
            head = lax.broadcasted_iota(jnp.int32, (1, LANES), 1) == 2 * hp + hh
            p = jnp.exp(s - _lane_pick(l_ref[_rows(b), :], head))
            ds = p * (dp - _lane_pick(dl_ref[_rows(b), :], head))
            dq = _dotg(ds, _with_prev(kc_ref, kp_ref, b, sl), 1, 0) * SCALE
            if hh == 0:
                half = dq
            else:
                dq_ref[_rows(b), sl] = jnp.where(lm, dq, half).astype(dq_ref.dtype)

    return pl.pallas_call(
        body, name=name, grid=(d, nb // nblk),
        in_specs=[cur, cur, prev, cur, prev, cur, per_head, per_head], out_specs=cur,
        out_shape=jax.ShapeDtypeStruct(q.shape, BF16),
        compiler_params=_params("parallel", "parallel"),
    )(q, k, k, v, v, do, lse, delta)


def _attn_bwd_kv(q, k, v, do, lse_t, delta_t, d, name):
    nb = q.shape[0] // BAND
    nblk = _blocks_per_step(nb)
    units = _units(nblk)
    cur, _, nxt = _attn_specs(nb, nblk)
    t_cur = pl.BlockSpec((1, N_HEADS, nblk * BAND), lambda r, j: (r, 0, j))
    t_nxt = pl.BlockSpec((1, N_HEADS, BAND), lambda r, j: (r, 0, jnp.minimum(nblk * (j + 1), nb - 1)))

    def body(k_ref, v_ref, qc_ref, qn_ref, doc_ref, don_ref, lc_ref, ln_ref, dlc_ref, dln_ref, dk_ref, dv_ref):
        band, has_next = _band_valid(pl.program_id(1) < nb // nblk - 1)
        col = lax.broadcasted_iota(jnp.int32, (BAND, 2 * BAND), 1)
        valid = [band] * (nblk - 1) + [band & ((col < BAND) | has_next)]

        def head_row(c_ref, n_ref, b, h):
            if b == nblk - 1:
                return jnp.concatenate([c_ref[0, h:h + 1, b * BAND:(b + 1) * BAND], n_ref[0, h:h + 1, :]], axis=1)
            return c_ref[0, h:h + 1, b * BAND:(b + 2) * BAND]

        def products(unit):
            b, hp, hh = unit
            sl, lm = _pair_cols(hp), _pair_mask(hh)
            return (_dotg(_masked(lm, k_ref[_rows(b), sl]), _with_next(qc_ref, qn_ref, b, sl, nblk), 1, 1),
                    _dotg(_masked(lm, v_ref[_rows(b), sl]), _with_next(doc_ref, don_ref, b, sl, nblk), 1, 1))

        ahead, half = products(units[0]), None
        for idx, (b, hp, hh) in enumerate(units):
            raw, dpt = ahead
            if idx + 1 < len(units):
                ahead = products(units[idx + 1])
            sl, lm, h = _pair_cols(hp), _pair_mask(hh), 2 * hp + hh
            st = jnp.where(valid[b], raw * SCALE, NEG_INF)
            pt = jnp.exp(st - head_row(lc_ref, ln_ref, b, h))
            dv = _dotg(pt, _with_next(doc_ref, don_ref, b, sl, nblk), 1, 0)
            dst = pt * (dpt - head_row(dlc_ref, dln_ref, b, h))
            dk = _dotg(dst, _with_next(qc_ref, qn_ref, b, sl, nblk), 1, 0) * SCALE
            if hh == 0:
                half = (dk, dv)
            else:
                dk_ref[_rows(b), sl] = jnp.where(lm, dk, half[0]).astype(dk_ref.dtype)
                dv_ref[_rows(b), sl] = jnp.where(lm, dv, half[1]).astype(dv_ref.dtype)

    return pl.pallas_call(
        body, name=name, grid=(d, nb // nblk),
        in_specs=[cur, cur, cur, nxt, cur, nxt, t_cur, t_nxt, t_cur, t_nxt], out_specs=[cur, cur],
        out_shape=[jax.ShapeDtypeStruct(q.shape, BF16), jax.ShapeDtypeStruct(q.shape, BF16)],
        compiler_params=_params("parallel", "parallel"),
    )(k, v, q, q, do, do, lse_t, lse_t, delta_t, delta_t)


def _spread_rows(a, d):
    return a.reshape(N_HEADS, a.shape[1] // d, d).transpose(2, 0, 1)


FF_TS = 512
FF_TC = 1024
FF_SUB = 256
HALO = 8
HALO_BF16 = 16
UP_BLOCKS = D_MODEL // FF_TC


def _conv3(ext, w, b):
    return b + w[0:1, :] * pltpu.roll(ext, 2, 0) + w[1:2, :] * pltpu.roll(ext, 1, 0) + w[2:3, :] * ext


def _ffn_specs(s, cols_first):
    nrb = s // FF_TS
    per, per16 = FF_TS // HALO, FF_TS // HALO_BF16

    def mk(block, fn):
        if cols_first:
            return pl.BlockSpec(block, lambda j, i: fn(i, j))
        return pl.BlockSpec(block, lambda i, j: fn(i, j))

    specs = types.SimpleNamespace(
        nrb=nrb, ncb=D_FF // FF_TC,
        row=mk((FF_TS, FF_TC), lambda i, j: (i, j)),
        before=mk((HALO, FF_TC), lambda i, j: (jnp.maximum(i * per - 1, 0), j)),
        after=mk((HALO, FF_TC), lambda i, j: (jnp.minimum((i + 1) * per, nrb * per - 1), j)),
        w=mk((3, FF_TC), lambda i, j: (0, j)),
        b=mk((1, FF_TC), lambda i, j: (0, j)),
        part=mk((HALO, FF_TC), lambda i, j: (i, j)),
        act=mk((FF_TS, D_MODEL), lambda i, j: (i, 0)),
        act_before=mk((HALO_BF16, D_MODEL), lambda i, j: (jnp.maximum(i * per16 - 1, 0), 0)),
        act_after=mk((HALO_BF16, D_MODEL), lambda i, j: (jnp.minimum((i + 1) * per16, nrb * per16 - 1), 0)),
        up_gate=mk((None, D_MODEL, FF_TC), lambda i, j: (j // UP_BLOCKS, 0, j % UP_BLOCKS)),
        up_val=mk((None, D_MODEL, FF_TC), lambda i, j: (N_DEV // 2 + j // UP_BLOCKS, 0, j % UP_BLOCKS)),
        down=mk((FF_TC, D_MODEL), lambda i, j: (j, 0)),
    )
    return specs


def _ffn_up_geglu(h, w_up, wg, wv, bg, bv, name):
    s = h.shape[0]
    sp = _ffn_specs(s, True)

    def body(h_ref, hb_ref, ugw_ref, uvw_ref, wg_ref, wv_ref, bg_ref, bv_ref, ug_ref, uv_ref, y_ref):
        keep = jnp.where(pl.program_id(1) > 0, 1.0, 0.0).astype(BF16)
        hext = jnp.concatenate([hb_ref[...] * keep, h_ref[...]], axis=0)
        eg = _dotg(hext, ugw_ref[...], 1, 0)
        ug_ref[...] = eg[HALO_BF16:, :]
        act = _gelu_tanh(_conv3(eg, wg_ref[...], bg_ref[...])[HALO_BF16:, :])
        ev = _dotg(hext, uvw_ref[...], 1, 0)
        uv_ref[...] = ev[HALO_BF16:, :]
        y_ref[...] = (act * _conv3(ev, wv_ref[...], bv_ref[...])[HALO_BF16:, :]).astype(y_ref.dtype)

    return pl.pallas_call(
        body, name=name, grid=(sp.ncb, sp.nrb),
        in_specs=[sp.act, sp.act_before, sp.up_gate, sp.up_val, sp.w, sp.w, sp.b, sp.b],
        out_specs=[sp.row, sp.row, sp.row],
        out_shape=[jax.ShapeDtypeStruct((s, D_FF), F32), jax.ShapeDtypeStruct((s, D_FF), F32),
                   jax.ShapeDtypeStruct((s, D_FF), BF16)],
        compiler_params=_params("parallel", "parallel"),
    )(h, h, w_up, w_up, wg, wv, bg, bv)


def _sum_parts(parts, name):
    n = parts.shape[0] // HALO

    def body(p_ref, o_ref):
        acc = p_ref[0:HALO, :]
        for t in range(1, n):
            acc = acc + p_ref[t * HALO:(t + 1) * HALO, :]
        o_ref[...] = acc

    return pl.pallas_call(body, name=name, out_shape=jax.ShapeDtypeStruct((HALO, parts.shape[1]), F32))(parts)


def _ffn_geglu_bwd(ug, uv, df, w_down, w_up, wg, wv, bg, bv, name):
    s = ug.shape[0]
    sp = _ffn_specs(s, False)
    nrb = sp.nrb
    rows = FF_TS + 2 * HALO
    lo, hi = HALO, HALO + FF_TS

    def body(ug_ref, uv_ref, hg_ref, hv_ref, ng_ref, nv_ref, df_ref, dfn_ref, dw_ref, ugw_ref, uvw_ref,
             wg_ref, wv_ref, bg_ref, bv_ref, dug_ref, duv_ref, dh_ref, dwg_ref, dwv_ref):
        i, j = pl.program_id(0), pl.program_id(1)
        keep_top = jnp.where(i > 0, 1.0, 0.0)
        keep_bot = jnp.where(i < nrb - 1, 1.0, 0.0).astype(BF16)
        dfe = jnp.concatenate([df_ref[...], dfn_ref[...] * keep_bot], axis=0)

        def back(dc, e, w, du_ref, sl):
            up1 = pltpu.roll(dc, rows - 1, 0)
            up2 = pltpu.roll(dc, rows - 2, 0)
            du = (w[2:3, :] * dc + w[1:2, :] * up1 + w[0:1, :] * up2)[lo:hi, :].astype(BF16)
            du_ref[:, sl] = du
            p1, p2 = up1 * e, up2 * e
            colsum = lambda p: jnp.sum(p[lo:hi, :], axis=0, keepdims=True)
            row = lambda p, t: p[t:t + 1, :]
            d_w1 = colsum(p1) + row(p1, lo - 1) - row(p1, hi - 1)
            d_w0 = colsum(p2) + row(p2, lo - 2) + row(p2, lo - 1) - row(p2, hi - 2) - row(p2, hi - 1)
            sums = [d_w0, d_w1, colsum(dc * e), colsum(dc), jnp.zeros((HALO - 4, FF_SUB), F32)]
            return du, jnp.concatenate(sums, axis=0)

        dh = None
        n_sub = FF_TC // FF_SUB
        cols = [slice(c * FF_SUB, (c + 1) * FF_SUB) for c in range(n_sub)]
        d_act = lambda c: _dotg(dfe, dw_ref[cols[c], :], 1, 1)[:FF_TS + HALO, :]
        ahead = d_act(0)
        for c in range(n_sub):
            sl, dy = cols[c], ahead
            if c + 1 < n_sub:
                ahead = d_act(c + 1)
            dye = jnp.concatenate([jnp.zeros((HALO, FF_SUB), F32), dy], axis=0)
            eg = jnp.concatenate([hg_ref[:, sl] * keep_top, ug_ref[:, sl], ng_ref[:, sl]], axis=0)
            ev = jnp.concatenate([hv_ref[:, sl] * keep_top, uv_ref[:, sl], nv_ref[:, sl]], axis=0)
            wg_, wv_ = wg_ref[:, sl], wv_ref[:, sl]
            gate = _conv3(eg, wg_, bg_ref[:, sl])
            val = _conv3(ev, wv_, bv_ref[:, sl])
            act, slope = _gelu_tanh_and_slope(gate)
            dug, dwg_ref[:, sl] = back((dye * val) * slope, eg, wg_, dug_ref, sl)
            duv, dwv_ref[:, sl] = back(dye * act, ev, wv_, duv_ref, sl)
            term = _dotg(dug, ugw_ref[:, sl], 1, 1) + _dotg(duv, uvw_ref[:, sl], 1, 1)
            dh = term if dh is None else dh + term

        @pl.when(j == 0)
        def _():
            dh_ref[...] = dh

        @pl.when(j > 0)
        def _():
            dh_ref[...] += dh

    parts = jax.ShapeDtypeStruct((nrb * HALO, D_FF), F32)
    dug, duv, dh, pg, pv = pl.pallas_call(
        body, name=name, grid=(nrb, sp.ncb),
        in_specs=[sp.row, sp.row, sp.before, sp.before, sp.after, sp.after, sp.act, sp.act_after, sp.down,
                  sp.up_gate, sp.up_val, sp.w, sp.w, sp.b, sp.b],
        out_specs=[sp.row, sp.row, sp.act, sp.part, sp.part],
        out_shape=[jax.ShapeDtypeStruct((s, D_FF), BF16), jax.ShapeDtypeStruct((s, D_FF), BF16),
                   jax.ShapeDtypeStruct((s, D_MODEL), F32), parts, parts],
        compiler_params=_params("parallel", "arbitrary"),
    )(ug, uv, ug, uv, ug, uv, df, df, w_down, w_up, w_up, wg, wv, bg, bv)
    return dug, duv, dh, _sum_parts(pg, name=name + "_sum_gate"), _sum_parts(pv, name=name + "_sum_val")


def _rope_tables(s):
    inv = ROPE_THETA ** (-jnp.arange(0, ROT_DIM, 2, dtype=F32) / ROT_DIM)
    ang = jnp.arange(s, dtype=F32)[:, None] * inv[None, :]
    cos8, sin8 = jnp.cos(ang), jnp.sin(ang)
    rest = HEAD_DIM - ROT_DIM
    cos_h = jnp.concatenate([cos8, cos8, jnp.ones((s, rest), F32)], axis=1)
    sin_h = jnp.concatenate([sin8, sin8, jnp.zeros((s, rest), F32)], axis=1)
    return jnp.tile(cos_h, (1, LANES // HEAD_DIM)), jnp.tile(sin_h, (1, LANES // HEAD_DIM))


def _all_heads(table):
    return jnp.concatenate([table] * (B_WIDTH // LANES), axis=1)


def _layer_fwd(x, w, cos, sin, last):
    sv = types.SimpleNamespace(x=x)
    (sv.h1,), _ = _rowwise(lambda xb, g: ((_rms(xb, g),), ()), [_full(x)], [w.g_pre], [(D_MODEL, BF16)], [],
                           ts=512, name="pre_mix_norm")
    sv.proj = _matmul(sv.h1, w.big("w_in", sv.h1), mode="nt", out_dtype=F32, name="proj")

    gate_consts = [w.vg, w.vb, *w.ws, w.bfull, w.ga]
    def gate_fwd_fn(zu, zv, *consts):
        chunks = [_gate_fn(zu[r], zv[r], *consts) for r in _chunk_rows(zu.shape[0])]
        return (jnp.concatenate(chunks, axis=0),), ()

    (na,), _ = _rowwise(gate_fwd_fn, [(sv.proj, A_WIDTH, 0), (sv.proj, A_WIDTH, 1)], gate_consts,
                        [(A_WIDTH, BF16)], [], ts=GATE_TS, name="gate_fwd")

    def rope_fn(qr, kr, vr, cs, sn):
        cs, sn = _all_heads(cs), _all_heads(sn)
        return (qr * cs + _rot_half(qr) * sn, kr * cs + _rot_half(kr) * sn, vr), ()

    def rope_all(qr, kr, vr, cs, sn):
        return rope_fn(qr, kr, vr, cs, sn)[0] * len(DILATIONS), ()

    qkv, _ = _rowwise(
        rope_all, [(sv.proj, B_WIDTH, 2), (sv.proj, B_WIDTH, 3), (sv.proj, B_WIDTH, 4), _full(cos), _full(sin)], [],
        [(B_WIDTH, BF16, d) for d in DILATIONS for _ in range(3)], [], ts=512, name="rope_fwd")
    sv.qkv = {d: qkv[3 * i:3 * i + 3] for i, d in enumerate(DILATIONS)}

    branch = []
    for d in DILATIONS:
        o, l = _attn_fwd(*sv.qkv[d], d, name=f"attn_fwd_d{d}")
        branch += [_full(o, d), _full(l, d)]

    def combine_fn(o1, l1, o2, l2, o3, l3, nab, gb):
        m = jnp.maximum(jnp.maximum(l1, l2), l3)
        e1, e2, e3 = jnp.exp(l1 - m), jnp.exp(l2 - m), jnp.exp(l3 - m)
        den = e1 + e2 + e3
        ob = (e1 / den) * o1 + (e2 / den) * o2 + (e3 / den) * o3
        mixed = jnp.concatenate([nab, _rms(ob, gb).astype(BF16)], axis=1)
        lse = _head_cols(m + jnp.log(den))
        return (mixed, ob, lse) + (lse,) * len(DILATIONS), ()

    (sv.mixed, sv.ob, sv.lse_rows, *lses), _ = _rowwise(
        combine_fn, branch + [_full(na)], [w.gb],
        [(D_MODEL, BF16), (B_WIDTH, F32), (LANES, F32, HEAD_ROWS)] + [(LANES, F32, d) for d in DILATIONS], [],
        ts=512, name="combine")
    sv.lse = dict(zip(DILATIONS, lses))
    sv.y = _matmul(sv.mixed, w.big("w_out", sv.mixed), mode="nn", out_dtype=F32, name="mix_out")

    def mid_fn(xb, yb, g1, g2):
        x1 = xb + _rms(yb, g1)
        return (x1, _rms(x1, g2)), ()

    (sv.x1, sv.h2), _ = _rowwise(mid_fn, [_full(x), _full(sv.y)], [w.g_pm, w.g_pf], [(D_MODEL, F32), (D_MODEL, BF16)], [],
                                 ts=512, name="post_mix_norm")
    conv_w = w.big("conv_w", sv.h2)
    sv.ug, sv.uv, sv.yff = _ffn_up_geglu(sv.h2, w.big("w_up", sv.h2), conv_w[:, :D_FF], conv_w[:, D_FF:],
                                         w.cb_g, w.cb_v, name="ffn_up_geglu")
    sv.f = _matmul(sv.yff, w.big("w_down", sv.yff), mode="nn", out_dtype=F32, name="ffn_down")
    if last:
        return None, sv
    (x2,), _ = _rowwise(lambda xb, fb, g: ((xb + _rms(fb, g),), ()), [_full(sv.x1), _full(sv.f)], [w.g_post],
                        [(D_MODEL, F32)], [], ts=512, name="post_ffn_norm")
    return x2, sv


def _layer_bwd(dx2, sv, w, cos, sin, emit):
    g = {}

    def post_fn(fb, dxb, gp):
        _, vjp = jax.vjp(_rms, fb, gp)
        df, dg = vjp(dxb)
        return (df,), (dg,)

    (df,), (g["post_ffn_norm"],) = _rowwise(post_fn, [_full(sv.f), _full(dx2)], [w.g_post], [(D_MODEL, BF16)],
                                            [(1, D_MODEL)], ts=512, name="post_ffn_norm_bwd")
    big = {"w_down": _matmul(sv.yff, df, mode="tn", out_dtype=BF16, name="ffn_down_dw").reshape(N_DEV, -1, D_MODEL)}
    conv_w = w.big("conv_w", df)
    dug, duv, dh2, dwg, dwv = _ffn_geglu_bwd(sv.ug, sv.uv, df, w.big("w_down", df), w.big("w_up", df),
                                             conv_w[:, :D_FF], conv_w[:, D_FF:], w.cb_g, w.cb_v, name="ffn_geglu_bwd")
    big["conv_w"] = jnp.concatenate([dwg[0:3], dwv[0:3]], axis=1).reshape(3, N_DEV, D_MODEL).transpose(1, 0, 2)
    g["conv_b"] = jnp.concatenate([dwg[3], dwv[3]], axis=0)
    big["w_up"] = _matmul_by_destination(sv.h2, dug, duv, name="ffn_up_dw")
    g_pm = w.g_pm + emit(big)

    def mid_fn(x1b, yb, dhb, dxb, g1, g2):
        _, vjp2 = jax.vjp(_rms, x1b, g2)
        dx1h, dg2 = vjp2(dhb)
        dx1 = dxb + dx1h
        _, vjp1 = jax.vjp(_rms, yb, g1)
        dy, dg1 = vjp1(dx1)
        return (dx1, dy), (dg1, dg2)

    (dx1, dy), (g["post_mix_norm"], g["pre_ffn_norm"]) = _rowwise(
        mid_fn, [_full(sv.x1), _full(sv.y), _full(dh2), _full(dx2)], [g_pm, w.g_pf],
        [(D_MODEL, F32), (D_MODEL, BF16)], [(1, D_MODEL), (1, D_MODEL)], ts=512, name="post_mix_norm_bwd")
    dmixed = _matmul(dy, w.big("w_out", dy), mode="nt", out_dtype=F32, name="mix_out_dx")
    dw_out = _matmul(sv.mixed, dy, mode="tn", out_dtype=BF16, name="mix_out_dw").reshape(N_DEV, -1, D_MODEL)
    g_b = w.gb + emit({"w_out": dw_out})

    def attn_out_fn(obb, dmb, gb):
        _, vjp = jax.vjp(_rms, obb, gb)
        do, dgb = vjp(dmb)
        delta = _head_cols(do * obb, whole_head=True)
        return (delta,) + (do,) * len(DILATIONS) + (delta,) * len(DILATIONS), (dgb,)

    (delta_rows, *outs), (g["out_norm_b"],) = _rowwise(
        attn_out_fn, [_full(sv.ob), (dmixed, B_WIDTH, 1)], [g_b],
        [(LANES, F32, HEAD_ROWS)] + [(B_WIDTH, BF16, d) for d in DILATIONS] + [(LANES, F32, d) for d in DILATIONS],
        [(1, B_WIDTH)], ts=512, name="attn_out_bwd")
    do = dict(zip(DILATIONS, outs[:len(DILATIONS)]))
    delta = dict(zip(DILATIONS, outs[len(DILATIONS):]))
    parts = {"q": [], "k": [], "v": []}
    for d in DILATIONS:
        qv, kv, vv = sv.qkv[d]
        dq = _attn_bwd_q(qv, kv, vv, do[d], sv.lse[d], delta[d], d, name=f"attn_bwd_q_d{d}")
        dk, dv = _attn_bwd_kv(qv, kv, vv, do[d], _spread_rows(sv.lse_rows, d), _spread_rows(delta_rows, d), d,
                              name=f"attn_bwd_kv_d{d}")
        parts["q"].append(_full(dq, d))
        parts["k"].append(_full(dk, d))
        parts["v"].append(_full(dv, d))

    def rope_bwd_fn(q1, q2, q3, k1, k2, k3, v1, v2, v3, cs, sn):
        cs, sn = _all_heads(cs), _all_heads(sn)

        def back(t):
            return t * cs - _rot_half(t * sn)
        return (jnp.concatenate([back(q1 + q2 + q3), back(k1 + k2 + k3), v1 + v2 + v3], axis=1),), ()

    (dzb,), _ = _rowwise(rope_bwd_fn, parts["q"] + parts["k"] + parts["v"] + [_full(cos), _full(sin)], [],
                         [(3 * B_WIDTH, BF16)], [], ts=512, name="rope_bwd")

    gate_consts = [w.vg, w.vb, *w.ws, w.bfull, w.ga]

    def gate_bwd_fn(zu, zv, dna, dzb_rows, *consts):
        dz, sums = [], None
        for r in _chunk_rows(zu.shape[0]):
            _, vjp = jax.vjp(_gate_fn, zu[r], zv[r], *consts)
            grads = vjp(dna[r])
            dz.append(jnp.concatenate([grads[0].astype(BF16), grads[1].astype(BF16), dzb_rows[r]], axis=1))
            sums = grads[2:] if sums is None else tuple(a + b for a, b in zip(sums, grads[2:]))
        return (jnp.concatenate(dz, axis=0),), tuple(sums)

    (dproj,), gsmall = _rowwise(
        gate_bwd_fn, [(sv.proj, A_WIDTH, 0), (sv.proj, A_WIDTH, 1), (dmixed, A_WIDTH, 0), _full(dzb)], gate_consts,
        [(IN_COLS, BF16)], [c.shape for c in gate_consts], ts=GATE_TS, name="gate_bwd")
    g["v_norm_g"], g["v_norm_b"] = gsmall[0], gsmall[1]
    g["w_spatial"] = jnp.stack(gsmall[2:6])
    g["b_spatial"] = _bias_reduce(gsmall[6], name="bias_reduce")[:, :A_GROUPS].T
    g["out_norm_a"] = gsmall[7]

    dh1 = _matmul(dproj, w.big("w_in", dproj), mode="nn", out_dtype=F32, name="proj_dx")
    dw_in = _matmul(dproj, sv.h1, mode="tn", out_dtype=BF16, name="proj_dw").reshape(N_DEV, -1, D_MODEL)
    g_pre = w.g_pre + emit({"w_in": dw_in})

    def pre_fn(xb, dhb, dxb, gp):
        _, vjp = jax.vjp(_rms, xb, gp)
        dxh, dg = vjp(dhb)
        return (dxb + dxh,), (dg,)

    (dx,), (g["pre_mix_norm"],) = _rowwise(pre_fn, [_full(sv.x), _full(dh1), _full(dx1)], [g_pre], [(D_MODEL, F32)],
                                           [(1, D_MODEL)], ts=512, name="pre_mix_norm_bwd")
    return dx, g


def _layer_weights(l, full, small):
    row = lambda a: a[l].reshape(1, -1)
    return types.SimpleNamespace(
        big=functools.partial(full, l),
        g_pre=row(small["pre_mix_norm"]), vg=row(small["v_norm_g"]), vb=row(small["v_norm_b"]),
        ws=[small["w_spatial"][l, gi] for gi in range(A_GROUPS)],
        bfull=jnp.repeat(small["b_spatial"][l].T, CHUNK, axis=1),
        ga=row(small["out_norm_a"]), gb=row(small["out_norm_b"]),
        g_pm=row(small["post_mix_norm"]), g_pf=row(small["pre_ffn_norm"]),
        cb_g=small["conv_b"][l][:D_FF].reshape(1, -1), cb_v=small["conv_b"][l][D_FF:].reshape(1, -1),
        g_post=row(small["post_ffn_norm"]))


def _local_step(x, target, full, small, emit, emit_small, started):
    s = x.shape[0]
    cos, sin = _rope_tables(s)
    ws = [_layer_weights(l, full, small) for l in range(N_LAYERS)]
    ws[0].g_pre = ws[0].g_pre + started
    saved = []
    h = x
    for l in range(N_LAYERS):
        h, sv = _layer_fwd(h, ws[l], cos, sin, last=l == N_LAYERS - 1)
        saved.append(sv)

    def loss_fn(xb, fb, tb, g):
        diff = (xb + _rms(fb, g)) - tb
        return (diff * (1.0 / D_MODEL),), (jnp.sum(diff * diff, axis=0, keepdims=True),)

    (dh,), (sq,) = _rowwise(loss_fn, [_full(saved[-1].x1), _full(saved[-1].f), _full(target)], [ws[-1].g_post],
                            [(D_MODEL, F32)], [(1, D_MODEL)], ts=512, name="loss")
    loss = 0.5 * jnp.sum(sq) * (1.0 / D_MODEL)
    grads = [None] * N_LAYERS
    for l in reversed(range(N_LAYERS)):
        dh, grads[l] = _layer_bwd(dh, saved[l], ws[l], cos, sin, functools.partial(emit, l))
        token = emit_small(l, grads[l], loss)
        if l > 0:
            ws[l - 1].g_post = ws[l - 1].g_post + token
    return loss, dh, grads


def _place():
    return lax.axis_index("x"), lax.axis_index("y"), lax.axis_index("c")


FLIPS = ((1, 0, 0), (0, 1, 0), (1, 1, 0), (0, 0, 1), (1, 0, 1), (0, 1, 1), (1, 1, 1))
HBM_SPEC = pl.BlockSpec(memory_space=pltpu.HBM)
SEM_SPEC = pl.BlockSpec(memory_space=pltpu.SEMAPHORE)
SPLIT_COPY = pltpu.CompilerParams(has_side_effects=pltpu.SideEffectType.DATAFLOW_SIDE_EFFECTING)


def _peers():
    mx, my, mc = _place()
    out = []
    for fx, fy, fc in FLIPS:
        px, py, pc = (1 - mx if fx else mx), (1 - my if fy else my), (1 - mc if fc else mc)
        out.append(((px, py, pc), 4 * px + 2 * py + pc))
    return out


def _flat_copies(scatter, src_refs, land_refs, send_sems, recv_sems):
    mx, my, mc = _place()
    me = 4 * mx + 2 * my + mc
    n = len(src_refs)
    copies = []
    for t in range(n):
        for i, (peer, number) in enumerate(_peers()):
            copies.append(pltpu.make_async_remote_copy(
                src_ref=src_refs[t].at[number] if scatter else src_refs[t],
                dst_ref=land_refs[t].at[i] if scatter else land_refs[t].at[me],
                send_sem=send_sems.at[t * len(FLIPS) + i], recv_sem=recv_sems.at[t * len(FLIPS) + i],
                device_id=peer, device_id_type=MESH_ID))
    return copies


def _flat_start(arrays, scatter, name):
    n = len(arrays)
    slots = len(FLIPS) if scatter else N_DEV
    lands = [lax.empty((slots,) + (a.shape[1:] if scatter else a.shape), a.dtype) for a in arrays]

    def body(*refs):
        src, land, (send_sems, recv_sems), token = refs[:n], refs[n:2 * n], refs[2 * n:2 * n + 2], refs[-1]
        for cp in _flat_copies(scatter, src, land, send_sems, recv_sems):
            cp.start()
        token[...] = jnp.zeros_like(token)

    hbm = [pltpu.HBM(a.shape, a.dtype) for a in arrays] + [pltpu.HBM(a.shape, a.dtype) for a in lands]
    sems = pltpu.SemaphoreType.DMA((n * len(FLIPS),))
    outs = pl.pallas_call(
        body, name=name, out_shape=(sems, sems, *hbm, jax.ShapeDtypeStruct((8, 128), F32)),
        in_specs=[HBM_SPEC] * (2 * n),
        out_specs=(SEM_SPEC, SEM_SPEC, *([HBM_SPEC] * (2 * n)), pl.BlockSpec(memory_space=pltpu.VMEM)),
        input_output_aliases={t: 2 + t for t in range(2 * n)}, compiler_params=SPLIT_COPY,
    )(*[pltpu.with_memory_space_constraint(a, pltpu.HBM) for a in (*arrays, *lands)])
    return types.SimpleNamespace(sems=outs[:2], thru=outs[2:2 + 2 * n], scatter=scatter, n=n), outs[-1][0:1, 0:1]


def _flat_wait(handle, after, name):
    n = handle.n

    def body(*refs):
        src, land, (send_sems, recv_sems) = refs[:n], refs[n:2 * n], refs[2 * n:2 * n + 2]
        for cp in _flat_copies(handle.scatter, src, land, send_sems, recv_sems):
            cp.wait_send()
            cp.wait_recv()

    outs = pl.pallas_call(
        body, name=name, out_shape=tuple(pltpu.HBM(a.shape, a.dtype) for a in handle.thru),
        in_specs=[HBM_SPEC] * (2 * n) + [SEM_SPEC, SEM_SPEC, ANY], out_specs=tuple([HBM_SPEC] * (2 * n)),
        input_output_aliases={t: t for t in range(2 * n)}, compiler_params=SPLIT_COPY,
    )(*handle.thru, *handle.sems, after)
    return outs[:n], outs[n:]


def _adamw(w, g, m, v):
    m2 = ADAM_B1 * m + (1.0 - ADAM_B1) * g
    v2 = ADAM_B2 * v + (1.0 - ADAM_B2) * (g * g)
    m_hat = m2 / (1.0 - ADAM_B1 ** ADAM_STEP)
    v_hat = v2 / (1.0 - ADAM_B2 ** ADAM_STEP)
    return -ADAM_LR * (m_hat / (jnp.sqrt(v_hat) + ADAM_EPS) + ADAM_WD * w), m2, v2


def _adamw_sharded(me, mine, landed, w, m, v, tr, name):
    _, r, c = w.shape
    nt = r // tr
    assert r % tr == 0 and len(mine) == len(landed) == N_LAYERS == 2, name
    per_layer = 1 + len(FLIPS)

    def body(me_ref, *refs):
        terms, (w_ref, m_ref, v_ref), outs = refs[:2 * per_layer], refs[2 * per_layer:2 * per_layer + 3], refs[-4:]
        layer = pl.program_id(0)

        def total(group):
            g = group[0][0].astype(F32)
            for t in group[1:]:
                g = g + t[0].astype(F32)
            return g

        g = jnp.where(layer == 0, total(terms[:per_layer]), total(terms[per_layer:]))
        d, m2, v2 = _adamw(w_ref[0], g, m_ref[0], v_ref[0])
        for o, val in zip(outs, (g, d, m2, v2)):
            o[0] = val

    def held(l):
        return lambda layer, i: jnp.where(layer == l, i, nt - 1 if l == 0 else 0)

    in_specs = []
    for l in range(N_LAYERS):
        rows = held(l)
        in_specs.append(pl.BlockSpec((1, tr, c), functools.partial(lambda layer, i, me_ref, rows: (me_ref[0], rows(layer, i), 0), rows=rows)))
        for k in range(len(FLIPS)):
            in_specs.append(pl.BlockSpec(
                (1, tr, c), functools.partial(lambda layer, i, me_ref, rows, k: (k, rows(layer, i), 0), rows=rows, k=k)))
    tile = pl.BlockSpec((1, tr, c), lambda layer, i, me_ref: (layer, i, 0))
    operands = []
    for l in range(N_LAYERS):
        operands += [mine[l]] + [landed[l]] * len(FLIPS)
    return pl.pallas_call(
        body, name=name, out_shape=[jax.ShapeDtypeStruct(w.shape, F32)] * 4,
        grid_spec=pltpu.PrefetchScalarGridSpec(
            num_scalar_prefetch=1, grid=(N_LAYERS, nt), in_specs=in_specs + [tile] * 3, out_specs=[tile] * 4),
        compiler_params=_params("arbitrary", "arbitrary"),
    )(me, *operands, w, m, v)


def _adamw_replicated(parts, w, m, v, name):
    n_layers = len(parts)

    def body(*refs):
        p_refs, (w_ref, m_ref, v_ref), outs = refs[:n_layers], refs[n_layers:n_layers + 3], refs[n_layers + 3:]
        for l, p_ref in enumerate(p_refs):
            g = p_ref[0]
            for j in range(1, N_DEV):
                g = g + p_ref[j]
            d, m2, v2 = _adamw(w_ref[l], g, m_ref[l], v_ref[l])
            for o, val in zip(outs, (g, d, m2, v2)):
                o[l] = val

    return pl.pallas_call(body, name=name, out_shape=[jax.ShapeDtypeStruct(w.shape, F32)] * 4,
                          compiler_params=pltpu.CompilerParams(vmem_limit_bytes=VMEM_LIMIT_BYTES))(*parts, w, m, v)


SMALL_USED = 80384


def _pack_small(vals, rider=None):
    pieces = [vals[n].reshape(-1) for n in SMALL_NAMES] + ([] if rider is None else [rider.reshape(1)])
    flat = jnp.concatenate(pieces)
    assert flat.shape[0] == SMALL_USED + (rider is not None)
    return jnp.concatenate([flat, jnp.zeros((SMALL_ROWS * D_MODEL - flat.shape[0],), F32)]).reshape(SMALL_ROWS, D_MODEL)


def _pack_small_layers(vals):
    flat = jnp.concatenate([vals[n].reshape(N_LAYERS, -1) for n in SMALL_NAMES], axis=1)
    return jnp.pad(flat, ((0, 0), (0, SMALL_ROWS * D_MODEL - SMALL_USED))).reshape(N_LAYERS, SMALL_ROWS, D_MODEL)


def _unpack_small_layers(packed, shapes):
    flat, out, at = packed.reshape(N_LAYERS, -1), {}, 0
    for n in SMALL_NAMES:
        size = math.prod(shapes[n][1:])
        out[n] = flat[:, at:at + size].reshape(shapes[n])
        at += size
    return out


GATHER_GROUPS = ((0, ("w_in",)), (0, ("w_out", "w_up", "conv_w")), (0, ("w_down",)),
                 (1, ("w_in",)), (1, ("w_out", "w_up", "conv_w")), (1, ("w_down",)))
ADAMW_TILE_ROWS = {"w_in": 320, "w_out": 128, "w_up": 256, "w_down": 256, "conv_w": 3}


def _assemble(name, land):
    if name == "w_in":
        return land.reshape(IN_COLS, D_MODEL)
    if name == "conv_w":
        return land.transpose(1, 0, 2).reshape(3, 2 * D_FF)
    if name == "w_up":
        return land
    return land.reshape(-1, D_MODEL)


def _start_gathers(wts, me):
    started, groups = jnp.zeros((1, 1), F32), []
    for gi, (l, names) in enumerate(GATHER_GROUPS):
        local = {"conv_w": lambda a: a, "w_in": lambda a: a.T.astype(BF16)}
        blocks = [local.get(n, lambda a: a.astype(BF16))(wts[n][l]) for n in names]
        if gi > 0:
            blocks[0] = blocks[0] + token.astype(blocks[0].dtype)
        handle, token = _flat_start(blocks, False, name=f"gather_start_{gi}")
        groups.append(types.SimpleNamespace(layer=l, names=names, blocks=blocks, handle=handle, got=None, index=gi))
        started = started + token

    def fetch(l, name, after):
        grp = next(gr for gr in groups if gr.layer == l and name in gr.names)
        if grp.got is None:
            lands = _flat_wait(grp.handle, after, name=f"gather_wait_{grp.index}")[1]
            grp.got = {}
            for n, blk, land in zip(grp.names, grp.blocks, lands):
                own = (me,) + (0,) * blk.ndim
                grp.got[n] = _assemble(n, lax.dynamic_update_slice(land, blk[None], own))
        return grp.got[name]

    return fetch, started


def kernel(x, pre_mix_norm, w_in, v_norm_g, v_norm_b, w_spatial, b_spatial, out_norm_a, out_norm_b, w_out, post_mix_norm, pre_ffn_norm, w_up, conv_w, conv_b, w_down, post_ffn_norm, loss_target, m_pre_mix_norm, m_w_in, m_v_norm_g, m_v_norm_b, m_w_spatial, m_b_spatial, m_out_norm_a, m_out_norm_b, m_w_out, m_post_mix_norm, m_pre_ffn_norm, m_w_up, m_conv_w, m_conv_b, m_w_down, m_post_ffn_norm, v_pre_mix_norm, v_w_in, v_v_norm_g, v_v_norm_b, v_w_spatial, v_b_spatial, v_out_norm_a, v_out_norm_b, v_w_out, v_post_mix_norm, v_pre_ffn_norm, v_w_up, v_conv_w, v_conv_b, v_w_down, v_post_ffn_norm):
    wts = dict(zip(WEIGHT_NAMES, (pre_mix_norm, w_in, v_norm_g, v_norm_b, w_spatial, b_spatial, out_norm_a, out_norm_b,
                                  w_out, post_mix_norm, pre_ffn_norm, w_up, conv_w, conv_b, w_down, post_ffn_norm)))
    mom1 = dict(zip(WEIGHT_NAMES, (m_pre_mix_norm, m_w_in, m_v_norm_g, m_v_norm_b, m_w_spatial, m_b_spatial, m_out_norm_a,
                                   m_out_norm_b, m_w_out, m_post_mix_norm, m_pre_ffn_norm, m_w_up, m_conv_w, m_conv_b,
                                   m_w_down, m_post_ffn_norm)))
    mom2 = dict(zip(WEIGHT_NAMES, (v_pre_mix_norm, v_w_in, v_v_norm_g, v_v_norm_b, v_w_spatial, v_b_spatial, v_out_norm_a,
                                   v_out_norm_b, v_w_out, v_post_mix_norm, v_pre_ffn_norm, v_w_up, v_conv_w, v_conv_b,
                                   v_w_down, v_post_ffn_norm)))
    mx, my, mc = _place()
    me = 4 * mx + 2 * my + mc

    fetch, started = _start_gathers(wts, me)
    scatters = []

    def emit(l, blocks):
        names = tuple(blocks)
        handle, token = _flat_start([blocks[n] for n in names], True, name=f"scatter_start_{l}_{'_'.join(names)}")
        scatters.append((l, names, handle))
        return token

    smalls = {}

    def emit_small(l, g, loss_local):
        pack = _pack_small({n: g[n] for n in SMALL_NAMES}, rider=loss_local if l == 0 else None)
        handle, token = _flat_start([pack], False, name=f"small_grads_start_{l}")
        smalls[l] = (pack, handle)
        return token

    _, dx, _ = _local_step(x[0], loss_target[0], fetch, wts, emit, emit_small, started)

    me_arr = jnp.reshape(me, (1,)).astype(jnp.int32)
    big_out = [{}, {}, {}, {}]

    def finish(group, after):
        mine, landed = {}, {}
        for l, names, handle in scatters:
            if names == group:
                sent, lands = _flat_wait(handle, after, name=f"scatter_wait_{l}_{'_'.join(names)}")
                for n, a, b in zip(names, sent, lands):
                    mine[l, n], landed[l, n] = a, b
        for n in group:
            flip = (lambda a: a.transpose(0, 2, 1)) if n == "w_in" else (lambda a: a)
            res = _adamw_sharded(me_arr, [mine[l, n] for l in range(N_LAYERS)], [landed[l, n] for l in range(N_LAYERS)],
                                 flip(wts[n]), flip(mom1[n]), flip(mom2[n]), ADAMW_TILE_ROWS[n], name=f"adamw_{n}")
            for kind in range(4):
                big_out[kind][n] = flip(res[kind])
        return res[0]

    done = smalls[0][1].thru[0]
    for group in [names for l, names, _ in scatters if l == 0][:-1]:
        done = finish(group, done)
    everyone = [None] * N_LAYERS
    for l in reversed(range(N_LAYERS)):
        pack, handle = smalls[l]
        (landed,) = _flat_wait(handle, done, name=f"small_grads_wait_{l}")[1]
        everyone[l] = lax.dynamic_update_slice(landed, pack[None], (me, 0, 0))
    packs = [_pack_small_layers({n: t[n] for n in SMALL_NAMES}) for t in (wts, mom1, mom2)]
    res = _adamw_replicated(everyone, *packs, name="adamw_replicated")
    loss = res[0][0].reshape(-1)[SMALL_USED]
    finish(scatters[-1][1], res[0])
    small_shapes = {n: wts[n].shape for n in SMALL_NAMES}
    small_out = [_unpack_small_layers(o, small_shapes) for o in res]

    outs = [loss, dx[None]]
    for kind in range(4):
        outs += [big_out[kind][n] if n in BIG_NAMES else small_out[kind][n] for n in WEIGHT_NAMES]
    return tuple(outs)
```

```python
import functools
import math
import types

import jax
import jax.numpy as jnp
from jax import lax
from jax.experimental import pallas as pl
from jax.experimental.pallas import tpu as pltpu

F32 = jnp.float32
BF16 = jnp.bfloat16

D_MODEL = 1024
A_WIDTH = 512
A_GROUPS = 4
CHUNK = 128
B_WIDTH = 512
HEAD_DIM = 64
N_HEADS = B_WIDTH // HEAD_DIM
ROT_DIM = 16
ROPE_THETA = 500000.0
BAND = 128
DILATIONS = (1, 4, 16)
IN_COLS = 2560
D_FF = 4096
EPS = 1e-6
NEG_INF = -1e30
N_DEV = 8
N_LAYERS = 2

ADAM_LR = 0.001
ADAM_B1 = 0.9
ADAM_B2 = 0.999
ADAM_EPS = 1e-08
ADAM_WD = 0.01
ADAM_STEP = 10

VMEM_LIMIT_BYTES = 56 * 1024 * 1024
MESH_ID = pl.DeviceIdType.MESH
ANY = pl.BlockSpec(memory_space=pl.ANY)

WEIGHT_NAMES = ("pre_mix_norm", "w_in", "v_norm_g", "v_norm_b", "w_spatial", "b_spatial", "out_norm_a", "out_norm_b",
                "w_out", "post_mix_norm", "pre_ffn_norm", "w_up", "conv_w", "conv_b", "w_down", "post_ffn_norm")
BIG_NAMES = ("w_in", "w_out", "w_up", "w_down", "conv_w")
SMALL_NAMES = tuple(n for n in WEIGHT_NAMES if n not in BIG_NAMES)

SMALL_ROWS = 80


def _params(*sem):
    return pltpu.CompilerParams(dimension_semantics=sem, vmem_limit_bytes=VMEM_LIMIT_BYTES)


def _dotg(a, b, ca, cb):
    return lax.dot_general(a.astype(BF16), b.astype(BF16), (((ca,), (cb,)), ((), ())), preferred_element_type=F32)


@jax.custom_vjp
def _bdot(a, b):
    return _dotg(a, b, 1, 0)


def _bdot_fwd(a, b):
    return _dotg(a, b, 1, 0), (a, b)


def _bdot_bwd(res, g):
    a, b = res
    return _dotg(g, b, 1, 1), _dotg(a, g, 0, 0)


_bdot.defvjp(_bdot_fwd, _bdot_bwd)


def _rms(x, g):
    return x * lax.rsqrt(jnp.mean(x * x, axis=-1, keepdims=True) + EPS) * g


def _layernorm(x, g, b):
    mu = jnp.mean(x, axis=-1, keepdims=True)
    xc = x - mu
    return xc * lax.rsqrt(jnp.mean(xc * xc, axis=-1, keepdims=True) + EPS) * g + b


def _gelu_erf(x):
    return x * (lax.erf(x * (1.0 / math.sqrt(2.0))) + 1.0) * 0.5


def _gelu_tanh(x):
    c = math.sqrt(2.0 / math.pi)
    return 0.5 * x * (1.0 + jnp.tanh(c * (x + 0.044715 * (x * x * x))))


def _gelu_tanh_and_slope(x):
    c, k = math.sqrt(2.0 / math.pi), 0.044715
    x2 = x * x
    t = jnp.tanh(c * (x + k * (x2 * x)))
    half_x, one_t = 0.5 * x, 1.0 + t
    return half_x * one_t, 0.5 * one_t + (half_x * (1.0 - t * t)) * (c + (3.0 * k * c) * x2)


def _rot_half(x):
    width = x.shape[1]
    lane = lax.broadcasted_iota(jnp.int32, x.shape, 1) % HEAD_DIM
    back = pltpu.roll(x, ROT_DIM // 2, 1)
    fwd = pltpu.roll(x, width - ROT_DIM // 2, 1)
    return jnp.where(lane < ROT_DIM // 2, -fwd, jnp.where(lane < ROT_DIM, back, 0.0))


def _split3(z):
    h0 = z.astype(BF16)
    r1 = z - h0.astype(F32)
    h1 = r1.astype(BF16)
    h2 = (r1 - h1.astype(F32)).astype(BF16)
    return h0, h1, h2


MATMUL_VMEM_BUDGET = 40 * 1024 * 1024


def _matmul_tiles(m, n, k, out_bytes):
    tn = n if n <= 1024 else (1280 if n % 1280 == 0 and n % 1024 else 1024)
    tk = k if k <= 1024 else (1280 if k % 1280 == 0 and k % 1024 else 1024)
    tm = m
    while tm > 256:
        blocks = 2 * 2 * (tm * tk + tk * tn) + 2 * out_bytes * tm * tn + (4 * tm * tn if k > tk else 0)
        if blocks <= MATMUL_VMEM_BUDGET and m % tm == 0:
            break
        tm //= 2
    return tm, tn, tk


def _matmul(a, b, *, mode, out_dtype, name, cols=None):
    wide = D_MODEL if cols is not None else None
    if mode == "nn":
        (m, k), (_, n) = a.shape, (b.shape if cols is None else (b.shape[1], cols[1] * wide))
    elif mode == "nt":
        (m, k), (n, _) = a.shape, (b.shape if cols is None else (b.shape[1], cols[1] * wide))
    else:
        (k, m), (_, n) = a.shape, b.shape
    tm, tn, tk = _matmul_tiles(m, n, k, jnp.dtype(out_dtype).itemsize)
    assert m % tm == 0 and n % tn == 0 and k % tk == 0, (name, m, n, k)
    nk = k // tk
    if mode == "nn":
        a_spec = pl.BlockSpec((tm, tk), lambda i, j, kk: (i, kk))
        b_spec = pl.BlockSpec((tk, tn), lambda i, j, kk: (kk, j))
        if cols is not None:
            assert tn == wide
            b_spec = pl.BlockSpec((None, tk, tn), lambda i, j, kk: (cols[0] + j, kk, 0))
        ca, cb = 1, 0
    elif mode == "nt":
        a_spec = pl.BlockSpec((tm, tk), lambda i, j, kk: (i, kk))
        b_spec = pl.BlockSpec((tn, tk), lambda i, j, kk: (j, kk))
        if cols is not None:
            assert tk == wide
            b_spec = pl.BlockSpec((None, tn, tk), lambda i, j, kk: (cols[0] + kk, j, 0))
        ca, cb = 1, 1
    else:
        a_spec = pl.BlockSpec((tk, tm), lambda i, j, kk: (kk, i))
        b_spec = pl.BlockSpec((tk, tn), lambda i, j, kk: (kk, j))
        ca, cb = 0, 0

    def body(a_ref, b_ref, o_ref, *acc):
        kk = pl.program_id(2)
        part = lax.dot_general(a_ref[...], b_ref[...], (((ca,), (cb,)), ((), ())), preferred_element_type=F32)
        if nk == 1:
            o_ref[...] = part.astype(o_ref.dtype)
            return
        acc_ref, = acc

        @pl.when(kk == 0)
        def _():
            acc_ref[...] = part

        @pl.when(kk > 0)
        def _():
            acc_ref[...] += part

        @pl.when(kk == nk - 1)
        def _():
            o_ref[...] = acc_ref[...].astype(o_ref.dtype)

    return pl.pallas_call(
        body, name=name, grid=(m // tm, n // tn, nk),
        in_specs=[a_spec, b_spec], out_specs=pl.BlockSpec((tm, tn), lambda i, j, kk: (i, j)),
        out_shape=jax.ShapeDtypeStruct((m, n), out_dtype),
        scratch_shapes=[pltpu.VMEM((tm, tn), F32)] if nk > 1 else [],
        compiler_params=_params("parallel", "parallel", "arbitrary"),
    )(a, b)


def _matmul_by_destination(a, b_lo, b_hi, *, name, tm=1024, tk=2048):
    (k, m), half = a.shape, N_DEV // 2
    assert b_lo.shape == b_hi.shape == (k, half * D_MODEL) and m % tm == 0 and k % tk == 0, name
    nk = k // tk

    def body(a_ref, lo_ref, hi_ref, o_ref, acc_ref):
        j, kk = pl.program_id(1), pl.program_id(2)

        def step(b_ref):
            part = lax.dot_general(a_ref[...], b_ref[...], (((0,), (0,)), ((), ())), preferred_element_type=F32)

            @pl.when(kk == 0)
            def _():
                acc_ref[...] = part

            @pl.when(kk > 0)
            def _():
                acc_ref[...] += part

        pl.when(j < half)(lambda: step(lo_ref))
        pl.when(j >= half)(lambda: step(hi_ref))

        @pl.when(kk == nk - 1)
        def _():
            o_ref[...] = acc_ref[...].astype(o_ref.dtype)

    lo_spec = pl.BlockSpec((tk, D_MODEL), lambda i, j, kk: (jnp.where(j < half, kk, nk - 1), jnp.minimum(j, half - 1)))
    hi_spec = pl.BlockSpec((tk, D_MODEL), lambda i, j, kk: (jnp.where(j >= half, kk, 0), jnp.maximum(j - half, 0)))
    return pl.pallas_call(
        body, name=name, grid=(m // tm, N_DEV, nk),
        in_specs=[pl.BlockSpec((tk, tm), lambda i, j, kk: (kk, i)), lo_spec, hi_spec],
        out_specs=pl.BlockSpec((None, tm, D_MODEL), lambda i, j, kk: (j, i, 0)),
        out_shape=jax.ShapeDtypeStruct((N_DEV, m, D_MODEL), BF16),
        scratch_shapes=[pltpu.VMEM((tm, D_MODEL), F32)],
        compiler_params=_params("parallel", "parallel", "arbitrary"),
    )(a, b_lo, b_hi)


LANES = 128


def _residues_to_rows(ref, scr, d):
    w = ref.shape[1] // d
    n = ref.shape[0]
    for r in range(d):
        for c in range(w // LANES):
            scr[c, pl.ds(r, n, stride=d), :] = ref[:, r * w + c * LANES:r * w + (c + 1) * LANES].astype(F32)
    return jnp.concatenate([scr[c] for c in range(w // LANES)], axis=1)


def _rows_to_residues(val, ref, scr, d):
    w = val.shape[1]
    n = ref.shape[0]
    for c in range(w // LANES):
        scr[c] = val[:, c * LANES:(c + 1) * LANES].astype(F32)
    for r in range(d):
        for c in range(w // LANES):
            ref[:, r * w + c * LANES:r * w + (c + 1) * LANES] = scr[c, pl.ds(r, n, stride=d), :].astype(ref.dtype)


HEAD_ROWS = 0


def _head_cols(z, whole_head=False):
    width = z.shape[1]
    a = lax.broadcasted_iota(jnp.int32, (width, LANES), 0)
    b = lax.broadcasted_iota(jnp.int32, (width, LANES), 1)
    pick = jnp.where((a // HEAD_DIM == b) if whole_head else (a == b * HEAD_DIM), 1.0, 0.0).astype(BF16)
    out = None
    for part in _split3(z):
        t = lax.dot_general(part, pick, (((1,), (0,)), ((), ())), preferred_element_type=F32)
        out = t if out is None else out + t
    return out


def _head_rows_block(cols):
    return cols.T[:N_HEADS, :]


def _rowwise(fn, rows, consts, out_rows, out_acc, *, ts, name):
    rows = [tuple(r) + (1,) * (4 - len(r)) for r in rows]
    out_rows = [tuple(o) + (1,) * (3 - len(o)) for o in out_rows]
    s = rows[0][0].shape[0] * rows[0][3]
    assert s % ts == 0, (name, s, ts)
    n_rows, n_in = len(rows), len(rows) + len(consts)
    n_row = len(out_rows)
    n_out = n_row + len(out_acc)
    moved = [(idx, w) for idx, (_, w, _, d) in enumerate(rows) if d > 1]
    moved += [(n_rows + idx, w) for idx, (w, _, d) in enumerate(out_rows) if d > 1]

    def body(*refs):
        scratch = dict(zip([key for key, _ in moved], refs[n_in + n_out:]))
        vals = []
        for idx, r in enumerate(refs[:n_in]):
            d = rows[idx][3] if idx < n_rows else 1
            vals.append(r[...] if d == 1 else _residues_to_rows(r, scratch[idx], d))
        row_vals, acc_vals = fn(*vals)
        for idx, (r, v) in enumerate(zip(refs[n_in:n_in + n_row], row_vals)):
            d = out_rows[idx][2]
            if d == 1:
                r[...] = v.astype(r.dtype)
            elif d == HEAD_ROWS:
                r[...] = _head_rows_block(v)
            else:
                _rows_to_residues(v, r, scratch[n_rows + idx], d)
        first = pl.program_id(0) == 0
        for r, v in zip(refs[n_in + n_row:n_in + n_out], acc_vals):
            @pl.when(first)
            def _(r=r, v=v):
                r[...] = v

            @pl.when(jnp.logical_not(first))
            def _(r=r, v=v):
                r[...] += v

    in_specs = [pl.BlockSpec((ts // d, d * w), functools.partial(lambda i, cb: (i, cb), cb=cb)) for _, w, cb, d in rows]
    in_specs += [pl.BlockSpec(c.shape, lambda i: (0, 0)) for c in consts]
    out_specs = [pl.BlockSpec((N_HEADS, ts), lambda i: (0, i)) if d == HEAD_ROWS else
                 pl.BlockSpec((ts // d, d * w), lambda i: (i, 0)) for w, _, d in out_rows]
    out_specs += [pl.BlockSpec(sh, lambda i: (0, 0)) for sh in out_acc]
    out_shape = [jax.ShapeDtypeStruct((N_HEADS, s) if d == HEAD_ROWS else (s // d, d * w), dt) for w, dt, d in out_rows]
    out_shape += [jax.ShapeDtypeStruct(sh, F32) for sh in out_acc]
    outs = pl.pallas_call(
        body, name=name, grid=(s // ts,), in_specs=in_specs, out_specs=out_specs, out_shape=out_shape,
        scratch_shapes=[pltpu.VMEM((w // LANES, ts, LANES), F32) for _, w in moved],
        compiler_params=_params("arbitrary" if out_acc else "parallel"),
    )(*[a for a, _, _, _ in rows], *consts)
    return outs[:n_row], outs[n_row:]


def _full(a, d=1):
    return (a, a.shape[1] // d, 0, d)


def _gate_fn(zu, zv, vg, vb, ws0, ws1, ws2, ws3, bfull, ga):
    u = _gelu_erf(zu)
    vn = _layernorm(_gelu_erf(zv), vg, vb)
    p = lax.broadcasted_iota(jnp.int32, (CHUNK, CHUNK), 0)
    q = lax.broadcasted_iota(jnp.int32, (CHUNK, CHUNK), 1)
    tril = jnp.where(q <= p, 1.0, 0.0)
    group = lax.broadcasted_iota(jnp.int32, (1, A_WIDTH), 1) // CHUNK
    sg = bfull
    for g, w in enumerate((ws0, ws1, ws2, ws3)):
        sg = sg + _bdot(w * tril, jnp.where(group == g, vn, 0.0))
    return _rms(u * sg, ga)


GATE_TS = 4 * CHUNK


def _chunk_rows(rows):
    return [slice(c * CHUNK, (c + 1) * CHUNK) for c in range(rows // CHUNK)]


def _bias_reduce(dbf, name):
    def body(x_ref, o_ref):
        lane = lax.broadcasted_iota(jnp.int32, (CHUNK, CHUNK), 1)
        out = jnp.zeros((CHUNK, CHUNK), F32)
        for g in range(A_GROUPS):
            out = jnp.where(lane == g, jnp.sum(x_ref[:, g * CHUNK:(g + 1) * CHUNK], axis=1, keepdims=True), out)
        o_ref[...] = out

    return pl.pallas_call(body, name=name, out_shape=jax.ShapeDtypeStruct((CHUNK, CHUNK), F32))(dbf)


def _pair_mask(hh):
    lane = lax.broadcasted_iota(jnp.int32, (1, 2 * HEAD_DIM), 1)
    return (lane >= HEAD_DIM * hh) & (lane < HEAD_DIM * (hh + 1))


def _lane_pick(x2, lm):
    return jnp.max(jnp.where(lm, x2, -jnp.inf), axis=1, keepdims=True)


SCALE = HEAD_DIM ** -0.5


def _blocks_per_step(nb):
    return next(n for n in (4, 2, 1) if nb % n == 0)


def _units(nblk):
    return [(b, hp, hh) for b in range(nblk) for hp in range(N_HEADS // 2) for hh in range(2)]


def _attn_specs(nb, nblk):
    cur = pl.BlockSpec((nblk * BAND, B_WIDTH), lambda r, j: (j, r))
    prev = pl.BlockSpec((BAND, B_WIDTH), lambda r, j: (jnp.maximum(nblk * j - 1, 0), r))
    nxt = pl.BlockSpec((BAND, B_WIDTH), lambda r, j: (jnp.minimum(nblk * (j + 1), nb - 1), r))
    return cur, prev, nxt


def _pair_cols(hp):
    return slice(2 * HEAD_DIM * hp, 2 * HEAD_DIM * (hp + 1))


def _rows(b):
    return slice(b * BAND, (b + 1) * BAND)


def _with_prev(cur_ref, prev_ref, b, sl):
    if b == 0:
        return jnp.concatenate([prev_ref[:, sl], cur_ref[_rows(0), sl]], axis=0)
    return cur_ref[(b - 1) * BAND:(b + 1) * BAND, sl]


def _with_next(cur_ref, next_ref, b, sl, nblk):
    if b == nblk - 1:
        return jnp.concatenate([cur_ref[_rows(b), sl], next_ref[:, sl]], axis=0)
    return cur_ref[b * BAND:(b + 2) * BAND, sl]


def _band_valid(other_exists):
    row = lax.broadcasted_iota(jnp.int32, (BAND, 2 * BAND), 0)
    col = lax.broadcasted_iota(jnp.int32, (BAND, 2 * BAND), 1)
    return (col >= row) & (col <= row + BAND), other_exists


def _masked(lm, x):
    return jnp.where(lm, x, jnp.zeros_like(x))


def _attn_fwd(q, k, v, d, name):
    nb = q.shape[0] // BAND
    nblk = _blocks_per_step(nb)
    units = _units(nblk)
    cur, prev, _ = _attn_specs(nb, nblk)

    def body(q_ref, kc_ref, kp_ref, vc_ref, vp_ref, o_ref, l_ref):
        band, has_prev = _band_valid(pl.program_id(1) > 0)
        col = lax.broadcasted_iota(jnp.int32, (BAND, 2 * BAND), 1)
        valid = [band & ((col >= BAND) | has_prev)] + [band] * (nblk - 1)

        def scores(unit):
            b, hp, hh = unit
            sl = _pair_cols(hp)
            return _dotg(_masked(_pair_mask(hh), q_ref[_rows(b), sl]), _with_prev(kc_ref, kp_ref, b, sl), 1, 1)

        ahead, half = scores(units[0]), None
        for idx, (b, hp, hh) in enumerate(units):
            raw = ahead
            if idx + 1 < len(units):
                ahead = scores(units[idx + 1])
            sl, lm = _pair_cols(hp), _pair_mask(hh)
            s = jnp.where(valid[b], raw * SCALE, NEG_INF)
            m = jnp.max(s, axis=1, keepdims=True)
            p = jnp.exp(s - m)
            den = jnp.sum(p, axis=1, keepdims=True)
            o = _dotg(p, _with_prev(vc_ref, vp_ref, b, sl), 1, 0) / den
            lse = m + jnp.log(den)
            if hh == 0:
                half = (o, lse)
            else:
                o_ref[_rows(b), sl] = jnp.where(lm, o, half[0])
                l_ref[_rows(b), sl] = jnp.where(lm, lse, half[1])

    return pl.pallas_call(
        body, name=name, grid=(d, nb // nblk), in_specs=[cur, cur, prev, cur, prev], out_specs=[cur, cur],
        out_shape=[jax.ShapeDtypeStruct(q.shape, F32), jax.ShapeDtypeStruct(q.shape, F32)],
        compiler_params=_params("parallel", "parallel"),
    )(q, k, k, v, v)


def _attn_bwd_q(q, k, v, do, lse, delta, d, name):
    nb = q.shape[0] // BAND
    nblk = _blocks_per_step(nb)
    units = _units(nblk)
    cur, prev, _ = _attn_specs(nb, nblk)
    per_head = pl.BlockSpec((nblk * BAND, LANES), lambda r, j: (j, r))

    def body(q_ref, kc_ref, kp_ref, vc_ref, vp_ref, do_ref, l_ref, dl_ref, dq_ref):
        band, has_prev = _band_valid(pl.program_id(1) > 0)
        col = lax.broadcasted_iota(jnp.int32, (BAND, 2 * BAND), 1)
        valid = [band & ((col >= BAND) | has_prev)] + [band] * (nblk - 1)

        def products(unit):
            b, hp, hh = unit
            sl, lm = _pair_cols(hp), _pair_mask(hh)
            return (_dotg(_masked(lm, q_ref[_rows(b), sl]), _with_prev(kc_ref, kp_ref, b, sl), 1, 1),
                    _dotg(_masked(lm, do_ref[_rows(b), sl]), _with_prev(vc_ref, vp_ref, b, sl), 1, 1))

        ahead, half = products(units[0]), None
        for idx, (b, hp, hh) in enumerate(units):
            raw, dp = ahead
            if idx + 1 < len(units):
                ahead = products(units[idx + 1])
            sl, lm = _pair_cols(hp), _pair_mask(hh)
            s = jnp.where(valid[b], raw * SCALE, NEG_INF)
            head = lax.broadcasted_iota(jnp.int32, (1, LANES), 1) == 2 * hp + hh
            p = jnp.exp(s - _lane_pick(l_ref[_rows(b), :], head))
            ds = p * (dp - _lane_pick(dl_ref[_rows(b), :], head))
            dq = _dotg(ds, _with_prev(kc_ref, kp_ref, b, sl), 1, 0) * SCALE
            if hh == 0:
                half = dq
            else:
                dq_ref[_rows(b), sl] = jnp.where(lm, dq, half).astype(dq_ref.dtype)

    return pl.pallas_call(
        body, name=name, grid=(d, nb // nblk),
        in_specs=[cur, cur, prev, cur, prev, cur, per_head, per_head], out_specs=cur,
        out_shape=jax.ShapeDtypeStruct(q.shape, BF16),
        compiler_params=_params("parallel", "parallel"),
    )(q, k, k, v, v, do, lse, delta)


def _attn_bwd_kv(q, k, v, do, lse_t, delta_t, d, name):
    nb = q.shape[0] // BAND
    nblk = _blocks_per_step(nb)
    units = _units(nblk)
    cur, _, nxt = _attn_specs(nb, nblk)
    t_cur = pl.BlockSpec((1, N_HEADS, nblk * BAND), lambda r, j: (r, 0, j))
    t_nxt = pl.BlockSpec((1, N_HEADS, BAND), lambda r, j: (r, 0, jnp.minimum(nblk * (j + 1), nb - 1)))

    def body(k_ref, v_ref, qc_ref, qn_ref, doc_ref, don_ref, lc_ref, ln_ref, dlc_ref, dln_ref, dk_ref, dv_ref):
        band, has_next = _band_valid(pl.program_id(1) < nb // nblk - 1)
        col = lax.broadcasted_iota(jnp.int32, (BAND, 2 * BAND), 1)
        valid = [band] * (nblk - 1) + [band & ((col < BAND) | has_next)]

        def head_row(c_ref, n_ref, b, h):
            if b == nblk - 1:
                return jnp.concatenate([c_ref[0, h:h + 1, b * BAND:(b + 1) * BAND], n_ref[0, h:h + 1, :]], axis=1)
            return c_ref[0, h:h + 1, b * BAND:(b + 2) * BAND]

        def products(unit):
            b, hp, hh = unit
            sl, lm = _pair_cols(hp), _pair_mask(hh)
            return (_dotg(_masked(lm, k_ref[_rows(b), sl]), _with_next(qc_ref, qn_ref, b, sl, nblk), 1, 1),
                    _dotg(_masked(lm, v_ref[_rows(b), sl]), _with_next(doc_ref, don_ref, b, sl, nblk), 1, 1))

        ahead, half = products(units[0]), None
        for idx, (b, hp, hh) in enumerate(units):
            raw, dpt = ahead
            if idx + 1 < len(units):
                ahead = products(units[idx + 1])
            sl, lm, h = _pair_cols(hp), _pair_mask(hh), 2 * hp + hh
            st = jnp.where(valid[b], raw * SCALE, NEG_INF)
            pt = jnp.exp(st - head_row(lc_ref, ln_ref, b, h))
            dv = _dotg(pt, _with_next(doc_ref, don_ref, b, sl, nblk), 1, 0)
            dst = pt * (dpt - head_row(dlc_ref, dln_ref, b, h))
            dk = _dotg(dst, _with_next(qc_ref, qn_ref, b, sl, nblk), 1, 0) * SCALE
            if hh == 0:
                half = (dk, dv)
            else:
                dk_ref[_rows(b), sl] = jnp.where(lm, dk, half[0]).astype(dk_ref.dtype)
                dv_ref[_rows(b), sl] = jnp.where(lm, dv, half[1]).astype(dv_ref.dtype)

    return pl.pallas_call(
        body, name=name, grid=(d, nb // nblk),
        in_specs=[cur, cur, cur, nxt, cur, nxt, t_cur, t_nxt, t_cur, t_nxt], out_specs=[cur, cur],
        out_shape=[jax.ShapeDtypeStruct(q.shape, BF16), jax.ShapeDtypeStruct(q.shape, BF16)],
        compiler_params=_params("parallel", "parallel"),
    )(k, v, q, q, do, do, lse_t, lse_t, delta_t, delta_t)


def _spread_rows(a, d):
    return a.reshape(N_HEADS, a.shape[1] // d, d).transpose(2, 0, 1)


FF_TS = 512
FF_TC = 1024
FF_SUB = 256
HALO = 8
HALO_BF16 = 16
UP_BLOCKS = D_MODEL // FF_TC


def _conv3(ext, w, b):
    return b + w[0:1, :] * pltpu.roll(ext, 2, 0) + w[1:2, :] * pltpu.roll(ext, 1, 0) + w[2:3, :] * ext


def _ffn_specs(s, cols_first):
    nrb = s // FF_TS
    per, per16 = FF_TS // HALO, FF_TS // HALO_BF16

    def mk(block, fn):
        if cols_first:
            return pl.BlockSpec(block, lambda j, i: fn(i, j))
        return pl.BlockSpec(block, lambda i, j: fn(i, j))

    specs = types.SimpleNamespace(
        nrb=nrb, ncb=D_FF // FF_TC,
        row=mk((FF_TS, FF_TC), lambda i, j: (i, j)),
        before=mk((HALO, FF_TC), lambda i, j: (jnp.maximum(i * per - 1, 0), j)),
        after=mk((HALO, FF_TC), lambda i, j: (jnp.minimum((i + 1) * per, nrb * per - 1), j)),
        w=mk((3, FF_TC), lambda i, j: (0, j)),
        b=mk((1, FF_TC), lambda i, j: (0, j)),
        part=mk((HALO, FF_TC), lambda i, j: (i, j)),
        act=mk((FF_TS, D_MODEL), lambda i, j: (i, 0)),
        act_before=mk((HALO_BF16, D_MODEL), lambda i, j: (jnp.maximum(i * per16 - 1, 0), 0)),
        act_after=mk((HALO_BF16, D_MODEL), lambda i, j: (jnp.minimum((i + 1) * per16, nrb * per16 - 1), 0)),
        up_gate=mk((None, D_MODEL, FF_TC), lambda i, j: (j // UP_BLOCKS, 0, j % UP_BLOCKS)),
        up_val=mk((None, D_MODEL, FF_TC), lambda i, j: (N_DEV // 2 + j // UP_BLOCKS, 0, j % UP_BLOCKS)),
        down=mk((FF_TC, D_MODEL), lambda i, j: (j, 0)),
    )
    return specs


def _ffn_up_geglu(h, w_up, wg, wv, bg, bv, name):
    s = h.shape[0]
    sp = _ffn_specs(s, True)

    def body(h_ref, hb_ref, ugw_ref, uvw_ref, wg_ref, wv_ref, bg_ref, bv_ref, ug_ref, uv_ref, y_ref):
        keep = jnp.where(pl.program_id(1) > 0, 1.0, 0.0).astype(BF16)
        hext = jnp.concatenate([hb_ref[...] * keep, h_ref[...]], axis=0)
        eg = _dotg(hext, ugw_ref[...], 1, 0)
        ug_ref[...] = eg[HALO_BF16:, :]
        act = _gelu_tanh(_conv3(eg, wg_ref[...], bg_ref[...])[HALO_BF16:, :])
        ev = _dotg(hext, uvw_ref[...], 1, 0)
        uv_ref[...] = ev[HALO_BF16:, :]
        y_ref[...] = (act * _conv3(ev, wv_ref[...], bv_ref[...])[HALO_BF16:, :]).astype(y_ref.dtype)

    return pl.pallas_call(
        body, name=name, grid=(sp.ncb, sp.nrb),
        in_specs=[sp.act, sp.act_before, sp.up_gate, sp.up_val, sp.w, sp.w, sp.b, sp.b],
        out_specs=[sp.row, sp.row, sp.row],
        out_shape=[jax.ShapeDtypeStruct((s, D_FF), F32), jax.ShapeDtypeStruct((s, D_FF), F32),
                   jax.ShapeDtypeStruct((s, D_FF), BF16)],
        compiler_params=_params("parallel", "parallel"),
    )(h, h, w_up, w_up, wg, wv, bg, bv)


def _sum_parts(parts, name):
    n = parts.shape[0] // HALO

    def body(p_ref, o_ref):
        acc = p_ref[0:HALO, :]
        for t in range(1, n):
            acc = acc + p_ref[t * HALO:(t + 1) * HALO, :]
        o_ref[...] = acc

    return pl.pallas_call(body, name=name, out_shape=jax.ShapeDtypeStruct((HALO, parts.shape[1]), F32))(parts)


def _ffn_geglu_bwd(ug, uv, df, w_down, w_up, wg, wv, bg, bv, name):
    s = ug.shape[0]
    sp = _ffn_specs(s, False)
    nrb = sp.nrb
    rows = FF_TS + 2 * HALO
    lo, hi = HALO, HALO + FF_TS

    def body(ug_ref, uv_ref, hg_ref, hv_ref, ng_ref, nv_ref, df_ref, dfn_ref, dw_ref, ugw_ref, uvw_ref,
             wg_ref, wv_ref, bg_ref, bv_ref, dug_ref, duv_ref, dh_ref, dwg_ref, dwv_ref):
        i, j = pl.program_id(0), pl.program_id(1)
        keep_top = jnp.where(i > 0, 1.0, 0.0)
        keep_bot = jnp.where(i < nrb - 1, 1.0, 0.0).astype(BF16)
        dfe = jnp.concatenate([df_ref[...], dfn_ref[...] * keep_bot], axis=0)

        def back(dc, e, w, du_ref, sl):
            up1 = pltpu.roll(dc, rows - 1, 0)
            up2 = pltpu.roll(dc, rows - 2, 0)
            du = (w[2:3, :] * dc + w[1:2, :] * up1 + w[0:1, :] * up2)[lo:hi, :].astype(BF16)
            du_ref[:, sl] = du
            p1, p2 = up1 * e, up2 * e
            colsum = lambda p: jnp.sum(p[lo:hi, :], axis=0, keepdims=True)
            row = lambda p, t: p[t:t + 1, :]
            d_w1 = colsum(p1) + row(p1, lo - 1) - row(p1, hi - 1)
            d_w0 = colsum(p2) + row(p2, lo - 2) + row(p2, lo - 1) - row(p2, hi - 2) - row(p2, hi - 1)
            sums = [d_w0, d_w1, colsum(dc * e), colsum(dc), jnp.zeros((HALO - 4, FF_SUB), F32)]
            return du, jnp.concatenate(sums, axis=0)

        dh = None
        n_sub = FF_TC // FF_SUB
        cols = [slice(c * FF_SUB, (c + 1) * FF_SUB) for c in range(n_sub)]
        d_act = lambda c: _dotg(dfe, dw_ref[cols[c], :], 1, 1)[:FF_TS + HALO, :]
        ahead = d_act(0)
        for c in range(n_sub):
            sl, dy = cols[c], ahead
            if c + 1 < n_sub:
                ahead = d_act(c + 1)
            dye = jnp.concatenate([jnp.zeros((HALO, FF_SUB), F32), dy], axis=0)
            eg = jnp.concatenate([hg_ref[:, sl] * keep_top, ug_ref[:, sl], ng_ref[:, sl]], axis=0)
            ev = jnp.concatenate([hv_ref[:, sl] * keep_top, uv_ref[:, sl], nv_ref[:, sl]], axis=0)
            wg_, wv_ = wg_ref[:, sl], wv_ref[:, sl]
            gate = _conv3(eg, wg_, bg_ref[:, sl])
            val = _conv3(ev, wv_, bv_ref[:, sl])
            act, slope = _gelu_tanh_and_slope(gate)
            dug, dwg_ref[:, sl] = back((dye * val) * slope, eg, wg_, dug_ref, sl)
            duv, dwv_ref[:, sl] = back(dye * act, ev, wv_, duv_ref, sl)
            term = _dotg(dug, ugw_ref[:, sl], 1, 1) + _dotg(duv, uvw_ref[:, sl], 1, 1)
            dh = term if dh is None else dh + term

        @pl.when(j == 0)
        def _():
            dh_ref[...] = dh

        @pl.when(j > 0)
        def _():
            dh_ref[...] += dh

    parts = jax.ShapeDtypeStruct((nrb * HALO, D_FF), F32)
    dug, duv, dh, pg, pv = pl.pallas_call(
        body, name=name, grid=(nrb, sp.ncb),
        in_specs=[sp.row, sp.row, sp.before, sp.before, sp.after, sp.after, sp.act, sp.act_after, sp.down,
                  sp.up_gate, sp.up_val, sp.w, sp.w, sp.b, sp.b],
        out_specs=[sp.row, sp.row, sp.act, sp.part, sp.part],
        out_shape=[jax.ShapeDtypeStruct((s, D_FF), BF16), jax.ShapeDtypeStruct((s, D_FF), BF16),
                   jax.ShapeDtypeStruct((s, D_MODEL), F32), parts, parts],
        compiler_params=_params("parallel", "arbitrary"),
    )(ug, uv, ug, uv, ug, uv, df, df, w_down, w_up, w_up, wg, wv, bg, bv)
    return dug, duv, dh, _sum_parts(pg, name=name + "_sum_gate"), _sum_parts(pv, name=name + "_sum_val")


def _rope_tables(s):
    inv = ROPE_THETA ** (-jnp.arange(0, ROT_DIM, 2, dtype=F32) / ROT_DIM)
    ang = jnp.arange(s, dtype=F32)[:, None] * inv[None, :]
    cos8, sin8 = jnp.cos(ang), jnp.sin(ang)
    rest = HEAD_DIM - ROT_DIM
    cos_h = jnp.concatenate([cos8, cos8, jnp.ones((s, rest), F32)], axis=1)
    sin_h = jnp.concatenate([sin8, sin8, jnp.zeros((s, rest), F32)], axis=1)
    return jnp.tile(cos_h, (1, LANES // HEAD_DIM)), jnp.tile(sin_h, (1, LANES // HEAD_DIM))


def _all_heads(table):
    return jnp.concatenate([table] * (B_WIDTH // LANES), axis=1)


def _layer_fwd(x, w, cos, sin, last):
    sv = types.SimpleNamespace(x=x)
    (sv.h1,), _ = _rowwise(lambda xb, g: ((_rms(xb, g),), ()), [_full(x)], [w.g_pre], [(D_MODEL, BF16)], [],
                           ts=1024, name="pre_mix_norm")
    sv.proj = _matmul(sv.h1, w.big("w_in", sv.h1), mode="nt", out_dtype=F32, name="proj")

    gate_consts = [w.vg, w.vb, *w.ws, w.bfull, w.ga]
    def gate_fwd_fn(zu, zv, *consts):
        chunks = [_gate_fn(zu[r], zv[r], *consts) for r in _chunk_rows(zu.shape[0])]
        return (jnp.concatenate(chunks, axis=0),), ()

    (na,), _ = _rowwise(gate_fwd_fn, [(sv.proj, A_WIDTH, 0), (sv.proj, A_WIDTH, 1)], gate_consts,
                        [(A_WIDTH, BF16)], [], ts=GATE_TS, name="gate_fwd")

    def rope_fn(qr, kr, vr, cs, sn):
        cs, sn = _all_heads(cs), _all_heads(sn)
        return (qr * cs + _rot_half(qr) * sn, kr * cs + _rot_half(kr) * sn, vr), ()

    def rope_all(qr, kr, vr, cs, sn):
        return rope_fn(qr, kr, vr, cs, sn)[0] * len(DILATIONS), ()

    qkv, _ = _rowwise(
        rope_all, [(sv.proj, B_WIDTH, 2), (sv.proj, B_WIDTH, 3), (sv.proj, B_WIDTH, 4), _full(cos), _full(sin)], [],
        [(B_WIDTH, BF16, d) for d in DILATIONS for _ in range(3)], [], ts=512, name="rope_fwd")
    sv.qkv = {d: qkv[3 * i:3 * i + 3] for i, d in enumerate(DILATIONS)}

    branch = []
    for d in DILATIONS:
        o, l = _attn_fwd(*sv.qkv[d], d, name=f"attn_fwd_d{d}")
        branch += [_full(o, d), _full(l, d)]

    def combine_fn(o1, l1, o2, l2, o3, l3, nab, gb):
        m = jnp.maximum(jnp.maximum(l1, l2), l3)
        e1, e2, e3 = jnp.exp(l1 - m), jnp.exp(l2 - m), jnp.exp(l3 - m)
        den = e1 + e2 + e3
        ob = (e1 / den) * o1 + (e2 / den) * o2 + (e3 / den) * o3
        mixed = jnp.concatenate([nab, _rms(ob, gb).astype(BF16)], axis=1)
        lse = _head_cols(m + jnp.log(den))
        return (mixed, ob, lse) + (lse,) * len(DILATIONS), ()

    (sv.mixed, sv.ob, sv.lse_rows, *lses), _ = _rowwise(
        combine_fn, branch + [_full(na)], [w.gb],
        [(D_MODEL, BF16), (B_WIDTH, F32), (LANES, F32, HEAD_ROWS)] + [(LANES, F32, d) for d in DILATIONS], [],
        ts=512, name="combine")
    sv.lse = dict(zip(DILATIONS, lses))
    sv.y = _matmul(sv.mixed, w.big("w_out", sv.mixed), mode="nn", out_dtype=F32, name="mix_out")

    def mid_fn(xb, yb, g1, g2):
        x1 = xb + _rms(yb, g1)
        return (x1, _rms(x1, g2)), ()

    (sv.x1, sv.h2), _ = _rowwise(mid_fn, [_full(x), _full(sv.y)], [w.g_pm, w.g_pf], [(D_MODEL, F32), (D_MODEL, BF16)], [],
                                 ts=1024, name="post_mix_norm")
    conv_w = w.big("conv_w", sv.h2)
    sv.ug, sv.uv, sv.yff = _ffn_up_geglu(sv.h2, w.big("w_up", sv.h2), conv_w[:, :D_FF], conv_w[:, D_FF:],
                                         w.cb_g, w.cb_v, name="ffn_up_geglu")
    sv.f = _matmul(sv.yff, w.big("w_down", sv.yff), mode="nn", out_dtype=F32, name="ffn_down")
    if last:
        return None, sv
    (x2,), _ = _rowwise(lambda xb, fb, g: ((xb + _rms(fb, g),), ()), [_full(sv.x1), _full(sv.f)], [w.g_post],
                        [(D_MODEL, F32)], [], ts=1024, name="post_ffn_norm")
    return x2, sv


def _layer_bwd(dx2, sv, w, cos, sin, emit):
    g = {}

    def post_fn(fb, dxb, gp):
        _, vjp = jax.vjp(_rms, fb, gp)
        df, dg = vjp(dxb)
        return (df,), (dg,)

    (df,), (g["post_ffn_norm"],) = _rowwise(post_fn, [_full(sv.f), _full(dx2)], [w.g_post], [(D_MODEL, BF16)],
                                            [(1, D_MODEL)], ts=1024, name="post_ffn_norm_bwd")
    big = {"w_down": _matmul(sv.yff, df, mode="tn", out_dtype=BF16, name="ffn_down_dw").reshape(N_DEV, -1, D_MODEL)}
    conv_w = w.big("conv_w", df)
    dug, duv, dh2, dwg, dwv = _ffn_geglu_bwd(sv.ug, sv.uv, df, w.big("w_down", df), w.big("w_up", df),
                                             conv_w[:, :D_FF], conv_w[:, D_FF:], w.cb_g, w.cb_v, name="ffn_geglu_bwd")
    big["conv_w"] = jnp.concatenate([dwg[0:3], dwv[0:3]], axis=1).reshape(3, N_DEV, D_MODEL).transpose(1, 0, 2)
    g["conv_b"] = jnp.concatenate([dwg[3], dwv[3]], axis=0)
    big["w_up"] = _matmul_by_destination(sv.h2, dug, duv, name="ffn_up_dw")
    g_pm = w.g_pm + emit(big)

    def mid_fn(x1b, yb, dhb, dxb, g1, g2):
        _, vjp2 = jax.vjp(_rms, x1b, g2)
        dx1h, dg2 = vjp2(dhb)
        dx1 = dxb + dx1h
        _, vjp1 = jax.vjp(_rms, yb, g1)
        dy, dg1 = vjp1(dx1)
        return (dx1, dy), (dg1, dg2)

    (dx1, dy), (g["post_mix_norm"], g["pre_ffn_norm"]) = _rowwise(
        mid_fn, [_full(sv.x1), _full(sv.y), _full(dh2), _full(dx2)], [g_pm, w.g_pf],
        [(D_MODEL, F32), (D_MODEL, BF16)], [(1, D_MODEL), (1, D_MODEL)], ts=512, name="post_mix_norm_bwd")
    dmixed = _matmul(dy, w.big("w_out", dy), mode="nt", out_dtype=F32, name="mix_out_dx")
    dw_out = _matmul(sv.mixed, dy, mode="tn", out_dtype=BF16, name="mix_out_dw").reshape(N_DEV, -1, D_MODEL)
    g_b = w.gb + emit({"w_out": dw_out})

    def attn_out_fn(obb, dmb, gb):
        _, vjp = jax.vjp(_rms, obb, gb)
        do, dgb = vjp(dmb)
        delta = _head_cols(do * obb, whole_head=True)
        return (delta,) + (do,) * len(DILATIONS) + (delta,) * len(DILATIONS), (dgb,)

    (delta_rows, *outs), (g["out_norm_b"],) = _rowwise(
        attn_out_fn, [_full(sv.ob), (dmixed, B_WIDTH, 1)], [g_b],
        [(LANES, F32, HEAD_ROWS)] + [(B_WIDTH, BF16, d) for d in DILATIONS] + [(LANES, F32, d) for d in DILATIONS],
        [(1, B_WIDTH)], ts=512, name="attn_out_bwd")
    do = dict(zip(DILATIONS, outs[:len(DILATIONS)]))
    delta = dict(zip(DILATIONS, outs[len(DILATIONS):]))
    parts = {"q": [], "k": [], "v": []}
    for d in DILATIONS:
        qv, kv, vv = sv.qkv[d]
        dq = _attn_bwd_q(qv, kv, vv, do[d], sv.lse[d], delta[d], d, name=f"attn_bwd_q_d{d}")
        dk, dv = _attn_bwd_kv(qv, kv, vv, do[d], _spread_rows(sv.lse_rows, d), _spread_rows(delta_rows, d), d,
                              name=f"attn_bwd_kv_d{d}")
        parts["q"].append(_full(dq, d))
        parts["k"].append(_full(dk, d))
        parts["v"].append(_full(dv, d))

    def rope_bwd_fn(q1, q2, q3, k1, k2, k3, v1, v2, v3, cs, sn):
        cs, sn = _all_heads(cs), _all_heads(sn)

        def back(t):
            return t * cs - _rot_half(t * sn)
        return (jnp.concatenate([back(q1 + q2 + q3), back(k1 + k2 + k3), v1 + v2 + v3], axis=1),), ()

    (dzb,), _ = _rowwise(rope_bwd_fn, parts["q"] + parts["k"] + parts["v"] + [_full(cos), _full(sin)], [],
                         [(3 * B_WIDTH, BF16)], [], ts=512, name="rope_bwd")

    gate_consts = [w.vg, w.vb, *w.ws, w.bfull, w.ga]

    def gate_bwd_fn(zu, zv, dna, dzb_rows, *consts):
        dz, sums = [], None
        for r in _chunk_rows(zu.shape[0]):
            _, vjp = jax.vjp(_gate_fn, zu[r], zv[r], *consts)
            grads = vjp(dna[r])
            dz.append(jnp.concatenate([grads[0].astype(BF16), grads[1].astype(BF16), dzb_rows[r]], axis=1))
            sums = grads[2:] if sums is None else tuple(a + b for a, b in zip(sums, grads[2:]))
        return (jnp.concatenate(dz, axis=0),), tuple(sums)

    (dproj,), gsmall = _rowwise(
        gate_bwd_fn, [(sv.proj, A_WIDTH, 0), (sv.proj, A_WIDTH, 1), (dmixed, A_WIDTH, 0), _full(dzb)], gate_consts,
        [(IN_COLS, BF16)], [c.shape for c in gate_consts], ts=GATE_TS, name="gate_bwd")
    g["v_norm_g"], g["v_norm_b"] = gsmall[0], gsmall[1]
    g["w_spatial"] = jnp.stack(gsmall[2:6])
    g["b_spatial"] = _bias_reduce(gsmall[6], name="bias_reduce")[:, :A_GROUPS].T
    g["out_norm_a"] = gsmall[7]

    dh1 = _matmul(dproj, w.big("w_in", dproj), mode="nn", out_dtype=F32, name="proj_dx")
    dw_in = _matmul(dproj, sv.h1, mode="tn", out_dtype=BF16, name="proj_dw").reshape(N_DEV, -1, D_MODEL)
    g_pre = w.g_pre + emit({"w_in": dw_in})

    def pre_fn(xb, dhb, dxb, gp):
        _, vjp = jax.vjp(_rms, xb, gp)
        dxh, dg = vjp(dhb)
        return (dxb + dxh,), (dg,)

    (dx,), (g["pre_mix_norm"],) = _rowwise(pre_fn, [_full(sv.x), _full(dh1), _full(dx1)], [g_pre], [(D_MODEL, F32)],
                                           [(1, D_MODEL)], ts=512, name="pre_mix_norm_bwd")
    return dx, g


def _layer_weights(l, full, small):
    row = lambda a: a[l].reshape(1, -1)
    return types.SimpleNamespace(
        big=functools.partial(full, l),
        g_pre=row(small["pre_mix_norm"]), vg=row(small["v_norm_g"]), vb=row(small["v_norm_b"]),
        ws=[small["w_spatial"][l, gi] for gi in range(A_GROUPS)],
        bfull=jnp.repeat(small["b_spatial"][l].T, CHUNK, axis=1),
        ga=row(small["out_norm_a"]), gb=row(small["out_norm_b"]),
        g_pm=row(small["post_mix_norm"]), g_pf=row(small["pre_ffn_norm"]),
        cb_g=small["conv_b"][l][:D_FF].reshape(1, -1), cb_v=small["conv_b"][l][D_FF:].reshape(1, -1),
        g_post=row(small["post_ffn_norm"]))


def _local_step(x, target, full, small, emit, emit_small, started):
    s = x.shape[0]
    cos, sin = _rope_tables(s)
    ws = [_layer_weights(l, full, small) for l in range(N_LAYERS)]
    ws[0].g_pre = ws[0].g_pre + started
    saved = []
    h = x
    for l in range(N_LAYERS):
        h, sv = _layer_fwd(h, ws[l], cos, sin, last=l == N_LAYERS - 1)
        saved.append(sv)

    def loss_fn(xb, fb, tb, g):
        diff = (xb + _rms(fb, g)) - tb
        return (diff * (1.0 / D_MODEL),), (jnp.sum(diff * diff, axis=0, keepdims=True),)

    (dh,), (sq,) = _rowwise(loss_fn, [_full(saved[-1].x1), _full(saved[-1].f), _full(target)], [ws[-1].g_post],
                            [(D_MODEL, F32)], [(1, D_MODEL)], ts=1024, name="loss")
    loss = 0.5 * jnp.sum(sq) * (1.0 / D_MODEL)
    grads = [None] * N_LAYERS
    for l in reversed(range(N_LAYERS)):
        dh, grads[l] = _layer_bwd(dh, saved[l], ws[l], cos, sin, functools.partial(emit, l))
        token = emit_small(l, grads[l], loss)
        if l > 0:
            ws[l - 1].g_post = ws[l - 1].g_post + token
    return loss, dh, grads


def _place():
    return lax.axis_index("x"), lax.axis_index("y"), lax.axis_index("c")


FLIPS = ((1, 0, 0), (0, 1, 0), (1, 1, 0), (0, 0, 1), (1, 0, 1), (0, 1, 1), (1, 1, 1))
HBM_SPEC = pl.BlockSpec(memory_space=pltpu.HBM)
SEM_SPEC = pl.BlockSpec(memory_space=pltpu.SEMAPHORE)
SPLIT_COPY = pltpu.CompilerParams(has_side_effects=pltpu.SideEffectType.DATAFLOW_SIDE_EFFECTING)


def _peers():
    mx, my, mc = _place()
    out = []
    for fx, fy, fc in FLIPS:
        px, py, pc = (1 - mx if fx else mx), (1 - my if fy else my), (1 - mc if fc else mc)
        out.append(((px, py, pc), 4 * px + 2 * py + pc))
    return out


def _flat_copies(scatter, src_refs, land_refs, send_sems, recv_sems):
    mx, my, mc = _place()
    me = 4 * mx + 2 * my + mc
    n = len(src_refs)
    copies = []
    for t in range(n):
        for i, (peer, number) in enumerate(_peers()):
            copies.append(pltpu.make_async_remote_copy(
                src_ref=src_refs[t].at[number] if scatter else src_refs[t],
                dst_ref=land_refs[t].at[i] if scatter else land_refs[t].at[me],
                send_sem=send_sems.at[t * len(FLIPS) + i], recv_sem=recv_sems.at[t * len(FLIPS) + i],
                device_id=peer, device_id_type=MESH_ID))
    return copies


def _flat_start(arrays, scatter, name):
    n = len(arrays)
    slots = len(FLIPS) if scatter else N_DEV
    lands = [lax.empty((slots,) + (a.shape[1:] if scatter else a.shape), a.dtype) for a in arrays]

    def body(*refs):
        src, land, (send_sems, recv_sems), token = refs[:n], refs[n:2 * n], refs[2 * n:2 * n + 2], refs[-1]
        for cp in _flat_copies(scatter, src, land, send_sems, recv_sems):
            cp.start()
        token[...] = jnp.zeros_like(token)

    hbm = [pltpu.HBM(a.shape, a.dtype) for a in arrays] + [pltpu.HBM(a.shape, a.dtype) for a in lands]
    sems = pltpu.SemaphoreType.DMA((n * len(FLIPS),))
    outs = pl.pallas_call(
        body, name=name, out_shape=(sems, sems, *hbm, jax.ShapeDtypeStruct((8, 128), F32)),
        in_specs=[HBM_SPEC] * (2 * n),
        out_specs=(SEM_SPEC, SEM_SPEC, *([HBM_SPEC] * (2 * n)), pl.BlockSpec(memory_space=pltpu.VMEM)),
        input_output_aliases={t: 2 + t for t in range(2 * n)}, compiler_params=SPLIT_COPY,
    )(*[pltpu.with_memory_space_constraint(a, pltpu.HBM) for a in (*arrays, *lands)])
    return types.SimpleNamespace(sems=outs[:2], thru=outs[2:2 + 2 * n], scatter=scatter, n=n), outs[-1][0:1, 0:1]


def _flat_wait(handle, after, name):
    n = handle.n

    def body(*refs):
        src, land, (send_sems, recv_sems) = refs[:n], refs[n:2 * n], refs[2 * n:2 * n + 2]
        for cp in _flat_copies(handle.scatter, src, land, send_sems, recv_sems):
            cp.wait_send()
            cp.wait_recv()

    outs = pl.pallas_call(
        body, name=name, out_shape=tuple(pltpu.HBM(a.shape, a.dtype) for a in handle.thru),
        in_specs=[HBM_SPEC] * (2 * n) + [SEM_SPEC, SEM_SPEC, ANY], out_specs=tuple([HBM_SPEC] * (2 * n)),
        input_output_aliases={t: t for t in range(2 * n)}, compiler_params=SPLIT_COPY,
    )(*handle.thru, *handle.sems, after)
    return outs[:n], outs[n:]


def _adamw(w, g, m, v):
    m2 = ADAM_B1 * m + (1.0 - ADAM_B1) * g
    v2 = ADAM_B2 * v + (1.0 - ADAM_B2) * (g * g)
    m_hat = m2 / (1.0 - ADAM_B1 ** ADAM_STEP)
    v_hat = v2 / (1.0 - ADAM_B2 ** ADAM_STEP)
    return -ADAM_LR * (m_hat / (jnp.sqrt(v_hat) + ADAM_EPS) + ADAM_WD * w), m2, v2


def _adamw_sharded(me, mine, landed, w, m, v, tr, name):
    _, r, c = w.shape
    nt = r // tr
    assert r % tr == 0 and len(mine) == len(landed) == N_LAYERS == 2, name
    per_layer = 1 + len(FLIPS)

    def body(me_ref, *refs):
        terms, (w_ref, m_ref, v_ref), outs = refs[:2 * per_layer], refs[2 * per_layer:2 * per_layer + 3], refs[-4:]
        layer = pl.program_id(0)

        def total(group):
            g = group[0][0].astype(F32)
            for t in group[1:]:
                g = g + t[0].astype(F32)
            return g

        g = jnp.where(layer == 0, total(terms[:per_layer]), total(terms[per_layer:]))
        d, m2, v2 = _adamw(w_ref[0], g, m_ref[0], v_ref[0])
        for o, val in zip(outs, (g, d, m2, v2)):
            o[0] = val

    def held(l):
        return lambda layer, i: jnp.where(layer == l, i, nt - 1 if l == 0 else 0)

    in_specs = []
    for l in range(N_LAYERS):
        rows = held(l)
        in_specs.append(pl.BlockSpec((1, tr, c), functools.partial(lambda layer, i, me_ref, rows: (me_ref[0], rows(layer, i), 0), rows=rows)))
        for k in range(len(FLIPS)):
            in_specs.append(pl.BlockSpec(
                (1, tr, c), functools.partial(lambda layer, i, me_ref, rows, k: (k, rows(layer, i), 0), rows=rows, k=k)))
    tile = pl.BlockSpec((1, tr, c), lambda layer, i, me_ref: (layer, i, 0))
    operands = []
    for l in range(N_LAYERS):
        operands += [mine[l]] + [landed[l]] * len(FLIPS)
    return pl.pallas_call(
        body, name=name, out_shape=[jax.ShapeDtypeStruct(w.shape, F32)] * 4,
        grid_spec=pltpu.PrefetchScalarGridSpec(
            num_scalar_prefetch=1, grid=(N_LAYERS, nt), in_specs=in_specs + [tile] * 3, out_specs=[tile] * 4),
        compiler_params=_params("arbitrary", "arbitrary"),
    )(me, *operands, w, m, v)


def _adamw_replicated(parts, w, m, v, name):
    n_layers = len(parts)

    def body(*refs):
        p_refs, (w_ref, m_ref, v_ref), outs = refs[:n_layers], refs[n_layers:n_layers + 3], refs[n_layers + 3:]
        for l, p_ref in enumerate(p_refs):
            g = p_ref[0]
            for j in range(1, N_DEV):
                g = g + p_ref[j]
            d, m2, v2 = _adamw(w_ref[l], g, m_ref[l], v_ref[l])
            for o, val in zip(outs, (g, d, m2, v2)):
                o[l] = val

    return pl.pallas_call(body, name=name, out_shape=[jax.ShapeDtypeStruct(w.shape, F32)] * 4,
                          compiler_params=pltpu.CompilerParams(vmem_limit_bytes=VMEM_LIMIT_BYTES))(*parts, w, m, v)


SMALL_USED = 80384


def _pack_small(vals, rider=None):
    pieces = [vals[n].reshape(-1) for n in SMALL_NAMES] + ([] if rider is None else [rider.reshape(1)])
    flat = jnp.concatenate(pieces)
    assert flat.shape[0] == SMALL_USED + (rider is not None)
    return jnp.concatenate([flat, jnp.zeros((SMALL_ROWS * D_MODEL - flat.shape[0],), F32)]).reshape(SMALL_ROWS, D_MODEL)


def _pack_small_layers(vals):
    flat = jnp.concatenate([vals[n].reshape(N_LAYERS, -1) for n in SMALL_NAMES], axis=1)
    return jnp.pad(flat, ((0, 0), (0, SMALL_ROWS * D_MODEL - SMALL_USED))).reshape(N_LAYERS, SMALL_ROWS, D_MODEL)


def _unpack_small_layers(packed, shapes):
    flat, out, at = packed.reshape(N_LAYERS, -1), {}, 0
    for n in SMALL_NAMES:
        size = math.prod(shapes[n][1:])
        out[n] = flat[:, at:at + size].reshape(shapes[n])
        at += size
    return out


GATHER_GROUPS = ((0, ("w_in",)), (0, ("w_out", "w_up", "conv_w")), (0, ("w_down",)),
                 (1, ("w_in",)), (1, ("w_out", "w_up", "conv_w")), (1, ("w_down",)))
ADAMW_TILE_ROWS = {"w_in": 320, "w_out": 128, "w_up": 256, "w_down": 256, "conv_w": 3}


def _assemble(name, land):
    if name == "w_in":
        return land.reshape(IN_COLS, D_MODEL)
    if name == "conv_w":
        return land.transpose(1, 0, 2).reshape(3, 2 * D_FF)
    if name == "w_up":
        return land
    return land.reshape(-1, D_MODEL)


def _start_gathers(wts, me):
    started, groups = jnp.zeros((1, 1), F32), []
    for gi, (l, names) in enumerate(GATHER_GROUPS):
        local = {"conv_w": lambda a: a, "w_in": lambda a: a.T.astype(BF16)}
        blocks = [local.get(n, lambda a: a.astype(BF16))(wts[n][l]) for n in names]
        if gi > 0:
            blocks[0] = blocks[0] + token.astype(blocks[0].dtype)
        handle, token = _flat_start(blocks, False, name=f"gather_start_{gi}")
        groups.append(types.SimpleNamespace(layer=l, names=names, blocks=blocks, handle=handle, got=None, index=gi))
        started = started + token

    def fetch(l, name, after):
        grp = next(gr for gr in groups if gr.layer == l and name in gr.names)
        if grp.got is None:
            lands = _flat_wait(grp.handle, after, name=f"gather_wait_{grp.index}")[1]
            grp.got = {}
            for n, blk, land in zip(grp.names, grp.blocks, lands):
                own = (me,) + (0,) * blk.ndim
                grp.got[n] = _assemble(n, lax.dynamic_update_slice(land, blk[None], own))
        return grp.got[name]

    return fetch, started


def kernel(x, pre_mix_norm, w_in, v_norm_g, v_norm_b, w_spatial, b_spatial, out_norm_a, out_norm_b, w_out, post_mix_norm, pre_ffn_norm, w_up, conv_w, conv_b, w_down, post_ffn_norm, loss_target, m_pre_mix_norm, m_w_in, m_v_norm_g, m_v_norm_b, m_w_spatial, m_b_spatial, m_out_norm_a, m_out_norm_b, m_w_out, m_post_mix_norm, m_pre_ffn_norm, m_w_up, m_conv_w, m_conv_b, m_w_down, m_post_ffn_norm, v_pre_mix_norm, v_w_in, v_v_norm_g, v_v_norm_b, v_w_spatial, v_b_spatial, v_out_norm_a, v_out_norm_b, v_w_out, v_post_mix_norm, v_pre_ffn_norm, v_w_up, v_conv_w, v_conv_b, v_w_down, v_post_ffn_norm):
    wts = dict(zip(WEIGHT_NAMES, (pre_mix_norm, w_in, v_norm_g, v_norm_b, w_spatial, b_spatial, out_norm_a, out_norm_b,
                                  w_out, post_mix_norm, pre_ffn_norm, w_up, conv_w, conv_b, w_down, post_ffn_norm)))
    mom1 = dict(zip(WEIGHT_NAMES, (m_pre_mix_norm, m_w_in, m_v_norm_g, m_v_norm_b, m_w_spatial, m_b_spatial, m_out_norm_a,
                                   m_out_norm_b, m_w_out, m_post_mix_norm, m_pre_ffn_norm, m_w_up, m_conv_w, m_conv_b,
                                   m_w_down, m_post_ffn_norm)))
    mom2 = dict(zip(WEIGHT_NAMES, (v_pre_mix_norm, v_w_in, v_v_norm_g, v_v_norm_b, v_w_spatial, v_b_spatial, v_out_norm_a,
                                   v_out_norm_b, v_w_out, v_post_mix_norm, v_pre_ffn_norm, v_w_up, v_conv_w, v_conv_b,
                                   v_w_down, v_post_ffn_norm)))
    mx, my, mc = _place()
    me = 4 * mx + 2 * my + mc

    fetch, started = _start_gathers(wts, me)
    scatters = []

    def emit(l, blocks):
        names = tuple(blocks)
        handle, token = _flat_start([blocks[n] for n in names], True, name=f"scatter_start_{l}_{'_'.join(names)}")
        scatters.append((l, names, handle))
        return token

    smalls = {}

    def emit_small(l, g, loss_local):
        pack = _pack_small({n: g[n] for n in SMALL_NAMES}, rider=loss_local if l == 0 else None)
        handle, token = _flat_start([pack], False, name=f"small_grads_start_{l}")
        smalls[l] = (pack, handle)
        return token

    _, dx, _ = _local_step(x[0], loss_target[0], fetch, wts, emit, emit_small, started)

    me_arr = jnp.reshape(me, (1,)).astype(jnp.int32)
    big_out = [{}, {}, {}, {}]

    def finish(group, after):
        mine, landed = {}, {}
        for l, names, handle in scatters:
            if names == group:
                sent, lands = _flat_wait(handle, after, name=f"scatter_wait_{l}_{'_'.join(names)}")
                for n, a, b in zip(names, sent, lands):
                    mine[l, n], landed[l, n] = a, b
        for n in group:
            flip = (lambda a: a.transpose(0, 2, 1)) if n == "w_in" else (lambda a: a)
            res = _adamw_sharded(me_arr, [mine[l, n] for l in range(N_LAYERS)], [landed[l, n] for l in range(N_LAYERS)],
                                 flip(wts[n]), flip(mom1[n]), flip(mom2[n]), ADAMW_TILE_ROWS[n], name=f"adamw_{n}")
            for kind in range(4):
                big_out[kind][n] = flip(res[kind])
        return res[0]

    done = smalls[0][1].thru[0]
    for group in [names for l, names, _ in scatters if l == 0][:-1]:
        done = finish(group, done)
    everyone = [None] * N_LAYERS
    for l in reversed(range(N_LAYERS)):
        pack, handle = smalls[l]
        (landed,) = _flat_wait(handle, done, name=f"small_grads_wait_{l}")[1]
        everyone[l] = lax.dynamic_update_slice(landed, pack[None], (me, 0, 0))
    packs = [_pack_small_layers({n: t[n] for n in SMALL_NAMES}) for t in (wts, mom1, mom2)]
    res = _adamw_replicated(everyone, *packs, name="adamw_replicated")
    loss = res[0][0].reshape(-1)[SMALL_USED]
    finish(scatters[-1][1], res[0])
    small_shapes = {n: wts[n].shape for n in SMALL_NAMES}
    small_out = [_unpack_small_layers(o, small_shapes) for o in res]

    outs = [loss, dx[None]]
    for kind in range(4):
        outs += [big_out[kind][n] if n in BIG_NAMES else small_out[kind][n] for n in WEIGHT_NAMES]
    return tuple(outs)
```
